```python
import math
import jax, jax.numpy as jnp
from jax import lax
import numpy as np

D_MODEL = 2048
BATCH = 8
SEQ = 8192
DEPTH = 1

PLE_DIM = 256
MIX_WIDTH = D_MODEL
ATTN_WIDTH = MIX_WIDTH // 2
HEAD_DIM = 64
N_HEADS = ATTN_WIDTH // HEAD_DIM
SSM_WIDTH = MIX_WIDTH - ATTN_WIDTH
SSM_GROUP = 16
N_SSM_GROUPS = SSM_WIDTH // SSM_GROUP
SSM_STATE = 64
D_FF = ((8 * D_MODEL // 3 + 127) // 128) * 128
DILATIONS = ((128, 1), (512, 4), (2048, 16))
SWA_BLOCK = 128
NORM_EPS = 1e-6
DT_MIN = 1e-3
DT_MAX = 1e-1
MASK_VALUE = -1e30

kernel_name = 'hymba_dilated_s5_macaron'


def rms_norm(x, g):
    xf = x.astype(jnp.float32)
    y = xf * lax.rsqrt(jnp.mean(xf * xf, axis=-1, keepdims=True) + NORM_EPS)
    return (y * g.astype(jnp.float32)).astype(x.dtype)


def swiglu(x, w_gate, w_up, w_down):
    return (jax.nn.silu(x @ w_gate) * (x @ w_up)) @ w_down


def banded_attention_stats(q, k, v, span):
    N, H, L, E = q.shape
    Q = SWA_BLOCK
    nb = -(-L // Q)
    Lp = nb * Q
    qf, kf, vf = (t.astype(jnp.float32) for t in (q, k, v))
    qb = jnp.pad(qf, ((0, 0), (0, 0), (0, Lp - L), (0, 0))).reshape(N, H, nb, Q, E)
    kp = jnp.pad(kf, ((0, 0), (0, 0), (Q, Lp - L), (0, 0)))
    vp = jnp.pad(vf, ((0, 0), (0, 0), (Q, Lp - L), (0, 0)))
    kb = jnp.concatenate([kp[:, :, :Lp].reshape(N, H, nb, Q, E), kp[:, :, Q:].reshape(N, H, nb, Q, E)], axis=3)
    vb = jnp.concatenate([vp[:, :, :Lp].reshape(N, H, nb, Q, E), vp[:, :, Q:].reshape(N, H, nb, Q, E)], axis=3)
    s = jnp.einsum('nhbqe,nhbke->nhbqk', qb, kb) * (E ** -0.5)
    qi = jnp.arange(Q)[:, None]
    ki = jnp.arange(2 * Q)[None, :]
    dist = qi + Q - ki
    blk = jnp.arange(nb)[:, None, None]
    valid = (dist >= 0) & (dist <= span) & (blk * Q + ki - Q >= 0)
    s = jnp.where(valid, s, MASK_VALUE)
    m = jnp.max(s, axis=-1)
    pexp = jnp.exp(s - m[..., None])
    l = jnp.sum(pexp, axis=-1)
    o = jnp.einsum('nhbqk,nhbke->nhbqe', pexp, vb)
    o = o.reshape(N, H, Lp, E)[:, :, :L]
    m = m.reshape(N, H, Lp)[:, :, :L]
    l = l.reshape(N, H, Lp)[:, :, :L]
    return o, m, l


def dilated_attention(q, k, v):
    B, S, H, E = q.shape
    outs, maxes, dens = [], [], []
    for window, d in DILATIONS:
        L = S // d
        span = window // d

        def to_residue(t):
            return t.reshape(B, L, d, H, E).transpose(0, 2, 3, 1, 4).reshape(B * d, H, L, E)

        o, m, l = banded_attention_stats(to_residue(q), to_residue(k), to_residue(v), span)
        outs.append(o.reshape(B, d, H, L, E).transpose(0, 3, 1, 2, 4).reshape(B, S, H, E))
        maxes.append(m.reshape(B, d, H, L).transpose(0, 3, 1, 2).reshape(B, S, H))
        dens.append(l.reshape(B, d, H, L).transpose(0, 3, 1, 2).reshape(B, S, H))
    m_all = jnp.stack(maxes, axis=0)
    m_glob = jnp.max(m_all, axis=0)
    w = jnp.exp(m_all - m_glob[None])
    num = sum(w[i][..., None] * outs[i] for i in range(len(DILATIONS)))
    den = sum(w[i] * dens[i] for i in range(len(DILATIONS)))
    return (num / den[..., None]).astype(q.dtype)


def _ssm_combine(left, right):
    ar_l, ai_l, br_l, bi_l = left
    ar_r, ai_r, br_r, bi_r = right
    return (ar_r * ar_l - ai_r * ai_l,
            ar_r * ai_l + ai_r * ar_l,
            ar_r * br_l - ai_r * bi_l + br_r,
            ar_r * bi_l + ai_r * br_l + bi_r)


def s5_mixer(u, lam_re, lam_im, log_dt, b_re, b_im, c_re, c_im, d_skip, w_glu, b_glu):
    B, S, _ = u.shape
    G, P, C = N_SSM_GROUPS, SSM_STATE, SSM_GROUP
    uf = u.astype(jnp.float32).reshape(B, S, G, C)
    lr = lam_re.astype(jnp.float32)
    li = lam_im.astype(jnp.float32)
    dt = jnp.exp(log_dt.astype(jnp.float32))[:, None]
    mag = jnp.exp(lr * dt)
    ar = mag * jnp.cos(li * dt)
    ai = mag * jnp.sin(li * dt)
    nr, ni = ar - 1.0, ai
    den = lr * lr + li * li
    cr = (nr * lr + ni * li) / den
    ci = (ni * lr - nr * li) / den
    br, bi = b_re.astype(jnp.float32), b_im.astype(jnp.float32)
    bbr = cr[..., None] * br - ci[..., None] * bi
    bbi = cr[..., None] * bi + ci[..., None] * br
    xr = jnp.einsum('gpc,bsgc->bsgp', bbr, uf)
    xi = jnp.einsum('gpc,bsgc->bsgp', bbi, uf)
    a_r = jnp.broadcast_to(ar[None, None], (1, S, G, P))
    a_i = jnp.broadcast_to(ai[None, None], (1, S, G, P))
    _, _, hr, hi = lax.associative_scan(_ssm_combine, (a_r, a_i, xr, xi), axis=1)
    y = (jnp.einsum('gcp,bsgp->bsgc', c_re.astype(jnp.float32), hr)
         - jnp.einsum('gcp,bsgp->bsgc', c_im.astype(jnp.float32), hi)
         + d_skip.astype(jnp.float32).reshape(G, C) * uf)
    y = jax.nn.gelu(y.reshape(B, S, SSM_WIDTH)).astype(u.dtype)
    return y * jax.nn.sigmoid(y @ w_glu + b_glu)


def _fwd_setup_inputs(seed: int = 0) -> dict:
    key = jax.random.key(seed)
    ks = iter(jax.random.split(key, 40))

    def nrm(shape, scale):
        return jax.random.normal(next(ks), shape, jnp.float32) * scale

    def gain(shape):
        return 1.0 + nrm(shape, 0.02)

    L_ = DEPTH
    G, P, C = N_SSM_GROUPS, SSM_STATE, SSM_GROUP
    return {
        'x': nrm((BATCH, SEQ, D_MODEL), 1.0),
        'p': nrm((DEPTH, BATCH, SEQ, PLE_DIM), 1.0),
        'ffn1_norm': gain((L_, D_MODEL)),
        'ffn1_w_gate': nrm((L_, D_MODEL, D_FF), D_MODEL ** -0.5),
        'ffn1_w_up': nrm((L_, D_MODEL, D_FF), D_MODEL ** -0.5),
        'ffn1_w_down': nrm((L_, D_FF, D_MODEL), D_FF ** -0.5),
        'mix_norm': gain((L_, D_MODEL)),
        'w_in': nrm((L_, D_MODEL, 3 * ATTN_WIDTH + SSM_WIDTH), D_MODEL ** -0.5),
        'attn_out_norm': gain((L_, ATTN_WIDTH)),
        'ssm_lambda_re': -0.5 + nrm((L_, G, P), 0.01),
        'ssm_lambda_im': math.pi * jnp.arange(P, dtype=jnp.float32)[None, None, :] + nrm((L_, G, P), 0.01),
        'ssm_log_dt': jax.random.uniform(next(ks), (L_, G), jnp.float32, math.log(DT_MIN), math.log(DT_MAX)),
        'ssm_b_re': nrm((L_, G, P, C), (2.0 * C) ** -0.5),
        'ssm_b_im': nrm((L_, G, P, C), (2.0 * C) ** -0.5),
        'ssm_c_re': nrm((L_, G, C, P), (2.0 * P) ** -0.5),
        'ssm_c_im': nrm((L_, G, C, P), (2.0 * P) ** -0.5),
        'ssm_d': nrm((L_, SSM_WIDTH), 1.0),
        'ssm_w_glu': nrm((L_, SSM_WIDTH, SSM_WIDTH), SSM_WIDTH ** -0.5),
        'ssm_b_glu': nrm((L_, SSM_WIDTH), 0.01),
        'ssm_out_norm': gain((L_, SSM_WIDTH)),
        'w_out': nrm((L_, MIX_WIDTH, D_MODEL), MIX_WIDTH ** -0.5),
        'ffn2_norm': gain((L_, D_MODEL)),
        'ffn2_w_gate': nrm((L_, D_MODEL, D_FF), D_MODEL ** -0.5),
        'ffn2_w_up': nrm((L_, D_MODEL, D_FF), D_MODEL ** -0.5),
        'ffn2_w_down': nrm((L_, D_FF, D_MODEL), D_FF ** -0.5),
        'ple_norm': gain((L_, D_MODEL)),
        'ple_w_gate': nrm((L_, D_MODEL, D_MODEL), D_MODEL ** -0.5),
        'ple_w_proj': nrm((L_, PLE_DIM, D_MODEL), PLE_DIM ** -0.5),
        'final_norm': gain((D_MODEL,)),
    }


def _fwd_reference(x, p, ffn1_norm, ffn1_w_gate, ffn1_w_up, ffn1_w_down, mix_norm, w_in,
              attn_out_norm, ssm_lambda_re, ssm_lambda_im, ssm_log_dt, ssm_b_re, ssm_b_im,
              ssm_c_re, ssm_c_im, ssm_d, ssm_w_glu, ssm_b_glu, ssm_out_norm, w_out,
              ffn2_norm, ffn2_w_gate, ffn2_w_up, ffn2_w_down, ple_norm, ple_w_gate,
              ple_w_proj, final_norm):
    B, S, _ = x.shape
    h = x
    for i in range(DEPTH):
        h = h + 0.5 * swiglu(rms_norm(h, ffn1_norm[i]), ffn1_w_gate[i], ffn1_w_up[i], ffn1_w_down[i])
        u = rms_norm(h, mix_norm[i])
        z = u @ w_in[i]
        q = z[..., :ATTN_WIDTH].reshape(B, S, N_HEADS, HEAD_DIM)
        k = z[..., ATTN_WIDTH:2 * ATTN_WIDTH].reshape(B, S, N_HEADS, HEAD_DIM)
        v = z[..., 2 * ATTN_WIDTH:3 * ATTN_WIDTH].reshape(B, S, N_HEADS, HEAD_DIM)
        s_in = z[..., 3 * ATTN_WIDTH:]
        ya = dilated_attention(q, k, v).reshape(B, S, ATTN_WIDTH)
        yb = s5_mixer(s_in, ssm_lambda_re[i], ssm_lambda_im[i], ssm_log_dt[i], ssm_b_re[i], ssm_b_im[i],
                      ssm_c_re[i], ssm_c_im[i], ssm_d[i], ssm_w_glu[i], ssm_b_glu[i])
        y = jnp.concatenate([rms_norm(ya, attn_out_norm[i]), rms_norm(yb, ssm_out_norm[i])], axis=-1)
        h = h + y @ w_out[i]
        h = h + 0.5 * swiglu(rms_norm(h, ffn2_norm[i]), ffn2_w_gate[i], ffn2_w_up[i], ffn2_w_down[i])
        gate = jax.nn.sigmoid(rms_norm(h, ple_norm[i]) @ ple_w_gate[i])
        h = h + gate * (p[i] @ ple_w_proj[i])
    return rms_norm(h, final_norm)


import jax as _jax
import jax.numpy as _jnp

TWIN_FORMAT = 'train_step'
FWD_PARAMS = ['x', 'p', 'ffn1_norm', 'ffn1_w_gate', 'ffn1_w_up', 'ffn1_w_down', 'mix_norm', 'w_in', 'attn_out_norm', 'ssm_lambda_re', 'ssm_lambda_im', 'ssm_log_dt', 'ssm_b_re', 'ssm_b_im', 'ssm_c_re', 'ssm_c_im', 'ssm_d', 'ssm_w_glu', 'ssm_b_glu', 'ssm_out_norm', 'w_out', 'ffn2_norm', 'ffn2_w_gate', 'ffn2_w_up', 'ffn2_w_down', 'ple_norm', 'ple_w_gate', 'ple_w_proj', 'final_norm']
TWIN_WEIGHTS = ['ffn1_norm', 'ffn1_w_gate', 'ffn1_w_up', 'ffn1_w_down', 'mix_norm', 'w_in', 'attn_out_norm', 'ssm_lambda_re', 'ssm_lambda_im', 'ssm_log_dt', 'ssm_b_re', 'ssm_b_im', 'ssm_c_re', 'ssm_c_im', 'ssm_d', 'ssm_w_glu', 'ssm_b_glu', 'ssm_out_norm', 'w_out', 'ffn2_norm', 'ffn2_w_gate', 'ffn2_w_up', 'ffn2_w_down', 'ple_norm', 'ple_w_gate', 'ple_w_proj', 'final_norm']
TWIN_DIFF_INPUT = 'x'
TWIN_INPUTS = ['x', 'p', 'ffn1_norm', 'ffn1_w_gate', 'ffn1_w_up', 'ffn1_w_down', 'mix_norm', 'w_in', 'attn_out_norm', 'ssm_lambda_re', 'ssm_lambda_im', 'ssm_log_dt', 'ssm_b_re', 'ssm_b_im', 'ssm_c_re', 'ssm_c_im', 'ssm_d', 'ssm_w_glu', 'ssm_b_glu', 'ssm_out_norm', 'w_out', 'ffn2_norm', 'ffn2_w_gate', 'ffn2_w_up', 'ffn2_w_down', 'ple_norm', 'ple_w_gate', 'ple_w_proj', 'final_norm', 'loss_target', 'm_ffn1_norm', 'm_ffn1_w_gate', 'm_ffn1_w_up', 'm_ffn1_w_down', 'm_mix_norm', 'm_w_in', 'm_attn_out_norm', 'm_ssm_lambda_re', 'm_ssm_lambda_im', 'm_ssm_log_dt', 'm_ssm_b_re', 'm_ssm_b_im', 'm_ssm_c_re', 'm_ssm_c_im', 'm_ssm_d', 'm_ssm_w_glu', 'm_ssm_b_glu', 'm_ssm_out_norm', 'm_w_out', 'm_ffn2_norm', 'm_ffn2_w_gate', 'm_ffn2_w_up', 'm_ffn2_w_down', 'm_ple_norm', 'm_ple_w_gate', 'm_ple_w_proj', 'm_final_norm', 'v_ffn1_norm', 'v_ffn1_w_gate', 'v_ffn1_w_up', 'v_ffn1_w_down', 'v_mix_norm', 'v_w_in', 'v_attn_out_norm', 'v_ssm_lambda_re', 'v_ssm_lambda_im', 'v_ssm_log_dt', 'v_ssm_b_re', 'v_ssm_b_im', 'v_ssm_c_re', 'v_ssm_c_im', 'v_ssm_d', 'v_ssm_w_glu', 'v_ssm_b_glu', 'v_ssm_out_norm', 'v_w_out', 'v_ffn2_norm', 'v_ffn2_w_gate', 'v_ffn2_w_up', 'v_ffn2_w_down', 'v_ple_norm', 'v_ple_w_gate', 'v_ple_w_proj', 'v_final_norm']
TWIN_OUTPUTS = ['loss', 'grad_x', 'grad_ffn1_norm', 'grad_ffn1_w_gate', 'grad_ffn1_w_up', 'grad_ffn1_w_down', 'grad_mix_norm', 'grad_w_in', 'grad_attn_out_norm', 'grad_ssm_lambda_re', 'grad_ssm_lambda_im', 'grad_ssm_log_dt', 'grad_ssm_b_re', 'grad_ssm_b_im', 'grad_ssm_c_re', 'grad_ssm_c_im', 'grad_ssm_d', 'grad_ssm_w_glu', 'grad_ssm_b_glu', 'grad_ssm_out_norm', 'grad_w_out', 'grad_ffn2_norm', 'grad_ffn2_w_gate', 'grad_ffn2_w_up', 'grad_ffn2_w_down', 'grad_ple_norm', 'grad_ple_w_gate', 'grad_ple_w_proj', 'grad_final_norm', 'delta_ffn1_norm', 'delta_ffn1_w_gate', 'delta_ffn1_w_up', 'delta_ffn1_w_down', 'delta_mix_norm', 'delta_w_in', 'delta_attn_out_norm', 'delta_ssm_lambda_re', 'delta_ssm_lambda_im', 'delta_ssm_log_dt', 'delta_ssm_b_re', 'delta_ssm_b_im', 'delta_ssm_c_re', 'delta_ssm_c_im', 'delta_ssm_d', 'delta_ssm_w_glu', 'delta_ssm_b_glu', 'delta_ssm_out_norm', 'delta_w_out', 'delta_ffn2_norm', 'delta_ffn2_w_gate', 'delta_ffn2_w_up', 'delta_ffn2_w_down', 'delta_ple_norm', 'delta_ple_w_gate', 'delta_ple_w_proj', 'delta_final_norm', 'new_m_ffn1_norm', 'new_m_ffn1_w_gate', 'new_m_ffn1_w_up', 'new_m_ffn1_w_down', 'new_m_mix_norm', 'new_m_w_in', 'new_m_attn_out_norm', 'new_m_ssm_lambda_re', 'new_m_ssm_lambda_im', 'new_m_ssm_log_dt', 'new_m_ssm_b_re', 'new_m_ssm_b_im', 'new_m_ssm_c_re', 'new_m_ssm_c_im', 'new_m_ssm_d', 'new_m_ssm_w_glu', 'new_m_ssm_b_glu', 'new_m_ssm_out_norm', 'new_m_w_out', 'new_m_ffn2_norm', 'new_m_ffn2_w_gate', 'new_m_ffn2_w_up', 'new_m_ffn2_w_down', 'new_m_ple_norm', 'new_m_ple_w_gate', 'new_m_ple_w_proj', 'new_m_final_norm', 'new_v_ffn1_norm', 'new_v_ffn1_w_gate', 'new_v_ffn1_w_up', 'new_v_ffn1_w_down', 'new_v_mix_norm', 'new_v_w_in', 'new_v_attn_out_norm', 'new_v_ssm_lambda_re', 'new_v_ssm_lambda_im', 'new_v_ssm_log_dt', 'new_v_ssm_b_re', 'new_v_ssm_b_im', 'new_v_ssm_c_re', 'new_v_ssm_c_im', 'new_v_ssm_d', 'new_v_ssm_w_glu', 'new_v_ssm_b_glu', 'new_v_ssm_out_norm', 'new_v_w_out', 'new_v_ffn2_norm', 'new_v_ffn2_w_gate', 'new_v_ffn2_w_up', 'new_v_ffn2_w_down', 'new_v_ple_norm', 'new_v_ple_w_gate', 'new_v_ple_w_proj', 'new_v_final_norm']
TWIN_LEAF_KINDS = {'loss': 'loss', 'grad_x': 'grad_x', 'grad_ffn1_norm': 'grad_w', 'grad_ffn1_w_gate': 'grad_w', 'grad_ffn1_w_up': 'grad_w', 'grad_ffn1_w_down': 'grad_w', 'grad_mix_norm': 'grad_w', 'grad_w_in': 'grad_w', 'grad_attn_out_norm': 'grad_w', 'grad_ssm_lambda_re': 'grad_w', 'grad_ssm_lambda_im': 'grad_w', 'grad_ssm_log_dt': 'grad_w', 'grad_ssm_b_re': 'grad_w', 'grad_ssm_b_im': 'grad_w', 'grad_ssm_c_re': 'grad_w', 'grad_ssm_c_im': 'grad_w', 'grad_ssm_d': 'grad_w', 'grad_ssm_w_glu': 'grad_w', 'grad_ssm_b_glu': 'grad_w', 'grad_ssm_out_norm': 'grad_w', 'grad_w_out': 'grad_w', 'grad_ffn2_norm': 'grad_w', 'grad_ffn2_w_gate': 'grad_w', 'grad_ffn2_w_up': 'grad_w', 'grad_ffn2_w_down': 'grad_w', 'grad_ple_norm': 'grad_w', 'grad_ple_w_gate': 'grad_w', 'grad_ple_w_proj': 'grad_w', 'grad_final_norm': 'grad_w', 'delta_ffn1_norm': 'delta_w', 'delta_ffn1_w_gate': 'delta_w', 'delta_ffn1_w_up': 'delta_w', 'delta_ffn1_w_down': 'delta_w', 'delta_mix_norm': 'delta_w', 'delta_w_in': 'delta_w', 'delta_attn_out_norm': 'delta_w', 'delta_ssm_lambda_re': 'delta_w', 'delta_ssm_lambda_im': 'delta_w', 'delta_ssm_log_dt': 'delta_w', 'delta_ssm_b_re': 'delta_w', 'delta_ssm_b_im': 'delta_w', 'delta_ssm_c_re': 'delta_w', 'delta_ssm_c_im': 'delta_w', 'delta_ssm_d': 'delta_w', 'delta_ssm_w_glu': 'delta_w', 'delta_ssm_b_glu': 'delta_w', 'delta_ssm_out_norm': 'delta_w', 'delta_w_out': 'delta_w', 'delta_ffn2_norm': 'delta_w', 'delta_ffn2_w_gate': 'delta_w', 'delta_ffn2_w_up': 'delta_w', 'delta_ffn2_w_down': 'delta_w', 'delta_ple_norm': 'delta_w', 'delta_ple_w_gate': 'delta_w', 'delta_ple_w_proj': 'delta_w', 'delta_final_norm': 'delta_w', 'new_m_ffn1_norm': 'new_m', 'new_m_ffn1_w_gate': 'new_m', 'new_m_ffn1_w_up': 'new_m', 'new_m_ffn1_w_down': 'new_m', 'new_m_mix_norm': 'new_m', 'new_m_w_in': 'new_m', 'new_m_attn_out_norm': 'new_m', 'new_m_ssm_lambda_re': 'new_m', 'new_m_ssm_lambda_im': 'new_m', 'new_m_ssm_log_dt': 'new_m', 'new_m_ssm_b_re': 'new_m', 'new_m_ssm_b_im': 'new_m', 'new_m_ssm_c_re': 'new_m', 'new_m_ssm_c_im': 'new_m', 'new_m_ssm_d': 'new_m', 'new_m_ssm_w_glu': 'new_m', 'new_m_ssm_b_glu': 'new_m', 'new_m_ssm_out_norm': 'new_m', 'new_m_w_out': 'new_m', 'new_m_ffn2_norm': 'new_m', 'new_m_ffn2_w_gate': 'new_m', 'new_m_ffn2_w_up': 'new_m', 'new_m_ffn2_w_down': 'new_m', 'new_m_ple_norm': 'new_m', 'new_m_ple_w_gate': 'new_m', 'new_m_ple_w_proj': 'new_m', 'new_m_final_norm': 'new_m', 'new_v_ffn1_norm': 'new_v', 'new_v_ffn1_w_gate': 'new_v', 'new_v_ffn1_w_up': 'new_v', 'new_v_ffn1_w_down': 'new_v', 'new_v_mix_norm': 'new_v', 'new_v_w_in': 'new_v', 'new_v_attn_out_norm': 'new_v', 'new_v_ssm_lambda_re': 'new_v', 'new_v_ssm_lambda_im': 'new_v', 'new_v_ssm_log_dt': 'new_v', 'new_v_ssm_b_re': 'new_v', 'new_v_ssm_b_im': 'new_v', 'new_v_ssm_c_re': 'new_v', 'new_v_ssm_c_im': 'new_v', 'new_v_ssm_d': 'new_v', 'new_v_ssm_w_glu': 'new_v', 'new_v_ssm_b_glu': 'new_v', 'new_v_ssm_out_norm': 'new_v', 'new_v_w_out': 'new_v', 'new_v_ffn2_norm': 'new_v', 'new_v_ffn2_w_gate': 'new_v', 'new_v_ffn2_w_up': 'new_v', 'new_v_ffn2_w_down': 'new_v', 'new_v_ple_norm': 'new_v', 'new_v_ple_w_gate': 'new_v', 'new_v_ple_w_proj': 'new_v', 'new_v_final_norm': 'new_v'}


def _forward(args):
    return _fwd_reference(*[args[k] for k in FWD_PARAMS])


def _output_shape():
    def fwd():
        inp = _fwd_setup_inputs(0)
        return _fwd_reference(*[inp[k] for k in FWD_PARAMS])
    out = _jax.eval_shape(fwd)
    return out.shape, out.dtype

N_MICROBATCH = 1
ADAM_LR = 0.001
ADAM_B1 = 0.9
ADAM_B2 = 0.999
ADAM_EPS = 1e-08
ADAM_WD = 0.01
ADAM_STEP = 10
PER_EXAMPLE_BATCH_AXIS = {'x': 0, 'p': 1, 'loss_target': 0}
SHARED_INPUTS = []
_WEIGHT_DTYPES = {'ffn1_norm': _jnp.float32, 'ffn1_w_gate': _jnp.float32, 'ffn1_w_up': _jnp.float32, 'ffn1_w_down': _jnp.float32, 'mix_norm': _jnp.float32, 'w_in': _jnp.float32, 'attn_out_norm': _jnp.float32, 'ssm_lambda_re': _jnp.float32, 'ssm_lambda_im': _jnp.float32, 'ssm_log_dt': _jnp.float32, 'ssm_b_re': _jnp.float32, 'ssm_b_im': _jnp.float32, 'ssm_c_re': _jnp.float32, 'ssm_c_im': _jnp.float32, 'ssm_d': _jnp.float32, 'ssm_w_glu': _jnp.float32, 'ssm_b_glu': _jnp.float32, 'ssm_out_norm': _jnp.float32, 'w_out': _jnp.float32, 'ffn2_norm': _jnp.float32, 'ffn2_w_gate': _jnp.float32, 'ffn2_w_up': _jnp.float32, 'ffn2_w_down': _jnp.float32, 'ple_norm': _jnp.float32, 'ple_w_gate': _jnp.float32, 'ple_w_proj': _jnp.float32, 'final_norm': _jnp.float32}
MOMENT_SCALE = {'ffn1_norm': 6.746570e-02, 'ffn1_w_gate': 2.701007e-02, 'ffn1_w_up': 2.619539e-02, 'ffn1_w_down': 4.290510e-02, 'mix_norm': 1.232865e-01, 'w_in': 8.663973e-02, 'attn_out_norm': 8.780294e-02, 'ssm_lambda_re': 4.864108e-03, 'ssm_lambda_im': 5.117979e-03, 'ssm_log_dt': 6.773310e+00, 'ssm_b_re': 3.017217e-03, 'ssm_b_im': 2.936964e-03, 'ssm_c_re': 6.054444e-03, 'ssm_c_im': 6.034844e-03, 'ssm_d': 1.100185e-01, 'ssm_w_glu': 2.522003e-02, 'ssm_b_glu': 4.258190e-02, 'ssm_out_norm': 9.241335e-02, 'w_out': 8.995866e-02, 'ffn2_norm': 3.497258e-02, 'ffn2_w_gate': 1.517128e-02, 'ffn2_w_up': 1.470841e-02, 'ffn2_w_down': 2.416167e-02, 'ple_norm': 1.810862e-02, 'ple_w_gate': 1.803335e-02, 'ple_w_proj': 4.353756e-02, 'final_norm': 3.204761e+01}


def _to_microbatches(a, axis):
    t = _jnp.moveaxis(a, axis, 0)
    t = t.reshape((N_MICROBATCH, t.shape[0] // N_MICROBATCH) + t.shape[1:])
    return _jnp.moveaxis(t, 1, axis + 1)


def setup_inputs(seed: int = 0) -> dict:
    inp = _fwd_setup_inputs(seed)
    key = _jax.random.fold_in(_jax.random.key(seed), 7919)
    shape, _ = _output_shape()
    out = dict(inp)
    out["loss_target"] = _jax.random.normal(_jax.random.fold_in(key, 0), shape, _jnp.float32)
    for i, name in enumerate(TWIN_WEIGHTS):
        w = inp[name].astype(_jnp.float32)
        if MOMENT_SCALE is None:
            s = _jnp.sqrt(_jnp.mean(_jnp.square(w)) + 1e-30)
        else:
            s = MOMENT_SCALE[name]
        km, kv = _jax.random.split(_jax.random.fold_in(key, i + 1))
        out[name] = w
        out["m_" + name] = s * _jax.random.normal(km, w.shape, _jnp.float32)
        out["v_" + name] = (s * s) * _jax.random.uniform(kv, w.shape, _jnp.float32, 0.5, 1.5)
    if N_MICROBATCH > 1:
        for name, axis in PER_EXAMPLE_BATCH_AXIS.items():
            out[name] = _to_microbatches(out[name], axis)
    return {'x': out['x'], 'p': out['p'], 'ffn1_norm': out['ffn1_norm'], 'ffn1_w_gate': out['ffn1_w_gate'], 'ffn1_w_up': out['ffn1_w_up'], 'ffn1_w_down': out['ffn1_w_down'], 'mix_norm': out['mix_norm'], 'w_in': out['w_in'], 'attn_out_norm': out['attn_out_norm'], 'ssm_lambda_re': out['ssm_lambda_re'], 'ssm_lambda_im': out['ssm_lambda_im'], 'ssm_log_dt': out['ssm_log_dt'], 'ssm_b_re': out['ssm_b_re'], 'ssm_b_im': out['ssm_b_im'], 'ssm_c_re': out['ssm_c_re'], 'ssm_c_im': out['ssm_c_im'], 'ssm_d': out['ssm_d'], 'ssm_w_glu': out['ssm_w_glu'], 'ssm_b_glu': out['ssm_b_glu'], 'ssm_out_norm': out['ssm_out_norm'], 'w_out': out['w_out'], 'ffn2_norm': out['ffn2_norm'], 'ffn2_w_gate': out['ffn2_w_gate'], 'ffn2_w_up': out['ffn2_w_up'], 'ffn2_w_down': out['ffn2_w_down'], 'ple_norm': out['ple_norm'], 'ple_w_gate': out['ple_w_gate'], 'ple_w_proj': out['ple_w_proj'], 'final_norm': out['final_norm'], 'loss_target': out['loss_target'], 'm_ffn1_norm': out['m_ffn1_norm'], 'm_ffn1_w_gate': out['m_ffn1_w_gate'], 'm_ffn1_w_up': out['m_ffn1_w_up'], 'm_ffn1_w_down': out['m_ffn1_w_down'], 'm_mix_norm': out['m_mix_norm'], 'm_w_in': out['m_w_in'], 'm_attn_out_norm': out['m_attn_out_norm'], 'm_ssm_lambda_re': out['m_ssm_lambda_re'], 'm_ssm_lambda_im': out['m_ssm_lambda_im'], 'm_ssm_log_dt': out['m_ssm_log_dt'], 'm_ssm_b_re': out['m_ssm_b_re'], 'm_ssm_b_im': out['m_ssm_b_im'], 'm_ssm_c_re': out['m_ssm_c_re'], 'm_ssm_c_im': out['m_ssm_c_im'], 'm_ssm_d': out['m_ssm_d'], 'm_ssm_w_glu': out['m_ssm_w_glu'], 'm_ssm_b_glu': out['m_ssm_b_glu'], 'm_ssm_out_norm': out['m_ssm_out_norm'], 'm_w_out': out['m_w_out'], 'm_ffn2_norm': out['m_ffn2_norm'], 'm_ffn2_w_gate': out['m_ffn2_w_gate'], 'm_ffn2_w_up': out['m_ffn2_w_up'], 'm_ffn2_w_down': out['m_ffn2_w_down'], 'm_ple_norm': out['m_ple_norm'], 'm_ple_w_gate': out['m_ple_w_gate'], 'm_ple_w_proj': out['m_ple_w_proj'], 'm_final_norm': out['m_final_norm'], 'v_ffn1_norm': out['v_ffn1_norm'], 'v_ffn1_w_gate': out['v_ffn1_w_gate'], 'v_ffn1_w_up': out['v_ffn1_w_up'], 'v_ffn1_w_down': out['v_ffn1_w_down'], 'v_mix_norm': out['v_mix_norm'], 'v_w_in': out['v_w_in'], 'v_attn_out_norm': out['v_attn_out_norm'], 'v_ssm_lambda_re': out['v_ssm_lambda_re'], 'v_ssm_lambda_im': out['v_ssm_lambda_im'], 'v_ssm_log_dt': out['v_ssm_log_dt'], 'v_ssm_b_re': out['v_ssm_b_re'], 'v_ssm_b_im': out['v_ssm_b_im'], 'v_ssm_c_re': out['v_ssm_c_re'], 'v_ssm_c_im': out['v_ssm_c_im'], 'v_ssm_d': out['v_ssm_d'], 'v_ssm_w_glu': out['v_ssm_w_glu'], 'v_ssm_b_glu': out['v_ssm_b_glu'], 'v_ssm_out_norm': out['v_ssm_out_norm'], 'v_w_out': out['v_w_out'], 'v_ffn2_norm': out['v_ffn2_norm'], 'v_ffn2_w_gate': out['v_ffn2_w_gate'], 'v_ffn2_w_up': out['v_ffn2_w_up'], 'v_ffn2_w_down': out['v_ffn2_w_down'], 'v_ple_norm': out['v_ple_norm'], 'v_ple_w_gate': out['v_ple_w_gate'], 'v_ple_w_proj': out['v_ple_w_proj'], 'v_final_norm': out['v_final_norm']}


def _loss(weights, diff, rest, loss_target):
    with _jax.named_scope("forward"):
        args = {**rest, TWIN_DIFF_INPUT: diff, **{k: w.astype(_WEIGHT_DTYPES[k]) for k, w in weights.items()}}
        y = _forward(args)
    with _jax.named_scope("loss_head"):
        err = _jnp.square(y.astype(_jnp.float32) - loss_target)
        return 0.5 * _jnp.sum(_jnp.mean(err, axis=-1)) if err.ndim else 0.5 * err


def _adamw(w, g, m, v):
    m = ADAM_B1 * m + (1.0 - ADAM_B1) * g
    v = ADAM_B2 * v + (1.0 - ADAM_B2) * _jnp.square(g)
    m_hat = m / (1.0 - ADAM_B1 ** ADAM_STEP)
    v_hat = v / (1.0 - ADAM_B2 ** ADAM_STEP)
    delta = -ADAM_LR * (m_hat / (_jnp.sqrt(v_hat) + ADAM_EPS) + ADAM_WD * w)
    return delta, m, v


def reference(x, p, ffn1_norm, ffn1_w_gate, ffn1_w_up, ffn1_w_down, mix_norm, w_in, attn_out_norm, ssm_lambda_re, ssm_lambda_im, ssm_log_dt, ssm_b_re, ssm_b_im, ssm_c_re, ssm_c_im, ssm_d, ssm_w_glu, ssm_b_glu, ssm_out_norm, w_out, ffn2_norm, ffn2_w_gate, ffn2_w_up, ffn2_w_down, ple_norm, ple_w_gate, ple_w_proj, final_norm, loss_target, m_ffn1_norm, m_ffn1_w_gate, m_ffn1_w_up, m_ffn1_w_down, m_mix_norm, m_w_in, m_attn_out_norm, m_ssm_lambda_re, m_ssm_lambda_im, m_ssm_log_dt, m_ssm_b_re, m_ssm_b_im, m_ssm_c_re, m_ssm_c_im, m_ssm_d, m_ssm_w_glu, m_ssm_b_glu, m_ssm_out_norm, m_w_out, m_ffn2_norm, m_ffn2_w_gate, m_ffn2_w_up, m_ffn2_w_down, m_ple_norm, m_ple_w_gate, m_ple_w_proj, m_final_norm, v_ffn1_norm, v_ffn1_w_gate, v_ffn1_w_up, v_ffn1_w_down, v_mix_norm, v_w_in, v_attn_out_norm, v_ssm_lambda_re, v_ssm_lambda_im, v_ssm_log_dt, v_ssm_b_re, v_ssm_b_im, v_ssm_c_re, v_ssm_c_im, v_ssm_d, v_ssm_w_glu, v_ssm_b_glu, v_ssm_out_norm, v_w_out, v_ffn2_norm, v_ffn2_w_gate, v_ffn2_w_up, v_ffn2_w_down, v_ple_norm, v_ple_w_gate, v_ple_w_proj, v_final_norm):
    given = dict(x=x, p=p, ffn1_norm=ffn1_norm, ffn1_w_gate=ffn1_w_gate, ffn1_w_up=ffn1_w_up, ffn1_w_down=ffn1_w_down, mix_norm=mix_norm, w_in=w_in, attn_out_norm=attn_out_norm, ssm_lambda_re=ssm_lambda_re, ssm_lambda_im=ssm_lambda_im, ssm_log_dt=ssm_log_dt, ssm_b_re=ssm_b_re, ssm_b_im=ssm_b_im, ssm_c_re=ssm_c_re, ssm_c_im=ssm_c_im, ssm_d=ssm_d, ssm_w_glu=ssm_w_glu, ssm_b_glu=ssm_b_glu, ssm_out_norm=ssm_out_norm, w_out=w_out, ffn2_norm=ffn2_norm, ffn2_w_gate=ffn2_w_gate, ffn2_w_up=ffn2_w_up, ffn2_w_down=ffn2_w_down, ple_norm=ple_norm, ple_w_gate=ple_w_gate, ple_w_proj=ple_w_proj, final_norm=final_norm, loss_target=loss_target, m_ffn1_norm=m_ffn1_norm, m_ffn1_w_gate=m_ffn1_w_gate, m_ffn1_w_up=m_ffn1_w_up, m_ffn1_w_down=m_ffn1_w_down, m_mix_norm=m_mix_norm, m_w_in=m_w_in, m_attn_out_norm=m_attn_out_norm, m_ssm_lambda_re=m_ssm_lambda_re, m_ssm_lambda_im=m_ssm_lambda_im, m_ssm_log_dt=m_ssm_log_dt, m_ssm_b_re=m_ssm_b_re, m_ssm_b_im=m_ssm_b_im, m_ssm_c_re=m_ssm_c_re, m_ssm_c_im=m_ssm_c_im, m_ssm_d=m_ssm_d, m_ssm_w_glu=m_ssm_w_glu, m_ssm_b_glu=m_ssm_b_glu, m_ssm_out_norm=m_ssm_out_norm, m_w_out=m_w_out, m_ffn2_norm=m_ffn2_norm, m_ffn2_w_gate=m_ffn2_w_gate, m_ffn2_w_up=m_ffn2_w_up, m_ffn2_w_down=m_ffn2_w_down, m_ple_norm=m_ple_norm, m_ple_w_gate=m_ple_w_gate, m_ple_w_proj=m_ple_w_proj, m_final_norm=m_final_norm, v_ffn1_norm=v_ffn1_norm, v_ffn1_w_gate=v_ffn1_w_gate, v_ffn1_w_up=v_ffn1_w_up, v_ffn1_w_down=v_ffn1_w_down, v_mix_norm=v_mix_norm, v_w_in=v_w_in, v_attn_out_norm=v_attn_out_norm, v_ssm_lambda_re=v_ssm_lambda_re, v_ssm_lambda_im=v_ssm_lambda_im, v_ssm_log_dt=v_ssm_log_dt, v_ssm_b_re=v_ssm_b_re, v_ssm_b_im=v_ssm_b_im, v_ssm_c_re=v_ssm_c_re, v_ssm_c_im=v_ssm_c_im, v_ssm_d=v_ssm_d, v_ssm_w_glu=v_ssm_w_glu, v_ssm_b_glu=v_ssm_b_glu, v_ssm_out_norm=v_ssm_out_norm, v_w_out=v_w_out, v_ffn2_norm=v_ffn2_norm, v_ffn2_w_gate=v_ffn2_w_gate, v_ffn2_w_up=v_ffn2_w_up, v_ffn2_w_down=v_ffn2_w_down, v_ple_norm=v_ple_norm, v_ple_w_gate=v_ple_w_gate, v_ple_w_proj=v_ple_w_proj, v_final_norm=v_final_norm)
    weights = {n: given[n] for n in TWIN_WEIGHTS}
    shared = {n: given[n] for n in SHARED_INPUTS}
    per_example = {n: given[n] for n in ['x', 'p']}
    grad_fn = _jax.value_and_grad(_loss, argnums=(0, 1))

    def one_microbatch(ex, loss_target):
        ex = dict(ex)
        diff = ex.pop(TWIN_DIFF_INPUT)
        return grad_fn(weights, diff, {**shared, **ex}, loss_target)

    if N_MICROBATCH == 1:
        loss, (grad_w, grad_x) = one_microbatch(per_example, given["loss_target"])
    else:
        def body(carry, xs):
            loss_sum, grad_sum = carry
            l_k, (gw_k, gx_k) = one_microbatch(xs[0], xs[1])
            with _jax.named_scope("update"):
                return (loss_sum + l_k, _jax.tree.map(_jnp.add, grad_sum, gw_k)), gx_k

        init = (_jnp.zeros((), _jnp.float32), _jax.tree.map(_jnp.zeros_like, weights))
        (loss, grad_w), grad_x = _jax.lax.scan(body, init, (per_example, given["loss_target"]))
    with _jax.named_scope("update"):
        delta_w, new_m, new_v = {}, {}, {}
        for n in TWIN_WEIGHTS:
            delta_w[n], new_m[n], new_v[n] = _adamw(weights[n], grad_w[n], given["m_" + n], given["v_" + n])
    return (loss, grad_x, *[grad_w[n] for n in TWIN_WEIGHTS], *[delta_w[n] for n in TWIN_WEIGHTS],
            *[new_m[n] for n in TWIN_WEIGHTS], *[new_v[n] for n in TWIN_WEIGHTS])
```

```python
import functools
import math

import jax
import jax.numpy as jnp
from jax import lax
from jax.experimental import pallas as pl
from jax.experimental.pallas import tpu as pltpu

F32 = jnp.float32
BF16 = jnp.bfloat16
MESH = pl.DeviceIdType.MESH
MESH_AXES = ("x", "y", "c")
N_CHIPS = 4
N_DEV = 8

V7X_VMEM_LIMIT_BYTES = 56 << 20
LANES = 128
SUBLANES = 8

HEAD_DIM = 64
SWA_BLOCK = 128
DILATIONS = (1, 4, 16)
SSM_BLOCK_GROUPS = 8
NORM_EPS = 1e-6
MASK_VALUE = -1e30

ADAM_LR = 0.001
ADAM_B1 = 0.9
ADAM_B2 = 0.999
ADAM_EPS = 1e-08
ADAM_WD = 0.01
ADAM_STEP = 10

GELU_C = math.sqrt(2.0 / math.pi)
GELU_K = 0.044715


def _pcall(body, **kw):
    return pl.pallas_call(body, **kw)


def _params(*sem):
    return pltpu.CompilerParams(dimension_semantics=sem, vmem_limit_bytes=V7X_VMEM_LIMIT_BYTES)


def _tile(n, target, align):
    best = None
    for t in range(align, min(n, target) + 1, align):
        if n % t == 0:
            best = t
    return n if best is None else best


def _sigmoid(x):
    return 1.0 / (1.0 + jnp.exp(-x))


def _mm(name, lhs, rhs, outs, pairs=((0, 0, 0),), epilogue=None, extras=(), ta=False, tb=False,
        tm=1024, tn=512, tk=2048):
    nl, nr, ne, no = len(lhs), len(rhs), len(extras), len(outs)
    n_acc = 1 + max(p[2] for p in pairs)
    (K, M) = lhs[0].shape if ta else lhs[0].shape[::-1]
    (N, K2) = rhs[0].shape if tb else rhs[0].shape[::-1]
    assert K == K2, (name, lhs[0].shape, rhs[0].shape)
    tm, tn, tk = _tile(M, tm, LANES), _tile(N, tn, LANES), _tile(K, tk, LANES)
    nk = K // tk
    if epilogue is None:
        epilogue = lambda accs, ex: accs
    dn = (((0 if ta else 1,), (1 if tb else 0,)), ((), ()))

    def body(*refs):
        l, r = refs[:nl], refs[nl:nl + nr]
        e = refs[nl + nr:nl + nr + ne]
        o = refs[nl + nr + ne:nl + nr + ne + no]
        acc = refs[nl + nr + ne + no:]
        k = pl.program_id(2)
        parts = [None] * n_acc
        for li, ri, ai in pairs:
            d = lax.dot_general(l[li][...].astype(BF16), r[ri][...].astype(BF16), dn,
                                preferred_element_type=F32)
            parts[ai] = d if parts[ai] is None else parts[ai] + d

        def finish(accs):
            res = epilogue(accs, [x[...] for x in e])
            for ref, val in zip(o, res):
                ref[...] = val.astype(ref.dtype)

        if nk == 1:
            finish(parts)
        else:
            @pl.when(k == 0)
            def _():
                for ai in range(n_acc):
                    acc[ai][...] = parts[ai]

            @pl.when(k > 0)
            def _():
                for ai in range(n_acc):
                    acc[ai][...] += parts[ai]

            @pl.when(k == nk - 1)
            def _():
                finish([a[...] for a in acc])

    lspec = pl.BlockSpec((tk, tm), lambda i, j, k: (k, i)) if ta else pl.BlockSpec((tm, tk), lambda i, j, k: (i, k))
    rspec = pl.BlockSpec((tn, tk), lambda i, j, k: (j, k)) if tb else pl.BlockSpec((tk, tn), lambda i, j, k: (k, j))
    especs = []
    for arr, kind in extras:
        if kind == "mn":
            especs.append(pl.BlockSpec((tm, tn), lambda i, j, k: (i, j)))
        elif kind == "n":
            especs.append(pl.BlockSpec((1, tn), lambda i, j, k: (0, j)))
        else:
            especs.append(pl.BlockSpec((tm, 1), lambda i, j, k: (i, 0)))
    res = _pcall(
        body, name=name,
        grid=(M // tm, N // tn, nk),
        in_specs=[lspec] * nl + [rspec] * nr + especs,
        out_specs=[pl.BlockSpec((tm, tn), lambda i, j, k: (i, j))] * no,
        out_shape=[jax.ShapeDtypeStruct((M, N), dt) for dt in outs],
        scratch_shapes=[pltpu.VMEM((tm, tn), F32)] * (n_acc if nk > 1 else 0),
        compiler_params=_params("parallel", "parallel", "arbitrary"),
    )(*lhs, *rhs, *[a for a, _ in extras])
    return res


def _rowwise(name, fn, ins, params, outs, accs=(), ts=256):
    S = ins[0].shape[0]
    ts = _tile(S, ts, 16)
    ni, npar, no, na = len(ins), len(params), len(outs), len(accs)

    def body(*refs):
        i_refs, p_refs = refs[:ni], refs[ni:ni + npar]
        o_refs = refs[ni + npar:ni + npar + no]
        a_refs = refs[ni + npar + no:]
        res_o, res_a = fn([r[...] for r in i_refs], [r[...] for r in p_refs])
        for ref, val in zip(o_refs, res_o):
            ref[...] = val.astype(ref.dtype)
        if na:
            @pl.when(pl.program_id(0) == 0)
            def _():
                for ref in a_refs:
                    ref[...] = jnp.zeros(ref.shape, F32)

            for ref, val in zip(a_refs, res_a):
                ref[...] += val

    res = _pcall(
        body, name=name,
        grid=(S // ts,),
        in_specs=[pl.BlockSpec((ts, a.shape[1]), lambda i: (i, 0)) for a in ins]
        + [pl.BlockSpec(p.shape, lambda i: (0, 0)) for p in params],
        out_specs=[pl.BlockSpec((ts, w), lambda i: (i, 0)) for w, _ in outs]
        + [pl.BlockSpec((1, w), lambda i: (0, 0)) for w in accs],
        out_shape=[jax.ShapeDtypeStruct((S, w), dt) for w, dt in outs]
        + [jax.ShapeDtypeStruct((1, w), F32) for w in accs],
        compiler_params=_params("arbitrary"),
    )(*ins, *params)
    return res


def _xhat(x):
    r = lax.rsqrt(jnp.mean(x * x, axis=-1, keepdims=True) + NORM_EPS)
    return x * r, r


def _rms_fwd(name, x, g):
    def fn(ins, ps):
        xh, _ = _xhat(ins[0])
        return [xh * ps[0]], []

    return _rowwise(name, fn, [x], [g], [(x.shape[1], BF16)])[0]


def _rms_bwd(name, dn, x, g, dres=None, copy_scale=None):
    w = x.shape[1]

    def fn(ins, ps):
        dn_, x_ = ins[0], ins[1]
        xh, r = _xhat(x_)
        dxh = dn_ * ps[0]
        dx = r * (dxh - xh * jnp.mean(dxh * xh, axis=-1, keepdims=True))
        if dres is not None:
            dx = dx + ins[2]
        o = [dx] + ([dx * copy_scale] if copy_scale is not None else [])
        return o, [jnp.sum(dn_ * xh, axis=0, keepdims=True)]

    ins = [dn, x] + ([dres] if dres is not None else [])
    outs = [(w, F32)] + ([(w, BF16)] if copy_scale is not None else [])
    res = _rowwise(name, fn, ins, [g], outs, accs=[w])
    return res[:-1], res[-1]


def _swiglu_epilogue(accs, ex):
    g, u = accs
    return [g, u, g * _sigmoid(g) * u]


def _dswiglu_epilogue(accs, ex):
    da = accs[0]
    g, u = ex
    sg = _sigmoid(g)
    return [da * u * (sg * (1.0 + g * (1.0 - sg))), da * (g * sg)]


def _ffn_fwd(tag, h, gnorm, wg, wu, wd):
    n = _rms_fwd(tag + "_norm", h, gnorm)
    g, u, a = _mm(tag + "_up", [n], [wg, wu], [F32, F32, BF16], pairs=((0, 0, 0), (0, 1, 1)),
                  epilogue=_swiglu_epilogue, tm=1024, tn=512)
    (hout,) = _mm(tag + "_down", [a], [wd], [F32], extras=[(h, "mn")],
                  epilogue=lambda accs, ex: [ex[0] + 0.5 * accs[0]], tm=1024, tn=1024, tk=512)
    return hout, (n, g, u, a)


def _ffn_bwd(tag, dh, dhb_half, h, gnorm, wg, wu, wd, saved, copy_scale):
    n, g, u, a = saved
    dg, du = _mm(tag + "_dact", [dhb_half], [wd], [BF16, BF16], tb=True, extras=[(g, "mn"), (u, "mn")],
                 epilogue=_dswiglu_epilogue, tm=1024, tn=512)
    (dwd,) = _mm(tag + "_dwd", [a], [dhb_half], [BF16], ta=True, tm=512, tn=2048, tk=1024)
    dwg, dwu = _mm(tag + "_dwgu", [n], [dg, du], [BF16, BF16], pairs=((0, 0, 0), (0, 1, 1)), ta=True,
                   tm=1024, tn=512, tk=1024)
    (dn,) = _mm(tag + "_dn", [dg, du], [wg, wu], [F32], pairs=((0, 0, 0), (1, 1, 0)), tb=True,
                tm=1024, tn=1024, tk=512)
    douts, dgn = _rms_bwd(tag + "_dnorm", dn, h, gnorm, dres=dh, copy_scale=copy_scale)
    return douts, dgn, dwg, dwu, dwd


def _attn_masks(b):
    qi = lax.broadcasted_iota(jnp.int32, (SWA_BLOCK, SWA_BLOCK), 0)
    ki = lax.broadcasted_iota(jnp.int32, (SWA_BLOCK, SWA_BLOCK), 1)
    return (ki >= qi) & (b > 0), ki <= qi


def _head_masks():
    lane = lax.broadcasted_iota(jnp.int32, (SWA_BLOCK, LANES), 1)
    return [lane < HEAD_DIM, lane >= HEAD_DIM]


def _per_head(t, first):
    sw = pltpu.roll(t, HEAD_DIM, 1)
    lo = lax.broadcasted_iota(jnp.int32, t.shape, 1) < HEAD_DIM
    return jnp.where(lo, t, sw) if first else jnp.where(lo, sw, t)


def _dot_nt(a, b):
    return lax.dot_general(a, b, (((1,), (1,)), ((), ())), preferred_element_type=F32)


def _dot_tn(a, b):
    return lax.dot_general(a, b, (((0,), (0,)), ((), ())), preferred_element_type=F32)


def _dot(a, b):
    return jnp.dot(a, b, preferred_element_type=F32)


def _keep(mask, t):
    return jnp.where(mask, t.astype(F32), 0.0).astype(BF16)


def _attn_views(S, A, d):
    hp_n = A // LANES

    def qkv(part, shift):
        def idx(r, hp, b):
            blk = jnp.clip(b + shift, 0, S // d // SWA_BLOCK - 1)
            return (blk, r * 3 * hp_n + part * hp_n + hp)
        return pl.BlockSpec((SWA_BLOCK, LANES), idx)

    def act(shift=0):
        def idx(r, hp, b):
            blk = jnp.clip(b + shift, 0, S // d // SWA_BLOCK - 1)
            return (blk, r * hp_n + hp)
        return pl.BlockSpec((SWA_BLOCK, LANES), idx)

    return qkv, act


def _attn_fwd_stage(name, qkv, d, prev, final):
    S, A3 = qkv.shape
    A = A3 // 3
    L = S // d
    scale = HEAD_DIM ** -0.5
    qv, av = _attn_views(S, A, d)
    has_prev = prev is not None

    def body(*refs):
        q_ref, kp_ref, kc_ref, vp_ref, vc_ref = refs[:5]
        p_refs = refs[5:8] if has_prev else ()
        o_refs = refs[5 + len(p_refs):]
        b = pl.program_id(2)
        vprev, vcur = _attn_masks(b)
        q, kp, kc, vp, vc = q_ref[...], kp_ref[...], kc_ref[...], vp_ref[...], vc_ref[...]
        hm = _head_masks()
        o = jnp.zeros((SWA_BLOCK, LANES), F32)
        m = jnp.zeros((SWA_BLOCK, LANES), F32)
        l = jnp.zeros((SWA_BLOCK, LANES), F32)
        for hh in range(2):
            qh = _keep(hm[hh], q)
            sp = jnp.where(vprev, _dot_nt(qh, kp) * scale, MASK_VALUE)
            sc = jnp.where(vcur, _dot_nt(qh, kc) * scale, MASK_VALUE)
            mh = jnp.maximum(jnp.max(sp, axis=-1, keepdims=True), jnp.max(sc, axis=-1, keepdims=True))
            pp, pc = jnp.exp(sp - mh), jnp.exp(sc - mh)
            lh = jnp.sum(pp, axis=-1, keepdims=True) + jnp.sum(pc, axis=-1, keepdims=True)
            vph = _keep(hm[hh], vp)
            vch = _keep(hm[hh], vc)
            o = o + _dot(pp.astype(BF16), vph) + _dot(pc.astype(BF16), vch)
            m = jnp.where(hm[hh], mh, m)
            l = jnp.where(hm[hh], lh, l)
        if has_prev:
            po, pm, pl_ = (r[...] for r in p_refs)
            mn = jnp.maximum(m, pm)
            w_new, w_old = jnp.exp(m - mn), jnp.exp(pm - mn)
            o = o * w_new + po * w_old
            l = l * w_new + pl_ * w_old
            m = mn
        if final:
            o_refs[0][...] = o / l
            o_refs[1][...] = m + jnp.log(l)
        else:
            o_refs[0][...] = o
            o_refs[1][...] = m
            o_refs[2][...] = l

    n_out = 2 if final else 3
    qk = qkv.reshape(L, d * A3)
    prev_v = [t.reshape(L, d * A) for t in prev] if has_prev else []
    res = _pcall(
        body, name=name,
        grid=(d, A // LANES, L // SWA_BLOCK),
        in_specs=[qv(0, 0), qv(1, -1), qv(1, 0), qv(2, -1), qv(2, 0)] + [av()] * len(prev_v),
        out_specs=[av()] * n_out,
        out_shape=[jax.ShapeDtypeStruct((L, d * A), F32)] * n_out,
        compiler_params=_params("parallel", "parallel", "arbitrary"),
    )(qk, qk, qk, qk, qk, *prev_v)
    return [t.reshape(S, A) for t in res]


def _attn_fwd(qkv):
    st = None
    for i, d in enumerate(DILATIONS):
        st = _attn_fwd_stage("attn_fwd_d%d" % d, qkv, d, st, final=(i == len(DILATIONS) - 1))
    return st


def _attn_dq_stage(name, qkv, do, lse, delta, d, prev):
    S, A3 = qkv.shape
    A = A3 // 3
    L = S // d
    scale = HEAD_DIM ** -0.5
    qv, av = _attn_views(S, A, d)
    has_prev = prev is not None

    def body(*refs):
        q_ref, kp_ref, kc_ref, vp_ref, vc_ref, do_ref, lse_ref, dl_ref = refs[:8]
        b = pl.program_id(2)
        vprev, vcur = _attn_masks(b)
        q, kp, kc, vp, vc = q_ref[...], kp_ref[...], kc_ref[...], vp_ref[...], vc_ref[...]
        do_, lse_, dl_ = do_ref[...], lse_ref[...], dl_ref[...]
        hm = _head_masks()
        dq = jnp.zeros((SWA_BLOCK, LANES), F32)
        for hh in range(2):
            qh = _keep(hm[hh], q)
            doh = _keep(hm[hh], do_)
            lh, dh = _per_head(lse_, hh == 0), _per_head(dl_, hh == 0)
            for k_, v_, valid in ((kp, vp, vprev), (kc, vc, vcur)):
                s = _dot_nt(qh, k_) * scale
                p = jnp.where(valid, jnp.exp(s - lh), 0.0)
                ds = p * (_dot_nt(doh, v_) - dh)
                dq = dq + _dot(ds.astype(BF16), _keep(hm[hh], k_))
        dq = dq * scale
        if has_prev:
            dq = dq + refs[8][...]
        refs[-1][...] = dq

    qk = qkv.reshape(L, d * A3)
    acts = [t.reshape(L, d * A) for t in (do, lse, delta)] + ([prev.reshape(L, d * A)] if has_prev else [])
    res = _pcall(
        body, name=name,
        grid=(d, A // LANES, L // SWA_BLOCK),
        in_specs=[qv(0, 0), qv(1, -1), qv(1, 0), qv(2, -1), qv(2, 0)] + [av()] * len(acts),
        out_specs=av(),
        out_shape=jax.ShapeDtypeStruct((L, d * A), F32),
        compiler_params=_params("parallel", "parallel", "arbitrary"),
    )(qk, qk, qk, qk, qk, *acts)
    return res.reshape(S, A)


def _attn_dkv_stage(name, qkv, do, lse, delta, d, prev):
    S, A3 = qkv.shape
    A = A3 // 3
    L = S // d
    nb = L // SWA_BLOCK
    scale = HEAD_DIM ** -0.5
    qv, av = _attn_views(S, A, d)
    has_prev = prev is not None

    def body(*refs):
        k_ref, v_ref, qc_ref, qn_ref, doc_ref, don_ref, lc_ref, ln_ref, dc_ref, dn_ref = refs[:10]
        j = pl.program_id(2)
        qi = lax.broadcasted_iota(jnp.int32, (SWA_BLOCK, SWA_BLOCK), 0)
        ki = lax.broadcasted_iota(jnp.int32, (SWA_BLOCK, SWA_BLOCK), 1)
        vcur = ki <= qi
        vnext = (ki >= qi) & (j < nb - 1)
        k_, v_ = k_ref[...], v_ref[...]
        hm = _head_masks()
        dk = jnp.zeros((SWA_BLOCK, LANES), F32)
        dv = jnp.zeros((SWA_BLOCK, LANES), F32)
        for hh in range(2):
            for q_r, do_r, l_r, d_r, valid in ((qc_ref, doc_ref, lc_ref, dc_ref, vcur),
                                               (qn_ref, don_ref, ln_ref, dn_ref, vnext)):
                q = q_r[...]
                qh = _keep(hm[hh], q)
                doh = _keep(hm[hh], do_r[...])
                lh, dh = _per_head(l_r[...], hh == 0), _per_head(d_r[...], hh == 0)
                s = _dot_nt(qh, k_) * scale
                p = jnp.where(valid, jnp.exp(s - lh), 0.0)
                dv = dv + _dot_tn(p.astype(BF16), doh)
                ds = p * (_dot_nt(doh, v_) - dh)
                dk = dk + _dot_tn(ds.astype(BF16), qh)
        dk = dk * scale
        if has_prev:
            dk = dk + refs[10][...]
            dv = dv + refs[11][...]
        refs[-2][...] = dk
        refs[-1][...] = dv

    qk = qkv.reshape(L, d * A3)
    acts = [t.reshape(L, d * A) for t in (do, lse, delta)]
    prev_v = [t.reshape(L, d * A) for t in prev] if has_prev else []
    res = _pcall(
        body, name=name,
        grid=(d, A // LANES, nb),
        in_specs=[qv(1, 0), qv(2, 0), qv(0, 0), qv(0, 1), av(), av(1), av(), av(1), av(), av(1)] + [av()] * len(prev_v),
        out_specs=[av(), av()],
        out_shape=[jax.ShapeDtypeStruct((L, d * A), F32)] * 2,
        compiler_params=_params("parallel", "parallel", "arbitrary"),
    )(qk, qk, qk, qk, acts[0], acts[0], acts[1], acts[1], acts[2], acts[2], *prev_v)
    return [t.reshape(S, A) for t in res]


def _attn_delta(dya, ya):
    S, A = ya.shape
    ri = lax.broadcasted_iota(jnp.int32, (A, A), 0) // HEAD_DIM
    ci = lax.broadcasted_iota(jnp.int32, (A, A), 1) // HEAD_DIM
    ones_bd = (ri == ci).astype(BF16)

    def fn(ins, ps):
        prod = ins[0] * ins[1]
        hi = prod.astype(BF16)
        lo = (prod - hi.astype(F32)).astype(BF16)
        return [_dot(hi, ps[0]) + _dot(lo, ps[0])], []

    return _rowwise("attn_delta", fn, [dya, ya], [ones_bd], [(A, F32)])[0]


def _attn_bwd(qkv, dya, ya, lse):
    delta = _attn_delta(dya, ya)
    dq, dkv = None, None
    for d in DILATIONS:
        dq = _attn_dq_stage("attn_dq_d%d" % d, qkv, dya, lse, delta, d, dq)
        dkv = _attn_dkv_stage("attn_dkv_d%d" % d, qkv, dya, lse, delta, d, dkv)
    return dq, dkv[0], dkv[1]


def _ssm_perm(a, T):
    S, w = a.shape
    return a.reshape(S // T, SUBLANES, T // SUBLANES, w).transpose(0, 2, 1, 3).reshape(S, w)


def _ssm_unperm(a, T):
    S, w = a.shape
    return a.reshape(S // T, T // SUBLANES, SUBLANES, w).transpose(0, 2, 1, 3).reshape(S, w)


def _ssm_powers(lam_ref, pw_ref, T, ns):
    n = (lax.broadcasted_iota(jnp.int32, (T, 1), 0) // SUBLANES + 1).astype(F32)
    mag = jnp.exp(n * lam_ref[0, 0:1, :])
    ang = n * lam_ref[0, 1:2, :]
    pw_ref[:, 0:ns] = mag * jnp.cos(ang)
    pw_ref[:, ns:2 * ns] = mag * jnp.sin(ang)


def _ssm_scan(xs, off, pw_ref, carry_ref, T, ns, reverse):
    Tc = T // SUBLANES
    sgn = -1.0 if reverse else 1.0
    ar, ai = pw_ref[0:SUBLANES, 0:ns], sgn * pw_ref[0:SUBLANES, ns:2 * ns]

    def rows(i):
        return pl.ds(pl.multiple_of(off + i * SUBLANES, SUBLANES), SUBLANES)

    def step(k, h):
        hr, hi = h
        r = rows(Tc - 1 - k if reverse else k)
        nr = ar * hr - ai * hi + xs[r, 0:ns]
        ni = ar * hi + ai * hr + xs[r, ns:2 * ns]
        xs[r, 0:ns] = nr
        xs[r, ns:2 * ns] = ni
        return nr, ni

    z = jnp.zeros((SUBLANES, ns), F32)
    er, ei = lax.fori_loop(0, Tc, step, (z, z))
    atr, ati = pw_ref[T - SUBLANES:T, 0:ns], sgn * pw_ref[T - SUBLANES:T, ns:2 * ns]
    rowid = lax.broadcasted_iota(jnp.int32, (SUBLANES, ns), 0)
    cr, ci = carry_ref[:, 0:ns], carry_ref[:, ns:2 * ns]
    ctr, cti = z, z
    for jj in range(SUBLANES):
        j = SUBLANES - 1 - jj if reverse else jj
        sel = rowid == j
        ctr, cti = jnp.where(sel, cr, ctr), jnp.where(sel, ci, cti)
        ejr = jnp.broadcast_to(jnp.sum(jnp.where(sel, er, 0.0), axis=0, keepdims=True), (SUBLANES, ns))
        eji = jnp.broadcast_to(jnp.sum(jnp.where(sel, ei, 0.0), axis=0, keepdims=True), (SUBLANES, ns))
        cr, ci = ejr + atr * cr - ati * ci, eji + atr * ci + ati * cr
    carry_ref[:, 0:ns] = cr
    carry_ref[:, ns:2 * ns] = ci

    def fix(i, _):
        r = rows(i)
        pr_rows = pl.ds(pl.multiple_of((Tc - 1 - i if reverse else i) * SUBLANES, SUBLANES), SUBLANES)
        pr, pi = pw_ref[pr_rows, 0:ns], sgn * pw_ref[pr_rows, ns:2 * ns]
        xs[r, 0:ns] += pr * ctr - pi * cti
        xs[r, ns:2 * ns] += pr * cti + pi * ctr
        return 0

    lax.fori_loop(0, Tc, fix, 0)
    return ctr, cti


def _ssm_fwd(ufp, bb, cc, lam_dt, drow, T):
    S, W = ufp.shape
    GB, cw, ns2 = bb.shape
    ns = ns2 // 2
    NCH = S // T

    def body(uf_ref, bb_ref, cc_ref, lam_ref, d_ref, y_ref, hs_ref, xs, pw, carry):
        @pl.when(pl.program_id(1) == 0)
        def _():
            _ssm_powers(lam_ref, pw, T, ns)
            carry[...] = jnp.zeros(carry.shape, F32)

        uf = uf_ref[...]
        xs[...] = _dot(uf.astype(BF16), bb_ref[0])
        hs_ref[0, 0] = carry[...]
        _ssm_scan(xs, 0, pw, carry, T, ns, reverse=False)
        y_ref[...] = _dot(xs[...].astype(BF16), cc_ref[0]) + d_ref[...] * uf

    return _pcall(
        body, name="ssm_fwd",
        grid=(GB, NCH),
        in_specs=[pl.BlockSpec((T, cw), lambda g, c: (c, g)),
                  pl.BlockSpec((1, cw, ns2), lambda g, c: (g, 0, 0)),
                  pl.BlockSpec((1, ns2, cw), lambda g, c: (g, 0, 0)),
                  pl.BlockSpec((1, 2, ns), lambda g, c: (g, 0, 0)),
                  pl.BlockSpec((1, cw), lambda g, c: (0, g))],
        out_specs=[pl.BlockSpec((T, cw), lambda g, c: (c, g)),
                   pl.BlockSpec((1, 1, SUBLANES, ns2), lambda g, c: (g, c, 0, 0))],
        out_shape=[jax.ShapeDtypeStruct((S, W), F32),
                   jax.ShapeDtypeStruct((GB, NCH, SUBLANES, ns2), F32)],
        scratch_shapes=[pltpu.VMEM((T, ns2), F32), pltpu.VMEM((T, ns2), F32), pltpu.VMEM((SUBLANES, ns2), F32)],
        compiler_params=_params("arbitrary", "arbitrary"),
    )(ufp, bb, cc, lam_dt, drow)


def _ssm_bwd(ufp, dyp, bb, bbt, cc, cct, lam_dt, drow, hstart, T):
    S, W = ufp.shape
    GB, cw, ns2 = bb.shape
    ns = ns2 // 2
    NCH = S // T

    def body(uf_ref, dy_ref, bb_ref, bbt_ref, cc_ref, cct_ref, lam_ref, d_ref, hs_ref,
             duf_ref, dbb_ref, dcc_ref, da_ref, dd_ref, hb, ls, pw, carry_f, carry_b):
        @pl.when(pl.program_id(1) == 0)
        def _():
            _ssm_powers(lam_ref, pw, T, ns)
            carry_b[...] = jnp.zeros(carry_b.shape, F32)
            dbb_ref[...] = jnp.zeros(dbb_ref.shape, F32)
            dcc_ref[...] = jnp.zeros(dcc_ref.shape, F32)
            da_ref[...] = jnp.zeros(da_ref.shape, F32)
            dd_ref[...] = jnp.zeros(dd_ref.shape, F32)

        uf, dy = uf_ref[...], dy_ref[...]
        ufb, dyb = uf.astype(BF16), dy.astype(BF16)
        hb[SUBLANES:T + SUBLANES, :] = _dot(ufb, bb_ref[0])
        carry_f[...] = hs_ref[0, 0]
        ctr, cti = _ssm_scan(hb, SUBLANES, pw, carry_f, T, ns, reverse=False)
        hb[0:SUBLANES, 0:ns] = ctr
        hb[0:SUBLANES, ns:ns2] = cti
        ls[...] = _dot(dyb, cct_ref[0])
        _ssm_scan(ls, 0, pw, carry_b, T, ns, reverse=True)
        lv = ls[...]
        lb = lv.astype(BF16)
        dbb_ref[0] += _dot_tn(ufb, lb)
        dcc_ref[0] += _dot_tn(hb[SUBLANES:T + SUBLANES, :].astype(BF16), dyb)
        lr, li = lv[:, 0:ns], lv[:, ns:ns2]
        hpr, hpi = hb[0:T, 0:ns], hb[0:T, ns:ns2]
        dar = jnp.sum(lr * hpr + li * hpi, axis=0, keepdims=True)
        dai = jnp.sum(li * hpr - lr * hpi, axis=0, keepdims=True)
        da_ref[0, :, 0:ns] += jnp.broadcast_to(dar, (SUBLANES, ns))
        da_ref[0, :, ns:ns2] += jnp.broadcast_to(dai, (SUBLANES, ns))
        duf_ref[...] = _dot(lb, bbt_ref[0]) + d_ref[...] * dy
        dd_ref[...] += jnp.sum(dy * uf, axis=0, keepdims=True)

    rc = lambda c: NCH - 1 - c
    return _pcall(
        body, name="ssm_bwd",
        grid=(GB, NCH),
        in_specs=[pl.BlockSpec((T, cw), lambda g, c: (rc(c), g)),
                  pl.BlockSpec((T, cw), lambda g, c: (rc(c), g)),
                  pl.BlockSpec((1, cw, ns2), lambda g, c: (g, 0, 0)),
                  pl.BlockSpec((1, ns2, cw), lambda g, c: (g, 0, 0)),
                  pl.BlockSpec((1, ns2, cw), lambda g, c: (g, 0, 0)),
                  pl.BlockSpec((1, cw, ns2), lambda g, c: (g, 0, 0)),
                  pl.BlockSpec((1, 2, ns), lambda g, c: (g, 0, 0)),
                  pl.BlockSpec((1, cw), lambda g, c: (0, g)),
                  pl.BlockSpec((1, 1, SUBLANES, ns2), lambda g, c: (g, rc(c), 0, 0))],
        out_specs=[pl.BlockSpec((T, cw), lambda g, c: (rc(c), g)),
                   pl.BlockSpec((1, cw, ns2), lambda g, c: (g, 0, 0)),
                   pl.BlockSpec((1, ns2, cw), lambda g, c: (g, 0, 0)),
                   pl.BlockSpec((1, SUBLANES, ns2), lambda g, c: (g, 0, 0)),
                   pl.BlockSpec((1, cw), lambda g, c: (0, g))],
        out_shape=[jax.ShapeDtypeStruct((S, W), F32),
                   jax.ShapeDtypeStruct((GB, cw, ns2), F32),
                   jax.ShapeDtypeStruct((GB, ns2, cw), F32),
                   jax.ShapeDtypeStruct((GB, SUBLANES, ns2), F32),
                   jax.ShapeDtypeStruct((1, W), F32)],
        scratch_shapes=[pltpu.VMEM((T + SUBLANES, ns2), F32), pltpu.VMEM((T, ns2), F32), pltpu.VMEM((T, ns2), F32),
                        pltpu.VMEM((SUBLANES, ns2), F32), pltpu.VMEM((SUBLANES, ns2), F32)],
        compiler_params=_params("arbitrary", "arbitrary"),
    )(ufp, dyp, bb, bbt, cc, cct, lam_dt, drow, hstart)


def _ssm_disc_math(lr, li, logdt, br, bi):
    dt = jnp.exp(logdt)
    mag = jnp.exp(lr * dt)
    ar = mag * jnp.cos(li * dt)
    ai = mag * jnp.sin(li * dt)
    nr, ni = ar - 1.0, ai
    den = lr * lr + li * li
    cr = (nr * lr + ni * li) / den
    ci = (ni * lr - nr * li) / den
    return ar, ai, cr * br - ci * bi, cr * bi + ci * br


def _ssm_disc(lr, li, logdt, br, bi):
    C = br.shape[1]

    def fn(ins, ps):
        _, _, bbr, bbi = _ssm_disc_math(*ins)
        dt = jnp.exp(ins[2])
        return [ins[0] * dt, ins[1] * dt, bbr, bbi], []

    return _rowwise("ssm_disc", fn, [lr, li, logdt, br, bi], [], [(1, F32), (1, F32), (C, F32), (C, F32)], ts=512)


def _ssm_disc_bwd(lr, li, logdt, br, bi, dar, dai, dbbr, dbbi):
    C = br.shape[1]

    def fn(ins, ps):
        _, vjp = jax.vjp(_ssm_disc_math, *ins[:5])
        return list(vjp(tuple(ins[5:]))), []

    return _rowwise("ssm_disc_bwd", fn, [lr, li, logdt, br, bi, dar, dai, dbbr, dbbi], [],
                    [(1, F32), (1, F32), (1, F32), (C, F32), (C, F32)], ts=512)


def _block_diag(t):
    GB, g, a, b = t.shape
    eye = jnp.eye(g, dtype=t.dtype)
    return (t[:, :, :, None, :] * eye[None, :, None, :, None]).reshape(GB, g * a, g * b)


def _block_diag_take(t, g):
    GB, ga, gb_ = t.shape
    a, b = ga // g, gb_ // g
    eye = jnp.eye(g, dtype=t.dtype)
    return (t.reshape(GB, g, a, g, b) * eye[None, :, None, :, None]).sum(axis=3)


def _loss_head(h4, tgt, gf):
    D = h4.shape[1]

    def fn(ins, ps):
        x, t = ins
        xh, r = _xhat(x)
        err = xh * ps[0] - t
        dn = err * (1.0 / D)
        dxh = dn * ps[0]
        dx = r * (dxh - xh * jnp.mean(dxh * xh, axis=-1, keepdims=True))
        return [dx], [jnp.sum(err * err, axis=0, keepdims=True), jnp.sum(dn * xh, axis=0, keepdims=True)]

    return _rowwise("loss_head", fn, [h4, tgt], [gf], [(D, F32)], accs=[D, D])


def _gelu(x):
    return 0.5 * x * (1.0 + jnp.tanh(GELU_C * (x + GELU_K * x * x * x)))


def _gelu_grad(x):
    t = jnp.tanh(GELU_C * (x + GELU_K * x * x * x))
    return 0.5 * (1.0 + t) + 0.5 * x * (1.0 - t * t) * GELU_C * (1.0 + 3.0 * GELU_K * x * x)


def _mesh_pos():
    return lax.axis_index("x"), lax.axis_index("y"), lax.axis_index("c")


def _other_chips(x, y):
    return [(1 - x, y), (x, 1 - y), (1 - x, 1 - y)]


def _remote(src, dst, send, recv, dev):
    return pltpu.make_async_remote_copy(src_ref=src, dst_ref=dst, send_sem=send, recv_sem=recv,
                                        device_id=dev, device_id_type=MESH)


ANY = pl.BlockSpec(memory_space=pl.ANY)


def _shard_region(ref, axis, shape, s, half=None):
    R, Cc = shape
    hr = R // 2
    if axis == 1:
        rows = pl.ds(0, R) if half is None else pl.ds(pl.multiple_of(half * hr, 16), hr)
        return ref.at[rows, pl.ds(pl.multiple_of(s * Cc, LANES), Cc)]
    if half is None:
        return ref.at[pl.ds(pl.multiple_of(s * R, 16), R), :]
    return ref.at[pl.ds(pl.multiple_of(s * R + half * hr, 16), hr), :]


def _all_gather_weights(shards, axes):
    nw = len(shards)
    shapes = [s.shape for s in shards]
    full = [((R, N_CHIPS * Cc) if ax == 1 else (N_CHIPS * R, Cc)) for (R, Cc), ax in zip(shapes, axes)]

    def body(*refs):
        sh, out = refs[:nw], refs[nw:2 * nw]
        send, recv, loc = refs[2 * nw:]
        x, y, c = _mesh_pos()
        me, sib = (x, y, c), (x, y, 1 - c)
        chips = _other_chips(x, y)
        sid = [2 * cx + cy for cx, cy in chips]
        my_s = 2 * x + y

        def reg(w, s, half=None):
            return _shard_region(out[w], axes[w], shapes[w], s, half)

        local = [pltpu.make_async_copy(sh[w], reg(w, my_s), loc.at[w]) for w in range(nw)]
        for cp in local:
            cp.start()
        sends = []
        for w in range(nw):
            hr = shapes[w][0] // 2
            src = sh[w].at[pl.ds(pl.multiple_of(c * hr, 16), hr), :]
            for j, chip in enumerate(chips):
                cp = _remote(src, reg(w, my_s, c), send.at[6 * w + j], recv.at[6 * w + j], (*chip, c))
                cp.start()
                sends.append(cp)
        for w in range(nw):
            for j in range(3):
                r_ = reg(w, sid[j], c)
                _remote(r_, r_, send.at[6 * w + j], recv.at[6 * w + j], me).wait_recv()
                cp = _remote(r_, r_, send.at[6 * w + 3 + j], recv.at[6 * w + 3 + j], sib)
                cp.start()
                sends.append(cp)
        for w in range(nw):
            for j in range(3):
                r_ = reg(w, sid[j], 1 - c)
                _remote(r_, r_, send.at[6 * w + 3 + j], recv.at[6 * w + 3 + j], me).wait_recv()
        for cp in sends:
            cp.wait_send()
        for cp in local:
            cp.wait()

    return _pcall(
        body, name="ag_weights",
        in_specs=[ANY] * nw, out_specs=[ANY] * nw,
        out_shape=[jax.ShapeDtypeStruct(f, BF16) for f in full],
        scratch_shapes=[pltpu.SemaphoreType.DMA((6 * nw,)), pltpu.SemaphoreType.DMA((6 * nw,)),
                        pltpu.SemaphoreType.DMA((nw,))],
    )(*shards)


def _half_of_all(ref, axis, hshape, half):
    hr = hshape[-2]
    rows = pl.ds(pl.multiple_of(half * hr, 16), hr)
    return ref.at[rows, :] if axis == 1 else ref.at[:, rows, :]


def _rs_pair(grads, axes):
    nw = len(grads)
    hshapes = [((g.shape[0] // 2, g.shape[1]) if ax == 1 else (g.shape[0], g.shape[1] // 2, g.shape[2]))
               for g, ax in zip(grads, axes)]

    def body(*refs):
        g, mine, theirs = refs[:nw], refs[nw:2 * nw], refs[2 * nw:3 * nw]
        send, recv, loc = refs[3 * nw:]
        x, y, c = _mesh_pos()
        cps = []
        for w in range(nw):
            lc = pltpu.make_async_copy(_half_of_all(g[w], axes[w], hshapes[w], c), mine[w], loc.at[w])
            rc = _remote(_half_of_all(g[w], axes[w], hshapes[w], 1 - c), theirs[w], send.at[w], recv.at[w], (x, y, 1 - c))
            lc.start()
            rc.start()
            cps.append((lc, rc))
        for lc, rc in cps:
            rc.wait()
            lc.wait()

    res = _pcall(
        body, name="rs_pair",
        in_specs=[ANY] * nw, out_specs=[ANY] * (2 * nw),
        out_shape=[jax.ShapeDtypeStruct(h, BF16) for h in hshapes] * 2,
        scratch_shapes=[pltpu.SemaphoreType.DMA((nw,)), pltpu.SemaphoreType.DMA((nw,)), pltpu.SemaphoreType.DMA((nw,))],
    )(*grads)
    return res[:nw], res[nw:]


def _rs_chips(qs, axes):
    nw = len(qs)
    pshapes = [((q.shape[0], q.shape[1] // N_CHIPS) if ax == 1 else q.shape[1:]) for q, ax in zip(qs, axes)]

    def body(*refs):
        q, own, got = refs[:nw], refs[nw:2 * nw], refs[2 * nw:3 * nw]
        send, recv, loc = refs[3 * nw:]
        x, y, c = _mesh_pos()
        chips = _other_chips(x, y)

        def part(w, s):
            if axes[w] == 1:
                cw = pshapes[w][1]
                return q[w].at[:, pl.ds(pl.multiple_of(s * cw, LANES), cw)]
            return q[w].at[s]

        cps = []
        for w in range(nw):
            lc = pltpu.make_async_copy(part(w, 2 * x + y), own[w], loc.at[w])
            lc.start()
            cps.append(lc)
            for j, (cx, cy) in enumerate(chips):
                rc = _remote(part(w, 2 * cx + cy), got[w].at[j], send.at[3 * w + j], recv.at[3 * w + j], (cx, cy, c))
                rc.start()
                cps.append(rc)
        for cp in cps:
            cp.wait()

    res = _pcall(
        body, name="rs_chips",
        in_specs=[ANY] * nw, out_specs=[ANY] * (2 * nw),
        out_shape=[jax.ShapeDtypeStruct(p, BF16) for p in pshapes]
        + [jax.ShapeDtypeStruct((3,) + tuple(p), BF16) for p in pshapes],
        scratch_shapes=[pltpu.SemaphoreType.DMA((3 * nw,)), pltpu.SemaphoreType.DMA((3 * nw,)), pltpu.SemaphoreType.DMA((nw,))],
    )(*qs)
    return res[:nw], res[nw:]


def _rs_swap(halves):
    nw = len(halves)
    shapes = [h.shape for h in halves]

    def body(*refs):
        h, out = refs[:nw], refs[nw:2 * nw]
        send, recv, loc = refs[2 * nw:]
        x, y, c = _mesh_pos()
        cps = []
        for w in range(nw):
            hr = shapes[w][0]
            dst = out[w].at[pl.ds(pl.multiple_of(c * hr, 8), hr), :]
            lc = pltpu.make_async_copy(h[w], dst, loc.at[w])
            rc = _remote(h[w], dst, send.at[w], recv.at[w], (x, y, 1 - c))
            lc.start()
            rc.start()
            cps += [lc, rc]
        for cp in cps:
            cp.wait()

    return _pcall(
        body, name="rs_swap",
        in_specs=[ANY] * nw, out_specs=[ANY] * nw,
        out_shape=[jax.ShapeDtypeStruct((2 * r, cc), F32) for r, cc in shapes],
        scratch_shapes=[pltpu.SemaphoreType.DMA((nw,)), pltpu.SemaphoreType.DMA((nw,)), pltpu.SemaphoreType.DMA((nw,))],
    )(*halves)


def _sum_tiles(name, arrs, out_dtype):
    R, Cc = arrs[0].shape
    tr = _tile(R, max(16, (1 << 20) // Cc // 16 * 16), 16)

    def body(*refs):
        acc = refs[0][...].astype(F32)
        for r in refs[1:-1]:
            acc = acc + r[...].astype(F32)
        refs[-1][...] = acc.astype(out_dtype)

    return _pcall(
        body, name=name, grid=(R // tr,),
        in_specs=[pl.BlockSpec((tr, Cc), lambda i: (i, 0))] * len(arrs),
        out_specs=pl.BlockSpec((tr, Cc), lambda i: (i, 0)),
        out_shape=jax.ShapeDtypeStruct((R, Cc), out_dtype),
        compiler_params=_params("parallel"),
    )(*arrs)


def _reduce_scatter(grads, axes):
    g3 = [g if ax == 1 else g.reshape(N_CHIPS, g.shape[0] // N_CHIPS, g.shape[1]) for g, ax in zip(grads, axes)]
    mine, theirs = _rs_pair(g3, axes)
    qs = []
    for w, (a, b) in enumerate(zip(mine, theirs)):
        q = _sum_tiles("rs_sum_pair%d" % w, [a.reshape(-1, a.shape[-1]), b.reshape(-1, b.shape[-1])], BF16)
        qs.append(q.reshape(a.shape))
    own, got = _rs_chips(qs, axes)
    halves = [_sum_tiles("rs_sum_chips%d" % w, [o, g_[0], g_[1], g_[2]], F32) for w, (o, g_) in enumerate(zip(own, got))]
    return _rs_swap(halves)


def _all_reduce_small(v):
    n = v.shape[0]

    def body(v_ref, out_ref, buf, send, recv):
        x, y, c = _mesh_pos()
        my = 4 * x + 2 * y + c
        buf[my] = v_ref[...]
        cps = []
        for k in range(1, N_DEV):
            fx, fy, fc = (k >> 2) & 1, (k >> 1) & 1, k & 1
            peer = (1 - x if fx else x, 1 - y if fy else y, 1 - c if fc else c)
            cp = _remote(v_ref, buf.at[my], send.at[k - 1], recv.at[k - 1], peer)
            cp.start()
            cps.append((cp, 4 * peer[0] + 2 * peer[1] + peer[2]))
        for k, (cp, pid) in enumerate(cps):
            _remote(v_ref, buf.at[pid], send.at[k], recv.at[k], (x, y, c)).wait_recv()
        acc = buf[0]
        for i in range(1, N_DEV):
            acc = acc + buf[i]
        out_ref[...] = acc
        for cp, _ in cps:
            cp.wait_send()

    return _pcall(
        body, name="ar_small",
        in_specs=[pl.BlockSpec(memory_space=pltpu.VMEM)], out_specs=pl.BlockSpec(memory_space=pltpu.VMEM),
        out_shape=jax.ShapeDtypeStruct((n, LANES), F32),
        scratch_shapes=[pltpu.VMEM((N_DEV, n, LANES), F32), pltpu.SemaphoreType.DMA((N_DEV - 1,)),
                        pltpu.SemaphoreType.DMA((N_DEV - 1,))],
        compiler_params=pltpu.CompilerParams(vmem_limit_bytes=V7X_VMEM_LIMIT_BYTES),
    )(v)


def _adamw(name, w, g, m, v):
    R, Cc = w.shape
    tr = _tile(R, max(8, (1 << 19) // Cc // 8 * 8), 8)
    c1 = 1.0 - ADAM_B1 ** ADAM_STEP
    c2 = 1.0 - ADAM_B2 ** ADAM_STEP

    def body(w_ref, g_ref, m_ref, v_ref, d_ref, nm_ref, nv_ref):
        g_ = g_ref[...]
        nm = ADAM_B1 * m_ref[...] + (1.0 - ADAM_B1) * g_
        nv = ADAM_B2 * v_ref[...] + (1.0 - ADAM_B2) * (g_ * g_)
        d_ref[...] = -ADAM_LR * ((nm / c1) / (jnp.sqrt(nv / c2) + ADAM_EPS) + ADAM_WD * w_ref[...])
        nm_ref[...] = nm
        nv_ref[...] = nv

    spec = pl.BlockSpec((tr, Cc), lambda i: (i, 0))
    return _pcall(
        body, name=name, grid=(R // tr,),
        in_specs=[spec] * 4, out_specs=[spec] * 3,
        out_shape=[jax.ShapeDtypeStruct((R, Cc), F32)] * 3,
        compiler_params=_params("parallel"),
    )(w, g, m, v)


def _pack(arrs, rows):
    flat = jnp.concatenate([a.reshape(-1) for a in arrs])
    return jnp.pad(flat, (0, rows * LANES - flat.shape[0])).reshape(rows, LANES)


def _unpack(packed, like):
    flat, out, o = packed.reshape(-1), [], 0
    for a in like:
        out.append(flat[o:o + a.size].reshape(a.shape))
        o += a.size
    return out


BIG = (
    ("ffn1_w_gate", 1), ("ffn1_w_up", 1), ("ffn1_w_down", 0), ("w_in", 1), ("ssm_w_glu", 0), ("w_out", 0),
    ("ffn2_w_gate", 1), ("ffn2_w_up", 1), ("ffn2_w_down", 0), ("ple_w_gate", 0), ("ple_w_proj", 1),
)
SMALL = ("ffn1_norm", "mix_norm", "attn_out_norm", "ssm_lambda_re", "ssm_lambda_im", "ssm_log_dt", "ssm_b_re", "ssm_b_im",
         "ssm_c_re", "ssm_c_im", "ssm_d", "ssm_b_glu", "ssm_out_norm", "ffn2_norm", "ple_norm", "final_norm")
WEIGHTS = ("ffn1_norm", "ffn1_w_gate", "ffn1_w_up", "ffn1_w_down", "mix_norm", "w_in", "attn_out_norm", "ssm_lambda_re",
           "ssm_lambda_im", "ssm_log_dt", "ssm_b_re", "ssm_b_im", "ssm_c_re", "ssm_c_im", "ssm_d", "ssm_w_glu", "ssm_b_glu",
           "ssm_out_norm", "w_out", "ffn2_norm", "ffn2_w_gate", "ffn2_w_up", "ffn2_w_down", "ple_norm", "ple_w_gate",
           "ple_w_proj", "final_norm")


def _pad_to(a, axis, n):
    pad = [(0, 0), (0, 0)]
    pad[axis] = (0, n - a.shape[axis])
    return jnp.pad(a, pad)


def _local_step(x, p, tgt, w, full):
    S, D = x.shape
    A = w["attn_out_norm"].shape[-1]
    W = w["ssm_d"].shape[-1]
    G, P = w["ssm_lambda_re"].shape[-2:]
    C = w["ssm_b_re"].shape[-1]
    GB = G // SSM_BLOCK_GROUPS
    T = min(1024, S)
    row = lambda name: w[name].reshape(1, -1)
    gs = {}

    h1, ffn1_saved = _ffn_fwd("ffn1", x, row("ffn1_norm"), full["ffn1_w_gate"], full["ffn1_w_up"], full["ffn1_w_down"])
    n2 = _rms_fwd("mix_norm", h1, row("mix_norm"))
    w_in = full["w_in"]
    (qkv,) = _mm("w_in_qkv", [n2], [w_in[:, :3 * A]], [BF16], tm=1024, tn=1024)
    (s_in,) = _mm("w_in_ssm", [n2], [w_in[:, 3 * A:]], [F32], tm=1024, tn=1024)
    ya, lse = _attn_fwd(qkv)

    col = lambda name: w[name].reshape(G * P, 1)
    logdt_x = jnp.repeat(w["ssm_log_dt"].reshape(G), P).reshape(G * P, 1)
    b_re, b_im = w["ssm_b_re"].reshape(G * P, C), w["ssm_b_im"].reshape(G * P, C)
    lrdt, lidt, bbr, bbi = _ssm_disc(col("ssm_lambda_re"), col("ssm_lambda_im"), logdt_x, b_re, b_im)
    gsz = SSM_BLOCK_GROUPS
    to_bb = lambda t: _block_diag(t.reshape(GB, gsz, P, C).transpose(0, 1, 3, 2))
    bb = jnp.concatenate([to_bb(bbr), to_bb(bbi)], axis=2).astype(BF16)
    to_cc = lambda t: _block_diag(t.reshape(GB, gsz, C, P).transpose(0, 1, 3, 2))
    cc = jnp.concatenate([to_cc(w["ssm_c_re"]), -to_cc(w["ssm_c_im"])], axis=1).astype(BF16)
    lam_dt = jnp.stack([lrdt.reshape(GB, gsz * P), lidt.reshape(GB, gsz * P)], axis=1)
    ufp = _ssm_perm(s_in, T)
    ypre, hstart = _ssm_fwd(ufp, bb, cc, lam_dt, row("ssm_d"), T)

    def glu_in(ins, ps):
        yg = _gelu(ins[0])
        return [yg, yg], []

    yg, ygb = _rowwise("ssm_gelu", glu_in, [ypre], [], [(W, F32), (W, BF16)])
    w_glu = full["ssm_w_glu"]

    def glu_out(accs, ex):
        gl = accs[0] + ex[1]
        return [ex[0] * _sigmoid(gl), gl]

    ybp, gl = _mm("ssm_glu", [ygb], [w_glu], [F32, F32], extras=[(yg, "mn"), (row("ssm_b_glu"), "n")],
                  epilogue=glu_out, tm=1024, tn=1024)
    yb = _ssm_unperm(ybp, T)
    na = _rms_fwd("attn_out_norm", ya, row("attn_out_norm"))
    nb = _rms_fwd("ssm_out_norm", yb, row("ssm_out_norm"))
    w_out = full["w_out"]
    (h2,) = _mm("w_out", [na, nb], [w_out[:A], w_out[A:]], [F32], pairs=((0, 0, 0), (1, 1, 0)), extras=[(h1, "mn")],
                epilogue=lambda accs, ex: [ex[0] + accs[0]], tm=1024, tn=1024)
    h3, ffn2_saved = _ffn_fwd("ffn2", h2, row("ffn2_norm"), full["ffn2_w_gate"], full["ffn2_w_up"], full["ffn2_w_down"])
    n4 = _rms_fwd("ple_norm", h3, row("ple_norm"))
    (pe,) = _mm("ple_proj", [p], [full["ple_w_proj"]], [F32], tm=1024, tn=1024)

    def ple_out(accs, ex):
        gate = _sigmoid(accs[0])
        return [ex[1] + gate * ex[0], gate]

    h4, gate = _mm("ple_gate", [n4], [full["ple_w_gate"]], [F32, F32], extras=[(pe, "mn"), (h3, "mn")],
                   epilogue=ple_out, tm=1024, tn=1024)

    dh4, err2, gs["final_norm"] = _loss_head(h4, tgt, row("final_norm"))
    loss = (0.5 / D) * jnp.sum(err2)

    def ple_bwd(ins, ps):
        dh, gt, pe_ = ins
        return [dh * gt, dh * pe_ * gt * (1.0 - gt)], []

    dpe, dpg = _rowwise("ple_bwd", ple_bwd, [dh4, gate, pe], [], [(D, BF16), (D, BF16)])
    (d_ple_proj,) = _mm("ple_dproj", [p], [dpe], [BF16], ta=True, tm=256, tn=2048, tk=1024)
    (d_ple_gate,) = _mm("ple_dgate", [n4], [dpg], [BF16], ta=True, tm=1024, tn=1024, tk=1024)
    (dn4,) = _mm("ple_dn", [dpg], [full["ple_w_gate"]], [F32], tb=True, tm=1024, tn=1024)
    (dh3, dh3b), gs["ple_norm"] = _rms_bwd("ple_dnorm", dn4, h3, row("ple_norm"), dres=dh4, copy_scale=0.5)
    (dh2, dh2b), gs["ffn2_norm"], d_ffn2_g, d_ffn2_u, d_ffn2_d = _ffn_bwd(
        "ffn2", dh3, dh3b, h2, row("ffn2_norm"), full["ffn2_w_gate"], full["ffn2_w_up"], full["ffn2_w_down"],
        ffn2_saved, copy_scale=1.0)
    (dna,) = _mm("w_out_dna", [dh2b], [w_out[:A]], [F32], tb=True, tm=1024, tn=1024)
    (dnb,) = _mm("w_out_dnb", [dh2b], [w_out[A:]], [F32], tb=True, tm=1024, tn=1024)
    (d_wout_a,) = _mm("w_out_dwa", [na], [dh2b], [BF16], ta=True, tm=1024, tn=1024, tk=1024)
    (d_wout_b,) = _mm("w_out_dwb", [nb], [dh2b], [BF16], ta=True, tm=1024, tn=1024, tk=1024)
    d_w_out = jnp.concatenate([d_wout_a, d_wout_b], axis=0)
    (dya,), gs["attn_out_norm"] = _rms_bwd("attn_out_dnorm", dna, ya, row("attn_out_norm"))
    (dyb,), gs["ssm_out_norm"] = _rms_bwd("ssm_out_dnorm", dnb, yb, row("ssm_out_norm"))

    dybp = _ssm_perm(dyb, T)

    def glu_bwd(ins, ps):
        dy, yg_, gl_ = ins
        sg = _sigmoid(gl_)
        dgl = dy * yg_ * sg * (1.0 - sg)
        return [dgl, dy * sg], [jnp.sum(dgl, axis=0, keepdims=True)]

    dgl, dyg_direct, gs["ssm_b_glu"] = _rowwise("ssm_glu_bwd", glu_bwd, [dybp, yg, gl], [], [(W, BF16), (W, F32)], accs=[W])
    (d_w_glu,) = _mm("ssm_dwglu", [ygb], [dgl], [BF16], ta=True, tm=1024, tn=1024, tk=1024)
    (dypre,) = _mm("ssm_dyg", [dgl], [w_glu], [F32], tb=True, extras=[(dyg_direct, "mn"), (ypre, "mn")],
                   epilogue=lambda accs, ex: [(accs[0] + ex[0]) * _gelu_grad(ex[1])], tm=1024, tn=1024)
    dufp, dbb, dcc, da, gs["ssm_d"] = _ssm_bwd(ufp, dypre, bb, bb.transpose(0, 2, 1), cc, cc.transpose(0, 2, 1),
                                               lam_dt, row("ssm_d"), hstart, T)
    ns = gsz * P
    from_bb = lambda t: _block_diag_take(t, gsz).transpose(0, 1, 3, 2).reshape(G * P, C)
    from_cc = lambda t: _block_diag_take(t, gsz).transpose(0, 1, 3, 2).reshape(w["ssm_c_re"].shape)
    gs["ssm_c_re"], gs["ssm_c_im"] = from_cc(dcc[:, :ns]), -from_cc(dcc[:, ns:])
    dar, dai = da[:, 0, :ns].reshape(G * P, 1), da[:, 0, ns:].reshape(G * P, 1)
    dlr, dli, dlogdt, dbr, dbi = _ssm_disc_bwd(col("ssm_lambda_re"), col("ssm_lambda_im"), logdt_x, b_re, b_im,
                                               dar, dai, from_bb(dbb[:, :, :ns]), from_bb(dbb[:, :, ns:]))
    gs["ssm_lambda_re"], gs["ssm_lambda_im"] = dlr.reshape(w["ssm_lambda_re"].shape), dli.reshape(w["ssm_lambda_im"].shape)
    gs["ssm_log_dt"] = dlogdt.reshape(G, P).sum(axis=1).reshape(w["ssm_log_dt"].shape)
    gs["ssm_b_re"], gs["ssm_b_im"] = dbr.reshape(w["ssm_b_re"].shape), dbi.reshape(w["ssm_b_im"].shape)
    ds_in = _ssm_unperm(dufp, T)

    dq, dk, dv = _attn_bwd(qkv, dya, ya, lse)
    dz = jnp.concatenate([dq, dk, dv, ds_in], axis=1).astype(BF16)
    (d_w_in,) = _mm("w_in_dw", [n2], [dz], [BF16], ta=True, tm=1024, tn=1024, tk=1024)
    (dn2,) = _mm("w_in_dn", [dz], [w_in], [F32], tb=True, tm=1024, tn=1024)
    (dh1, dh1b), gs["mix_norm"] = _rms_bwd("mix_dnorm", dn2, h1, row("mix_norm"), dres=dh2, copy_scale=0.5)
    (dx,), gs["ffn1_norm"], d_ffn1_g, d_ffn1_u, d_ffn1_d = _ffn_bwd(
        "ffn1", dh1, dh1b, x, row("ffn1_norm"), full["ffn1_w_gate"], full["ffn1_w_up"], full["ffn1_w_down"],
        ffn1_saved, copy_scale=None)

    big = {"ffn1_w_gate": d_ffn1_g, "ffn1_w_up": d_ffn1_u, "ffn1_w_down": d_ffn1_d, "w_in": d_w_in, "ssm_w_glu": d_w_glu,
           "w_out": d_w_out, "ffn2_w_gate": d_ffn2_g, "ffn2_w_up": d_ffn2_u, "ffn2_w_down": d_ffn2_d,
           "ple_w_gate": d_ple_gate, "ple_w_proj": d_ple_proj}
    small = {k: gs[k].reshape(w[k].shape) for k in SMALL}
    return loss, dx, big, small


def kernel(x, p, ffn1_norm, ffn1_w_gate, ffn1_w_up, ffn1_w_down, mix_norm, w_in, attn_out_norm, ssm_lambda_re, ssm_lambda_im, ssm_log_dt, ssm_b_re, ssm_b_im, ssm_c_re, ssm_c_im, ssm_d, ssm_w_glu, ssm_b_glu, ssm_out_norm, w_out, ffn2_norm, ffn2_w_gate, ffn2_w_up, ffn2_w_down, ple_norm, ple_w_gate, ple_w_proj, final_norm, loss_target, m_ffn1_norm, m_ffn1_w_gate, m_ffn1_w_up, m_ffn1_w_down, m_mix_norm, m_w_in, m_attn_out_norm, m_ssm_lambda_re, m_ssm_lambda_im, m_ssm_log_dt, m_ssm_b_re, m_ssm_b_im, m_ssm_c_re, m_ssm_c_im, m_ssm_d, m_ssm_w_glu, m_ssm_b_glu, m_ssm_out_norm, m_w_out, m_ffn2_norm, m_ffn2_w_gate, m_ffn2_w_up, m_ffn2_w_down, m_ple_norm, m_ple_w_gate, m_ple_w_proj, m_final_norm, v_ffn1_norm, v_ffn1_w_gate, v_ffn1_w_up, v_ffn1_w_down, v_mix_norm, v_w_in, v_attn_out_norm, v_ssm_lambda_re, v_ssm_lambda_im, v_ssm_log_dt, v_ssm_b_re, v_ssm_b_im, v_ssm_c_re, v_ssm_c_im, v_ssm_d, v_ssm_w_glu, v_ssm_b_glu, v_ssm_out_norm, v_w_out, v_ffn2_norm, v_ffn2_w_gate, v_ffn2_w_up, v_ffn2_w_down, v_ple_norm, v_ple_w_gate, v_ple_w_proj, v_final_norm):
    args = locals()
    w = {k: args[k] for k in WEIGHTS}
    m = {k: args["m_" + k] for k in WEIGHTS}
    v = {k: args["v_" + k] for k in WEIGHTS}
    w2 = {k: w[k].reshape(w[k].shape[-2:]) for k, _ in BIG}

    axes = [ax for _, ax in BIG]
    padded = {k: -(-w2[k].shape[ax] // LANES) * LANES for k, ax in BIG}
    shards = [_pad_to(w2[k].astype(BF16), ax, padded[k]) for k, ax in BIG]
    full = dict(zip([k for k, _ in BIG], _all_gather_weights(shards, axes)))

    loss_local, dx, gbig, gsmall = _local_step(x[0], p[0, 0], loss_target[0], w, full)
    loss = lax.psum(loss_local, MESH_AXES)

    summed = _reduce_scatter([gbig[k] for k, _ in BIG], axes)
    n_small = sum(w[k].size for k in SMALL)
    rows = -(-n_small // (SUBLANES * LANES)) * SUBLANES
    gs_sum = _all_reduce_small(_pack([gsmall[k] for k in SMALL], rows))

    grads, delta, new_m, new_v = {}, {}, {}, {}
    for (k, ax), gfull in zip(BIG, summed):
        g2 = lax.slice_in_dim(gfull, 0, w2[k].shape[ax], axis=ax)
        d2, nm2, nv2 = _adamw("adamw_" + k, w2[k], g2, m[k].reshape(w2[k].shape), v[k].reshape(w2[k].shape))
        grads[k], delta[k], new_m[k], new_v[k] = (t.reshape(w[k].shape) for t in (g2, d2, nm2, nv2))
    small_like = [w[k] for k in SMALL]
    ds, nms, nvs = _adamw("adamw_small", _pack(small_like, rows), gs_sum, _pack([m[k] for k in SMALL], rows),
                          _pack([v[k] for k in SMALL], rows))
    for k, g_, d_, nm_, nv_ in zip(SMALL, _unpack(gs_sum, small_like), _unpack(ds, small_like),
                                   _unpack(nms, small_like), _unpack(nvs, small_like)):
        grads[k], delta[k], new_m[k], new_v[k] = g_, d_, nm_, nv_

    return (loss, dx[None], *[grads[k] for k in WEIGHTS], *[delta[k] for k in WEIGHTS],
            *[new_m[k] for k in WEIGHTS], *[new_v[k] for k in WEIGHTS])
```

```python
import functools
import math

import jax
import jax.numpy as jnp
from jax import lax
from jax.experimental import pallas as pl
from jax.experimental.pallas import tpu as pltpu

F32 = jnp.float32
BF16 = jnp.bfloat16
MESH = pl.DeviceIdType.MESH
MESH_AXES = ("x", "y", "c")
N_CHIPS = 4
N_DEV = 8

V7X_VMEM_LIMIT_BYTES = 56 << 20
LANES = 128
SUBLANES = 8

HEAD_DIM = 64
SWA_BLOCK = 128
DILATIONS = (1, 4, 16)
SSM_BLOCK_GROUPS = 8
NORM_EPS = 1e-6
MASK_VALUE = -1e30

ADAM_LR = 0.001
ADAM_B1 = 0.9
ADAM_B2 = 0.999
ADAM_EPS = 1e-08
ADAM_WD = 0.01
ADAM_STEP = 10

GELU_C = math.sqrt(2.0 / math.pi)
GELU_K = 0.044715


def _pcall(body, **kw):
    return pl.pallas_call(body, **kw)


def _params(*sem):
    return pltpu.CompilerParams(dimension_semantics=sem, vmem_limit_bytes=V7X_VMEM_LIMIT_BYTES)


def _tile(n, target, align):
    best = None
    for t in range(align, min(n, target) + 1, align):
        if n % t == 0:
            best = t
    return n if best is None else best


def _sigmoid(x):
    return 1.0 / (1.0 + jnp.exp(-x))


def _mm(name, lhs, rhs, outs, pairs=((0, 0, 0),), epilogue=None, extras=(), ta=False, tb=False,
        tm=1024, tn=512, tk=2048):
    nl, nr, ne, no = len(lhs), len(rhs), len(extras), len(outs)
    n_acc = 1 + max(p[2] for p in pairs)
    (K, M) = lhs[0].shape if ta else lhs[0].shape[::-1]
    (N, K2) = rhs[0].shape if tb else rhs[0].shape[::-1]
    assert K == K2, (name, lhs[0].shape, rhs[0].shape)
    tm, tn, tk = _tile(M, tm, LANES), _tile(N, tn, LANES), _tile(K, tk, LANES)
    nk = K // tk
    if epilogue is None:
        epilogue = lambda accs, ex: accs
    dn = (((0 if ta else 1,), (1 if tb else 0,)), ((), ()))

    def body(*refs):
        l, r = refs[:nl], refs[nl:nl + nr]
        e = refs[nl + nr:nl + nr + ne]
        o = refs[nl + nr + ne:nl + nr + ne + no]
        acc = refs[nl + nr + ne + no:]
        k = pl.program_id(2)
        parts = [None] * n_acc
        for li, ri, ai in pairs:
            d = lax.dot_general(l[li][...].astype(BF16), r[ri][...].astype(BF16), dn,
                                preferred_element_type=F32)
            parts[ai] = d if parts[ai] is None else parts[ai] + d

        def finish(accs):
            res = epilogue(accs, [x[...] for x in e])
            for ref, val in zip(o, res):
                ref[...] = val.astype(ref.dtype)

        if nk == 1:
            finish(parts)
        else:
            @pl.when(k == 0)
            def _():
                for ai in range(n_acc):
                    acc[ai][...] = parts[ai]

            @pl.when(k > 0)
            def _():
                for ai in range(n_acc):
                    acc[ai][...] += parts[ai]

            @pl.when(k == nk - 1)
            def _():
                finish([a[...] for a in acc])

    lspec = pl.BlockSpec((tk, tm), lambda i, j, k: (k, i)) if ta else pl.BlockSpec((tm, tk), lambda i, j, k: (i, k))
    rspec = pl.BlockSpec((tn, tk), lambda i, j, k: (j, k)) if tb else pl.BlockSpec((tk, tn), lambda i, j, k: (k, j))
    especs = []
    for arr, kind in extras:
        if kind == "mn":
            especs.append(pl.BlockSpec((tm, tn), lambda i, j, k: (i, j)))
        elif kind == "n":
            especs.append(pl.BlockSpec((1, tn), lambda i, j, k: (0, j)))
        else:
            especs.append(pl.BlockSpec((tm, 1), lambda i, j, k: (i, 0)))
    res = _pcall(
        body, name=name,
        grid=(M // tm, N // tn, nk),
        in_specs=[lspec] * nl + [rspec] * nr + especs,
        out_specs=[pl.BlockSpec((tm, tn), lambda i, j, k: (i, j))] * no,
        out_shape=[jax.ShapeDtypeStruct((M, N), dt) for dt in outs],
        scratch_shapes=[pltpu.VMEM((tm, tn), F32)] * (n_acc if nk > 1 else 0),
        compiler_params=_params("parallel", "parallel", "arbitrary"),
    )(*lhs, *rhs, *[a for a, _ in extras])
    return res


def _rowwise(name, fn, ins, params, outs, accs=(), ts=256):
    S = ins[0].shape[0]
    ts = _tile(S, ts, 16)
    ni, npar, no, na = len(ins), len(params), len(outs), len(accs)

    def body(*refs):
        i_refs, p_refs = refs[:ni], refs[ni:ni + npar]
        o_refs = refs[ni + npar:ni + npar + no]
        a_refs = refs[ni + npar + no:]
        res_o, res_a = fn([r[...] for r in i_refs], [r[...] for r in p_refs])
        for ref, val in zip(o_refs, res_o):
            ref[...] = val.astype(ref.dtype)
        if na:
            @pl.when(pl.program_id(0) == 0)
            def _():
                for ref in a_refs:
                    ref[...] = jnp.zeros(ref.shape, F32)

            for ref, val in zip(a_refs, res_a):
                ref[...] += val

    res = _pcall(
        body, name=name,
        grid=(S // ts,),
        in_specs=[pl.BlockSpec((ts, a.shape[1]), lambda i: (i, 0)) for a in ins]
        + [pl.BlockSpec(p.shape, lambda i: (0, 0)) for p in params],
        out_specs=[pl.BlockSpec((ts, w), lambda i: (i, 0)) for w, _ in outs]
        + [pl.BlockSpec((1, w), lambda i: (0, 0)) for w in accs],
        out_shape=[jax.ShapeDtypeStruct((S, w), dt) for w, dt in outs]
        + [jax.ShapeDtypeStruct((1, w), F32) for w in accs],
        compiler_params=_params("arbitrary"),
    )(*ins, *params)
    return res


def _xhat(x):
    r = lax.rsqrt(jnp.mean(x * x, axis=-1, keepdims=True) + NORM_EPS)
    return x * r, r


def _rms_fwd(name, x, g):
    def fn(ins, ps):
        xh, _ = _xhat(ins[0])
        return [xh * ps[0]], []

    return _rowwise(name, fn, [x], [g], [(x.shape[1], BF16)])[0]


def _rms_bwd(name, dn, x, g, dres=None, copy_scale=None):
    w = x.shape[1]

    def fn(ins, ps):
        dn_, x_ = ins[0], ins[1]
        xh, r = _xhat(x_)
        dxh = dn_ * ps[0]
        dx = r * (dxh - xh * jnp.mean(dxh * xh, axis=-1, keepdims=True))
        if dres is not None:
            dx = dx + ins[2]
        o = [dx] + ([dx * copy_scale] if copy_scale is not None else [])
        return o, [jnp.sum(dn_ * xh, axis=0, keepdims=True)]

    ins = [dn, x] + ([dres] if dres is not None else [])
    outs = [(w, F32)] + ([(w, BF16)] if copy_scale is not None else [])
    res = _rowwise(name, fn, ins, [g], outs, accs=[w])
    return res[:-1], res[-1]


def _swiglu_epilogue(accs, ex):
    g, u = accs
    return [g, u, g * _sigmoid(g) * u]


def _dswiglu_epilogue(accs, ex):
    da = accs[0]
    g, u = ex
    sg = _sigmoid(g)
    return [da * u * (sg * (1.0 + g * (1.0 - sg))), da * (g * sg)]


def _ffn_fwd(tag, h, gnorm, wg, wu, wd):
    n = _rms_fwd(tag + "_norm", h, gnorm)
    g, u, a = _mm(tag + "_up", [n], [wg, wu], [F32, F32, BF16], pairs=((0, 0, 0), (0, 1, 1)),
                  epilogue=_swiglu_epilogue, tm=1024, tn=512)
    (hout,) = _mm(tag + "_down", [a], [wd], [F32], extras=[(h, "mn")],
                  epilogue=lambda accs, ex: [ex[0] + 0.5 * accs[0]], tm=1024, tn=1024, tk=512)
    return hout, (n, g, u, a)


def _ffn_bwd(tag, dh, dhb_half, h, gnorm, wg, wu, wd, saved, copy_scale):
    n, g, u, a = saved
    dg, du = _mm(tag + "_dact", [dhb_half], [wd], [BF16, BF16], tb=True, extras=[(g, "mn"), (u, "mn")],
                 epilogue=_dswiglu_epilogue, tm=1024, tn=512)
    (dwd,) = _mm(tag + "_dwd", [a], [dhb_half], [BF16], ta=True, tm=512, tn=2048, tk=1024)
    dwg, dwu = _mm(tag + "_dwgu", [n], [dg, du], [BF16, BF16], pairs=((0, 0, 0), (0, 1, 1)), ta=True,
                   tm=1024, tn=512, tk=1024)
    (dn,) = _mm(tag + "_dn", [dg, du], [wg, wu], [F32], pairs=((0, 0, 0), (1, 1, 0)), tb=True,
                tm=1024, tn=1024, tk=512)
    douts, dgn = _rms_bwd(tag + "_dnorm", dn, h, gnorm, dres=dh, copy_scale=copy_scale)
    return douts, dgn, dwg, dwu, dwd


def _attn_masks(b):
    qi = lax.broadcasted_iota(jnp.int32, (SWA_BLOCK, SWA_BLOCK), 0)
    ki = lax.broadcasted_iota(jnp.int32, (SWA_BLOCK, SWA_BLOCK), 1)
    return (ki >= qi) & (b > 0), ki <= qi


def _head_masks():
    lane = lax.broadcasted_iota(jnp.int32, (SWA_BLOCK, LANES), 1)
    return [lane < HEAD_DIM, lane >= HEAD_DIM]


def _per_head(t, first):
    sw = pltpu.roll(t, HEAD_DIM, 1)
    lo = lax.broadcasted_iota(jnp.int32, t.shape, 1) < HEAD_DIM
    return jnp.where(lo, t, sw) if first else jnp.where(lo, sw, t)


def _dot_nt(a, b):
    return lax.dot_general(a, b, (((1,), (1,)), ((), ())), preferred_element_type=F32)


def _dot_tn(a, b):
    return lax.dot_general(a, b, (((0,), (0,)), ((), ())), preferred_element_type=F32)


def _dot(a, b):
    return jnp.dot(a, b, preferred_element_type=F32)


def _keep(mask, t):
    return jnp.where(mask, t.astype(F32), 0.0).astype(BF16)


def _attn_views(S, A, d):
    hp_n = A // LANES

    def qkv(part, shift):
        def idx(r, hp, b):
            blk = jnp.clip(b + shift, 0, S // d // SWA_BLOCK - 1)
            return (blk, r * 3 * hp_n + part * hp_n + hp)
        return pl.BlockSpec((SWA_BLOCK, LANES), idx)

    def act(shift=0):
        def idx(r, hp, b):
            blk = jnp.clip(b + shift, 0, S // d // SWA_BLOCK - 1)
            return (blk, r * hp_n + hp)
        return pl.BlockSpec((SWA_BLOCK, LANES), idx)

    return qkv, act


def _attn_fwd_stage(name, qkv, d, prev, final):
    S, A3 = qkv.shape
    A = A3 // 3
    L = S // d
    scale = HEAD_DIM ** -0.5
    qv, av = _attn_views(S, A, d)
    has_prev = prev is not None

    def body(*refs):
        q_ref, kp_ref, kc_ref, vp_ref, vc_ref = refs[:5]
        p_refs = refs[5:8] if has_prev else ()
        o_refs = refs[5 + len(p_refs):]
        b = pl.program_id(2)
        vprev, vcur = _attn_masks(b)
        q, kp, kc, vp, vc = q_ref[...], kp_ref[...], kc_ref[...], vp_ref[...], vc_ref[...]
        hm = _head_masks()
        o = jnp.zeros((SWA_BLOCK, LANES), F32)
        m = jnp.zeros((SWA_BLOCK, LANES), F32)
        l = jnp.zeros((SWA_BLOCK, LANES), F32)
        for hh in range(2):
            qh = _keep(hm[hh], q)
            sp = jnp.where(vprev, _dot_nt(qh, kp) * scale, MASK_VALUE)
            sc = jnp.where(vcur, _dot_nt(qh, kc) * scale, MASK_VALUE)
            mh = jnp.maximum(jnp.max(sp, axis=-1, keepdims=True), jnp.max(sc, axis=-1, keepdims=True))
            pp, pc = jnp.exp(sp - mh), jnp.exp(sc - mh)
            lh = jnp.sum(pp, axis=-1, keepdims=True) + jnp.sum(pc, axis=-1, keepdims=True)
            vph = _keep(hm[hh], vp)
            vch = _keep(hm[hh], vc)
            o = o + _dot(pp.astype(BF16), vph) + _dot(pc.astype(BF16), vch)
            m = jnp.where(hm[hh], mh, m)
            l = jnp.where(hm[hh], lh, l)
        if has_prev:
            po, pm, pl_ = (r[...] for r in p_refs)
            mn = jnp.maximum(m, pm)
            w_new, w_old = jnp.exp(m - mn), jnp.exp(pm - mn)
            o = o * w_new + po * w_old
            l = l * w_new + pl_ * w_old
            m = mn
        if final:
            o_refs[0][...] = o / l
            o_refs[1][...] = m + jnp.log(l)
        else:
            o_refs[0][...] = o
            o_refs[1][...] = m
            o_refs[2][...] = l

    n_out = 2 if final else 3
    qk = qkv.reshape(L, d * A3)
    prev_v = [t.reshape(L, d * A) for t in prev] if has_prev else []
    res = _pcall(
        body, name=name,
        grid=(d, A // LANES, L // SWA_BLOCK),
        in_specs=[qv(0, 0), qv(1, -1), qv(1, 0), qv(2, -1), qv(2, 0)] + [av()] * len(prev_v),
        out_specs=[av()] * n_out,
        out_shape=[jax.ShapeDtypeStruct((L, d * A), F32)] * n_out,
        compiler_params=_params("parallel", "parallel", "arbitrary"),
    )(qk, qk, qk, qk, qk, *prev_v)
    return [t.reshape(S, A) for t in res]


def _attn_fwd(qkv):
    st = None
    for i, d in enumerate(DILATIONS):
        st = _attn_fwd_stage("attn_fwd_d%d" % d, qkv, d, st, final=(i == len(DILATIONS) - 1))
    return st


def _attn_dq_stage(name, qkv, do, lse, delta, d, prev):
    S, A3 = qkv.shape
    A = A3 // 3
    L = S // d
    scale = HEAD_DIM ** -0.5
    qv, av = _attn_views(S, A, d)
    has_prev = prev is not None

    def body(*refs):
        q_ref, kp_ref, kc_ref, vp_ref, vc_ref, do_ref, lse_ref, dl_ref = refs[:8]
        b = pl.program_id(2)
        vprev, vcur = _attn_masks(b)
        q, kp, kc, vp, vc = q_ref[...], kp_ref[...], kc_ref[...], vp_ref[...], vc_ref[...]
        do_, lse_, dl_ = do_ref[...], lse_ref[...], dl_ref[...]
        hm = _head_masks()
        dq = jnp.zeros((SWA_BLOCK, LANES), F32)
        for hh in range(2):
            qh = _keep(hm[hh], q)
            doh = _keep(hm[hh], do_)
            lh, dh = _per_head(lse_, hh == 0), _per_head(dl_, hh == 0)
            for k_, v_, valid in ((kp, vp, vprev), (kc, vc, vcur)):
                s = _dot_nt(qh, k_) * scale
                p = jnp.where(valid, jnp.exp(s - lh), 0.0)
                ds = p * (_dot_nt(doh, v_) - dh)
                dq = dq + _dot(ds.astype(BF16), _keep(hm[hh], k_))
        dq = dq * scale
        if has_prev:
            dq = dq + refs[8][...]
        refs[-1][...] = dq

    qk = qkv.reshape(L, d * A3)
    acts = [t.reshape(L, d * A) for t in (do, lse, delta)] + ([prev.reshape(L, d * A)] if has_prev else [])
    res = _pcall(
        body, name=name,
        grid=(d, A // LANES, L // SWA_BLOCK),
        in_specs=[qv(0, 0), qv(1, -1), qv(1, 0), qv(2, -1), qv(2, 0)] + [av()] * len(acts),
        out_specs=av(),
        out_shape=jax.ShapeDtypeStruct((L, d * A), F32),
        compiler_params=_params("parallel", "parallel", "arbitrary"),
    )(qk, qk, qk, qk, qk, *acts)
    return res.reshape(S, A)


def _attn_dkv_stage(name, qkv, do, lse, delta, d, prev):
    S, A3 = qkv.shape
    A = A3 // 3
    L = S // d
    nb = L // SWA_BLOCK
    scale = HEAD_DIM ** -0.5
    qv, av = _attn_views(S, A, d)
    has_prev = prev is not None

    def body(*refs):
        k_ref, v_ref, qc_ref, qn_ref, doc_ref, don_ref, lc_ref, ln_ref, dc_ref, dn_ref = refs[:10]
        j = pl.program_id(2)
        qi = lax.broadcasted_iota(jnp.int32, (SWA_BLOCK, SWA_BLOCK), 0)
        ki = lax.broadcasted_iota(jnp.int32, (SWA_BLOCK, SWA_BLOCK), 1)
        vcur = ki <= qi
        vnext = (ki >= qi) & (j < nb - 1)
        k_, v_ = k_ref[...], v_ref[...]
        hm = _head_masks()
        dk = jnp.zeros((SWA_BLOCK, LANES), F32)
        dv = jnp.zeros((SWA_BLOCK, LANES), F32)
        for hh in range(2):
            for q_r, do_r, l_r, d_r, valid in ((qc_ref, doc_ref, lc_ref, dc_ref, vcur),
                                               (qn_ref, don_ref, ln_ref, dn_ref, vnext)):
                q = q_r[...]
                qh = _keep(hm[hh], q)
                doh = _keep(hm[hh], do_r[...])
                lh, dh = _per_head(l_r[...], hh == 0), _per_head(d_r[...], hh == 0)
                s = _dot_nt(qh, k_) * scale
                p = jnp.where(valid, jnp.exp(s - lh), 0.0)
                dv = dv + _dot_tn(p.astype(BF16), doh)
                ds = p * (_dot_nt(doh, v_) - dh)
                dk = dk + _dot_tn(ds.astype(BF16), qh)
        dk = dk * scale
        if has_prev:
            dk = dk + refs[10][...]
            dv = dv + refs[11][...]
        refs[-2][...] = dk
        refs[-1][...] = dv

    qk = qkv.reshape(L, d * A3)
    acts = [t.reshape(L, d * A) for t in (do, lse, delta)]
    prev_v = [t.reshape(L, d * A) for t in prev] if has_prev else []
    res = _pcall(
        body, name=name,
        grid=(d, A // LANES, nb),
        in_specs=[qv(1, 0), qv(2, 0), qv(0, 0), qv(0, 1), av(), av(1), av(), av(1), av(), av(1)] + [av()] * len(prev_v),
        out_specs=[av(), av()],
        out_shape=[jax.ShapeDtypeStruct((L, d * A), F32)] * 2,
        compiler_params=_params("parallel", "parallel", "arbitrary"),
    )(qk, qk, qk, qk, acts[0], acts[0], acts[1], acts[1], acts[2], acts[2], *prev_v)
    return [t.reshape(S, A) for t in res]


def _attn_delta(dya, ya):
    S, A = ya.shape
    ri = lax.broadcasted_iota(jnp.int32, (A, A), 0) // HEAD_DIM
    ci = lax.broadcasted_iota(jnp.int32, (A, A), 1) // HEAD_DIM
    ones_bd = (ri == ci).astype(BF16)

    def fn(ins, ps):
        prod = ins[0] * ins[1]
        hi = prod.astype(BF16)
        lo = (prod - hi.astype(F32)).astype(BF16)
        return [_dot(hi, ps[0]) + _dot(lo, ps[0])], []

    return _rowwise("attn_delta", fn, [dya, ya], [ones_bd], [(A, F32)])[0]


def _attn_bwd(qkv, dya, ya, lse):
    delta = _attn_delta(dya, ya)
    dq, dkv = None, None
    for d in DILATIONS:
        dq = _attn_dq_stage("attn_dq_d%d" % d, qkv, dya, lse, delta, d, dq)
        dkv = _attn_dkv_stage("attn_dkv_d%d" % d, qkv, dya, lse, delta, d, dkv)
    return dq, dkv[0], dkv[1]


def _ssm_perm(a, T):
    S, w = a.shape
    return a.reshape(S // T, SUBLANES, T // SUBLANES, w).transpose(0, 2, 1, 3).reshape(S, w)


def _ssm_unperm(a, T):
    S, w = a.shape
    return a.reshape(S // T, T // SUBLANES, SUBLANES, w).transpose(0, 2, 1, 3).reshape(S, w)


def _ssm_powers(lam_ref, pw_ref, T, ns):
    n = (lax.broadcasted_iota(jnp.int32, (T, 1), 0) // SUBLANES + 1).astype(F32)
    mag = jnp.exp(n * lam_ref[0, 0:1, :])
    ang = n * lam_ref[0, 1:2, :]
    pw_ref[:, 0:ns] = mag * jnp.cos(ang)
    pw_ref[:, ns:2 * ns] = mag * jnp.sin(ang)


def _ssm_scan(xs, off, pw_ref, carry_ref, T, ns, reverse):
    Tc = T // SUBLANES
    sgn = -1.0 if reverse else 1.0
    ar, ai = pw_ref[0:SUBLANES, 0:ns], sgn * pw_ref[0:SUBLANES, ns:2 * ns]

    def rows(i):
        return pl.ds(pl.multiple_of(off + i * SUBLANES, SUBLANES), SUBLANES)

    def step(k, h):
        hr, hi = h
        r = rows(Tc - 1 - k if reverse else k)
        nr = ar * hr - ai * hi + xs[r, 0:ns]
        ni = ar * hi + ai * hr + xs[r, ns:2 * ns]
        xs[r, 0:ns] = nr
        xs[r, ns:2 * ns] = ni
        return nr, ni

    z = jnp.zeros((SUBLANES, ns), F32)
    er, ei = lax.fori_loop(0, Tc, step, (z, z))
    atr, ati = pw_ref[T - SUBLANES:T, 0:ns], sgn * pw_ref[T - SUBLANES:T, ns:2 * ns]
    rowid = lax.broadcasted_iota(jnp.int32, (SUBLANES, ns), 0)
    cr, ci = carry_ref[:, 0:ns], carry_ref[:, ns:2 * ns]
    ctr, cti = z, z
    for jj in range(SUBLANES):
        j = SUBLANES - 1 - jj if reverse else jj
        sel = rowid == j
        ctr, cti = jnp.where(sel, cr, ctr), jnp.where(sel, ci, cti)
        ejr = jnp.broadcast_to(jnp.sum(jnp.where(sel, er, 0.0), axis=0, keepdims=True), (SUBLANES, ns))
        eji = jnp.broadcast_to(jnp.sum(jnp.where(sel, ei, 0.0), axis=0, keepdims=True), (SUBLANES, ns))
        cr, ci = ejr + atr * cr - ati * ci, eji + atr * ci + ati * cr
    carry_ref[:, 0:ns] = cr
    carry_ref[:, ns:2 * ns] = ci

    def fix(i, _):
        r = rows(i)
        pr_rows = pl.ds(pl.multiple_of((Tc - 1 - i if reverse else i) * SUBLANES, SUBLANES), SUBLANES)
        pr, pi = pw_ref[pr_rows, 0:ns], sgn * pw_ref[pr_rows, ns:2 * ns]
        xs[r, 0:ns] += pr * ctr - pi * cti
        xs[r, ns:2 * ns] += pr * cti + pi * ctr
        return 0

    lax.fori_loop(0, Tc, fix, 0)
    return ctr, cti


def _ssm_fwd(ufp, bb, cc, lam_dt, drow, T):
    S, W = ufp.shape
    GB, cw, ns2 = bb.shape
    ns = ns2 // 2
    NCH = S // T

    def body(uf_ref, bb_ref, cc_ref, lam_ref, d_ref, y_ref, hs_ref, xs, pw, carry):
        @pl.when(pl.program_id(1) == 0)
        def _():
            _ssm_powers(lam_ref, pw, T, ns)
            carry[...] = jnp.zeros(carry.shape, F32)

        uf = uf_ref[...]
        xs[...] = _dot(uf.astype(BF16), bb_ref[0])
        hs_ref[0, 0] = carry[...]
        _ssm_scan(xs, 0, pw, carry, T, ns, reverse=False)
        y_ref[...] = _dot(xs[...].astype(BF16), cc_ref[0]) + d_ref[...] * uf

    return _pcall(
        body, name="ssm_fwd",
        grid=(GB, NCH),
        in_specs=[pl.BlockSpec((T, cw), lambda g, c: (c, g)),
                  pl.BlockSpec((1, cw, ns2), lambda g, c: (g, 0, 0)),
                  pl.BlockSpec((1, ns2, cw), lambda g, c: (g, 0, 0)),
                  pl.BlockSpec((1, 2, ns), lambda g, c: (g, 0, 0)),
                  pl.BlockSpec((1, cw), lambda g, c: (0, g))],
        out_specs=[pl.BlockSpec((T, cw), lambda g, c: (c, g)),
                   pl.BlockSpec((1, 1, SUBLANES, ns2), lambda g, c: (g, c, 0, 0))],
        out_shape=[jax.ShapeDtypeStruct((S, W), F32),
                   jax.ShapeDtypeStruct((GB, NCH, SUBLANES, ns2), F32)],
        scratch_shapes=[pltpu.VMEM((T, ns2), F32), pltpu.VMEM((T, ns2), F32), pltpu.VMEM((SUBLANES, ns2), F32)],
        compiler_params=_params("arbitrary", "arbitrary"),
    )(ufp, bb, cc, lam_dt, drow)


def _ssm_bwd(ufp, dyp, bb, bbt, cc, cct, lam_dt, drow, hstart, T):
    S, W = ufp.shape
    GB, cw, ns2 = bb.shape
    ns = ns2 // 2
    NCH = S // T

    def body(uf_ref, dy_ref, bb_ref, bbt_ref, cc_ref, cct_ref, lam_ref, d_ref, hs_ref,
             duf_ref, dbb_ref, dcc_ref, da_ref, dd_ref, hb, ls, pw, carry_f, carry_b):
        @pl.when(pl.program_id(1) == 0)
        def _():
            _ssm_powers(lam_ref, pw, T, ns)
            carry_b[...] = jnp.zeros(carry_b.shape, F32)
            dbb_ref[...] = jnp.zeros(dbb_ref.shape, F32)
            dcc_ref[...] = jnp.zeros(dcc_ref.shape, F32)
            da_ref[...] = jnp.zeros(da_ref.shape, F32)
            dd_ref[...] = jnp.zeros(dd_ref.shape, F32)

        uf, dy = uf_ref[...], dy_ref[...]
        ufb, dyb = uf.astype(BF16), dy.astype(BF16)
        hb[SUBLANES:T + SUBLANES, :] = _dot(ufb, bb_ref[0])
        carry_f[...] = hs_ref[0, 0]
        ctr, cti = _ssm_scan(hb, SUBLANES, pw, carry_f, T, ns, reverse=False)
        hb[0:SUBLANES, 0:ns] = ctr
        hb[0:SUBLANES, ns:ns2] = cti
        ls[...] = _dot(dyb, cct_ref[0])
        _ssm_scan(ls, 0, pw, carry_b, T, ns, reverse=True)
        lv = ls[...]
        lb = lv.astype(BF16)
        dbb_ref[0] += _dot_tn(ufb, lb)
        dcc_ref[0] += _dot_tn(hb[SUBLANES:T + SUBLANES, :].astype(BF16), dyb)
        lr, li = lv[:, 0:ns], lv[:, ns:ns2]
        hpr, hpi = hb[0:T, 0:ns], hb[0:T, ns:ns2]
        dar = jnp.sum(lr * hpr + li * hpi, axis=0, keepdims=True)
        dai = jnp.sum(li * hpr - lr * hpi, axis=0, keepdims=True)
        da_ref[0, :, 0:ns] += jnp.broadcast_to(dar, (SUBLANES, ns))
        da_ref[0, :, ns:ns2] += jnp.broadcast_to(dai, (SUBLANES, ns))
        duf_ref[...] = _dot(lb, bbt_ref[0]) + d_ref[...] * dy
        dd_ref[...] += jnp.sum(dy * uf, axis=0, keepdims=True)

    rc = lambda c: NCH - 1 - c
    return _pcall(
        body, name="ssm_bwd",
        grid=(GB, NCH),
        in_specs=[pl.BlockSpec((T, cw), lambda g, c: (rc(c), g)),
                  pl.BlockSpec((T, cw), lambda g, c: (rc(c), g)),
                  pl.BlockSpec((1, cw, ns2), lambda g, c: (g, 0, 0)),
                  pl.BlockSpec((1, ns2, cw), lambda g, c: (g, 0, 0)),
                  pl.BlockSpec((1, ns2, cw), lambda g, c: (g, 0, 0)),
                  pl.BlockSpec((1, cw, ns2), lambda g, c: (g, 0, 0)),
                  pl.BlockSpec((1, 2, ns), lambda g, c: (g, 0, 0)),
                  pl.BlockSpec((1, cw), lambda g, c: (0, g)),
                  pl.BlockSpec((1, 1, SUBLANES, ns2), lambda g, c: (g, rc(c), 0, 0))],
        out_specs=[pl.BlockSpec((T, cw), lambda g, c: (rc(c), g)),
                   pl.BlockSpec((1, cw, ns2), lambda g, c: (g, 0, 0)),
                   pl.BlockSpec((1, ns2, cw), lambda g, c: (g, 0, 0)),
                   pl.BlockSpec((1, SUBLANES, ns2), lambda g, c: (g, 0, 0)),
                   pl.BlockSpec((1, cw), lambda g, c: (0, g))],
        out_shape=[jax.ShapeDtypeStruct((S, W), F32),
                   jax.ShapeDtypeStruct((GB, cw, ns2), F32),
                   jax.ShapeDtypeStruct((GB, ns2, cw), F32),
                   jax.ShapeDtypeStruct((GB, SUBLANES, ns2), F32),
                   jax.ShapeDtypeStruct((1, W), F32)],
        scratch_shapes=[pltpu.VMEM((T + SUBLANES, ns2), F32), pltpu.VMEM((T, ns2), F32), pltpu.VMEM((T, ns2), F32),
                        pltpu.VMEM((SUBLANES, ns2), F32), pltpu.VMEM((SUBLANES, ns2), F32)],
        compiler_params=_params("arbitrary", "arbitrary"),
    )(ufp, dyp, bb, bbt, cc, cct, lam_dt, drow, hstart)


def _ssm_disc_math(lr, li, logdt, br, bi):
    dt = jnp.exp(logdt)
    mag = jnp.exp(lr * dt)
    ar = mag * jnp.cos(li * dt)
    ai = mag * jnp.sin(li * dt)
    nr, ni = ar - 1.0, ai
    den = lr * lr + li * li
    cr = (nr * lr + ni * li) / den
    ci = (ni * lr - nr * li) / den
    return ar, ai, cr * br - ci * bi, cr * bi + ci * br


def _ssm_disc(lr, li, logdt, br, bi):
    C = br.shape[1]

    def fn(ins, ps):
        _, _, bbr, bbi = _ssm_disc_math(*ins)
        dt = jnp.exp(ins[2])
        return [ins[0] * dt, ins[1] * dt, bbr, bbi], []

    return _rowwise("ssm_disc", fn, [lr, li, logdt, br, bi], [], [(1, F32), (1, F32), (C, F32), (C, F32)], ts=512)


def _ssm_disc_bwd(lr, li, logdt, br, bi, dar, dai, dbbr, dbbi):
    C = br.shape[1]

    def fn(ins, ps):
        _, vjp = jax.vjp(_ssm_disc_math, *ins[:5])
        return list(vjp(tuple(ins[5:]))), []

    return _rowwise("ssm_disc_bwd", fn, [lr, li, logdt, br, bi, dar, dai, dbbr, dbbi], [],
                    [(1, F32), (1, F32), (1, F32), (C, F32), (C, F32)], ts=512)


def _block_diag(t):
    GB, g, a, b = t.shape
    eye = jnp.eye(g, dtype=t.dtype)
    return (t[:, :, :, None, :] * eye[None, :, None, :, None]).reshape(GB, g * a, g * b)


def _block_diag_take(t, g):
    GB, ga, gb_ = t.shape
    a, b = ga // g, gb_ // g
    eye = jnp.eye(g, dtype=t.dtype)
    return (t.reshape(GB, g, a, g, b) * eye[None, :, None, :, None]).sum(axis=3)


def _loss_head(h4, tgt, gf):
    D = h4.shape[1]

    def fn(ins, ps):
        x, t = ins
        xh, r = _xhat(x)
        err = xh * ps[0] - t
        dn = err * (1.0 / D)
        dxh = dn * ps[0]
        dx = r * (dxh - xh * jnp.mean(dxh * xh, axis=-1, keepdims=True))
        return [dx], [jnp.sum(err * err, axis=0, keepdims=True), jnp.sum(dn * xh, axis=0, keepdims=True)]

    return _rowwise("loss_head", fn, [h4, tgt], [gf], [(D, F32)], accs=[D, D])


def _gelu(x):
    return 0.5 * x * (1.0 + jnp.tanh(GELU_C * (x + GELU_K * x * x * x)))


def _gelu_grad(x):
    t = jnp.tanh(GELU_C * (x + GELU_K * x * x * x))
    return 0.5 * (1.0 + t) + 0.5 * x * (1.0 - t * t) * GELU_C * (1.0 + 3.0 * GELU_K * x * x)


def _mesh_pos():
    return lax.axis_index("x"), lax.axis_index("y"), lax.axis_index("c")


def _other_chips(x, y):
    return [(1 - x, y), (x, 1 - y), (1 - x, 1 - y)]


def _remote(src, dst, send, recv, dev):
    return pltpu.make_async_remote_copy(src_ref=src, dst_ref=dst, send_sem=send, recv_sem=recv,
                                        device_id=dev, device_id_type=MESH)


ANY = pl.BlockSpec(memory_space=pl.ANY)


def _shard_region(ref, axis, shape, s, half=None):
    R, Cc = shape
    hr = R // 2
    if axis == 1:
        rows = pl.ds(0, R) if half is None else pl.ds(pl.multiple_of(half * hr, 16), hr)
        return ref.at[rows, pl.ds(pl.multiple_of(s * Cc, LANES), Cc)]
    if half is None:
        return ref.at[pl.ds(pl.multiple_of(s * R, 16), R), :]
    return ref.at[pl.ds(pl.multiple_of(s * R + half * hr, 16), hr), :]


def _all_gather_weights(shards, axes):
    nw = len(shards)
    shapes = [s.shape for s in shards]
    full = [((R, N_CHIPS * Cc) if ax == 1 else (N_CHIPS * R, Cc)) for (R, Cc), ax in zip(shapes, axes)]

    def body(*refs):
        sh, out = refs[:nw], refs[nw:2 * nw]
        send, recv, loc = refs[2 * nw:]
        x, y, c = _mesh_pos()
        me, sib = (x, y, c), (x, y, 1 - c)
        chips = _other_chips(x, y)
        sid = [2 * cx + cy for cx, cy in chips]
        my_s = 2 * x + y

        def reg(w, s, half=None):
            return _shard_region(out[w], axes[w], shapes[w], s, half)

        local = [pltpu.make_async_copy(sh[w], reg(w, my_s), loc.at[w]) for w in range(nw)]
        for cp in local:
            cp.start()
        sends = []
        for w in range(nw):
            hr = shapes[w][0] // 2
            src = sh[w].at[pl.ds(pl.multiple_of(c * hr, 16), hr), :]
            for j, chip in enumerate(chips):
                cp = _remote(src, reg(w, my_s, c), send.at[6 * w + j], recv.at[6 * w + j], (*chip, c))
                cp.start()
                sends.append(cp)
        for w in range(nw):
            for j in range(3):
                r_ = reg(w, sid[j], c)
                _remote(r_, r_, send.at[6 * w + j], recv.at[6 * w + j], me).wait_recv()
                cp = _remote(r_, r_, send.at[6 * w + 3 + j], recv.at[6 * w + 3 + j], sib)
                cp.start()
                sends.append(cp)
        for w in range(nw):
            for j in range(3):
                r_ = reg(w, sid[j], 1 - c)
                _remote(r_, r_, send.at[6 * w + 3 + j], recv.at[6 * w + 3 + j], me).wait_recv()
        for cp in sends:
            cp.wait_send()
        for cp in local:
            cp.wait()

    return _pcall(
        body, name="ag_weights",
        in_specs=[ANY] * nw, out_specs=[ANY] * nw,
        out_shape=[jax.ShapeDtypeStruct(f, BF16) for f in full],
        scratch_shapes=[pltpu.SemaphoreType.DMA((6 * nw,)), pltpu.SemaphoreType.DMA((6 * nw,)),
                        pltpu.SemaphoreType.DMA((nw,))],
    )(*shards)


def _push_pair(name, src, tr, nblk, src_block, out_rows, dst_block, local):
    cw = src.shape[1]
    c_arr = lax.axis_index("c").astype(jnp.int32).reshape(1)

    def body(c_ref, src_ref, out_ref, send, recv, lsem):
        i = pl.program_id(0)
        x, y, c = _mesh_pos()
        dst = out_ref.at[pl.ds(pl.multiple_of(dst_block(i, c_ref[0]) * tr, 16), tr), :]
        cp = _remote(src_ref, dst, send, recv, (x, y, 1 - c))
        cp.start()
        if local:
            lc = pltpu.make_async_copy(src_ref, dst, lsem)
            lc.start()
            lc.wait()
        cp.wait_send()

        @pl.when(i == nblk - 1)
        def _():
            got = out_ref.at[pl.ds(0, nblk * tr), :]
            _remote(got, got, send, recv, (x, y, c)).wait_recv()

    return _pcall(
        body, name=name,
        grid_spec=pltpu.PrefetchScalarGridSpec(
            num_scalar_prefetch=1, grid=(nblk,),
            in_specs=[pl.BlockSpec((tr, cw), lambda i, c_ref: (src_block(i, c_ref[0]), 0))],
            out_specs=ANY,
            scratch_shapes=[pltpu.SemaphoreType.DMA, pltpu.SemaphoreType.DMA, pltpu.SemaphoreType.DMA]),
        out_shape=jax.ShapeDtypeStruct((out_rows, cw), src.dtype),
        compiler_params=_params("arbitrary"),
    )(c_arr, src)


def _sum_half(name, g, theirs, tr, nblk, src_block):
    cw = g.shape[1]
    c_arr = lax.axis_index("c").astype(jnp.int32).reshape(1)

    def body(c_ref, g_ref, t_ref, o_ref):
        o_ref[...] = (g_ref[...].astype(F32) + t_ref[...].astype(F32)).astype(BF16)

    return _pcall(
        body, name=name,
        grid_spec=pltpu.PrefetchScalarGridSpec(
            num_scalar_prefetch=1, grid=(nblk,),
            in_specs=[pl.BlockSpec((tr, cw), lambda i, c_ref: (src_block(i, c_ref[0]), 0)),
                      pl.BlockSpec((tr, cw), lambda i, c_ref: (i, 0))],
            out_specs=pl.BlockSpec((tr, cw), lambda i, c_ref: (i, 0))),
        out_shape=jax.ShapeDtypeStruct((nblk * tr, cw), BF16),
        compiler_params=_params("arbitrary"),
    )(c_arr, g, theirs)


def _rs_chips(qs, axes):
    nw = len(qs)
    pshapes = [((q.shape[0], q.shape[1] // N_CHIPS) if ax == 1 else q.shape[1:]) for q, ax in zip(qs, axes)]

    def body(*refs):
        q, own, got = refs[:nw], refs[nw:2 * nw], refs[2 * nw:3 * nw]
        send, recv, loc = refs[3 * nw:]
        x, y, c = _mesh_pos()
        chips = _other_chips(x, y)

        def part(w, s):
            if axes[w] == 1:
                cw = pshapes[w][1]
                return q[w].at[:, pl.ds(pl.multiple_of(s * cw, LANES), cw)]
            return q[w].at[s]

        cps = []
        for w in range(nw):
            lc = pltpu.make_async_copy(part(w, 2 * x + y), own[w], loc.at[w])
            lc.start()
            cps.append(lc)
            for j, (cx, cy) in enumerate(chips):
                rc = _remote(part(w, 2 * cx + cy), got[w].at[j], send.at[3 * w + j], recv.at[3 * w + j], (cx, cy, c))
                rc.start()
                cps.append(rc)
        for cp in cps:
            cp.wait()

    res = _pcall(
        body, name="rs_chips",
        in_specs=[ANY] * nw, out_specs=[ANY] * (2 * nw),
        out_shape=[jax.ShapeDtypeStruct(p, BF16) for p in pshapes]
        + [jax.ShapeDtypeStruct((3,) + tuple(p), BF16) for p in pshapes],
        scratch_shapes=[pltpu.SemaphoreType.DMA((3 * nw,)), pltpu.SemaphoreType.DMA((3 * nw,)), pltpu.SemaphoreType.DMA((nw,))],
    )(*qs)
    return res[:nw], res[nw:]


def _sum_tiles(name, arrs, out_dtype):
    R, Cc = arrs[0].shape
    tr = _tile(R, max(16, (1 << 20) // Cc // 16 * 16), 16)

    def body(*refs):
        acc = refs[0][...].astype(F32)
        for r in refs[1:-1]:
            acc = acc + r[...].astype(F32)
        refs[-1][...] = acc.astype(out_dtype)

    return _pcall(
        body, name=name, grid=(R // tr,),
        in_specs=[pl.BlockSpec((tr, Cc), lambda i: (i, 0))] * len(arrs),
        out_specs=pl.BlockSpec((tr, Cc), lambda i: (i, 0)),
        out_shape=jax.ShapeDtypeStruct((R, Cc), out_dtype),
        compiler_params=_params("parallel"),
    )(*arrs)


def _reduce_scatter(grads, axes):
    block_bytes = 3 << 19
    qs, hrs = [], []
    for w, (g, ax) in enumerate(zip(grads, axes)):
        rows, cw = g.shape
        hr = rows // 2 if ax == 1 else rows // N_CHIPS // 2
        tr = _tile(hr, max(16, block_bytes // (cw * 2) // 16 * 16), 16)
        nth = hr // tr
        if ax == 1:
            nblk, blk = nth, (lambda i, half, nth=nth: half * nth + i)
        else:
            nblk, blk = N_CHIPS * nth, (lambda i, half, nth=nth: (i // nth) * (2 * nth) + half * nth + i % nth)
        theirs = _push_pair("rs_pair%d" % w, g, tr, nblk, lambda i, c, blk=blk: blk(i, 1 - c), nblk * tr,
                            lambda i, c: i, local=False)
        q = _sum_half("rs_sum_pair%d" % w, g, theirs, tr, nblk, blk)
        qs.append(q if ax == 1 else q.reshape(N_CHIPS, hr, cw))
        hrs.append(hr)
    own, got = _rs_chips(qs, axes)
    outs = []
    for w, (o, g_, hr) in enumerate(zip(own, got, hrs)):
        half = _sum_tiles("rs_sum_chips%d" % w, [o, g_[0], g_[1], g_[2]], F32)
        tr = _tile(hr, max(16, block_bytes // (half.shape[1] * 4) // 16 * 16), 16)
        nth = hr // tr
        outs.append(_push_pair("rs_swap%d" % w, half, tr, nth, lambda i, c: i, 2 * hr,
                               lambda i, c, nth=nth: c * nth + i, local=True))
    return outs


def _all_reduce_small(v):
    n = v.shape[0]

    def body(v_ref, out_ref, buf, send, recv):
        x, y, c = _mesh_pos()
        my = 4 * x + 2 * y + c
        buf[my] = v_ref[...]
        cps = []
        for k in range(1, N_DEV):
            fx, fy, fc = (k >> 2) & 1, (k >> 1) & 1, k & 1
            peer = (1 - x if fx else x, 1 - y if fy else y, 1 - c if fc else c)
            cp = _remote(v_ref, buf.at[my], send.at[k - 1], recv.at[k - 1], peer)
            cp.start()
            cps.append((cp, 4 * peer[0] + 2 * peer[1] + peer[2]))
        for k, (cp, pid) in enumerate(cps):
            _remote(v_ref, buf.at[pid], send.at[k], recv.at[k], (x, y, c)).wait_recv()
        acc = buf[0]
        for i in range(1, N_DEV):
            acc = acc + buf[i]
        out_ref[...] = acc
        for cp, _ in cps:
            cp.wait_send()

    return _pcall(
        body, name="ar_small",
        in_specs=[pl.BlockSpec(memory_space=pltpu.VMEM)], out_specs=pl.BlockSpec(memory_space=pltpu.VMEM),
        out_shape=jax.ShapeDtypeStruct((n, LANES), F32),
        scratch_shapes=[pltpu.VMEM((N_DEV, n, LANES), F32), pltpu.SemaphoreType.DMA((N_DEV - 1,)),
                        pltpu.SemaphoreType.DMA((N_DEV - 1,))],
        compiler_params=pltpu.CompilerParams(vmem_limit_bytes=V7X_VMEM_LIMIT_BYTES),
    )(v)


def _adamw(name, w, g, m, v):
    R, Cc = w.shape
    tr = _tile(R, max(8, (1 << 19) // Cc // 8 * 8), 8)
    c1 = 1.0 - ADAM_B1 ** ADAM_STEP
    c2 = 1.0 - ADAM_B2 ** ADAM_STEP

    def body(w_ref, g_ref, m_ref, v_ref, d_ref, nm_ref, nv_ref):
        g_ = g_ref[...]
        nm = ADAM_B1 * m_ref[...] + (1.0 - ADAM_B1) * g_
        nv = ADAM_B2 * v_ref[...] + (1.0 - ADAM_B2) * (g_ * g_)
        d_ref[...] = -ADAM_LR * ((nm / c1) / (jnp.sqrt(nv / c2) + ADAM_EPS) + ADAM_WD * w_ref[...])
        nm_ref[...] = nm
        nv_ref[...] = nv

    spec = pl.BlockSpec((tr, Cc), lambda i: (i, 0))
    return _pcall(
        body, name=name, grid=(R // tr,),
        in_specs=[spec] * 4, out_specs=[spec] * 3,
        out_shape=[jax.ShapeDtypeStruct((R, Cc), F32)] * 3,
        compiler_params=_params("parallel"),
    )(w, g, m, v)


def _pack(arrs, rows):
    flat = jnp.concatenate([a.reshape(-1) for a in arrs])
    return jnp.pad(flat, (0, rows * LANES - flat.shape[0])).reshape(rows, LANES)


def _unpack(packed, like):
    flat, out, o = packed.reshape(-1), [], 0
    for a in like:
        out.append(flat[o:o + a.size].reshape(a.shape))
        o += a.size
    return out


BIG = (
    ("ffn1_w_gate", 1), ("ffn1_w_up", 1), ("ffn1_w_down", 0), ("w_in", 1), ("ssm_w_glu", 0), ("w_out", 0),
    ("ffn2_w_gate", 1), ("ffn2_w_up", 1), ("ffn2_w_down", 0), ("ple_w_gate", 0), ("ple_w_proj", 1),
)
SMALL = ("ffn1_norm", "mix_norm", "attn_out_norm", "ssm_lambda_re", "ssm_lambda_im", "ssm_log_dt", "ssm_b_re", "ssm_b_im",
         "ssm_c_re", "ssm_c_im", "ssm_d", "ssm_b_glu", "ssm_out_norm", "ffn2_norm", "ple_norm", "final_norm")
WEIGHTS = ("ffn1_norm", "ffn1_w_gate", "ffn1_w_up", "ffn1_w_down", "mix_norm", "w_in", "attn_out_norm", "ssm_lambda_re",
           "ssm_lambda_im", "ssm_log_dt", "ssm_b_re", "ssm_b_im", "ssm_c_re", "ssm_c_im", "ssm_d", "ssm_w_glu", "ssm_b_glu",
           "ssm_out_norm", "w_out", "ffn2_norm", "ffn2_w_gate", "ffn2_w_up", "ffn2_w_down", "ple_norm", "ple_w_gate",
           "ple_w_proj", "final_norm")


def _pad_to(a, axis, n):
    pad = [(0, 0), (0, 0)]
    pad[axis] = (0, n - a.shape[axis])
    return jnp.pad(a, pad)


def _local_step(x, p, tgt, w, full):
    S, D = x.shape
    A = w["attn_out_norm"].shape[-1]
    W = w["ssm_d"].shape[-1]
    G, P = w["ssm_lambda_re"].shape[-2:]
    C = w["ssm_b_re"].shape[-1]
    GB = G // SSM_BLOCK_GROUPS
    T = min(1024, S)
    row = lambda name: w[name].reshape(1, -1)
    gs = {}

    h1, ffn1_saved = _ffn_fwd("ffn1", x, row("ffn1_norm"), full["ffn1_w_gate"], full["ffn1_w_up"], full["ffn1_w_down"])
    n2 = _rms_fwd("mix_norm", h1, row("mix_norm"))
    w_in = full["w_in"]
    (qkv,) = _mm("w_in_qkv", [n2], [w_in[:, :3 * A]], [BF16], tm=1024, tn=1024)
    (s_in,) = _mm("w_in_ssm", [n2], [w_in[:, 3 * A:]], [F32], tm=1024, tn=1024)
    ya, lse = _attn_fwd(qkv)

    col = lambda name: w[name].reshape(G * P, 1)
    logdt_x = jnp.repeat(w["ssm_log_dt"].reshape(G), P).reshape(G * P, 1)
    b_re, b_im = w["ssm_b_re"].reshape(G * P, C), w["ssm_b_im"].reshape(G * P, C)
    lrdt, lidt, bbr, bbi = _ssm_disc(col("ssm_lambda_re"), col("ssm_lambda_im"), logdt_x, b_re, b_im)
    gsz = SSM_BLOCK_GROUPS
    to_bb = lambda t: _block_diag(t.reshape(GB, gsz, P, C).transpose(0, 1, 3, 2))
    bb = jnp.concatenate([to_bb(bbr), to_bb(bbi)], axis=2).astype(BF16)
    to_cc = lambda t: _block_diag(t.reshape(GB, gsz, C, P).transpose(0, 1, 3, 2))
    cc = jnp.concatenate([to_cc(w["ssm_c_re"]), -to_cc(w["ssm_c_im"])], axis=1).astype(BF16)
    lam_dt = jnp.stack([lrdt.reshape(GB, gsz * P), lidt.reshape(GB, gsz * P)], axis=1)
    ufp = _ssm_perm(s_in, T)
    ypre, hstart = _ssm_fwd(ufp, bb, cc, lam_dt, row("ssm_d"), T)

    def glu_in(ins, ps):
        yg = _gelu(ins[0])
        return [yg, yg], []

    yg, ygb = _rowwise("ssm_gelu", glu_in, [ypre], [], [(W, F32), (W, BF16)])
    w_glu = full["ssm_w_glu"]

    def glu_out(accs, ex):
        gl = accs[0] + ex[1]
        return [ex[0] * _sigmoid(gl), gl]

    ybp, gl = _mm("ssm_glu", [ygb], [w_glu], [F32, F32], extras=[(yg, "mn"), (row("ssm_b_glu"), "n")],
                  epilogue=glu_out, tm=1024, tn=1024)
    yb = _ssm_unperm(ybp, T)
    na = _rms_fwd("attn_out_norm", ya, row("attn_out_norm"))
    nb = _rms_fwd("ssm_out_norm", yb, row("ssm_out_norm"))
    w_out = full["w_out"]
    (h2,) = _mm("w_out", [na, nb], [w_out[:A], w_out[A:]], [F32], pairs=((0, 0, 0), (1, 1, 0)), extras=[(h1, "mn")],
                epilogue=lambda accs, ex: [ex[0] + accs[0]], tm=1024, tn=1024)
    h3, ffn2_saved = _ffn_fwd("ffn2", h2, row("ffn2_norm"), full["ffn2_w_gate"], full["ffn2_w_up"], full["ffn2_w_down"])
    n4 = _rms_fwd("ple_norm", h3, row("ple_norm"))
    (pe,) = _mm("ple_proj", [p], [full["ple_w_proj"]], [F32], tm=1024, tn=1024)

    def ple_out(accs, ex):
        gate = _sigmoid(accs[0])
        return [ex[1] + gate * ex[0], gate]

    h4, gate = _mm("ple_gate", [n4], [full["ple_w_gate"]], [F32, F32], extras=[(pe, "mn"), (h3, "mn")],
                   epilogue=ple_out, tm=1024, tn=1024)

    dh4, err2, gs["final_norm"] = _loss_head(h4, tgt, row("final_norm"))
    loss = (0.5 / D) * jnp.sum(err2)

    def ple_bwd(ins, ps):
        dh, gt, pe_ = ins
        return [dh * gt, dh * pe_ * gt * (1.0 - gt)], []

    dpe, dpg = _rowwise("ple_bwd", ple_bwd, [dh4, gate, pe], [], [(D, BF16), (D, BF16)])
    (d_ple_proj,) = _mm("ple_dproj", [p], [dpe], [BF16], ta=True, tm=256, tn=2048, tk=1024)
    (d_ple_gate,) = _mm("ple_dgate", [n4], [dpg], [BF16], ta=True, tm=1024, tn=1024, tk=1024)
    (dn4,) = _mm("ple_dn", [dpg], [full["ple_w_gate"]], [F32], tb=True, tm=1024, tn=1024)
    (dh3, dh3b), gs["ple_norm"] = _rms_bwd("ple_dnorm", dn4, h3, row("ple_norm"), dres=dh4, copy_scale=0.5)
    (dh2, dh2b), gs["ffn2_norm"], d_ffn2_g, d_ffn2_u, d_ffn2_d = _ffn_bwd(
        "ffn2", dh3, dh3b, h2, row("ffn2_norm"), full["ffn2_w_gate"], full["ffn2_w_up"], full["ffn2_w_down"],
        ffn2_saved, copy_scale=1.0)
    (dna,) = _mm("w_out_dna", [dh2b], [w_out[:A]], [F32], tb=True, tm=1024, tn=1024)
    (dnb,) = _mm("w_out_dnb", [dh2b], [w_out[A:]], [F32], tb=True, tm=1024, tn=1024)
    (d_wout_a,) = _mm("w_out_dwa", [na], [dh2b], [BF16], ta=True, tm=1024, tn=1024, tk=1024)
    (d_wout_b,) = _mm("w_out_dwb", [nb], [dh2b], [BF16], ta=True, tm=1024, tn=1024, tk=1024)
    d_w_out = jnp.concatenate([d_wout_a, d_wout_b], axis=0)
    (dya,), gs["attn_out_norm"] = _rms_bwd("attn_out_dnorm", dna, ya, row("attn_out_norm"))
    (dyb,), gs["ssm_out_norm"] = _rms_bwd("ssm_out_dnorm", dnb, yb, row("ssm_out_norm"))

    dybp = _ssm_perm(dyb, T)

    def glu_bwd(ins, ps):
        dy, yg_, gl_ = ins
        sg = _sigmoid(gl_)
        dgl = dy * yg_ * sg * (1.0 - sg)
        return [dgl, dy * sg], [jnp.sum(dgl, axis=0, keepdims=True)]

    dgl, dyg_direct, gs["ssm_b_glu"] = _rowwise("ssm_glu_bwd", glu_bwd, [dybp, yg, gl], [], [(W, BF16), (W, F32)], accs=[W])
    (d_w_glu,) = _mm("ssm_dwglu", [ygb], [dgl], [BF16], ta=True, tm=1024, tn=1024, tk=1024)
    (dypre,) = _mm("ssm_dyg", [dgl], [w_glu], [F32], tb=True, extras=[(dyg_direct, "mn"), (ypre, "mn")],
                   epilogue=lambda accs, ex: [(accs[0] + ex[0]) * _gelu_grad(ex[1])], tm=1024, tn=1024)
    dufp, dbb, dcc, da, gs["ssm_d"] = _ssm_bwd(ufp, dypre, bb, bb.transpose(0, 2, 1), cc, cc.transpose(0, 2, 1),
                                               lam_dt, row("ssm_d"), hstart, T)
    ns = gsz * P
    from_bb = lambda t: _block_diag_take(t, gsz).transpose(0, 1, 3, 2).reshape(G * P, C)
    from_cc = lambda t: _block_diag_take(t, gsz).transpose(0, 1, 3, 2).reshape(w["ssm_c_re"].shape)
    gs["ssm_c_re"], gs["ssm_c_im"] = from_cc(dcc[:, :ns]), -from_cc(dcc[:, ns:])
    dar, dai = da[:, 0, :ns].reshape(G * P, 1), da[:, 0, ns:].reshape(G * P, 1)
    dlr, dli, dlogdt, dbr, dbi = _ssm_disc_bwd(col("ssm_lambda_re"), col("ssm_lambda_im"), logdt_x, b_re, b_im,
                                               dar, dai, from_bb(dbb[:, :, :ns]), from_bb(dbb[:, :, ns:]))
    gs["ssm_lambda_re"], gs["ssm_lambda_im"] = dlr.reshape(w["ssm_lambda_re"].shape), dli.reshape(w["ssm_lambda_im"].shape)
    gs["ssm_log_dt"] = dlogdt.reshape(G, P).sum(axis=1).reshape(w["ssm_log_dt"].shape)
    gs["ssm_b_re"], gs["ssm_b_im"] = dbr.reshape(w["ssm_b_re"].shape), dbi.reshape(w["ssm_b_im"].shape)
    ds_in = _ssm_unperm(dufp, T)

    dq, dk, dv = _attn_bwd(qkv, dya, ya, lse)
    dz = jnp.concatenate([dq, dk, dv, ds_in], axis=1).astype(BF16)
    (d_w_in,) = _mm("w_in_dw", [n2], [dz], [BF16], ta=True, tm=1024, tn=1024, tk=1024)
    (dn2,) = _mm("w_in_dn", [dz], [w_in], [F32], tb=True, tm=1024, tn=1024)
    (dh1, dh1b), gs["mix_norm"] = _rms_bwd("mix_dnorm", dn2, h1, row("mix_norm"), dres=dh2, copy_scale=0.5)
    (dx,), gs["ffn1_norm"], d_ffn1_g, d_ffn1_u, d_ffn1_d = _ffn_bwd(
        "ffn1", dh1, dh1b, x, row("ffn1_norm"), full["ffn1_w_gate"], full["ffn1_w_up"], full["ffn1_w_down"],
        ffn1_saved, copy_scale=None)

    big = {"ffn1_w_gate": d_ffn1_g, "ffn1_w_up": d_ffn1_u, "ffn1_w_down": d_ffn1_d, "w_in": d_w_in, "ssm_w_glu": d_w_glu,
           "w_out": d_w_out, "ffn2_w_gate": d_ffn2_g, "ffn2_w_up": d_ffn2_u, "ffn2_w_down": d_ffn2_d,
           "ple_w_gate": d_ple_gate, "ple_w_proj": d_ple_proj}
    small = {k: gs[k].reshape(w[k].shape) for k in SMALL}
    return loss, dx, big, small


def kernel(x, p, ffn1_norm, ffn1_w_gate, ffn1_w_up, ffn1_w_down, mix_norm, w_in, attn_out_norm, ssm_lambda_re, ssm_lambda_im, ssm_log_dt, ssm_b_re, ssm_b_im, ssm_c_re, ssm_c_im, ssm_d, ssm_w_glu, ssm_b_glu, ssm_out_norm, w_out, ffn2_norm, ffn2_w_gate, ffn2_w_up, ffn2_w_down, ple_norm, ple_w_gate, ple_w_proj, final_norm, loss_target, m_ffn1_norm, m_ffn1_w_gate, m_ffn1_w_up, m_ffn1_w_down, m_mix_norm, m_w_in, m_attn_out_norm, m_ssm_lambda_re, m_ssm_lambda_im, m_ssm_log_dt, m_ssm_b_re, m_ssm_b_im, m_ssm_c_re, m_ssm_c_im, m_ssm_d, m_ssm_w_glu, m_ssm_b_glu, m_ssm_out_norm, m_w_out, m_ffn2_norm, m_ffn2_w_gate, m_ffn2_w_up, m_ffn2_w_down, m_ple_norm, m_ple_w_gate, m_ple_w_proj, m_final_norm, v_ffn1_norm, v_ffn1_w_gate, v_ffn1_w_up, v_ffn1_w_down, v_mix_norm, v_w_in, v_attn_out_norm, v_ssm_lambda_re, v_ssm_lambda_im, v_ssm_log_dt, v_ssm_b_re, v_ssm_b_im, v_ssm_c_re, v_ssm_c_im, v_ssm_d, v_ssm_w_glu, v_ssm_b_glu, v_ssm_out_norm, v_w_out, v_ffn2_norm, v_ffn2_w_gate, v_ffn2_w_up, v_ffn2_w_down, v_ple_norm, v_ple_w_gate, v_ple_w_proj, v_final_norm):
    args = locals()
    w = {k: args[k] for k in WEIGHTS}
    m = {k: args["m_" + k] for k in WEIGHTS}
    v = {k: args["v_" + k] for k in WEIGHTS}
    w2 = {k: w[k].reshape(w[k].shape[-2:]) for k, _ in BIG}

    axes = [ax for _, ax in BIG]
    padded = {k: -(-w2[k].shape[ax] // LANES) * LANES for k, ax in BIG}
    shards = [_pad_to(w2[k].astype(BF16), ax, padded[k]) for k, ax in BIG]
    full = dict(zip([k for k, _ in BIG], _all_gather_weights(shards, axes)))

    loss_local, dx, gbig, gsmall = _local_step(x[0], p[0, 0], loss_target[0], w, full)
    loss = lax.psum(loss_local, MESH_AXES)

    summed = _reduce_scatter([gbig[k] for k, _ in BIG], axes)
    n_small = sum(w[k].size for k in SMALL)
    rows = -(-n_small // (SUBLANES * LANES)) * SUBLANES
    gs_sum = _all_reduce_small(_pack([gsmall[k] for k in SMALL], rows))

    grads, delta, new_m, new_v = {}, {}, {}, {}
    for (k, ax), gfull in zip(BIG, summed):
        g2 = lax.slice_in_dim(gfull, 0, w2[k].shape[ax], axis=ax)
        d2, nm2, nv2 = _adamw("adamw_" + k, w2[k], g2, m[k].reshape(w2[k].shape), v[k].reshape(w2[k].shape))
        grads[k], delta[k], new_m[k], new_v[k] = (t.reshape(w[k].shape) for t in (g2, d2, nm2, nv2))
    small_like = [w[k] for k in SMALL]
    ds, nms, nvs = _adamw("adamw_small", _pack(small_like, rows), gs_sum, _pack([m[k] for k in SMALL], rows),
                          _pack([v[k] for k in SMALL], rows))
    for k, g_, d_, nm_, nv_ in zip(SMALL, _unpack(gs_sum, small_like), _unpack(ds, small_like),
                                   _unpack(nms, small_like), _unpack(nvs, small_like)):
        grads[k], delta[k], new_m[k], new_v[k] = g_, d_, nm_, nv_

    return (loss, dx[None], *[grads[k] for k in WEIGHTS], *[delta[k] for k in WEIGHTS],
            *[new_m[k] for k in WEIGHTS], *[new_v[k] for k in WEIGHTS])
```

```python
import functools
import math

import jax
import jax.numpy as jnp
from jax import lax
from jax.experimental import pallas as pl
from jax.experimental.pallas import tpu as pltpu

F32 = jnp.float32
BF16 = jnp.bfloat16
MESH = pl.DeviceIdType.MESH
MESH_AXES = ("x", "y", "c")
N_CHIPS = 4
N_DEV = 8

V7X_VMEM_LIMIT_BYTES = 56 << 20
LANES = 128
SUBLANES = 8

HEAD_DIM = 64
SWA_BLOCK = 128
DILATIONS = (1, 4, 16)
SSM_BLOCK_GROUPS = 8
NORM_EPS = 1e-6
MASK_VALUE = -1e30

ADAM_LR = 0.001
ADAM_B1 = 0.9
ADAM_B2 = 0.999
ADAM_EPS = 1e-08
ADAM_WD = 0.01
ADAM_STEP = 10

GELU_C = math.sqrt(2.0 / math.pi)
GELU_K = 0.044715


def _pcall(body, **kw):
    return pl.pallas_call(body, **kw)


def _params(*sem):
    return pltpu.CompilerParams(dimension_semantics=sem, vmem_limit_bytes=V7X_VMEM_LIMIT_BYTES)


def _tile(n, target, align):
    best = None
    for t in range(align, min(n, target) + 1, align):
        if n % t == 0:
            best = t
    return n if best is None else best


def _sigmoid(x):
    return 1.0 / (1.0 + jnp.exp(-x))


def _mm(name, lhs, rhs, outs, pairs=((0, 0, 0),), epilogue=None, extras=(), ta=False, tb=False,
        tm=1024, tn=512, tk=2048):
    nl, nr, ne, no = len(lhs), len(rhs), len(extras), len(outs)
    n_acc = 1 + max(p[2] for p in pairs)
    (K, M) = lhs[0].shape if ta else lhs[0].shape[::-1]
    (N, K2) = rhs[0].shape if tb else rhs[0].shape[::-1]
    assert K == K2, (name, lhs[0].shape, rhs[0].shape)
    tm, tn, tk = _tile(M, tm, LANES), _tile(N, tn, LANES), _tile(K, tk, LANES)
    nk = K // tk
    if epilogue is None:
        epilogue = lambda accs, ex: accs
    dn = (((0 if ta else 1,), (1 if tb else 0,)), ((), ()))

    def body(*refs):
        l, r = refs[:nl], refs[nl:nl + nr]
        e = refs[nl + nr:nl + nr + ne]
        o = refs[nl + nr + ne:nl + nr + ne + no]
        acc = refs[nl + nr + ne + no:]
        k = pl.program_id(2)
        parts = [None] * n_acc
        for li, ri, ai in pairs:
            d = lax.dot_general(l[li][...].astype(BF16), r[ri][...].astype(BF16), dn,
                                preferred_element_type=F32)
            parts[ai] = d if parts[ai] is None else parts[ai] + d

        def finish(accs):
            res = epilogue(accs, [x[...] for x in e])
            for ref, val in zip(o, res):
                ref[...] = val.astype(ref.dtype)

        if nk == 1:
            finish(parts)
        else:
            @pl.when(k == 0)
            def _():
                for ai in range(n_acc):
                    acc[ai][...] = parts[ai]

            @pl.when(k > 0)
            def _():
                for ai in range(n_acc):
                    acc[ai][...] += parts[ai]

            @pl.when(k == nk - 1)
            def _():
                finish([a[...] for a in acc])

    lspec = pl.BlockSpec((tk, tm), lambda i, j, k: (k, i)) if ta else pl.BlockSpec((tm, tk), lambda i, j, k: (i, k))
    rspec = pl.BlockSpec((tn, tk), lambda i, j, k: (j, k)) if tb else pl.BlockSpec((tk, tn), lambda i, j, k: (k, j))
    especs = []
    for arr, kind in extras:
        if kind == "mn":
            especs.append(pl.BlockSpec((tm, tn), lambda i, j, k: (i, j)))
        elif kind == "n":
            especs.append(pl.BlockSpec((1, tn), lambda i, j, k: (0, j)))
        else:
            especs.append(pl.BlockSpec((tm, 1), lambda i, j, k: (i, 0)))
    res = _pcall(
        body, name=name,
        grid=(M // tm, N // tn, nk),
        in_specs=[lspec] * nl + [rspec] * nr + especs,
        out_specs=[pl.BlockSpec((tm, tn), lambda i, j, k: (i, j))] * no,
        out_shape=[jax.ShapeDtypeStruct((M, N), dt) for dt in outs],
        scratch_shapes=[pltpu.VMEM((tm, tn), F32)] * (n_acc if nk > 1 else 0),
        compiler_params=_params("parallel", "parallel", "arbitrary"),
    )(*lhs, *rhs, *[a for a, _ in extras])
    return res


def _rowwise(name, fn, ins, params, outs, accs=(), ts=256):
    S = ins[0].shape[0]
    ts = _tile(S, ts, 16)
    ni, npar, no, na = len(ins), len(params), len(outs), len(accs)

    def body(*refs):
        i_refs, p_refs = refs[:ni], refs[ni:ni + npar]
        o_refs = refs[ni + npar:ni + npar + no]
        a_refs = refs[ni + npar + no:]
        res_o, res_a = fn([r[...] for r in i_refs], [r[...] for r in p_refs])
        for ref, val in zip(o_refs, res_o):
            ref[...] = val.astype(ref.dtype)
        if na:
            @pl.when(pl.program_id(0) == 0)
            def _():
                for ref in a_refs:
                    ref[...] = jnp.zeros(ref.shape, F32)

            for ref, val in zip(a_refs, res_a):
                ref[...] += val

    res = _pcall(
        body, name=name,
        grid=(S // ts,),
        in_specs=[pl.BlockSpec((ts, a.shape[1]), lambda i: (i, 0)) for a in ins]
        + [pl.BlockSpec(p.shape, lambda i: (0, 0)) for p in params],
        out_specs=[pl.BlockSpec((ts, w), lambda i: (i, 0)) for w, _ in outs]
        + [pl.BlockSpec((1, w), lambda i: (0, 0)) for w in accs],
        out_shape=[jax.ShapeDtypeStruct((S, w), dt) for w, dt in outs]
        + [jax.ShapeDtypeStruct((1, w), F32) for w in accs],
        compiler_params=_params("arbitrary"),
    )(*ins, *params)
    return res


def _xhat(x):
    r = lax.rsqrt(jnp.mean(x * x, axis=-1, keepdims=True) + NORM_EPS)
    return x * r, r


def _rms_fwd(name, x, g):
    def fn(ins, ps):
        xh, _ = _xhat(ins[0])
        return [xh * ps[0]], []

    return _rowwise(name, fn, [x], [g], [(x.shape[1], BF16)])[0]


def _rms_bwd(name, dn, x, g, dres=None, copy_scale=None):
    w = x.shape[1]

    def fn(ins, ps):
        dn_, x_ = ins[0], ins[1]
        xh, r = _xhat(x_)
        dxh = dn_ * ps[0]
        dx = r * (dxh - xh * jnp.mean(dxh * xh, axis=-1, keepdims=True))
        if dres is not None:
            dx = dx + ins[2]
        o = [dx] + ([dx * copy_scale] if copy_scale is not None else [])
        return o, [jnp.sum(dn_ * xh, axis=0, keepdims=True)]

    ins = [dn, x] + ([dres] if dres is not None else [])
    outs = [(w, F32)] + ([(w, BF16)] if copy_scale is not None else [])
    res = _rowwise(name, fn, ins, [g], outs, accs=[w])
    return res[:-1], res[-1]


def _swiglu_epilogue(accs, ex):
    g, u = accs
    return [g, u, g * _sigmoid(g) * u]


def _dswiglu_epilogue(accs, ex):
    da = accs[0]
    g, u = ex
    sg = _sigmoid(g)
    return [da * u * (sg * (1.0 + g * (1.0 - sg))), da * (g * sg)]


def _ffn_fwd(tag, h, gnorm, wg, wu, wd):
    n = _rms_fwd(tag + "_norm", h, gnorm)
    g, u, a = _mm(tag + "_up", [n], [wg, wu], [F32, F32, BF16], pairs=((0, 0, 0), (0, 1, 1)),
                  epilogue=_swiglu_epilogue, tm=1024, tn=512)
    (hout,) = _mm(tag + "_down", [a], [wd], [F32], extras=[(h, "mn")],
                  epilogue=lambda accs, ex: [ex[0] + 0.5 * accs[0]], tm=512, tn=1024, tk=8192)
    return hout, (n, g, u, a)


def _ffn_bwd(tag, dh, dhb_half, h, gnorm, wg, wu, wd, saved, copy_scale):
    n, g, u, a = saved
    dg, du = _mm(tag + "_dact", [dhb_half], [wd], [BF16, BF16], tb=True, extras=[(g, "mn"), (u, "mn")],
                 epilogue=_dswiglu_epilogue, tm=1024, tn=512)
    (dwd,) = _mm(tag + "_dwd", [a], [dhb_half], [BF16], ta=True, tm=512, tn=2048, tk=2048)
    dwg, dwu = _mm(tag + "_dwgu", [n], [dg, du], [BF16, BF16], pairs=((0, 0, 0), (0, 1, 1)), ta=True,
                   tm=1024, tn=512, tk=2048)
    (dn,) = _mm(tag + "_dn", [dg, du], [wg, wu], [F32], pairs=((0, 0, 0), (1, 1, 0)), tb=True,
                tm=1024, tn=1024, tk=1408)
    douts, dgn = _rms_bwd(tag + "_dnorm", dn, h, gnorm, dres=dh, copy_scale=copy_scale)
    return douts, dgn, dwg, dwu, dwd


ATTN_HEAD_PAIRS = 4


def _to_attn_order(a):
    S, w = a.shape
    return a.reshape(S // 16, 16, w).transpose(1, 0, 2).reshape(S, w)


def _from_attn_order(a):
    S, w = a.shape
    return a.reshape(16, S // 16, w).transpose(1, 0, 2).reshape(S, w)


def _attn_geom(S, d):
    s16 = S // 16
    if d == 16:
        return (16, s16), (1, SWA_BLOCK), (lambda r, b: (r, b)), 16, s16 // SWA_BLOCK
    if d == 4:
        return (4, 4, s16), (4, 1, SWA_BLOCK // 4), (lambda r, b: (0, r, b)), 4, s16 // (SWA_BLOCK // 4)
    return (16, s16), (16, SWA_BLOCK // 16), (lambda r, b: (0, b)), 1, s16 // (SWA_BLOCK // 16)


def _attn_pos(rho, d):
    if d == 16:
        return rho
    if d == 4:
        return 4 * (rho & 31) + (rho >> 5)
    return 16 * (rho & 7) + (rho >> 3)


def _attn_spec(S, d, lb, col, shift=0):
    _, blk, idx, _, nb = _attn_geom(S, d)
    return pl.BlockSpec(blk + (lb,), lambda r, cb, b: idx(r, jnp.clip(b + shift, 0, nb - 1)) + (col(cb),))


def _attn_view(a, d):
    return a.reshape(_attn_geom(a.shape[0], d)[0] + (a.shape[1],))


def _attn_valid(d):
    qp = _attn_pos(lax.broadcasted_iota(jnp.int32, (SWA_BLOCK, 2 * SWA_BLOCK), 0), d)
    kk = lax.broadcasted_iota(jnp.int32, (SWA_BLOCK, 2 * SWA_BLOCK), 1)
    kp = _attn_pos(kk & (SWA_BLOCK - 1), d)
    is_prev = kk < SWA_BLOCK
    return qp, kp, is_prev


def _head_masks(rows=SWA_BLOCK):
    lane = lax.broadcasted_iota(jnp.int32, (rows, LANES), 1)
    return [lane < HEAD_DIM, lane >= HEAD_DIM]


def _attn_ld(ref, sl):
    t = ref[(slice(None),) * (len(ref.shape) - 1) + (sl,)]
    return t.reshape(-1, t.shape[-1])


def _attn_st(ref, sl, val):
    ref[(slice(None),) * (len(ref.shape) - 1) + (sl,)] = val.reshape(ref.shape[:-1] + (val.shape[-1],))


def _per_head(t, first):
    sw = pltpu.roll(t, HEAD_DIM, 1)
    lo = lax.broadcasted_iota(jnp.int32, t.shape, 1) < HEAD_DIM
    return jnp.where(lo, t, sw) if first else jnp.where(lo, sw, t)


def _dot_nt(a, b):
    return lax.dot_general(a, b, (((1,), (1,)), ((), ())), preferred_element_type=F32)


def _dot_tn(a, b):
    return lax.dot_general(a, b, (((0,), (0,)), ((), ())), preferred_element_type=F32)


def _dot(a, b):
    return jnp.dot(a, b, preferred_element_type=F32)


def _keep(mask, t):
    return jnp.where(mask, t.astype(F32), 0.0).astype(BF16)


def _attn_cols(A):
    lb = min(A, LANES * ATTN_HEAD_PAIRS)
    ncol = A // lb
    return lb, ncol, [lambda cb, part=part: part * ncol + cb for part in range(3)], (lambda cb: cb)


def _attn_fwd_stage(name, qkv, d, prev, final):
    S, A3 = qkv.shape
    A = A3 // 3
    lb, ncol, (cq, ck, cv), ca = _attn_cols(A)
    view, _, _, nres, nb = _attn_geom(S, d)
    scale = HEAD_DIM ** -0.5
    has_prev = prev is not None

    def body(*refs):
        q_ref, kp_ref, kc_ref, vp_ref, vc_ref = refs[:5]
        p_refs = refs[5:8] if has_prev else ()
        o_refs = refs[5 + len(p_refs):]
        b = pl.program_id(2)
        qp, kp_, is_prev = _attn_valid(d)
        valid = (is_prev & (kp_ >= qp) & (b > 0)) | (jnp.logical_not(is_prev) & (kp_ <= qp))
        hm, hm2 = _head_masks(), _head_masks(2 * SWA_BLOCK)
        for hp in range(lb // LANES):
            sl = slice(hp * LANES, (hp + 1) * LANES)
            q = _attn_ld(q_ref, sl)
            k2 = jnp.concatenate([_attn_ld(kp_ref, sl), _attn_ld(kc_ref, sl)], axis=0).astype(BF16)
            v2 = jnp.concatenate([_attn_ld(vp_ref, sl), _attn_ld(vc_ref, sl)], axis=0)
            o = jnp.zeros((SWA_BLOCK, LANES), F32)
            m = jnp.zeros((SWA_BLOCK, LANES), F32)
            l = jnp.zeros((SWA_BLOCK, LANES), F32)
            for hh in range(2):
                s = jnp.where(valid, _dot_nt(_keep(hm[hh], q), k2) * scale, MASK_VALUE)
                mh = jnp.max(s, axis=-1, keepdims=True)
                p = jnp.exp(s - mh)
                lh = jnp.sum(p, axis=-1, keepdims=True)
                o = o + _dot(p.astype(BF16), _keep(hm2[hh], v2))
                m = jnp.where(hm[hh], mh, m)
                l = jnp.where(hm[hh], lh, l)
            if has_prev:
                po, pm, pl_ = (_attn_ld(r, sl) for r in p_refs)
                mn = jnp.maximum(m, pm)
                w_new, w_old = jnp.exp(m - mn), jnp.exp(pm - mn)
                o = o * w_new + po * w_old
                l = l * w_new + pl_ * w_old
                m = mn
            if final:
                _attn_st(o_refs[0], sl, o / l)
                _attn_st(o_refs[1], sl, m + jnp.log(l))
            else:
                _attn_st(o_refs[0], sl, o)
                _attn_st(o_refs[1], sl, m)
                _attn_st(o_refs[2], sl, l)

    n_out = 2 if final else 3
    qk = _attn_view(qkv, d)
    prev_v = [_attn_view(t, d) for t in prev] if has_prev else []
    sp = functools.partial(_attn_spec, S, d, lb)
    res = _pcall(
        body, name=name,
        grid=(nres, ncol, nb),
        in_specs=[sp(cq), sp(ck, -1), sp(ck), sp(cv, -1), sp(cv)] + [sp(ca)] * len(prev_v),
        out_specs=[sp(ca)] * n_out,
        out_shape=[jax.ShapeDtypeStruct(view + (A,), F32)] * n_out,
        compiler_params=_params("parallel", "parallel", "arbitrary"),
    )(qk, qk, qk, qk, qk, *prev_v)
    return [t.reshape(S, A) for t in res]


def _attn_fwd(qkv):
    st = None
    for i, d in enumerate(DILATIONS):
        st = _attn_fwd_stage("attn_fwd_d%d" % d, qkv, d, st, final=(i == len(DILATIONS) - 1))
    return st


def _attn_dq_stage(name, qkv, do, lse, delta, d, prev):
    S, A3 = qkv.shape
    A = A3 // 3
    lb, ncol, (cq, ck, cv), ca = _attn_cols(A)
    view, _, _, nres, nb = _attn_geom(S, d)
    scale = HEAD_DIM ** -0.5
    has_prev = prev is not None

    def body(*refs):
        q_ref, kp_ref, kc_ref, vp_ref, vc_ref, do_ref, lse_ref, dl_ref = refs[:8]
        b = pl.program_id(2)
        qp, kp_, is_prev = _attn_valid(d)
        valid = (is_prev & (kp_ >= qp) & (b > 0)) | (jnp.logical_not(is_prev) & (kp_ <= qp))
        hm, hm2 = _head_masks(), _head_masks(2 * SWA_BLOCK)
        for hp in range(lb // LANES):
            sl = slice(hp * LANES, (hp + 1) * LANES)
            q, do_, lse_, dl_ = (_attn_ld(r, sl) for r in (q_ref, do_ref, lse_ref, dl_ref))
            k2 = jnp.concatenate([_attn_ld(kp_ref, sl), _attn_ld(kc_ref, sl)], axis=0)
            v2 = jnp.concatenate([_attn_ld(vp_ref, sl), _attn_ld(vc_ref, sl)], axis=0).astype(BF16)
            k2b = k2.astype(BF16)
            dq = jnp.zeros((SWA_BLOCK, LANES), F32)
            for hh in range(2):
                doh = _keep(hm[hh], do_)
                lh, dh = _per_head(lse_, hh == 0), _per_head(dl_, hh == 0)
                lh2, dh2 = jnp.concatenate([lh, lh], axis=1), jnp.concatenate([dh, dh], axis=1)
                s = _dot_nt(_keep(hm[hh], q), k2b) * scale
                p = jnp.where(valid, jnp.exp(s - lh2), 0.0)
                ds = p * (_dot_nt(doh, v2) - dh2)
                dq = dq + _dot(ds.astype(BF16), _keep(hm2[hh], k2))
            dq = dq * scale
            if has_prev:
                dq = dq + _attn_ld(refs[8], sl)
            _attn_st(refs[-1], sl, dq)

    qk = _attn_view(qkv, d)
    acts = [_attn_view(t, d) for t in (do, lse, delta)] + ([_attn_view(prev, d)] if has_prev else [])
    sp = functools.partial(_attn_spec, S, d, lb)
    res = _pcall(
        body, name=name,
        grid=(nres, ncol, nb),
        in_specs=[sp(cq), sp(ck, -1), sp(ck), sp(cv, -1), sp(cv)] + [sp(ca)] * len(acts),
        out_specs=sp(ca),
        out_shape=jax.ShapeDtypeStruct(view + (A,), F32),
        compiler_params=_params("parallel", "parallel", "arbitrary"),
    )(qk, qk, qk, qk, qk, *acts)
    return res.reshape(S, A)


def _attn_dkv_stage(name, qkv, do, lse, delta, d, prev):
    S, A3 = qkv.shape
    A = A3 // 3
    lb, ncol, (cq, ck, cv), ca = _attn_cols(A)
    view, _, _, nres, nb = _attn_geom(S, d)
    scale = HEAD_DIM ** -0.5
    has_prev = prev is not None

    def body(*refs):
        k_ref, v_ref, qc_ref, qn_ref, doc_ref, don_ref, lc_ref, ln_ref, dc_ref, dn_ref = refs[:10]
        j = pl.program_id(2)
        rr = lax.broadcasted_iota(jnp.int32, (2 * SWA_BLOCK, SWA_BLOCK), 0)
        qp = _attn_pos(rr & (SWA_BLOCK - 1), d)
        kp_ = _attn_pos(lax.broadcasted_iota(jnp.int32, (2 * SWA_BLOCK, SWA_BLOCK), 1), d)
        valid = ((rr < SWA_BLOCK) & (kp_ <= qp)) | ((rr >= SWA_BLOCK) & (kp_ >= qp) & (j < nb - 1))
        hm2 = _head_masks(2 * SWA_BLOCK)
        for hp in range(lb // LANES):
            sl = slice(hp * LANES, (hp + 1) * LANES)
            kb, vb = _attn_ld(k_ref, sl).astype(BF16), _attn_ld(v_ref, sl).astype(BF16)
            q2 = jnp.concatenate([_attn_ld(qc_ref, sl), _attn_ld(qn_ref, sl)], axis=0)
            do2 = jnp.concatenate([_attn_ld(doc_ref, sl), _attn_ld(don_ref, sl)], axis=0)
            l2 = jnp.concatenate([_attn_ld(lc_ref, sl), _attn_ld(ln_ref, sl)], axis=0)
            d2 = jnp.concatenate([_attn_ld(dc_ref, sl), _attn_ld(dn_ref, sl)], axis=0)
            dk = jnp.zeros((SWA_BLOCK, LANES), F32)
            dv = jnp.zeros((SWA_BLOCK, LANES), F32)
            for hh in range(2):
                qh, doh = _keep(hm2[hh], q2), _keep(hm2[hh], do2)
                lh, dh = _per_head(l2, hh == 0), _per_head(d2, hh == 0)
                s = _dot_nt(qh, kb) * scale
                p = jnp.where(valid, jnp.exp(s - lh), 0.0)
                dv = dv + _dot_tn(p.astype(BF16), doh)
                ds = p * (_dot_nt(doh, vb) - dh)
                dk = dk + _dot_tn(ds.astype(BF16), qh)
            dk = dk * scale
            if has_prev:
                dk = dk + _attn_ld(refs[10], sl)
                dv = dv + _attn_ld(refs[11], sl)
            _attn_st(refs[-2], sl, dk)
            _attn_st(refs[-1], sl, dv)

    qk = _attn_view(qkv, d)
    acts = [_attn_view(t, d) for t in (do, lse, delta)]
    prev_v = [_attn_view(t, d) for t in prev] if has_prev else []
    sp = functools.partial(_attn_spec, S, d, lb)
    res = _pcall(
        body, name=name,
        grid=(nres, ncol, nb),
        in_specs=[sp(ck), sp(cv), sp(cq), sp(cq, 1), sp(ca), sp(ca, 1), sp(ca), sp(ca, 1), sp(ca), sp(ca, 1)]
        + [sp(ca)] * len(prev_v),
        out_specs=[sp(ca), sp(ca)],
        out_shape=[jax.ShapeDtypeStruct(view + (A,), F32)] * 2,
        compiler_params=_params("parallel", "parallel", "arbitrary"),
    )(qk, qk, qk, qk, acts[0], acts[0], acts[1], acts[1], acts[2], acts[2], *prev_v)
    return [t.reshape(S, A) for t in res]


def _attn_delta(dya, ya):
    S, A = ya.shape
    ri = lax.broadcasted_iota(jnp.int32, (A, A), 0) // HEAD_DIM
    ci = lax.broadcasted_iota(jnp.int32, (A, A), 1) // HEAD_DIM
    ones_bd = (ri == ci).astype(BF16)

    def fn(ins, ps):
        prod = ins[0] * ins[1]
        hi = prod.astype(BF16)
        lo = (prod - hi.astype(F32)).astype(BF16)
        return [_dot(hi, ps[0]) + _dot(lo, ps[0])], []

    return _rowwise("attn_delta", fn, [dya, ya], [ones_bd], [(A, F32)])[0]


def _attn_bwd(qkv, dya, ya, lse):
    delta = _attn_delta(dya, ya)
    dq, dkv = None, None
    for d in DILATIONS:
        dq = _attn_dq_stage("attn_dq_d%d" % d, qkv, dya, lse, delta, d, dq)
        dkv = _attn_dkv_stage("attn_dkv_d%d" % d, qkv, dya, lse, delta, d, dkv)
    return dq, dkv[0], dkv[1]


def _ssm_perm(a, T):
    S, w = a.shape
    return a.reshape(S // T, SUBLANES, T // SUBLANES, w).transpose(0, 2, 1, 3).reshape(S, w)


def _ssm_unperm(a, T):
    S, w = a.shape
    return a.reshape(S // T, T // SUBLANES, SUBLANES, w).transpose(0, 2, 1, 3).reshape(S, w)


def _ssm_powers(lam_ref, pw_ref, T, ns):
    n = (lax.broadcasted_iota(jnp.int32, (T, 1), 0) // SUBLANES + 1).astype(F32)
    mag = jnp.exp(n * lam_ref[0, 0:1, :])
    ang = n * lam_ref[0, 1:2, :]
    pw_ref[:, 0:ns] = mag * jnp.cos(ang)
    pw_ref[:, ns:2 * ns] = mag * jnp.sin(ang)


def _ssm_scan(xs, off, pw_ref, carry_ref, T, ns, reverse):
    Tc = T // SUBLANES
    sgn = -1.0 if reverse else 1.0
    ar, ai = pw_ref[0:SUBLANES, 0:ns], sgn * pw_ref[0:SUBLANES, ns:2 * ns]

    def rows(i):
        return pl.ds(pl.multiple_of(off + i * SUBLANES, SUBLANES), SUBLANES)

    def step(k, h):
        hr, hi = h
        r = rows(Tc - 1 - k if reverse else k)
        nr = ar * hr - ai * hi + xs[r, 0:ns]
        ni = ar * hi + ai * hr + xs[r, ns:2 * ns]
        xs[r, 0:ns] = nr
        xs[r, ns:2 * ns] = ni
        return nr, ni

    z = jnp.zeros((SUBLANES, ns), F32)
    er, ei = lax.fori_loop(0, Tc, step, (z, z))
    atr, ati = pw_ref[T - SUBLANES:T, 0:ns], sgn * pw_ref[T - SUBLANES:T, ns:2 * ns]
    rowid = lax.broadcasted_iota(jnp.int32, (SUBLANES, ns), 0)
    cr, ci = carry_ref[:, 0:ns], carry_ref[:, ns:2 * ns]
    ctr, cti = z, z
    for jj in range(SUBLANES):
        j = SUBLANES - 1 - jj if reverse else jj
        sel = rowid == j
        ctr, cti = jnp.where(sel, cr, ctr), jnp.where(sel, ci, cti)
        ejr = jnp.broadcast_to(jnp.sum(jnp.where(sel, er, 0.0), axis=0, keepdims=True), (SUBLANES, ns))
        eji = jnp.broadcast_to(jnp.sum(jnp.where(sel, ei, 0.0), axis=0, keepdims=True), (SUBLANES, ns))
        cr, ci = ejr + atr * cr - ati * ci, eji + atr * ci + ati * cr
    carry_ref[:, 0:ns] = cr
    carry_ref[:, ns:2 * ns] = ci

    def fix(i, _):
        r = rows(i)
        pr_rows = pl.ds(pl.multiple_of((Tc - 1 - i if reverse else i) * SUBLANES, SUBLANES), SUBLANES)
        pr, pi = pw_ref[pr_rows, 0:ns], sgn * pw_ref[pr_rows, ns:2 * ns]
        xs[r, 0:ns] += pr * ctr - pi * cti
        xs[r, ns:2 * ns] += pr * cti + pi * ctr
        return 0

    lax.fori_loop(0, Tc, fix, 0)
    return ctr, cti


def _ssm_fwd(ufp, bb, cc, lam_dt, drow, T):
    S, W = ufp.shape
    GB, cw, ns2 = bb.shape
    ns = ns2 // 2
    NCH = S // T

    def body(uf_ref, bb_ref, cc_ref, lam_ref, d_ref, y_ref, hs_ref, xs, pw, carry):
        @pl.when(pl.program_id(1) == 0)
        def _():
            _ssm_powers(lam_ref, pw, T, ns)
            carry[...] = jnp.zeros(carry.shape, F32)

        uf = uf_ref[...]
        xs[...] = _dot(uf.astype(BF16), bb_ref[0])
        hs_ref[0, 0] = carry[...]
        _ssm_scan(xs, 0, pw, carry, T, ns, reverse=False)
        y_ref[...] = _dot(xs[...].astype(BF16), cc_ref[0]) + d_ref[...] * uf

    return _pcall(
        body, name="ssm_fwd",
        grid=(GB, NCH),
        in_specs=[pl.BlockSpec((T, cw), lambda g, c: (c, g)),
                  pl.BlockSpec((1, cw, ns2), lambda g, c: (g, 0, 0)),
                  pl.BlockSpec((1, ns2, cw), lambda g, c: (g, 0, 0)),
                  pl.BlockSpec((1, 2, ns), lambda g, c: (g, 0, 0)),
                  pl.BlockSpec((1, cw), lambda g, c: (0, g))],
        out_specs=[pl.BlockSpec((T, cw), lambda g, c: (c, g)),
                   pl.BlockSpec((1, 1, SUBLANES, ns2), lambda g, c: (g, c, 0, 0))],
        out_shape=[jax.ShapeDtypeStruct((S, W), F32),
                   jax.ShapeDtypeStruct((GB, NCH, SUBLANES, ns2), F32)],
        scratch_shapes=[pltpu.VMEM((T, ns2), F32), pltpu.VMEM((T, ns2), F32), pltpu.VMEM((SUBLANES, ns2), F32)],
        compiler_params=_params("arbitrary", "arbitrary"),
    )(ufp, bb, cc, lam_dt, drow)


def _ssm_bwd(ufp, dyp, bb, bbt, cc, cct, lam_dt, drow, hstart, T):
    S, W = ufp.shape
    GB, cw, ns2 = bb.shape
    ns = ns2 // 2
    NCH = S // T

    def body(uf_ref, dy_ref, bb_ref, bbt_ref, cc_ref, cct_ref, lam_ref, d_ref, hs_ref,
             duf_ref, dbb_ref, dcc_ref, da_ref, dd_ref, hb, ls, pw, carry_f, carry_b):
        @pl.when(pl.program_id(1) == 0)
        def _():
            _ssm_powers(lam_ref, pw, T, ns)
            carry_b[...] = jnp.zeros(carry_b.shape, F32)
            dbb_ref[...] = jnp.zeros(dbb_ref.shape, F32)
            dcc_ref[...] = jnp.zeros(dcc_ref.shape, F32)
            da_ref[...] = jnp.zeros(da_ref.shape, F32)
            dd_ref[...] = jnp.zeros(dd_ref.shape, F32)

        uf, dy = uf_ref[...], dy_ref[...]
        ufb, dyb = uf.astype(BF16), dy.astype(BF16)
        hb[SUBLANES:T + SUBLANES, :] = _dot(ufb, bb_ref[0])
        carry_f[...] = hs_ref[0, 0]
        ctr, cti = _ssm_scan(hb, SUBLANES, pw, carry_f, T, ns, reverse=False)
        hb[0:SUBLANES, 0:ns] = ctr
        hb[0:SUBLANES, ns:ns2] = cti
        ls[...] = _dot(dyb, cct_ref[0])
        _ssm_scan(ls, 0, pw, carry_b, T, ns, reverse=True)
        lv = ls[...]
        lb = lv.astype(BF16)
        dbb_ref[0] += _dot_tn(ufb, lb)
        dcc_ref[0] += _dot_tn(hb[SUBLANES:T + SUBLANES, :].astype(BF16), dyb)
        lr, li = lv[:, 0:ns], lv[:, ns:ns2]
        hpr, hpi = hb[0:T, 0:ns], hb[0:T, ns:ns2]
        dar = jnp.sum(lr * hpr + li * hpi, axis=0, keepdims=True)
        dai = jnp.sum(li * hpr - lr * hpi, axis=0, keepdims=True)
        da_ref[0, :, 0:ns] += jnp.broadcast_to(dar, (SUBLANES, ns))
        da_ref[0, :, ns:ns2] += jnp.broadcast_to(dai, (SUBLANES, ns))
        duf_ref[...] = _dot(lb, bbt_ref[0]) + d_ref[...] * dy
        dd_ref[...] += jnp.sum(dy * uf, axis=0, keepdims=True)

    rc = lambda c: NCH - 1 - c
    return _pcall(
        body, name="ssm_bwd",
        grid=(GB, NCH),
        in_specs=[pl.BlockSpec((T, cw), lambda g, c: (rc(c), g)),
                  pl.BlockSpec((T, cw), lambda g, c: (rc(c), g)),
                  pl.BlockSpec((1, cw, ns2), lambda g, c: (g, 0, 0)),
                  pl.BlockSpec((1, ns2, cw), lambda g, c: (g, 0, 0)),
                  pl.BlockSpec((1, ns2, cw), lambda g, c: (g, 0, 0)),
                  pl.BlockSpec((1, cw, ns2), lambda g, c: (g, 0, 0)),
                  pl.BlockSpec((1, 2, ns), lambda g, c: (g, 0, 0)),
                  pl.BlockSpec((1, cw), lambda g, c: (0, g)),
                  pl.BlockSpec((1, 1, SUBLANES, ns2), lambda g, c: (g, rc(c), 0, 0))],
        out_specs=[pl.BlockSpec((T, cw), lambda g, c: (rc(c), g)),
                   pl.BlockSpec((1, cw, ns2), lambda g, c: (g, 0, 0)),
                   pl.BlockSpec((1, ns2, cw), lambda g, c: (g, 0, 0)),
                   pl.BlockSpec((1, SUBLANES, ns2), lambda g, c: (g, 0, 0)),
                   pl.BlockSpec((1, cw), lambda g, c: (0, g))],
        out_shape=[jax.ShapeDtypeStruct((S, W), F32),
                   jax.ShapeDtypeStruct((GB, cw, ns2), F32),
                   jax.ShapeDtypeStruct((GB, ns2, cw), F32),
                   jax.ShapeDtypeStruct((GB, SUBLANES, ns2), F32),
                   jax.ShapeDtypeStruct((1, W), F32)],
        scratch_shapes=[pltpu.VMEM((T + SUBLANES, ns2), F32), pltpu.VMEM((T, ns2), F32), pltpu.VMEM((T, ns2), F32),
                        pltpu.VMEM((SUBLANES, ns2), F32), pltpu.VMEM((SUBLANES, ns2), F32)],
        compiler_params=_params("arbitrary", "arbitrary"),
    )(ufp, dyp, bb, bbt, cc, cct, lam_dt, drow, hstart)


def _ssm_disc_math(lr, li, logdt, br, bi):
    dt = jnp.exp(logdt)
    mag = jnp.exp(lr * dt)
    ar = mag * jnp.cos(li * dt)
    ai = mag * jnp.sin(li * dt)
    nr, ni = ar - 1.0, ai
    den = lr * lr + li * li
    cr = (nr * lr + ni * li) / den
    ci = (ni * lr - nr * li) / den
    return ar, ai, cr * br - ci * bi, cr * bi + ci * br


def _ssm_disc(lr, li, logdt, br, bi):
    C = br.shape[1]

    def fn(ins, ps):
        _, _, bbr, bbi = _ssm_disc_math(*ins)
        dt = jnp.exp(ins[2])
        return [ins[0] * dt, ins[1] * dt, bbr, bbi], []

    return _rowwise("ssm_disc", fn, [lr, li, logdt, br, bi], [], [(1, F32), (1, F32), (C, F32), (C, F32)], ts=512)


def _ssm_disc_bwd(lr, li, logdt, br, bi, dar, dai, dbbr, dbbi):
    C = br.shape[1]

    def fn(ins, ps):
        _, vjp = jax.vjp(_ssm_disc_math, *ins[:5])
        return list(vjp(tuple(ins[5:]))), []

    return _rowwise("ssm_disc_bwd", fn, [lr, li, logdt, br, bi, dar, dai, dbbr, dbbi], [],
                    [(1, F32), (1, F32), (1, F32), (C, F32), (C, F32)], ts=512)


def _block_diag(t):
    GB, g, a, b = t.shape
    eye = jnp.eye(g, dtype=t.dtype)
    return (t[:, :, :, None, :] * eye[None, :, None, :, None]).reshape(GB, g * a, g * b)


def _block_diag_take(t, g):
    GB, ga, gb_ = t.shape
    a, b = ga // g, gb_ // g
    eye = jnp.eye(g, dtype=t.dtype)
    return (t.reshape(GB, g, a, g, b) * eye[None, :, None, :, None]).sum(axis=3)


def _loss_head(h4, tgt, gf):
    D = h4.shape[1]

    def fn(ins, ps):
        x, t = ins
        xh, r = _xhat(x)
        err = xh * ps[0] - t
        dn = err * (1.0 / D)
        dxh = dn * ps[0]
        dx = r * (dxh - xh * jnp.mean(dxh * xh, axis=-1, keepdims=True))
        return [dx], [jnp.sum(err * err, axis=0, keepdims=True), jnp.sum(dn * xh, axis=0, keepdims=True)]

    return _rowwise("loss_head", fn, [h4, tgt], [gf], [(D, F32)], accs=[D, D])


def _gelu(x):
    return 0.5 * x * (1.0 + jnp.tanh(GELU_C * (x + GELU_K * x * x * x)))


def _gelu_grad(x):
    t = jnp.tanh(GELU_C * (x + GELU_K * x * x * x))
    return 0.5 * (1.0 + t) + 0.5 * x * (1.0 - t * t) * GELU_C * (1.0 + 3.0 * GELU_K * x * x)


def _mesh_pos():
    return lax.axis_index("x"), lax.axis_index("y"), lax.axis_index("c")


def _other_chips(x, y):
    return [(1 - x, y), (x, 1 - y), (1 - x, 1 - y)]


def _remote(src, dst, send, recv, dev):
    return pltpu.make_async_remote_copy(src_ref=src, dst_ref=dst, send_sem=send, recv_sem=recv,
                                        device_id=dev, device_id_type=MESH)


ANY = pl.BlockSpec(memory_space=pl.ANY)


def _shard_region(ref, axis, shape, s, half=None):
    R, Cc = shape
    hr = R // 2
    if axis == 1:
        rows = pl.ds(0, R) if half is None else pl.ds(pl.multiple_of(half * hr, 16), hr)
        return ref.at[rows, pl.ds(pl.multiple_of(s * Cc, LANES), Cc)]
    if half is None:
        return ref.at[pl.ds(pl.multiple_of(s * R, 16), R), :]
    return ref.at[pl.ds(pl.multiple_of(s * R + half * hr, 16), hr), :]


def _all_gather_weights(shards, axes):
    nw = len(shards)
    shapes = [s.shape for s in shards]
    full = [((R, N_CHIPS * Cc) if ax == 1 else (N_CHIPS * R, Cc)) for (R, Cc), ax in zip(shapes, axes)]

    def body(*refs):
        sh, out = refs[:nw], refs[nw:2 * nw]
        send, recv, loc = refs[2 * nw:]
        x, y, c = _mesh_pos()
        me, sib = (x, y, c), (x, y, 1 - c)
        chips = _other_chips(x, y)
        sid = [2 * cx + cy for cx, cy in chips]
        my_s = 2 * x + y

        def reg(w, s, half=None):
            return _shard_region(out[w], axes[w], shapes[w], s, half)

        local = [pltpu.make_async_copy(sh[w], reg(w, my_s), loc.at[w]) for w in range(nw)]
        for cp in local:
            cp.start()
        sends = []
        for w in range(nw):
            hr = shapes[w][0] // 2
            src = sh[w].at[pl.ds(pl.multiple_of(c * hr, 16), hr), :]
            for j, chip in enumerate(chips):
                cp = _remote(src, reg(w, my_s, c), send.at[6 * w + j], recv.at[6 * w + j], (*chip, c))
                cp.start()
                sends.append(cp)
        for w in range(nw):
            for j in range(3):
                r_ = reg(w, sid[j], c)
                _remote(r_, r_, send.at[6 * w + j], recv.at[6 * w + j], me).wait_recv()
                cp = _remote(r_, r_, send.at[6 * w + 3 + j], recv.at[6 * w + 3 + j], sib)
                cp.start()
                sends.append(cp)
        for w in range(nw):
            for j in range(3):
                r_ = reg(w, sid[j], 1 - c)
                _remote(r_, r_, send.at[6 * w + 3 + j], recv.at[6 * w + 3 + j], me).wait_recv()
        for cp in sends:
            cp.wait_send()
        for cp in local:
            cp.wait()

    return _pcall(
        body, name="ag_weights",
        in_specs=[ANY] * nw, out_specs=[ANY] * nw,
        out_shape=[jax.ShapeDtypeStruct(f, BF16) for f in full],
        scratch_shapes=[pltpu.SemaphoreType.DMA((6 * nw,)), pltpu.SemaphoreType.DMA((6 * nw,)),
                        pltpu.SemaphoreType.DMA((nw,))],
    )(*shards)


def _push_pair(name, src, tr, nblk, src_block, out_rows, dst_block, local):
    cw = src.shape[1]
    c_arr = lax.axis_index("c").astype(jnp.int32).reshape(1)

    def body(c_ref, src_ref, out_ref, send, recv, lsem):
        i = pl.program_id(0)
        x, y, c = _mesh_pos()
        dst = out_ref.at[pl.ds(pl.multiple_of(dst_block(i, c_ref[0]) * tr, 16), tr), :]
        cp = _remote(src_ref, dst, send, recv, (x, y, 1 - c))
        cp.start()
        if local:
            lc = pltpu.make_async_copy(src_ref, dst, lsem)
            lc.start()
            lc.wait()
        cp.wait_send()

        @pl.when(i == nblk - 1)
        def _():
            got = out_ref.at[pl.ds(0, nblk * tr), :]
            _remote(got, got, send, recv, (x, y, c)).wait_recv()

    return _pcall(
        body, name=name,
        grid_spec=pltpu.PrefetchScalarGridSpec(
            num_scalar_prefetch=1, grid=(nblk,),
            in_specs=[pl.BlockSpec((tr, cw), lambda i, c_ref: (src_block(i, c_ref[0]), 0))],
            out_specs=ANY,
            scratch_shapes=[pltpu.SemaphoreType.DMA, pltpu.SemaphoreType.DMA, pltpu.SemaphoreType.DMA]),
        out_shape=jax.ShapeDtypeStruct((out_rows, cw), src.dtype),
        compiler_params=_params("arbitrary"),
    )(c_arr, src)


def _sum_half(name, g, theirs, tr, nblk, src_block):
    cw = g.shape[1]
    c_arr = lax.axis_index("c").astype(jnp.int32).reshape(1)

    def body(c_ref, g_ref, t_ref, o_ref):
        o_ref[...] = (g_ref[...].astype(F32) + t_ref[...].astype(F32)).astype(BF16)

    return _pcall(
        body, name=name,
        grid_spec=pltpu.PrefetchScalarGridSpec(
            num_scalar_prefetch=1, grid=(nblk,),
            in_specs=[pl.BlockSpec((tr, cw), lambda i, c_ref: (src_block(i, c_ref[0]), 0)),
                      pl.BlockSpec((tr, cw), lambda i, c_ref: (i, 0))],
            out_specs=pl.BlockSpec((tr, cw), lambda i, c_ref: (i, 0))),
        out_shape=jax.ShapeDtypeStruct((nblk * tr, cw), BF16),
        compiler_params=_params("arbitrary"),
    )(c_arr, g, theirs)


def _rs_chips(qs, axes):
    nw = len(qs)
    pshapes = [((q.shape[0], q.shape[1] // N_CHIPS) if ax == 1 else q.shape[1:]) for q, ax in zip(qs, axes)]

    def body(*refs):
        q, own, got = refs[:nw], refs[nw:2 * nw], refs[2 * nw:3 * nw]
        send, recv, loc = refs[3 * nw:]
        x, y, c = _mesh_pos()
        chips = _other_chips(x, y)

        def part(w, s):
            if axes[w] == 1:
                cw = pshapes[w][1]
                return q[w].at[:, pl.ds(pl.multiple_of(s * cw, LANES), cw)]
            return q[w].at[s]

        cps = []
        for w in range(nw):
            lc = pltpu.make_async_copy(part(w, 2 * x + y), own[w], loc.at[w])
            lc.start()
            cps.append(lc)
            for j, (cx, cy) in enumerate(chips):
                rc = _remote(part(w, 2 * cx + cy), got[w].at[j], send.at[3 * w + j], recv.at[3 * w + j], (cx, cy, c))
                rc.start()
                cps.append(rc)
        for cp in cps:
            cp.wait()

    res = _pcall(
        body, name="rs_chips",
        in_specs=[ANY] * nw, out_specs=[ANY] * (2 * nw),
        out_shape=[jax.ShapeDtypeStruct(p, BF16) for p in pshapes]
        + [jax.ShapeDtypeStruct((3,) + tuple(p), BF16) for p in pshapes],
        scratch_shapes=[pltpu.SemaphoreType.DMA((3 * nw,)), pltpu.SemaphoreType.DMA((3 * nw,)), pltpu.SemaphoreType.DMA((nw,))],
    )(*qs)
    return res[:nw], res[nw:]


def _sum_tiles(name, arrs, out_dtype):
    R, Cc = arrs[0].shape
    tr = _tile(R, max(16, (1 << 20) // Cc // 16 * 16), 16)

    def body(*refs):
        acc = refs[0][...].astype(F32)
        for r in refs[1:-1]:
            acc = acc + r[...].astype(F32)
        refs[-1][...] = acc.astype(out_dtype)

    return _pcall(
        body, name=name, grid=(R // tr,),
        in_specs=[pl.BlockSpec((tr, Cc), lambda i: (i, 0))] * len(arrs),
        out_specs=pl.BlockSpec((tr, Cc), lambda i: (i, 0)),
        out_shape=jax.ShapeDtypeStruct((R, Cc), out_dtype),
        compiler_params=_params("parallel"),
    )(*arrs)


def _reduce_scatter(grads, axes):
    block_bytes = 3 << 19
    qs, hrs = [], []
    for w, (g, ax) in enumerate(zip(grads, axes)):
        rows, cw = g.shape
        hr = rows // 2 if ax == 1 else rows // N_CHIPS // 2
        tr = _tile(hr, max(16, block_bytes // (cw * 2) // 16 * 16), 16)
        nth = hr // tr
        if ax == 1:
            nblk, blk = nth, (lambda i, half, nth=nth: half * nth + i)
        else:
            nblk, blk = N_CHIPS * nth, (lambda i, half, nth=nth: (i // nth) * (2 * nth) + half * nth + i % nth)
        theirs = _push_pair("rs_pair%d" % w, g, tr, nblk, lambda i, c, blk=blk: blk(i, 1 - c), nblk * tr,
                            lambda i, c: i, local=False)
        q = _sum_half("rs_sum_pair%d" % w, g, theirs, tr, nblk, blk)
        qs.append(q if ax == 1 else q.reshape(N_CHIPS, hr, cw))
        hrs.append(hr)
    own, got = _rs_chips(qs, axes)
    outs = []
    for w, (o, g_, hr) in enumerate(zip(own, got, hrs)):
        half = _sum_tiles("rs_sum_chips%d" % w, [o, g_[0], g_[1], g_[2]], F32)
        tr = _tile(hr, max(16, block_bytes // (half.shape[1] * 4) // 16 * 16), 16)
        nth = hr // tr
        outs.append(_push_pair("rs_swap%d" % w, half, tr, nth, lambda i, c: i, 2 * hr,
                               lambda i, c, nth=nth: c * nth + i, local=True))
    return outs


def _all_reduce_small(v):
    n = v.shape[0]

    def body(v_ref, out_ref, buf, send, recv):
        x, y, c = _mesh_pos()
        my = 4 * x + 2 * y + c
        buf[my] = v_ref[...]
        cps = []
        for k in range(1, N_DEV):
            fx, fy, fc = (k >> 2) & 1, (k >> 1) & 1, k & 1
            peer = (1 - x if fx else x, 1 - y if fy else y, 1 - c if fc else c)
            cp = _remote(v_ref, buf.at[my], send.at[k - 1], recv.at[k - 1], peer)
            cp.start()
            cps.append((cp, 4 * peer[0] + 2 * peer[1] + peer[2]))
        for k, (cp, pid) in enumerate(cps):
            _remote(v_ref, buf.at[pid], send.at[k], recv.at[k], (x, y, c)).wait_recv()
        acc = buf[0]
        for i in range(1, N_DEV):
            acc = acc + buf[i]
        out_ref[...] = acc
        for cp, _ in cps:
            cp.wait_send()

    return _pcall(
        body, name="ar_small",
        in_specs=[pl.BlockSpec(memory_space=pltpu.VMEM)], out_specs=pl.BlockSpec(memory_space=pltpu.VMEM),
        out_shape=jax.ShapeDtypeStruct((n, LANES), F32),
        scratch_shapes=[pltpu.VMEM((N_DEV, n, LANES), F32), pltpu.SemaphoreType.DMA((N_DEV - 1,)),
                        pltpu.SemaphoreType.DMA((N_DEV - 1,))],
        compiler_params=pltpu.CompilerParams(vmem_limit_bytes=V7X_VMEM_LIMIT_BYTES),
    )(v)


def _adamw(name, w, g, m, v):
    R, Cc = w.shape
    tr = _tile(R, max(8, (1 << 19) // Cc // 8 * 8), 8)
    c1 = 1.0 - ADAM_B1 ** ADAM_STEP
    c2 = 1.0 - ADAM_B2 ** ADAM_STEP

    def body(w_ref, g_ref, m_ref, v_ref, d_ref, nm_ref, nv_ref):
        g_ = g_ref[...]
        nm = ADAM_B1 * m_ref[...] + (1.0 - ADAM_B1) * g_
        nv = ADAM_B2 * v_ref[...] + (1.0 - ADAM_B2) * (g_ * g_)
        d_ref[...] = -ADAM_LR * ((nm / c1) / (jnp.sqrt(nv / c2) + ADAM_EPS) + ADAM_WD * w_ref[...])
        nm_ref[...] = nm
        nv_ref[...] = nv

    spec = pl.BlockSpec((tr, Cc), lambda i: (i, 0))
    return _pcall(
        body, name=name, grid=(R // tr,),
        in_specs=[spec] * 4, out_specs=[spec] * 3,
        out_shape=[jax.ShapeDtypeStruct((R, Cc), F32)] * 3,
        compiler_params=_params("parallel"),
    )(w, g, m, v)


def _pack(arrs, rows):
    flat = jnp.concatenate([a.reshape(-1) for a in arrs])
    return jnp.pad(flat, (0, rows * LANES - flat.shape[0])).reshape(rows, LANES)


def _unpack(packed, like):
    flat, out, o = packed.reshape(-1), [], 0
    for a in like:
        out.append(flat[o:o + a.size].reshape(a.shape))
        o += a.size
    return out


BIG = (
    ("ffn1_w_gate", 1), ("ffn1_w_up", 1), ("ffn1_w_down", 0), ("w_in", 1), ("ssm_w_glu", 0), ("w_out", 0),
    ("ffn2_w_gate", 1), ("ffn2_w_up", 1), ("ffn2_w_down", 0), ("ple_w_gate", 0), ("ple_w_proj", 1),
)
SMALL = ("ffn1_norm", "mix_norm", "attn_out_norm", "ssm_lambda_re", "ssm_lambda_im", "ssm_log_dt", "ssm_b_re", "ssm_b_im",
         "ssm_c_re", "ssm_c_im", "ssm_d", "ssm_b_glu", "ssm_out_norm", "ffn2_norm", "ple_norm", "final_norm")
WEIGHTS = ("ffn1_norm", "ffn1_w_gate", "ffn1_w_up", "ffn1_w_down", "mix_norm", "w_in", "attn_out_norm", "ssm_lambda_re",
           "ssm_lambda_im", "ssm_log_dt", "ssm_b_re", "ssm_b_im", "ssm_c_re", "ssm_c_im", "ssm_d", "ssm_w_glu", "ssm_b_glu",
           "ssm_out_norm", "w_out", "ffn2_norm", "ffn2_w_gate", "ffn2_w_up", "ffn2_w_down", "ple_norm", "ple_w_gate",
           "ple_w_proj", "final_norm")


def _pad_to(a, axis, n):
    pad = [(0, 0), (0, 0)]
    pad[axis] = (0, n - a.shape[axis])
    return jnp.pad(a, pad)


def _local_step(x, p, tgt, w, full):
    S, D = x.shape
    A = w["attn_out_norm"].shape[-1]
    W = w["ssm_d"].shape[-1]
    G, P = w["ssm_lambda_re"].shape[-2:]
    C = w["ssm_b_re"].shape[-1]
    GB = G // SSM_BLOCK_GROUPS
    T = min(1024, S)
    row = lambda name: w[name].reshape(1, -1)
    gs = {}

    h1, ffn1_saved = _ffn_fwd("ffn1", x, row("ffn1_norm"), full["ffn1_w_gate"], full["ffn1_w_up"], full["ffn1_w_down"])
    n2 = _rms_fwd("mix_norm", h1, row("mix_norm"))
    w_in = full["w_in"]
    n2p = _to_attn_order(n2)
    (qkv,) = _mm("w_in_qkv", [n2p], [w_in[:, :3 * A]], [F32], tm=1024, tn=1024)
    (s_in,) = _mm("w_in_ssm", [n2], [w_in[:, 3 * A:]], [F32], tm=1024, tn=1024)
    ya, lse = _attn_fwd(qkv)

    col = lambda name: w[name].reshape(G * P, 1)
    logdt_x = jnp.repeat(w["ssm_log_dt"].reshape(G), P).reshape(G * P, 1)
    b_re, b_im = w["ssm_b_re"].reshape(G * P, C), w["ssm_b_im"].reshape(G * P, C)
    lrdt, lidt, bbr, bbi = _ssm_disc(col("ssm_lambda_re"), col("ssm_lambda_im"), logdt_x, b_re, b_im)
    gsz = SSM_BLOCK_GROUPS
    to_bb = lambda t: _block_diag(t.reshape(GB, gsz, P, C).transpose(0, 1, 3, 2))
    bb = jnp.concatenate([to_bb(bbr), to_bb(bbi)], axis=2).astype(BF16)
    to_cc = lambda t: _block_diag(t.reshape(GB, gsz, C, P).transpose(0, 1, 3, 2))
    cc = jnp.concatenate([to_cc(w["ssm_c_re"]), -to_cc(w["ssm_c_im"])], axis=1).astype(BF16)
    lam_dt = jnp.stack([lrdt.reshape(GB, gsz * P), lidt.reshape(GB, gsz * P)], axis=1)
    ufp = _ssm_perm(s_in, T)
    ypre, hstart = _ssm_fwd(ufp, bb, cc, lam_dt, row("ssm_d"), T)

    def glu_in(ins, ps):
        yg = _gelu(ins[0])
        return [yg, yg], []

    yg, ygb = _rowwise("ssm_gelu", glu_in, [ypre], [], [(W, F32), (W, BF16)])
    w_glu = full["ssm_w_glu"]

    def glu_out(accs, ex):
        gl = accs[0] + ex[1]
        return [ex[0] * _sigmoid(gl), gl]

    ybp, gl = _mm("ssm_glu", [ygb], [w_glu], [F32, F32], extras=[(yg, "mn"), (row("ssm_b_glu"), "n")],
                  epilogue=glu_out, tm=1024, tn=1024)
    yb = _ssm_unperm(ybp, T)
    na = _from_attn_order(_rms_fwd("attn_out_norm", ya, row("attn_out_norm")))
    nb = _rms_fwd("ssm_out_norm", yb, row("ssm_out_norm"))
    w_out = full["w_out"]
    (h2,) = _mm("w_out", [na, nb], [w_out[:A], w_out[A:]], [F32], pairs=((0, 0, 0), (1, 1, 0)), extras=[(h1, "mn")],
                epilogue=lambda accs, ex: [ex[0] + accs[0]], tm=1024, tn=1024)
    h3, ffn2_saved = _ffn_fwd("ffn2", h2, row("ffn2_norm"), full["ffn2_w_gate"], full["ffn2_w_up"], full["ffn2_w_down"])
    n4 = _rms_fwd("ple_norm", h3, row("ple_norm"))
    (pe,) = _mm("ple_proj", [p], [full["ple_w_proj"]], [F32], tm=1024, tn=1024)

    def ple_out(accs, ex):
        gate = _sigmoid(accs[0])
        return [ex[1] + gate * ex[0], gate]

    h4, gate = _mm("ple_gate", [n4], [full["ple_w_gate"]], [F32, F32], extras=[(pe, "mn"), (h3, "mn")],
                   epilogue=ple_out, tm=1024, tn=1024)

    dh4, err2, gs["final_norm"] = _loss_head(h4, tgt, row("final_norm"))
    loss = (0.5 / D) * jnp.sum(err2)

    def ple_bwd(ins, ps):
        dh, gt, pe_ = ins
        return [dh * gt, dh * pe_ * gt * (1.0 - gt)], []

    dpe, dpg = _rowwise("ple_bwd", ple_bwd, [dh4, gate, pe], [], [(D, BF16), (D, BF16)])
    (d_ple_proj,) = _mm("ple_dproj", [p], [dpe], [BF16], ta=True, tm=256, tn=2048, tk=1024)
    (d_ple_gate,) = _mm("ple_dgate", [n4], [dpg], [BF16], ta=True, tm=1024, tn=1024, tk=2048)
    (dn4,) = _mm("ple_dn", [dpg], [full["ple_w_gate"]], [F32], tb=True, tm=1024, tn=1024)
    (dh3, dh3b), gs["ple_norm"] = _rms_bwd("ple_dnorm", dn4, h3, row("ple_norm"), dres=dh4, copy_scale=0.5)
    (dh2, dh2b), gs["ffn2_norm"], d_ffn2_g, d_ffn2_u, d_ffn2_d = _ffn_bwd(
        "ffn2", dh3, dh3b, h2, row("ffn2_norm"), full["ffn2_w_gate"], full["ffn2_w_up"], full["ffn2_w_down"],
        ffn2_saved, copy_scale=1.0)
    (dna,) = _mm("w_out_dna", [_to_attn_order(dh2b)], [w_out[:A]], [F32], tb=True, tm=1024, tn=1024)
    (dnb,) = _mm("w_out_dnb", [dh2b], [w_out[A:]], [F32], tb=True, tm=1024, tn=1024)
    (d_wout_a,) = _mm("w_out_dwa", [na], [dh2b], [BF16], ta=True, tm=1024, tn=1024, tk=2048)
    (d_wout_b,) = _mm("w_out_dwb", [nb], [dh2b], [BF16], ta=True, tm=1024, tn=1024, tk=2048)
    d_w_out = jnp.concatenate([d_wout_a, d_wout_b], axis=0)
    (dya,), gs["attn_out_norm"] = _rms_bwd("attn_out_dnorm", dna, ya, row("attn_out_norm"))
    (dyb,), gs["ssm_out_norm"] = _rms_bwd("ssm_out_dnorm", dnb, yb, row("ssm_out_norm"))

    dybp = _ssm_perm(dyb, T)

    def glu_bwd(ins, ps):
        dy, yg_, gl_ = ins
        sg = _sigmoid(gl_)
        dgl = dy * yg_ * sg * (1.0 - sg)
        return [dgl, dy * sg], [jnp.sum(dgl, axis=0, keepdims=True)]

    dgl, dyg_direct, gs["ssm_b_glu"] = _rowwise("ssm_glu_bwd", glu_bwd, [dybp, yg, gl], [], [(W, BF16), (W, F32)], accs=[W])
    (d_w_glu,) = _mm("ssm_dwglu", [ygb], [dgl], [BF16], ta=True, tm=1024, tn=1024, tk=2048)
    (dypre,) = _mm("ssm_dyg", [dgl], [w_glu], [F32], tb=True, extras=[(dyg_direct, "mn"), (ypre, "mn")],
                   epilogue=lambda accs, ex: [(accs[0] + ex[0]) * _gelu_grad(ex[1])], tm=1024, tn=1024)
    dufp, dbb, dcc, da, gs["ssm_d"] = _ssm_bwd(ufp, dypre, bb, bb.transpose(0, 2, 1), cc, cc.transpose(0, 2, 1),
                                               lam_dt, row("ssm_d"), hstart, T)
    ns = gsz * P
    from_bb = lambda t: _block_diag_take(t, gsz).transpose(0, 1, 3, 2).reshape(G * P, C)
    from_cc = lambda t: _block_diag_take(t, gsz).transpose(0, 1, 3, 2).reshape(w["ssm_c_re"].shape)
    gs["ssm_c_re"], gs["ssm_c_im"] = from_cc(dcc[:, :ns]), -from_cc(dcc[:, ns:])
    dar, dai = da[:, 0, :ns].reshape(G * P, 1), da[:, 0, ns:].reshape(G * P, 1)
    dlr, dli, dlogdt, dbr, dbi = _ssm_disc_bwd(col("ssm_lambda_re"), col("ssm_lambda_im"), logdt_x, b_re, b_im,
                                               dar, dai, from_bb(dbb[:, :, :ns]), from_bb(dbb[:, :, ns:]))
    gs["ssm_lambda_re"], gs["ssm_lambda_im"] = dlr.reshape(w["ssm_lambda_re"].shape), dli.reshape(w["ssm_lambda_im"].shape)
    gs["ssm_log_dt"] = dlogdt.reshape(G, P).sum(axis=1).reshape(w["ssm_log_dt"].shape)
    gs["ssm_b_re"], gs["ssm_b_im"] = dbr.reshape(w["ssm_b_re"].shape), dbi.reshape(w["ssm_b_im"].shape)
    ds_in = _ssm_unperm(dufp, T)

    dq, dk, dv = _attn_bwd(qkv, dya, ya, lse)
    dqkv = jnp.concatenate([dq, dk, dv], axis=1).astype(BF16)
    (d_w_qkv,) = _mm("w_in_dw_qkv", [n2p], [dqkv], [BF16], ta=True, tm=1024, tn=1024, tk=2048)
    (d_w_s,) = _mm("w_in_dw_ssm", [n2], [ds_in], [BF16], ta=True, tm=1024, tn=1024, tk=2048)
    d_w_in = jnp.concatenate([d_w_qkv, d_w_s], axis=1)
    dz = jnp.concatenate([_from_attn_order(dqkv), ds_in.astype(BF16)], axis=1)
    (dn2,) = _mm("w_in_dn", [dz], [w_in], [F32], tb=True, tm=1024, tn=1024)
    (dh1, dh1b), gs["mix_norm"] = _rms_bwd("mix_dnorm", dn2, h1, row("mix_norm"), dres=dh2, copy_scale=0.5)
    (dx,), gs["ffn1_norm"], d_ffn1_g, d_ffn1_u, d_ffn1_d = _ffn_bwd(
        "ffn1", dh1, dh1b, x, row("ffn1_norm"), full["ffn1_w_gate"], full["ffn1_w_up"], full["ffn1_w_down"],
        ffn1_saved, copy_scale=None)

    big = {"ffn1_w_gate": d_ffn1_g, "ffn1_w_up": d_ffn1_u, "ffn1_w_down": d_ffn1_d, "w_in": d_w_in, "ssm_w_glu": d_w_glu,
           "w_out": d_w_out, "ffn2_w_gate": d_ffn2_g, "ffn2_w_up": d_ffn2_u, "ffn2_w_down": d_ffn2_d,
           "ple_w_gate": d_ple_gate, "ple_w_proj": d_ple_proj}
    small = {k: gs[k].reshape(w[k].shape) for k in SMALL}
    return loss, dx, big, small


def kernel(x, p, ffn1_norm, ffn1_w_gate, ffn1_w_up, ffn1_w_down, mix_norm, w_in, attn_out_norm, ssm_lambda_re, ssm_lambda_im, ssm_log_dt, ssm_b_re, ssm_b_im, ssm_c_re, ssm_c_im, ssm_d, ssm_w_glu, ssm_b_glu, ssm_out_norm, w_out, ffn2_norm, ffn2_w_gate, ffn2_w_up, ffn2_w_down, ple_norm, ple_w_gate, ple_w_proj, final_norm, loss_target, m_ffn1_norm, m_ffn1_w_gate, m_ffn1_w_up, m_ffn1_w_down, m_mix_norm, m_w_in, m_attn_out_norm, m_ssm_lambda_re, m_ssm_lambda_im, m_ssm_log_dt, m_ssm_b_re, m_ssm_b_im, m_ssm_c_re, m_ssm_c_im, m_ssm_d, m_ssm_w_glu, m_ssm_b_glu, m_ssm_out_norm, m_w_out, m_ffn2_norm, m_ffn2_w_gate, m_ffn2_w_up, m_ffn2_w_down, m_ple_norm, m_ple_w_gate, m_ple_w_proj, m_final_norm, v_ffn1_norm, v_ffn1_w_gate, v_ffn1_w_up, v_ffn1_w_down, v_mix_norm, v_w_in, v_attn_out_norm, v_ssm_lambda_re, v_ssm_lambda_im, v_ssm_log_dt, v_ssm_b_re, v_ssm_b_im, v_ssm_c_re, v_ssm_c_im, v_ssm_d, v_ssm_w_glu, v_ssm_b_glu, v_ssm_out_norm, v_w_out, v_ffn2_norm, v_ffn2_w_gate, v_ffn2_w_up, v_ffn2_w_down, v_ple_norm, v_ple_w_gate, v_ple_w_proj, v_final_norm):
    args = locals()
    w = {k: args[k] for k in WEIGHTS}
    m = {k: args["m_" + k] for k in WEIGHTS}
    v = {k: args["v_" + k] for k in WEIGHTS}
    w2 = {k: w[k].reshape(w[k].shape[-2:]) for k, _ in BIG}

    axes = [ax for _, ax in BIG]
    padded = {k: -(-w2[k].shape[ax] // LANES) * LANES for k, ax in BIG}
    shards = [_pad_to(w2[k].astype(BF16), ax, padded[k]) for k, ax in BIG]
    full = dict(zip([k for k, _ in BIG], _all_gather_weights(shards, axes)))

    loss_local, dx, gbig, gsmall = _local_step(x[0], p[0, 0], loss_target[0], w, full)
    loss = lax.psum(loss_local, MESH_AXES)

    summed = _reduce_scatter([gbig[k] for k, _ in BIG], axes)
    n_small = sum(w[k].size for k in SMALL)
    rows = -(-n_small // (SUBLANES * LANES)) * SUBLANES
    gs_sum = _all_reduce_small(_pack([gsmall[k] for k in SMALL], rows))

    grads, delta, new_m, new_v = {}, {}, {}, {}
    for (k, ax), gfull in zip(BIG, summed):
        g2 = lax.slice_in_dim(gfull, 0, w2[k].shape[ax], axis=ax)
        d2, nm2, nv2 = _adamw("adamw_" + k, w2[k], g2, m[k].reshape(w2[k].shape), v[k].reshape(w2[k].shape))
        grads[k], delta[k], new_m[k], new_v[k] = (t.reshape(w[k].shape) for t in (g2, d2, nm2, nv2))
    small_like = [w[k] for k in SMALL]
    ds, nms, nvs = _adamw("adamw_small", _pack(small_like, rows), gs_sum, _pack([m[k] for k in SMALL], rows),
                          _pack([v[k] for k in SMALL], rows))
    for k, g_, d_, nm_, nv_ in zip(SMALL, _unpack(gs_sum, small_like), _unpack(ds, small_like),
                                   _unpack(nms, small_like), _unpack(nvs, small_like)):
        grads[k], delta[k], new_m[k], new_v[k] = g_, d_, nm_, nv_

    return (loss, dx[None], *[grads[k] for k in WEIGHTS], *[delta[k] for k in WEIGHTS],
            *[new_m[k] for k in WEIGHTS], *[new_v[k] for k in WEIGHTS])
```

```python
import functools
import math

import jax
import jax.numpy as jnp
from jax import lax
from jax.experimental import pallas as pl
from jax.experimental.pallas import tpu as pltpu

F32 = jnp.float32
BF16 = jnp.bfloat16
MESH = pl.DeviceIdType.MESH
MESH_AXES = ("x", "y", "c")
N_CHIPS = 4
N_DEV = 8

V7X_VMEM_LIMIT_BYTES = 56 << 20
LANES = 128
SUBLANES = 8

HEAD_DIM = 64
SWA_BLOCK = 128
DILATIONS = (1, 4, 16)
SSM_BLOCK_GROUPS = 8
NORM_EPS = 1e-6
MASK_VALUE = -1e30

ADAM_LR = 0.001
ADAM_B1 = 0.9
ADAM_B2 = 0.999
ADAM_EPS = 1e-08
ADAM_WD = 0.01
ADAM_STEP = 10

GELU_C = math.sqrt(2.0 / math.pi)
GELU_K = 0.044715


def _pcall(body, **kw):
    return pl.pallas_call(body, **kw)


def _params(*sem):
    return pltpu.CompilerParams(dimension_semantics=sem, vmem_limit_bytes=V7X_VMEM_LIMIT_BYTES)


def _tile(n, target, align):
    best = None
    for t in range(align, min(n, target) + 1, align):
        if n % t == 0:
            best = t
    return n if best is None else best


def _sigmoid(x):
    return 1.0 / (1.0 + jnp.exp(-x))


def _mm(name, lhs, rhs, outs, pairs=((0, 0, 0),), epilogue=None, extras=(), ta=False, tb=False,
        tm=1024, tn=512, tk=2048, n_split=1):
    nl, nr, ne, no = len(lhs), len(rhs), len(extras), len(outs)
    n_acc = 1 + max(p[2] for p in pairs)
    (K, M) = lhs[0].shape if ta else lhs[0].shape[::-1]
    (N, K2) = rhs[0].shape if tb else rhs[0].shape[::-1]
    assert K == K2, (name, lhs[0].shape, rhs[0].shape)
    tm, tn, tk = _tile(M, tm, LANES), _tile(N, tn, LANES), _tile(K, tk, LANES)
    nk = K // tk
    if epilogue is None:
        epilogue = lambda accs, ex: accs
    dn = (((0 if ta else 1,), (1 if tb else 0,)), ((), ()))

    def body(*refs):
        l, r = refs[:nl], refs[nl:nl + nr]
        e = refs[nl + nr:nl + nr + ne]
        o = refs[nl + nr + ne:nl + nr + ne + no]
        acc = refs[nl + nr + ne + no:]
        k = pl.program_id(2)
        if nk == 1 and n_split > 1:
            cs = tn // n_split
            for c in range(n_split):
                cols = slice(c * cs, (c + 1) * cs)
                parts = [None] * n_acc
                for li, ri, ai in pairs:
                    rt = r[ri][cols, :] if tb else r[ri][:, cols]
                    d = lax.dot_general(l[li][...].astype(BF16), rt.astype(BF16), dn, preferred_element_type=F32)
                    parts[ai] = d if parts[ai] is None else parts[ai] + d
                ex = [x[...] if kind == "m" else x[:, cols] for x, (_, kind) in zip(e, extras)]
                for ref, val in zip(o, epilogue(parts, ex)):
                    ref[:, cols] = val.astype(ref.dtype)
            return
        parts = [None] * n_acc
        for li, ri, ai in pairs:
            d = lax.dot_general(l[li][...].astype(BF16), r[ri][...].astype(BF16), dn,
                                preferred_element_type=F32)
            parts[ai] = d if parts[ai] is None else parts[ai] + d

        def finish(accs):
            res = epilogue(accs, [x[...] for x in e])
            for ref, val in zip(o, res):
                ref[...] = val.astype(ref.dtype)

        if nk == 1:
            finish(parts)
        else:
            @pl.when(k == 0)
            def _():
                for ai in range(n_acc):
                    acc[ai][...] = parts[ai]

            @pl.when(k > 0)
            def _():
                for ai in range(n_acc):
                    acc[ai][...] += parts[ai]

            @pl.when(k == nk - 1)
            def _():
                finish([a[...] for a in acc])

    lspec = pl.BlockSpec((tk, tm), lambda i, j, k: (k, i)) if ta else pl.BlockSpec((tm, tk), lambda i, j, k: (i, k))
    rspec = pl.BlockSpec((tn, tk), lambda i, j, k: (j, k)) if tb else pl.BlockSpec((tk, tn), lambda i, j, k: (k, j))
    especs = []
    for arr, kind in extras:
        if kind == "mn":
            especs.append(pl.BlockSpec((tm, tn), lambda i, j, k: (i, j)))
        elif kind == "n":
            especs.append(pl.BlockSpec((1, tn), lambda i, j, k: (0, j)))
        else:
            especs.append(pl.BlockSpec((tm, 1), lambda i, j, k: (i, 0)))
    res = _pcall(
        body, name=name,
        grid=(M // tm, N // tn, nk),
        in_specs=[lspec] * nl + [rspec] * nr + especs,
        out_specs=[pl.BlockSpec((tm, tn), lambda i, j, k: (i, j))] * no,
        out_shape=[jax.ShapeDtypeStruct((M, N), dt) for dt in outs],
        scratch_shapes=[pltpu.VMEM((tm, tn), F32)] * (n_acc if nk > 1 else 0),
        compiler_params=_params("parallel", "parallel", "arbitrary"),
    )(*lhs, *rhs, *[a for a, _ in extras])
    return res


def _rowwise(name, fn, ins, params, outs, accs=(), ts=256):
    S = ins[0].shape[0]
    ts = _tile(S, ts, 16)
    ni, npar, no, na = len(ins), len(params), len(outs), len(accs)

    def body(*refs):
        i_refs, p_refs = refs[:ni], refs[ni:ni + npar]
        o_refs = refs[ni + npar:ni + npar + no]
        a_refs = refs[ni + npar + no:]
        res_o, res_a = fn([r[...] for r in i_refs], [r[...] for r in p_refs])
        for ref, val in zip(o_refs, res_o):
            ref[...] = val.astype(ref.dtype)
        if na:
            @pl.when(pl.program_id(0) == 0)
            def _():
                for ref in a_refs:
                    ref[...] = jnp.zeros(ref.shape, F32)

            for ref, val in zip(a_refs, res_a):
                ref[...] += val

    res = _pcall(
        body, name=name,
        grid=(S // ts,),
        in_specs=[pl.BlockSpec((ts, a.shape[1]), lambda i: (i, 0)) for a in ins]
        + [pl.BlockSpec(p.shape, lambda i: (0, 0)) for p in params],
        out_specs=[pl.BlockSpec((ts, w), lambda i: (i, 0)) for w, _ in outs]
        + [pl.BlockSpec((1, w), lambda i: (0, 0)) for w in accs],
        out_shape=[jax.ShapeDtypeStruct((S, w), dt) for w, dt in outs]
        + [jax.ShapeDtypeStruct((1, w), F32) for w in accs],
        compiler_params=_params("arbitrary"),
    )(*ins, *params)
    return res


def _xhat(x):
    r = lax.rsqrt(jnp.mean(x * x, axis=-1, keepdims=True) + NORM_EPS)
    return x * r, r


def _rms_fwd(name, x, g):
    def fn(ins, ps):
        xh, _ = _xhat(ins[0])
        return [xh * ps[0]], []

    return _rowwise(name, fn, [x], [g], [(x.shape[1], BF16)])[0]


def _rms_bwd(name, dn, x, g, dres=None, copy_scale=None):
    w = x.shape[1]

    def fn(ins, ps):
        dn_, x_ = ins[0], ins[1]
        xh, r = _xhat(x_)
        dxh = dn_ * ps[0]
        dx = r * (dxh - xh * jnp.mean(dxh * xh, axis=-1, keepdims=True))
        if dres is not None:
            dx = dx + ins[2]
        o = [dx] + ([dx * copy_scale] if copy_scale is not None else [])
        return o, [jnp.sum(dn_ * xh, axis=0, keepdims=True)]

    ins = [dn, x] + ([dres] if dres is not None else [])
    outs = [(w, F32)] + ([(w, BF16)] if copy_scale is not None else [])
    res = _rowwise(name, fn, ins, [g], outs, accs=[w])
    return res[:-1], res[-1]


def _swiglu_epilogue(accs, ex):
    g, u = accs
    return [g, u, g * _sigmoid(g) * u]


def _dswiglu_epilogue(accs, ex):
    da = accs[0]
    g, u = ex
    sg = _sigmoid(g)
    return [da * u * (sg * (1.0 + g * (1.0 - sg))), da * (g * sg)]


def _ffn_fwd(tag, h, gnorm, wg, wu, wd):
    n = _rms_fwd(tag + "_norm", h, gnorm)
    g, u, a = _mm(tag + "_up", [n], [wg, wu], [F32, F32, BF16], pairs=((0, 0, 0), (0, 1, 1)),
                  epilogue=_swiglu_epilogue, tm=1024, tn=512, n_split=2)
    (hout,) = _mm(tag + "_down", [a], [wd], [F32], extras=[(h, "mn")],
                  epilogue=lambda accs, ex: [ex[0] + 0.5 * accs[0]], tm=512, tn=1024, tk=8192)
    return hout, (n, g, u, a)


def _ffn_bwd(tag, dh, dhb_half, h, gnorm, wg, wu, wd, saved, copy_scale):
    n, g, u, a = saved
    dg, du = _mm(tag + "_dact", [dhb_half], [wd], [BF16, BF16], tb=True, extras=[(g, "mn"), (u, "mn")],
                 epilogue=_dswiglu_epilogue, tm=1024, tn=512, n_split=2)
    (dwd,) = _mm(tag + "_dwd", [a], [dhb_half], [BF16], ta=True, tm=512, tn=2048, tk=2048)
    dwg, dwu = _mm(tag + "_dwgu", [n], [dg, du], [BF16, BF16], pairs=((0, 0, 0), (0, 1, 1)), ta=True,
                   tm=1024, tn=512, tk=2048)
    (dn,) = _mm(tag + "_dn", [dg, du], [wg, wu], [F32], pairs=((0, 0, 0), (1, 1, 0)), tb=True,
                tm=1024, tn=1024, tk=1408)
    douts, dgn = _rms_bwd(tag + "_dnorm", dn, h, gnorm, dres=dh, copy_scale=copy_scale)
    return douts, dgn, dwg, dwu, dwd


ATTN_HEAD_PAIRS = 8


def _to_attn_order(a):
    S, w = a.shape
    return a.reshape(S // 16, 16, w).transpose(1, 0, 2).reshape(S, w)


def _from_attn_order(a):
    S, w = a.shape
    return a.reshape(16, S // 16, w).transpose(1, 0, 2).reshape(S, w)


def _attn_geom(S, d):
    s16 = S // 16
    if d == 16:
        return (16, s16), (1, SWA_BLOCK), (lambda r, b: (r, b)), 16, s16 // SWA_BLOCK
    if d == 4:
        return (4, 4, s16), (4, 1, SWA_BLOCK // 4), (lambda r, b: (0, r, b)), 4, s16 // (SWA_BLOCK // 4)
    return (16, s16), (16, SWA_BLOCK // 16), (lambda r, b: (0, b)), 1, s16 // (SWA_BLOCK // 16)


def _attn_pos(rho, d):
    if d == 16:
        return rho
    if d == 4:
        return 4 * (rho & 31) + (rho >> 5)
    return 16 * (rho & 7) + (rho >> 3)


def _attn_spec(S, d, lb, col, shift=0):
    _, blk, idx, _, nb = _attn_geom(S, d)
    return pl.BlockSpec(blk + (lb,), lambda r, cb, b: idx(r, jnp.clip(b + shift, 0, nb - 1)) + (col(cb),))


def _attn_view(a, d):
    return a.reshape(_attn_geom(a.shape[0], d)[0] + (a.shape[1],))


def _attn_valid(d):
    qp = _attn_pos(lax.broadcasted_iota(jnp.int32, (SWA_BLOCK, 2 * SWA_BLOCK), 0), d)
    kk = lax.broadcasted_iota(jnp.int32, (SWA_BLOCK, 2 * SWA_BLOCK), 1)
    kp = _attn_pos(kk & (SWA_BLOCK - 1), d)
    is_prev = kk < SWA_BLOCK
    return qp, kp, is_prev


def _head_masks(rows=SWA_BLOCK):
    lane = lax.broadcasted_iota(jnp.int32, (rows, LANES), 1)
    return [lane < HEAD_DIM, lane >= HEAD_DIM]


def _attn_ld(ref, sl):
    t = ref[(slice(None),) * (len(ref.shape) - 1) + (sl,)]
    return t.reshape(-1, t.shape[-1])


def _attn_st(ref, sl, val):
    ref[(slice(None),) * (len(ref.shape) - 1) + (sl,)] = val.reshape(ref.shape[:-1] + (val.shape[-1],))


def _per_head(t, first):
    sw = pltpu.roll(t, HEAD_DIM, 1)
    lo = lax.broadcasted_iota(jnp.int32, t.shape, 1) < HEAD_DIM
    return jnp.where(lo, t, sw) if first else jnp.where(lo, sw, t)


def _dot_nt(a, b):
    return lax.dot_general(a, b, (((1,), (1,)), ((), ())), preferred_element_type=F32)


def _dot_tn(a, b):
    return lax.dot_general(a, b, (((0,), (0,)), ((), ())), preferred_element_type=F32)


def _dot(a, b):
    return jnp.dot(a, b, preferred_element_type=F32)


def _keep(mask, t):
    return jnp.where(mask, t.astype(F32), 0.0).astype(BF16)


def _attn_cols(A):
    lb = min(A, LANES * ATTN_HEAD_PAIRS)
    ncol = A // lb
    return lb, ncol, [lambda cb, part=part: part * ncol + cb for part in range(3)], (lambda cb: cb)


def _attn_fwd_stage(name, qkv, d, prev, final):
    S, A3 = qkv.shape
    A = A3 // 3
    lb, ncol, (cq, ck, cv), ca = _attn_cols(A)
    view, _, _, nres, nb = _attn_geom(S, d)
    scale = HEAD_DIM ** -0.5
    has_prev = prev is not None

    def body(*refs):
        q_ref, kp_ref, kc_ref, vp_ref, vc_ref = refs[:5]
        p_refs = refs[5:8] if has_prev else ()
        o_refs = refs[5 + len(p_refs):]
        b = pl.program_id(2)
        qp, kp_, is_prev = _attn_valid(d)
        valid = (is_prev & (kp_ >= qp) & (b > 0)) | (jnp.logical_not(is_prev) & (kp_ <= qp))
        hm, hm2 = _head_masks(), _head_masks(2 * SWA_BLOCK)
        for hp in range(lb // LANES):
            sl = slice(hp * LANES, (hp + 1) * LANES)
            q = _attn_ld(q_ref, sl)
            k2 = jnp.concatenate([_attn_ld(kp_ref, sl), _attn_ld(kc_ref, sl)], axis=0).astype(BF16)
            v2 = jnp.concatenate([_attn_ld(vp_ref, sl), _attn_ld(vc_ref, sl)], axis=0)
            o = jnp.zeros((SWA_BLOCK, LANES), F32)
            m = jnp.zeros((SWA_BLOCK, LANES), F32)
            l = jnp.zeros((SWA_BLOCK, LANES), F32)
            for hh in range(2):
                s = jnp.where(valid, _dot_nt(_keep(hm[hh], q), k2) * scale, MASK_VALUE)
                mh = jnp.max(s, axis=-1, keepdims=True)
                p = jnp.exp(s - mh)
                lh = jnp.sum(p, axis=-1, keepdims=True)
                o = o + _dot(p.astype(BF16), _keep(hm2[hh], v2))
                m = jnp.where(hm[hh], mh, m)
                l = jnp.where(hm[hh], lh, l)
            if has_prev:
                po, pm, pl_ = (_attn_ld(r, sl) for r in p_refs)
                mn = jnp.maximum(m, pm)
                w_new, w_old = jnp.exp(m - mn), jnp.exp(pm - mn)
                o = o * w_new + po * w_old
                l = l * w_new + pl_ * w_old
                m = mn
            if final:
                _attn_st(o_refs[0], sl, o / l)
                _attn_st(o_refs[1], sl, m + jnp.log(l))
            else:
                _attn_st(o_refs[0], sl, o)
                _attn_st(o_refs[1], sl, m)
                _attn_st(o_refs[2], sl, l)

    n_out = 2 if final else 3
    qk = _attn_view(qkv, d)
    prev_v = [_attn_view(t, d) for t in prev] if has_prev else []
    sp = functools.partial(_attn_spec, S, d, lb)
    res = _pcall(
        body, name=name,
        grid=(nres, ncol, nb),
        in_specs=[sp(cq), sp(ck, -1), sp(ck), sp(cv, -1), sp(cv)] + [sp(ca)] * len(prev_v),
        out_specs=[sp(ca)] * n_out,
        out_shape=[jax.ShapeDtypeStruct(view + (A,), F32)] * n_out,
        compiler_params=_params("parallel", "parallel", "arbitrary"),
    )(qk, qk, qk, qk, qk, *prev_v)
    return [t.reshape(S, A) for t in res]


def _attn_fwd(qkv):
    st = None
    for i, d in enumerate(DILATIONS):
        st = _attn_fwd_stage("attn_fwd_d%d" % d, qkv, d, st, final=(i == len(DILATIONS) - 1))
    return st


def _attn_dq_stage(name, qkv, do, lse, delta, d, prev):
    S, A3 = qkv.shape
    A = A3 // 3
    lb, ncol, (cq, ck, cv), ca = _attn_cols(A)
    view, _, _, nres, nb = _attn_geom(S, d)
    scale = HEAD_DIM ** -0.5
    has_prev = prev is not None

    def body(*refs):
        q_ref, kp_ref, kc_ref, vp_ref, vc_ref, do_ref, lse_ref, dl_ref = refs[:8]
        b = pl.program_id(2)
        qp, kp_, is_prev = _attn_valid(d)
        valid = (is_prev & (kp_ >= qp) & (b > 0)) | (jnp.logical_not(is_prev) & (kp_ <= qp))
        hm, hm2 = _head_masks(), _head_masks(2 * SWA_BLOCK)
        for hp in range(lb // LANES):
            sl = slice(hp * LANES, (hp + 1) * LANES)
            q, do_, lse_, dl_ = (_attn_ld(r, sl) for r in (q_ref, do_ref, lse_ref, dl_ref))
            k2 = jnp.concatenate([_attn_ld(kp_ref, sl), _attn_ld(kc_ref, sl)], axis=0)
            v2 = jnp.concatenate([_attn_ld(vp_ref, sl), _attn_ld(vc_ref, sl)], axis=0).astype(BF16)
            k2b = k2.astype(BF16)
            dq = jnp.zeros((SWA_BLOCK, LANES), F32)
            for hh in range(2):
                doh = _keep(hm[hh], do_)
                lh, dh = _per_head(lse_, hh == 0), _per_head(dl_, hh == 0)
                lh2, dh2 = jnp.concatenate([lh, lh], axis=1), jnp.concatenate([dh, dh], axis=1)
                s = _dot_nt(_keep(hm[hh], q), k2b) * scale
                p = jnp.where(valid, jnp.exp(s - lh2), 0.0)
                ds = p * (_dot_nt(doh, v2) - dh2)
                dq = dq + _dot(ds.astype(BF16), _keep(hm2[hh], k2))
            dq = dq * scale
            if has_prev:
                dq = dq + _attn_ld(refs[8], sl)
            _attn_st(refs[-1], sl, dq)

    qk = _attn_view(qkv, d)
    acts = [_attn_view(t, d) for t in (do, lse, delta)] + ([_attn_view(prev, d)] if has_prev else [])
    sp = functools.partial(_attn_spec, S, d, lb)
    res = _pcall(
        body, name=name,
        grid=(nres, ncol, nb),
        in_specs=[sp(cq), sp(ck, -1), sp(ck), sp(cv, -1), sp(cv)] + [sp(ca)] * len(acts),
        out_specs=sp(ca),
        out_shape=jax.ShapeDtypeStruct(view + (A,), F32),
        compiler_params=_params("parallel", "parallel", "arbitrary"),
    )(qk, qk, qk, qk, qk, *acts)
    return res.reshape(S, A)


def _attn_dkv_stage(name, qkv, do, lse, delta, d, prev):
    S, A3 = qkv.shape
    A = A3 // 3
    lb, ncol, (cq, ck, cv), ca = _attn_cols(A)
    view, _, _, nres, nb = _attn_geom(S, d)
    scale = HEAD_DIM ** -0.5
    has_prev = prev is not None

    def body(*refs):
        k_ref, v_ref, qc_ref, qn_ref, doc_ref, don_ref, lc_ref, ln_ref, dc_ref, dn_ref = refs[:10]
        j = pl.program_id(2)
        rr = lax.broadcasted_iota(jnp.int32, (2 * SWA_BLOCK, SWA_BLOCK), 0)
        qp = _attn_pos(rr & (SWA_BLOCK - 1), d)
        kp_ = _attn_pos(lax.broadcasted_iota(jnp.int32, (2 * SWA_BLOCK, SWA_BLOCK), 1), d)
        valid = ((rr < SWA_BLOCK) & (kp_ <= qp)) | ((rr >= SWA_BLOCK) & (kp_ >= qp) & (j < nb - 1))
        hm2 = _head_masks(2 * SWA_BLOCK)
        for hp in range(lb // LANES):
            sl = slice(hp * LANES, (hp + 1) * LANES)
            kb, vb = _attn_ld(k_ref, sl).astype(BF16), _attn_ld(v_ref, sl).astype(BF16)
            q2 = jnp.concatenate([_attn_ld(qc_ref, sl), _attn_ld(qn_ref, sl)], axis=0)
            do2 = jnp.concatenate([_attn_ld(doc_ref, sl), _attn_ld(don_ref, sl)], axis=0)
            l2 = jnp.concatenate([_attn_ld(lc_ref, sl), _attn_ld(ln_ref, sl)], axis=0)
            d2 = jnp.concatenate([_attn_ld(dc_ref, sl), _attn_ld(dn_ref, sl)], axis=0)
            dk = jnp.zeros((SWA_BLOCK, LANES), F32)
            dv = jnp.zeros((SWA_BLOCK, LANES), F32)
            for hh in range(2):
                qh, doh = _keep(hm2[hh], q2), _keep(hm2[hh], do2)
                lh, dh = _per_head(l2, hh == 0), _per_head(d2, hh == 0)
                s = _dot_nt(qh, kb) * scale
                p = jnp.where(valid, jnp.exp(s - lh), 0.0)
                dv = dv + _dot_tn(p.astype(BF16), doh)
                ds = p * (_dot_nt(doh, vb) - dh)
                dk = dk + _dot_tn(ds.astype(BF16), qh)
            dk = dk * scale
            if has_prev:
                dk = dk + _attn_ld(refs[10], sl)
                dv = dv + _attn_ld(refs[11], sl)
            _attn_st(refs[-2], sl, dk)
            _attn_st(refs[-1], sl, dv)

    qk = _attn_view(qkv, d)
    acts = [_attn_view(t, d) for t in (do, lse, delta)]
    prev_v = [_attn_view(t, d) for t in prev] if has_prev else []
    sp = functools.partial(_attn_spec, S, d, lb)
    res = _pcall(
        body, name=name,
        grid=(nres, ncol, nb),
        in_specs=[sp(ck), sp(cv), sp(cq), sp(cq, 1), sp(ca), sp(ca, 1), sp(ca), sp(ca, 1), sp(ca), sp(ca, 1)]
        + [sp(ca)] * len(prev_v),
        out_specs=[sp(ca), sp(ca)],
        out_shape=[jax.ShapeDtypeStruct(view + (A,), F32)] * 2,
        compiler_params=_params("parallel", "parallel", "arbitrary"),
    )(qk, qk, qk, qk, acts[0], acts[0], acts[1], acts[1], acts[2], acts[2], *prev_v)
    return [t.reshape(S, A) for t in res]


def _attn_delta(dya, ya):
    S, A = ya.shape
    ri = lax.broadcasted_iota(jnp.int32, (A, A), 0) // HEAD_DIM
    ci = lax.broadcasted_iota(jnp.int32, (A, A), 1) // HEAD_DIM
    ones_bd = (ri == ci).astype(BF16)

    def fn(ins, ps):
        prod = ins[0] * ins[1]
        hi = prod.astype(BF16)
        lo = (prod - hi.astype(F32)).astype(BF16)
        return [_dot(hi, ps[0]) + _dot(lo, ps[0])], []

    return _rowwise("attn_delta", fn, [dya, ya], [ones_bd], [(A, F32)])[0]


def _attn_bwd(qkv, dya, ya, lse):
    delta = _attn_delta(dya, ya)
    dq, dkv = None, None
    for d in DILATIONS:
        dq = _attn_dq_stage("attn_dq_d%d" % d, qkv, dya, lse, delta, d, dq)
        dkv = _attn_dkv_stage("attn_dkv_d%d" % d, qkv, dya, lse, delta, d, dkv)
    return dq, dkv[0], dkv[1]


def _ssm_perm(a, T):
    S, w = a.shape
    return a.reshape(S // T, SUBLANES, T // SUBLANES, w).transpose(0, 2, 1, 3).reshape(S, w)


def _ssm_unperm(a, T):
    S, w = a.shape
    return a.reshape(S // T, T // SUBLANES, SUBLANES, w).transpose(0, 2, 1, 3).reshape(S, w)


def _ssm_powers(lam_ref, pw_ref, T, ns):
    n = (lax.broadcasted_iota(jnp.int32, (T, 1), 0) // SUBLANES + 1).astype(F32)
    mag = jnp.exp(n * lam_ref[0, 0:1, :])
    ang = n * lam_ref[0, 1:2, :]
    pw_ref[:, 0:ns] = mag * jnp.cos(ang)
    pw_ref[:, ns:2 * ns] = mag * jnp.sin(ang)


def _ssm_scan(xs, off, pw_ref, carry_ref, T, ns, reverse):
    Tc = T // SUBLANES
    sgn = -1.0 if reverse else 1.0
    ar, ai = pw_ref[0:SUBLANES, 0:ns], sgn * pw_ref[0:SUBLANES, ns:2 * ns]

    def rows(i):
        return pl.ds(pl.multiple_of(off + i * SUBLANES, SUBLANES), SUBLANES)

    def step(k, h):
        hr, hi = h
        r = rows(Tc - 1 - k if reverse else k)
        nr = ar * hr - ai * hi + xs[r, 0:ns]
        ni = ar * hi + ai * hr + xs[r, ns:2 * ns]
        xs[r, 0:ns] = nr
        xs[r, ns:2 * ns] = ni
        return nr, ni

    z = jnp.zeros((SUBLANES, ns), F32)
    er, ei = lax.fori_loop(0, Tc, step, (z, z))
    atr, ati = pw_ref[T - SUBLANES:T, 0:ns], sgn * pw_ref[T - SUBLANES:T, ns:2 * ns]
    rowid = lax.broadcasted_iota(jnp.int32, (SUBLANES, ns), 0)
    cr, ci = carry_ref[:, 0:ns], carry_ref[:, ns:2 * ns]
    ctr, cti = z, z
    for jj in range(SUBLANES):
        j = SUBLANES - 1 - jj if reverse else jj
        sel = rowid == j
        ctr, cti = jnp.where(sel, cr, ctr), jnp.where(sel, ci, cti)
        ejr = jnp.broadcast_to(jnp.sum(jnp.where(sel, er, 0.0), axis=0, keepdims=True), (SUBLANES, ns))
        eji = jnp.broadcast_to(jnp.sum(jnp.where(sel, ei, 0.0), axis=0, keepdims=True), (SUBLANES, ns))
        cr, ci = ejr + atr * cr - ati * ci, eji + atr * ci + ati * cr
    carry_ref[:, 0:ns] = cr
    carry_ref[:, ns:2 * ns] = ci

    def fix(i, _):
        r = rows(i)
        pr_rows = pl.ds(pl.multiple_of((Tc - 1 - i if reverse else i) * SUBLANES, SUBLANES), SUBLANES)
        pr, pi = pw_ref[pr_rows, 0:ns], sgn * pw_ref[pr_rows, ns:2 * ns]
        xs[r, 0:ns] += pr * ctr - pi * cti
        xs[r, ns:2 * ns] += pr * cti + pi * ctr
        return 0

    lax.fori_loop(0, Tc, fix, 0)
    return ctr, cti


def _ssm_fwd(ufp, bb, cc, lam_dt, drow, T):
    S, W = ufp.shape
    GB, cw, ns2 = bb.shape
    ns = ns2 // 2
    NCH = S // T

    def body(uf_ref, bb_ref, cc_ref, lam_ref, d_ref, y_ref, hs_ref, xs, pw, carry):
        @pl.when(pl.program_id(1) == 0)
        def _():
            _ssm_powers(lam_ref, pw, T, ns)
            carry[...] = jnp.zeros(carry.shape, F32)

        uf = uf_ref[...]
        xs[...] = _dot(uf.astype(BF16), bb_ref[0])
        hs_ref[0, 0] = carry[...]
        _ssm_scan(xs, 0, pw, carry, T, ns, reverse=False)
        y_ref[...] = _dot(xs[...].astype(BF16), cc_ref[0]) + d_ref[...] * uf

    return _pcall(
        body, name="ssm_fwd",
        grid=(GB, NCH),
        in_specs=[pl.BlockSpec((T, cw), lambda g, c: (c, g)),
                  pl.BlockSpec((1, cw, ns2), lambda g, c: (g, 0, 0)),
                  pl.BlockSpec((1, ns2, cw), lambda g, c: (g, 0, 0)),
                  pl.BlockSpec((1, 2, ns), lambda g, c: (g, 0, 0)),
                  pl.BlockSpec((1, cw), lambda g, c: (0, g))],
        out_specs=[pl.BlockSpec((T, cw), lambda g, c: (c, g)),
                   pl.BlockSpec((1, 1, SUBLANES, ns2), lambda g, c: (g, c, 0, 0))],
        out_shape=[jax.ShapeDtypeStruct((S, W), F32),
                   jax.ShapeDtypeStruct((GB, NCH, SUBLANES, ns2), F32)],
        scratch_shapes=[pltpu.VMEM((T, ns2), F32), pltpu.VMEM((T, ns2), F32), pltpu.VMEM((SUBLANES, ns2), F32)],
        compiler_params=_params("arbitrary", "arbitrary"),
    )(ufp, bb, cc, lam_dt, drow)


def _ssm_bwd(ufp, dyp, bb, bbt, cc, cct, lam_dt, drow, hstart, T):
    S, W = ufp.shape
    GB, cw, ns2 = bb.shape
    ns = ns2 // 2
    NCH = S // T

    def body(uf_ref, dy_ref, bb_ref, bbt_ref, cc_ref, cct_ref, lam_ref, d_ref, hs_ref,
             duf_ref, dbb_ref, dcc_ref, da_ref, dd_ref, hb, ls, pw, carry_f, carry_b):
        @pl.when(pl.program_id(1) == 0)
        def _():
            _ssm_powers(lam_ref, pw, T, ns)
            carry_b[...] = jnp.zeros(carry_b.shape, F32)
            dbb_ref[...] = jnp.zeros(dbb_ref.shape, F32)
            dcc_ref[...] = jnp.zeros(dcc_ref.shape, F32)
            da_ref[...] = jnp.zeros(da_ref.shape, F32)
            dd_ref[...] = jnp.zeros(dd_ref.shape, F32)

        uf, dy = uf_ref[...], dy_ref[...]
        ufb, dyb = uf.astype(BF16), dy.astype(BF16)
        hb[SUBLANES:T + SUBLANES, :] = _dot(ufb, bb_ref[0])
        carry_f[...] = hs_ref[0, 0]
        ctr, cti = _ssm_scan(hb, SUBLANES, pw, carry_f, T, ns, reverse=False)
        hb[0:SUBLANES, 0:ns] = ctr
        hb[0:SUBLANES, ns:ns2] = cti
        ls[...] = _dot(dyb, cct_ref[0])
        _ssm_scan(ls, 0, pw, carry_b, T, ns, reverse=True)
        lb = ls[...].astype(BF16)
        dbb_ref[0] += _dot_tn(ufb, lb)
        dcc_ref[0] += _dot_tn(hb[SUBLANES:T + SUBLANES, :].astype(BF16), dyb)

        def da_step(i, acc):
            r = pl.ds(pl.multiple_of(i * SUBLANES, SUBLANES), SUBLANES)
            lr, li, hpr, hpi = ls[r, 0:ns], ls[r, ns:ns2], hb[r, 0:ns], hb[r, ns:ns2]
            return acc[0] + lr * hpr + li * hpi, acc[1] + li * hpr - lr * hpi

        z = jnp.zeros((SUBLANES, ns), F32)
        dar, dai = lax.fori_loop(0, T // SUBLANES, da_step, (z, z))
        da_ref[0, :, 0:ns] += dar
        da_ref[0, :, ns:ns2] += dai
        duf_ref[...] = _dot(lb, bbt_ref[0]) + d_ref[...] * dy
        dd_ref[...] += jnp.sum(dy * uf, axis=0, keepdims=True)

    rc = lambda c: NCH - 1 - c
    return _pcall(
        body, name="ssm_bwd",
        grid=(GB, NCH),
        in_specs=[pl.BlockSpec((T, cw), lambda g, c: (rc(c), g)),
                  pl.BlockSpec((T, cw), lambda g, c: (rc(c), g)),
                  pl.BlockSpec((1, cw, ns2), lambda g, c: (g, 0, 0)),
                  pl.BlockSpec((1, ns2, cw), lambda g, c: (g, 0, 0)),
                  pl.BlockSpec((1, ns2, cw), lambda g, c: (g, 0, 0)),
                  pl.BlockSpec((1, cw, ns2), lambda g, c: (g, 0, 0)),
                  pl.BlockSpec((1, 2, ns), lambda g, c: (g, 0, 0)),
                  pl.BlockSpec((1, cw), lambda g, c: (0, g)),
                  pl.BlockSpec((1, 1, SUBLANES, ns2), lambda g, c: (g, rc(c), 0, 0))],
        out_specs=[pl.BlockSpec((T, cw), lambda g, c: (rc(c), g)),
                   pl.BlockSpec((1, cw, ns2), lambda g, c: (g, 0, 0)),
                   pl.BlockSpec((1, ns2, cw), lambda g, c: (g, 0, 0)),
                   pl.BlockSpec((1, SUBLANES, ns2), lambda g, c: (g, 0, 0)),
                   pl.BlockSpec((1, cw), lambda g, c: (0, g))],
        out_shape=[jax.ShapeDtypeStruct((S, W), F32),
                   jax.ShapeDtypeStruct((GB, cw, ns2), F32),
                   jax.ShapeDtypeStruct((GB, ns2, cw), F32),
                   jax.ShapeDtypeStruct((GB, SUBLANES, ns2), F32),
                   jax.ShapeDtypeStruct((1, W), F32)],
        scratch_shapes=[pltpu.VMEM((T + SUBLANES, ns2), F32), pltpu.VMEM((T, ns2), F32), pltpu.VMEM((T, ns2), F32),
                        pltpu.VMEM((SUBLANES, ns2), F32), pltpu.VMEM((SUBLANES, ns2), F32)],
        compiler_params=_params("arbitrary", "arbitrary"),
    )(ufp, dyp, bb, bbt, cc, cct, lam_dt, drow, hstart)


def _ssm_disc_math(lr, li, logdt, br, bi):
    dt = jnp.exp(logdt)
    mag = jnp.exp(lr * dt)
    ar = mag * jnp.cos(li * dt)
    ai = mag * jnp.sin(li * dt)
    nr, ni = ar - 1.0, ai
    den = lr * lr + li * li
    cr = (nr * lr + ni * li) / den
    ci = (ni * lr - nr * li) / den
    return ar, ai, cr * br - ci * bi, cr * bi + ci * br


def _ssm_disc(lr, li, logdt, br, bi):
    C = br.shape[1]

    def fn(ins, ps):
        _, _, bbr, bbi = _ssm_disc_math(*ins)
        dt = jnp.exp(ins[2])
        return [ins[0] * dt, ins[1] * dt, bbr, bbi], []

    return _rowwise("ssm_disc", fn, [lr, li, logdt, br, bi], [], [(1, F32), (1, F32), (C, F32), (C, F32)], ts=512)


def _ssm_disc_bwd(lr, li, logdt, br, bi, dar, dai, dbbr, dbbi):
    C = br.shape[1]

    def fn(ins, ps):
        _, vjp = jax.vjp(_ssm_disc_math, *ins[:5])
        return list(vjp(tuple(ins[5:]))), []

    return _rowwise("ssm_disc_bwd", fn, [lr, li, logdt, br, bi, dar, dai, dbbr, dbbi], [],
                    [(1, F32), (1, F32), (1, F32), (C, F32), (C, F32)], ts=512)


def _block_diag(t):
    GB, g, a, b = t.shape
    eye = jnp.eye(g, dtype=t.dtype)
    return (t[:, :, :, None, :] * eye[None, :, None, :, None]).reshape(GB, g * a, g * b)


def _block_diag_take(t, g):
    GB, ga, gb_ = t.shape
    a, b = ga // g, gb_ // g
    eye = jnp.eye(g, dtype=t.dtype)
    return (t.reshape(GB, g, a, g, b) * eye[None, :, None, :, None]).sum(axis=3)


def _loss_head(h4, tgt, gf):
    D = h4.shape[1]

    def fn(ins, ps):
        x, t = ins
        xh, r = _xhat(x)
        err = xh * ps[0] - t
        dn = err * (1.0 / D)
        dxh = dn * ps[0]
        dx = r * (dxh - xh * jnp.mean(dxh * xh, axis=-1, keepdims=True))
        return [dx], [jnp.sum(err * err, axis=0, keepdims=True), jnp.sum(dn * xh, axis=0, keepdims=True)]

    return _rowwise("loss_head", fn, [h4, tgt], [gf], [(D, F32)], accs=[D, D])


def _gelu(x):
    return 0.5 * x * (1.0 + jnp.tanh(GELU_C * (x + GELU_K * x * x * x)))


def _gelu_grad(x):
    t = jnp.tanh(GELU_C * (x + GELU_K * x * x * x))
    return 0.5 * (1.0 + t) + 0.5 * x * (1.0 - t * t) * GELU_C * (1.0 + 3.0 * GELU_K * x * x)


def _mesh_pos():
    return lax.axis_index("x"), lax.axis_index("y"), lax.axis_index("c")


def _other_chips(x, y):
    return [(1 - x, y), (x, 1 - y), (1 - x, 1 - y)]


def _remote(src, dst, send, recv, dev):
    return pltpu.make_async_remote_copy(src_ref=src, dst_ref=dst, send_sem=send, recv_sem=recv,
                                        device_id=dev, device_id_type=MESH)


ANY = pl.BlockSpec(memory_space=pl.ANY)


def _shard_region(ref, axis, shape, s, half=None):
    R, Cc = shape
    hr = R // 2
    if axis == 1:
        rows = pl.ds(0, R) if half is None else pl.ds(pl.multiple_of(half * hr, 16), hr)
        return ref.at[rows, pl.ds(pl.multiple_of(s * Cc, LANES), Cc)]
    if half is None:
        return ref.at[pl.ds(pl.multiple_of(s * R, 16), R), :]
    return ref.at[pl.ds(pl.multiple_of(s * R + half * hr, 16), hr), :]


def _all_gather_weights(shards, axes):
    nw = len(shards)
    shapes = [s.shape for s in shards]
    full = [((R, N_CHIPS * Cc) if ax == 1 else (N_CHIPS * R, Cc)) for (R, Cc), ax in zip(shapes, axes)]

    def body(*refs):
        sh, out = refs[:nw], refs[nw:2 * nw]
        send, recv, loc = refs[2 * nw:]
        x, y, c = _mesh_pos()
        me, sib = (x, y, c), (x, y, 1 - c)
        chips = _other_chips(x, y)
        sid = [2 * cx + cy for cx, cy in chips]
        my_s = 2 * x + y

        def reg(w, s, half=None):
            return _shard_region(out[w], axes[w], shapes[w], s, half)

        local = [pltpu.make_async_copy(sh[w], reg(w, my_s), loc.at[w]) for w in range(nw)]
        for cp in local:
            cp.start()
        sends = []
        for w in range(nw):
            hr = shapes[w][0] // 2
            src = sh[w].at[pl.ds(pl.multiple_of(c * hr, 16), hr), :]
            for j, chip in enumerate(chips):
                cp = _remote(src, reg(w, my_s, c), send.at[6 * w + j], recv.at[6 * w + j], (*chip, c))
                cp.start()
                sends.append(cp)
        for w in range(nw):
            for j in range(3):
                r_ = reg(w, sid[j], c)
                _remote(r_, r_, send.at[6 * w + j], recv.at[6 * w + j], me).wait_recv()
                cp = _remote(r_, r_, send.at[6 * w + 3 + j], recv.at[6 * w + 3 + j], sib)
                cp.start()
                sends.append(cp)
        for w in range(nw):
            for j in range(3):
                r_ = reg(w, sid[j], 1 - c)
                _remote(r_, r_, send.at[6 * w + 3 + j], recv.at[6 * w + 3 + j], me).wait_recv()
        for cp in sends:
            cp.wait_send()
        for cp in local:
            cp.wait()

    return _pcall(
        body, name="ag_weights",
        in_specs=[ANY] * nw, out_specs=[ANY] * nw,
        out_shape=[jax.ShapeDtypeStruct(f, BF16) for f in full],
        scratch_shapes=[pltpu.SemaphoreType.DMA((6 * nw,)), pltpu.SemaphoreType.DMA((6 * nw,)),
                        pltpu.SemaphoreType.DMA((nw,))],
    )(*shards)


def _push_pair(name, src, tr, nblk, src_block, out_rows, dst_block, local):
    cw = src.shape[1]
    c_arr = lax.axis_index("c").astype(jnp.int32).reshape(1)

    def body(c_ref, src_ref, out_ref, send, recv, lsem):
        i = pl.program_id(0)
        x, y, c = _mesh_pos()
        dst = out_ref.at[pl.ds(pl.multiple_of(dst_block(i, c_ref[0]) * tr, 16), tr), :]
        cp = _remote(src_ref, dst, send, recv, (x, y, 1 - c))
        cp.start()
        if local:
            lc = pltpu.make_async_copy(src_ref, dst, lsem)
            lc.start()
            lc.wait()
        cp.wait_send()

        @pl.when(i == nblk - 1)
        def _():
            got = out_ref.at[pl.ds(0, nblk * tr), :]
            _remote(got, got, send, recv, (x, y, c)).wait_recv()

    return _pcall(
        body, name=name,
        grid_spec=pltpu.PrefetchScalarGridSpec(
            num_scalar_prefetch=1, grid=(nblk,),
            in_specs=[pl.BlockSpec((tr, cw), lambda i, c_ref: (src_block(i, c_ref[0]), 0))],
            out_specs=ANY,
            scratch_shapes=[pltpu.SemaphoreType.DMA, pltpu.SemaphoreType.DMA, pltpu.SemaphoreType.DMA]),
        out_shape=jax.ShapeDtypeStruct((out_rows, cw), src.dtype),
        compiler_params=_params("arbitrary"),
    )(c_arr, src)


def _sum_half(name, g, theirs, tr, nblk, src_block):
    cw = g.shape[1]
    c_arr = lax.axis_index("c").astype(jnp.int32).reshape(1)

    def body(c_ref, g_ref, t_ref, o_ref):
        o_ref[...] = (g_ref[...].astype(F32) + t_ref[...].astype(F32)).astype(BF16)

    return _pcall(
        body, name=name,
        grid_spec=pltpu.PrefetchScalarGridSpec(
            num_scalar_prefetch=1, grid=(nblk,),
            in_specs=[pl.BlockSpec((tr, cw), lambda i, c_ref: (src_block(i, c_ref[0]), 0)),
                      pl.BlockSpec((tr, cw), lambda i, c_ref: (i, 0))],
            out_specs=pl.BlockSpec((tr, cw), lambda i, c_ref: (i, 0))),
        out_shape=jax.ShapeDtypeStruct((nblk * tr, cw), BF16),
        compiler_params=_params("arbitrary"),
    )(c_arr, g, theirs)


def _rs_chips(qs, axes):
    nw = len(qs)
    pshapes = [((q.shape[0], q.shape[1] // N_CHIPS) if ax == 1 else q.shape[1:]) for q, ax in zip(qs, axes)]

    def body(*refs):
        q, own, got = refs[:nw], refs[nw:2 * nw], refs[2 * nw:3 * nw]
        send, recv, loc = refs[3 * nw:]
        x, y, c = _mesh_pos()
        chips = _other_chips(x, y)

        def part(w, s):
            if axes[w] == 1:
                cw = pshapes[w][1]
                return q[w].at[:, pl.ds(pl.multiple_of(s * cw, LANES), cw)]
            return q[w].at[s]

        cps = []
        for w in range(nw):
            lc = pltpu.make_async_copy(part(w, 2 * x + y), own[w], loc.at[w])
            lc.start()
            cps.append(lc)
            for j, (cx, cy) in enumerate(chips):
                rc = _remote(part(w, 2 * cx + cy), got[w].at[j], send.at[3 * w + j], recv.at[3 * w + j], (cx, cy, c))
                rc.start()
                cps.append(rc)
        for cp in cps:
            cp.wait()

    res = _pcall(
        body, name="rs_chips",
        in_specs=[ANY] * nw, out_specs=[ANY] * (2 * nw),
        out_shape=[jax.ShapeDtypeStruct(p, BF16) for p in pshapes]
        + [jax.ShapeDtypeStruct((3,) + tuple(p), BF16) for p in pshapes],
        scratch_shapes=[pltpu.SemaphoreType.DMA((3 * nw,)), pltpu.SemaphoreType.DMA((3 * nw,)), pltpu.SemaphoreType.DMA((nw,))],
    )(*qs)
    return res[:nw], res[nw:]


def _sum_tiles(name, arrs, out_dtype):
    R, Cc = arrs[0].shape
    tr = _tile(R, max(16, (1 << 20) // Cc // 16 * 16), 16)

    def body(*refs):
        acc = refs[0][...].astype(F32)
        for r in refs[1:-1]:
            acc = acc + r[...].astype(F32)
        refs[-1][...] = acc.astype(out_dtype)

    return _pcall(
        body, name=name, grid=(R // tr,),
        in_specs=[pl.BlockSpec((tr, Cc), lambda i: (i, 0))] * len(arrs),
        out_specs=pl.BlockSpec((tr, Cc), lambda i: (i, 0)),
        out_shape=jax.ShapeDtypeStruct((R, Cc), out_dtype),
        compiler_params=_params("parallel"),
    )(*arrs)


def _reduce_scatter(grads, axes):
    block_bytes = 3 << 19
    qs, hrs = [], []
    for w, (g, ax) in enumerate(zip(grads, axes)):
        rows, cw = g.shape
        hr = rows // 2 if ax == 1 else rows // N_CHIPS // 2
        tr = _tile(hr, max(16, block_bytes // (cw * 2) // 16 * 16), 16)
        nth = hr // tr
        if ax == 1:
            nblk, blk = nth, (lambda i, half, nth=nth: half * nth + i)
        else:
            nblk, blk = N_CHIPS * nth, (lambda i, half, nth=nth: (i // nth) * (2 * nth) + half * nth + i % nth)
        theirs = _push_pair("rs_pair%d" % w, g, tr, nblk, lambda i, c, blk=blk: blk(i, 1 - c), nblk * tr,
                            lambda i, c: i, local=False)
        q = _sum_half("rs_sum_pair%d" % w, g, theirs, tr, nblk, blk)
        qs.append(q if ax == 1 else q.reshape(N_CHIPS, hr, cw))
        hrs.append(hr)
    own, got = _rs_chips(qs, axes)
    outs = []
    for w, (o, g_, hr) in enumerate(zip(own, got, hrs)):
        half = _sum_tiles("rs_sum_chips%d" % w, [o, g_[0], g_[1], g_[2]], F32)
        tr = _tile(hr, max(16, block_bytes // (half.shape[1] * 4) // 16 * 16), 16)
        nth = hr // tr
        outs.append(_push_pair("rs_swap%d" % w, half, tr, nth, lambda i, c: i, 2 * hr,
                               lambda i, c, nth=nth: c * nth + i, local=True))
    return outs


def _all_reduce_small(v):
    n = v.shape[0]

    def body(v_ref, out_ref, buf, send, recv):
        x, y, c = _mesh_pos()
        my = 4 * x + 2 * y + c
        buf[my] = v_ref[...]
        cps = []
        for k in range(1, N_DEV):
            fx, fy, fc = (k >> 2) & 1, (k >> 1) & 1, k & 1
            peer = (1 - x if fx else x, 1 - y if fy else y, 1 - c if fc else c)
            cp = _remote(v_ref, buf.at[my], send.at[k - 1], recv.at[k - 1], peer)
            cp.start()
            cps.append((cp, 4 * peer[0] + 2 * peer[1] + peer[2]))
        for k, (cp, pid) in enumerate(cps):
            _remote(v_ref, buf.at[pid], send.at[k], recv.at[k], (x, y, c)).wait_recv()
        acc = buf[0]
        for i in range(1, N_DEV):
            acc = acc + buf[i]
        out_ref[...] = acc
        for cp, _ in cps:
            cp.wait_send()

    return _pcall(
        body, name="ar_small",
        in_specs=[pl.BlockSpec(memory_space=pltpu.VMEM)], out_specs=pl.BlockSpec(memory_space=pltpu.VMEM),
        out_shape=jax.ShapeDtypeStruct((n, LANES), F32),
        scratch_shapes=[pltpu.VMEM((N_DEV, n, LANES), F32), pltpu.SemaphoreType.DMA((N_DEV - 1,)),
                        pltpu.SemaphoreType.DMA((N_DEV - 1,))],
        compiler_params=pltpu.CompilerParams(vmem_limit_bytes=V7X_VMEM_LIMIT_BYTES),
    )(v)


def _adamw(name, w, g, m, v):
    R, Cc = w.shape
    tr = _tile(R, max(8, (1 << 19) // Cc // 8 * 8), 8)
    c1 = 1.0 - ADAM_B1 ** ADAM_STEP
    c2 = 1.0 - ADAM_B2 ** ADAM_STEP

    def body(w_ref, g_ref, m_ref, v_ref, d_ref, nm_ref, nv_ref):
        g_ = g_ref[...]
        nm = ADAM_B1 * m_ref[...] + (1.0 - ADAM_B1) * g_
        nv = ADAM_B2 * v_ref[...] + (1.0 - ADAM_B2) * (g_ * g_)
        d_ref[...] = -ADAM_LR * ((nm / c1) / (jnp.sqrt(nv / c2) + ADAM_EPS) + ADAM_WD * w_ref[...])
        nm_ref[...] = nm
        nv_ref[...] = nv

    spec = pl.BlockSpec((tr, Cc), lambda i: (i, 0))
    return _pcall(
        body, name=name, grid=(R // tr,),
        in_specs=[spec] * 4, out_specs=[spec] * 3,
        out_shape=[jax.ShapeDtypeStruct((R, Cc), F32)] * 3,
        compiler_params=_params("parallel"),
    )(w, g, m, v)


def _pack(arrs, rows):
    flat = jnp.concatenate([a.reshape(-1) for a in arrs])
    return jnp.pad(flat, (0, rows * LANES - flat.shape[0])).reshape(rows, LANES)


def _unpack(packed, like):
    flat, out, o = packed.reshape(-1), [], 0
    for a in like:
        out.append(flat[o:o + a.size].reshape(a.shape))
        o += a.size
    return out


BIG = (
    ("ffn1_w_gate", 1), ("ffn1_w_up", 1), ("ffn1_w_down", 0), ("w_in", 1), ("ssm_w_glu", 0), ("w_out", 0),
    ("ffn2_w_gate", 1), ("ffn2_w_up", 1), ("ffn2_w_down", 0), ("ple_w_gate", 0), ("ple_w_proj", 1),
)
SMALL = ("ffn1_norm", "mix_norm", "attn_out_norm", "ssm_lambda_re", "ssm_lambda_im", "ssm_log_dt", "ssm_b_re", "ssm_b_im",
         "ssm_c_re", "ssm_c_im", "ssm_d", "ssm_b_glu", "ssm_out_norm", "ffn2_norm", "ple_norm", "final_norm")
WEIGHTS = ("ffn1_norm", "ffn1_w_gate", "ffn1_w_up", "ffn1_w_down", "mix_norm", "w_in", "attn_out_norm", "ssm_lambda_re",
           "ssm_lambda_im", "ssm_log_dt", "ssm_b_re", "ssm_b_im", "ssm_c_re", "ssm_c_im", "ssm_d", "ssm_w_glu", "ssm_b_glu",
           "ssm_out_norm", "w_out", "ffn2_norm", "ffn2_w_gate", "ffn2_w_up", "ffn2_w_down", "ple_norm", "ple_w_gate",
           "ple_w_proj", "final_norm")


def _pad_to(a, axis, n):
    pad = [(0, 0), (0, 0)]
    pad[axis] = (0, n - a.shape[axis])
    return jnp.pad(a, pad)


def _local_step(x, p, tgt, w, full):
    S, D = x.shape
    A = w["attn_out_norm"].shape[-1]
    W = w["ssm_d"].shape[-1]
    G, P = w["ssm_lambda_re"].shape[-2:]
    C = w["ssm_b_re"].shape[-1]
    GB = G // SSM_BLOCK_GROUPS
    T = min(1024, S)
    row = lambda name: w[name].reshape(1, -1)
    gs = {}

    h1, ffn1_saved = _ffn_fwd("ffn1", x, row("ffn1_norm"), full["ffn1_w_gate"], full["ffn1_w_up"], full["ffn1_w_down"])
    n2 = _rms_fwd("mix_norm", h1, row("mix_norm"))
    w_in = full["w_in"]
    n2p = _to_attn_order(n2)
    (qkv,) = _mm("w_in_qkv", [n2p], [w_in[:, :3 * A]], [F32], tm=1024, tn=1024)
    (s_in,) = _mm("w_in_ssm", [n2], [w_in[:, 3 * A:]], [F32], tm=1024, tn=1024)
    ya, lse = _attn_fwd(qkv)

    col = lambda name: w[name].reshape(G * P, 1)
    logdt_x = jnp.repeat(w["ssm_log_dt"].reshape(G), P).reshape(G * P, 1)
    b_re, b_im = w["ssm_b_re"].reshape(G * P, C), w["ssm_b_im"].reshape(G * P, C)
    lrdt, lidt, bbr, bbi = _ssm_disc(col("ssm_lambda_re"), col("ssm_lambda_im"), logdt_x, b_re, b_im)
    gsz = SSM_BLOCK_GROUPS
    to_bb = lambda t: _block_diag(t.reshape(GB, gsz, P, C).transpose(0, 1, 3, 2))
    bb = jnp.concatenate([to_bb(bbr), to_bb(bbi)], axis=2).astype(BF16)
    to_cc = lambda t: _block_diag(t.reshape(GB, gsz, C, P).transpose(0, 1, 3, 2))
    cc = jnp.concatenate([to_cc(w["ssm_c_re"]), -to_cc(w["ssm_c_im"])], axis=1).astype(BF16)
    lam_dt = jnp.stack([lrdt.reshape(GB, gsz * P), lidt.reshape(GB, gsz * P)], axis=1)
    ufp = _ssm_perm(s_in, T)
    ypre, hstart = _ssm_fwd(ufp, bb, cc, lam_dt, row("ssm_d"), T)

    def glu_in(ins, ps):
        yg = _gelu(ins[0])
        return [yg, yg], []

    yg, ygb = _rowwise("ssm_gelu", glu_in, [ypre], [], [(W, F32), (W, BF16)])
    w_glu = full["ssm_w_glu"]

    def glu_out(accs, ex):
        gl = accs[0] + ex[1]
        return [ex[0] * _sigmoid(gl), gl]

    ybp, gl = _mm("ssm_glu", [ygb], [w_glu], [F32, F32], extras=[(yg, "mn"), (row("ssm_b_glu"), "n")],
                  epilogue=glu_out, tm=1024, tn=1024, n_split=4)
    yb = _ssm_unperm(ybp, T)
    na = _from_attn_order(_rms_fwd("attn_out_norm", ya, row("attn_out_norm")))
    nb = _rms_fwd("ssm_out_norm", yb, row("ssm_out_norm"))
    w_out = full["w_out"]
    (h2,) = _mm("w_out", [na, nb], [w_out[:A], w_out[A:]], [F32], pairs=((0, 0, 0), (1, 1, 0)), extras=[(h1, "mn")],
                epilogue=lambda accs, ex: [ex[0] + accs[0]], tm=1024, tn=1024)
    h3, ffn2_saved = _ffn_fwd("ffn2", h2, row("ffn2_norm"), full["ffn2_w_gate"], full["ffn2_w_up"], full["ffn2_w_down"])
    n4 = _rms_fwd("ple_norm", h3, row("ple_norm"))
    (pe,) = _mm("ple_proj", [p], [full["ple_w_proj"]], [F32], tm=1024, tn=1024)

    def ple_out(accs, ex):
        gate = _sigmoid(accs[0])
        return [ex[1] + gate * ex[0], gate]

    h4, gate = _mm("ple_gate", [n4], [full["ple_w_gate"]], [F32, F32], extras=[(pe, "mn"), (h3, "mn")],
                   epilogue=ple_out, tm=1024, tn=1024, n_split=4)

    dh4, err2, gs["final_norm"] = _loss_head(h4, tgt, row("final_norm"))
    loss = (0.5 / D) * jnp.sum(err2)

    def ple_bwd(ins, ps):
        dh, gt, pe_ = ins
        return [dh * gt, dh * pe_ * gt * (1.0 - gt)], []

    dpe, dpg = _rowwise("ple_bwd", ple_bwd, [dh4, gate, pe], [], [(D, BF16), (D, BF16)])
    (d_ple_proj,) = _mm("ple_dproj", [p], [dpe], [BF16], ta=True, tm=256, tn=2048, tk=1024)
    (d_ple_gate,) = _mm("ple_dgate", [n4], [dpg], [BF16], ta=True, tm=1024, tn=1024, tk=2048)
    (dn4,) = _mm("ple_dn", [dpg], [full["ple_w_gate"]], [F32], tb=True, tm=1024, tn=1024)
    (dh3, dh3b), gs["ple_norm"] = _rms_bwd("ple_dnorm", dn4, h3, row("ple_norm"), dres=dh4, copy_scale=0.5)
    (dh2, dh2b), gs["ffn2_norm"], d_ffn2_g, d_ffn2_u, d_ffn2_d = _ffn_bwd(
        "ffn2", dh3, dh3b, h2, row("ffn2_norm"), full["ffn2_w_gate"], full["ffn2_w_up"], full["ffn2_w_down"],
        ffn2_saved, copy_scale=1.0)
    (dna,) = _mm("w_out_dna", [_to_attn_order(dh2b)], [w_out[:A]], [F32], tb=True, tm=1024, tn=1024)
    (dnb,) = _mm("w_out_dnb", [dh2b], [w_out[A:]], [F32], tb=True, tm=1024, tn=1024)
    (d_wout_a,) = _mm("w_out_dwa", [na], [dh2b], [BF16], ta=True, tm=1024, tn=1024, tk=2048)
    (d_wout_b,) = _mm("w_out_dwb", [nb], [dh2b], [BF16], ta=True, tm=1024, tn=1024, tk=2048)
    d_w_out = jnp.concatenate([d_wout_a, d_wout_b], axis=0)
    (dya,), gs["attn_out_norm"] = _rms_bwd("attn_out_dnorm", dna, ya, row("attn_out_norm"))
    (dyb,), gs["ssm_out_norm"] = _rms_bwd("ssm_out_dnorm", dnb, yb, row("ssm_out_norm"))

    dybp = _ssm_perm(dyb, T)

    def glu_bwd(ins, ps):
        dy, yg_, gl_ = ins
        sg = _sigmoid(gl_)
        dgl = dy * yg_ * sg * (1.0 - sg)
        return [dgl, dy * sg], [jnp.sum(dgl, axis=0, keepdims=True)]

    dgl, dyg_direct, gs["ssm_b_glu"] = _rowwise("ssm_glu_bwd", glu_bwd, [dybp, yg, gl], [], [(W, BF16), (W, F32)], accs=[W])
    (d_w_glu,) = _mm("ssm_dwglu", [ygb], [dgl], [BF16], ta=True, tm=1024, tn=1024, tk=2048)
    (dypre,) = _mm("ssm_dyg", [dgl], [w_glu], [F32], tb=True, extras=[(dyg_direct, "mn"), (ypre, "mn")],
                   epilogue=lambda accs, ex: [(accs[0] + ex[0]) * _gelu_grad(ex[1])], tm=1024, tn=1024, n_split=4)
    dufp, dbb, dcc, da, gs["ssm_d"] = _ssm_bwd(ufp, dypre, bb, bb.transpose(0, 2, 1), cc, cc.transpose(0, 2, 1),
                                               lam_dt, row("ssm_d"), hstart, T)
    ns = gsz * P
    from_bb = lambda t: _block_diag_take(t, gsz).transpose(0, 1, 3, 2).reshape(G * P, C)
    from_cc = lambda t: _block_diag_take(t, gsz).transpose(0, 1, 3, 2).reshape(w["ssm_c_re"].shape)
    gs["ssm_c_re"], gs["ssm_c_im"] = from_cc(dcc[:, :ns]), -from_cc(dcc[:, ns:])
    da = da.sum(axis=1)
    dar, dai = da[:, :ns].reshape(G * P, 1), da[:, ns:].reshape(G * P, 1)
    dlr, dli, dlogdt, dbr, dbi = _ssm_disc_bwd(col("ssm_lambda_re"), col("ssm_lambda_im"), logdt_x, b_re, b_im,
                                               dar, dai, from_bb(dbb[:, :, :ns]), from_bb(dbb[:, :, ns:]))
    gs["ssm_lambda_re"], gs["ssm_lambda_im"] = dlr.reshape(w["ssm_lambda_re"].shape), dli.reshape(w["ssm_lambda_im"].shape)
    gs["ssm_log_dt"] = dlogdt.reshape(G, P).sum(axis=1).reshape(w["ssm_log_dt"].shape)
    gs["ssm_b_re"], gs["ssm_b_im"] = dbr.reshape(w["ssm_b_re"].shape), dbi.reshape(w["ssm_b_im"].shape)
    ds_in = _ssm_unperm(dufp, T)

    dq, dk, dv = _attn_bwd(qkv, dya, ya, lse)
    dqkv = jnp.concatenate([dq, dk, dv], axis=1).astype(BF16)
    (d_w_qkv,) = _mm("w_in_dw_qkv", [n2p], [dqkv], [BF16], ta=True, tm=1024, tn=1024, tk=2048)
    (d_w_s,) = _mm("w_in_dw_ssm", [n2], [ds_in], [BF16], ta=True, tm=1024, tn=1024, tk=2048)
    d_w_in = jnp.concatenate([d_w_qkv, d_w_s], axis=1)
    dz = jnp.concatenate([_from_attn_order(dqkv), ds_in.astype(BF16)], axis=1)
    (dn2,) = _mm("w_in_dn", [dz], [w_in], [F32], tb=True, tm=1024, tn=1024)
    (dh1, dh1b), gs["mix_norm"] = _rms_bwd("mix_dnorm", dn2, h1, row("mix_norm"), dres=dh2, copy_scale=0.5)
    (dx,), gs["ffn1_norm"], d_ffn1_g, d_ffn1_u, d_ffn1_d = _ffn_bwd(
        "ffn1", dh1, dh1b, x, row("ffn1_norm"), full["ffn1_w_gate"], full["ffn1_w_up"], full["ffn1_w_down"],
        ffn1_saved, copy_scale=None)

    big = {"ffn1_w_gate": d_ffn1_g, "ffn1_w_up": d_ffn1_u, "ffn1_w_down": d_ffn1_d, "w_in": d_w_in, "ssm_w_glu": d_w_glu,
           "w_out": d_w_out, "ffn2_w_gate": d_ffn2_g, "ffn2_w_up": d_ffn2_u, "ffn2_w_down": d_ffn2_d,
           "ple_w_gate": d_ple_gate, "ple_w_proj": d_ple_proj}
    small = {k: gs[k].reshape(w[k].shape) for k in SMALL}
    return loss, dx, big, small


def kernel(x, p, ffn1_norm, ffn1_w_gate, ffn1_w_up, ffn1_w_down, mix_norm, w_in, attn_out_norm, ssm_lambda_re, ssm_lambda_im, ssm_log_dt, ssm_b_re, ssm_b_im, ssm_c_re, ssm_c_im, ssm_d, ssm_w_glu, ssm_b_glu, ssm_out_norm, w_out, ffn2_norm, ffn2_w_gate, ffn2_w_up, ffn2_w_down, ple_norm, ple_w_gate, ple_w_proj, final_norm, loss_target, m_ffn1_norm, m_ffn1_w_gate, m_ffn1_w_up, m_ffn1_w_down, m_mix_norm, m_w_in, m_attn_out_norm, m_ssm_lambda_re, m_ssm_lambda_im, m_ssm_log_dt, m_ssm_b_re, m_ssm_b_im, m_ssm_c_re, m_ssm_c_im, m_ssm_d, m_ssm_w_glu, m_ssm_b_glu, m_ssm_out_norm, m_w_out, m_ffn2_norm, m_ffn2_w_gate, m_ffn2_w_up, m_ffn2_w_down, m_ple_norm, m_ple_w_gate, m_ple_w_proj, m_final_norm, v_ffn1_norm, v_ffn1_w_gate, v_ffn1_w_up, v_ffn1_w_down, v_mix_norm, v_w_in, v_attn_out_norm, v_ssm_lambda_re, v_ssm_lambda_im, v_ssm_log_dt, v_ssm_b_re, v_ssm_b_im, v_ssm_c_re, v_ssm_c_im, v_ssm_d, v_ssm_w_glu, v_ssm_b_glu, v_ssm_out_norm, v_w_out, v_ffn2_norm, v_ffn2_w_gate, v_ffn2_w_up, v_ffn2_w_down, v_ple_norm, v_ple_w_gate, v_ple_w_proj, v_final_norm):
    args = locals()
    w = {k: args[k] for k in WEIGHTS}
    m = {k: args["m_" + k] for k in WEIGHTS}
    v = {k: args["v_" + k] for k in WEIGHTS}
    w2 = {k: w[k].reshape(w[k].shape[-2:]) for k, _ in BIG}

    axes = [ax for _, ax in BIG]
    padded = {k: -(-w2[k].shape[ax] // LANES) * LANES for k, ax in BIG}
    shards = [_pad_to(w2[k].astype(BF16), ax, padded[k]) for k, ax in BIG]
    full = dict(zip([k for k, _ in BIG], _all_gather_weights(shards, axes)))

    loss_local, dx, gbig, gsmall = _local_step(x[0], p[0, 0], loss_target[0], w, full)
    loss = lax.psum(loss_local, MESH_AXES)

    summed = _reduce_scatter([gbig[k] for k, _ in BIG], axes)
    n_small = sum(w[k].size for k in SMALL)
    rows = -(-n_small // (SUBLANES * LANES)) * SUBLANES
    gs_sum = _all_reduce_small(_pack([gsmall[k] for k in SMALL], rows))

    grads, delta, new_m, new_v = {}, {}, {}, {}
    for (k, ax), gfull in zip(BIG, summed):
        g2 = lax.slice_in_dim(gfull, 0, w2[k].shape[ax], axis=ax)
        d2, nm2, nv2 = _adamw("adamw_" + k, w2[k], g2, m[k].reshape(w2[k].shape), v[k].reshape(w2[k].shape))
        grads[k], delta[k], new_m[k], new_v[k] = (t.reshape(w[k].shape) for t in (g2, d2, nm2, nv2))
    small_like = [w[k] for k in SMALL]
    ds, nms, nvs = _adamw("adamw_small", _pack(small_like, rows), gs_sum, _pack([m[k] for k in SMALL], rows),
                          _pack([v[k] for k in SMALL], rows))
    for k, g_, d_, nm_, nv_ in zip(SMALL, _unpack(gs_sum, small_like), _unpack(ds, small_like),
                                   _unpack(nms, small_like), _unpack(nvs, small_like)):
        grads[k], delta[k], new_m[k], new_v[k] = g_, d_, nm_, nv_

    return (loss, dx[None], *[grads[k] for k in WEIGHTS], *[delta[k] for k in WEIGHTS],
            *[new_m[k] for k in WEIGHTS], *[new_v[k] for k in WEIGHTS])
```

```python
import functools
import math

import jax
import jax.numpy as jnp
from jax import lax
from jax.experimental import pallas as pl
from jax.experimental.pallas import tpu as pltpu

F32 = jnp.float32
BF16 = jnp.bfloat16
MESH = pl.DeviceIdType.MESH
MESH_AXES = ("x", "y", "c")
N_CHIPS = 4
N_DEV = 8

V7X_VMEM_LIMIT_BYTES = 56 << 20
LANES = 128
SUBLANES = 8

HEAD_DIM = 64
SWA_BLOCK = 128
DILATIONS = (1, 4, 16)
SSM_BLOCK_GROUPS = 8
NORM_EPS = 1e-6
MASK_VALUE = -1e30

ADAM_LR = 0.001
ADAM_B1 = 0.9
ADAM_B2 = 0.999
ADAM_EPS = 1e-08
ADAM_WD = 0.01
ADAM_STEP = 10

GELU_C = math.sqrt(2.0 / math.pi)
GELU_K = 0.044715


def _pcall(body, **kw):
    return pl.pallas_call(body, **kw)


def _params(*sem):
    return pltpu.CompilerParams(dimension_semantics=sem, vmem_limit_bytes=V7X_VMEM_LIMIT_BYTES)


def _tile(n, target, align):
    best = None
    for t in range(align, min(n, target) + 1, align):
        if n % t == 0:
            best = t
    return n if best is None else best


def _sigmoid(x):
    return 1.0 / (1.0 + jnp.exp(-x))


def _mm(name, lhs, rhs, outs, pairs=((0, 0, 0),), epilogue=None, extras=(), ta=False, tb=False,
        tm=1024, tn=512, tk=2048, n_split=1):
    nl, nr, ne, no = len(lhs), len(rhs), len(extras), len(outs)
    n_acc = 1 + max(p[2] for p in pairs)
    (K, M) = lhs[0].shape if ta else lhs[0].shape[::-1]
    (N, K2) = rhs[0].shape if tb else rhs[0].shape[::-1]
    assert K == K2, (name, lhs[0].shape, rhs[0].shape)
    tm, tn, tk = _tile(M, tm, LANES), _tile(N, tn, LANES), _tile(K, tk, LANES)
    nk = K // tk
    if epilogue is None:
        epilogue = lambda accs, ex: accs
    dn = (((0 if ta else 1,), (1 if tb else 0,)), ((), ()))

    def body(*refs):
        l, r = refs[:nl], refs[nl:nl + nr]
        e = refs[nl + nr:nl + nr + ne]
        o = refs[nl + nr + ne:nl + nr + ne + no]
        acc = refs[nl + nr + ne + no:]
        k = pl.program_id(2)
        if nk == 1 and n_split > 1:
            cs = tn // n_split
            for c in range(n_split):
                cols = slice(c * cs, (c + 1) * cs)
                parts = [None] * n_acc
                for li, ri, ai in pairs:
                    rt = r[ri][cols, :] if tb else r[ri][:, cols]
                    d = lax.dot_general(l[li][...].astype(BF16), rt.astype(BF16), dn, preferred_element_type=F32)
                    parts[ai] = d if parts[ai] is None else parts[ai] + d
                ex = [x[...] if kind == "m" else x[:, cols] for x, (_, kind) in zip(e, extras)]
                for ref, val in zip(o, epilogue(parts, ex)):
                    ref[:, cols] = val.astype(ref.dtype)
            return
        parts = [None] * n_acc
        for li, ri, ai in pairs:
            d = lax.dot_general(l[li][...].astype(BF16), r[ri][...].astype(BF16), dn,
                                preferred_element_type=F32)
            parts[ai] = d if parts[ai] is None else parts[ai] + d

        def finish(accs):
            res = epilogue(accs, [x[...] for x in e])
            for ref, val in zip(o, res):
                ref[...] = val.astype(ref.dtype)

        if nk == 1:
            finish(parts)
        else:
            @pl.when(k == 0)
            def _():
                for ai in range(n_acc):
                    acc[ai][...] = parts[ai]

            @pl.when(k > 0)
            def _():
                for ai in range(n_acc):
                    acc[ai][...] += parts[ai]

            @pl.when(k == nk - 1)
            def _():
                finish([a[...] for a in acc])

    lspec = pl.BlockSpec((tk, tm), lambda i, j, k: (k, i)) if ta else pl.BlockSpec((tm, tk), lambda i, j, k: (i, k))
    rspec = pl.BlockSpec((tn, tk), lambda i, j, k: (j, k)) if tb else pl.BlockSpec((tk, tn), lambda i, j, k: (k, j))
    especs = []
    for arr, kind in extras:
        if kind == "mn":
            especs.append(pl.BlockSpec((tm, tn), lambda i, j, k: (i, j)))
        elif kind == "n":
            especs.append(pl.BlockSpec((1, tn), lambda i, j, k: (0, j)))
        else:
            especs.append(pl.BlockSpec((tm, 1), lambda i, j, k: (i, 0)))
    res = _pcall(
        body, name=name,
        grid=(M // tm, N // tn, nk),
        in_specs=[lspec] * nl + [rspec] * nr + especs,
        out_specs=[pl.BlockSpec((tm, tn), lambda i, j, k: (i, j))] * no,
        out_shape=[jax.ShapeDtypeStruct((M, N), dt) for dt in outs],
        scratch_shapes=[pltpu.VMEM((tm, tn), F32)] * (n_acc if nk > 1 else 0),
        compiler_params=_params("parallel", "parallel", "arbitrary"),
    )(*lhs, *rhs, *[a for a, _ in extras])
    return res


def _rowwise(name, fn, ins, params, outs, accs=(), ts=256):
    S = ins[0].shape[0]
    ts = _tile(S, ts, 16)
    ni, npar, no, na = len(ins), len(params), len(outs), len(accs)

    def body(*refs):
        i_refs, p_refs = refs[:ni], refs[ni:ni + npar]
        o_refs = refs[ni + npar:ni + npar + no]
        a_refs = refs[ni + npar + no:]
        res_o, res_a = fn([r[...] for r in i_refs], [r[...] for r in p_refs])
        for ref, val in zip(o_refs, res_o):
            ref[...] = val.astype(ref.dtype)
        if na:
            @pl.when(pl.program_id(0) == 0)
            def _():
                for ref in a_refs:
                    ref[...] = jnp.zeros(ref.shape, F32)

            for ref, val in zip(a_refs, res_a):
                ref[...] += val

    res = _pcall(
        body, name=name,
        grid=(S // ts,),
        in_specs=[pl.BlockSpec((ts, a.shape[1]), lambda i: (i, 0)) for a in ins]
        + [pl.BlockSpec(p.shape, lambda i: (0, 0)) for p in params],
        out_specs=[pl.BlockSpec((ts, w), lambda i: (i, 0)) for w, _ in outs]
        + [pl.BlockSpec((1, w), lambda i: (0, 0)) for w in accs],
        out_shape=[jax.ShapeDtypeStruct((S, w), dt) for w, dt in outs]
        + [jax.ShapeDtypeStruct((1, w), F32) for w in accs],
        compiler_params=_params("arbitrary"),
    )(*ins, *params)
    return res


def _xhat(x):
    r = lax.rsqrt(jnp.mean(x * x, axis=-1, keepdims=True) + NORM_EPS)
    return x * r, r


def _rms_fwd(name, x, g):
    def fn(ins, ps):
        xh, _ = _xhat(ins[0])
        return [xh * ps[0]], []

    return _rowwise(name, fn, [x], [g], [(x.shape[1], BF16)])[0]


def _rms_bwd(name, dn, x, g, dres=None, copy_scale=None):
    w = x.shape[1]

    def fn(ins, ps):
        dn_, x_ = ins[0], ins[1]
        xh, r = _xhat(x_)
        dxh = dn_ * ps[0]
        dx = r * (dxh - xh * jnp.mean(dxh * xh, axis=-1, keepdims=True))
        if dres is not None:
            dx = dx + ins[2]
        o = [dx] + ([dx * copy_scale] if copy_scale is not None else [])
        return o, [jnp.sum(dn_ * xh, axis=0, keepdims=True)]

    ins = [dn, x] + ([dres] if dres is not None else [])
    outs = [(w, F32)] + ([(w, BF16)] if copy_scale is not None else [])
    res = _rowwise(name, fn, ins, [g], outs, accs=[w])
    return res[:-1], res[-1]


def _swiglu_epilogue(accs, ex):
    g, u = accs
    return [g, u, g * _sigmoid(g) * u]


def _dswiglu_epilogue(accs, ex):
    da = accs[0]
    g, u = ex[0].astype(F32), ex[1].astype(F32)
    sg = _sigmoid(g)
    return [da * u * (sg * (1.0 + g * (1.0 - sg))), da * (g * sg)]


def _ffn_fwd(tag, h, gnorm, wg, wu, wd):
    n = _rms_fwd(tag + "_norm", h, gnorm)
    g, u, a = _mm(tag + "_up", [n], [wg, wu], [BF16, BF16, BF16], pairs=((0, 0, 0), (0, 1, 1)),
                  epilogue=_swiglu_epilogue, tm=1024, tn=512, n_split=2)
    (hout,) = _mm(tag + "_down", [a], [wd], [F32], extras=[(h, "mn")],
                  epilogue=lambda accs, ex: [ex[0] + 0.5 * accs[0]], tm=512, tn=1024, tk=8192)
    return hout, (n, g, u, a)


def _ffn_bwd(tag, dh, dhb_half, h, gnorm, wg, wu, wd, saved, copy_scale):
    n, g, u, a = saved
    dg, du = _mm(tag + "_dact", [dhb_half], [wd], [BF16, BF16], tb=True, extras=[(g, "mn"), (u, "mn")],
                 epilogue=_dswiglu_epilogue, tm=1024, tn=512, n_split=2)
    (dwd,) = _mm(tag + "_dwd", [a], [dhb_half], [BF16], ta=True, tm=512, tn=2048, tk=2048)
    dwg, dwu = _mm(tag + "_dwgu", [n], [dg, du], [BF16, BF16], pairs=((0, 0, 0), (0, 1, 1)), ta=True,
                   tm=1024, tn=512, tk=2048)
    (dn,) = _mm(tag + "_dn", [dg, du], [wg, wu], [F32], pairs=((0, 0, 0), (1, 1, 0)), tb=True,
                tm=1024, tn=1024, tk=1408)
    douts, dgn = _rms_bwd(tag + "_dnorm", dn, h, gnorm, dres=dh, copy_scale=copy_scale)
    return douts, dgn, dwg, dwu, dwd


ATTN_HEAD_PAIRS = 8


def _to_attn_order(a):
    S, w = a.shape
    return a.reshape(S // 16, 16, w).transpose(1, 0, 2).reshape(S, w)


def _from_attn_order(a):
    S, w = a.shape
    return a.reshape(16, S // 16, w).transpose(1, 0, 2).reshape(S, w)


def _attn_geom(S, d):
    s16 = S // 16
    if d == 16:
        return (16, s16), (1, SWA_BLOCK), (lambda r, b: (r, b)), 16, s16 // SWA_BLOCK
    if d == 4:
        return (4, 4, s16), (4, 1, SWA_BLOCK // 4), (lambda r, b: (0, r, b)), 4, s16 // (SWA_BLOCK // 4)
    return (16, s16), (16, SWA_BLOCK // 16), (lambda r, b: (0, b)), 1, s16 // (SWA_BLOCK // 16)


def _attn_pos(rho, d):
    if d == 16:
        return rho
    if d == 4:
        return 4 * (rho & 31) + (rho >> 5)
    return 16 * (rho & 7) + (rho >> 3)


def _attn_spec(S, d, lb, col, shift=0):
    _, blk, idx, _, nb = _attn_geom(S, d)
    return pl.BlockSpec(blk + (lb,), lambda r, cb, b: idx(r, jnp.clip(b + shift, 0, nb - 1)) + (col(cb),))


def _attn_view(a, d):
    return a.reshape(_attn_geom(a.shape[0], d)[0] + (a.shape[1],))


def _attn_valid(d):
    qp = _attn_pos(lax.broadcasted_iota(jnp.int32, (SWA_BLOCK, 2 * SWA_BLOCK), 0), d)
    kk = lax.broadcasted_iota(jnp.int32, (SWA_BLOCK, 2 * SWA_BLOCK), 1)
    kp = _attn_pos(kk & (SWA_BLOCK - 1), d)
    is_prev = kk < SWA_BLOCK
    return qp, kp, is_prev


def _head_masks(rows=SWA_BLOCK):
    lane = lax.broadcasted_iota(jnp.int32, (rows, LANES), 1)
    return [lane < HEAD_DIM, lane >= HEAD_DIM]


def _attn_ld(ref, sl):
    t = ref[(slice(None),) * (len(ref.shape) - 1) + (sl,)]
    return t.reshape(-1, t.shape[-1])


def _attn_st(ref, sl, val):
    ref[(slice(None),) * (len(ref.shape) - 1) + (sl,)] = val.reshape(ref.shape[:-1] + (val.shape[-1],))


def _per_head(t, first):
    sw = pltpu.roll(t, HEAD_DIM, 1)
    lo = lax.broadcasted_iota(jnp.int32, t.shape, 1) < HEAD_DIM
    return jnp.where(lo, t, sw) if first else jnp.where(lo, sw, t)


def _dot_nt(a, b):
    return lax.dot_general(a, b, (((1,), (1,)), ((), ())), preferred_element_type=F32)


def _dot_tn(a, b):
    return lax.dot_general(a, b, (((0,), (0,)), ((), ())), preferred_element_type=F32)


def _dot(a, b):
    return jnp.dot(a, b, preferred_element_type=F32)


def _keep(mask, t):
    return jnp.where(mask, t.astype(F32), 0.0).astype(BF16)


def _attn_cols(A):
    lb = min(A, LANES * ATTN_HEAD_PAIRS)
    ncol = A // lb
    return lb, ncol, [lambda cb, part=part: part * ncol + cb for part in range(3)], (lambda cb: cb)


def _attn_fwd_stage(name, qkv, d, prev, final):
    S, A3 = qkv.shape
    A = A3 // 3
    lb, ncol, (cq, ck, cv), ca = _attn_cols(A)
    view, _, _, nres, nb = _attn_geom(S, d)
    scale = HEAD_DIM ** -0.5
    has_prev = prev is not None

    def body(*refs):
        q_ref, kp_ref, kc_ref, vp_ref, vc_ref = refs[:5]
        p_refs = refs[5:8] if has_prev else ()
        o_refs = refs[5 + len(p_refs):]
        b = pl.program_id(2)
        qp, kp_, is_prev = _attn_valid(d)
        valid = (is_prev & (kp_ >= qp) & (b > 0)) | (jnp.logical_not(is_prev) & (kp_ <= qp))
        hm, hm2 = _head_masks(), _head_masks(2 * SWA_BLOCK)
        for hp in range(lb // LANES):
            sl = slice(hp * LANES, (hp + 1) * LANES)
            q = _attn_ld(q_ref, sl)
            k2 = jnp.concatenate([_attn_ld(kp_ref, sl), _attn_ld(kc_ref, sl)], axis=0).astype(BF16)
            v2 = jnp.concatenate([_attn_ld(vp_ref, sl), _attn_ld(vc_ref, sl)], axis=0)
            o = jnp.zeros((SWA_BLOCK, LANES), F32)
            m = jnp.zeros((SWA_BLOCK, LANES), F32)
            l = jnp.zeros((SWA_BLOCK, LANES), F32)
            for hh in range(2):
                s = jnp.where(valid, _dot_nt(_keep(hm[hh], q), k2) * scale, MASK_VALUE)
                mh = jnp.max(s, axis=-1, keepdims=True)
                p = jnp.exp(s - mh)
                lh = jnp.sum(p, axis=-1, keepdims=True)
                o = o + _dot(p.astype(BF16), _keep(hm2[hh], v2))
                m = jnp.where(hm[hh], mh, m)
                l = jnp.where(hm[hh], lh, l)
            if has_prev:
                po, pm, pl_ = (_attn_ld(r, sl) for r in p_refs)
                mn = jnp.maximum(m, pm)
                w_new, w_old = jnp.exp(m - mn), jnp.exp(pm - mn)
                o = o * w_new + po * w_old
                l = l * w_new + pl_ * w_old
                m = mn
            if final:
                _attn_st(o_refs[0], sl, o / l)
                _attn_st(o_refs[1], sl, m + jnp.log(l))
            else:
                _attn_st(o_refs[0], sl, o)
                _attn_st(o_refs[1], sl, m)
                _attn_st(o_refs[2], sl, l)

    n_out = 2 if final else 3
    qk = _attn_view(qkv, d)
    prev_v = [_attn_view(t, d) for t in prev] if has_prev else []
    sp = functools.partial(_attn_spec, S, d, lb)
    res = _pcall(
        body, name=name,
        grid=(nres, ncol, nb),
        in_specs=[sp(cq), sp(ck, -1), sp(ck), sp(cv, -1), sp(cv)] + [sp(ca)] * len(prev_v),
        out_specs=[sp(ca)] * n_out,
        out_shape=[jax.ShapeDtypeStruct(view + (A,), F32)] * n_out,
        compiler_params=_params("parallel", "parallel", "arbitrary"),
    )(qk, qk, qk, qk, qk, *prev_v)
    return [t.reshape(S, A) for t in res]


def _attn_fwd(qkv):
    st = None
    for i, d in enumerate(DILATIONS):
        st = _attn_fwd_stage("attn_fwd_d%d" % d, qkv, d, st, final=(i == len(DILATIONS) - 1))
    return st


def _attn_dq_stage(name, qkv, do, lse, delta, d, prev):
    S, A3 = qkv.shape
    A = A3 // 3
    lb, ncol, (cq, ck, cv), ca = _attn_cols(A)
    view, _, _, nres, nb = _attn_geom(S, d)
    scale = HEAD_DIM ** -0.5
    has_prev = prev is not None

    def body(*refs):
        q_ref, kp_ref, kc_ref, vp_ref, vc_ref, do_ref, lse_ref, dl_ref = refs[:8]
        b = pl.program_id(2)
        qp, kp_, is_prev = _attn_valid(d)
        valid = (is_prev & (kp_ >= qp) & (b > 0)) | (jnp.logical_not(is_prev) & (kp_ <= qp))
        hm, hm2 = _head_masks(), _head_masks(2 * SWA_BLOCK)
        for hp in range(lb // LANES):
            sl = slice(hp * LANES, (hp + 1) * LANES)
            q, do_, lse_, dl_ = (_attn_ld(r, sl) for r in (q_ref, do_ref, lse_ref, dl_ref))
            k2 = jnp.concatenate([_attn_ld(kp_ref, sl), _attn_ld(kc_ref, sl)], axis=0)
            v2 = jnp.concatenate([_attn_ld(vp_ref, sl), _attn_ld(vc_ref, sl)], axis=0).astype(BF16)
            k2b = k2.astype(BF16)
            dq = jnp.zeros((SWA_BLOCK, LANES), F32)
            for hh in range(2):
                doh = _keep(hm[hh], do_)
                lh, dh = _per_head(lse_, hh == 0), _per_head(dl_, hh == 0)
                lh2, dh2 = jnp.concatenate([lh, lh], axis=1), jnp.concatenate([dh, dh], axis=1)
                s = _dot_nt(_keep(hm[hh], q), k2b) * scale
                p = jnp.where(valid, jnp.exp(s - lh2), 0.0)
                ds = p * (_dot_nt(doh, v2) - dh2)
                dq = dq + _dot(ds.astype(BF16), _keep(hm2[hh], k2))
            dq = dq * scale
            if has_prev:
                dq = dq + _attn_ld(refs[8], sl)
            _attn_st(refs[-1], sl, dq)

    qk = _attn_view(qkv, d)
    acts = [_attn_view(t, d) for t in (do, lse, delta)] + ([_attn_view(prev, d)] if has_prev else [])
    sp = functools.partial(_attn_spec, S, d, lb)
    res = _pcall(
        body, name=name,
        grid=(nres, ncol, nb),
        in_specs=[sp(cq), sp(ck, -1), sp(ck), sp(cv, -1), sp(cv)] + [sp(ca)] * len(acts),
        out_specs=sp(ca),
        out_shape=jax.ShapeDtypeStruct(view + (A,), F32),
        compiler_params=_params("parallel", "parallel", "arbitrary"),
    )(qk, qk, qk, qk, qk, *acts)
    return res.reshape(S, A)


def _attn_dkv_stage(name, qkv, do, lse, delta, d, prev):
    S, A3 = qkv.shape
    A = A3 // 3
    lb, ncol, (cq, ck, cv), ca = _attn_cols(A)
    view, _, _, nres, nb = _attn_geom(S, d)
    scale = HEAD_DIM ** -0.5
    has_prev = prev is not None

    def body(*refs):
        k_ref, v_ref, qc_ref, qn_ref, doc_ref, don_ref, lc_ref, ln_ref, dc_ref, dn_ref = refs[:10]
        j = pl.program_id(2)
        rr = lax.broadcasted_iota(jnp.int32, (2 * SWA_BLOCK, SWA_BLOCK), 0)
        qp = _attn_pos(rr & (SWA_BLOCK - 1), d)
        kp_ = _attn_pos(lax.broadcasted_iota(jnp.int32, (2 * SWA_BLOCK, SWA_BLOCK), 1), d)
        valid = ((rr < SWA_BLOCK) & (kp_ <= qp)) | ((rr >= SWA_BLOCK) & (kp_ >= qp) & (j < nb - 1))
        hm2 = _head_masks(2 * SWA_BLOCK)
        for hp in range(lb // LANES):
            sl = slice(hp * LANES, (hp + 1) * LANES)
            kb, vb = _attn_ld(k_ref, sl).astype(BF16), _attn_ld(v_ref, sl).astype(BF16)
            q2 = jnp.concatenate([_attn_ld(qc_ref, sl), _attn_ld(qn_ref, sl)], axis=0)
            do2 = jnp.concatenate([_attn_ld(doc_ref, sl), _attn_ld(don_ref, sl)], axis=0)
            l2 = jnp.concatenate([_attn_ld(lc_ref, sl), _attn_ld(ln_ref, sl)], axis=0)
            d2 = jnp.concatenate([_attn_ld(dc_ref, sl), _attn_ld(dn_ref, sl)], axis=0)
            dk = jnp.zeros((SWA_BLOCK, LANES), F32)
            dv = jnp.zeros((SWA_BLOCK, LANES), F32)
            for hh in range(2):
                qh, doh = _keep(hm2[hh], q2), _keep(hm2[hh], do2)
                lh, dh = _per_head(l2, hh == 0), _per_head(d2, hh == 0)
                s = _dot_nt(qh, kb) * scale
                p = jnp.where(valid, jnp.exp(s - lh), 0.0)
                dv = dv + _dot_tn(p.astype(BF16), doh)
                ds = p * (_dot_nt(doh, vb) - dh)
                dk = dk + _dot_tn(ds.astype(BF16), qh)
            dk = dk * scale
            if has_prev:
                dk = dk + _attn_ld(refs[10], sl)
                dv = dv + _attn_ld(refs[11], sl)
            _attn_st(refs[-2], sl, dk)
            _attn_st(refs[-1], sl, dv)

    qk = _attn_view(qkv, d)
    acts = [_attn_view(t, d) for t in (do, lse, delta)]
    prev_v = [_attn_view(t, d) for t in prev] if has_prev else []
    sp = functools.partial(_attn_spec, S, d, lb)
    res = _pcall(
        body, name=name,
        grid=(nres, ncol, nb),
        in_specs=[sp(ck), sp(cv), sp(cq), sp(cq, 1), sp(ca), sp(ca, 1), sp(ca), sp(ca, 1), sp(ca), sp(ca, 1)]
        + [sp(ca)] * len(prev_v),
        out_specs=[sp(ca), sp(ca)],
        out_shape=[jax.ShapeDtypeStruct(view + (A,), F32)] * 2,
        compiler_params=_params("parallel", "parallel", "arbitrary"),
    )(qk, qk, qk, qk, acts[0], acts[0], acts[1], acts[1], acts[2], acts[2], *prev_v)
    return [t.reshape(S, A) for t in res]


def _attn_delta(dya, ya):
    S, A = ya.shape
    ri = lax.broadcasted_iota(jnp.int32, (A, A), 0) // HEAD_DIM
    ci = lax.broadcasted_iota(jnp.int32, (A, A), 1) // HEAD_DIM
    ones_bd = (ri == ci).astype(BF16)

    def fn(ins, ps):
        prod = ins[0] * ins[1]
        hi = prod.astype(BF16)
        lo = (prod - hi.astype(F32)).astype(BF16)
        return [_dot(hi, ps[0]) + _dot(lo, ps[0])], []

    return _rowwise("attn_delta", fn, [dya, ya], [ones_bd], [(A, F32)])[0]


def _attn_bwd(qkv, dya, ya, lse):
    delta = _attn_delta(dya, ya)
    dq, dkv = None, None
    for d in DILATIONS:
        dq = _attn_dq_stage("attn_dq_d%d" % d, qkv, dya, lse, delta, d, dq)
        dkv = _attn_dkv_stage("attn_dkv_d%d" % d, qkv, dya, lse, delta, d, dkv)
    return dq, dkv[0], dkv[1]


def _ssm_perm(a, T):
    S, w = a.shape
    return a.reshape(S // T, SUBLANES, T // SUBLANES, w).transpose(0, 2, 1, 3).reshape(S, w)


def _ssm_unperm(a, T):
    S, w = a.shape
    return a.reshape(S // T, T // SUBLANES, SUBLANES, w).transpose(0, 2, 1, 3).reshape(S, w)


def _ssm_powers(lam_ref, pw_ref, T, ns):
    n = (lax.broadcasted_iota(jnp.int32, (T, 1), 0) // SUBLANES + 1).astype(F32)
    mag = jnp.exp(n * lam_ref[0, 0:1, :])
    ang = n * lam_ref[0, 1:2, :]
    pw_ref[:, 0:ns] = mag * jnp.cos(ang)
    pw_ref[:, ns:2 * ns] = mag * jnp.sin(ang)


def _ssm_scan(xs, off, pw_ref, carry_ref, T, ns, reverse):
    Tc = T // SUBLANES
    sgn = -1.0 if reverse else 1.0
    ar, ai = pw_ref[0:SUBLANES, 0:ns], sgn * pw_ref[0:SUBLANES, ns:2 * ns]

    def rows(i):
        return pl.ds(pl.multiple_of(off + i * SUBLANES, SUBLANES), SUBLANES)

    def step(k, h):
        hr, hi = h
        r = rows(Tc - 1 - k if reverse else k)
        nr = ar * hr - ai * hi + xs[r, 0:ns]
        ni = ar * hi + ai * hr + xs[r, ns:2 * ns]
        xs[r, 0:ns] = nr
        xs[r, ns:2 * ns] = ni
        return nr, ni

    z = jnp.zeros((SUBLANES, ns), F32)
    er, ei = lax.fori_loop(0, Tc, step, (z, z))
    atr, ati = pw_ref[T - SUBLANES:T, 0:ns], sgn * pw_ref[T - SUBLANES:T, ns:2 * ns]
    rowid = lax.broadcasted_iota(jnp.int32, (SUBLANES, ns), 0)
    cr, ci = carry_ref[:, 0:ns], carry_ref[:, ns:2 * ns]
    ctr, cti = z, z
    for jj in range(SUBLANES):
        j = SUBLANES - 1 - jj if reverse else jj
        sel = rowid == j
        ctr, cti = jnp.where(sel, cr, ctr), jnp.where(sel, ci, cti)
        ejr = jnp.broadcast_to(jnp.sum(jnp.where(sel, er, 0.0), axis=0, keepdims=True), (SUBLANES, ns))
        eji = jnp.broadcast_to(jnp.sum(jnp.where(sel, ei, 0.0), axis=0, keepdims=True), (SUBLANES, ns))
        cr, ci = ejr + atr * cr - ati * ci, eji + atr * ci + ati * cr
    carry_ref[:, 0:ns] = cr
    carry_ref[:, ns:2 * ns] = ci

    def fix(i, _):
        r = rows(i)
        pr_rows = pl.ds(pl.multiple_of((Tc - 1 - i if reverse else i) * SUBLANES, SUBLANES), SUBLANES)
        pr, pi = pw_ref[pr_rows, 0:ns], sgn * pw_ref[pr_rows, ns:2 * ns]
        xs[r, 0:ns] += pr * ctr - pi * cti
        xs[r, ns:2 * ns] += pr * cti + pi * ctr
        return 0

    lax.fori_loop(0, Tc, fix, 0)
    return ctr, cti


def _ssm_fwd(ufp, bb, cc, lam_dt, drow, T):
    S, W = ufp.shape
    GB, cw, ns2 = bb.shape
    ns = ns2 // 2
    NCH = S // T

    def body(uf_ref, bb_ref, cc_ref, lam_ref, d_ref, y_ref, hs_ref, xs, pw, carry):
        @pl.when(pl.program_id(1) == 0)
        def _():
            _ssm_powers(lam_ref, pw, T, ns)
            carry[...] = jnp.zeros(carry.shape, F32)

        uf = uf_ref[...]
        xs[...] = _dot(uf.astype(BF16), bb_ref[0])
        hs_ref[0, 0] = carry[...]
        _ssm_scan(xs, 0, pw, carry, T, ns, reverse=False)
        y_ref[...] = _dot(xs[...].astype(BF16), cc_ref[0]) + d_ref[...] * uf

    return _pcall(
        body, name="ssm_fwd",
        grid=(GB, NCH),
        in_specs=[pl.BlockSpec((T, cw), lambda g, c: (c, g)),
                  pl.BlockSpec((1, cw, ns2), lambda g, c: (g, 0, 0)),
                  pl.BlockSpec((1, ns2, cw), lambda g, c: (g, 0, 0)),
                  pl.BlockSpec((1, 2, ns), lambda g, c: (g, 0, 0)),
                  pl.BlockSpec((1, cw), lambda g, c: (0, g))],
        out_specs=[pl.BlockSpec((T, cw), lambda g, c: (c, g)),
                   pl.BlockSpec((1, 1, SUBLANES, ns2), lambda g, c: (g, c, 0, 0))],
        out_shape=[jax.ShapeDtypeStruct((S, W), F32),
                   jax.ShapeDtypeStruct((GB, NCH, SUBLANES, ns2), F32)],
        scratch_shapes=[pltpu.VMEM((T, ns2), F32), pltpu.VMEM((T, ns2), F32), pltpu.VMEM((SUBLANES, ns2), F32)],
        compiler_params=_params("arbitrary", "arbitrary"),
    )(ufp, bb, cc, lam_dt, drow)


def _ssm_bwd(ufp, dyp, bb, bbt, cc, cct, lam_dt, drow, hstart, T):
    S, W = ufp.shape
    GB, cw, ns2 = bb.shape
    ns = ns2 // 2
    NCH = S // T

    def body(uf_ref, dy_ref, bb_ref, bbt_ref, cc_ref, cct_ref, lam_ref, d_ref, hs_ref,
             duf_ref, dbb_ref, dcc_ref, da_ref, dd_ref, hb, ls, pw, carry_f, carry_b):
        @pl.when(pl.program_id(1) == 0)
        def _():
            _ssm_powers(lam_ref, pw, T, ns)
            carry_b[...] = jnp.zeros(carry_b.shape, F32)
            dbb_ref[...] = jnp.zeros(dbb_ref.shape, F32)
            dcc_ref[...] = jnp.zeros(dcc_ref.shape, F32)
            da_ref[...] = jnp.zeros(da_ref.shape, F32)
            dd_ref[...] = jnp.zeros(dd_ref.shape, F32)

        uf, dy = uf_ref[...], dy_ref[...]
        ufb, dyb = uf.astype(BF16), dy.astype(BF16)
        hb[SUBLANES:T + SUBLANES, :] = _dot(ufb, bb_ref[0])
        carry_f[...] = hs_ref[0, 0]
        ctr, cti = _ssm_scan(hb, SUBLANES, pw, carry_f, T, ns, reverse=False)
        hb[0:SUBLANES, 0:ns] = ctr
        hb[0:SUBLANES, ns:ns2] = cti
        ls[...] = _dot(dyb, cct_ref[0])
        _ssm_scan(ls, 0, pw, carry_b, T, ns, reverse=True)
        lv = ls[...]
        lb = lv.astype(BF16)
        dbb_ref[0] += _dot_tn(ufb, lb)
        dcc_ref[0] += _dot_tn(hb[SUBLANES:T + SUBLANES, :].astype(BF16), dyb)
        lr, li = lv[:, 0:ns], lv[:, ns:ns2]
        hpr, hpi = hb[0:T, 0:ns], hb[0:T, ns:ns2]
        dar = jnp.sum(lr * hpr + li * hpi, axis=0, keepdims=True)
        dai = jnp.sum(li * hpr - lr * hpi, axis=0, keepdims=True)
        da_ref[0, 0:1, 0:ns] += dar
        da_ref[0, 0:1, ns:ns2] += dai
        duf_ref[...] = _dot(lb, bbt_ref[0]) + d_ref[...] * dy
        dd_ref[...] += jnp.sum(dy * uf, axis=0, keepdims=True)

    rc = lambda c: NCH - 1 - c
    return _pcall(
        body, name="ssm_bwd",
        grid=(GB, NCH),
        in_specs=[pl.BlockSpec((T, cw), lambda g, c: (rc(c), g)),
                  pl.BlockSpec((T, cw), lambda g, c: (rc(c), g)),
                  pl.BlockSpec((1, cw, ns2), lambda g, c: (g, 0, 0)),
                  pl.BlockSpec((1, ns2, cw), lambda g, c: (g, 0, 0)),
                  pl.BlockSpec((1, ns2, cw), lambda g, c: (g, 0, 0)),
                  pl.BlockSpec((1, cw, ns2), lambda g, c: (g, 0, 0)),
                  pl.BlockSpec((1, 2, ns), lambda g, c: (g, 0, 0)),
                  pl.BlockSpec((1, cw), lambda g, c: (0, g)),
                  pl.BlockSpec((1, 1, SUBLANES, ns2), lambda g, c: (g, rc(c), 0, 0))],
        out_specs=[pl.BlockSpec((T, cw), lambda g, c: (rc(c), g)),
                   pl.BlockSpec((1, cw, ns2), lambda g, c: (g, 0, 0)),
                   pl.BlockSpec((1, ns2, cw), lambda g, c: (g, 0, 0)),
                   pl.BlockSpec((1, SUBLANES, ns2), lambda g, c: (g, 0, 0)),
                   pl.BlockSpec((1, cw), lambda g, c: (0, g))],
        out_shape=[jax.ShapeDtypeStruct((S, W), F32),
                   jax.ShapeDtypeStruct((GB, cw, ns2), F32),
                   jax.ShapeDtypeStruct((GB, ns2, cw), F32),
                   jax.ShapeDtypeStruct((GB, SUBLANES, ns2), F32),
                   jax.ShapeDtypeStruct((1, W), F32)],
        scratch_shapes=[pltpu.VMEM((T + SUBLANES, ns2), F32), pltpu.VMEM((T, ns2), F32), pltpu.VMEM((T, ns2), F32),
                        pltpu.VMEM((SUBLANES, ns2), F32), pltpu.VMEM((SUBLANES, ns2), F32)],
        compiler_params=_params("arbitrary", "arbitrary"),
    )(ufp, dyp, bb, bbt, cc, cct, lam_dt, drow, hstart)


def _ssm_disc_math(lr, li, logdt, br, bi):
    dt = jnp.exp(logdt)
    mag = jnp.exp(lr * dt)
    ar = mag * jnp.cos(li * dt)
    ai = mag * jnp.sin(li * dt)
    nr, ni = ar - 1.0, ai
    den = lr * lr + li * li
    cr = (nr * lr + ni * li) / den
    ci = (ni * lr - nr * li) / den
    return ar, ai, cr * br - ci * bi, cr * bi + ci * br


def _ssm_disc(lr, li, logdt, br, bi):
    C = br.shape[1]

    def fn(ins, ps):
        _, _, bbr, bbi = _ssm_disc_math(*ins)
        dt = jnp.exp(ins[2])
        return [ins[0] * dt, ins[1] * dt, bbr, bbi], []

    return _rowwise("ssm_disc", fn, [lr, li, logdt, br, bi], [], [(1, F32), (1, F32), (C, F32), (C, F32)], ts=512)


def _ssm_disc_bwd(lr, li, logdt, br, bi, dar, dai, dbbr, dbbi):
    C = br.shape[1]

    def fn(ins, ps):
        _, vjp = jax.vjp(_ssm_disc_math, *ins[:5])
        return list(vjp(tuple(ins[5:]))), []

    return _rowwise("ssm_disc_bwd", fn, [lr, li, logdt, br, bi, dar, dai, dbbr, dbbi], [],
                    [(1, F32), (1, F32), (1, F32), (C, F32), (C, F32)], ts=512)


def _block_diag(t):
    GB, g, a, b = t.shape
    eye = jnp.eye(g, dtype=t.dtype)
    return (t[:, :, :, None, :] * eye[None, :, None, :, None]).reshape(GB, g * a, g * b)


def _block_diag_take(t, g):
    GB, ga, gb_ = t.shape
    a, b = ga // g, gb_ // g
    eye = jnp.eye(g, dtype=t.dtype)
    return (t.reshape(GB, g, a, g, b) * eye[None, :, None, :, None]).sum(axis=3)


def _loss_head(h4, tgt, gf):
    D = h4.shape[1]

    def fn(ins, ps):
        x, t = ins
        xh, r = _xhat(x)
        err = xh * ps[0] - t
        dn = err * (1.0 / D)
        dxh = dn * ps[0]
        dx = r * (dxh - xh * jnp.mean(dxh * xh, axis=-1, keepdims=True))
        return [dx], [jnp.sum(err * err, axis=0, keepdims=True), jnp.sum(dn * xh, axis=0, keepdims=True)]

    return _rowwise("loss_head", fn, [h4, tgt], [gf], [(D, F32)], accs=[D, D])


def _gelu(x):
    return 0.5 * x * (1.0 + jnp.tanh(GELU_C * (x + GELU_K * x * x * x)))


def _gelu_grad(x):
    t = jnp.tanh(GELU_C * (x + GELU_K * x * x * x))
    return 0.5 * (1.0 + t) + 0.5 * x * (1.0 - t * t) * GELU_C * (1.0 + 3.0 * GELU_K * x * x)


def _mesh_pos():
    return lax.axis_index("x"), lax.axis_index("y"), lax.axis_index("c")


def _other_chips(x, y):
    return [(1 - x, y), (x, 1 - y), (1 - x, 1 - y)]


def _remote(src, dst, send, recv, dev):
    return pltpu.make_async_remote_copy(src_ref=src, dst_ref=dst, send_sem=send, recv_sem=recv,
                                        device_id=dev, device_id_type=MESH)


ANY = pl.BlockSpec(memory_space=pl.ANY)


COMM_BLOCK_BYTES = 3 << 19


def _place():
    x, y, c = _mesh_pos()
    return jnp.stack([c] + [2 * cx + cy for cx, cy in _other_chips(x, y)] + [2 * x + y]).astype(jnp.int32)


def _send_chips(name, srcs, specs, tr, nth, cw):
    hr = nth * tr
    n = len(srcs)

    def body(*refs):
        got_ref, send, recv = refs[1 + n:]
        t = pl.program_id(0)
        x, y, c = _mesh_pos()
        cps = []
        for j, chip in enumerate(_other_chips(x, y)):
            dst = got_ref.at[pl.ds(pl.multiple_of(j * hr + t * tr, 16), tr), :]
            cp = _remote(refs[1 + j % n], dst, send.at[j], recv.at[j], (*chip, c))
            cp.start()
            cps.append(cp)
        for cp in cps:
            cp.wait_send()

        @pl.when(t == nth - 1)
        def _():
            for j in range(3):
                r_ = got_ref.at[pl.ds(j * hr, hr), :]
                _remote(r_, r_, send.at[j], recv.at[j], (x, y, c)).wait_recv()

    return _pcall(
        body, name=name,
        grid_spec=pltpu.PrefetchScalarGridSpec(
            num_scalar_prefetch=1, grid=(nth,), in_specs=specs, out_specs=ANY,
            scratch_shapes=[pltpu.SemaphoreType.DMA((3,)), pltpu.SemaphoreType.DMA((3,))]),
        out_shape=jax.ShapeDtypeStruct((3 * hr, cw), srcs[0].dtype),
        compiler_params=_params("arbitrary"),
    )(_place(), *srcs)


def _ag_assemble(name, shard, stage, axis, tr, nth):
    R, cc = shard.shape
    hr = nth * tr
    full = (R, N_CHIPS * cc) if axis == 1 else (N_CHIPS * R, cc)

    def body(pl_ref, s0, s1, s2, h0, h1, out_ref, send, recv, lsem):
        t = pl.program_id(0)
        x, y, c = _mesh_pos()

        def region(s, half):
            if axis == 1:
                return out_ref.at[pl.ds(pl.multiple_of(half * hr + t * tr, 16), tr), pl.ds(pl.multiple_of(s * cc, LANES), cc)]
            return out_ref.at[pl.ds(pl.multiple_of(s * R + half * hr + t * tr, 16), tr), :]

        cps = []
        for j, src in enumerate((s0, s1, s2)):
            dst = region(pl_ref[1 + j], c)
            cps.append(_remote(src, dst, send.at[j], recv, (x, y, 1 - c)))
            cps.append(pltpu.make_async_copy(src, dst, lsem.at[j]))
        for half, src in enumerate((h0, h1)):
            cps.append(pltpu.make_async_copy(src, region(pl_ref[4], half), lsem.at[3 + half]))
        for cp in cps:
            cp.start()
        for k, cp in enumerate(cps):
            if k < 6 and k % 2 == 0:
                cp.wait_send()
            else:
                cp.wait()

        @pl.when(t == nth - 1)
        def _():
            r_ = out_ref.at[pl.ds(0, hr), pl.ds(0, 3 * cc)] if axis == 1 else out_ref.at[pl.ds(0, 3 * hr), :]
            _remote(r_, r_, send.at[0], recv, (x, y, c)).wait_recv()

    blk = lambda f: pl.BlockSpec((tr, cc), f)
    return _pcall(
        body, name=name,
        grid_spec=pltpu.PrefetchScalarGridSpec(
            num_scalar_prefetch=1, grid=(nth,),
            in_specs=[blk(lambda t, p, j=j: (j * nth + t, 0)) for j in range(3)]
            + [blk(lambda t, p, h=h: (h * nth + t, 0)) for h in range(2)],
            out_specs=ANY,
            scratch_shapes=[pltpu.SemaphoreType.DMA((3,)), pltpu.SemaphoreType.DMA, pltpu.SemaphoreType.DMA((5,))]),
        out_shape=jax.ShapeDtypeStruct(full, shard.dtype),
        compiler_params=_params("arbitrary"),
    )(_place(), stage, stage, stage, shard, shard)


def _all_gather_weights(shards, axes):
    out = []
    for w, (sh, ax) in enumerate(zip(shards, axes)):
        R, cc = sh.shape
        hr = R // 2
        tr = _tile(hr, max(16, COMM_BLOCK_BYTES // (cc * 2) // 16 * 16), 16)
        nth = hr // tr
        stage = _send_chips("ag_send%d" % w, [sh], [pl.BlockSpec((tr, cc), lambda t, p, nth=nth: (p[0] * nth + t, 0))],
                            tr, nth, cc)
        out.append(_ag_assemble("ag_asm%d" % w, sh, stage, ax, tr, nth))
    return out


def _push_pair(name, src, tr, nblk, src_block, out_rows, dst_block, local):
    cw = src.shape[1]
    c_arr = lax.axis_index("c").astype(jnp.int32).reshape(1)

    def body(c_ref, src_ref, out_ref, send, recv, lsem):
        i = pl.program_id(0)
        x, y, c = _mesh_pos()
        dst = out_ref.at[pl.ds(pl.multiple_of(dst_block(i, c_ref[0]) * tr, 16), tr), :]
        cp = _remote(src_ref, dst, send, recv, (x, y, 1 - c))
        cp.start()
        if local:
            lc = pltpu.make_async_copy(src_ref, dst, lsem)
            lc.start()
            lc.wait()
        cp.wait_send()

        @pl.when(i == nblk - 1)
        def _():
            got = out_ref.at[pl.ds(0, nblk * tr), :]
            _remote(got, got, send, recv, (x, y, c)).wait_recv()

    return _pcall(
        body, name=name,
        grid_spec=pltpu.PrefetchScalarGridSpec(
            num_scalar_prefetch=1, grid=(nblk,),
            in_specs=[pl.BlockSpec((tr, cw), lambda i, c_ref: (src_block(i, c_ref[0]), 0))],
            out_specs=ANY,
            scratch_shapes=[pltpu.SemaphoreType.DMA, pltpu.SemaphoreType.DMA, pltpu.SemaphoreType.DMA]),
        out_shape=jax.ShapeDtypeStruct((out_rows, cw), src.dtype),
        compiler_params=_params("arbitrary"),
    )(c_arr, src)


def _sum_half(name, g, theirs, tr, nblk, src_block):
    cw = g.shape[1]
    c_arr = lax.axis_index("c").astype(jnp.int32).reshape(1)

    def body(c_ref, g_ref, t_ref, o_ref):
        o_ref[...] = (g_ref[...].astype(F32) + t_ref[...].astype(F32)).astype(BF16)

    return _pcall(
        body, name=name,
        grid_spec=pltpu.PrefetchScalarGridSpec(
            num_scalar_prefetch=1, grid=(nblk,),
            in_specs=[pl.BlockSpec((tr, cw), lambda i, c_ref: (src_block(i, c_ref[0]), 0)),
                      pl.BlockSpec((tr, cw), lambda i, c_ref: (i, 0))],
            out_specs=pl.BlockSpec((tr, cw), lambda i, c_ref: (i, 0))),
        out_shape=jax.ShapeDtypeStruct((nblk * tr, cw), BF16),
        compiler_params=_params("arbitrary"),
    )(c_arr, g, theirs)


def _sum_chips(name, q, got, qspec, tr, nth, cw):
    def body(p_ref, q_ref, g0, g1, g2, o_ref):
        o_ref[...] = q_ref[...].astype(F32) + g0[...].astype(F32) + g1[...].astype(F32) + g2[...].astype(F32)

    return _pcall(
        body, name=name,
        grid_spec=pltpu.PrefetchScalarGridSpec(
            num_scalar_prefetch=1, grid=(nth,),
            in_specs=[qspec] + [pl.BlockSpec((tr, cw), lambda t, p, j=j: (j * nth + t, 0)) for j in range(3)],
            out_specs=pl.BlockSpec((tr, cw), lambda t, p: (t, 0))),
        out_shape=jax.ShapeDtypeStruct((nth * tr, cw), F32),
        compiler_params=_params("arbitrary"),
    )(_place(), q, got, got, got)


def _reduce_scatter(grads, axes):
    outs = []
    for w, (g, ax) in enumerate(zip(grads, axes)):
        rows, gw = g.shape
        hr = rows // 2 if ax == 1 else rows // N_CHIPS // 2
        tr = _tile(hr, max(16, COMM_BLOCK_BYTES // (gw * 2) // 16 * 16), 16)
        nth = hr // tr
        if ax == 1:
            nblk, blk = nth, (lambda i, half, nth=nth: half * nth + i)
        else:
            nblk, blk = N_CHIPS * nth, (lambda i, half, nth=nth: (i // nth) * (2 * nth) + half * nth + i % nth)
        theirs = _push_pair("rs_pair%d" % w, g, tr, nblk, lambda i, c, blk=blk: blk(i, 1 - c), nblk * tr,
                            lambda i, c: i, local=False)
        q = _sum_half("rs_sum_pair%d" % w, g, theirs, tr, nblk, blk)

        cw = gw // N_CHIPS if ax == 1 else gw
        tr = _tile(hr, max(16, COMM_BLOCK_BYTES // (cw * 2) // 16 * 16), 16)
        nth = hr // tr
        if ax == 1:
            part = lambda k: pl.BlockSpec((tr, cw), lambda t, p: (t, p[k]))
        else:
            part = lambda k, nth=nth: pl.BlockSpec((tr, cw), lambda t, p: (p[k] * nth + t, 0))
        got = _send_chips("rs_send%d" % w, [q, q, q], [part(1), part(2), part(3)], tr, nth, cw)
        half = _sum_chips("rs_sum_chips%d" % w, q, got, part(4), tr, nth, cw)

        tr = _tile(hr, max(16, COMM_BLOCK_BYTES // (cw * 4) // 16 * 16), 16)
        nth = hr // tr
        outs.append(_push_pair("rs_swap%d" % w, half, tr, nth, lambda i, c: i, 2 * hr,
                               lambda i, c, nth=nth: c * nth + i, local=True))
    return outs


def _all_reduce_small(v):
    n = v.shape[0]

    def body(v_ref, out_ref, buf, send, recv):
        x, y, c = _mesh_pos()
        my = 4 * x + 2 * y + c
        buf[my] = v_ref[...]
        cps = []
        for k in range(1, N_DEV):
            fx, fy, fc = (k >> 2) & 1, (k >> 1) & 1, k & 1
            peer = (1 - x if fx else x, 1 - y if fy else y, 1 - c if fc else c)
            cp = _remote(v_ref, buf.at[my], send.at[k - 1], recv.at[k - 1], peer)
            cp.start()
            cps.append((cp, 4 * peer[0] + 2 * peer[1] + peer[2]))
        for k, (cp, pid) in enumerate(cps):
            _remote(v_ref, buf.at[pid], send.at[k], recv.at[k], (x, y, c)).wait_recv()
        acc = buf[0]
        for i in range(1, N_DEV):
            acc = acc + buf[i]
        out_ref[...] = acc
        for cp, _ in cps:
            cp.wait_send()

    return _pcall(
        body, name="ar_small",
        in_specs=[pl.BlockSpec(memory_space=pltpu.VMEM)], out_specs=pl.BlockSpec(memory_space=pltpu.VMEM),
        out_shape=jax.ShapeDtypeStruct((n, LANES), F32),
        scratch_shapes=[pltpu.VMEM((N_DEV, n, LANES), F32), pltpu.SemaphoreType.DMA((N_DEV - 1,)),
                        pltpu.SemaphoreType.DMA((N_DEV - 1,))],
        compiler_params=pltpu.CompilerParams(vmem_limit_bytes=V7X_VMEM_LIMIT_BYTES),
    )(v)


def _adamw(name, w, g, m, v):
    R, Cc = w.shape
    tr = _tile(R, max(8, (1 << 19) // Cc // 8 * 8), 8)
    c1 = 1.0 - ADAM_B1 ** ADAM_STEP
    c2 = 1.0 - ADAM_B2 ** ADAM_STEP

    def body(w_ref, g_ref, m_ref, v_ref, d_ref, nm_ref, nv_ref):
        g_ = g_ref[...]
        nm = ADAM_B1 * m_ref[...] + (1.0 - ADAM_B1) * g_
        nv = ADAM_B2 * v_ref[...] + (1.0 - ADAM_B2) * (g_ * g_)
        d_ref[...] = -ADAM_LR * ((nm / c1) / (jnp.sqrt(nv / c2) + ADAM_EPS) + ADAM_WD * w_ref[...])
        nm_ref[...] = nm
        nv_ref[...] = nv

    spec = pl.BlockSpec((tr, Cc), lambda i: (i, 0))
    return _pcall(
        body, name=name, grid=(R // tr,),
        in_specs=[spec] * 4, out_specs=[spec] * 3,
        out_shape=[jax.ShapeDtypeStruct((R, Cc), F32)] * 3,
        compiler_params=_params("parallel"),
    )(w, g, m, v)


def _pack(arrs, rows):
    flat = jnp.concatenate([a.reshape(-1) for a in arrs])
    return jnp.pad(flat, (0, rows * LANES - flat.shape[0])).reshape(rows, LANES)


def _unpack(packed, like):
    flat, out, o = packed.reshape(-1), [], 0
    for a in like:
        out.append(flat[o:o + a.size].reshape(a.shape))
        o += a.size
    return out


BIG = (
    ("ffn1_w_gate", 1), ("ffn1_w_up", 1), ("ffn1_w_down", 0), ("w_in", 1), ("ssm_w_glu", 0), ("w_out", 0),
    ("ffn2_w_gate", 1), ("ffn2_w_up", 1), ("ffn2_w_down", 0), ("ple_w_gate", 0), ("ple_w_proj", 1),
)
SMALL = ("ffn1_norm", "mix_norm", "attn_out_norm", "ssm_lambda_re", "ssm_lambda_im", "ssm_log_dt", "ssm_b_re", "ssm_b_im",
         "ssm_c_re", "ssm_c_im", "ssm_d", "ssm_b_glu", "ssm_out_norm", "ffn2_norm", "ple_norm", "final_norm")
WEIGHTS = ("ffn1_norm", "ffn1_w_gate", "ffn1_w_up", "ffn1_w_down", "mix_norm", "w_in", "attn_out_norm", "ssm_lambda_re",
           "ssm_lambda_im", "ssm_log_dt", "ssm_b_re", "ssm_b_im", "ssm_c_re", "ssm_c_im", "ssm_d", "ssm_w_glu", "ssm_b_glu",
           "ssm_out_norm", "w_out", "ffn2_norm", "ffn2_w_gate", "ffn2_w_up", "ffn2_w_down", "ple_norm", "ple_w_gate",
           "ple_w_proj", "final_norm")


def _pad_to(a, axis, n):
    pad = [(0, 0), (0, 0)]
    pad[axis] = (0, n - a.shape[axis])
    return jnp.pad(a, pad)


def _local_step(x, p, tgt, w, full):
    S, D = x.shape
    A = w["attn_out_norm"].shape[-1]
    W = w["ssm_d"].shape[-1]
    G, P = w["ssm_lambda_re"].shape[-2:]
    C = w["ssm_b_re"].shape[-1]
    GB = G // SSM_BLOCK_GROUPS
    T = min(1024, S)
    row = lambda name: w[name].reshape(1, -1)
    gs = {}

    h1, ffn1_saved = _ffn_fwd("ffn1", x, row("ffn1_norm"), full["ffn1_w_gate"], full["ffn1_w_up"], full["ffn1_w_down"])
    n2 = _rms_fwd("mix_norm", h1, row("mix_norm"))
    w_in = full["w_in"]
    n2p = _to_attn_order(n2)
    (qkv,) = _mm("w_in_qkv", [n2p], [w_in[:, :3 * A]], [F32], tm=1024, tn=1024)
    (s_in,) = _mm("w_in_ssm", [n2], [w_in[:, 3 * A:]], [F32], tm=1024, tn=1024)
    ya, lse = _attn_fwd(qkv)

    col = lambda name: w[name].reshape(G * P, 1)
    logdt_x = jnp.repeat(w["ssm_log_dt"].reshape(G), P).reshape(G * P, 1)
    b_re, b_im = w["ssm_b_re"].reshape(G * P, C), w["ssm_b_im"].reshape(G * P, C)
    lrdt, lidt, bbr, bbi = _ssm_disc(col("ssm_lambda_re"), col("ssm_lambda_im"), logdt_x, b_re, b_im)
    gsz = SSM_BLOCK_GROUPS
    to_bb = lambda t: _block_diag(t.reshape(GB, gsz, P, C).transpose(0, 1, 3, 2))
    bb = jnp.concatenate([to_bb(bbr), to_bb(bbi)], axis=2).astype(BF16)
    to_cc = lambda t: _block_diag(t.reshape(GB, gsz, C, P).transpose(0, 1, 3, 2))
    cc = jnp.concatenate([to_cc(w["ssm_c_re"]), -to_cc(w["ssm_c_im"])], axis=1).astype(BF16)
    lam_dt = jnp.stack([lrdt.reshape(GB, gsz * P), lidt.reshape(GB, gsz * P)], axis=1)
    ufp = _ssm_perm(s_in, T)
    ypre, hstart = _ssm_fwd(ufp, bb, cc, lam_dt, row("ssm_d"), T)

    def glu_in(ins, ps):
        yg = _gelu(ins[0])
        return [yg, yg], []

    yg, ygb = _rowwise("ssm_gelu", glu_in, [ypre], [], [(W, F32), (W, BF16)])
    w_glu = full["ssm_w_glu"]

    def glu_out(accs, ex):
        gl = accs[0] + ex[1]
        return [ex[0] * _sigmoid(gl), gl]

    ybp, gl = _mm("ssm_glu", [ygb], [w_glu], [F32, F32], extras=[(yg, "mn"), (row("ssm_b_glu"), "n")],
                  epilogue=glu_out, tm=1024, tn=1024, n_split=4)
    yb = _ssm_unperm(ybp, T)
    na = _from_attn_order(_rms_fwd("attn_out_norm", ya, row("attn_out_norm")))
    nb = _rms_fwd("ssm_out_norm", yb, row("ssm_out_norm"))
    w_out = full["w_out"]
    (h2,) = _mm("w_out", [na, nb], [w_out[:A], w_out[A:]], [F32], pairs=((0, 0, 0), (1, 1, 0)), extras=[(h1, "mn")],
                epilogue=lambda accs, ex: [ex[0] + accs[0]], tm=1024, tn=1024)
    h3, ffn2_saved = _ffn_fwd("ffn2", h2, row("ffn2_norm"), full["ffn2_w_gate"], full["ffn2_w_up"], full["ffn2_w_down"])
    n4 = _rms_fwd("ple_norm", h3, row("ple_norm"))
    (pe,) = _mm("ple_proj", [p], [full["ple_w_proj"]], [F32], tm=1024, tn=1024)

    def ple_out(accs, ex):
        gate = _sigmoid(accs[0])
        return [ex[1] + gate * ex[0], gate]

    h4, gate = _mm("ple_gate", [n4], [full["ple_w_gate"]], [F32, F32], extras=[(pe, "mn"), (h3, "mn")],
                   epilogue=ple_out, tm=1024, tn=1024, n_split=4)

    dh4, err2, gs["final_norm"] = _loss_head(h4, tgt, row("final_norm"))
    loss = (0.5 / D) * jnp.sum(err2)

    def ple_bwd(ins, ps):
        dh, gt, pe_ = ins
        return [dh * gt, dh * pe_ * gt * (1.0 - gt)], []

    dpe, dpg = _rowwise("ple_bwd", ple_bwd, [dh4, gate, pe], [], [(D, BF16), (D, BF16)])
    (d_ple_proj,) = _mm("ple_dproj", [p], [dpe], [BF16], ta=True, tm=256, tn=2048, tk=1024)
    (d_ple_gate,) = _mm("ple_dgate", [n4], [dpg], [BF16], ta=True, tm=1024, tn=1024, tk=2048)
    (dn4,) = _mm("ple_dn", [dpg], [full["ple_w_gate"]], [F32], tb=True, tm=1024, tn=1024)
    (dh3, dh3b), gs["ple_norm"] = _rms_bwd("ple_dnorm", dn4, h3, row("ple_norm"), dres=dh4, copy_scale=0.5)
    (dh2, dh2b), gs["ffn2_norm"], d_ffn2_g, d_ffn2_u, d_ffn2_d = _ffn_bwd(
        "ffn2", dh3, dh3b, h2, row("ffn2_norm"), full["ffn2_w_gate"], full["ffn2_w_up"], full["ffn2_w_down"],
        ffn2_saved, copy_scale=1.0)
    (dna,) = _mm("w_out_dna", [_to_attn_order(dh2b)], [w_out[:A]], [F32], tb=True, tm=1024, tn=1024)
    (dnb,) = _mm("w_out_dnb", [dh2b], [w_out[A:]], [F32], tb=True, tm=1024, tn=1024)
    (d_wout_a,) = _mm("w_out_dwa", [na], [dh2b], [BF16], ta=True, tm=1024, tn=1024, tk=2048)
    (d_wout_b,) = _mm("w_out_dwb", [nb], [dh2b], [BF16], ta=True, tm=1024, tn=1024, tk=2048)
    d_w_out = jnp.concatenate([d_wout_a, d_wout_b], axis=0)
    (dya,), gs["attn_out_norm"] = _rms_bwd("attn_out_dnorm", dna, ya, row("attn_out_norm"))
    (dyb,), gs["ssm_out_norm"] = _rms_bwd("ssm_out_dnorm", dnb, yb, row("ssm_out_norm"))

    dybp = _ssm_perm(dyb, T)

    def glu_bwd(ins, ps):
        dy, yg_, gl_ = ins
        sg = _sigmoid(gl_)
        dgl = dy * yg_ * sg * (1.0 - sg)
        return [dgl, dy * sg], [jnp.sum(dgl, axis=0, keepdims=True)]

    dgl, dyg_direct, gs["ssm_b_glu"] = _rowwise("ssm_glu_bwd", glu_bwd, [dybp, yg, gl], [], [(W, BF16), (W, F32)], accs=[W])
    (d_w_glu,) = _mm("ssm_dwglu", [ygb], [dgl], [BF16], ta=True, tm=1024, tn=1024, tk=2048)
    (dypre,) = _mm("ssm_dyg", [dgl], [w_glu], [F32], tb=True, extras=[(dyg_direct, "mn"), (ypre, "mn")],
                   epilogue=lambda accs, ex: [(accs[0] + ex[0]) * _gelu_grad(ex[1])], tm=1024, tn=1024, n_split=4)
    dufp, dbb, dcc, da, gs["ssm_d"] = _ssm_bwd(ufp, dypre, bb, bb.transpose(0, 2, 1), cc, cc.transpose(0, 2, 1),
                                               lam_dt, row("ssm_d"), hstart, T)
    ns = gsz * P
    from_bb = lambda t: _block_diag_take(t, gsz).transpose(0, 1, 3, 2).reshape(G * P, C)
    from_cc = lambda t: _block_diag_take(t, gsz).transpose(0, 1, 3, 2).reshape(w["ssm_c_re"].shape)
    gs["ssm_c_re"], gs["ssm_c_im"] = from_cc(dcc[:, :ns]), -from_cc(dcc[:, ns:])
    da = da.sum(axis=1)
    dar, dai = da[:, :ns].reshape(G * P, 1), da[:, ns:].reshape(G * P, 1)
    dlr, dli, dlogdt, dbr, dbi = _ssm_disc_bwd(col("ssm_lambda_re"), col("ssm_lambda_im"), logdt_x, b_re, b_im,
                                               dar, dai, from_bb(dbb[:, :, :ns]), from_bb(dbb[:, :, ns:]))
    gs["ssm_lambda_re"], gs["ssm_lambda_im"] = dlr.reshape(w["ssm_lambda_re"].shape), dli.reshape(w["ssm_lambda_im"].shape)
    gs["ssm_log_dt"] = dlogdt.reshape(G, P).sum(axis=1).reshape(w["ssm_log_dt"].shape)
    gs["ssm_b_re"], gs["ssm_b_im"] = dbr.reshape(w["ssm_b_re"].shape), dbi.reshape(w["ssm_b_im"].shape)
    ds_in = _ssm_unperm(dufp, T)

    dq, dk, dv = _attn_bwd(qkv, dya, ya, lse)
    dqkv = jnp.concatenate([dq, dk, dv], axis=1).astype(BF16)
    (d_w_qkv,) = _mm("w_in_dw_qkv", [n2p], [dqkv], [BF16], ta=True, tm=1024, tn=1024, tk=2048)
    (d_w_s,) = _mm("w_in_dw_ssm", [n2], [ds_in], [BF16], ta=True, tm=1024, tn=1024, tk=2048)
    d_w_in = jnp.concatenate([d_w_qkv, d_w_s], axis=1)
    dz = jnp.concatenate([_from_attn_order(dqkv), ds_in.astype(BF16)], axis=1)
    (dn2,) = _mm("w_in_dn", [dz], [w_in], [F32], tb=True, tm=1024, tn=1024)
    (dh1, dh1b), gs["mix_norm"] = _rms_bwd("mix_dnorm", dn2, h1, row("mix_norm"), dres=dh2, copy_scale=0.5)
    (dx,), gs["ffn1_norm"], d_ffn1_g, d_ffn1_u, d_ffn1_d = _ffn_bwd(
        "ffn1", dh1, dh1b, x, row("ffn1_norm"), full["ffn1_w_gate"], full["ffn1_w_up"], full["ffn1_w_down"],
        ffn1_saved, copy_scale=None)

    big = {"ffn1_w_gate": d_ffn1_g, "ffn1_w_up": d_ffn1_u, "ffn1_w_down": d_ffn1_d, "w_in": d_w_in, "ssm_w_glu": d_w_glu,
           "w_out": d_w_out, "ffn2_w_gate": d_ffn2_g, "ffn2_w_up": d_ffn2_u, "ffn2_w_down": d_ffn2_d,
           "ple_w_gate": d_ple_gate, "ple_w_proj": d_ple_proj}
    small = {k: gs[k].reshape(w[k].shape) for k in SMALL}
    return loss, dx, big, small


def kernel(x, p, ffn1_norm, ffn1_w_gate, ffn1_w_up, ffn1_w_down, mix_norm, w_in, attn_out_norm, ssm_lambda_re, ssm_lambda_im, ssm_log_dt, ssm_b_re, ssm_b_im, ssm_c_re, ssm_c_im, ssm_d, ssm_w_glu, ssm_b_glu, ssm_out_norm, w_out, ffn2_norm, ffn2_w_gate, ffn2_w_up, ffn2_w_down, ple_norm, ple_w_gate, ple_w_proj, final_norm, loss_target, m_ffn1_norm, m_ffn1_w_gate, m_ffn1_w_up, m_ffn1_w_down, m_mix_norm, m_w_in, m_attn_out_norm, m_ssm_lambda_re, m_ssm_lambda_im, m_ssm_log_dt, m_ssm_b_re, m_ssm_b_im, m_ssm_c_re, m_ssm_c_im, m_ssm_d, m_ssm_w_glu, m_ssm_b_glu, m_ssm_out_norm, m_w_out, m_ffn2_norm, m_ffn2_w_gate, m_ffn2_w_up, m_ffn2_w_down, m_ple_norm, m_ple_w_gate, m_ple_w_proj, m_final_norm, v_ffn1_norm, v_ffn1_w_gate, v_ffn1_w_up, v_ffn1_w_down, v_mix_norm, v_w_in, v_attn_out_norm, v_ssm_lambda_re, v_ssm_lambda_im, v_ssm_log_dt, v_ssm_b_re, v_ssm_b_im, v_ssm_c_re, v_ssm_c_im, v_ssm_d, v_ssm_w_glu, v_ssm_b_glu, v_ssm_out_norm, v_w_out, v_ffn2_norm, v_ffn2_w_gate, v_ffn2_w_up, v_ffn2_w_down, v_ple_norm, v_ple_w_gate, v_ple_w_proj, v_final_norm):
    args = locals()
    w = {k: args[k] for k in WEIGHTS}
    m = {k: args["m_" + k] for k in WEIGHTS}
    v = {k: args["v_" + k] for k in WEIGHTS}
    w2 = {k: w[k].reshape(w[k].shape[-2:]) for k, _ in BIG}

    axes = [ax for _, ax in BIG]
    padded = {k: -(-w2[k].shape[ax] // LANES) * LANES for k, ax in BIG}
    shards = [_pad_to(w2[k].astype(BF16), ax, padded[k]) for k, ax in BIG]
    full = dict(zip([k for k, _ in BIG], _all_gather_weights(shards, axes)))

    loss_local, dx, gbig, gsmall = _local_step(x[0], p[0, 0], loss_target[0], w, full)
    loss = lax.psum(loss_local, MESH_AXES)

    summed = _reduce_scatter([gbig[k] for k, _ in BIG], axes)
    n_small = sum(w[k].size for k in SMALL)
    rows = -(-n_small // (SUBLANES * LANES)) * SUBLANES
    gs_sum = _all_reduce_small(_pack([gsmall[k] for k in SMALL], rows))

    grads, delta, new_m, new_v = {}, {}, {}, {}
    for (k, ax), gfull in zip(BIG, summed):
        g2 = lax.slice_in_dim(gfull, 0, w2[k].shape[ax], axis=ax)
        d2, nm2, nv2 = _adamw("adamw_" + k, w2[k], g2, m[k].reshape(w2[k].shape), v[k].reshape(w2[k].shape))
        grads[k], delta[k], new_m[k], new_v[k] = (t.reshape(w[k].shape) for t in (g2, d2, nm2, nv2))
    small_like = [w[k] for k in SMALL]
    ds, nms, nvs = _adamw("adamw_small", _pack(small_like, rows), gs_sum, _pack([m[k] for k in SMALL], rows),
                          _pack([v[k] for k in SMALL], rows))
    for k, g_, d_, nm_, nv_ in zip(SMALL, _unpack(gs_sum, small_like), _unpack(ds, small_like),
                                   _unpack(nms, small_like), _unpack(nvs, small_like)):
        grads[k], delta[k], new_m[k], new_v[k] = g_, d_, nm_, nv_

    return (loss, dx[None], *[grads[k] for k in WEIGHTS], *[delta[k] for k in WEIGHTS],
            *[new_m[k] for k in WEIGHTS], *[new_v[k] for k in WEIGHTS])
```

```python
import functools
import math

import jax
import jax.numpy as jnp
from jax import lax
from jax.experimental import pallas as pl
from jax.experimental.pallas import tpu as pltpu

F32 = jnp.float32
BF16 = jnp.bfloat16
MESH = pl.DeviceIdType.MESH
MESH_AXES = ("x", "y", "c")
N_CHIPS = 4
N_DEV = 8

V7X_VMEM_LIMIT_BYTES = 56 << 20
LANES = 128
SUBLANES = 8

HEAD_DIM = 64
SWA_BLOCK = 128
DILATIONS = (1, 4, 16)
SSM_BLOCK_GROUPS = 8
NORM_EPS = 1e-6
MASK_VALUE = -1e30

ADAM_LR = 0.001
ADAM_B1 = 0.9
ADAM_B2 = 0.999
ADAM_EPS = 1e-08
ADAM_WD = 0.01
ADAM_STEP = 10

GELU_C = math.sqrt(2.0 / math.pi)
GELU_K = 0.044715


def _pcall(body, **kw):
    return pl.pallas_call(body, **kw)


def _params(*sem):
    return pltpu.CompilerParams(dimension_semantics=sem, vmem_limit_bytes=V7X_VMEM_LIMIT_BYTES)


def _tile(n, target, align):
    best = None
    for t in range(align, min(n, target) + 1, align):
        if n % t == 0:
            best = t
    return n if best is None else best


def _sigmoid(x):
    return 1.0 / (1.0 + jnp.exp(-x))


class _Host:
    def __init__(self, ins, out_shapes, n_sem, start, wait):
        self.ins, self.out_shapes, self.n_sem, self.start, self.wait = ins, out_shapes, n_sem, start, wait


def _mm(name, lhs, rhs, outs, pairs=((0, 0, 0),), epilogue=None, extras=(), ta=False, tb=False,
        tm=1024, tn=512, tk=2048, host=None):
    nl, nr, ne, no = len(lhs), len(rhs), len(extras), len(outs)
    nhi, nho = (len(host.ins), len(host.out_shapes)) if host else (0, 0)
    n_acc = 1 + max(p[2] for p in pairs)
    (K, M) = lhs[0].shape if ta else lhs[0].shape[::-1]
    (N, K2) = rhs[0].shape if tb else rhs[0].shape[::-1]
    assert K == K2, (name, lhs[0].shape, rhs[0].shape)
    tm, tn, tk = _tile(M, tm, LANES), _tile(N, tn, LANES), _tile(K, tk, LANES)
    ni, nj, nk = M // tm, N // tn, K // tk
    n_scr = n_acc if nk > 1 else 0
    if epilogue is None:
        epilogue = lambda accs, ex: accs
    dn = (((0 if ta else 1,), (1 if tb else 0,)), ((), ()))

    def body(*refs):
        refs = list(refs)
        take = lambda n: [refs.pop(0) for _ in range(n)]
        l, r, e, hin, o, hout, acc = take(nl), take(nr), take(ne), take(nhi), take(no), take(nho), take(n_scr)
        i, j, k = pl.program_id(0), pl.program_id(1), pl.program_id(2)
        if host:
            @pl.when((i == 0) & (j == 0) & (k == 0))
            def _():
                host.start(hin, hout, *refs)

        parts = [None] * n_acc
        for li, ri, ai in pairs:
            d = lax.dot_general(l[li][...].astype(BF16), r[ri][...].astype(BF16), dn,
                                preferred_element_type=F32)
            parts[ai] = d if parts[ai] is None else parts[ai] + d

        def finish(accs):
            res = epilogue(accs, [x[...] for x in e])
            for ref, val in zip(o, res):
                ref[...] = val.astype(ref.dtype)

        if nk == 1:
            finish(parts)
        else:
            @pl.when(k == 0)
            def _():
                for ai in range(n_acc):
                    acc[ai][...] = parts[ai]

            @pl.when(k > 0)
            def _():
                for ai in range(n_acc):
                    acc[ai][...] += parts[ai]

            @pl.when(k == nk - 1)
            def _():
                finish([a[...] for a in acc])

        if host:
            @pl.when((i == ni - 1) & (j == nj - 1) & (k == nk - 1))
            def _():
                host.wait(hin, hout, *refs)

    lspec = pl.BlockSpec((tk, tm), lambda i, j, k: (k, i)) if ta else pl.BlockSpec((tm, tk), lambda i, j, k: (i, k))
    rspec = pl.BlockSpec((tn, tk), lambda i, j, k: (j, k)) if tb else pl.BlockSpec((tk, tn), lambda i, j, k: (k, j))
    especs = []
    for arr, kind in extras:
        if kind == "mn":
            especs.append(pl.BlockSpec((tm, tn), lambda i, j, k: (i, j)))
        elif kind == "n":
            especs.append(pl.BlockSpec((1, tn), lambda i, j, k: (0, j)))
        else:
            especs.append(pl.BlockSpec((tm, 1), lambda i, j, k: (i, 0)))
    any_spec = pl.BlockSpec(memory_space=pl.ANY)
    sems = [pltpu.SemaphoreType.DMA((host.n_sem,)), pltpu.SemaphoreType.DMA((host.n_sem,))] if host else []
    res = _pcall(
        body, name=name,
        grid=(ni, nj, nk),
        in_specs=[lspec] * nl + [rspec] * nr + especs + [any_spec] * nhi,
        out_specs=[pl.BlockSpec((tm, tn), lambda i, j, k: (i, j))] * no + [any_spec] * nho,
        out_shape=[jax.ShapeDtypeStruct((M, N), dt) for dt in outs] + (list(host.out_shapes) if host else []),
        scratch_shapes=[pltpu.VMEM((tm, tn), F32)] * n_scr + sems,
        compiler_params=_params(*(("arbitrary",) * 3 if host else ("parallel", "parallel", "arbitrary"))),
    )(*lhs, *rhs, *[a for a, _ in extras], *(host.ins if host else []))
    return res


def _rowwise(name, fn, ins, params, outs, accs=(), ts=256):
    S = ins[0].shape[0]
    ts = _tile(S, ts, 16)
    ni, npar, no, na = len(ins), len(params), len(outs), len(accs)

    def body(*refs):
        i_refs, p_refs = refs[:ni], refs[ni:ni + npar]
        o_refs = refs[ni + npar:ni + npar + no]
        a_refs = refs[ni + npar + no:]
        res_o, res_a = fn([r[...] for r in i_refs], [r[...] for r in p_refs])
        for ref, val in zip(o_refs, res_o):
            ref[...] = val.astype(ref.dtype)
        if na:
            @pl.when(pl.program_id(0) == 0)
            def _():
                for ref in a_refs:
                    ref[...] = jnp.zeros(ref.shape, F32)

            for ref, val in zip(a_refs, res_a):
                ref[...] += val

    res = _pcall(
        body, name=name,
        grid=(S // ts,),
        in_specs=[pl.BlockSpec((ts, a.shape[1]), lambda i: (i, 0)) for a in ins]
        + [pl.BlockSpec(p.shape, lambda i: (0, 0)) for p in params],
        out_specs=[pl.BlockSpec((ts, w), lambda i: (i, 0)) for w, _ in outs]
        + [pl.BlockSpec((1, w), lambda i: (0, 0)) for w in accs],
        out_shape=[jax.ShapeDtypeStruct((S, w), dt) for w, dt in outs]
        + [jax.ShapeDtypeStruct((1, w), F32) for w in accs],
        compiler_params=_params("arbitrary"),
    )(*ins, *params)
    return res


def _xhat(x):
    r = lax.rsqrt(jnp.mean(x * x, axis=-1, keepdims=True) + NORM_EPS)
    return x * r, r


def _rms_fwd(name, x, g):
    def fn(ins, ps):
        xh, _ = _xhat(ins[0])
        return [xh * ps[0]], []

    return _rowwise(name, fn, [x], [g], [(x.shape[1], BF16)])[0]


def _rms_bwd(name, dn, x, g, dres=None, copy_scale=None):
    w = x.shape[1]

    def fn(ins, ps):
        dn_, x_ = ins[0], ins[1]
        xh, r = _xhat(x_)
        dxh = dn_ * ps[0]
        dx = r * (dxh - xh * jnp.mean(dxh * xh, axis=-1, keepdims=True))
        if dres is not None:
            dx = dx + ins[2]
        o = [dx] + ([dx * copy_scale] if copy_scale is not None else [])
        return o, [jnp.sum(dn_ * xh, axis=0, keepdims=True)]

    ins = [dn, x] + ([dres] if dres is not None else [])
    outs = [(w, F32)] + ([(w, BF16)] if copy_scale is not None else [])
    res = _rowwise(name, fn, ins, [g], outs, accs=[w])
    return res[:-1], res[-1]


def _swiglu_epilogue(accs, ex):
    g, u = accs
    return [g, u, g * _sigmoid(g) * u]


def _dswiglu_epilogue(accs, ex):
    da = accs[0]
    g, u = ex[0].astype(F32), ex[1].astype(F32)
    sg = _sigmoid(g)
    return [da * u * (sg * (1.0 + g * (1.0 - sg))), da * (g * sg)]


def _ffn_fwd(tag, h, gnorm, wg, wu, wd, hosts=(None, None)):
    n = _rms_fwd(tag + "_norm", h, gnorm)
    g, u, a, *ho_up = _mm(tag + "_up", [n], [wg, wu], [BF16, BF16, BF16], pairs=((0, 0, 0), (0, 1, 1)),
                          epilogue=_swiglu_epilogue, tm=1024, tn=512, host=hosts[0])
    hout, *ho_down = _mm(tag + "_down", [a], [wd], [F32], extras=[(h, "mn")],
                         epilogue=lambda accs, ex: [ex[0] + 0.5 * accs[0]], tm=512, tn=1024, tk=8192, host=hosts[1])
    return hout, (n, g, u, a), (ho_up, ho_down)


def _ffn_bwd(tag, dh, dhb_half, h, gnorm, wg, wu, wd, saved, copy_scale, hosts=(None, None)):
    n, g, u, a = saved
    dg, du = _mm(tag + "_dact", [dhb_half], [wd], [BF16, BF16], tb=True, extras=[(g, "mn"), (u, "mn")],
                 epilogue=_dswiglu_epilogue, tm=1024, tn=512)
    (dwd,) = _mm(tag + "_dwd", [a], [dhb_half], [BF16], ta=True, tm=512, tn=2048, tk=2048)
    dwg, dwu, *ho_w = _mm(tag + "_dwgu", [n], [dg, du], [BF16, BF16], pairs=((0, 0, 0), (0, 1, 1)), ta=True,
                          tm=1024, tn=512, tk=2048, host=hosts[0])
    dn, *ho_n = _mm(tag + "_dn", [dg, du], [wg, wu], [F32], pairs=((0, 0, 0), (1, 1, 0)), tb=True,
                    tm=1024, tn=1024, tk=1408, host=hosts[1])
    douts, dgn = _rms_bwd(tag + "_dnorm", dn, h, gnorm, dres=dh, copy_scale=copy_scale)
    return douts, dgn, dwg, dwu, dwd, (ho_w, ho_n)


ATTN_HEAD_PAIRS = 8


def _to_attn_order(a):
    S, w = a.shape
    return a.reshape(S // 16, 16, w).transpose(1, 0, 2).reshape(S, w)


def _from_attn_order(a):
    S, w = a.shape
    return a.reshape(16, S // 16, w).transpose(1, 0, 2).reshape(S, w)


def _attn_geom(S, d):
    s16 = S // 16
    if d == 16:
        return (16, s16), (1, SWA_BLOCK), (lambda r, b: (r, b)), 16, s16 // SWA_BLOCK
    if d == 4:
        return (4, 4, s16), (4, 1, SWA_BLOCK // 4), (lambda r, b: (0, r, b)), 4, s16 // (SWA_BLOCK // 4)
    return (16, s16), (16, SWA_BLOCK // 16), (lambda r, b: (0, b)), 1, s16 // (SWA_BLOCK // 16)


def _attn_pos(rho, d):
    if d == 16:
        return rho
    if d == 4:
        return 4 * (rho & 31) + (rho >> 5)
    return 16 * (rho & 7) + (rho >> 3)


def _attn_spec(S, d, lb, col, shift=0):
    _, blk, idx, _, nb = _attn_geom(S, d)
    return pl.BlockSpec(blk + (lb,), lambda r, cb, b: idx(r, jnp.clip(b + shift, 0, nb - 1)) + (col(cb),))


def _attn_view(a, d):
    return a.reshape(_attn_geom(a.shape[0], d)[0] + (a.shape[1],))


def _attn_valid(d):
    qp = _attn_pos(lax.broadcasted_iota(jnp.int32, (SWA_BLOCK, 2 * SWA_BLOCK), 0), d)
    kk = lax.broadcasted_iota(jnp.int32, (SWA_BLOCK, 2 * SWA_BLOCK), 1)
    kp = _attn_pos(kk & (SWA_BLOCK - 1), d)
    is_prev = kk < SWA_BLOCK
    return qp, kp, is_prev


def _head_masks(rows=SWA_BLOCK):
    lane = lax.broadcasted_iota(jnp.int32, (rows, LANES), 1)
    return [lane < HEAD_DIM, lane >= HEAD_DIM]


def _attn_ld(ref, sl):
    t = ref[(slice(None),) * (len(ref.shape) - 1) + (sl,)]
    return t.reshape(-1, t.shape[-1])


def _attn_st(ref, sl, val):
    ref[(slice(None),) * (len(ref.shape) - 1) + (sl,)] = val.reshape(ref.shape[:-1] + (val.shape[-1],))


def _per_head(t, first):
    sw = pltpu.roll(t, HEAD_DIM, 1)
    lo = lax.broadcasted_iota(jnp.int32, t.shape, 1) < HEAD_DIM
    return jnp.where(lo, t, sw) if first else jnp.where(lo, sw, t)


def _dot_nt(a, b):
    return lax.dot_general(a, b, (((1,), (1,)), ((), ())), preferred_element_type=F32)


def _dot_tn(a, b):
    return lax.dot_general(a, b, (((0,), (0,)), ((), ())), preferred_element_type=F32)


def _dot(a, b):
    return jnp.dot(a, b, preferred_element_type=F32)


def _keep(mask, t):
    return jnp.where(mask, t.astype(F32), 0.0).astype(BF16)


def _attn_cols(A):
    lb = min(A, LANES * ATTN_HEAD_PAIRS)
    ncol = A // lb
    return lb, ncol, [lambda cb, part=part: part * ncol + cb for part in range(3)], (lambda cb: cb)


def _attn_fwd_stage(name, qkv, d, prev, final):
    S, A3 = qkv.shape
    A = A3 // 3
    lb, ncol, (cq, ck, cv), ca = _attn_cols(A)
    view, _, _, nres, nb = _attn_geom(S, d)
    scale = HEAD_DIM ** -0.5
    has_prev = prev is not None

    def body(*refs):
        q_ref, kp_ref, kc_ref, vp_ref, vc_ref = refs[:5]
        p_refs = refs[5:8] if has_prev else ()
        o_refs = refs[5 + len(p_refs):]
        b = pl.program_id(2)
        qp, kp_, is_prev = _attn_valid(d)
        valid = (is_prev & (kp_ >= qp) & (b > 0)) | (jnp.logical_not(is_prev) & (kp_ <= qp))
        hm, hm2 = _head_masks(), _head_masks(2 * SWA_BLOCK)
        for hp in range(lb // LANES):
            sl = slice(hp * LANES, (hp + 1) * LANES)
            q = _attn_ld(q_ref, sl)
            k2 = jnp.concatenate([_attn_ld(kp_ref, sl), _attn_ld(kc_ref, sl)], axis=0).astype(BF16)
            v2 = jnp.concatenate([_attn_ld(vp_ref, sl), _attn_ld(vc_ref, sl)], axis=0)
            o = jnp.zeros((SWA_BLOCK, LANES), F32)
            m = jnp.zeros((SWA_BLOCK, LANES), F32)
            l = jnp.zeros((SWA_BLOCK, LANES), F32)
            for hh in range(2):
                s = jnp.where(valid, _dot_nt(_keep(hm[hh], q), k2) * scale, MASK_VALUE)
                mh = jnp.max(s, axis=-1, keepdims=True)
                p = jnp.exp(s - mh)
                lh = jnp.sum(p, axis=-1, keepdims=True)
                o = o + _dot(p.astype(BF16), _keep(hm2[hh], v2))
                m = jnp.where(hm[hh], mh, m)
                l = jnp.where(hm[hh], lh, l)
            if has_prev:
                po, pm, pl_ = (_attn_ld(r, sl) for r in p_refs)
                mn = jnp.maximum(m, pm)
                w_new, w_old = jnp.exp(m - mn), jnp.exp(pm - mn)
                o = o * w_new + po * w_old
                l = l * w_new + pl_ * w_old
                m = mn
            if final:
                _attn_st(o_refs[0], sl, o / l)
                _attn_st(o_refs[1], sl, m + jnp.log(l))
            else:
                _attn_st(o_refs[0], sl, o)
                _attn_st(o_refs[1], sl, m)
                _attn_st(o_refs[2], sl, l)

    n_out = 2 if final else 3
    qk = _attn_view(qkv, d)
    prev_v = [_attn_view(t, d) for t in prev] if has_prev else []
    sp = functools.partial(_attn_spec, S, d, lb)
    res = _pcall(
        body, name=name,
        grid=(nres, ncol, nb),
        in_specs=[sp(cq), sp(ck, -1), sp(ck), sp(cv, -1), sp(cv)] + [sp(ca)] * len(prev_v),
        out_specs=[sp(ca)] * n_out,
        out_shape=[jax.ShapeDtypeStruct(view + (A,), F32)] * n_out,
        compiler_params=_params("parallel", "parallel", "arbitrary"),
    )(qk, qk, qk, qk, qk, *prev_v)
    return [t.reshape(S, A) for t in res]


def _attn_fwd(qkv):
    st = None
    for i, d in enumerate(DILATIONS):
        st = _attn_fwd_stage("attn_fwd_d%d" % d, qkv, d, st, final=(i == len(DILATIONS) - 1))
    return st


def _attn_dq_stage(name, qkv, do, lse, delta, d, prev):
    S, A3 = qkv.shape
    A = A3 // 3
    lb, ncol, (cq, ck, cv), ca = _attn_cols(A)
    view, _, _, nres, nb = _attn_geom(S, d)
    scale = HEAD_DIM ** -0.5
    has_prev = prev is not None

    def body(*refs):
        q_ref, kp_ref, kc_ref, vp_ref, vc_ref, do_ref, lse_ref, dl_ref = refs[:8]
        b = pl.program_id(2)
        qp, kp_, is_prev = _attn_valid(d)
        valid = (is_prev & (kp_ >= qp) & (b > 0)) | (jnp.logical_not(is_prev) & (kp_ <= qp))
        hm, hm2 = _head_masks(), _head_masks(2 * SWA_BLOCK)
        for hp in range(lb // LANES):
            sl = slice(hp * LANES, (hp + 1) * LANES)
            q, do_, lse_, dl_ = (_attn_ld(r, sl) for r in (q_ref, do_ref, lse_ref, dl_ref))
            k2 = jnp.concatenate([_attn_ld(kp_ref, sl), _attn_ld(kc_ref, sl)], axis=0)
            v2 = jnp.concatenate([_attn_ld(vp_ref, sl), _attn_ld(vc_ref, sl)], axis=0).astype(BF16)
            k2b = k2.astype(BF16)
            dq = jnp.zeros((SWA_BLOCK, LANES), F32)
            for hh in range(2):
                doh = _keep(hm[hh], do_)
                lh, dh = _per_head(lse_, hh == 0), _per_head(dl_, hh == 0)
                lh2, dh2 = jnp.concatenate([lh, lh], axis=1), jnp.concatenate([dh, dh], axis=1)
                s = _dot_nt(_keep(hm[hh], q), k2b) * scale
                p = jnp.where(valid, jnp.exp(s - lh2), 0.0)
                ds = p * (_dot_nt(doh, v2) - dh2)
                dq = dq + _dot(ds.astype(BF16), _keep(hm2[hh], k2))
            dq = dq * scale
            if has_prev:
                dq = dq + _attn_ld(refs[8], sl)
            _attn_st(refs[-1], sl, dq)

    qk = _attn_view(qkv, d)
    acts = [_attn_view(t, d) for t in (do, lse, delta)] + ([_attn_view(prev, d)] if has_prev else [])
    sp = functools.partial(_attn_spec, S, d, lb)
    res = _pcall(
        body, name=name,
        grid=(nres, ncol, nb),
        in_specs=[sp(cq), sp(ck, -1), sp(ck), sp(cv, -1), sp(cv)] + [sp(ca)] * len(acts),
        out_specs=sp(ca),
        out_shape=jax.ShapeDtypeStruct(view + (A,), F32),
        compiler_params=_params("parallel", "parallel", "arbitrary"),
    )(qk, qk, qk, qk, qk, *acts)
    return res.reshape(S, A)


def _attn_dkv_stage(name, qkv, do, lse, delta, d, prev):
    S, A3 = qkv.shape
    A = A3 // 3
    lb, ncol, (cq, ck, cv), ca = _attn_cols(A)
    view, _, _, nres, nb = _attn_geom(S, d)
    scale = HEAD_DIM ** -0.5
    has_prev = prev is not None

    def body(*refs):
        k_ref, v_ref, qc_ref, qn_ref, doc_ref, don_ref, lc_ref, ln_ref, dc_ref, dn_ref = refs[:10]
        j = pl.program_id(2)
        rr = lax.broadcasted_iota(jnp.int32, (2 * SWA_BLOCK, SWA_BLOCK), 0)
        qp = _attn_pos(rr & (SWA_BLOCK - 1), d)
        kp_ = _attn_pos(lax.broadcasted_iota(jnp.int32, (2 * SWA_BLOCK, SWA_BLOCK), 1), d)
        valid = ((rr < SWA_BLOCK) & (kp_ <= qp)) | ((rr >= SWA_BLOCK) & (kp_ >= qp) & (j < nb - 1))
        hm2 = _head_masks(2 * SWA_BLOCK)
        for hp in range(lb // LANES):
            sl = slice(hp * LANES, (hp + 1) * LANES)
            kb, vb = _attn_ld(k_ref, sl).astype(BF16), _attn_ld(v_ref, sl).astype(BF16)
            q2 = jnp.concatenate([_attn_ld(qc_ref, sl), _attn_ld(qn_ref, sl)], axis=0)
            do2 = jnp.concatenate([_attn_ld(doc_ref, sl), _attn_ld(don_ref, sl)], axis=0)
            l2 = jnp.concatenate([_attn_ld(lc_ref, sl), _attn_ld(ln_ref, sl)], axis=0)
            d2 = jnp.concatenate([_attn_ld(dc_ref, sl), _attn_ld(dn_ref, sl)], axis=0)
            dk = jnp.zeros((SWA_BLOCK, LANES), F32)
            dv = jnp.zeros((SWA_BLOCK, LANES), F32)
            for hh in range(2):
                qh, doh = _keep(hm2[hh], q2), _keep(hm2[hh], do2)
                lh, dh = _per_head(l2, hh == 0), _per_head(d2, hh == 0)
                s = _dot_nt(qh, kb) * scale
                p = jnp.where(valid, jnp.exp(s - lh), 0.0)
                dv = dv + _dot_tn(p.astype(BF16), doh)
                ds = p * (_dot_nt(doh, vb) - dh)
                dk = dk + _dot_tn(ds.astype(BF16), qh)
            dk = dk * scale
            if has_prev:
                dk = dk + _attn_ld(refs[10], sl)
                dv = dv + _attn_ld(refs[11], sl)
            _attn_st(refs[-2], sl, dk)
            _attn_st(refs[-1], sl, dv)

    qk = _attn_view(qkv, d)
    acts = [_attn_view(t, d) for t in (do, lse, delta)]
    prev_v = [_attn_view(t, d) for t in prev] if has_prev else []
    sp = functools.partial(_attn_spec, S, d, lb)
    res = _pcall(
        body, name=name,
        grid=(nres, ncol, nb),
        in_specs=[sp(ck), sp(cv), sp(cq), sp(cq, 1), sp(ca), sp(ca, 1), sp(ca), sp(ca, 1), sp(ca), sp(ca, 1)]
        + [sp(ca)] * len(prev_v),
        out_specs=[sp(ca), sp(ca)],
        out_shape=[jax.ShapeDtypeStruct(view + (A,), F32)] * 2,
        compiler_params=_params("parallel", "parallel", "arbitrary"),
    )(qk, qk, qk, qk, acts[0], acts[0], acts[1], acts[1], acts[2], acts[2], *prev_v)
    return [t.reshape(S, A) for t in res]


def _attn_delta(dya, ya):
    S, A = ya.shape
    ri = lax.broadcasted_iota(jnp.int32, (A, A), 0) // HEAD_DIM
    ci = lax.broadcasted_iota(jnp.int32, (A, A), 1) // HEAD_DIM
    ones_bd = (ri == ci).astype(BF16)

    def fn(ins, ps):
        prod = ins[0] * ins[1]
        hi = prod.astype(BF16)
        lo = (prod - hi.astype(F32)).astype(BF16)
        return [_dot(hi, ps[0]) + _dot(lo, ps[0])], []

    return _rowwise("attn_delta", fn, [dya, ya], [ones_bd], [(A, F32)])[0]


def _attn_bwd(qkv, dya, ya, lse):
    delta = _attn_delta(dya, ya)
    dq, dkv = None, None
    for d in DILATIONS:
        dq = _attn_dq_stage("attn_dq_d%d" % d, qkv, dya, lse, delta, d, dq)
        dkv = _attn_dkv_stage("attn_dkv_d%d" % d, qkv, dya, lse, delta, d, dkv)
    return dq, dkv[0], dkv[1]


def _ssm_perm(a, T):
    S, w = a.shape
    return a.reshape(S // T, SUBLANES, T // SUBLANES, w).transpose(0, 2, 1, 3).reshape(S, w)


def _ssm_unperm(a, T):
    S, w = a.shape
    return a.reshape(S // T, T // SUBLANES, SUBLANES, w).transpose(0, 2, 1, 3).reshape(S, w)


def _ssm_powers(lam_ref, pw_ref, T, ns):
    n = (lax.broadcasted_iota(jnp.int32, (T, 1), 0) // SUBLANES + 1).astype(F32)
    mag = jnp.exp(n * lam_ref[0, 0:1, :])
    ang = n * lam_ref[0, 1:2, :]
    pw_ref[:, 0:ns] = mag * jnp.cos(ang)
    pw_ref[:, ns:2 * ns] = mag * jnp.sin(ang)


def _ssm_scan(xs, off, pw_ref, carry_ref, T, ns, reverse):
    Tc = T // SUBLANES
    sgn = -1.0 if reverse else 1.0
    ar, ai = pw_ref[0:SUBLANES, 0:ns], sgn * pw_ref[0:SUBLANES, ns:2 * ns]

    def rows(i):
        return pl.ds(pl.multiple_of(off + i * SUBLANES, SUBLANES), SUBLANES)

    def step(k, h):
        hr, hi = h
        r = rows(Tc - 1 - k if reverse else k)
        nr = ar * hr - ai * hi + xs[r, 0:ns]
        ni = ar * hi + ai * hr + xs[r, ns:2 * ns]
        xs[r, 0:ns] = nr
        xs[r, ns:2 * ns] = ni
        return nr, ni

    z = jnp.zeros((SUBLANES, ns), F32)
    er, ei = lax.fori_loop(0, Tc, step, (z, z))
    atr, ati = pw_ref[T - SUBLANES:T, 0:ns], sgn * pw_ref[T - SUBLANES:T, ns:2 * ns]
    rowid = lax.broadcasted_iota(jnp.int32, (SUBLANES, ns), 0)
    cr, ci = carry_ref[:, 0:ns], carry_ref[:, ns:2 * ns]
    ctr, cti = z, z
    for jj in range(SUBLANES):
        j = SUBLANES - 1 - jj if reverse else jj
        sel = rowid == j
        ctr, cti = jnp.where(sel, cr, ctr), jnp.where(sel, ci, cti)
        ejr = jnp.broadcast_to(jnp.sum(jnp.where(sel, er, 0.0), axis=0, keepdims=True), (SUBLANES, ns))
        eji = jnp.broadcast_to(jnp.sum(jnp.where(sel, ei, 0.0), axis=0, keepdims=True), (SUBLANES, ns))
        cr, ci = ejr + atr * cr - ati * ci, eji + atr * ci + ati * cr
    carry_ref[:, 0:ns] = cr
    carry_ref[:, ns:2 * ns] = ci

    def fix(i, _):
        r = rows(i)
        pr_rows = pl.ds(pl.multiple_of((Tc - 1 - i if reverse else i) * SUBLANES, SUBLANES), SUBLANES)
        pr, pi = pw_ref[pr_rows, 0:ns], sgn * pw_ref[pr_rows, ns:2 * ns]
        xs[r, 0:ns] += pr * ctr - pi * cti
        xs[r, ns:2 * ns] += pr * cti + pi * ctr
        return 0

    lax.fori_loop(0, Tc, fix, 0)
    return ctr, cti


def _ssm_fwd(ufp, bb, cc, lam_dt, drow, T):
    S, W = ufp.shape
    GB, cw, ns2 = bb.shape
    ns = ns2 // 2
    NCH = S // T

    def body(uf_ref, bb_ref, cc_ref, lam_ref, d_ref, y_ref, hs_ref, xs, pw, carry):
        @pl.when(pl.program_id(1) == 0)
        def _():
            _ssm_powers(lam_ref, pw, T, ns)
            carry[...] = jnp.zeros(carry.shape, F32)

        uf = uf_ref[...]
        xs[...] = _dot(uf.astype(BF16), bb_ref[0])
        hs_ref[0, 0] = carry[...]
        _ssm_scan(xs, 0, pw, carry, T, ns, reverse=False)
        y_ref[...] = _dot(xs[...].astype(BF16), cc_ref[0]) + d_ref[...] * uf

    return _pcall(
        body, name="ssm_fwd",
        grid=(GB, NCH),
        in_specs=[pl.BlockSpec((T, cw), lambda g, c: (c, g)),
                  pl.BlockSpec((1, cw, ns2), lambda g, c: (g, 0, 0)),
                  pl.BlockSpec((1, ns2, cw), lambda g, c: (g, 0, 0)),
                  pl.BlockSpec((1, 2, ns), lambda g, c: (g, 0, 0)),
                  pl.BlockSpec((1, cw), lambda g, c: (0, g))],
        out_specs=[pl.BlockSpec((T, cw), lambda g, c: (c, g)),
                   pl.BlockSpec((1, 1, SUBLANES, ns2), lambda g, c: (g, c, 0, 0))],
        out_shape=[jax.ShapeDtypeStruct((S, W), F32),
                   jax.ShapeDtypeStruct((GB, NCH, SUBLANES, ns2), F32)],
        scratch_shapes=[pltpu.VMEM((T, ns2), F32), pltpu.VMEM((T, ns2), F32), pltpu.VMEM((SUBLANES, ns2), F32)],
        compiler_params=_params("arbitrary", "arbitrary"),
    )(ufp, bb, cc, lam_dt, drow)


def _ssm_bwd(ufp, dyp, bb, bbt, cc, cct, lam_dt, drow, hstart, T):
    S, W = ufp.shape
    GB, cw, ns2 = bb.shape
    ns = ns2 // 2
    NCH = S // T

    def body(uf_ref, dy_ref, bb_ref, bbt_ref, cc_ref, cct_ref, lam_ref, d_ref, hs_ref,
             duf_ref, dbb_ref, dcc_ref, da_ref, dd_ref, hb, ls, pw, carry_f, carry_b):
        @pl.when(pl.program_id(1) == 0)
        def _():
            _ssm_powers(lam_ref, pw, T, ns)
            carry_b[...] = jnp.zeros(carry_b.shape, F32)
            dbb_ref[...] = jnp.zeros(dbb_ref.shape, F32)
            dcc_ref[...] = jnp.zeros(dcc_ref.shape, F32)
            da_ref[...] = jnp.zeros(da_ref.shape, F32)
            dd_ref[...] = jnp.zeros(dd_ref.shape, F32)

        uf, dy = uf_ref[...], dy_ref[...]
        ufb, dyb = uf.astype(BF16), dy.astype(BF16)
        hb[SUBLANES:T + SUBLANES, :] = _dot(ufb, bb_ref[0])
        carry_f[...] = hs_ref[0, 0]
        ctr, cti = _ssm_scan(hb, SUBLANES, pw, carry_f, T, ns, reverse=False)
        hb[0:SUBLANES, 0:ns] = ctr
        hb[0:SUBLANES, ns:ns2] = cti
        ls[...] = _dot(dyb, cct_ref[0])
        _ssm_scan(ls, 0, pw, carry_b, T, ns, reverse=True)
        lv = ls[...]
        lb = lv.astype(BF16)
        dbb_ref[0] += _dot_tn(ufb, lb)
        dcc_ref[0] += _dot_tn(hb[SUBLANES:T + SUBLANES, :].astype(BF16), dyb)
        lr, li = lv[:, 0:ns], lv[:, ns:ns2]
        hpr, hpi = hb[0:T, 0:ns], hb[0:T, ns:ns2]
        dar = jnp.sum(lr * hpr + li * hpi, axis=0, keepdims=True)
        dai = jnp.sum(li * hpr - lr * hpi, axis=0, keepdims=True)
        da_ref[0, 0:1, 0:ns] += dar
        da_ref[0, 0:1, ns:ns2] += dai
        duf_ref[...] = _dot(lb, bbt_ref[0]) + d_ref[...] * dy
        dd_ref[...] += jnp.sum(dy * uf, axis=0, keepdims=True)

    rc = lambda c: NCH - 1 - c
    return _pcall(
        body, name="ssm_bwd",
        grid=(GB, NCH),
        in_specs=[pl.BlockSpec((T, cw), lambda g, c: (rc(c), g)),
                  pl.BlockSpec((T, cw), lambda g, c: (rc(c), g)),
                  pl.BlockSpec((1, cw, ns2), lambda g, c: (g, 0, 0)),
                  pl.BlockSpec((1, ns2, cw), lambda g, c: (g, 0, 0)),
                  pl.BlockSpec((1, ns2, cw), lambda g, c: (g, 0, 0)),
                  pl.BlockSpec((1, cw, ns2), lambda g, c: (g, 0, 0)),
                  pl.BlockSpec((1, 2, ns), lambda g, c: (g, 0, 0)),
                  pl.BlockSpec((1, cw), lambda g, c: (0, g)),
                  pl.BlockSpec((1, 1, SUBLANES, ns2), lambda g, c: (g, rc(c), 0, 0))],
        out_specs=[pl.BlockSpec((T, cw), lambda g, c: (rc(c), g)),
                   pl.BlockSpec((1, cw, ns2), lambda g, c: (g, 0, 0)),
                   pl.BlockSpec((1, ns2, cw), lambda g, c: (g, 0, 0)),
                   pl.BlockSpec((1, SUBLANES, ns2), lambda g, c: (g, 0, 0)),
                   pl.BlockSpec((1, cw), lambda g, c: (0, g))],
        out_shape=[jax.ShapeDtypeStruct((S, W), F32),
                   jax.ShapeDtypeStruct((GB, cw, ns2), F32),
                   jax.ShapeDtypeStruct((GB, ns2, cw), F32),
                   jax.ShapeDtypeStruct((GB, SUBLANES, ns2), F32),
                   jax.ShapeDtypeStruct((1, W), F32)],
        scratch_shapes=[pltpu.VMEM((T + SUBLANES, ns2), F32), pltpu.VMEM((T, ns2), F32), pltpu.VMEM((T, ns2), F32),
                        pltpu.VMEM((SUBLANES, ns2), F32), pltpu.VMEM((SUBLANES, ns2), F32)],
        compiler_params=_params("arbitrary", "arbitrary"),
    )(ufp, dyp, bb, bbt, cc, cct, lam_dt, drow, hstart)


def _ssm_disc_math(lr, li, logdt, br, bi):
    dt = jnp.exp(logdt)
    mag = jnp.exp(lr * dt)
    ar = mag * jnp.cos(li * dt)
    ai = mag * jnp.sin(li * dt)
    nr, ni = ar - 1.0, ai
    den = lr * lr + li * li
    cr = (nr * lr + ni * li) / den
    ci = (ni * lr - nr * li) / den
    return ar, ai, cr * br - ci * bi, cr * bi + ci * br


def _ssm_disc(lr, li, logdt, br, bi):
    C = br.shape[1]

    def fn(ins, ps):
        _, _, bbr, bbi = _ssm_disc_math(*ins)
        dt = jnp.exp(ins[2])
        return [ins[0] * dt, ins[1] * dt, bbr, bbi], []

    return _rowwise("ssm_disc", fn, [lr, li, logdt, br, bi], [], [(1, F32), (1, F32), (C, F32), (C, F32)], ts=512)


def _ssm_disc_bwd(lr, li, logdt, br, bi, dar, dai, dbbr, dbbi):
    C = br.shape[1]

    def fn(ins, ps):
        _, vjp = jax.vjp(_ssm_disc_math, *ins[:5])
        return list(vjp(tuple(ins[5:]))), []

    return _rowwise("ssm_disc_bwd", fn, [lr, li, logdt, br, bi, dar, dai, dbbr, dbbi], [],
                    [(1, F32), (1, F32), (1, F32), (C, F32), (C, F32)], ts=512)


def _block_diag(t):
    GB, g, a, b = t.shape
    eye = jnp.eye(g, dtype=t.dtype)
    return (t[:, :, :, None, :] * eye[None, :, None, :, None]).reshape(GB, g * a, g * b)


def _block_diag_take(t, g):
    GB, ga, gb_ = t.shape
    a, b = ga // g, gb_ // g
    eye = jnp.eye(g, dtype=t.dtype)
    return (t.reshape(GB, g, a, g, b) * eye[None, :, None, :, None]).sum(axis=3)


def _loss_head(h4, tgt, gf):
    D = h4.shape[1]

    def fn(ins, ps):
        x, t = ins
        xh, r = _xhat(x)
        err = xh * ps[0] - t
        dn = err * (1.0 / D)
        dxh = dn * ps[0]
        dx = r * (dxh - xh * jnp.mean(dxh * xh, axis=-1, keepdims=True))
        return [dx], [jnp.sum(err * err, axis=0, keepdims=True), jnp.sum(dn * xh, axis=0, keepdims=True)]

    return _rowwise("loss_head", fn, [h4, tgt], [gf], [(D, F32)], accs=[D, D])


def _gelu(x):
    return 0.5 * x * (1.0 + jnp.tanh(GELU_C * (x + GELU_K * x * x * x)))


def _gelu_grad(x):
    t = jnp.tanh(GELU_C * (x + GELU_K * x * x * x))
    return 0.5 * (1.0 + t) + 0.5 * x * (1.0 - t * t) * GELU_C * (1.0 + 3.0 * GELU_K * x * x)


def _mesh_pos():
    return lax.axis_index("x"), lax.axis_index("y"), lax.axis_index("c")


def _other_chips(x, y):
    return [(1 - x, y), (x, 1 - y), (1 - x, 1 - y)]


def _remote(src, dst, send, recv, dev):
    return pltpu.make_async_remote_copy(src_ref=src, dst_ref=dst, send_sem=send, recv_sem=recv,
                                        device_id=dev, device_id_type=MESH)


ANY = pl.BlockSpec(memory_space=pl.ANY)


COMM_BLOCK_BYTES = 3 << 19


def _place():
    x, y, c = _mesh_pos()
    return jnp.stack([c] + [2 * cx + cy for cx, cy in _other_chips(x, y)] + [2 * x + y]).astype(jnp.int32)


def _send_chips(name, srcs, specs, tr, nth, cw):
    hr = nth * tr
    n = len(srcs)

    def body(*refs):
        got_ref, send, recv = refs[1 + n:]
        t = pl.program_id(0)
        x, y, c = _mesh_pos()
        cps = []
        for j, chip in enumerate(_other_chips(x, y)):
            dst = got_ref.at[pl.ds(pl.multiple_of(j * hr + t * tr, 16), tr), :]
            cp = _remote(refs[1 + j % n], dst, send.at[j], recv.at[j], (*chip, c))
            cp.start()
            cps.append(cp)
        for cp in cps:
            cp.wait_send()

        @pl.when(t == nth - 1)
        def _():
            for j in range(3):
                r_ = got_ref.at[pl.ds(j * hr, hr), :]
                _remote(r_, r_, send.at[j], recv.at[j], (x, y, c)).wait_recv()

    return _pcall(
        body, name=name,
        grid_spec=pltpu.PrefetchScalarGridSpec(
            num_scalar_prefetch=1, grid=(nth,), in_specs=specs, out_specs=ANY,
            scratch_shapes=[pltpu.SemaphoreType.DMA((3,)), pltpu.SemaphoreType.DMA((3,))]),
        out_shape=jax.ShapeDtypeStruct((3 * hr, cw), srcs[0].dtype),
        compiler_params=_params("arbitrary"),
    )(_place(), *srcs)


def _ag_assemble(name, shard, stage, axis, tr, nth):
    R, cc = shard.shape
    hr = nth * tr
    full = (R, N_CHIPS * cc) if axis == 1 else (N_CHIPS * R, cc)

    def body(pl_ref, s0, s1, s2, h0, h1, out_ref, send, recv, lsem):
        t = pl.program_id(0)
        x, y, c = _mesh_pos()

        def region(s, half):
            if axis == 1:
                return out_ref.at[pl.ds(pl.multiple_of(half * hr + t * tr, 16), tr), pl.ds(pl.multiple_of(s * cc, LANES), cc)]
            return out_ref.at[pl.ds(pl.multiple_of(s * R + half * hr + t * tr, 16), tr), :]

        cps = []
        for j, src in enumerate((s0, s1, s2)):
            dst = region(pl_ref[1 + j], c)
            cps.append(_remote(src, dst, send.at[j], recv, (x, y, 1 - c)))
            cps.append(pltpu.make_async_copy(src, dst, lsem.at[j]))
        for half, src in enumerate((h0, h1)):
            cps.append(pltpu.make_async_copy(src, region(pl_ref[4], half), lsem.at[3 + half]))
        for cp in cps:
            cp.start()
        for k, cp in enumerate(cps):
            if k < 6 and k % 2 == 0:
                cp.wait_send()
            else:
                cp.wait()

        @pl.when(t == nth - 1)
        def _():
            r_ = out_ref.at[pl.ds(0, hr), pl.ds(0, 3 * cc)] if axis == 1 else out_ref.at[pl.ds(0, 3 * hr), :]
            _remote(r_, r_, send.at[0], recv, (x, y, c)).wait_recv()

    blk = lambda f: pl.BlockSpec((tr, cc), f)
    return _pcall(
        body, name=name,
        grid_spec=pltpu.PrefetchScalarGridSpec(
            num_scalar_prefetch=1, grid=(nth,),
            in_specs=[blk(lambda t, p, j=j: (j * nth + t, 0)) for j in range(3)]
            + [blk(lambda t, p, h=h: (h * nth + t, 0)) for h in range(2)],
            out_specs=ANY,
            scratch_shapes=[pltpu.SemaphoreType.DMA((3,)), pltpu.SemaphoreType.DMA, pltpu.SemaphoreType.DMA((5,))]),
        out_shape=jax.ShapeDtypeStruct(full, shard.dtype),
        compiler_params=_params("arbitrary"),
    )(_place(), stage, stage, stage, shard, shard)


def _comm_rows(hr, row_bytes):
    return _tile(hr, max(16, COMM_BLOCK_BYTES // row_bytes // 16 * 16), 16)


def _host_send(items):
    def copies(ins, outs, send, recv):
        x, y, c = _mesh_pos()
        cps = []
        for w, (_, kind, hr, cw) in enumerate(items):
            for j, (cx, cy) in enumerate(_other_chips(x, y)):
                s = 2 * cx + cy
                if kind == "half":
                    src = ins[w].at[pl.ds(pl.multiple_of(c * hr, 16), hr), :]
                elif kind == "cols":
                    src = ins[w].at[:, pl.ds(pl.multiple_of(s * cw, LANES), cw)]
                else:
                    src = ins[w].at[pl.ds(pl.multiple_of(s * hr, 16), hr), :]
                cps.append(_remote(src, outs[w].at[pl.ds(j * hr, hr), :], send.at[3 * w + j], recv.at[3 * w + j], (cx, cy, c)))
        return cps

    def start(ins, outs, send, recv):
        for cp in copies(ins, outs, send, recv):
            cp.start()

    def wait(ins, outs, send, recv):
        for cp in copies(ins, outs, send, recv):
            cp.wait()

    return _Host([a for a, _, _, _ in items], [jax.ShapeDtypeStruct((3 * hr, cw), a.dtype) for a, _, hr, cw in items],
                 3 * len(items), start, wait)


def _ag_send(name, sh):
    R, cc = sh.shape
    tr = _comm_rows(R // 2, cc * 2)
    nth = R // 2 // tr
    return _send_chips(name, [sh], [pl.BlockSpec((tr, cc), lambda t, p: (p[0] * nth + t, 0))], tr, nth, cc)


def _ag_finish(name, sh, stage, axis):
    R, cc = sh.shape
    tr = _comm_rows(R // 2, cc * 2)
    return _ag_assemble(name, sh, stage, axis, tr, R // 2 // tr)


def _all_gather_weights(shards, axes):
    return [_ag_finish("ag_asm%d" % w, sh, _ag_send("ag_send%d" % w, sh), ax) for w, (sh, ax) in enumerate(zip(shards, axes))]


def _push_pair(name, src, tr, nblk, src_block, out_rows, dst_block, local):
    cw = src.shape[1]
    c_arr = lax.axis_index("c").astype(jnp.int32).reshape(1)

    def body(c_ref, src_ref, out_ref, send, recv, lsem):
        i = pl.program_id(0)
        x, y, c = _mesh_pos()
        dst = out_ref.at[pl.ds(pl.multiple_of(dst_block(i, c_ref[0]) * tr, 16), tr), :]
        cp = _remote(src_ref, dst, send, recv, (x, y, 1 - c))
        cp.start()
        if local:
            lc = pltpu.make_async_copy(src_ref, dst, lsem)
            lc.start()
            lc.wait()
        cp.wait_send()

        @pl.when(i == nblk - 1)
        def _():
            got = out_ref.at[pl.ds(0, nblk * tr), :]
            _remote(got, got, send, recv, (x, y, c)).wait_recv()

    return _pcall(
        body, name=name,
        grid_spec=pltpu.PrefetchScalarGridSpec(
            num_scalar_prefetch=1, grid=(nblk,),
            in_specs=[pl.BlockSpec((tr, cw), lambda i, c_ref: (src_block(i, c_ref[0]), 0))],
            out_specs=ANY,
            scratch_shapes=[pltpu.SemaphoreType.DMA, pltpu.SemaphoreType.DMA, pltpu.SemaphoreType.DMA]),
        out_shape=jax.ShapeDtypeStruct((out_rows, cw), src.dtype),
        compiler_params=_params("arbitrary"),
    )(c_arr, src)


def _sum_half(name, g, theirs, tr, nblk, src_block):
    cw = g.shape[1]
    c_arr = lax.axis_index("c").astype(jnp.int32).reshape(1)

    def body(c_ref, g_ref, t_ref, o_ref):
        o_ref[...] = (g_ref[...].astype(F32) + t_ref[...].astype(F32)).astype(BF16)

    return _pcall(
        body, name=name,
        grid_spec=pltpu.PrefetchScalarGridSpec(
            num_scalar_prefetch=1, grid=(nblk,),
            in_specs=[pl.BlockSpec((tr, cw), lambda i, c_ref: (src_block(i, c_ref[0]), 0)),
                      pl.BlockSpec((tr, cw), lambda i, c_ref: (i, 0))],
            out_specs=pl.BlockSpec((tr, cw), lambda i, c_ref: (i, 0))),
        out_shape=jax.ShapeDtypeStruct((nblk * tr, cw), BF16),
        compiler_params=_params("arbitrary"),
    )(c_arr, g, theirs)


def _sum_chips(name, q, got, qspec, tr, nth, cw):
    def body(p_ref, q_ref, g0, g1, g2, o_ref):
        o_ref[...] = q_ref[...].astype(F32) + g0[...].astype(F32) + g1[...].astype(F32) + g2[...].astype(F32)

    return _pcall(
        body, name=name,
        grid_spec=pltpu.PrefetchScalarGridSpec(
            num_scalar_prefetch=1, grid=(nth,),
            in_specs=[qspec] + [pl.BlockSpec((tr, cw), lambda t, p, j=j: (j * nth + t, 0)) for j in range(3)],
            out_specs=pl.BlockSpec((tr, cw), lambda t, p: (t, 0))),
        out_shape=jax.ShapeDtypeStruct((nth * tr, cw), F32),
        compiler_params=_params("arbitrary"),
    )(_place(), q, got, got, got)


def _rs_geom(g, axis):
    rows, gw = g.shape
    return (rows // 2, gw // N_CHIPS) if axis == 1 else (rows // N_CHIPS // 2, gw)


def _rs_pair_sum(tag, g, axis):
    hr, _ = _rs_geom(g, axis)
    tr = _comm_rows(hr, g.shape[1] * 2)
    nth = hr // tr
    if axis == 1:
        nblk, blk = nth, (lambda i, half: half * nth + i)
    else:
        nblk, blk = N_CHIPS * nth, (lambda i, half: (i // nth) * (2 * nth) + half * nth + i % nth)
    theirs = _push_pair("rs_pair_" + tag, g, tr, nblk, lambda i, c: blk(i, 1 - c), nblk * tr, lambda i, c: i, local=False)
    return _sum_half("rs_sum_pair_" + tag, g, theirs, tr, nblk, blk)


def _rs_part(q, axis, hr, cw, tr):
    nth = hr // tr
    if axis == 1:
        return lambda k: pl.BlockSpec((tr, cw), lambda t, p: (t, p[k]))
    return lambda k: pl.BlockSpec((tr, cw), lambda t, p: (p[k] * nth + t, 0))


def _rs_send(tag, q, axis, hr, cw):
    tr = _comm_rows(hr, cw * 2)
    part = _rs_part(q, axis, hr, cw, tr)
    return _send_chips("rs_send_" + tag, [q, q, q], [part(1), part(2), part(3)], tr, hr // tr, cw)


def _rs_finish(tag, q, got, axis, hr, cw):
    tr = _comm_rows(hr, cw * 2)
    half = _sum_chips("rs_sum_chips_" + tag, q, got, _rs_part(q, axis, hr, cw, tr)(4), tr, hr // tr, cw)
    tr = _comm_rows(hr, cw * 4)
    nth = hr // tr
    return _push_pair("rs_swap_" + tag, half, tr, nth, lambda i, c: i, 2 * hr, lambda i, c: c * nth + i, local=True)


def _reduce_scatter(grads, axes):
    outs = []
    for w, (g, ax) in enumerate(zip(grads, axes)):
        hr, cw = _rs_geom(g, ax)
        q = _rs_pair_sum(str(w), g, ax)
        outs.append(_rs_finish(str(w), q, _rs_send(str(w), q, ax, hr, cw), ax, hr, cw))
    return outs


class _Exchange:
    FIRST = ("ffn1_w_gate", "ffn1_w_up", "ffn1_w_down")
    BESIDE = (("ffn2_w_gate", "ffn2_w_up"), ("ffn2_w_down", "w_in", "ssm_w_glu", "w_out", "ple_w_gate", "ple_w_proj"))

    def __init__(self, shards, axes):
        self.shards, self.axes = shards, axes

    def first(self):
        return {k: _ag_finish("ag_asm_" + k, self.shards[k], _ag_send("ag_send_" + k, self.shards[k]), self.axes[k])
                for k in self.FIRST}

    def fwd_hosts(self):
        item = lambda k: (self.shards[k], "half", self.shards[k].shape[0] // 2, self.shards[k].shape[1])
        return tuple(_host_send([item(k) for k in names]) for names in self.BESIDE)

    def rest(self, stages):
        return {k: _ag_finish("ag_asm_" + k, self.shards[k], st, self.axes[k])
                for names, sts in zip(self.BESIDE, stages) for k, st in zip(names, sts)}

    def bwd_hosts(self, grads):
        self.q = {k: _rs_pair_sum(k, g, self.axes[k]) for k, g in grads.items()}
        self.geom = {k: _rs_geom(g, self.axes[k]) for k, g in grads.items()}
        item = lambda k: (self.q[k], "cols" if self.axes[k] == 1 else "rows") + self.geom[k]
        return tuple(_host_send([item(k) for k in names]) for names in self.BESIDE)

    def finish(self, gots, grads):
        out = {k: _rs_finish(k, self.q[k], got, self.axes[k], *self.geom[k])
               for names, gs in zip(self.BESIDE, gots) for k, got in zip(names, gs)}
        for k, g in grads.items():
            hr, cw = _rs_geom(g, self.axes[k])
            q = _rs_pair_sum(k, g, self.axes[k])
            out[k] = _rs_finish(k, q, _rs_send(k, q, self.axes[k], hr, cw), self.axes[k], hr, cw)
        return out


def _all_reduce_small(v):
    n = v.shape[0]

    def body(v_ref, out_ref, buf, send, recv):
        x, y, c = _mesh_pos()
        my = 4 * x + 2 * y + c
        buf[my] = v_ref[...]
        cps = []
        for k in range(1, N_DEV):
            fx, fy, fc = (k >> 2) & 1, (k >> 1) & 1, k & 1
            peer = (1 - x if fx else x, 1 - y if fy else y, 1 - c if fc else c)
            cp = _remote(v_ref, buf.at[my], send.at[k - 1], recv.at[k - 1], peer)
            cp.start()
            cps.append((cp, 4 * peer[0] + 2 * peer[1] + peer[2]))
        for k, (cp, pid) in enumerate(cps):
            _remote(v_ref, buf.at[pid], send.at[k], recv.at[k], (x, y, c)).wait_recv()
        acc = buf[0]
        for i in range(1, N_DEV):
            acc = acc + buf[i]
        out_ref[...] = acc
        for cp, _ in cps:
            cp.wait_send()

    return _pcall(
        body, name="ar_small",
        in_specs=[pl.BlockSpec(memory_space=pltpu.VMEM)], out_specs=pl.BlockSpec(memory_space=pltpu.VMEM),
        out_shape=jax.ShapeDtypeStruct((n, LANES), F32),
        scratch_shapes=[pltpu.VMEM((N_DEV, n, LANES), F32), pltpu.SemaphoreType.DMA((N_DEV - 1,)),
                        pltpu.SemaphoreType.DMA((N_DEV - 1,))],
        compiler_params=pltpu.CompilerParams(vmem_limit_bytes=V7X_VMEM_LIMIT_BYTES),
    )(v)


def _adamw(name, w, g, m, v):
    R, Cc = w.shape
    tr = _tile(R, max(8, (1 << 19) // Cc // 8 * 8), 8)
    c1 = 1.0 - ADAM_B1 ** ADAM_STEP
    c2 = 1.0 - ADAM_B2 ** ADAM_STEP

    def body(w_ref, g_ref, m_ref, v_ref, d_ref, nm_ref, nv_ref):
        g_ = g_ref[...]
        nm = ADAM_B1 * m_ref[...] + (1.0 - ADAM_B1) * g_
        nv = ADAM_B2 * v_ref[...] + (1.0 - ADAM_B2) * (g_ * g_)
        d_ref[...] = -ADAM_LR * ((nm / c1) / (jnp.sqrt(nv / c2) + ADAM_EPS) + ADAM_WD * w_ref[...])
        nm_ref[...] = nm
        nv_ref[...] = nv

    spec = pl.BlockSpec((tr, Cc), lambda i: (i, 0))
    return _pcall(
        body, name=name, grid=(R // tr,),
        in_specs=[spec] * 4, out_specs=[spec] * 3,
        out_shape=[jax.ShapeDtypeStruct((R, Cc), F32)] * 3,
        compiler_params=_params("parallel"),
    )(w, g, m, v)


def _pack(arrs, rows):
    flat = jnp.concatenate([a.reshape(-1) for a in arrs])
    return jnp.pad(flat, (0, rows * LANES - flat.shape[0])).reshape(rows, LANES)


def _unpack(packed, like):
    flat, out, o = packed.reshape(-1), [], 0
    for a in like:
        out.append(flat[o:o + a.size].reshape(a.shape))
        o += a.size
    return out


BIG = (
    ("ffn1_w_gate", 1), ("ffn1_w_up", 1), ("ffn1_w_down", 0), ("w_in", 1), ("ssm_w_glu", 0), ("w_out", 0),
    ("ffn2_w_gate", 1), ("ffn2_w_up", 1), ("ffn2_w_down", 0), ("ple_w_gate", 0), ("ple_w_proj", 1),
)
SMALL = ("ffn1_norm", "mix_norm", "attn_out_norm", "ssm_lambda_re", "ssm_lambda_im", "ssm_log_dt", "ssm_b_re", "ssm_b_im",
         "ssm_c_re", "ssm_c_im", "ssm_d", "ssm_b_glu", "ssm_out_norm", "ffn2_norm", "ple_norm", "final_norm")
WEIGHTS = ("ffn1_norm", "ffn1_w_gate", "ffn1_w_up", "ffn1_w_down", "mix_norm", "w_in", "attn_out_norm", "ssm_lambda_re",
           "ssm_lambda_im", "ssm_log_dt", "ssm_b_re", "ssm_b_im", "ssm_c_re", "ssm_c_im", "ssm_d", "ssm_w_glu", "ssm_b_glu",
           "ssm_out_norm", "w_out", "ffn2_norm", "ffn2_w_gate", "ffn2_w_up", "ffn2_w_down", "ple_norm", "ple_w_gate",
           "ple_w_proj", "final_norm")


def _pad_to(a, axis, n):
    pad = [(0, 0), (0, 0)]
    pad[axis] = (0, n - a.shape[axis])
    return jnp.pad(a, pad)


def _local_step(x, p, tgt, w, ex):
    S, D = x.shape
    A = w["attn_out_norm"].shape[-1]
    W = w["ssm_d"].shape[-1]
    G, P = w["ssm_lambda_re"].shape[-2:]
    C = w["ssm_b_re"].shape[-1]
    GB = G // SSM_BLOCK_GROUPS
    T = min(1024, S)
    row = lambda name: w[name].reshape(1, -1)
    gs = {}

    full = ex.first()
    h1, ffn1_saved, stages = _ffn_fwd("ffn1", x, row("ffn1_norm"), full["ffn1_w_gate"], full["ffn1_w_up"],
                                      full["ffn1_w_down"], hosts=ex.fwd_hosts())
    full.update(ex.rest(stages))
    n2 = _rms_fwd("mix_norm", h1, row("mix_norm"))
    w_in = full["w_in"]
    n2p = _to_attn_order(n2)
    (qkv,) = _mm("w_in_qkv", [n2p], [w_in[:, :3 * A]], [F32], tm=1024, tn=1024)
    (s_in,) = _mm("w_in_ssm", [n2], [w_in[:, 3 * A:]], [F32], tm=1024, tn=1024)
    ya, lse = _attn_fwd(qkv)

    col = lambda name: w[name].reshape(G * P, 1)
    logdt_x = jnp.repeat(w["ssm_log_dt"].reshape(G), P).reshape(G * P, 1)
    b_re, b_im = w["ssm_b_re"].reshape(G * P, C), w["ssm_b_im"].reshape(G * P, C)
    lrdt, lidt, bbr, bbi = _ssm_disc(col("ssm_lambda_re"), col("ssm_lambda_im"), logdt_x, b_re, b_im)
    gsz = SSM_BLOCK_GROUPS
    to_bb = lambda t: _block_diag(t.reshape(GB, gsz, P, C).transpose(0, 1, 3, 2))
    bb = jnp.concatenate([to_bb(bbr), to_bb(bbi)], axis=2).astype(BF16)
    to_cc = lambda t: _block_diag(t.reshape(GB, gsz, C, P).transpose(0, 1, 3, 2))
    cc = jnp.concatenate([to_cc(w["ssm_c_re"]), -to_cc(w["ssm_c_im"])], axis=1).astype(BF16)
    lam_dt = jnp.stack([lrdt.reshape(GB, gsz * P), lidt.reshape(GB, gsz * P)], axis=1)
    ufp = _ssm_perm(s_in, T)
    ypre, hstart = _ssm_fwd(ufp, bb, cc, lam_dt, row("ssm_d"), T)

    def glu_in(ins, ps):
        yg = _gelu(ins[0])
        return [yg, yg], []

    yg, ygb = _rowwise("ssm_gelu", glu_in, [ypre], [], [(W, F32), (W, BF16)])
    w_glu = full["ssm_w_glu"]

    def glu_out(accs, ex):
        gl = accs[0] + ex[1]
        return [ex[0] * _sigmoid(gl), gl]

    ybp, gl = _mm("ssm_glu", [ygb], [w_glu], [F32, F32], extras=[(yg, "mn"), (row("ssm_b_glu"), "n")],
                  epilogue=glu_out, tm=1024, tn=1024)
    yb = _ssm_unperm(ybp, T)
    na = _from_attn_order(_rms_fwd("attn_out_norm", ya, row("attn_out_norm")))
    nb = _rms_fwd("ssm_out_norm", yb, row("ssm_out_norm"))
    w_out = full["w_out"]
    (h2,) = _mm("w_out", [na, nb], [w_out[:A], w_out[A:]], [F32], pairs=((0, 0, 0), (1, 1, 0)), extras=[(h1, "mn")],
                epilogue=lambda accs, ex: [ex[0] + accs[0]], tm=1024, tn=1024)
    h3, ffn2_saved, _ = _ffn_fwd("ffn2", h2, row("ffn2_norm"), full["ffn2_w_gate"], full["ffn2_w_up"], full["ffn2_w_down"])
    n4 = _rms_fwd("ple_norm", h3, row("ple_norm"))
    (pe,) = _mm("ple_proj", [p], [full["ple_w_proj"]], [F32], tm=1024, tn=1024)

    def ple_out(accs, ex):
        gate = _sigmoid(accs[0])
        return [ex[1] + gate * ex[0], gate]

    h4, gate = _mm("ple_gate", [n4], [full["ple_w_gate"]], [F32, F32], extras=[(pe, "mn"), (h3, "mn")],
                   epilogue=ple_out, tm=1024, tn=1024)

    dh4, err2, gs["final_norm"] = _loss_head(h4, tgt, row("final_norm"))
    loss = (0.5 / D) * jnp.sum(err2)

    def ple_bwd(ins, ps):
        dh, gt, pe_ = ins
        return [dh * gt, dh * pe_ * gt * (1.0 - gt)], []

    dpe, dpg = _rowwise("ple_bwd", ple_bwd, [dh4, gate, pe], [], [(D, BF16), (D, BF16)])
    (d_ple_proj,) = _mm("ple_dproj", [p], [dpe], [BF16], ta=True, tm=256, tn=2048, tk=1024)
    (d_ple_gate,) = _mm("ple_dgate", [n4], [dpg], [BF16], ta=True, tm=1024, tn=1024, tk=2048)
    (dn4,) = _mm("ple_dn", [dpg], [full["ple_w_gate"]], [F32], tb=True, tm=1024, tn=1024)
    (dh3, dh3b), gs["ple_norm"] = _rms_bwd("ple_dnorm", dn4, h3, row("ple_norm"), dres=dh4, copy_scale=0.5)
    (dh2, dh2b), gs["ffn2_norm"], d_ffn2_g, d_ffn2_u, d_ffn2_d, _ = _ffn_bwd(
        "ffn2", dh3, dh3b, h2, row("ffn2_norm"), full["ffn2_w_gate"], full["ffn2_w_up"], full["ffn2_w_down"],
        ffn2_saved, copy_scale=1.0)
    (dna,) = _mm("w_out_dna", [_to_attn_order(dh2b)], [w_out[:A]], [F32], tb=True, tm=1024, tn=1024)
    (dnb,) = _mm("w_out_dnb", [dh2b], [w_out[A:]], [F32], tb=True, tm=1024, tn=1024)
    (d_wout_a,) = _mm("w_out_dwa", [na], [dh2b], [BF16], ta=True, tm=1024, tn=1024, tk=2048)
    (d_wout_b,) = _mm("w_out_dwb", [nb], [dh2b], [BF16], ta=True, tm=1024, tn=1024, tk=2048)
    d_w_out = jnp.concatenate([d_wout_a, d_wout_b], axis=0)
    (dya,), gs["attn_out_norm"] = _rms_bwd("attn_out_dnorm", dna, ya, row("attn_out_norm"))
    (dyb,), gs["ssm_out_norm"] = _rms_bwd("ssm_out_dnorm", dnb, yb, row("ssm_out_norm"))

    dybp = _ssm_perm(dyb, T)

    def glu_bwd(ins, ps):
        dy, yg_, gl_ = ins
        sg = _sigmoid(gl_)
        dgl = dy * yg_ * sg * (1.0 - sg)
        return [dgl, dy * sg], [jnp.sum(dgl, axis=0, keepdims=True)]

    dgl, dyg_direct, gs["ssm_b_glu"] = _rowwise("ssm_glu_bwd", glu_bwd, [dybp, yg, gl], [], [(W, BF16), (W, F32)], accs=[W])
    (d_w_glu,) = _mm("ssm_dwglu", [ygb], [dgl], [BF16], ta=True, tm=1024, tn=1024, tk=2048)
    (dypre,) = _mm("ssm_dyg", [dgl], [w_glu], [F32], tb=True, extras=[(dyg_direct, "mn"), (ypre, "mn")],
                   epilogue=lambda accs, ex: [(accs[0] + ex[0]) * _gelu_grad(ex[1])], tm=1024, tn=1024)
    dufp, dbb, dcc, da, gs["ssm_d"] = _ssm_bwd(ufp, dypre, bb, bb.transpose(0, 2, 1), cc, cc.transpose(0, 2, 1),
                                               lam_dt, row("ssm_d"), hstart, T)
    ns = gsz * P
    from_bb = lambda t: _block_diag_take(t, gsz).transpose(0, 1, 3, 2).reshape(G * P, C)
    from_cc = lambda t: _block_diag_take(t, gsz).transpose(0, 1, 3, 2).reshape(w["ssm_c_re"].shape)
    gs["ssm_c_re"], gs["ssm_c_im"] = from_cc(dcc[:, :ns]), -from_cc(dcc[:, ns:])
    da = da.sum(axis=1)
    dar, dai = da[:, :ns].reshape(G * P, 1), da[:, ns:].reshape(G * P, 1)
    dlr, dli, dlogdt, dbr, dbi = _ssm_disc_bwd(col("ssm_lambda_re"), col("ssm_lambda_im"), logdt_x, b_re, b_im,
                                               dar, dai, from_bb(dbb[:, :, :ns]), from_bb(dbb[:, :, ns:]))
    gs["ssm_lambda_re"], gs["ssm_lambda_im"] = dlr.reshape(w["ssm_lambda_re"].shape), dli.reshape(w["ssm_lambda_im"].shape)
    gs["ssm_log_dt"] = dlogdt.reshape(G, P).sum(axis=1).reshape(w["ssm_log_dt"].shape)
    gs["ssm_b_re"], gs["ssm_b_im"] = dbr.reshape(w["ssm_b_re"].shape), dbi.reshape(w["ssm_b_im"].shape)
    ds_in = _ssm_unperm(dufp, T)

    dq, dk, dv = _attn_bwd(qkv, dya, ya, lse)
    dqkv = jnp.concatenate([dq, dk, dv], axis=1).astype(BF16)
    (d_w_qkv,) = _mm("w_in_dw_qkv", [n2p], [dqkv], [BF16], ta=True, tm=1024, tn=1024, tk=2048)
    (d_w_s,) = _mm("w_in_dw_ssm", [n2], [ds_in], [BF16], ta=True, tm=1024, tn=1024, tk=2048)
    d_w_in = jnp.concatenate([d_w_qkv, d_w_s], axis=1)
    dz = jnp.concatenate([_from_attn_order(dqkv), ds_in.astype(BF16)], axis=1)
    (dn2,) = _mm("w_in_dn", [dz], [w_in], [F32], tb=True, tm=1024, tn=1024)
    (dh1, dh1b), gs["mix_norm"] = _rms_bwd("mix_dnorm", dn2, h1, row("mix_norm"), dres=dh2, copy_scale=0.5)
    gots_hosts = ex.bwd_hosts({"w_in": d_w_in, "ssm_w_glu": d_w_glu, "w_out": d_w_out, "ffn2_w_gate": d_ffn2_g,
                               "ffn2_w_up": d_ffn2_u, "ffn2_w_down": d_ffn2_d, "ple_w_gate": d_ple_gate,
                               "ple_w_proj": d_ple_proj})
    (dx,), gs["ffn1_norm"], d_ffn1_g, d_ffn1_u, d_ffn1_d, gots = _ffn_bwd(
        "ffn1", dh1, dh1b, x, row("ffn1_norm"), full["ffn1_w_gate"], full["ffn1_w_up"], full["ffn1_w_down"],
        ffn1_saved, copy_scale=None, hosts=gots_hosts)
    big = ex.finish(gots, {"ffn1_w_gate": d_ffn1_g, "ffn1_w_up": d_ffn1_u, "ffn1_w_down": d_ffn1_d})
    small = {k: gs[k].reshape(w[k].shape) for k in SMALL}
    return loss, dx, big, small


def kernel(x, p, ffn1_norm, ffn1_w_gate, ffn1_w_up, ffn1_w_down, mix_norm, w_in, attn_out_norm, ssm_lambda_re, ssm_lambda_im, ssm_log_dt, ssm_b_re, ssm_b_im, ssm_c_re, ssm_c_im, ssm_d, ssm_w_glu, ssm_b_glu, ssm_out_norm, w_out, ffn2_norm, ffn2_w_gate, ffn2_w_up, ffn2_w_down, ple_norm, ple_w_gate, ple_w_proj, final_norm, loss_target, m_ffn1_norm, m_ffn1_w_gate, m_ffn1_w_up, m_ffn1_w_down, m_mix_norm, m_w_in, m_attn_out_norm, m_ssm_lambda_re, m_ssm_lambda_im, m_ssm_log_dt, m_ssm_b_re, m_ssm_b_im, m_ssm_c_re, m_ssm_c_im, m_ssm_d, m_ssm_w_glu, m_ssm_b_glu, m_ssm_out_norm, m_w_out, m_ffn2_norm, m_ffn2_w_gate, m_ffn2_w_up, m_ffn2_w_down, m_ple_norm, m_ple_w_gate, m_ple_w_proj, m_final_norm, v_ffn1_norm, v_ffn1_w_gate, v_ffn1_w_up, v_ffn1_w_down, v_mix_norm, v_w_in, v_attn_out_norm, v_ssm_lambda_re, v_ssm_lambda_im, v_ssm_log_dt, v_ssm_b_re, v_ssm_b_im, v_ssm_c_re, v_ssm_c_im, v_ssm_d, v_ssm_w_glu, v_ssm_b_glu, v_ssm_out_norm, v_w_out, v_ffn2_norm, v_ffn2_w_gate, v_ffn2_w_up, v_ffn2_w_down, v_ple_norm, v_ple_w_gate, v_ple_w_proj, v_final_norm):
    args = locals()
    w = {k: args[k] for k in WEIGHTS}
    m = {k: args["m_" + k] for k in WEIGHTS}
    v = {k: args["v_" + k] for k in WEIGHTS}
    w2 = {k: w[k].reshape(w[k].shape[-2:]) for k, _ in BIG}

    axes = [ax for _, ax in BIG]
    padded = {k: -(-w2[k].shape[ax] // LANES) * LANES for k, ax in BIG}
    shards = [_pad_to(w2[k].astype(BF16), ax, padded[k]) for k, ax in BIG]
    ex = _Exchange(dict(zip([k for k, _ in BIG], shards)), dict(BIG))
    loss_local, dx, summed, gsmall = _local_step(x[0], p[0, 0], loss_target[0], w, ex)
    loss = lax.psum(loss_local, MESH_AXES)
    n_small = sum(w[k].size for k in SMALL)
    rows = -(-n_small // (SUBLANES * LANES)) * SUBLANES
    gs_sum = _all_reduce_small(_pack([gsmall[k] for k in SMALL], rows))

    grads, delta, new_m, new_v = {}, {}, {}, {}
    for k, ax in BIG:
        gfull = summed[k]
        g2 = lax.slice_in_dim(gfull, 0, w2[k].shape[ax], axis=ax)
        d2, nm2, nv2 = _adamw("adamw_" + k, w2[k], g2, m[k].reshape(w2[k].shape), v[k].reshape(w2[k].shape))
        grads[k], delta[k], new_m[k], new_v[k] = (t.reshape(w[k].shape) for t in (g2, d2, nm2, nv2))
    small_like = [w[k] for k in SMALL]
    ds, nms, nvs = _adamw("adamw_small", _pack(small_like, rows), gs_sum, _pack([m[k] for k in SMALL], rows),
                          _pack([v[k] for k in SMALL], rows))
    for k, g_, d_, nm_, nv_ in zip(SMALL, _unpack(gs_sum, small_like), _unpack(ds, small_like),
                                   _unpack(nms, small_like), _unpack(nvs, small_like)):
        grads[k], delta[k], new_m[k], new_v[k] = g_, d_, nm_, nv_

    return (loss, dx[None], *[grads[k] for k in WEIGHTS], *[delta[k] for k in WEIGHTS],
            *[new_m[k] for k in WEIGHTS], *[new_v[k] for k in WEIGHTS])
```

```python
import functools
import math

import jax
import jax.numpy as jnp
from jax import lax
from jax.experimental import pallas as pl
from jax.experimental.pallas import tpu as pltpu

F32 = jnp.float32
BF16 = jnp.bfloat16
MESH = pl.DeviceIdType.MESH
MESH_AXES = ("x", "y", "c")
N_CHIPS = 4
N_DEV = 8

V7X_VMEM_LIMIT_BYTES = 56 << 20
LANES = 128
SUBLANES = 8

HEAD_DIM = 64
SWA_BLOCK = 128
DILATIONS = (1, 4, 16)
SSM_BLOCK_GROUPS = 8
NORM_EPS = 1e-6
MASK_VALUE = -1e30

ADAM_LR = 0.001
ADAM_B1 = 0.9
ADAM_B2 = 0.999
ADAM_EPS = 1e-08
ADAM_WD = 0.01
ADAM_STEP = 10

GELU_C = math.sqrt(2.0 / math.pi)
GELU_K = 0.044715


def _pcall(body, **kw):
    return pl.pallas_call(body, **kw)


def _params(*sem):
    return pltpu.CompilerParams(dimension_semantics=sem, vmem_limit_bytes=V7X_VMEM_LIMIT_BYTES)


def _tile(n, target, align):
    best = None
    for t in range(align, min(n, target) + 1, align):
        if n % t == 0:
            best = t
    return n if best is None else best


def _sigmoid(x):
    return 1.0 / (1.0 + jnp.exp(-x))


class _Host:
    def __init__(self, ins, out_shapes, n_sem, start, wait):
        self.ins, self.out_shapes, self.n_sem, self.start, self.wait = ins, out_shapes, n_sem, start, wait


def _mm(name, lhs, rhs, outs, pairs=((0, 0, 0),), epilogue=None, extras=(), ta=False, tb=False,
        tm=1024, tn=512, tk=2048, host=None):
    nl, nr, ne, no = len(lhs), len(rhs), len(extras), len(outs)
    nhi, nho = (len(host.ins), len(host.out_shapes)) if host else (0, 0)
    n_acc = 1 + max(p[2] for p in pairs)
    (K, M) = lhs[0].shape if ta else lhs[0].shape[::-1]
    (N, K2) = rhs[0].shape if tb else rhs[0].shape[::-1]
    assert K == K2, (name, lhs[0].shape, rhs[0].shape)
    tm, tn, tk = _tile(M, tm, LANES), _tile(N, tn, LANES), _tile(K, tk, LANES)
    ni, nj, nk = M // tm, N // tn, K // tk
    n_scr = n_acc if nk > 1 else 0
    if epilogue is None:
        epilogue = lambda accs, ex: accs
    dn = (((0 if ta else 1,), (1 if tb else 0,)), ((), ()))

    def body(*refs):
        refs = list(refs)
        take = lambda n: [refs.pop(0) for _ in range(n)]
        l, r, e, hin, o, hout, acc = take(nl), take(nr), take(ne), take(nhi), take(no), take(nho), take(n_scr)
        i, j, k = pl.program_id(0), pl.program_id(1), pl.program_id(2)
        if host:
            @pl.when((i == 0) & (j == 0) & (k == 0))
            def _():
                host.start(hin, hout, *refs)

        parts = [None] * n_acc
        for li, ri, ai in pairs:
            d = lax.dot_general(l[li][...].astype(BF16), r[ri][...].astype(BF16), dn,
                                preferred_element_type=F32)
            parts[ai] = d if parts[ai] is None else parts[ai] + d

        def finish(accs):
            res = epilogue(accs, [x[...] for x in e])
            for ref, val in zip(o, res):
                ref[...] = val.astype(ref.dtype)

        if nk == 1:
            finish(parts)
        else:
            @pl.when(k == 0)
            def _():
                for ai in range(n_acc):
                    acc[ai][...] = parts[ai]

            @pl.when(k > 0)
            def _():
                for ai in range(n_acc):
                    acc[ai][...] += parts[ai]

            @pl.when(k == nk - 1)
            def _():
                finish([a[...] for a in acc])

        if host:
            @pl.when((i == ni - 1) & (j == nj - 1) & (k == nk - 1))
            def _():
                host.wait(hin, hout, *refs)

    lspec = pl.BlockSpec((tk, tm), lambda i, j, k: (k, i)) if ta else pl.BlockSpec((tm, tk), lambda i, j, k: (i, k))
    rspec = pl.BlockSpec((tn, tk), lambda i, j, k: (j, k)) if tb else pl.BlockSpec((tk, tn), lambda i, j, k: (k, j))
    especs = []
    for arr, kind in extras:
        if kind == "mn":
            especs.append(pl.BlockSpec((tm, tn), lambda i, j, k: (i, j)))
        elif kind == "n":
            especs.append(pl.BlockSpec((1, tn), lambda i, j, k: (0, j)))
        else:
            especs.append(pl.BlockSpec((tm, 1), lambda i, j, k: (i, 0)))
    any_spec = pl.BlockSpec(memory_space=pl.ANY)
    sems = [pltpu.SemaphoreType.DMA((host.n_sem,)), pltpu.SemaphoreType.DMA((host.n_sem,))] if host else []
    res = _pcall(
        body, name=name,
        grid=(ni, nj, nk),
        in_specs=[lspec] * nl + [rspec] * nr + especs + [any_spec] * nhi,
        out_specs=[pl.BlockSpec((tm, tn), lambda i, j, k: (i, j))] * no + [any_spec] * nho,
        out_shape=[jax.ShapeDtypeStruct((M, N), dt) for dt in outs] + (list(host.out_shapes) if host else []),
        scratch_shapes=[pltpu.VMEM((tm, tn), F32)] * n_scr + sems,
        compiler_params=_params(*(("arbitrary",) * 3 if host else ("parallel", "parallel", "arbitrary"))),
    )(*lhs, *rhs, *[a for a, _ in extras], *(host.ins if host else []))
    return res


def _rowwise(name, fn, ins, params, outs, accs=(), ts=256):
    S = ins[0].shape[0]
    ts = _tile(S, ts, 16)
    ni, npar, no, na = len(ins), len(params), len(outs), len(accs)

    def body(*refs):
        i_refs, p_refs = refs[:ni], refs[ni:ni + npar]
        o_refs = refs[ni + npar:ni + npar + no]
        a_refs = refs[ni + npar + no:]
        res_o, res_a = fn([r[...] for r in i_refs], [r[...] for r in p_refs])
        for ref, val in zip(o_refs, res_o):
            ref[...] = val.astype(ref.dtype)
        if na:
            @pl.when(pl.program_id(0) == 0)
            def _():
                for ref in a_refs:
                    ref[...] = jnp.zeros(ref.shape, F32)

            for ref, val in zip(a_refs, res_a):
                ref[...] += val

    res = _pcall(
        body, name=name,
        grid=(S // ts,),
        in_specs=[pl.BlockSpec((ts, a.shape[1]), lambda i: (i, 0)) for a in ins]
        + [pl.BlockSpec(p.shape, lambda i: (0, 0)) for p in params],
        out_specs=[pl.BlockSpec((ts, w), lambda i: (i, 0)) for w, _ in outs]
        + [pl.BlockSpec((1, w), lambda i: (0, 0)) for w in accs],
        out_shape=[jax.ShapeDtypeStruct((S, w), dt) for w, dt in outs]
        + [jax.ShapeDtypeStruct((1, w), F32) for w in accs],
        compiler_params=_params("arbitrary"),
    )(*ins, *params)
    return res


def _xhat(x):
    r = lax.rsqrt(jnp.mean(x * x, axis=-1, keepdims=True) + NORM_EPS)
    return x * r, r


def _rms_fwd(name, x, g):
    def fn(ins, ps):
        xh, _ = _xhat(ins[0])
        return [xh * ps[0]], []

    return _rowwise(name, fn, [x], [g], [(x.shape[1], BF16)])[0]


def _rms_bwd(name, dn, x, g, dres=None, copy_scale=None):
    w = x.shape[1]

    def fn(ins, ps):
        dn_, x_ = ins[0], ins[1]
        xh, r = _xhat(x_)
        dxh = dn_ * ps[0]
        dx = r * (dxh - xh * jnp.mean(dxh * xh, axis=-1, keepdims=True))
        if dres is not None:
            dx = dx + ins[2]
        o = [dx] + ([dx * copy_scale] if copy_scale is not None else [])
        return o, [jnp.sum(dn_ * xh, axis=0, keepdims=True)]

    ins = [dn, x] + ([dres] if dres is not None else [])
    outs = [(w, F32)] + ([(w, BF16)] if copy_scale is not None else [])
    res = _rowwise(name, fn, ins, [g], outs, accs=[w])
    return res[:-1], res[-1]


def _swiglu_epilogue(accs, ex):
    g, u = accs
    return [g, u, g * _sigmoid(g) * u]


def _dswiglu_epilogue(accs, ex):
    da = accs[0]
    g, u = ex[0].astype(F32), ex[1].astype(F32)
    sg = _sigmoid(g)
    return [da * u * (sg * (1.0 + g * (1.0 - sg))), da * (g * sg)]


def _carried(ex, kind, kernel):
    if kernel not in (ex.AG_PLAN if kind == "ag" else ex.RS_PLAN):
        return None, lambda outs: None
    if kind == "ag":
        return ex.ag_host(kernel), lambda outs: ex.ag_done(kernel, outs)
    return ex.rs_host(kernel), lambda outs: ex.rs_done(kernel, outs)


def _ffn_fwd(tag, h, gnorm, ex):
    n = _rms_fwd(tag + "_norm", h, gnorm)
    host, done = _carried(ex, "ag", tag + "_up")
    g, u, a, *outs = _mm(tag + "_up", [n], [ex.weight(tag + "_w_gate"), ex.weight(tag + "_w_up")], [BF16, BF16, BF16],
                         pairs=((0, 0, 0), (0, 1, 1)), epilogue=_swiglu_epilogue, tm=1024, tn=512, host=host)
    done(outs)
    host, done = _carried(ex, "ag", tag + "_down")
    hout, *outs = _mm(tag + "_down", [a], [ex.weight(tag + "_w_down")], [F32], extras=[(h, "mn")],
                      epilogue=lambda accs, ex_: [ex_[0] + 0.5 * accs[0]], tm=512, tn=1024, tk=8192, host=host)
    done(outs)
    return hout, (n, g, u, a)


def _ffn_bwd(tag, dh, dhb_half, h, gnorm, ex, saved, copy_scale):
    n, g, u, a = saved
    wg, wu, wd = (ex.weight(tag + k) for k in ("_w_gate", "_w_up", "_w_down"))
    host, done = _carried(ex, "rs", tag + "_dact")
    dg, du, *outs = _mm(tag + "_dact", [dhb_half], [wd], [BF16, BF16], tb=True, extras=[(g, "mn"), (u, "mn")],
                        epilogue=_dswiglu_epilogue, tm=1024, tn=512, host=host)
    done(outs)
    host, done = _carried(ex, "rs", tag + "_dwd")
    dwd, *outs = _mm(tag + "_dwd", [a], [dhb_half], [BF16], ta=True, tm=512, tn=2048, tk=2048, host=host)
    done(outs)
    ex.grad(tag + "_w_down", dwd)
    host, done = _carried(ex, "rs", tag + "_dwgu")
    dwg, dwu, *outs = _mm(tag + "_dwgu", [n], [dg, du], [BF16, BF16], pairs=((0, 0, 0), (0, 1, 1)), ta=True,
                          tm=1024, tn=512, tk=2048, host=host)
    done(outs)
    ex.grad(tag + "_w_gate", dwg)
    ex.grad(tag + "_w_up", dwu)
    host, done = _carried(ex, "rs", tag + "_dn")
    dn, *outs = _mm(tag + "_dn", [dg, du], [wg, wu], [F32], pairs=((0, 0, 0), (1, 1, 0)), tb=True,
                    tm=1024, tn=1024, tk=1408, host=host)
    done(outs)
    return _rms_bwd(tag + "_dnorm", dn, h, gnorm, dres=dh, copy_scale=copy_scale)


ATTN_HEAD_PAIRS = 8


def _to_attn_order(a):
    S, w = a.shape
    return a.reshape(S // 16, 16, w).transpose(1, 0, 2).reshape(S, w)


def _from_attn_order(a):
    S, w = a.shape
    return a.reshape(16, S // 16, w).transpose(1, 0, 2).reshape(S, w)


def _attn_geom(S, d):
    s16 = S // 16
    if d == 16:
        return (16, s16), (1, SWA_BLOCK), (lambda r, b: (r, b)), 16, s16 // SWA_BLOCK
    if d == 4:
        return (4, 4, s16), (4, 1, SWA_BLOCK // 4), (lambda r, b: (0, r, b)), 4, s16 // (SWA_BLOCK // 4)
    return (16, s16), (16, SWA_BLOCK // 16), (lambda r, b: (0, b)), 1, s16 // (SWA_BLOCK // 16)


def _attn_pos(rho, d):
    if d == 16:
        return rho
    if d == 4:
        return 4 * (rho & 31) + (rho >> 5)
    return 16 * (rho & 7) + (rho >> 3)


def _attn_spec(S, d, lb, col, shift=0):
    _, blk, idx, _, nb = _attn_geom(S, d)
    return pl.BlockSpec(blk + (lb,), lambda r, cb, b: idx(r, jnp.clip(b + shift, 0, nb - 1)) + (col(cb),))


def _attn_view(a, d):
    return a.reshape(_attn_geom(a.shape[0], d)[0] + (a.shape[1],))


def _attn_valid(d):
    qp = _attn_pos(lax.broadcasted_iota(jnp.int32, (SWA_BLOCK, 2 * SWA_BLOCK), 0), d)
    kk = lax.broadcasted_iota(jnp.int32, (SWA_BLOCK, 2 * SWA_BLOCK), 1)
    kp = _attn_pos(kk & (SWA_BLOCK - 1), d)
    is_prev = kk < SWA_BLOCK
    return qp, kp, is_prev


def _head_masks(rows=SWA_BLOCK):
    lane = lax.broadcasted_iota(jnp.int32, (rows, LANES), 1)
    return [lane < HEAD_DIM, lane >= HEAD_DIM]


def _attn_ld(ref, sl):
    t = ref[(slice(None),) * (len(ref.shape) - 1) + (sl,)]
    return t.reshape(-1, t.shape[-1])


def _attn_st(ref, sl, val):
    ref[(slice(None),) * (len(ref.shape) - 1) + (sl,)] = val.reshape(ref.shape[:-1] + (val.shape[-1],))


def _per_head(t, first):
    sw = pltpu.roll(t, HEAD_DIM, 1)
    lo = lax.broadcasted_iota(jnp.int32, t.shape, 1) < HEAD_DIM
    return jnp.where(lo, t, sw) if first else jnp.where(lo, sw, t)


def _dot_nt(a, b):
    return lax.dot_general(a, b, (((1,), (1,)), ((), ())), preferred_element_type=F32)


def _dot_tn(a, b):
    return lax.dot_general(a, b, (((0,), (0,)), ((), ())), preferred_element_type=F32)


def _dot(a, b):
    return jnp.dot(a, b, preferred_element_type=F32)


def _keep(mask, t):
    return jnp.where(mask, t.astype(F32), 0.0).astype(BF16)


def _attn_cols(A):
    lb = min(A, LANES * ATTN_HEAD_PAIRS)
    ncol = A // lb
    return lb, ncol, [lambda cb, part=part: part * ncol + cb for part in range(3)], (lambda cb: cb)


def _attn_fwd_stage(name, qkv, d, prev, final, host=None):
    S, A3 = qkv.shape
    A = A3 // 3
    lb, ncol, (cq, ck, cv), ca = _attn_cols(A)
    view, _, _, nres, nb = _attn_geom(S, d)
    scale = HEAD_DIM ** -0.5
    has_prev = prev is not None
    n_out = 2 if final else 3
    nhi, nho = (len(host.ins), len(host.out_shapes)) if host else (0, 0)

    def body(*refs):
        q_ref, kp_ref, kc_ref, vp_ref, vc_ref = refs[:5]
        p_refs = refs[5:8] if has_prev else ()
        n_in = 5 + len(p_refs)
        hin, o_refs = refs[n_in:n_in + nhi], refs[n_in + nhi:n_in + nhi + n_out]
        hout, sems = refs[n_in + nhi + n_out:n_in + nhi + n_out + nho], refs[n_in + nhi + n_out + nho:]
        b = pl.program_id(2)
        if host:
            @pl.when((pl.program_id(0) == 0) & (pl.program_id(1) == 0) & (b == 0))
            def _():
                host.start(hin, hout, *sems)
        qp, kp_, is_prev = _attn_valid(d)
        valid = (is_prev & (kp_ >= qp) & (b > 0)) | (jnp.logical_not(is_prev) & (kp_ <= qp))
        hm, hm2 = _head_masks(), _head_masks(2 * SWA_BLOCK)
        for hp in range(lb // LANES):
            sl = slice(hp * LANES, (hp + 1) * LANES)
            q = _attn_ld(q_ref, sl)
            k2 = jnp.concatenate([_attn_ld(kp_ref, sl), _attn_ld(kc_ref, sl)], axis=0).astype(BF16)
            v2 = jnp.concatenate([_attn_ld(vp_ref, sl), _attn_ld(vc_ref, sl)], axis=0)
            o = jnp.zeros((SWA_BLOCK, LANES), F32)
            m = jnp.zeros((SWA_BLOCK, LANES), F32)
            l = jnp.zeros((SWA_BLOCK, LANES), F32)
            for hh in range(2):
                s = jnp.where(valid, _dot_nt(_keep(hm[hh], q), k2) * scale, MASK_VALUE)
                mh = jnp.max(s, axis=-1, keepdims=True)
                p = jnp.exp(s - mh)
                lh = jnp.sum(p, axis=-1, keepdims=True)
                o = o + _dot(p.astype(BF16), _keep(hm2[hh], v2))
                m = jnp.where(hm[hh], mh, m)
                l = jnp.where(hm[hh], lh, l)
            if has_prev:
                po, pm, pl_ = (_attn_ld(r, sl) for r in p_refs)
                mn = jnp.maximum(m, pm)
                w_new, w_old = jnp.exp(m - mn), jnp.exp(pm - mn)
                o = o * w_new + po * w_old
                l = l * w_new + pl_ * w_old
                m = mn
            if final:
                _attn_st(o_refs[0], sl, o / l)
                _attn_st(o_refs[1], sl, m + jnp.log(l))
            else:
                _attn_st(o_refs[0], sl, o)
                _attn_st(o_refs[1], sl, m)
                _attn_st(o_refs[2], sl, l)

        if host:
            @pl.when((pl.program_id(0) == nres - 1) & (pl.program_id(1) == ncol - 1) & (b == nb - 1))
            def _():
                host.wait(hin, hout, *sems)

    qk = _attn_view(qkv, d)
    prev_v = [_attn_view(t, d) for t in prev] if has_prev else []
    sp = functools.partial(_attn_spec, S, d, lb)
    any_spec = pl.BlockSpec(memory_space=pl.ANY)
    res = _pcall(
        body, name=name,
        grid=(nres, ncol, nb),
        in_specs=[sp(cq), sp(ck, -1), sp(ck), sp(cv, -1), sp(cv)] + [sp(ca)] * len(prev_v) + [any_spec] * nhi,
        out_specs=[sp(ca)] * n_out + [any_spec] * nho,
        out_shape=[jax.ShapeDtypeStruct(view + (A,), F32)] * n_out + (list(host.out_shapes) if host else []),
        scratch_shapes=[pltpu.SemaphoreType.DMA((host.n_sem,)), pltpu.SemaphoreType.DMA((host.n_sem,))] if host else [],
        compiler_params=_params(*(("arbitrary",) * 3 if host else ("parallel", "parallel", "arbitrary"))),
    )(qk, qk, qk, qk, qk, *prev_v, *(host.ins if host else []))
    return [t.reshape(S, A) for t in res[:n_out]], res[n_out:]


def _attn_fwd(qkv, ex):
    st = None
    for i, d in enumerate(DILATIONS):
        name = "attn_fwd_d%d" % d
        host, done = _carried(ex, "ag", name)
        st, outs = _attn_fwd_stage(name, qkv, d, st, final=(i == len(DILATIONS) - 1), host=host)
        done(outs)
    return st


def _attn_dq_stage(name, qkv, do, lse, delta, d, prev):
    S, A3 = qkv.shape
    A = A3 // 3
    lb, ncol, (cq, ck, cv), ca = _attn_cols(A)
    view, _, _, nres, nb = _attn_geom(S, d)
    scale = HEAD_DIM ** -0.5
    has_prev = prev is not None

    def body(*refs):
        q_ref, kp_ref, kc_ref, vp_ref, vc_ref, do_ref, lse_ref, dl_ref = refs[:8]
        b = pl.program_id(2)
        qp, kp_, is_prev = _attn_valid(d)
        valid = (is_prev & (kp_ >= qp) & (b > 0)) | (jnp.logical_not(is_prev) & (kp_ <= qp))
        hm, hm2 = _head_masks(), _head_masks(2 * SWA_BLOCK)
        for hp in range(lb // LANES):
            sl = slice(hp * LANES, (hp + 1) * LANES)
            q, do_, lse_, dl_ = (_attn_ld(r, sl) for r in (q_ref, do_ref, lse_ref, dl_ref))
            k2 = jnp.concatenate([_attn_ld(kp_ref, sl), _attn_ld(kc_ref, sl)], axis=0)
            v2 = jnp.concatenate([_attn_ld(vp_ref, sl), _attn_ld(vc_ref, sl)], axis=0).astype(BF16)
            k2b = k2.astype(BF16)
            dq = jnp.zeros((SWA_BLOCK, LANES), F32)
            for hh in range(2):
                doh = _keep(hm[hh], do_)
                lh, dh = _per_head(lse_, hh == 0), _per_head(dl_, hh == 0)
                lh2, dh2 = jnp.concatenate([lh, lh], axis=1), jnp.concatenate([dh, dh], axis=1)
                s = _dot_nt(_keep(hm[hh], q), k2b) * scale
                p = jnp.where(valid, jnp.exp(s - lh2), 0.0)
                ds = p * (_dot_nt(doh, v2) - dh2)
                dq = dq + _dot(ds.astype(BF16), _keep(hm2[hh], k2))
            dq = dq * scale
            if has_prev:
                dq = dq + _attn_ld(refs[8], sl)
            _attn_st(refs[-1], sl, dq)

    qk = _attn_view(qkv, d)
    acts = [_attn_view(t, d) for t in (do, lse, delta)] + ([_attn_view(prev, d)] if has_prev else [])
    sp = functools.partial(_attn_spec, S, d, lb)
    res = _pcall(
        body, name=name,
        grid=(nres, ncol, nb),
        in_specs=[sp(cq), sp(ck, -1), sp(ck), sp(cv, -1), sp(cv)] + [sp(ca)] * len(acts),
        out_specs=sp(ca),
        out_shape=jax.ShapeDtypeStruct(view + (A,), F32),
        compiler_params=_params("parallel", "parallel", "arbitrary"),
    )(qk, qk, qk, qk, qk, *acts)
    return res.reshape(S, A)


def _attn_dkv_stage(name, qkv, do, lse, delta, d, prev):
    S, A3 = qkv.shape
    A = A3 // 3
    lb, ncol, (cq, ck, cv), ca = _attn_cols(A)
    view, _, _, nres, nb = _attn_geom(S, d)
    scale = HEAD_DIM ** -0.5
    has_prev = prev is not None

    def body(*refs):
        k_ref, v_ref, qc_ref, qn_ref, doc_ref, don_ref, lc_ref, ln_ref, dc_ref, dn_ref = refs[:10]
        j = pl.program_id(2)
        rr = lax.broadcasted_iota(jnp.int32, (2 * SWA_BLOCK, SWA_BLOCK), 0)
        qp = _attn_pos(rr & (SWA_BLOCK - 1), d)
        kp_ = _attn_pos(lax.broadcasted_iota(jnp.int32, (2 * SWA_BLOCK, SWA_BLOCK), 1), d)
        valid = ((rr < SWA_BLOCK) & (kp_ <= qp)) | ((rr >= SWA_BLOCK) & (kp_ >= qp) & (j < nb - 1))
        hm2 = _head_masks(2 * SWA_BLOCK)
        for hp in range(lb // LANES):
            sl = slice(hp * LANES, (hp + 1) * LANES)
            kb, vb = _attn_ld(k_ref, sl).astype(BF16), _attn_ld(v_ref, sl).astype(BF16)
            q2 = jnp.concatenate([_attn_ld(qc_ref, sl), _attn_ld(qn_ref, sl)], axis=0)
            do2 = jnp.concatenate([_attn_ld(doc_ref, sl), _attn_ld(don_ref, sl)], axis=0)
            l2 = jnp.concatenate([_attn_ld(lc_ref, sl), _attn_ld(ln_ref, sl)], axis=0)
            d2 = jnp.concatenate([_attn_ld(dc_ref, sl), _attn_ld(dn_ref, sl)], axis=0)
            dk = jnp.zeros((SWA_BLOCK, LANES), F32)
            dv = jnp.zeros((SWA_BLOCK, LANES), F32)
            for hh in range(2):
                qh, doh = _keep(hm2[hh], q2), _keep(hm2[hh], do2)
                lh, dh = _per_head(l2, hh == 0), _per_head(d2, hh == 0)
                s = _dot_nt(qh, kb) * scale
                p = jnp.where(valid, jnp.exp(s - lh), 0.0)
                dv = dv + _dot_tn(p.astype(BF16), doh)
                ds = p * (_dot_nt(doh, vb) - dh)
                dk = dk + _dot_tn(ds.astype(BF16), qh)
            dk = dk * scale
            if has_prev:
                dk = dk + _attn_ld(refs[10], sl)
                dv = dv + _attn_ld(refs[11], sl)
            _attn_st(refs[-2], sl, dk)
            _attn_st(refs[-1], sl, dv)

    qk = _attn_view(qkv, d)
    acts = [_attn_view(t, d) for t in (do, lse, delta)]
    prev_v = [_attn_view(t, d) for t in prev] if has_prev else []
    sp = functools.partial(_attn_spec, S, d, lb)
    res = _pcall(
        body, name=name,
        grid=(nres, ncol, nb),
        in_specs=[sp(ck), sp(cv), sp(cq), sp(cq, 1), sp(ca), sp(ca, 1), sp(ca), sp(ca, 1), sp(ca), sp(ca, 1)]
        + [sp(ca)] * len(prev_v),
        out_specs=[sp(ca), sp(ca)],
        out_shape=[jax.ShapeDtypeStruct(view + (A,), F32)] * 2,
        compiler_params=_params("parallel", "parallel", "arbitrary"),
    )(qk, qk, qk, qk, acts[0], acts[0], acts[1], acts[1], acts[2], acts[2], *prev_v)
    return [t.reshape(S, A) for t in res]


def _attn_delta(dya, ya):
    S, A = ya.shape
    ri = lax.broadcasted_iota(jnp.int32, (A, A), 0) // HEAD_DIM
    ci = lax.broadcasted_iota(jnp.int32, (A, A), 1) // HEAD_DIM
    ones_bd = (ri == ci).astype(BF16)

    def fn(ins, ps):
        prod = ins[0] * ins[1]
        hi = prod.astype(BF16)
        lo = (prod - hi.astype(F32)).astype(BF16)
        return [_dot(hi, ps[0]) + _dot(lo, ps[0])], []

    return _rowwise("attn_delta", fn, [dya, ya], [ones_bd], [(A, F32)])[0]


def _attn_bwd(qkv, dya, ya, lse):
    delta = _attn_delta(dya, ya)
    dq, dkv = None, None
    for d in DILATIONS:
        dq = _attn_dq_stage("attn_dq_d%d" % d, qkv, dya, lse, delta, d, dq)
        dkv = _attn_dkv_stage("attn_dkv_d%d" % d, qkv, dya, lse, delta, d, dkv)
    return dq, dkv[0], dkv[1]


def _ssm_perm(a, T):
    S, w = a.shape
    return a.reshape(S // T, SUBLANES, T // SUBLANES, w).transpose(0, 2, 1, 3).reshape(S, w)


def _ssm_unperm(a, T):
    S, w = a.shape
    return a.reshape(S // T, T // SUBLANES, SUBLANES, w).transpose(0, 2, 1, 3).reshape(S, w)


def _ssm_powers(lam_ref, pw_ref, T, ns):
    n = (lax.broadcasted_iota(jnp.int32, (T, 1), 0) // SUBLANES + 1).astype(F32)
    mag = jnp.exp(n * lam_ref[0, 0:1, :])
    ang = n * lam_ref[0, 1:2, :]
    pw_ref[:, 0:ns] = mag * jnp.cos(ang)
    pw_ref[:, ns:2 * ns] = mag * jnp.sin(ang)


def _ssm_scan(xs, off, pw_ref, carry_ref, T, ns, reverse):
    Tc = T // SUBLANES
    sgn = -1.0 if reverse else 1.0
    ar, ai = pw_ref[0:SUBLANES, 0:ns], sgn * pw_ref[0:SUBLANES, ns:2 * ns]

    def rows(i):
        return pl.ds(pl.multiple_of(off + i * SUBLANES, SUBLANES), SUBLANES)

    def step(k, h):
        hr, hi = h
        r = rows(Tc - 1 - k if reverse else k)
        nr = ar * hr - ai * hi + xs[r, 0:ns]
        ni = ar * hi + ai * hr + xs[r, ns:2 * ns]
        xs[r, 0:ns] = nr
        xs[r, ns:2 * ns] = ni
        return nr, ni

    z = jnp.zeros((SUBLANES, ns), F32)
    er, ei = lax.fori_loop(0, Tc, step, (z, z))
    atr, ati = pw_ref[T - SUBLANES:T, 0:ns], sgn * pw_ref[T - SUBLANES:T, ns:2 * ns]
    rowid = lax.broadcasted_iota(jnp.int32, (SUBLANES, ns), 0)
    cr, ci = carry_ref[:, 0:ns], carry_ref[:, ns:2 * ns]
    ctr, cti = z, z
    for jj in range(SUBLANES):
        j = SUBLANES - 1 - jj if reverse else jj
        sel = rowid == j
        ctr, cti = jnp.where(sel, cr, ctr), jnp.where(sel, ci, cti)
        ejr = jnp.broadcast_to(jnp.sum(jnp.where(sel, er, 0.0), axis=0, keepdims=True), (SUBLANES, ns))
        eji = jnp.broadcast_to(jnp.sum(jnp.where(sel, ei, 0.0), axis=0, keepdims=True), (SUBLANES, ns))
        cr, ci = ejr + atr * cr - ati * ci, eji + atr * ci + ati * cr
    carry_ref[:, 0:ns] = cr
    carry_ref[:, ns:2 * ns] = ci

    def fix(i, _):
        r = rows(i)
        pr_rows = pl.ds(pl.multiple_of((Tc - 1 - i if reverse else i) * SUBLANES, SUBLANES), SUBLANES)
        pr, pi = pw_ref[pr_rows, 0:ns], sgn * pw_ref[pr_rows, ns:2 * ns]
        xs[r, 0:ns] += pr * ctr - pi * cti
        xs[r, ns:2 * ns] += pr * cti + pi * ctr
        return 0

    lax.fori_loop(0, Tc, fix, 0)
    return ctr, cti


def _ssm_fwd(ufp, bb, cc, lam_dt, drow, T):
    S, W = ufp.shape
    GB, cw, ns2 = bb.shape
    ns = ns2 // 2
    NCH = S // T

    def body(uf_ref, bb_ref, cc_ref, lam_ref, d_ref, y_ref, hs_ref, xs, pw, carry):
        @pl.when(pl.program_id(1) == 0)
        def _():
            _ssm_powers(lam_ref, pw, T, ns)
            carry[...] = jnp.zeros(carry.shape, F32)

        uf = uf_ref[...]
        xs[...] = _dot(uf.astype(BF16), bb_ref[0])
        hs_ref[0, 0] = carry[...]
        _ssm_scan(xs, 0, pw, carry, T, ns, reverse=False)
        y_ref[...] = _dot(xs[...].astype(BF16), cc_ref[0]) + d_ref[...] * uf

    return _pcall(
        body, name="ssm_fwd",
        grid=(GB, NCH),
        in_specs=[pl.BlockSpec((T, cw), lambda g, c: (c, g)),
                  pl.BlockSpec((1, cw, ns2), lambda g, c: (g, 0, 0)),
                  pl.BlockSpec((1, ns2, cw), lambda g, c: (g, 0, 0)),
                  pl.BlockSpec((1, 2, ns), lambda g, c: (g, 0, 0)),
                  pl.BlockSpec((1, cw), lambda g, c: (0, g))],
        out_specs=[pl.BlockSpec((T, cw), lambda g, c: (c, g)),
                   pl.BlockSpec((1, 1, SUBLANES, ns2), lambda g, c: (g, c, 0, 0))],
        out_shape=[jax.ShapeDtypeStruct((S, W), F32),
                   jax.ShapeDtypeStruct((GB, NCH, SUBLANES, ns2), F32)],
        scratch_shapes=[pltpu.VMEM((T, ns2), F32), pltpu.VMEM((T, ns2), F32), pltpu.VMEM((SUBLANES, ns2), F32)],
        compiler_params=_params("arbitrary", "arbitrary"),
    )(ufp, bb, cc, lam_dt, drow)


def _ssm_bwd(ufp, dyp, bb, bbt, cc, cct, lam_dt, drow, hstart, T):
    S, W = ufp.shape
    GB, cw, ns2 = bb.shape
    ns = ns2 // 2
    NCH = S // T

    def body(uf_ref, dy_ref, bb_ref, bbt_ref, cc_ref, cct_ref, lam_ref, d_ref, hs_ref,
             duf_ref, dbb_ref, dcc_ref, da_ref, dd_ref, hb, ls, pw, carry_f, carry_b):
        @pl.when(pl.program_id(1) == 0)
        def _():
            _ssm_powers(lam_ref, pw, T, ns)
            carry_b[...] = jnp.zeros(carry_b.shape, F32)
            dbb_ref[...] = jnp.zeros(dbb_ref.shape, F32)
            dcc_ref[...] = jnp.zeros(dcc_ref.shape, F32)
            da_ref[...] = jnp.zeros(da_ref.shape, F32)
            dd_ref[...] = jnp.zeros(dd_ref.shape, F32)

        uf, dy = uf_ref[...], dy_ref[...]
        ufb, dyb = uf.astype(BF16), dy.astype(BF16)
        hb[SUBLANES:T + SUBLANES, :] = _dot(ufb, bb_ref[0])
        carry_f[...] = hs_ref[0, 0]
        ctr, cti = _ssm_scan(hb, SUBLANES, pw, carry_f, T, ns, reverse=False)
        hb[0:SUBLANES, 0:ns] = ctr
        hb[0:SUBLANES, ns:ns2] = cti
        ls[...] = _dot(dyb, cct_ref[0])
        _ssm_scan(ls, 0, pw, carry_b, T, ns, reverse=True)
        lv = ls[...]
        lb = lv.astype(BF16)
        dbb_ref[0] += _dot_tn(ufb, lb)
        dcc_ref[0] += _dot_tn(hb[SUBLANES:T + SUBLANES, :].astype(BF16), dyb)
        lr, li = lv[:, 0:ns], lv[:, ns:ns2]
        hpr, hpi = hb[0:T, 0:ns], hb[0:T, ns:ns2]
        dar = jnp.sum(lr * hpr + li * hpi, axis=0, keepdims=True)
        dai = jnp.sum(li * hpr - lr * hpi, axis=0, keepdims=True)
        da_ref[0, 0:1, 0:ns] += dar
        da_ref[0, 0:1, ns:ns2] += dai
        duf_ref[...] = _dot(lb, bbt_ref[0]) + d_ref[...] * dy
        dd_ref[...] += jnp.sum(dy * uf, axis=0, keepdims=True)

    rc = lambda c: NCH - 1 - c
    return _pcall(
        body, name="ssm_bwd",
        grid=(GB, NCH),
        in_specs=[pl.BlockSpec((T, cw), lambda g, c: (rc(c), g)),
                  pl.BlockSpec((T, cw), lambda g, c: (rc(c), g)),
                  pl.BlockSpec((1, cw, ns2), lambda g, c: (g, 0, 0)),
                  pl.BlockSpec((1, ns2, cw), lambda g, c: (g, 0, 0)),
                  pl.BlockSpec((1, ns2, cw), lambda g, c: (g, 0, 0)),
                  pl.BlockSpec((1, cw, ns2), lambda g, c: (g, 0, 0)),
                  pl.BlockSpec((1, 2, ns), lambda g, c: (g, 0, 0)),
                  pl.BlockSpec((1, cw), lambda g, c: (0, g)),
                  pl.BlockSpec((1, 1, SUBLANES, ns2), lambda g, c: (g, rc(c), 0, 0))],
        out_specs=[pl.BlockSpec((T, cw), lambda g, c: (rc(c), g)),
                   pl.BlockSpec((1, cw, ns2), lambda g, c: (g, 0, 0)),
                   pl.BlockSpec((1, ns2, cw), lambda g, c: (g, 0, 0)),
                   pl.BlockSpec((1, SUBLANES, ns2), lambda g, c: (g, 0, 0)),
                   pl.BlockSpec((1, cw), lambda g, c: (0, g))],
        out_shape=[jax.ShapeDtypeStruct((S, W), F32),
                   jax.ShapeDtypeStruct((GB, cw, ns2), F32),
                   jax.ShapeDtypeStruct((GB, ns2, cw), F32),
                   jax.ShapeDtypeStruct((GB, SUBLANES, ns2), F32),
                   jax.ShapeDtypeStruct((1, W), F32)],
        scratch_shapes=[pltpu.VMEM((T + SUBLANES, ns2), F32), pltpu.VMEM((T, ns2), F32), pltpu.VMEM((T, ns2), F32),
                        pltpu.VMEM((SUBLANES, ns2), F32), pltpu.VMEM((SUBLANES, ns2), F32)],
        compiler_params=_params("arbitrary", "arbitrary"),
    )(ufp, dyp, bb, bbt, cc, cct, lam_dt, drow, hstart)


def _ssm_disc_math(lr, li, logdt, br, bi):
    dt = jnp.exp(logdt)
    mag = jnp.exp(lr * dt)
    ar = mag * jnp.cos(li * dt)
    ai = mag * jnp.sin(li * dt)
    nr, ni = ar - 1.0, ai
    den = lr * lr + li * li
    cr = (nr * lr + ni * li) / den
    ci = (ni * lr - nr * li) / den
    return ar, ai, cr * br - ci * bi, cr * bi + ci * br


def _ssm_disc(lr, li, logdt, br, bi):
    C = br.shape[1]

    def fn(ins, ps):
        _, _, bbr, bbi = _ssm_disc_math(*ins)
        dt = jnp.exp(ins[2])
        return [ins[0] * dt, ins[1] * dt, bbr, bbi], []

    return _rowwise("ssm_disc", fn, [lr, li, logdt, br, bi], [], [(1, F32), (1, F32), (C, F32), (C, F32)], ts=512)


def _ssm_disc_bwd(lr, li, logdt, br, bi, dar, dai, dbbr, dbbi):
    C = br.shape[1]

    def fn(ins, ps):
        _, vjp = jax.vjp(_ssm_disc_math, *ins[:5])
        return list(vjp(tuple(ins[5:]))), []

    return _rowwise("ssm_disc_bwd", fn, [lr, li, logdt, br, bi, dar, dai, dbbr, dbbi], [],
                    [(1, F32), (1, F32), (1, F32), (C, F32), (C, F32)], ts=512)


def _block_diag(t):
    GB, g, a, b = t.shape
    eye = jnp.eye(g, dtype=t.dtype)
    return (t[:, :, :, None, :] * eye[None, :, None, :, None]).reshape(GB, g * a, g * b)


def _block_diag_take(t, g):
    GB, ga, gb_ = t.shape
    a, b = ga // g, gb_ // g
    eye = jnp.eye(g, dtype=t.dtype)
    return (t.reshape(GB, g, a, g, b) * eye[None, :, None, :, None]).sum(axis=3)


def _loss_head(h4, tgt, gf):
    D = h4.shape[1]

    def fn(ins, ps):
        x, t = ins
        xh, r = _xhat(x)
        err = xh * ps[0] - t
        dn = err * (1.0 / D)
        dxh = dn * ps[0]
        dx = r * (dxh - xh * jnp.mean(dxh * xh, axis=-1, keepdims=True))
        return [dx], [jnp.sum(err * err, axis=0, keepdims=True), jnp.sum(dn * xh, axis=0, keepdims=True)]

    return _rowwise("loss_head", fn, [h4, tgt], [gf], [(D, F32)], accs=[D, D])


def _gelu(x):
    return 0.5 * x * (1.0 + jnp.tanh(GELU_C * (x + GELU_K * x * x * x)))


def _gelu_grad(x):
    t = jnp.tanh(GELU_C * (x + GELU_K * x * x * x))
    return 0.5 * (1.0 + t) + 0.5 * x * (1.0 - t * t) * GELU_C * (1.0 + 3.0 * GELU_K * x * x)


def _mesh_pos():
    return lax.axis_index("x"), lax.axis_index("y"), lax.axis_index("c")


def _other_chips(x, y):
    return [(1 - x, y), (x, 1 - y), (1 - x, 1 - y)]


def _remote(src, dst, send, recv, dev):
    return pltpu.make_async_remote_copy(src_ref=src, dst_ref=dst, send_sem=send, recv_sem=recv,
                                        device_id=dev, device_id_type=MESH)


ANY = pl.BlockSpec(memory_space=pl.ANY)


COMM_BLOCK_BYTES = 3 << 19


def _place():
    x, y, c = _mesh_pos()
    return jnp.stack([c] + [2 * cx + cy for cx, cy in _other_chips(x, y)] + [2 * x + y]).astype(jnp.int32)


def _send_chips(name, srcs, specs, tr, nth, cw):
    hr = nth * tr
    n = len(srcs)

    def body(*refs):
        got_ref, send, recv = refs[1 + n:]
        t = pl.program_id(0)
        x, y, c = _mesh_pos()
        cps = []
        for j, chip in enumerate(_other_chips(x, y)):
            dst = got_ref.at[pl.ds(pl.multiple_of(j * hr + t * tr, 16), tr), :]
            cp = _remote(refs[1 + j % n], dst, send.at[j], recv.at[j], (*chip, c))
            cp.start()
            cps.append(cp)
        for cp in cps:
            cp.wait_send()

        @pl.when(t == nth - 1)
        def _():
            for j in range(3):
                r_ = got_ref.at[pl.ds(j * hr, hr), :]
                _remote(r_, r_, send.at[j], recv.at[j], (x, y, c)).wait_recv()

    return _pcall(
        body, name=name,
        grid_spec=pltpu.PrefetchScalarGridSpec(
            num_scalar_prefetch=1, grid=(nth,), in_specs=specs, out_specs=ANY,
            scratch_shapes=[pltpu.SemaphoreType.DMA((3,)), pltpu.SemaphoreType.DMA((3,))]),
        out_shape=jax.ShapeDtypeStruct((3 * hr, cw), srcs[0].dtype),
        compiler_params=_params("arbitrary"),
    )(_place(), *srcs)


def _ag_assemble(name, shard, stage, axis, tr, nth):
    R, cc = shard.shape
    hr = nth * tr
    full = (R, N_CHIPS * cc) if axis == 1 else (N_CHIPS * R, cc)

    def body(pl_ref, s0, s1, s2, h0, h1, out_ref, send, recv, lsem):
        t = pl.program_id(0)
        x, y, c = _mesh_pos()

        def region(s, half):
            if axis == 1:
                return out_ref.at[pl.ds(pl.multiple_of(half * hr + t * tr, 16), tr), pl.ds(pl.multiple_of(s * cc, LANES), cc)]
            return out_ref.at[pl.ds(pl.multiple_of(s * R + half * hr + t * tr, 16), tr), :]

        cps = []
        for j, src in enumerate((s0, s1, s2)):
            dst = region(pl_ref[1 + j], c)
            cps.append(_remote(src, dst, send.at[j], recv, (x, y, 1 - c)))
            cps.append(pltpu.make_async_copy(src, dst, lsem.at[j]))
        for half, src in enumerate((h0, h1)):
            cps.append(pltpu.make_async_copy(src, region(pl_ref[4], half), lsem.at[3 + half]))
        for cp in cps:
            cp.start()
        for k, cp in enumerate(cps):
            if k < 6 and k % 2 == 0:
                cp.wait_send()
            else:
                cp.wait()

        @pl.when(t == nth - 1)
        def _():
            r_ = out_ref.at[pl.ds(0, hr), pl.ds(0, 3 * cc)] if axis == 1 else out_ref.at[pl.ds(0, 3 * hr), :]
            _remote(r_, r_, send.at[0], recv, (x, y, c)).wait_recv()

    blk = lambda f: pl.BlockSpec((tr, cc), f)
    return _pcall(
        body, name=name,
        grid_spec=pltpu.PrefetchScalarGridSpec(
            num_scalar_prefetch=1, grid=(nth,),
            in_specs=[blk(lambda t, p, j=j: (j * nth + t, 0)) for j in range(3)]
            + [blk(lambda t, p, h=h: (h * nth + t, 0)) for h in range(2)],
            out_specs=ANY,
            scratch_shapes=[pltpu.SemaphoreType.DMA((3,)), pltpu.SemaphoreType.DMA, pltpu.SemaphoreType.DMA((5,))]),
        out_shape=jax.ShapeDtypeStruct(full, shard.dtype),
        compiler_params=_params("arbitrary"),
    )(_place(), stage, stage, stage, shard, shard)


def _comm_rows(hr, row_bytes):
    return _tile(hr, max(16, COMM_BLOCK_BYTES // row_bytes // 16 * 16), 16)


def _host_send(items):
    def copies(ins, outs, send, recv):
        x, y, c = _mesh_pos()
        cps = []
        for w, (_, kind, hr, cw) in enumerate(items):
            for j, (cx, cy) in enumerate(_other_chips(x, y)):
                s = 2 * cx + cy
                if kind == "half":
                    src = ins[w].at[pl.ds(pl.multiple_of(c * hr, 16), hr), :]
                elif kind == "cols":
                    src = ins[w].at[:, pl.ds(pl.multiple_of(s * cw, LANES), cw)]
                else:
                    src = ins[w].at[pl.ds(pl.multiple_of(s * hr, 16), hr), :]
                cps.append(_remote(src, outs[w].at[pl.ds(j * hr, hr), :], send.at[3 * w + j], recv.at[3 * w + j], (cx, cy, c)))
        return cps

    def start(ins, outs, send, recv):
        for cp in copies(ins, outs, send, recv):
            cp.start()

    def wait(ins, outs, send, recv):
        for cp in copies(ins, outs, send, recv):
            cp.wait()

    return _Host([a for a, _, _, _ in items], [jax.ShapeDtypeStruct((3 * hr, cw), a.dtype) for a, _, hr, cw in items],
                 3 * len(items), start, wait)


def _ag_send(name, sh):
    R, cc = sh.shape
    tr = _comm_rows(R // 2, cc * 2)
    nth = R // 2 // tr
    return _send_chips(name, [sh], [pl.BlockSpec((tr, cc), lambda t, p: (p[0] * nth + t, 0))], tr, nth, cc)


def _ag_finish(name, sh, stage, axis):
    R, cc = sh.shape
    tr = _comm_rows(R // 2, cc * 2)
    return _ag_assemble(name, sh, stage, axis, tr, R // 2 // tr)


def _all_gather_weights(shards, axes):
    return [_ag_finish("ag_asm%d" % w, sh, _ag_send("ag_send%d" % w, sh), ax) for w, (sh, ax) in enumerate(zip(shards, axes))]


def _push_pair(name, src, tr, nblk, src_block, out_rows, dst_block, local):
    cw = src.shape[1]
    c_arr = lax.axis_index("c").astype(jnp.int32).reshape(1)

    def body(c_ref, src_ref, out_ref, send, recv, lsem):
        i = pl.program_id(0)
        x, y, c = _mesh_pos()
        dst = out_ref.at[pl.ds(pl.multiple_of(dst_block(i, c_ref[0]) * tr, 16), tr), :]
        cp = _remote(src_ref, dst, send, recv, (x, y, 1 - c))
        cp.start()
        if local:
            lc = pltpu.make_async_copy(src_ref, dst, lsem)
            lc.start()
            lc.wait()
        cp.wait_send()

        @pl.when(i == nblk - 1)
        def _():
            got = out_ref.at[pl.ds(0, nblk * tr), :]
            _remote(got, got, send, recv, (x, y, c)).wait_recv()

    return _pcall(
        body, name=name,
        grid_spec=pltpu.PrefetchScalarGridSpec(
            num_scalar_prefetch=1, grid=(nblk,),
            in_specs=[pl.BlockSpec((tr, cw), lambda i, c_ref: (src_block(i, c_ref[0]), 0))],
            out_specs=ANY,
            scratch_shapes=[pltpu.SemaphoreType.DMA, pltpu.SemaphoreType.DMA, pltpu.SemaphoreType.DMA]),
        out_shape=jax.ShapeDtypeStruct((out_rows, cw), src.dtype),
        compiler_params=_params("arbitrary"),
    )(c_arr, src)


def _sum_half(name, g, theirs, tr, nblk, src_block):
    cw = g.shape[1]
    c_arr = lax.axis_index("c").astype(jnp.int32).reshape(1)

    def body(c_ref, g_ref, t_ref, o_ref):
        o_ref[...] = (g_ref[...].astype(F32) + t_ref[...].astype(F32)).astype(BF16)

    return _pcall(
        body, name=name,
        grid_spec=pltpu.PrefetchScalarGridSpec(
            num_scalar_prefetch=1, grid=(nblk,),
            in_specs=[pl.BlockSpec((tr, cw), lambda i, c_ref: (src_block(i, c_ref[0]), 0)),
                      pl.BlockSpec((tr, cw), lambda i, c_ref: (i, 0))],
            out_specs=pl.BlockSpec((tr, cw), lambda i, c_ref: (i, 0))),
        out_shape=jax.ShapeDtypeStruct((nblk * tr, cw), BF16),
        compiler_params=_params("arbitrary"),
    )(c_arr, g, theirs)


def _sum_chips(name, q, got, qspec, tr, nth, cw):
    def body(p_ref, q_ref, g0, g1, g2, o_ref):
        o_ref[...] = q_ref[...].astype(F32) + g0[...].astype(F32) + g1[...].astype(F32) + g2[...].astype(F32)

    return _pcall(
        body, name=name,
        grid_spec=pltpu.PrefetchScalarGridSpec(
            num_scalar_prefetch=1, grid=(nth,),
            in_specs=[qspec] + [pl.BlockSpec((tr, cw), lambda t, p, j=j: (j * nth + t, 0)) for j in range(3)],
            out_specs=pl.BlockSpec((tr, cw), lambda t, p: (t, 0))),
        out_shape=jax.ShapeDtypeStruct((nth * tr, cw), F32),
        compiler_params=_params("arbitrary"),
    )(_place(), q, got, got, got)


def _rs_geom(g, axis):
    rows, gw = g.shape
    return (rows // 2, gw // N_CHIPS) if axis == 1 else (rows // N_CHIPS // 2, gw)


def _rs_pair_sum(tag, g, axis):
    hr, _ = _rs_geom(g, axis)
    tr = _comm_rows(hr, g.shape[1] * 2)
    nth = hr // tr
    if axis == 1:
        nblk, blk = nth, (lambda i, half: half * nth + i)
    else:
        nblk, blk = N_CHIPS * nth, (lambda i, half: (i // nth) * (2 * nth) + half * nth + i % nth)
    theirs = _push_pair("rs_pair_" + tag, g, tr, nblk, lambda i, c: blk(i, 1 - c), nblk * tr, lambda i, c: i, local=False)
    return _sum_half("rs_sum_pair_" + tag, g, theirs, tr, nblk, blk)


def _rs_part(q, axis, hr, cw, tr):
    nth = hr // tr
    if axis == 1:
        return lambda k: pl.BlockSpec((tr, cw), lambda t, p: (t, p[k]))
    return lambda k: pl.BlockSpec((tr, cw), lambda t, p: (p[k] * nth + t, 0))


def _rs_send(tag, q, axis, hr, cw):
    tr = _comm_rows(hr, cw * 2)
    part = _rs_part(q, axis, hr, cw, tr)
    return _send_chips("rs_send_" + tag, [q, q, q], [part(1), part(2), part(3)], tr, hr // tr, cw)


def _rs_finish(tag, q, got, axis, hr, cw):
    tr = _comm_rows(hr, cw * 2)
    half = _sum_chips("rs_sum_chips_" + tag, q, got, _rs_part(q, axis, hr, cw, tr)(4), tr, hr // tr, cw)
    tr = _comm_rows(hr, cw * 4)
    nth = hr // tr
    return _push_pair("rs_swap_" + tag, half, tr, nth, lambda i, c: i, 2 * hr, lambda i, c: c * nth + i, local=True)


def _reduce_scatter(grads, axes):
    outs = []
    for w, (g, ax) in enumerate(zip(grads, axes)):
        hr, cw = _rs_geom(g, ax)
        q = _rs_pair_sum(str(w), g, ax)
        outs.append(_rs_finish(str(w), q, _rs_send(str(w), q, ax, hr, cw), ax, hr, cw))
    return outs


class _Exchange:
    AG_PLAN = {"ffn1_up": ("ffn1_w_down", "ffn2_w_gate"), "ffn1_down": ("ffn2_w_up", "w_in"),
               "w_in_qkv": ("ssm_w_glu", "w_out", "ple_w_gate", "ple_w_proj"), "attn_fwd_d1": ("ffn2_w_down",)}
    RS_PLAN = {"ffn1_dact": ("ffn2_w_gate", "ple_w_gate", "ple_w_proj"), "ffn1_dwd": ("ffn2_w_up",),
               "ffn1_dwgu": ("ffn2_w_down", "w_in", "ssm_w_glu", "w_out", "ffn1_w_down"),
               "ffn1_dn": ("ffn1_w_gate", "ffn1_w_up")}

    def __init__(self, shards, axes):
        self.shards, self.axes = shards, axes
        self.stage, self.full, self.q, self.geom, self.got = {}, {}, {}, {}, {}

    def ag_host(self, kernel):
        item = lambda k: (self.shards[k], "half", self.shards[k].shape[0] // 2, self.shards[k].shape[1])
        return _host_send([item(k) for k in self.AG_PLAN[kernel]])

    def ag_done(self, kernel, stages):
        self.stage.update(zip(self.AG_PLAN[kernel], stages))

    def weight(self, k):
        if k not in self.full:
            stage = self.stage[k] if k in self.stage else _ag_send("ag_send_" + k, self.shards[k])
            self.full[k] = _ag_finish("ag_asm_" + k, self.shards[k], stage, self.axes[k])
        return self.full[k]

    def grad(self, k, g):
        self.q[k], self.geom[k] = _rs_pair_sum(k, g, self.axes[k]), _rs_geom(g, self.axes[k])

    def rs_host(self, kernel):
        item = lambda k: (self.q[k], "cols" if self.axes[k] == 1 else "rows") + self.geom[k]
        return _host_send([item(k) for k in self.RS_PLAN[kernel]])

    def rs_done(self, kernel, gots):
        self.got.update(zip(self.RS_PLAN[kernel], gots))

    def finish(self):
        return {k: _rs_finish(k, q, self.got[k] if k in self.got else _rs_send(k, q, self.axes[k], *self.geom[k]),
                              self.axes[k], *self.geom[k]) for k, q in self.q.items()}


def _all_reduce_small(v):
    n = v.shape[0]

    def body(v_ref, out_ref, buf, send, recv):
        x, y, c = _mesh_pos()
        my = 4 * x + 2 * y + c
        buf[my] = v_ref[...]
        cps = []
        for k in range(1, N_DEV):
            fx, fy, fc = (k >> 2) & 1, (k >> 1) & 1, k & 1
            peer = (1 - x if fx else x, 1 - y if fy else y, 1 - c if fc else c)
            cp = _remote(v_ref, buf.at[my], send.at[k - 1], recv.at[k - 1], peer)
            cp.start()
            cps.append((cp, 4 * peer[0] + 2 * peer[1] + peer[2]))
        for k, (cp, pid) in enumerate(cps):
            _remote(v_ref, buf.at[pid], send.at[k], recv.at[k], (x, y, c)).wait_recv()
        acc = buf[0]
        for i in range(1, N_DEV):
            acc = acc + buf[i]
        out_ref[...] = acc
        for cp, _ in cps:
            cp.wait_send()

    return _pcall(
        body, name="ar_small",
        in_specs=[pl.BlockSpec(memory_space=pltpu.VMEM)], out_specs=pl.BlockSpec(memory_space=pltpu.VMEM),
        out_shape=jax.ShapeDtypeStruct((n, LANES), F32),
        scratch_shapes=[pltpu.VMEM((N_DEV, n, LANES), F32), pltpu.SemaphoreType.DMA((N_DEV - 1,)),
                        pltpu.SemaphoreType.DMA((N_DEV - 1,))],
        compiler_params=pltpu.CompilerParams(vmem_limit_bytes=V7X_VMEM_LIMIT_BYTES),
    )(v)


def _adamw(name, w, g, m, v):
    R, Cc = w.shape
    tr = _tile(R, max(8, (1 << 19) // Cc // 8 * 8), 8)
    c1 = 1.0 - ADAM_B1 ** ADAM_STEP
    c2 = 1.0 - ADAM_B2 ** ADAM_STEP

    def body(w_ref, g_ref, m_ref, v_ref, d_ref, nm_ref, nv_ref):
        g_ = g_ref[...]
        nm = ADAM_B1 * m_ref[...] + (1.0 - ADAM_B1) * g_
        nv = ADAM_B2 * v_ref[...] + (1.0 - ADAM_B2) * (g_ * g_)
        d_ref[...] = -ADAM_LR * ((nm / c1) / (jnp.sqrt(nv / c2) + ADAM_EPS) + ADAM_WD * w_ref[...])
        nm_ref[...] = nm
        nv_ref[...] = nv

    spec = pl.BlockSpec((tr, Cc), lambda i: (i, 0))
    return _pcall(
        body, name=name, grid=(R // tr,),
        in_specs=[spec] * 4, out_specs=[spec] * 3,
        out_shape=[jax.ShapeDtypeStruct((R, Cc), F32)] * 3,
        compiler_params=_params("parallel"),
    )(w, g, m, v)


def _pack(arrs, rows):
    flat = jnp.concatenate([a.reshape(-1) for a in arrs])
    return jnp.pad(flat, (0, rows * LANES - flat.shape[0])).reshape(rows, LANES)


def _unpack(packed, like):
    flat, out, o = packed.reshape(-1), [], 0
    for a in like:
        out.append(flat[o:o + a.size].reshape(a.shape))
        o += a.size
    return out


BIG = (
    ("ffn1_w_gate", 1), ("ffn1_w_up", 1), ("ffn1_w_down", 0), ("w_in", 1), ("ssm_w_glu", 0), ("w_out", 0),
    ("ffn2_w_gate", 1), ("ffn2_w_up", 1), ("ffn2_w_down", 0), ("ple_w_gate", 0), ("ple_w_proj", 1),
)
SMALL = ("ffn1_norm", "mix_norm", "attn_out_norm", "ssm_lambda_re", "ssm_lambda_im", "ssm_log_dt", "ssm_b_re", "ssm_b_im",
         "ssm_c_re", "ssm_c_im", "ssm_d", "ssm_b_glu", "ssm_out_norm", "ffn2_norm", "ple_norm", "final_norm")
WEIGHTS = ("ffn1_norm", "ffn1_w_gate", "ffn1_w_up", "ffn1_w_down", "mix_norm", "w_in", "attn_out_norm", "ssm_lambda_re",
           "ssm_lambda_im", "ssm_log_dt", "ssm_b_re", "ssm_b_im", "ssm_c_re", "ssm_c_im", "ssm_d", "ssm_w_glu", "ssm_b_glu",
           "ssm_out_norm", "w_out", "ffn2_norm", "ffn2_w_gate", "ffn2_w_up", "ffn2_w_down", "ple_norm", "ple_w_gate",
           "ple_w_proj", "final_norm")


def _pad_to(a, axis, n):
    pad = [(0, 0), (0, 0)]
    pad[axis] = (0, n - a.shape[axis])
    return jnp.pad(a, pad)


def _local_step(x, p, tgt, w, ex):
    S, D = x.shape
    A = w["attn_out_norm"].shape[-1]
    W = w["ssm_d"].shape[-1]
    G, P = w["ssm_lambda_re"].shape[-2:]
    C = w["ssm_b_re"].shape[-1]
    GB = G // SSM_BLOCK_GROUPS
    T = min(1024, S)
    row = lambda name: w[name].reshape(1, -1)
    gs = {}

    h1, ffn1_saved = _ffn_fwd("ffn1", x, row("ffn1_norm"), ex)
    n2 = _rms_fwd("mix_norm", h1, row("mix_norm"))
    w_in = ex.weight("w_in")
    n2p = _to_attn_order(n2)
    host, done = _carried(ex, "ag", "w_in_qkv")
    qkv, *outs = _mm("w_in_qkv", [n2p], [w_in[:, :3 * A]], [F32], tm=1024, tn=1024, host=host)
    done(outs)
    (s_in,) = _mm("w_in_ssm", [n2], [w_in[:, 3 * A:]], [F32], tm=1024, tn=1024)
    ya, lse = _attn_fwd(qkv, ex)

    col = lambda name: w[name].reshape(G * P, 1)
    logdt_x = jnp.repeat(w["ssm_log_dt"].reshape(G), P).reshape(G * P, 1)
    b_re, b_im = w["ssm_b_re"].reshape(G * P, C), w["ssm_b_im"].reshape(G * P, C)
    lrdt, lidt, bbr, bbi = _ssm_disc(col("ssm_lambda_re"), col("ssm_lambda_im"), logdt_x, b_re, b_im)
    gsz = SSM_BLOCK_GROUPS
    to_bb = lambda t: _block_diag(t.reshape(GB, gsz, P, C).transpose(0, 1, 3, 2))
    bb = jnp.concatenate([to_bb(bbr), to_bb(bbi)], axis=2).astype(BF16)
    to_cc = lambda t: _block_diag(t.reshape(GB, gsz, C, P).transpose(0, 1, 3, 2))
    cc = jnp.concatenate([to_cc(w["ssm_c_re"]), -to_cc(w["ssm_c_im"])], axis=1).astype(BF16)
    lam_dt = jnp.stack([lrdt.reshape(GB, gsz * P), lidt.reshape(GB, gsz * P)], axis=1)
    ufp = _ssm_perm(s_in, T)
    ypre, hstart = _ssm_fwd(ufp, bb, cc, lam_dt, row("ssm_d"), T)

    def glu_in(ins, ps):
        yg = _gelu(ins[0])
        return [yg, yg], []

    yg, ygb = _rowwise("ssm_gelu", glu_in, [ypre], [], [(W, F32), (W, BF16)])
    w_glu = ex.weight("ssm_w_glu")

    def glu_out(accs, ex):
        gl = accs[0] + ex[1]
        return [ex[0] * _sigmoid(gl), gl]

    ybp, gl = _mm("ssm_glu", [ygb], [w_glu], [F32, F32], extras=[(yg, "mn"), (row("ssm_b_glu"), "n")],
                  epilogue=glu_out, tm=1024, tn=1024)
    yb = _ssm_unperm(ybp, T)
    na = _from_attn_order(_rms_fwd("attn_out_norm", ya, row("attn_out_norm")))
    nb = _rms_fwd("ssm_out_norm", yb, row("ssm_out_norm"))
    w_out = ex.weight("w_out")
    (h2,) = _mm("w_out", [na, nb], [w_out[:A], w_out[A:]], [F32], pairs=((0, 0, 0), (1, 1, 0)), extras=[(h1, "mn")],
                epilogue=lambda accs, ex: [ex[0] + accs[0]], tm=1024, tn=1024)
    h3, ffn2_saved = _ffn_fwd("ffn2", h2, row("ffn2_norm"), ex)
    n4 = _rms_fwd("ple_norm", h3, row("ple_norm"))
    (pe,) = _mm("ple_proj", [p], [ex.weight("ple_w_proj")], [F32], tm=1024, tn=1024)

    def ple_out(accs, ex):
        gate = _sigmoid(accs[0])
        return [ex[1] + gate * ex[0], gate]

    h4, gate = _mm("ple_gate", [n4], [ex.weight("ple_w_gate")], [F32, F32], extras=[(pe, "mn"), (h3, "mn")],
                   epilogue=ple_out, tm=1024, tn=1024)

    dh4, err2, gs["final_norm"] = _loss_head(h4, tgt, row("final_norm"))
    loss = (0.5 / D) * jnp.sum(err2)

    def ple_bwd(ins, ps):
        dh, gt, pe_ = ins
        return [dh * gt, dh * pe_ * gt * (1.0 - gt)], []

    dpe, dpg = _rowwise("ple_bwd", ple_bwd, [dh4, gate, pe], [], [(D, BF16), (D, BF16)])
    (d_ple_proj,) = _mm("ple_dproj", [p], [dpe], [BF16], ta=True, tm=256, tn=2048, tk=1024)
    (d_ple_gate,) = _mm("ple_dgate", [n4], [dpg], [BF16], ta=True, tm=1024, tn=1024, tk=2048)
    (dn4,) = _mm("ple_dn", [dpg], [ex.weight("ple_w_gate")], [F32], tb=True, tm=1024, tn=1024)
    (dh3, dh3b), gs["ple_norm"] = _rms_bwd("ple_dnorm", dn4, h3, row("ple_norm"), dres=dh4, copy_scale=0.5)
    (dh2, dh2b), gs["ffn2_norm"] = _ffn_bwd("ffn2", dh3, dh3b, h2, row("ffn2_norm"), ex, ffn2_saved, copy_scale=1.0)
    (dna,) = _mm("w_out_dna", [_to_attn_order(dh2b)], [w_out[:A]], [F32], tb=True, tm=1024, tn=1024)
    (dnb,) = _mm("w_out_dnb", [dh2b], [w_out[A:]], [F32], tb=True, tm=1024, tn=1024)
    (d_wout_a,) = _mm("w_out_dwa", [na], [dh2b], [BF16], ta=True, tm=1024, tn=1024, tk=2048)
    (d_wout_b,) = _mm("w_out_dwb", [nb], [dh2b], [BF16], ta=True, tm=1024, tn=1024, tk=2048)
    d_w_out = jnp.concatenate([d_wout_a, d_wout_b], axis=0)
    (dya,), gs["attn_out_norm"] = _rms_bwd("attn_out_dnorm", dna, ya, row("attn_out_norm"))
    (dyb,), gs["ssm_out_norm"] = _rms_bwd("ssm_out_dnorm", dnb, yb, row("ssm_out_norm"))

    dybp = _ssm_perm(dyb, T)

    def glu_bwd(ins, ps):
        dy, yg_, gl_ = ins
        sg = _sigmoid(gl_)
        dgl = dy * yg_ * sg * (1.0 - sg)
        return [dgl, dy * sg], [jnp.sum(dgl, axis=0, keepdims=True)]

    dgl, dyg_direct, gs["ssm_b_glu"] = _rowwise("ssm_glu_bwd", glu_bwd, [dybp, yg, gl], [], [(W, BF16), (W, F32)], accs=[W])
    (d_w_glu,) = _mm("ssm_dwglu", [ygb], [dgl], [BF16], ta=True, tm=1024, tn=1024, tk=2048)
    (dypre,) = _mm("ssm_dyg", [dgl], [w_glu], [F32], tb=True, extras=[(dyg_direct, "mn"), (ypre, "mn")],
                   epilogue=lambda accs, ex: [(accs[0] + ex[0]) * _gelu_grad(ex[1])], tm=1024, tn=1024)
    dufp, dbb, dcc, da, gs["ssm_d"] = _ssm_bwd(ufp, dypre, bb, bb.transpose(0, 2, 1), cc, cc.transpose(0, 2, 1),
                                               lam_dt, row("ssm_d"), hstart, T)
    ns = gsz * P
    from_bb = lambda t: _block_diag_take(t, gsz).transpose(0, 1, 3, 2).reshape(G * P, C)
    from_cc = lambda t: _block_diag_take(t, gsz).transpose(0, 1, 3, 2).reshape(w["ssm_c_re"].shape)
    gs["ssm_c_re"], gs["ssm_c_im"] = from_cc(dcc[:, :ns]), -from_cc(dcc[:, ns:])
    da = da.sum(axis=1)
    dar, dai = da[:, :ns].reshape(G * P, 1), da[:, ns:].reshape(G * P, 1)
    dlr, dli, dlogdt, dbr, dbi = _ssm_disc_bwd(col("ssm_lambda_re"), col("ssm_lambda_im"), logdt_x, b_re, b_im,
                                               dar, dai, from_bb(dbb[:, :, :ns]), from_bb(dbb[:, :, ns:]))
    gs["ssm_lambda_re"], gs["ssm_lambda_im"] = dlr.reshape(w["ssm_lambda_re"].shape), dli.reshape(w["ssm_lambda_im"].shape)
    gs["ssm_log_dt"] = dlogdt.reshape(G, P).sum(axis=1).reshape(w["ssm_log_dt"].shape)
    gs["ssm_b_re"], gs["ssm_b_im"] = dbr.reshape(w["ssm_b_re"].shape), dbi.reshape(w["ssm_b_im"].shape)
    ds_in = _ssm_unperm(dufp, T)

    dq, dk, dv = _attn_bwd(qkv, dya, ya, lse)
    dqkv = jnp.concatenate([dq, dk, dv], axis=1).astype(BF16)
    (d_w_qkv,) = _mm("w_in_dw_qkv", [n2p], [dqkv], [BF16], ta=True, tm=1024, tn=1024, tk=2048)
    (d_w_s,) = _mm("w_in_dw_ssm", [n2], [ds_in], [BF16], ta=True, tm=1024, tn=1024, tk=2048)
    d_w_in = jnp.concatenate([d_w_qkv, d_w_s], axis=1)
    dz = jnp.concatenate([_from_attn_order(dqkv), ds_in.astype(BF16)], axis=1)
    (dn2,) = _mm("w_in_dn", [dz], [w_in], [F32], tb=True, tm=1024, tn=1024)
    (dh1, dh1b), gs["mix_norm"] = _rms_bwd("mix_dnorm", dn2, h1, row("mix_norm"), dres=dh2, copy_scale=0.5)
    for k, g in (("ple_w_gate", d_ple_gate), ("ple_w_proj", d_ple_proj), ("w_out", d_w_out), ("ssm_w_glu", d_w_glu),
                 ("w_in", d_w_in)):
        ex.grad(k, g)
    (dx,), gs["ffn1_norm"] = _ffn_bwd("ffn1", dh1, dh1b, x, row("ffn1_norm"), ex, ffn1_saved, copy_scale=None)
    small = {k: gs[k].reshape(w[k].shape) for k in SMALL}
    return loss, dx, ex.finish(), small


def kernel(x, p, ffn1_norm, ffn1_w_gate, ffn1_w_up, ffn1_w_down, mix_norm, w_in, attn_out_norm, ssm_lambda_re, ssm_lambda_im, ssm_log_dt, ssm_b_re, ssm_b_im, ssm_c_re, ssm_c_im, ssm_d, ssm_w_glu, ssm_b_glu, ssm_out_norm, w_out, ffn2_norm, ffn2_w_gate, ffn2_w_up, ffn2_w_down, ple_norm, ple_w_gate, ple_w_proj, final_norm, loss_target, m_ffn1_norm, m_ffn1_w_gate, m_ffn1_w_up, m_ffn1_w_down, m_mix_norm, m_w_in, m_attn_out_norm, m_ssm_lambda_re, m_ssm_lambda_im, m_ssm_log_dt, m_ssm_b_re, m_ssm_b_im, m_ssm_c_re, m_ssm_c_im, m_ssm_d, m_ssm_w_glu, m_ssm_b_glu, m_ssm_out_norm, m_w_out, m_ffn2_norm, m_ffn2_w_gate, m_ffn2_w_up, m_ffn2_w_down, m_ple_norm, m_ple_w_gate, m_ple_w_proj, m_final_norm, v_ffn1_norm, v_ffn1_w_gate, v_ffn1_w_up, v_ffn1_w_down, v_mix_norm, v_w_in, v_attn_out_norm, v_ssm_lambda_re, v_ssm_lambda_im, v_ssm_log_dt, v_ssm_b_re, v_ssm_b_im, v_ssm_c_re, v_ssm_c_im, v_ssm_d, v_ssm_w_glu, v_ssm_b_glu, v_ssm_out_norm, v_w_out, v_ffn2_norm, v_ffn2_w_gate, v_ffn2_w_up, v_ffn2_w_down, v_ple_norm, v_ple_w_gate, v_ple_w_proj, v_final_norm):
    args = locals()
    w = {k: args[k] for k in WEIGHTS}
    m = {k: args["m_" + k] for k in WEIGHTS}
    v = {k: args["v_" + k] for k in WEIGHTS}
    w2 = {k: w[k].reshape(w[k].shape[-2:]) for k, _ in BIG}

    axes = [ax for _, ax in BIG]
    padded = {k: -(-w2[k].shape[ax] // LANES) * LANES for k, ax in BIG}
    shards = [_pad_to(w2[k].astype(BF16), ax, padded[k]) for k, ax in BIG]
    ex = _Exchange(dict(zip([k for k, _ in BIG], shards)), dict(BIG))
    loss_local, dx, summed, gsmall = _local_step(x[0], p[0, 0], loss_target[0], w, ex)
    loss = lax.psum(loss_local, MESH_AXES)
    n_small = sum(w[k].size for k in SMALL)
    rows = -(-n_small // (SUBLANES * LANES)) * SUBLANES
    gs_sum = _all_reduce_small(_pack([gsmall[k] for k in SMALL], rows))

    grads, delta, new_m, new_v = {}, {}, {}, {}
    for k, ax in BIG:
        gfull = summed[k]
        g2 = lax.slice_in_dim(gfull, 0, w2[k].shape[ax], axis=ax)
        d2, nm2, nv2 = _adamw("adamw_" + k, w2[k], g2, m[k].reshape(w2[k].shape), v[k].reshape(w2[k].shape))
        grads[k], delta[k], new_m[k], new_v[k] = (t.reshape(w[k].shape) for t in (g2, d2, nm2, nv2))
    small_like = [w[k] for k in SMALL]
    ds, nms, nvs = _adamw("adamw_small", _pack(small_like, rows), gs_sum, _pack([m[k] for k in SMALL], rows),
                          _pack([v[k] for k in SMALL], rows))
    for k, g_, d_, nm_, nv_ in zip(SMALL, _unpack(gs_sum, small_like), _unpack(ds, small_like),
                                   _unpack(nms, small_like), _unpack(nvs, small_like)):
        grads[k], delta[k], new_m[k], new_v[k] = g_, d_, nm_, nv_

    return (loss, dx[None], *[grads[k] for k in WEIGHTS], *[delta[k] for k in WEIGHTS],
            *[new_m[k] for k in WEIGHTS], *[new_v[k] for k in WEIGHTS])
```

```python
import functools
import math

import jax
import jax.numpy as jnp
from jax import lax
from jax.experimental import pallas as pl
from jax.experimental.pallas import tpu as pltpu

F32 = jnp.float32
BF16 = jnp.bfloat16
MESH = pl.DeviceIdType.MESH
MESH_AXES = ("x", "y", "c")
N_CHIPS = 4
N_DEV = 8

V7X_VMEM_LIMIT_BYTES = 56 << 20
LANES = 128
SUBLANES = 8

HEAD_DIM = 64
SWA_BLOCK = 128
DILATIONS = (1, 4, 16)
SSM_BLOCK_GROUPS = 8
NORM_EPS = 1e-6
MASK_VALUE = -1e30

ADAM_LR = 0.001
ADAM_B1 = 0.9
ADAM_B2 = 0.999
ADAM_EPS = 1e-08
ADAM_WD = 0.01
ADAM_STEP = 10

GELU_C = math.sqrt(2.0 / math.pi)
GELU_K = 0.044715


def _pcall(body, **kw):
    return pl.pallas_call(body, **kw)


def _params(*sem):
    return pltpu.CompilerParams(dimension_semantics=sem, vmem_limit_bytes=V7X_VMEM_LIMIT_BYTES)


def _tile(n, target, align):
    best = None
    for t in range(align, min(n, target) + 1, align):
        if n % t == 0:
            best = t
    return n if best is None else best


def _sigmoid(x):
    return 0.5 * jnp.tanh(0.5 * x) + 0.5


class _Host:
    def __init__(self, ins, out_shapes, n_sem, start, wait):
        self.ins, self.out_shapes, self.n_sem, self.start, self.wait = ins, out_shapes, n_sem, start, wait


def _mm(name, lhs, rhs, outs, pairs=((0, 0, 0),), epilogue=None, extras=(), ta=False, tb=False,
        tm=1024, tn=512, tk=2048, host=None):
    nl, nr, ne, no = len(lhs), len(rhs), len(extras), len(outs)
    nhi, nho = (len(host.ins), len(host.out_shapes)) if host else (0, 0)
    n_acc = 1 + max(p[2] for p in pairs)
    (K, M) = lhs[0].shape if ta else lhs[0].shape[::-1]
    (N, K2) = rhs[0].shape if tb else rhs[0].shape[::-1]
    assert K == K2, (name, lhs[0].shape, rhs[0].shape)
    tm, tn, tk = _tile(M, tm, LANES), _tile(N, tn, LANES), _tile(K, tk, LANES)
    ni, nj, nk = M // tm, N // tn, K // tk
    n_scr = n_acc if nk > 1 else 0
    if epilogue is None:
        epilogue = lambda accs, ex: accs
    dn = (((0 if ta else 1,), (1 if tb else 0,)), ((), ()))

    def body(*refs):
        refs = list(refs)
        take = lambda n: [refs.pop(0) for _ in range(n)]
        l, r, e, hin, o, hout, acc = take(nl), take(nr), take(ne), take(nhi), take(no), take(nho), take(n_scr)
        i, j, k = pl.program_id(0), pl.program_id(1), pl.program_id(2)
        if host:
            @pl.when((i == 0) & (j == 0) & (k == 0))
            def _():
                host.start(hin, hout, *refs)

        parts = [None] * n_acc
        for li, ri, ai in pairs:
            d = lax.dot_general(l[li][...].astype(BF16), r[ri][...].astype(BF16), dn,
                                preferred_element_type=F32)
            parts[ai] = d if parts[ai] is None else parts[ai] + d

        def finish(accs):
            res = epilogue(accs, [x[...] for x in e])
            for ref, val in zip(o, res):
                ref[...] = val.astype(ref.dtype)

        if nk == 1:
            finish(parts)
        else:
            @pl.when(k == 0)
            def _():
                for ai in range(n_acc):
                    acc[ai][...] = parts[ai]

            @pl.when(k > 0)
            def _():
                for ai in range(n_acc):
                    acc[ai][...] += parts[ai]

            @pl.when(k == nk - 1)
            def _():
                finish([a[...] for a in acc])

        if host:
            @pl.when((i == ni - 1) & (j == nj - 1) & (k == nk - 1))
            def _():
                host.wait(hin, hout, *refs)

    lspec = pl.BlockSpec((tk, tm), lambda i, j, k: (k, i)) if ta else pl.BlockSpec((tm, tk), lambda i, j, k: (i, k))
    rspec = pl.BlockSpec((tn, tk), lambda i, j, k: (j, k)) if tb else pl.BlockSpec((tk, tn), lambda i, j, k: (k, j))
    especs = []
    for arr, kind in extras:
        if kind == "mn":
            especs.append(pl.BlockSpec((tm, tn), lambda i, j, k: (i, j)))
        elif kind == "n":
            especs.append(pl.BlockSpec((1, tn), lambda i, j, k: (0, j)))
        else:
            especs.append(pl.BlockSpec((tm, 1), lambda i, j, k: (i, 0)))
    any_spec = pl.BlockSpec(memory_space=pl.ANY)
    sems = [pltpu.SemaphoreType.DMA((host.n_sem,)), pltpu.SemaphoreType.DMA((host.n_sem,))] if host else []
    res = _pcall(
        body, name=name,
        grid=(ni, nj, nk),
        in_specs=[lspec] * nl + [rspec] * nr + especs + [any_spec] * nhi,
        out_specs=[pl.BlockSpec((tm, tn), lambda i, j, k: (i, j))] * no + [any_spec] * nho,
        out_shape=[jax.ShapeDtypeStruct((M, N), dt) for dt in outs] + (list(host.out_shapes) if host else []),
        scratch_shapes=[pltpu.VMEM((tm, tn), F32)] * n_scr + sems,
        compiler_params=_params(*(("arbitrary",) * 3 if host else ("parallel", "parallel", "arbitrary"))),
    )(*lhs, *rhs, *[a for a, _ in extras], *(host.ins if host else []))
    return res


def _rowwise(name, fn, ins, params, outs, accs=(), ts=256):
    S = ins[0].shape[0]
    ts = _tile(S, ts, 16)
    ni, npar, no, na = len(ins), len(params), len(outs), len(accs)

    def body(*refs):
        i_refs, p_refs = refs[:ni], refs[ni:ni + npar]
        o_refs = refs[ni + npar:ni + npar + no]
        a_refs = refs[ni + npar + no:]
        res_o, res_a = fn([r[...] for r in i_refs], [r[...] for r in p_refs])
        for ref, val in zip(o_refs, res_o):
            ref[...] = val.astype(ref.dtype)
        if na:
            @pl.when(pl.program_id(0) == 0)
            def _():
                for ref in a_refs:
                    ref[...] = jnp.zeros(ref.shape, F32)

            for ref, val in zip(a_refs, res_a):
                ref[...] += val

    res = _pcall(
        body, name=name,
        grid=(S // ts,),
        in_specs=[pl.BlockSpec((ts, a.shape[1]), lambda i: (i, 0)) for a in ins]
        + [pl.BlockSpec(p.shape, lambda i: (0, 0)) for p in params],
        out_specs=[pl.BlockSpec((ts, w), lambda i: (i, 0)) for w, _ in outs]
        + [pl.BlockSpec((1, w), lambda i: (0, 0)) for w in accs],
        out_shape=[jax.ShapeDtypeStruct((S, w), dt) for w, dt in outs]
        + [jax.ShapeDtypeStruct((1, w), F32) for w in accs],
        compiler_params=_params("arbitrary"),
    )(*ins, *params)
    return res


def _xhat(x):
    r = lax.rsqrt(jnp.mean(x * x, axis=-1, keepdims=True) + NORM_EPS)
    return x * r, r


def _rms_fwd(name, x, g):
    def fn(ins, ps):
        xh, _ = _xhat(ins[0])
        return [xh * ps[0]], []

    return _rowwise(name, fn, [x], [g], [(x.shape[1], BF16)])[0]


def _rms_bwd(name, dn, x, g, dres=None, copy_scale=None):
    w = x.shape[1]

    def fn(ins, ps):
        dn_, x_ = ins[0], ins[1]
        xh, r = _xhat(x_)
        dxh = dn_ * ps[0]
        dx = r * (dxh - xh * jnp.mean(dxh * xh, axis=-1, keepdims=True))
        if dres is not None:
            dx = dx + ins[2]
        o = [dx] + ([dx * copy_scale] if copy_scale is not None else [])
        return o, [jnp.sum(dn_ * xh, axis=0, keepdims=True)]

    ins = [dn, x] + ([dres] if dres is not None else [])
    outs = [(w, F32)] + ([(w, BF16)] if copy_scale is not None else [])
    res = _rowwise(name, fn, ins, [g], outs, accs=[w])
    return res[:-1], res[-1]


def _swiglu_epilogue(accs, ex):
    g, u = accs
    sg = _sigmoid(g)
    s = g * sg
    return [u * (sg + s * (1.0 - sg)), s, s * u]


def _dswiglu_epilogue(accs, ex):
    da = accs[0]
    return [da * ex[0].astype(F32), da * ex[1].astype(F32)]


def _carried(ex, kind, kernel):
    if kernel not in (ex.AG_PLAN if kind == "ag" else ex.RS_PLAN):
        return None, lambda outs: None
    if kind == "ag":
        return ex.ag_host(kernel), lambda outs: ex.ag_done(kernel, outs)
    return ex.rs_host(kernel), lambda outs: ex.rs_done(kernel, outs)


def _ffn_fwd(tag, h, gnorm, ex):
    n = _rms_fwd(tag + "_norm", h, gnorm)
    host, done = _carried(ex, "ag", tag + "_up")
    g, u, a, *outs = _mm(tag + "_up", [n], [ex.weight(tag + "_w_gate"), ex.weight(tag + "_w_up")], [BF16, BF16, BF16],
                         pairs=((0, 0, 0), (0, 1, 1)), epilogue=_swiglu_epilogue, tm=1024, tn=512, host=host)
    done(outs)
    host, done = _carried(ex, "ag", tag + "_down")
    hout, *outs = _mm(tag + "_down", [a], [ex.weight(tag + "_w_down")], [F32], extras=[(h, "mn")],
                      epilogue=lambda accs, ex_: [ex_[0] + 0.5 * accs[0]], tm=512, tn=1024, tk=8192, host=host)
    done(outs)
    return hout, (n, g, u, a)


def _ffn_bwd(tag, dh, dhb_half, h, gnorm, ex, saved, copy_scale):
    n, g, u, a = saved
    wg, wu, wd = (ex.weight(tag + k) for k in ("_w_gate", "_w_up", "_w_down"))
    host, done = _carried(ex, "rs", tag + "_dact")
    dg, du, *outs = _mm(tag + "_dact", [dhb_half], [wd], [BF16, BF16], tb=True, extras=[(g, "mn"), (u, "mn")],
                        epilogue=_dswiglu_epilogue, tm=1024, tn=512, host=host)
    done(outs)
    host, done = _carried(ex, "rs", tag + "_dwd")
    dwd, *outs = _mm(tag + "_dwd", [a], [dhb_half], [BF16], ta=True, tm=512, tn=2048, tk=2048, host=host)
    done(outs)
    ex.grad(tag + "_w_down", dwd)
    host, done = _carried(ex, "rs", tag + "_dwgu")
    dwg, dwu, *outs = _mm(tag + "_dwgu", [n], [dg, du], [BF16, BF16], pairs=((0, 0, 0), (0, 1, 1)), ta=True,
                          tm=1024, tn=512, tk=2048, host=host)
    done(outs)
    ex.grad(tag + "_w_gate", dwg)
    ex.grad(tag + "_w_up", dwu)
    host, done = _carried(ex, "rs", tag + "_dn")
    dn, *outs = _mm(tag + "_dn", [dg, du], [wg, wu], [F32], pairs=((0, 0, 0), (1, 1, 0)), tb=True,
                    tm=1024, tn=1024, tk=1408, host=host)
    done(outs)
    return _rms_bwd(tag + "_dnorm", dn, h, gnorm, dres=dh, copy_scale=copy_scale)


ATTN_HEAD_PAIRS = 8


def _to_attn_order(a):
    S, w = a.shape
    return a.reshape(S // 16, 16, w).transpose(1, 0, 2).reshape(S, w)


def _from_attn_order(a):
    S, w = a.shape
    return a.reshape(16, S // 16, w).transpose(1, 0, 2).reshape(S, w)


def _attn_geom(S, d):
    s16 = S // 16
    if d == 16:
        return (16, s16), (1, SWA_BLOCK), (lambda r, b: (r, b)), 16, s16 // SWA_BLOCK
    if d == 4:
        return (4, 4, s16), (4, 1, SWA_BLOCK // 4), (lambda r, b: (0, r, b)), 4, s16 // (SWA_BLOCK // 4)
    return (16, s16), (16, SWA_BLOCK // 16), (lambda r, b: (0, b)), 1, s16 // (SWA_BLOCK // 16)


def _attn_pos(rho, d):
    if d == 16:
        return rho
    if d == 4:
        return 4 * (rho & 31) + (rho >> 5)
    return 16 * (rho & 7) + (rho >> 3)


def _attn_spec(S, d, lb, col, shift=0):
    _, blk, idx, _, nb = _attn_geom(S, d)
    return pl.BlockSpec(blk + (lb,), lambda r, cb, b: idx(r, jnp.clip(b + shift, 0, nb - 1)) + (col(cb),))


def _attn_view(a, d):
    return a.reshape(_attn_geom(a.shape[0], d)[0] + (a.shape[1],))


def _attn_valid(d):
    qp = _attn_pos(lax.broadcasted_iota(jnp.int32, (SWA_BLOCK, 2 * SWA_BLOCK), 0), d)
    kk = lax.broadcasted_iota(jnp.int32, (SWA_BLOCK, 2 * SWA_BLOCK), 1)
    kp = _attn_pos(kk & (SWA_BLOCK - 1), d)
    is_prev = kk < SWA_BLOCK
    return qp, kp, is_prev


def _head_masks(rows=SWA_BLOCK):
    lane = lax.broadcasted_iota(jnp.int32, (rows, LANES), 1)
    return [lane < HEAD_DIM, lane >= HEAD_DIM]


def _attn_ld(ref, sl):
    t = ref[(slice(None),) * (len(ref.shape) - 1) + (sl,)]
    return t.reshape(-1, t.shape[-1])


def _attn_st(ref, sl, val):
    ref[(slice(None),) * (len(ref.shape) - 1) + (sl,)] = val.reshape(ref.shape[:-1] + (val.shape[-1],))


def _per_head(t, first):
    sw = pltpu.roll(t, HEAD_DIM, 1)
    lo = lax.broadcasted_iota(jnp.int32, t.shape, 1) < HEAD_DIM
    return jnp.where(lo, t, sw) if first else jnp.where(lo, sw, t)


def _dot_nt(a, b):
    return lax.dot_general(a, b, (((1,), (1,)), ((), ())), preferred_element_type=F32)


def _dot_tn(a, b):
    return lax.dot_general(a, b, (((0,), (0,)), ((), ())), preferred_element_type=F32)


def _dot(a, b):
    return jnp.dot(a, b, preferred_element_type=F32)


def _keep(mask, t):
    return jnp.where(mask, t.astype(F32), 0.0).astype(BF16)


def _attn_cols(A):
    lb = min(A, LANES * ATTN_HEAD_PAIRS)
    ncol = A // lb
    return lb, ncol, [lambda cb, part=part: part * ncol + cb for part in range(3)], (lambda cb: cb)


def _attn_fwd_stage(name, qkv, d, prev, final, host=None):
    S, A3 = qkv.shape
    A = A3 // 3
    lb, ncol, (cq, ck, cv), ca = _attn_cols(A)
    view, _, _, nres, nb = _attn_geom(S, d)
    scale = HEAD_DIM ** -0.5
    has_prev = prev is not None
    n_out = 2 if final else 3
    nhi, nho = (len(host.ins), len(host.out_shapes)) if host else (0, 0)

    def body(*refs):
        q_ref, kp_ref, kc_ref, vp_ref, vc_ref = refs[:5]
        p_refs = refs[5:8] if has_prev else ()
        n_in = 5 + len(p_refs)
        hin, o_refs = refs[n_in:n_in + nhi], refs[n_in + nhi:n_in + nhi + n_out]
        hout, sems = refs[n_in + nhi + n_out:n_in + nhi + n_out + nho], refs[n_in + nhi + n_out + nho:]
        b = pl.program_id(2)
        if host:
            @pl.when((pl.program_id(0) == 0) & (pl.program_id(1) == 0) & (b == 0))
            def _():
                host.start(hin, hout, *sems)
        qp, kp_, is_prev = _attn_valid(d)
        valid = (is_prev & (kp_ >= qp) & (b > 0)) | (jnp.logical_not(is_prev) & (kp_ <= qp))
        hm, hm2 = _head_masks(), _head_masks(2 * SWA_BLOCK)
        for hp in range(lb // LANES):
            sl = slice(hp * LANES, (hp + 1) * LANES)
            q = _attn_ld(q_ref, sl)
            k2 = jnp.concatenate([_attn_ld(kp_ref, sl), _attn_ld(kc_ref, sl)], axis=0).astype(BF16)
            v2 = jnp.concatenate([_attn_ld(vp_ref, sl), _attn_ld(vc_ref, sl)], axis=0)
            o = jnp.zeros((SWA_BLOCK, LANES), F32)
            m = jnp.zeros((SWA_BLOCK, LANES), F32)
            l = jnp.zeros((SWA_BLOCK, LANES), F32)
            for hh in range(2):
                s = jnp.where(valid, _dot_nt(_keep(hm[hh], q), k2) * scale, MASK_VALUE)
                mh = jnp.max(s, axis=-1, keepdims=True)
                p = jnp.exp(s - mh)
                lh = jnp.sum(p, axis=-1, keepdims=True)
                o = o + _dot(p.astype(BF16), _keep(hm2[hh], v2))
                m = jnp.where(hm[hh], mh, m)
                l = jnp.where(hm[hh], lh, l)
            if has_prev:
                po, pm, pl_ = (_attn_ld(r, sl) for r in p_refs)
                mn = jnp.maximum(m, pm)
                w_new, w_old = jnp.exp(m - mn), jnp.exp(pm - mn)
                o = o * w_new + po * w_old
                l = l * w_new + pl_ * w_old
                m = mn
            if final:
                _attn_st(o_refs[0], sl, o / l)
                _attn_st(o_refs[1], sl, m + jnp.log(l))
            else:
                _attn_st(o_refs[0], sl, o)
                _attn_st(o_refs[1], sl, m)
                _attn_st(o_refs[2], sl, l)

        if host:
            @pl.when((pl.program_id(0) == nres - 1) & (pl.program_id(1) == ncol - 1) & (b == nb - 1))
            def _():
                host.wait(hin, hout, *sems)

    qk = _attn_view(qkv, d)
    prev_v = [_attn_view(t, d) for t in prev] if has_prev else []
    sp = functools.partial(_attn_spec, S, d, lb)
    any_spec = pl.BlockSpec(memory_space=pl.ANY)
    res = _pcall(
        body, name=name,
        grid=(nres, ncol, nb),
        in_specs=[sp(cq), sp(ck, -1), sp(ck), sp(cv, -1), sp(cv)] + [sp(ca)] * len(prev_v) + [any_spec] * nhi,
        out_specs=[sp(ca)] * n_out + [any_spec] * nho,
        out_shape=[jax.ShapeDtypeStruct(view + (A,), F32)] * n_out + (list(host.out_shapes) if host else []),
        scratch_shapes=[pltpu.SemaphoreType.DMA((host.n_sem,)), pltpu.SemaphoreType.DMA((host.n_sem,))] if host else [],
        compiler_params=_params(*(("arbitrary",) * 3 if host else ("parallel", "parallel", "arbitrary"))),
    )(qk, qk, qk, qk, qk, *prev_v, *(host.ins if host else []))
    return [t.reshape(S, A) for t in res[:n_out]], res[n_out:]


def _attn_fwd(qkv, ex):
    st = None
    for i, d in enumerate(DILATIONS):
        name = "attn_fwd_d%d" % d
        host, done = _carried(ex, "ag", name)
        st, outs = _attn_fwd_stage(name, qkv, d, st, final=(i == len(DILATIONS) - 1), host=host)
        done(outs)
    return st


def _attn_dq_stage(name, qkv, do, lse, delta, d, prev):
    S, A3 = qkv.shape
    A = A3 // 3
    lb, ncol, (cq, ck, cv), ca = _attn_cols(A)
    view, _, _, nres, nb = _attn_geom(S, d)
    scale = HEAD_DIM ** -0.5
    has_prev = prev is not None

    def body(*refs):
        q_ref, kp_ref, kc_ref, vp_ref, vc_ref, do_ref, lse_ref, dl_ref = refs[:8]
        b = pl.program_id(2)
        qp, kp_, is_prev = _attn_valid(d)
        valid = (is_prev & (kp_ >= qp) & (b > 0)) | (jnp.logical_not(is_prev) & (kp_ <= qp))
        hm, hm2 = _head_masks(), _head_masks(2 * SWA_BLOCK)
        for hp in range(lb // LANES):
            sl = slice(hp * LANES, (hp + 1) * LANES)
            q, do_, lse_, dl_ = (_attn_ld(r, sl) for r in (q_ref, do_ref, lse_ref, dl_ref))
            k2 = jnp.concatenate([_attn_ld(kp_ref, sl), _attn_ld(kc_ref, sl)], axis=0)
            v2 = jnp.concatenate([_attn_ld(vp_ref, sl), _attn_ld(vc_ref, sl)], axis=0).astype(BF16)
            k2b = k2.astype(BF16)
            dq = jnp.zeros((SWA_BLOCK, LANES), F32)
            for hh in range(2):
                doh = _keep(hm[hh], do_)
                lh, dh = _per_head(lse_, hh == 0), _per_head(dl_, hh == 0)
                lh2, dh2 = jnp.concatenate([lh, lh], axis=1), jnp.concatenate([dh, dh], axis=1)
                s = _dot_nt(_keep(hm[hh], q), k2b) * scale
                p = jnp.where(valid, jnp.exp(s - lh2), 0.0)
                ds = p * (_dot_nt(doh, v2) - dh2)
                dq = dq + _dot(ds.astype(BF16), _keep(hm2[hh], k2))
            dq = dq * scale
            if has_prev:
                dq = dq + _attn_ld(refs[8], sl)
            _attn_st(refs[-1], sl, dq)

    qk = _attn_view(qkv, d)
    acts = [_attn_view(t, d) for t in (do, lse, delta)] + ([_attn_view(prev, d)] if has_prev else [])
    sp = functools.partial(_attn_spec, S, d, lb)
    res = _pcall(
        body, name=name,
        grid=(nres, ncol, nb),
        in_specs=[sp(cq), sp(ck, -1), sp(ck), sp(cv, -1), sp(cv)] + [sp(ca)] * len(acts),
        out_specs=sp(ca),
        out_shape=jax.ShapeDtypeStruct(view + (A,), F32),
        compiler_params=_params("parallel", "parallel", "arbitrary"),
    )(qk, qk, qk, qk, qk, *acts)
    return res.reshape(S, A)


def _attn_dkv_stage(name, qkv, do, lse, delta, d, prev):
    S, A3 = qkv.shape
    A = A3 // 3
    lb, ncol, (cq, ck, cv), ca = _attn_cols(A)
    view, _, _, nres, nb = _attn_geom(S, d)
    scale = HEAD_DIM ** -0.5
    has_prev = prev is not None

    def body(*refs):
        k_ref, v_ref, qc_ref, qn_ref, doc_ref, don_ref, lc_ref, ln_ref, dc_ref, dn_ref = refs[:10]
        j = pl.program_id(2)
        rr = lax.broadcasted_iota(jnp.int32, (2 * SWA_BLOCK, SWA_BLOCK), 0)
        qp = _attn_pos(rr & (SWA_BLOCK - 1), d)
        kp_ = _attn_pos(lax.broadcasted_iota(jnp.int32, (2 * SWA_BLOCK, SWA_BLOCK), 1), d)
        valid = ((rr < SWA_BLOCK) & (kp_ <= qp)) | ((rr >= SWA_BLOCK) & (kp_ >= qp) & (j < nb - 1))
        hm2 = _head_masks(2 * SWA_BLOCK)
        for hp in range(lb // LANES):
            sl = slice(hp * LANES, (hp + 1) * LANES)
            kb, vb = _attn_ld(k_ref, sl).astype(BF16), _attn_ld(v_ref, sl).astype(BF16)
            q2 = jnp.concatenate([_attn_ld(qc_ref, sl), _attn_ld(qn_ref, sl)], axis=0)
            do2 = jnp.concatenate([_attn_ld(doc_ref, sl), _attn_ld(don_ref, sl)], axis=0)
            l2 = jnp.concatenate([_attn_ld(lc_ref, sl), _attn_ld(ln_ref, sl)], axis=0)
            d2 = jnp.concatenate([_attn_ld(dc_ref, sl), _attn_ld(dn_ref, sl)], axis=0)
            dk = jnp.zeros((SWA_BLOCK, LANES), F32)
            dv = jnp.zeros((SWA_BLOCK, LANES), F32)
            for hh in range(2):
                qh, doh = _keep(hm2[hh], q2), _keep(hm2[hh], do2)
                lh, dh = _per_head(l2, hh == 0), _per_head(d2, hh == 0)
                s = _dot_nt(qh, kb) * scale
                p = jnp.where(valid, jnp.exp(s - lh), 0.0)
                dv = dv + _dot_tn(p.astype(BF16), doh)
                ds = p * (_dot_nt(doh, vb) - dh)
                dk = dk + _dot_tn(ds.astype(BF16), qh)
            dk = dk * scale
            if has_prev:
                dk = dk + _attn_ld(refs[10], sl)
                dv = dv + _attn_ld(refs[11], sl)
            _attn_st(refs[-2], sl, dk)
            _attn_st(refs[-1], sl, dv)

    qk = _attn_view(qkv, d)
    acts = [_attn_view(t, d) for t in (do, lse, delta)]
    prev_v = [_attn_view(t, d) for t in prev] if has_prev else []
    sp = functools.partial(_attn_spec, S, d, lb)
    res = _pcall(
        body, name=name,
        grid=(nres, ncol, nb),
        in_specs=[sp(ck), sp(cv), sp(cq), sp(cq, 1), sp(ca), sp(ca, 1), sp(ca), sp(ca, 1), sp(ca), sp(ca, 1)]
        + [sp(ca)] * len(prev_v),
        out_specs=[sp(ca), sp(ca)],
        out_shape=[jax.ShapeDtypeStruct(view + (A,), F32)] * 2,
        compiler_params=_params("parallel", "parallel", "arbitrary"),
    )(qk, qk, qk, qk, acts[0], acts[0], acts[1], acts[1], acts[2], acts[2], *prev_v)
    return [t.reshape(S, A) for t in res]


def _attn_delta(dya, ya):
    S, A = ya.shape
    ri = lax.broadcasted_iota(jnp.int32, (A, A), 0) // HEAD_DIM
    ci = lax.broadcasted_iota(jnp.int32, (A, A), 1) // HEAD_DIM
    ones_bd = (ri == ci).astype(BF16)

    def fn(ins, ps):
        prod = ins[0] * ins[1]
        hi = prod.astype(BF16)
        lo = (prod - hi.astype(F32)).astype(BF16)
        return [_dot(hi, ps[0]) + _dot(lo, ps[0])], []

    return _rowwise("attn_delta", fn, [dya, ya], [ones_bd], [(A, F32)])[0]


def _attn_bwd_stage(name, qkv, do, lse, delta, d, prev):
    S, A3 = qkv.shape
    A = A3 // 3
    lb, ncol, (cq, ck, cv), ca = _attn_cols(A)
    view, _, _, nres, nb = _attn_geom(S, d)
    scale = HEAD_DIM ** -0.5
    has_prev = prev is not None
    lane_slices = [slice(hp * LANES, (hp + 1) * LANES) for hp in range(lb // LANES)]

    def body(*refs):
        q_ref, kp_ref, kc_ref, vp_ref, vc_ref, do_ref, lse_ref, dl_ref = refs[:8]
        p_refs = refs[8:11] if has_prev else ()
        dq_ref, dk_ref, dv_ref, dk_c, dv_c = refs[8 + len(p_refs):]
        b = pl.program_id(2)

        def put_keys(sl, dk, dv):
            if has_prev:
                dk, dv = dk + _attn_ld(p_refs[1], sl), dv + _attn_ld(p_refs[2], sl)
            _attn_st(dk_ref, sl, dk)
            _attn_st(dv_ref, sl, dv)

        @pl.when(b == 0)
        def _():
            dk_c[...] = jnp.zeros(dk_c.shape, F32)
            dv_c[...] = jnp.zeros(dv_c.shape, F32)

        @pl.when(b < nb)
        def _():
            qp, kp_, is_prev = _attn_valid(d)
            valid = (is_prev & (kp_ >= qp) & (b > 0)) | (jnp.logical_not(is_prev) & (kp_ <= qp))
            hm, hm2 = _head_masks(), _head_masks(2 * SWA_BLOCK)
            for sl in lane_slices:
                q, do_, lse_, dl_ = (_attn_ld(r, sl) for r in (q_ref, do_ref, lse_ref, dl_ref))
                k2 = jnp.concatenate([_attn_ld(kp_ref, sl), _attn_ld(kc_ref, sl)], axis=0)
                v2 = jnp.concatenate([_attn_ld(vp_ref, sl), _attn_ld(vc_ref, sl)], axis=0).astype(BF16)
                k2b = k2.astype(BF16)
                dq = jnp.zeros((SWA_BLOCK, LANES), F32)
                dk2 = jnp.zeros((2 * SWA_BLOCK, LANES), F32)
                dv2 = jnp.zeros((2 * SWA_BLOCK, LANES), F32)
                for hh in range(2):
                    qh, doh = _keep(hm[hh], q), _keep(hm[hh], do_)
                    lh, dh = _per_head(lse_, hh == 0), _per_head(dl_, hh == 0)
                    lh2, dh2 = jnp.concatenate([lh, lh], axis=1), jnp.concatenate([dh, dh], axis=1)
                    p = jnp.where(valid, jnp.exp(_dot_nt(qh, k2b) * scale - lh2), 0.0)
                    ds = (p * (_dot_nt(doh, v2) - dh2)).astype(BF16)
                    dq = dq + _dot(ds, _keep(hm2[hh], k2))
                    dk2 = dk2 + _dot_tn(ds, qh)
                    dv2 = dv2 + _dot_tn(p.astype(BF16), doh)
                dq, dk2 = dq * scale, dk2 * scale
                if has_prev:
                    dq = dq + _attn_ld(p_refs[0], sl)
                _attn_st(dq_ref, sl, dq)
                put_keys(sl, dk_c[:, sl] + dk2[:SWA_BLOCK], dv_c[:, sl] + dv2[:SWA_BLOCK])
                dk_c[:, sl] = dk2[SWA_BLOCK:]
                dv_c[:, sl] = dv2[SWA_BLOCK:]

        @pl.when(b == nb)
        def _():
            for sl in lane_slices:
                put_keys(sl, dk_c[:, sl], dv_c[:, sl])

    qk = _attn_view(qkv, d)
    acts = [_attn_view(t, d) for t in (do, lse, delta)] + ([_attn_view(t, d) for t in prev] if has_prev else [])
    sp = functools.partial(_attn_spec, S, d, lb)
    res = _pcall(
        body, name=name,
        grid=(nres, ncol, nb + 1),
        in_specs=[sp(cq), sp(ck, -1), sp(ck), sp(cv, -1), sp(cv), sp(ca), sp(ca), sp(ca)]
        + ([sp(ca), sp(ca, -1), sp(ca, -1)] if has_prev else []),
        out_specs=[sp(ca), sp(ca, -1), sp(ca, -1)],
        out_shape=[jax.ShapeDtypeStruct(view + (A,), F32)] * 3,
        scratch_shapes=[pltpu.VMEM((SWA_BLOCK, lb), F32)] * 2,
        compiler_params=_params("parallel", "parallel", "arbitrary"),
    )(qk, qk, qk, qk, qk, *acts)
    return [t.reshape(S, A) for t in res]


def _attn_bwd(qkv, dya, ya, lse):
    delta = _attn_delta(dya, ya)
    sums = None
    for d in DILATIONS:
        sums = _attn_bwd_stage("attn_bwd_d%d" % d, qkv, dya, lse, delta, d, sums)
    return sums


def _ssm_perm(a, T):
    S, w = a.shape
    return a.reshape(S // T, SUBLANES, T // SUBLANES, w).transpose(0, 2, 1, 3).reshape(S, w)


def _ssm_unperm(a, T):
    S, w = a.shape
    return a.reshape(S // T, T // SUBLANES, SUBLANES, w).transpose(0, 2, 1, 3).reshape(S, w)


def _ssm_powers(lam_ref, pw_ref, T, ns):
    n = (lax.broadcasted_iota(jnp.int32, (T, 1), 0) // SUBLANES + 1).astype(F32)
    mag = jnp.exp(n * lam_ref[0, 0:1, :])
    ang = n * lam_ref[0, 1:2, :]
    pw_ref[:, 0:ns] = mag * jnp.cos(ang)
    pw_ref[:, ns:2 * ns] = mag * jnp.sin(ang)


def _ssm_scan(xs, off, pw_ref, carry_ref, T, ns, reverse):
    Tc = T // SUBLANES
    sgn = -1.0 if reverse else 1.0
    ar, ai = pw_ref[0:SUBLANES, 0:ns], sgn * pw_ref[0:SUBLANES, ns:2 * ns]

    def rows(i):
        return pl.ds(pl.multiple_of(off + i * SUBLANES, SUBLANES), SUBLANES)

    def step(k, h):
        hr, hi = h
        r = rows(Tc - 1 - k if reverse else k)
        nr = ar * hr - ai * hi + xs[r, 0:ns]
        ni = ar * hi + ai * hr + xs[r, ns:2 * ns]
        xs[r, 0:ns] = nr
        xs[r, ns:2 * ns] = ni
        return nr, ni

    z = jnp.zeros((SUBLANES, ns), F32)
    er, ei = lax.fori_loop(0, Tc, step, (z, z), unroll=4)
    atr, ati = pw_ref[T - SUBLANES:T, 0:ns], sgn * pw_ref[T - SUBLANES:T, ns:2 * ns]
    rowid = lax.broadcasted_iota(jnp.int32, (SUBLANES, ns), 0)
    cr, ci = carry_ref[:, 0:ns], carry_ref[:, ns:2 * ns]
    ctr, cti = z, z
    for jj in range(SUBLANES):
        j = SUBLANES - 1 - jj if reverse else jj
        sel = rowid == j
        ctr, cti = jnp.where(sel, cr, ctr), jnp.where(sel, ci, cti)
        ejr = jnp.broadcast_to(jnp.sum(jnp.where(sel, er, 0.0), axis=0, keepdims=True), (SUBLANES, ns))
        eji = jnp.broadcast_to(jnp.sum(jnp.where(sel, ei, 0.0), axis=0, keepdims=True), (SUBLANES, ns))
        cr, ci = ejr + atr * cr - ati * ci, eji + atr * ci + ati * cr
    carry_ref[:, 0:ns] = cr
    carry_ref[:, ns:2 * ns] = ci

    def fix(i, _):
        r = rows(i)
        pr_rows = pl.ds(pl.multiple_of((Tc - 1 - i if reverse else i) * SUBLANES, SUBLANES), SUBLANES)
        pr, pi = pw_ref[pr_rows, 0:ns], sgn * pw_ref[pr_rows, ns:2 * ns]
        xs[r, 0:ns] += pr * ctr - pi * cti
        xs[r, ns:2 * ns] += pr * cti + pi * ctr
        return 0

    lax.fori_loop(0, Tc, fix, 0, unroll=4)
    return ctr, cti


def _ssm_fwd(ufp, bb, cc, lam_dt, drow, T):
    S, W = ufp.shape
    GB, cw, ns2 = bb.shape
    ns = ns2 // 2
    NCH = S // T

    def body(uf_ref, bb_ref, cc_ref, lam_ref, d_ref, y_ref, hs_ref, xs, pw, carry):
        @pl.when(pl.program_id(1) == 0)
        def _():
            _ssm_powers(lam_ref, pw, T, ns)
            carry[...] = jnp.zeros(carry.shape, F32)

        uf = uf_ref[...]
        xs[...] = _dot(uf.astype(BF16), bb_ref[0])
        hs_ref[0, 0] = carry[...]
        _ssm_scan(xs, 0, pw, carry, T, ns, reverse=False)
        y_ref[...] = _dot(xs[...].astype(BF16), cc_ref[0]) + d_ref[...] * uf

    return _pcall(
        body, name="ssm_fwd",
        grid=(GB, NCH),
        in_specs=[pl.BlockSpec((T, cw), lambda g, c: (c, g)),
                  pl.BlockSpec((1, cw, ns2), lambda g, c: (g, 0, 0)),
                  pl.BlockSpec((1, ns2, cw), lambda g, c: (g, 0, 0)),
                  pl.BlockSpec((1, 2, ns), lambda g, c: (g, 0, 0)),
                  pl.BlockSpec((1, cw), lambda g, c: (0, g))],
        out_specs=[pl.BlockSpec((T, cw), lambda g, c: (c, g)),
                   pl.BlockSpec((1, 1, SUBLANES, ns2), lambda g, c: (g, c, 0, 0))],
        out_shape=[jax.ShapeDtypeStruct((S, W), F32),
                   jax.ShapeDtypeStruct((GB, NCH, SUBLANES, ns2), F32)],
        scratch_shapes=[pltpu.VMEM((T, ns2), F32), pltpu.VMEM((T, ns2), F32), pltpu.VMEM((SUBLANES, ns2), F32)],
        compiler_params=_params("arbitrary", "arbitrary"),
    )(ufp, bb, cc, lam_dt, drow)


def _ssm_bwd(ufp, dyp, bb, bbt, cc, cct, lam_dt, drow, hstart, T):
    S, W = ufp.shape
    GB, cw, ns2 = bb.shape
    ns = ns2 // 2
    NCH = S // T

    def body(uf_ref, dy_ref, bb_ref, bbt_ref, cc_ref, cct_ref, lam_ref, d_ref, hs_ref,
             duf_ref, dbb_ref, dcc_ref, da_ref, dd_ref, hb, ls, pw, carry_f, carry_b):
        @pl.when(pl.program_id(1) == 0)
        def _():
            _ssm_powers(lam_ref, pw, T, ns)
            carry_b[...] = jnp.zeros(carry_b.shape, F32)
            dbb_ref[...] = jnp.zeros(dbb_ref.shape, F32)
            dcc_ref[...] = jnp.zeros(dcc_ref.shape, F32)
            da_ref[...] = jnp.zeros(da_ref.shape, F32)
            dd_ref[...] = jnp.zeros(dd_ref.shape, F32)

        uf, dy = uf_ref[...], dy_ref[...]
        ufb, dyb = uf.astype(BF16), dy.astype(BF16)
        hb[SUBLANES:T + SUBLANES, :] = _dot(ufb, bb_ref[0])
        carry_f[...] = hs_ref[0, 0]
        ctr, cti = _ssm_scan(hb, SUBLANES, pw, carry_f, T, ns, reverse=False)
        hb[0:SUBLANES, 0:ns] = ctr
        hb[0:SUBLANES, ns:ns2] = cti
        ls[...] = _dot(dyb, cct_ref[0])
        _ssm_scan(ls, 0, pw, carry_b, T, ns, reverse=True)
        lv = ls[...]
        lb = lv.astype(BF16)
        dbb_ref[0] += _dot_tn(ufb, lb)
        dcc_ref[0] += _dot_tn(hb[SUBLANES:T + SUBLANES, :].astype(BF16), dyb)
        lr, li = lv[:, 0:ns], lv[:, ns:ns2]
        hpr, hpi = hb[0:T, 0:ns], hb[0:T, ns:ns2]
        dar = jnp.sum(lr * hpr + li * hpi, axis=0, keepdims=True)
        dai = jnp.sum(li * hpr - lr * hpi, axis=0, keepdims=True)
        da_ref[0, 0:1, 0:ns] += dar
        da_ref[0, 0:1, ns:ns2] += dai
        duf_ref[...] = _dot(lb, bbt_ref[0]) + d_ref[...] * dy
        dd_ref[...] += jnp.sum(dy * uf, axis=0, keepdims=True)

    rc = lambda c: NCH - 1 - c
    return _pcall(
        body, name="ssm_bwd",
        grid=(GB, NCH),
        in_specs=[pl.BlockSpec((T, cw), lambda g, c: (rc(c), g)),
                  pl.BlockSpec((T, cw), lambda g, c: (rc(c), g)),
                  pl.BlockSpec((1, cw, ns2), lambda g, c: (g, 0, 0)),
                  pl.BlockSpec((1, ns2, cw), lambda g, c: (g, 0, 0)),
                  pl.BlockSpec((1, ns2, cw), lambda g, c: (g, 0, 0)),
                  pl.BlockSpec((1, cw, ns2), lambda g, c: (g, 0, 0)),
                  pl.BlockSpec((1, 2, ns), lambda g, c: (g, 0, 0)),
                  pl.BlockSpec((1, cw), lambda g, c: (0, g)),
                  pl.BlockSpec((1, 1, SUBLANES, ns2), lambda g, c: (g, rc(c), 0, 0))],
        out_specs=[pl.BlockSpec((T, cw), lambda g, c: (rc(c), g)),
                   pl.BlockSpec((1, cw, ns2), lambda g, c: (g, 0, 0)),
                   pl.BlockSpec((1, ns2, cw), lambda g, c: (g, 0, 0)),
                   pl.BlockSpec((1, SUBLANES, ns2), lambda g, c: (g, 0, 0)),
                   pl.BlockSpec((1, cw), lambda g, c: (0, g))],
        out_shape=[jax.ShapeDtypeStruct((S, W), F32),
                   jax.ShapeDtypeStruct((GB, cw, ns2), F32),
                   jax.ShapeDtypeStruct((GB, ns2, cw), F32),
                   jax.ShapeDtypeStruct((GB, SUBLANES, ns2), F32),
                   jax.ShapeDtypeStruct((1, W), F32)],
        scratch_shapes=[pltpu.VMEM((T + SUBLANES, ns2), F32), pltpu.VMEM((T, ns2), F32), pltpu.VMEM((T, ns2), F32),
                        pltpu.VMEM((SUBLANES, ns2), F32), pltpu.VMEM((SUBLANES, ns2), F32)],
        compiler_params=_params("arbitrary", "arbitrary"),
    )(ufp, dyp, bb, bbt, cc, cct, lam_dt, drow, hstart)


def _ssm_disc_math(lr, li, logdt, br, bi):
    dt = jnp.exp(logdt)
    mag = jnp.exp(lr * dt)
    ar = mag * jnp.cos(li * dt)
    ai = mag * jnp.sin(li * dt)
    nr, ni = ar - 1.0, ai
    den = lr * lr + li * li
    cr = (nr * lr + ni * li) / den
    ci = (ni * lr - nr * li) / den
    return ar, ai, cr * br - ci * bi, cr * bi + ci * br


def _ssm_disc(lr, li, logdt, br, bi):
    C = br.shape[1]

    def fn(ins, ps):
        _, _, bbr, bbi = _ssm_disc_math(*ins)
        dt = jnp.exp(ins[2])
        return [ins[0] * dt, ins[1] * dt, bbr, bbi], []

    return _rowwise("ssm_disc", fn, [lr, li, logdt, br, bi], [], [(1, F32), (1, F32), (C, F32), (C, F32)], ts=512)


def _ssm_disc_bwd(lr, li, logdt, br, bi, dar, dai, dbbr, dbbi):
    C = br.shape[1]

    def fn(ins, ps):
        _, vjp = jax.vjp(_ssm_disc_math, *ins[:5])
        return list(vjp(tuple(ins[5:]))), []

    return _rowwise("ssm_disc_bwd", fn, [lr, li, logdt, br, bi, dar, dai, dbbr, dbbi], [],
                    [(1, F32), (1, F32), (1, F32), (C, F32), (C, F32)], ts=512)


def _block_diag(t):
    GB, g, a, b = t.shape
    eye = jnp.eye(g, dtype=t.dtype)
    return (t[:, :, :, None, :] * eye[None, :, None, :, None]).reshape(GB, g * a, g * b)


def _block_diag_take(t, g):
    GB, ga, gb_ = t.shape
    a, b = ga // g, gb_ // g
    eye = jnp.eye(g, dtype=t.dtype)
    return (t.reshape(GB, g, a, g, b) * eye[None, :, None, :, None]).sum(axis=3)


def _loss_head(h4, tgt, gf):
    D = h4.shape[1]

    def fn(ins, ps):
        x, t = ins
        xh, r = _xhat(x)
        err = xh * ps[0] - t
        dn = err * (1.0 / D)
        dxh = dn * ps[0]
        dx = r * (dxh - xh * jnp.mean(dxh * xh, axis=-1, keepdims=True))
        return [dx], [jnp.sum(err * err, axis=0, keepdims=True), jnp.sum(dn * xh, axis=0, keepdims=True)]

    return _rowwise("loss_head", fn, [h4, tgt], [gf], [(D, F32)], accs=[D, D])


def _gelu(x):
    return 0.5 * x * (1.0 + jnp.tanh(GELU_C * (x + GELU_K * x * x * x)))


def _gelu_grad(x):
    t = jnp.tanh(GELU_C * (x + GELU_K * x * x * x))
    return 0.5 * (1.0 + t) + 0.5 * x * (1.0 - t * t) * GELU_C * (1.0 + 3.0 * GELU_K * x * x)


def _mesh_pos():
    return lax.axis_index("x"), lax.axis_index("y"), lax.axis_index("c")


def _other_chips(x, y):
    return [(1 - x, y), (x, 1 - y), (1 - x, 1 - y)]


def _remote(src, dst, send, recv, dev):
    return pltpu.make_async_remote_copy(src_ref=src, dst_ref=dst, send_sem=send, recv_sem=recv,
                                        device_id=dev, device_id_type=MESH)


ANY = pl.BlockSpec(memory_space=pl.ANY)


COMM_BLOCK_BYTES = 3 << 19


def _place():
    x, y, c = _mesh_pos()
    return jnp.stack([c] + [2 * cx + cy for cx, cy in _other_chips(x, y)] + [2 * x + y]).astype(jnp.int32)


def _send_chips(name, srcs, specs, tr, nth, cw):
    hr = nth * tr
    n = len(srcs)

    def body(*refs):
        got_ref, send, recv = refs[1 + n:]
        t = pl.program_id(0)
        x, y, c = _mesh_pos()
        cps = []
        for j, chip in enumerate(_other_chips(x, y)):
            dst = got_ref.at[pl.ds(pl.multiple_of(j * hr + t * tr, 16), tr), :]
            cp = _remote(refs[1 + j % n], dst, send.at[j], recv.at[j], (*chip, c))
            cp.start()
            cps.append(cp)
        for cp in cps:
            cp.wait_send()

        @pl.when(t == nth - 1)
        def _():
            for j in range(3):
                r_ = got_ref.at[pl.ds(j * hr, hr), :]
                _remote(r_, r_, send.at[j], recv.at[j], (x, y, c)).wait_recv()

    return _pcall(
        body, name=name,
        grid_spec=pltpu.PrefetchScalarGridSpec(
            num_scalar_prefetch=1, grid=(nth,), in_specs=specs, out_specs=ANY,
            scratch_shapes=[pltpu.SemaphoreType.DMA((3,)), pltpu.SemaphoreType.DMA((3,))]),
        out_shape=jax.ShapeDtypeStruct((3 * hr, cw), srcs[0].dtype),
        compiler_params=_params("arbitrary"),
    )(_place(), *srcs)


def _ag_assemble(name, shard, stage, axis, tr, nth):
    R, cc = shard.shape
    hr = nth * tr
    full = (R, N_CHIPS * cc) if axis == 1 else (N_CHIPS * R, cc)

    def body(pl_ref, s0, s1, s2, h0, h1, out_ref, send, recv, lsem):
        t = pl.program_id(0)
        x, y, c = _mesh_pos()

        def region(s, half):
            if axis == 1:
                return out_ref.at[pl.ds(pl.multiple_of(half * hr + t * tr, 16), tr), pl.ds(pl.multiple_of(s * cc, LANES), cc)]
            return out_ref.at[pl.ds(pl.multiple_of(s * R + half * hr + t * tr, 16), tr), :]

        cps = []
        for j, src in enumerate((s0, s1, s2)):
            dst = region(pl_ref[1 + j], c)
            cps.append(_remote(src, dst, send.at[j], recv, (x, y, 1 - c)))
            cps.append(pltpu.make_async_copy(src, dst, lsem.at[j]))
        for half, src in enumerate((h0, h1)):
            cps.append(pltpu.make_async_copy(src, region(pl_ref[4], half), lsem.at[3 + half]))
        for cp in cps:
            cp.start()
        for k, cp in enumerate(cps):
            if k < 6 and k % 2 == 0:
                cp.wait_send()
            else:
                cp.wait()

        @pl.when(t == nth - 1)
        def _():
            r_ = out_ref.at[pl.ds(0, hr), pl.ds(0, 3 * cc)] if axis == 1 else out_ref.at[pl.ds(0, 3 * hr), :]
            _remote(r_, r_, send.at[0], recv, (x, y, c)).wait_recv()

    blk = lambda f: pl.BlockSpec((tr, cc), f)
    return _pcall(
        body, name=name,
        grid_spec=pltpu.PrefetchScalarGridSpec(
            num_scalar_prefetch=1, grid=(nth,),
            in_specs=[blk(lambda t, p, j=j: (j * nth + t, 0)) for j in range(3)]
            + [blk(lambda t, p, h=h: (h * nth + t, 0)) for h in range(2)],
            out_specs=ANY,
            scratch_shapes=[pltpu.SemaphoreType.DMA((3,)), pltpu.SemaphoreType.DMA, pltpu.SemaphoreType.DMA((5,))]),
        out_shape=jax.ShapeDtypeStruct(full, shard.dtype),
        compiler_params=_params("arbitrary"),
    )(_place(), stage, stage, stage, shard, shard)


def _comm_rows(hr, row_bytes):
    return _tile(hr, max(16, COMM_BLOCK_BYTES // row_bytes // 16 * 16), 16)


def _host_send(items):
    def copies(ins, outs, send, recv):
        x, y, c = _mesh_pos()
        cps = []
        for w, (_, kind, hr, cw) in enumerate(items):
            for j, (cx, cy) in enumerate(_other_chips(x, y)):
                s = 2 * cx + cy
                if kind == "half":
                    src = ins[w].at[pl.ds(pl.multiple_of(c * hr, 16), hr), :]
                elif kind == "cols":
                    src = ins[w].at[:, pl.ds(pl.multiple_of(s * cw, LANES), cw)]
                else:
                    src = ins[w].at[pl.ds(pl.multiple_of(s * hr, 16), hr), :]
                cps.append(_remote(src, outs[w].at[pl.ds(j * hr, hr), :], send.at[3 * w + j], recv.at[3 * w + j], (cx, cy, c)))
        return cps

    def start(ins, outs, send, recv):
        for cp in copies(ins, outs, send, recv):
            cp.start()

    def wait(ins, outs, send, recv):
        for cp in copies(ins, outs, send, recv):
            cp.wait()

    return _Host([a for a, _, _, _ in items], [jax.ShapeDtypeStruct((3 * hr, cw), a.dtype) for a, _, hr, cw in items],
                 3 * len(items), start, wait)


def _ag_send(name, sh):
    R, cc = sh.shape
    tr = _comm_rows(R // 2, cc * 2)
    nth = R // 2 // tr
    return _send_chips(name, [sh], [pl.BlockSpec((tr, cc), lambda t, p: (p[0] * nth + t, 0))], tr, nth, cc)


def _ag_finish(name, sh, stage, axis):
    R, cc = sh.shape
    tr = _comm_rows(R // 2, cc * 2)
    return _ag_assemble(name, sh, stage, axis, tr, R // 2 // tr)


def _all_gather_weights(shards, axes):
    return [_ag_finish("ag_asm%d" % w, sh, _ag_send("ag_send%d" % w, sh), ax) for w, (sh, ax) in enumerate(zip(shards, axes))]


def _push_pair(name, src, tr, nblk, src_block, out_rows, dst_block, local):
    cw = src.shape[1]
    c_arr = lax.axis_index("c").astype(jnp.int32).reshape(1)

    def body(c_ref, src_ref, out_ref, send, recv, lsem):
        i = pl.program_id(0)
        x, y, c = _mesh_pos()
        dst = out_ref.at[pl.ds(pl.multiple_of(dst_block(i, c_ref[0]) * tr, 16), tr), :]
        cp = _remote(src_ref, dst, send, recv, (x, y, 1 - c))
        cp.start()
        if local:
            lc = pltpu.make_async_copy(src_ref, dst, lsem)
            lc.start()
            lc.wait()
        cp.wait_send()

        @pl.when(i == nblk - 1)
        def _():
            got = out_ref.at[pl.ds(0, nblk * tr), :]
            _remote(got, got, send, recv, (x, y, c)).wait_recv()

    return _pcall(
        body, name=name,
        grid_spec=pltpu.PrefetchScalarGridSpec(
            num_scalar_prefetch=1, grid=(nblk,),
            in_specs=[pl.BlockSpec((tr, cw), lambda i, c_ref: (src_block(i, c_ref[0]), 0))],
            out_specs=ANY,
            scratch_shapes=[pltpu.SemaphoreType.DMA, pltpu.SemaphoreType.DMA, pltpu.SemaphoreType.DMA]),
        out_shape=jax.ShapeDtypeStruct((out_rows, cw), src.dtype),
        compiler_params=_params("arbitrary"),
    )(c_arr, src)


def _sum_half(name, g, theirs, tr, nblk, src_block):
    cw = g.shape[1]
    c_arr = lax.axis_index("c").astype(jnp.int32).reshape(1)

    def body(c_ref, g_ref, t_ref, o_ref):
        o_ref[...] = (g_ref[...].astype(F32) + t_ref[...].astype(F32)).astype(BF16)

    return _pcall(
        body, name=name,
        grid_spec=pltpu.PrefetchScalarGridSpec(
            num_scalar_prefetch=1, grid=(nblk,),
            in_specs=[pl.BlockSpec((tr, cw), lambda i, c_ref: (src_block(i, c_ref[0]), 0)),
                      pl.BlockSpec((tr, cw), lambda i, c_ref: (i, 0))],
            out_specs=pl.BlockSpec((tr, cw), lambda i, c_ref: (i, 0))),
        out_shape=jax.ShapeDtypeStruct((nblk * tr, cw), BF16),
        compiler_params=_params("arbitrary"),
    )(c_arr, g, theirs)


def _sum_chips(name, q, got, qspec, tr, nth, cw):
    def body(p_ref, q_ref, g0, g1, g2, o_ref):
        o_ref[...] = q_ref[...].astype(F32) + g0[...].astype(F32) + g1[...].astype(F32) + g2[...].astype(F32)

    return _pcall(
        body, name=name,
        grid_spec=pltpu.PrefetchScalarGridSpec(
            num_scalar_prefetch=1, grid=(nth,),
            in_specs=[qspec] + [pl.BlockSpec((tr, cw), lambda t, p, j=j: (j * nth + t, 0)) for j in range(3)],
            out_specs=pl.BlockSpec((tr, cw), lambda t, p: (t, 0))),
        out_shape=jax.ShapeDtypeStruct((nth * tr, cw), F32),
        compiler_params=_params("arbitrary"),
    )(_place(), q, got, got, got)


def _rs_geom(g, axis):
    rows, gw = g.shape
    return (rows // 2, gw // N_CHIPS) if axis == 1 else (rows // N_CHIPS // 2, gw)


def _rs_pair_sum(tag, g, axis):
    hr, _ = _rs_geom(g, axis)
    tr = _comm_rows(hr, g.shape[1] * 2)
    nth = hr // tr
    if axis == 1:
        nblk, blk = nth, (lambda i, half: half * nth + i)
    else:
        nblk, blk = N_CHIPS * nth, (lambda i, half: (i // nth) * (2 * nth) + half * nth + i % nth)
    theirs = _push_pair("rs_pair_" + tag, g, tr, nblk, lambda i, c: blk(i, 1 - c), nblk * tr, lambda i, c: i, local=False)
    return _sum_half("rs_sum_pair_" + tag, g, theirs, tr, nblk, blk)


def _rs_part(q, axis, hr, cw, tr):
    nth = hr // tr
    if axis == 1:
        return lambda k: pl.BlockSpec((tr, cw), lambda t, p: (t, p[k]))
    return lambda k: pl.BlockSpec((tr, cw), lambda t, p: (p[k] * nth + t, 0))


def _rs_send(tag, q, axis, hr, cw):
    tr = _comm_rows(hr, cw * 2)
    part = _rs_part(q, axis, hr, cw, tr)
    return _send_chips("rs_send_" + tag, [q, q, q], [part(1), part(2), part(3)], tr, hr // tr, cw)


def _rs_finish(tag, q, got, axis, hr, cw):
    tr = _comm_rows(hr, cw * 2)
    half = _sum_chips("rs_sum_chips_" + tag, q, got, _rs_part(q, axis, hr, cw, tr)(4), tr, hr // tr, cw)
    tr = _comm_rows(hr, cw * 4)
    nth = hr // tr
    return _push_pair("rs_swap_" + tag, half, tr, nth, lambda i, c: i, 2 * hr, lambda i, c: c * nth + i, local=True)


def _reduce_scatter(grads, axes):
    outs = []
    for w, (g, ax) in enumerate(zip(grads, axes)):
        hr, cw = _rs_geom(g, ax)
        q = _rs_pair_sum(str(w), g, ax)
        outs.append(_rs_finish(str(w), q, _rs_send(str(w), q, ax, hr, cw), ax, hr, cw))
    return outs


class _Exchange:
    AG_PLAN = {"ffn1_up": ("ffn1_w_down", "ffn2_w_gate"), "ffn1_down": ("ffn2_w_up", "w_in"),
               "w_in_qkv": ("ssm_w_glu", "w_out", "ple_w_gate", "ple_w_proj"), "attn_fwd_d1": ("ffn2_w_down",)}
    RS_PLAN = {"ffn1_dact": ("ffn2_w_gate", "ple_w_gate", "ple_w_proj"), "ffn1_dwd": ("ffn2_w_up",),
               "ffn1_dwgu": ("ffn2_w_down", "w_in", "ssm_w_glu", "w_out", "ffn1_w_down"),
               "ffn1_dn": ("ffn1_w_gate", "ffn1_w_up")}

    def __init__(self, shards, axes):
        self.shards, self.axes = shards, axes
        self.stage, self.full, self.q, self.geom, self.got = {}, {}, {}, {}, {}

    def ag_host(self, kernel):
        item = lambda k: (self.shards[k], "half", self.shards[k].shape[0] // 2, self.shards[k].shape[1])
        return _host_send([item(k) for k in self.AG_PLAN[kernel]])

    def ag_done(self, kernel, stages):
        self.stage.update(zip(self.AG_PLAN[kernel], stages))

    def weight(self, k):
        if k not in self.full:
            stage = self.stage[k] if k in self.stage else _ag_send("ag_send_" + k, self.shards[k])
            self.full[k] = _ag_finish("ag_asm_" + k, self.shards[k], stage, self.axes[k])
        return self.full[k]

    def grad(self, k, g):
        self.q[k], self.geom[k] = _rs_pair_sum(k, g, self.axes[k]), _rs_geom(g, self.axes[k])

    def rs_host(self, kernel):
        item = lambda k: (self.q[k], "cols" if self.axes[k] == 1 else "rows") + self.geom[k]
        return _host_send([item(k) for k in self.RS_PLAN[kernel]])

    def rs_done(self, kernel, gots):
        self.got.update(zip(self.RS_PLAN[kernel], gots))

    def finish(self):
        return {k: _rs_finish(k, q, self.got[k] if k in self.got else _rs_send(k, q, self.axes[k], *self.geom[k]),
                              self.axes[k], *self.geom[k]) for k, q in self.q.items()}


def _all_reduce_small(v):
    n = v.shape[0]

    def body(v_ref, out_ref, buf, send, recv):
        x, y, c = _mesh_pos()
        my = 4 * x + 2 * y + c
        buf[my] = v_ref[...]
        cps = []
        for k in range(1, N_DEV):
            fx, fy, fc = (k >> 2) & 1, (k >> 1) & 1, k & 1
            peer = (1 - x if fx else x, 1 - y if fy else y, 1 - c if fc else c)
            cp = _remote(v_ref, buf.at[my], send.at[k - 1], recv.at[k - 1], peer)
            cp.start()
            cps.append((cp, 4 * peer[0] + 2 * peer[1] + peer[2]))
        for k, (cp, pid) in enumerate(cps):
            _remote(v_ref, buf.at[pid], send.at[k], recv.at[k], (x, y, c)).wait_recv()
        acc = buf[0]
        for i in range(1, N_DEV):
            acc = acc + buf[i]
        out_ref[...] = acc
        for cp, _ in cps:
            cp.wait_send()

    return _pcall(
        body, name="ar_small",
        in_specs=[pl.BlockSpec(memory_space=pltpu.VMEM)], out_specs=pl.BlockSpec(memory_space=pltpu.VMEM),
        out_shape=jax.ShapeDtypeStruct((n, LANES), F32),
        scratch_shapes=[pltpu.VMEM((N_DEV, n, LANES), F32), pltpu.SemaphoreType.DMA((N_DEV - 1,)),
                        pltpu.SemaphoreType.DMA((N_DEV - 1,))],
        compiler_params=pltpu.CompilerParams(vmem_limit_bytes=V7X_VMEM_LIMIT_BYTES),
    )(v)


def _adamw(name, w, g, m, v):
    R, Cc = w.shape
    tr = _tile(R, max(8, (1 << 19) // Cc // 8 * 8), 8)
    c1 = 1.0 - ADAM_B1 ** ADAM_STEP
    c2 = 1.0 - ADAM_B2 ** ADAM_STEP

    def body(w_ref, g_ref, m_ref, v_ref, d_ref, nm_ref, nv_ref):
        g_ = g_ref[...]
        nm = ADAM_B1 * m_ref[...] + (1.0 - ADAM_B1) * g_
        nv = ADAM_B2 * v_ref[...] + (1.0 - ADAM_B2) * (g_ * g_)
        d_ref[...] = -ADAM_LR * ((nm / c1) / (jnp.sqrt(nv / c2) + ADAM_EPS) + ADAM_WD * w_ref[...])
        nm_ref[...] = nm
        nv_ref[...] = nv

    spec = pl.BlockSpec((tr, Cc), lambda i: (i, 0))
    return _pcall(
        body, name=name, grid=(R // tr,),
        in_specs=[spec] * 4, out_specs=[spec] * 3,
        out_shape=[jax.ShapeDtypeStruct((R, Cc), F32)] * 3,
        compiler_params=_params("parallel"),
    )(w, g, m, v)


def _pack(arrs, rows):
    flat = jnp.concatenate([a.reshape(-1) for a in arrs])
    return jnp.pad(flat, (0, rows * LANES - flat.shape[0])).reshape(rows, LANES)


def _unpack(packed, like):
    flat, out, o = packed.reshape(-1), [], 0
    for a in like:
        out.append(flat[o:o + a.size].reshape(a.shape))
        o += a.size
    return out


BIG = (
    ("ffn1_w_gate", 1), ("ffn1_w_up", 1), ("ffn1_w_down", 0), ("w_in", 1), ("ssm_w_glu", 0), ("w_out", 0),
    ("ffn2_w_gate", 1), ("ffn2_w_up", 1), ("ffn2_w_down", 0), ("ple_w_gate", 0), ("ple_w_proj", 1),
)
SMALL = ("ffn1_norm", "mix_norm", "attn_out_norm", "ssm_lambda_re", "ssm_lambda_im", "ssm_log_dt", "ssm_b_re", "ssm_b_im",
         "ssm_c_re", "ssm_c_im", "ssm_d", "ssm_b_glu", "ssm_out_norm", "ffn2_norm", "ple_norm", "final_norm")
WEIGHTS = ("ffn1_norm", "ffn1_w_gate", "ffn1_w_up", "ffn1_w_down", "mix_norm", "w_in", "attn_out_norm", "ssm_lambda_re",
           "ssm_lambda_im", "ssm_log_dt", "ssm_b_re", "ssm_b_im", "ssm_c_re", "ssm_c_im", "ssm_d", "ssm_w_glu", "ssm_b_glu",
           "ssm_out_norm", "w_out", "ffn2_norm", "ffn2_w_gate", "ffn2_w_up", "ffn2_w_down", "ple_norm", "ple_w_gate",
           "ple_w_proj", "final_norm")


def _pad_to(a, axis, n):
    pad = [(0, 0), (0, 0)]
    pad[axis] = (0, n - a.shape[axis])
    return jnp.pad(a, pad)


def _local_step(x, p, tgt, w, ex):
    S, D = x.shape
    A = w["attn_out_norm"].shape[-1]
    W = w["ssm_d"].shape[-1]
    G, P = w["ssm_lambda_re"].shape[-2:]
    C = w["ssm_b_re"].shape[-1]
    GB = G // SSM_BLOCK_GROUPS
    T = min(1024, S)
    row = lambda name: w[name].reshape(1, -1)
    gs = {}

    h1, ffn1_saved = _ffn_fwd("ffn1", x, row("ffn1_norm"), ex)
    n2 = _rms_fwd("mix_norm", h1, row("mix_norm"))
    w_in = ex.weight("w_in")
    n2p = _to_attn_order(n2)
    host, done = _carried(ex, "ag", "w_in_qkv")
    qkv, *outs = _mm("w_in_qkv", [n2p], [w_in[:, :3 * A]], [F32], tm=1024, tn=1024, host=host)
    done(outs)
    (s_in,) = _mm("w_in_ssm", [n2], [w_in[:, 3 * A:]], [F32], tm=1024, tn=1024)
    ya, lse = _attn_fwd(qkv, ex)

    col = lambda name: w[name].reshape(G * P, 1)
    logdt_x = jnp.repeat(w["ssm_log_dt"].reshape(G), P).reshape(G * P, 1)
    b_re, b_im = w["ssm_b_re"].reshape(G * P, C), w["ssm_b_im"].reshape(G * P, C)
    lrdt, lidt, bbr, bbi = _ssm_disc(col("ssm_lambda_re"), col("ssm_lambda_im"), logdt_x, b_re, b_im)
    gsz = SSM_BLOCK_GROUPS
    to_bb = lambda t: _block_diag(t.reshape(GB, gsz, P, C).transpose(0, 1, 3, 2))
    bb = jnp.concatenate([to_bb(bbr), to_bb(bbi)], axis=2).astype(BF16)
    to_cc = lambda t: _block_diag(t.reshape(GB, gsz, C, P).transpose(0, 1, 3, 2))
    cc = jnp.concatenate([to_cc(w["ssm_c_re"]), -to_cc(w["ssm_c_im"])], axis=1).astype(BF16)
    lam_dt = jnp.stack([lrdt.reshape(GB, gsz * P), lidt.reshape(GB, gsz * P)], axis=1)
    ufp = _ssm_perm(s_in, T)
    ypre, hstart = _ssm_fwd(ufp, bb, cc, lam_dt, row("ssm_d"), T)

    def glu_in(ins, ps):
        yg = _gelu(ins[0])
        return [yg, yg], []

    yg, ygb = _rowwise("ssm_gelu", glu_in, [ypre], [], [(W, F32), (W, BF16)])
    w_glu = ex.weight("ssm_w_glu")

    def glu_out(accs, ex):
        gl = accs[0] + ex[1]
        return [ex[0] * _sigmoid(gl), gl]

    ybp, gl = _mm("ssm_glu", [ygb], [w_glu], [F32, F32], extras=[(yg, "mn"), (row("ssm_b_glu"), "n")],
                  epilogue=glu_out, tm=1024, tn=1024)
    yb = _ssm_unperm(ybp, T)
    na = _from_attn_order(_rms_fwd("attn_out_norm", ya, row("attn_out_norm")))
    nb = _rms_fwd("ssm_out_norm", yb, row("ssm_out_norm"))
    w_out = ex.weight("w_out")
    (h2,) = _mm("w_out", [na, nb], [w_out[:A], w_out[A:]], [F32], pairs=((0, 0, 0), (1, 1, 0)), extras=[(h1, "mn")],
                epilogue=lambda accs, ex: [ex[0] + accs[0]], tm=1024, tn=1024)
    h3, ffn2_saved = _ffn_fwd("ffn2", h2, row("ffn2_norm"), ex)
    n4 = _rms_fwd("ple_norm", h3, row("ple_norm"))
    (pe,) = _mm("ple_proj", [p], [ex.weight("ple_w_proj")], [F32], tm=1024, tn=1024)

    def ple_out(accs, ex):
        gate = _sigmoid(accs[0])
        return [ex[1] + gate * ex[0], gate]

    h4, gate = _mm("ple_gate", [n4], [ex.weight("ple_w_gate")], [F32, F32], extras=[(pe, "mn"), (h3, "mn")],
                   epilogue=ple_out, tm=1024, tn=1024)

    dh4, err2, gs["final_norm"] = _loss_head(h4, tgt, row("final_norm"))
    loss = (0.5 / D) * jnp.sum(err2)

    def ple_bwd(ins, ps):
        dh, gt, pe_ = ins
        return [dh * gt, dh * pe_ * gt * (1.0 - gt)], []

    dpe, dpg = _rowwise("ple_bwd", ple_bwd, [dh4, gate, pe], [], [(D, BF16), (D, BF16)])
    (d_ple_proj,) = _mm("ple_dproj", [p], [dpe], [BF16], ta=True, tm=256, tn=2048, tk=1024)
    (d_ple_gate,) = _mm("ple_dgate", [n4], [dpg], [BF16], ta=True, tm=1024, tn=1024, tk=2048)
    (dn4,) = _mm("ple_dn", [dpg], [ex.weight("ple_w_gate")], [F32], tb=True, tm=1024, tn=1024)
    (dh3, dh3b), gs["ple_norm"] = _rms_bwd("ple_dnorm", dn4, h3, row("ple_norm"), dres=dh4, copy_scale=0.5)
    (dh2, dh2b), gs["ffn2_norm"] = _ffn_bwd("ffn2", dh3, dh3b, h2, row("ffn2_norm"), ex, ffn2_saved, copy_scale=1.0)
    (dna,) = _mm("w_out_dna", [_to_attn_order(dh2b)], [w_out[:A]], [F32], tb=True, tm=1024, tn=1024)
    (dnb,) = _mm("w_out_dnb", [dh2b], [w_out[A:]], [F32], tb=True, tm=1024, tn=1024)
    (d_wout_a,) = _mm("w_out_dwa", [na], [dh2b], [BF16], ta=True, tm=1024, tn=1024, tk=2048)
    (d_wout_b,) = _mm("w_out_dwb", [nb], [dh2b], [BF16], ta=True, tm=1024, tn=1024, tk=2048)
    d_w_out = jnp.concatenate([d_wout_a, d_wout_b], axis=0)
    (dya,), gs["attn_out_norm"] = _rms_bwd("attn_out_dnorm", dna, ya, row("attn_out_norm"))
    (dyb,), gs["ssm_out_norm"] = _rms_bwd("ssm_out_dnorm", dnb, yb, row("ssm_out_norm"))

    dybp = _ssm_perm(dyb, T)

    def glu_bwd(ins, ps):
        dy, yg_, gl_ = ins
        sg = _sigmoid(gl_)
        dgl = dy * yg_ * sg * (1.0 - sg)
        return [dgl, dy * sg], [jnp.sum(dgl, axis=0, keepdims=True)]

    dgl, dyg_direct, gs["ssm_b_glu"] = _rowwise("ssm_glu_bwd", glu_bwd, [dybp, yg, gl], [], [(W, BF16), (W, F32)], accs=[W])
    (d_w_glu,) = _mm("ssm_dwglu", [ygb], [dgl], [BF16], ta=True, tm=1024, tn=1024, tk=2048)
    (dypre,) = _mm("ssm_dyg", [dgl], [w_glu], [F32], tb=True, extras=[(dyg_direct, "mn"), (ypre, "mn")],
                   epilogue=lambda accs, ex: [(accs[0] + ex[0]) * _gelu_grad(ex[1])], tm=1024, tn=1024)
    dufp, dbb, dcc, da, gs["ssm_d"] = _ssm_bwd(ufp, dypre, bb, bb.transpose(0, 2, 1), cc, cc.transpose(0, 2, 1),
                                               lam_dt, row("ssm_d"), hstart, T)
    ns = gsz * P
    from_bb = lambda t: _block_diag_take(t, gsz).transpose(0, 1, 3, 2).reshape(G * P, C)
    from_cc = lambda t: _block_diag_take(t, gsz).transpose(0, 1, 3, 2).reshape(w["ssm_c_re"].shape)
    gs["ssm_c_re"], gs["ssm_c_im"] = from_cc(dcc[:, :ns]), -from_cc(dcc[:, ns:])
    da = da.sum(axis=1)
    dar, dai = da[:, :ns].reshape(G * P, 1), da[:, ns:].reshape(G * P, 1)
    dlr, dli, dlogdt, dbr, dbi = _ssm_disc_bwd(col("ssm_lambda_re"), col("ssm_lambda_im"), logdt_x, b_re, b_im,
                                               dar, dai, from_bb(dbb[:, :, :ns]), from_bb(dbb[:, :, ns:]))
    gs["ssm_lambda_re"], gs["ssm_lambda_im"] = dlr.reshape(w["ssm_lambda_re"].shape), dli.reshape(w["ssm_lambda_im"].shape)
    gs["ssm_log_dt"] = dlogdt.reshape(G, P).sum(axis=1).reshape(w["ssm_log_dt"].shape)
    gs["ssm_b_re"], gs["ssm_b_im"] = dbr.reshape(w["ssm_b_re"].shape), dbi.reshape(w["ssm_b_im"].shape)
    ds_in = _ssm_unperm(dufp, T)

    dq, dk, dv = _attn_bwd(qkv, dya, ya, lse)
    dqkv = jnp.concatenate([dq, dk, dv], axis=1).astype(BF16)
    (d_w_qkv,) = _mm("w_in_dw_qkv", [n2p], [dqkv], [BF16], ta=True, tm=1024, tn=1024, tk=2048)
    (d_w_s,) = _mm("w_in_dw_ssm", [n2], [ds_in], [BF16], ta=True, tm=1024, tn=1024, tk=2048)
    d_w_in = jnp.concatenate([d_w_qkv, d_w_s], axis=1)
    dz = jnp.concatenate([_from_attn_order(dqkv), ds_in.astype(BF16)], axis=1)
    (dn2,) = _mm("w_in_dn", [dz], [w_in], [F32], tb=True, tm=1024, tn=1024)
    (dh1, dh1b), gs["mix_norm"] = _rms_bwd("mix_dnorm", dn2, h1, row("mix_norm"), dres=dh2, copy_scale=0.5)
    for k, g in (("ple_w_gate", d_ple_gate), ("ple_w_proj", d_ple_proj), ("w_out", d_w_out), ("ssm_w_glu", d_w_glu),
                 ("w_in", d_w_in)):
        ex.grad(k, g)
    (dx,), gs["ffn1_norm"] = _ffn_bwd("ffn1", dh1, dh1b, x, row("ffn1_norm"), ex, ffn1_saved, copy_scale=None)
    small = {k: gs[k].reshape(w[k].shape) for k in SMALL}
    return loss, dx, ex.finish(), small


def kernel(x, p, ffn1_norm, ffn1_w_gate, ffn1_w_up, ffn1_w_down, mix_norm, w_in, attn_out_norm, ssm_lambda_re, ssm_lambda_im, ssm_log_dt, ssm_b_re, ssm_b_im, ssm_c_re, ssm_c_im, ssm_d, ssm_w_glu, ssm_b_glu, ssm_out_norm, w_out, ffn2_norm, ffn2_w_gate, ffn2_w_up, ffn2_w_down, ple_norm, ple_w_gate, ple_w_proj, final_norm, loss_target, m_ffn1_norm, m_ffn1_w_gate, m_ffn1_w_up, m_ffn1_w_down, m_mix_norm, m_w_in, m_attn_out_norm, m_ssm_lambda_re, m_ssm_lambda_im, m_ssm_log_dt, m_ssm_b_re, m_ssm_b_im, m_ssm_c_re, m_ssm_c_im, m_ssm_d, m_ssm_w_glu, m_ssm_b_glu, m_ssm_out_norm, m_w_out, m_ffn2_norm, m_ffn2_w_gate, m_ffn2_w_up, m_ffn2_w_down, m_ple_norm, m_ple_w_gate, m_ple_w_proj, m_final_norm, v_ffn1_norm, v_ffn1_w_gate, v_ffn1_w_up, v_ffn1_w_down, v_mix_norm, v_w_in, v_attn_out_norm, v_ssm_lambda_re, v_ssm_lambda_im, v_ssm_log_dt, v_ssm_b_re, v_ssm_b_im, v_ssm_c_re, v_ssm_c_im, v_ssm_d, v_ssm_w_glu, v_ssm_b_glu, v_ssm_out_norm, v_w_out, v_ffn2_norm, v_ffn2_w_gate, v_ffn2_w_up, v_ffn2_w_down, v_ple_norm, v_ple_w_gate, v_ple_w_proj, v_final_norm):
    args = locals()
    w = {k: args[k] for k in WEIGHTS}
    m = {k: args["m_" + k] for k in WEIGHTS}
    v = {k: args["v_" + k] for k in WEIGHTS}
    w2 = {k: w[k].reshape(w[k].shape[-2:]) for k, _ in BIG}

    axes = [ax for _, ax in BIG]
    padded = {k: -(-w2[k].shape[ax] // LANES) * LANES for k, ax in BIG}
    shards = [_pad_to(w2[k].astype(BF16), ax, padded[k]) for k, ax in BIG]
    ex = _Exchange(dict(zip([k for k, _ in BIG], shards)), dict(BIG))
    loss_local, dx, summed, gsmall = _local_step(x[0], p[0, 0], loss_target[0], w, ex)
    loss = lax.psum(loss_local, MESH_AXES)
    n_small = sum(w[k].size for k in SMALL)
    rows = -(-n_small // (SUBLANES * LANES)) * SUBLANES
    gs_sum = _all_reduce_small(_pack([gsmall[k] for k in SMALL], rows))

    grads, delta, new_m, new_v = {}, {}, {}, {}
    for k, ax in BIG:
        gfull = summed[k]
        g2 = lax.slice_in_dim(gfull, 0, w2[k].shape[ax], axis=ax)
        d2, nm2, nv2 = _adamw("adamw_" + k, w2[k], g2, m[k].reshape(w2[k].shape), v[k].reshape(w2[k].shape))
        grads[k], delta[k], new_m[k], new_v[k] = (t.reshape(w[k].shape) for t in (g2, d2, nm2, nv2))
    small_like = [w[k] for k in SMALL]
    ds, nms, nvs = _adamw("adamw_small", _pack(small_like, rows), gs_sum, _pack([m[k] for k in SMALL], rows),
                          _pack([v[k] for k in SMALL], rows))
    for k, g_, d_, nm_, nv_ in zip(SMALL, _unpack(gs_sum, small_like), _unpack(ds, small_like),
                                   _unpack(nms, small_like), _unpack(nvs, small_like)):
        grads[k], delta[k], new_m[k], new_v[k] = g_, d_, nm_, nv_

    return (loss, dx[None], *[grads[k] for k in WEIGHTS], *[delta[k] for k in WEIGHTS],
            *[new_m[k] for k in WEIGHTS], *[new_v[k] for k in WEIGHTS])
```

```python
import functools
import math

import jax
import jax.numpy as jnp
from jax import lax
from jax.experimental import pallas as pl
from jax.experimental.pallas import tpu as pltpu

F32 = jnp.float32
BF16 = jnp.bfloat16
MESH = pl.DeviceIdType.MESH
MESH_AXES = ("x", "y", "c")
N_CHIPS = 4
N_DEV = 8

V7X_VMEM_LIMIT_BYTES = 56 << 20
LANES = 128
SUBLANES = 8

HEAD_DIM = 64
SWA_BLOCK = 128
DILATIONS = (1, 4, 16)
SSM_BLOCK_GROUPS = 8
NORM_EPS = 1e-6
MASK_VALUE = -1e30

ADAM_LR = 0.001
ADAM_B1 = 0.9
ADAM_B2 = 0.999
ADAM_EPS = 1e-08
ADAM_WD = 0.01
ADAM_STEP = 10

GELU_C = math.sqrt(2.0 / math.pi)
GELU_K = 0.044715


def _pcall(body, **kw):
    return pl.pallas_call(body, **kw)


def _params(*sem):
    return pltpu.CompilerParams(dimension_semantics=sem, vmem_limit_bytes=V7X_VMEM_LIMIT_BYTES)


def _tile(n, target, align):
    best = None
    for t in range(align, min(n, target) + 1, align):
        if n % t == 0:
            best = t
    return n if best is None else best


def _sigmoid(x):
    return 0.5 * jnp.tanh(0.5 * x) + 0.5


class _Host:
    def __init__(self, ins, out_shapes, n_sem, start, wait):
        self.ins, self.out_shapes, self.n_sem, self.start, self.wait = ins, out_shapes, n_sem, start, wait


def _mm(name, lhs, rhs, outs, pairs=((0, 0, 0),), epilogue=None, extras=(), ta=False, tb=False,
        tm=1024, tn=512, tk=2048, host=None):
    nl, nr, ne, no = len(lhs), len(rhs), len(extras), len(outs)
    nhi, nho = (len(host.ins), len(host.out_shapes)) if host else (0, 0)
    n_acc = 1 + max(p[2] for p in pairs)
    (K, M) = lhs[0].shape if ta else lhs[0].shape[::-1]
    (N, K2) = rhs[0].shape if tb else rhs[0].shape[::-1]
    assert K == K2, (name, lhs[0].shape, rhs[0].shape)
    tm, tn, tk = _tile(M, tm, LANES), _tile(N, tn, LANES), _tile(K, tk, LANES)
    ni, nj, nk = M // tm, N // tn, K // tk
    n_scr = n_acc if nk > 1 else 0
    if epilogue is None:
        epilogue = lambda accs, ex: accs
    dn = (((0 if ta else 1,), (1 if tb else 0,)), ((), ()))

    def body(*refs):
        refs = list(refs)
        take = lambda n: [refs.pop(0) for _ in range(n)]
        l, r, e, hin, o, hout, acc = take(nl), take(nr), take(ne), take(nhi), take(no), take(nho), take(n_scr)
        i, j, k = pl.program_id(0), pl.program_id(1), pl.program_id(2)
        if host:
            @pl.when((i == 0) & (j == 0) & (k == 0))
            def _():
                host.start(hin, hout, *refs)

        parts = [None] * n_acc
        for li, ri, ai in pairs:
            d = lax.dot_general(l[li][...].astype(BF16), r[ri][...].astype(BF16), dn,
                                preferred_element_type=F32)
            parts[ai] = d if parts[ai] is None else parts[ai] + d

        def finish(accs):
            res = epilogue(accs, [x[...] for x in e])
            for ref, val in zip(o, res):
                ref[...] = val.astype(ref.dtype)

        if nk == 1:
            finish(parts)
        else:
            @pl.when(k == 0)
            def _():
                for ai in range(n_acc):
                    acc[ai][...] = parts[ai]

            @pl.when(k > 0)
            def _():
                for ai in range(n_acc):
                    acc[ai][...] += parts[ai]

            @pl.when(k == nk - 1)
            def _():
                finish([a[...] for a in acc])

        if host:
            @pl.when((i == ni - 1) & (j == nj - 1) & (k == nk - 1))
            def _():
                host.wait(hin, hout, *refs)

    lspec = pl.BlockSpec((tk, tm), lambda i, j, k: (k, i)) if ta else pl.BlockSpec((tm, tk), lambda i, j, k: (i, k))
    rspec = pl.BlockSpec((tn, tk), lambda i, j, k: (j, k)) if tb else pl.BlockSpec((tk, tn), lambda i, j, k: (k, j))
    especs = []
    for arr, kind in extras:
        if kind == "mn":
            especs.append(pl.BlockSpec((tm, tn), lambda i, j, k: (i, j)))
        elif kind == "n":
            especs.append(pl.BlockSpec((1, tn), lambda i, j, k: (0, j)))
        else:
            especs.append(pl.BlockSpec((tm, 1), lambda i, j, k: (i, 0)))
    any_spec = pl.BlockSpec(memory_space=pl.ANY)
    sems = [pltpu.SemaphoreType.DMA((host.n_sem,)), pltpu.SemaphoreType.DMA((host.n_sem,))] if host else []
    res = _pcall(
        body, name=name,
        grid=(ni, nj, nk),
        in_specs=[lspec] * nl + [rspec] * nr + especs + [any_spec] * nhi,
        out_specs=[pl.BlockSpec((tm, tn), lambda i, j, k: (i, j))] * no + [any_spec] * nho,
        out_shape=[jax.ShapeDtypeStruct((M, N), dt) for dt in outs] + (list(host.out_shapes) if host else []),
        scratch_shapes=[pltpu.VMEM((tm, tn), F32)] * n_scr + sems,
        compiler_params=_params(*(("arbitrary",) * 3 if host else ("parallel", "parallel", "arbitrary"))),
    )(*lhs, *rhs, *[a for a, _ in extras], *(host.ins if host else []))
    return res


def _rowwise(name, fn, ins, params, outs, accs=(), ts=256):
    S = ins[0].shape[0]
    ts = _tile(S, ts, 16)
    ni, npar, no, na = len(ins), len(params), len(outs), len(accs)

    def body(*refs):
        i_refs, p_refs = refs[:ni], refs[ni:ni + npar]
        o_refs = refs[ni + npar:ni + npar + no]
        a_refs = refs[ni + npar + no:]
        res_o, res_a = fn([r[...] for r in i_refs], [r[...] for r in p_refs])
        for ref, val in zip(o_refs, res_o):
            ref[...] = val.astype(ref.dtype)
        if na:
            @pl.when(pl.program_id(0) == 0)
            def _():
                for ref in a_refs:
                    ref[...] = jnp.zeros(ref.shape, F32)

            for ref, val in zip(a_refs, res_a):
                ref[...] += val

    res = _pcall(
        body, name=name,
        grid=(S // ts,),
        in_specs=[pl.BlockSpec((ts, a.shape[1]), lambda i: (i, 0)) for a in ins]
        + [pl.BlockSpec(p.shape, lambda i: (0, 0)) for p in params],
        out_specs=[pl.BlockSpec((ts, w), lambda i: (i, 0)) for w, _ in outs]
        + [pl.BlockSpec((1, w), lambda i: (0, 0)) for w in accs],
        out_shape=[jax.ShapeDtypeStruct((S, w), dt) for w, dt in outs]
        + [jax.ShapeDtypeStruct((1, w), F32) for w in accs],
        compiler_params=_params("arbitrary"),
    )(*ins, *params)
    return res


def _xhat(x):
    r = lax.rsqrt(jnp.mean(x * x, axis=-1, keepdims=True) + NORM_EPS)
    return x * r, r


def _rms_fwd(name, x, g):
    def fn(ins, ps):
        xh, _ = _xhat(ins[0])
        return [xh * ps[0]], []

    return _rowwise(name, fn, [x], [g], [(x.shape[1], BF16)])[0]


def _rms_bwd(name, dn, x, g, dres=None, copy_scale=None):
    w = x.shape[1]

    def fn(ins, ps):
        dn_, x_ = ins[0], ins[1]
        xh, r = _xhat(x_)
        dxh = dn_ * ps[0]
        dx = r * (dxh - xh * jnp.mean(dxh * xh, axis=-1, keepdims=True))
        if dres is not None:
            dx = dx + ins[2]
        o = [dx] + ([dx * copy_scale] if copy_scale is not None else [])
        return o, [jnp.sum(dn_ * xh, axis=0, keepdims=True)]

    ins = [dn, x] + ([dres] if dres is not None else [])
    outs = [(w, F32)] + ([(w, BF16)] if copy_scale is not None else [])
    res = _rowwise(name, fn, ins, [g], outs, accs=[w])
    return res[:-1], res[-1]


def _swiglu_epilogue(accs, ex):
    g, u = accs
    sg = _sigmoid(g)
    s = g * sg
    return [u * (sg + s * (1.0 - sg)), s, s * u]


def _dswiglu_epilogue(accs, ex):
    da = accs[0]
    return [da * ex[0].astype(F32), da * ex[1].astype(F32)]


def _carried(ex, kind, kernel):
    if kernel not in (ex.AG_PLAN if kind == "ag" else ex.RS_PLAN):
        return None, lambda outs: None
    if kind == "ag":
        return ex.ag_host(kernel), lambda outs: ex.ag_done(kernel, outs)
    return ex.rs_host(kernel), lambda outs: ex.rs_done(kernel, outs)


def _ffn_fwd(tag, h, gnorm, ex):
    n = _rms_fwd(tag + "_norm", h, gnorm)
    host, done = _carried(ex, "ag", tag + "_up")
    g, u, a, *outs = _mm(tag + "_up", [n], [ex.weight(tag + "_w_gate"), ex.weight(tag + "_w_up")], [BF16, BF16, BF16],
                         pairs=((0, 0, 0), (0, 1, 1)), epilogue=_swiglu_epilogue, tm=1024, tn=512, host=host)
    done(outs)
    host, done = _carried(ex, "ag", tag + "_down")
    hout, *outs = _mm(tag + "_down", [a], [ex.weight(tag + "_w_down")], [F32], extras=[(h, "mn")],
                      epilogue=lambda accs, ex_: [ex_[0] + 0.5 * accs[0]], tm=512, tn=1024, tk=8192, host=host)
    done(outs)
    return hout, (n, g, u, a)


def _ffn_bwd(tag, dh, dhb_half, h, gnorm, ex, saved, copy_scale):
    n, g, u, a = saved
    wg, wu, wd = (ex.weight(tag + k) for k in ("_w_gate", "_w_up", "_w_down"))
    host, done = _carried(ex, "rs", tag + "_dact")
    dg, du, *outs = _mm(tag + "_dact", [dhb_half], [wd], [BF16, BF16], tb=True, extras=[(g, "mn"), (u, "mn")],
                        epilogue=_dswiglu_epilogue, tm=1024, tn=512, host=host)
    done(outs)
    host, done = _carried(ex, "rs", tag + "_dwd")
    dwd, *outs = _mm(tag + "_dwd", [a], [dhb_half], [BF16], ta=True, tm=512, tn=2048, tk=2048, host=host)
    done(outs)
    ex.grad(tag + "_w_down", dwd)
    host, done = _carried(ex, "rs", tag + "_dwgu")
    dwg, dwu, *outs = _mm(tag + "_dwgu", [n], [dg, du], [BF16, BF16], pairs=((0, 0, 0), (0, 1, 1)), ta=True,
                          tm=1024, tn=512, tk=2048, host=host)
    done(outs)
    ex.grad(tag + "_w_gate", dwg)
    ex.grad(tag + "_w_up", dwu)
    host, done = _carried(ex, "rs", tag + "_dn")
    dn, *outs = _mm(tag + "_dn", [dg, du], [wg, wu], [F32], pairs=((0, 0, 0), (1, 1, 0)), tb=True,
                    tm=1024, tn=1024, tk=1408, host=host)
    done(outs)
    return _rms_bwd(tag + "_dnorm", dn, h, gnorm, dres=dh, copy_scale=copy_scale)


ATTN_HEAD_PAIRS = 8


def _to_attn_order(a):
    S, w = a.shape
    return a.reshape(S // 16, 16, w).transpose(1, 0, 2).reshape(S, w)


def _from_attn_order(a):
    S, w = a.shape
    return a.reshape(16, S // 16, w).transpose(1, 0, 2).reshape(S, w)


def _attn_geom(S, d):
    s16 = S // 16
    if d == 16:
        return (16, s16), (1, SWA_BLOCK), (lambda r, b: (r, b)), 16, s16 // SWA_BLOCK
    if d == 4:
        return (4, 4, s16), (4, 1, SWA_BLOCK // 4), (lambda r, b: (0, r, b)), 4, s16 // (SWA_BLOCK // 4)
    return (16, s16), (16, SWA_BLOCK // 16), (lambda r, b: (0, b)), 1, s16 // (SWA_BLOCK // 16)


def _attn_pos(rho, d):
    if d == 16:
        return rho
    if d == 4:
        return 4 * (rho & 31) + (rho >> 5)
    return 16 * (rho & 7) + (rho >> 3)


def _attn_spec(S, d, lb, col, shift=0):
    _, blk, idx, _, nb = _attn_geom(S, d)
    return pl.BlockSpec(blk + (lb,), lambda r, cb, b: idx(r, jnp.clip(b + shift, 0, nb - 1)) + (col(cb),))


def _attn_view(a, d):
    return a.reshape(_attn_geom(a.shape[0], d)[0] + (a.shape[1],))


def _attn_valid(d):
    qp = _attn_pos(lax.broadcasted_iota(jnp.int32, (SWA_BLOCK, 2 * SWA_BLOCK), 0), d)
    kk = lax.broadcasted_iota(jnp.int32, (SWA_BLOCK, 2 * SWA_BLOCK), 1)
    kp = _attn_pos(kk & (SWA_BLOCK - 1), d)
    is_prev = kk < SWA_BLOCK
    return qp, kp, is_prev


def _head_masks(rows=SWA_BLOCK):
    lane = lax.broadcasted_iota(jnp.int32, (rows, LANES), 1)
    return [lane < HEAD_DIM, lane >= HEAD_DIM]


def _attn_ld(ref, sl):
    t = ref[(slice(None),) * (len(ref.shape) - 1) + (sl,)]
    return t.reshape(-1, t.shape[-1])


def _attn_st(ref, sl, val):
    ref[(slice(None),) * (len(ref.shape) - 1) + (sl,)] = val.reshape(ref.shape[:-1] + (val.shape[-1],))


def _per_head(t, first):
    sw = pltpu.roll(t, HEAD_DIM, 1)
    lo = lax.broadcasted_iota(jnp.int32, t.shape, 1) < HEAD_DIM
    return jnp.where(lo, t, sw) if first else jnp.where(lo, sw, t)


def _dot_nt(a, b):
    return lax.dot_general(a, b, (((1,), (1,)), ((), ())), preferred_element_type=F32)


def _dot_tn(a, b):
    return lax.dot_general(a, b, (((0,), (0,)), ((), ())), preferred_element_type=F32)


def _dot(a, b):
    return jnp.dot(a, b, preferred_element_type=F32)


def _keep(mask, t):
    return jnp.where(mask, t.astype(F32), 0.0).astype(BF16)


def _attn_cols(A):
    lb = min(A, LANES * ATTN_HEAD_PAIRS)
    ncol = A // lb
    return lb, ncol, [lambda cb, part=part: part * ncol + cb for part in range(3)], (lambda cb: cb)


def _attn_fwd_stage(name, qkv, d, prev, final, host=None):
    S, A3 = qkv.shape
    A = A3 // 3
    lb, ncol, (cq, ck, cv), ca = _attn_cols(A)
    view, _, _, nres, nb = _attn_geom(S, d)
    scale = HEAD_DIM ** -0.5
    has_prev = prev is not None
    n_out = 2 if final else 3
    nhi, nho = (len(host.ins), len(host.out_shapes)) if host else (0, 0)

    def body(*refs):
        q_ref, kp_ref, kc_ref, vp_ref, vc_ref = refs[:5]
        p_refs = refs[5:8] if has_prev else ()
        n_in = 5 + len(p_refs)
        hin, o_refs = refs[n_in:n_in + nhi], refs[n_in + nhi:n_in + nhi + n_out]
        hout, sems = refs[n_in + nhi + n_out:n_in + nhi + n_out + nho], refs[n_in + nhi + n_out + nho:]
        b = pl.program_id(2)
        if host:
            @pl.when((pl.program_id(0) == 0) & (pl.program_id(1) == 0) & (b == 0))
            def _():
                host.start(hin, hout, *sems)
        qp, kp_, is_prev = _attn_valid(d)
        valid = (is_prev & (kp_ >= qp) & (b > 0)) | (jnp.logical_not(is_prev) & (kp_ <= qp))
        hm, hm2 = _head_masks(), _head_masks(2 * SWA_BLOCK)
        for hp in range(lb // LANES):
            sl = slice(hp * LANES, (hp + 1) * LANES)
            q = _attn_ld(q_ref, sl)
            k2 = jnp.concatenate([_attn_ld(kp_ref, sl), _attn_ld(kc_ref, sl)], axis=0).astype(BF16)
            v2 = jnp.concatenate([_attn_ld(vp_ref, sl), _attn_ld(vc_ref, sl)], axis=0)
            o = jnp.zeros((SWA_BLOCK, LANES), F32)
            m = jnp.zeros((SWA_BLOCK, LANES), F32)
            l = jnp.zeros((SWA_BLOCK, LANES), F32)
            for hh in range(2):
                s = jnp.where(valid, _dot_nt(_keep(hm[hh], q), k2) * scale, MASK_VALUE)
                mh = jnp.max(s, axis=-1, keepdims=True)
                p = jnp.exp(s - mh)
                lh = jnp.sum(p, axis=-1, keepdims=True)
                o = o + _dot(p.astype(BF16), _keep(hm2[hh], v2))
                m = jnp.where(hm[hh], mh, m)
                l = jnp.where(hm[hh], lh, l)
            if has_prev:
                po, pm, pl_ = (_attn_ld(r, sl) for r in p_refs)
                mn = jnp.maximum(m, pm)
                w_new, w_old = jnp.exp(m - mn), jnp.exp(pm - mn)
                o = o * w_new + po * w_old
                l = l * w_new + pl_ * w_old
                m = mn
            if final:
                _attn_st(o_refs[0], sl, o / l)
                _attn_st(o_refs[1], sl, m + jnp.log(l))
            else:
                _attn_st(o_refs[0], sl, o)
                _attn_st(o_refs[1], sl, m)
                _attn_st(o_refs[2], sl, l)

        if host:
            @pl.when((pl.program_id(0) == nres - 1) & (pl.program_id(1) == ncol - 1) & (b == nb - 1))
            def _():
                host.wait(hin, hout, *sems)

    qk = _attn_view(qkv, d)
    prev_v = [_attn_view(t, d) for t in prev] if has_prev else []
    sp = functools.partial(_attn_spec, S, d, lb)
    any_spec = pl.BlockSpec(memory_space=pl.ANY)
    res = _pcall(
        body, name=name,
        grid=(nres, ncol, nb),
        in_specs=[sp(cq), sp(ck, -1), sp(ck), sp(cv, -1), sp(cv)] + [sp(ca)] * len(prev_v) + [any_spec] * nhi,
        out_specs=[sp(ca)] * n_out + [any_spec] * nho,
        out_shape=[jax.ShapeDtypeStruct(view + (A,), F32)] * n_out + (list(host.out_shapes) if host else []),
        scratch_shapes=[pltpu.SemaphoreType.DMA((host.n_sem,)), pltpu.SemaphoreType.DMA((host.n_sem,))] if host else [],
        compiler_params=_params(*(("arbitrary",) * 3 if host else ("parallel", "parallel", "arbitrary"))),
    )(qk, qk, qk, qk, qk, *prev_v, *(host.ins if host else []))
    return [t.reshape(S, A) for t in res[:n_out]], res[n_out:]


def _attn_fwd(qkv, ex):
    st = None
    for i, d in enumerate(DILATIONS):
        name = "attn_fwd_d%d" % d
        host, done = _carried(ex, "ag", name)
        st, outs = _attn_fwd_stage(name, qkv, d, st, final=(i == len(DILATIONS) - 1), host=host)
        done(outs)
    return st


def _attn_delta(dya, ya):
    S, A = ya.shape
    ri = lax.broadcasted_iota(jnp.int32, (A, A), 0) // HEAD_DIM
    ci = lax.broadcasted_iota(jnp.int32, (A, A), 1) // HEAD_DIM
    ones_bd = (ri == ci).astype(BF16)

    def fn(ins, ps):
        prod = ins[0] * ins[1]
        hi = prod.astype(BF16)
        lo = (prod - hi.astype(F32)).astype(BF16)
        return [_dot(hi, ps[0]) + _dot(lo, ps[0])], []

    return _rowwise("attn_delta", fn, [dya, ya], [ones_bd], [(A, F32)])[0]


def _attn_bwd_stage(name, qkv, do, lse, delta, d, prev):
    S, A3 = qkv.shape
    A = A3 // 3
    lb, ncol, (cq, ck, cv), ca = _attn_cols(A)
    view, _, _, nres, nb = _attn_geom(S, d)
    scale = HEAD_DIM ** -0.5
    has_prev = prev is not None
    lane_slices = [slice(hp * LANES, (hp + 1) * LANES) for hp in range(lb // LANES)]

    def body(*refs):
        q_ref, kp_ref, kc_ref, vp_ref, vc_ref, do_ref, lse_ref, dl_ref = refs[:8]
        p_refs = refs[8:11] if has_prev else ()
        dq_ref, dk_ref, dv_ref, dk_c, dv_c = refs[8 + len(p_refs):]
        b = pl.program_id(2)

        def put_keys(sl, dk, dv):
            if has_prev:
                dk, dv = dk + _attn_ld(p_refs[1], sl), dv + _attn_ld(p_refs[2], sl)
            _attn_st(dk_ref, sl, dk)
            _attn_st(dv_ref, sl, dv)

        @pl.when(b == 0)
        def _():
            dk_c[...] = jnp.zeros(dk_c.shape, F32)
            dv_c[...] = jnp.zeros(dv_c.shape, F32)

        @pl.when(b < nb)
        def _():
            qp, kp_, is_prev = _attn_valid(d)
            valid = (is_prev & (kp_ >= qp) & (b > 0)) | (jnp.logical_not(is_prev) & (kp_ <= qp))
            hm, hm2 = _head_masks(), _head_masks(2 * SWA_BLOCK)
            for sl in lane_slices:
                q, do_, lse_, dl_ = (_attn_ld(r, sl) for r in (q_ref, do_ref, lse_ref, dl_ref))
                k2 = jnp.concatenate([_attn_ld(kp_ref, sl), _attn_ld(kc_ref, sl)], axis=0)
                v2 = jnp.concatenate([_attn_ld(vp_ref, sl), _attn_ld(vc_ref, sl)], axis=0).astype(BF16)
                k2b = k2.astype(BF16)
                dq = jnp.zeros((SWA_BLOCK, LANES), F32)
                dk2 = jnp.zeros((2 * SWA_BLOCK, LANES), F32)
                dv2 = jnp.zeros((2 * SWA_BLOCK, LANES), F32)
                for hh in range(2):
                    qh, doh = _keep(hm[hh], q), _keep(hm[hh], do_)
                    lh, dh = _per_head(lse_, hh == 0), _per_head(dl_, hh == 0)
                    lh2, dh2 = jnp.concatenate([lh, lh], axis=1), jnp.concatenate([dh, dh], axis=1)
                    p = jnp.where(valid, jnp.exp(_dot_nt(qh, k2b) * scale - lh2), 0.0)
                    ds = (p * (_dot_nt(doh, v2) - dh2)).astype(BF16)
                    dq = dq + _dot(ds, _keep(hm2[hh], k2))
                    dk2 = dk2 + _dot_tn(ds, qh)
                    dv2 = dv2 + _dot_tn(p.astype(BF16), doh)
                dq, dk2 = dq * scale, dk2 * scale
                if has_prev:
                    dq = dq + _attn_ld(p_refs[0], sl)
                _attn_st(dq_ref, sl, dq)
                put_keys(sl, dk_c[:, sl] + dk2[:SWA_BLOCK], dv_c[:, sl] + dv2[:SWA_BLOCK])
                dk_c[:, sl] = dk2[SWA_BLOCK:]
                dv_c[:, sl] = dv2[SWA_BLOCK:]

        @pl.when(b == nb)
        def _():
            for sl in lane_slices:
                put_keys(sl, dk_c[:, sl], dv_c[:, sl])

    qk = _attn_view(qkv, d)
    acts = [_attn_view(t, d) for t in (do, lse, delta)] + ([_attn_view(t, d) for t in prev] if has_prev else [])
    sp = functools.partial(_attn_spec, S, d, lb)
    res = _pcall(
        body, name=name,
        grid=(nres, ncol, nb + 1),
        in_specs=[sp(cq), sp(ck, -1), sp(ck), sp(cv, -1), sp(cv), sp(ca), sp(ca), sp(ca)]
        + ([sp(ca), sp(ca, -1), sp(ca, -1)] if has_prev else []),
        out_specs=[sp(ca), sp(ca, -1), sp(ca, -1)],
        out_shape=[jax.ShapeDtypeStruct(view + (A,), F32)] * 3,
        scratch_shapes=[pltpu.VMEM((SWA_BLOCK, lb), F32)] * 2,
        compiler_params=_params("parallel", "parallel", "arbitrary"),
    )(qk, qk, qk, qk, qk, *acts)
    return [t.reshape(S, A) for t in res]


def _attn_bwd(qkv, dya, ya, lse):
    delta = _attn_delta(dya, ya)
    sums = None
    for d in DILATIONS:
        sums = _attn_bwd_stage("attn_bwd_d%d" % d, qkv, dya, lse, delta, d, sums)
    return sums


def _ssm_perm(a, T):
    S, w = a.shape
    return a.reshape(S // T, SUBLANES, T // SUBLANES, w).transpose(0, 2, 1, 3).reshape(S, w)


def _ssm_unperm(a, T):
    S, w = a.shape
    return a.reshape(S // T, T // SUBLANES, SUBLANES, w).transpose(0, 2, 1, 3).reshape(S, w)


def _ssm_powers(lam_ref, pw_ref, T, ns):
    n = (lax.broadcasted_iota(jnp.int32, (T, 1), 0) // SUBLANES + 1).astype(F32)
    mag = jnp.exp(n * lam_ref[0, 0:1, :])
    ang = n * lam_ref[0, 1:2, :]
    pw_ref[:, 0:ns] = mag * jnp.cos(ang)
    pw_ref[:, ns:2 * ns] = mag * jnp.sin(ang)


def _ssm_scan(xs, off, pw_ref, carry_ref, T, ns, reverse):
    Tc = T // SUBLANES
    sgn = -1.0 if reverse else 1.0
    ar, ai = pw_ref[0:SUBLANES, 0:ns], sgn * pw_ref[0:SUBLANES, ns:2 * ns]

    def rows(i):
        return pl.ds(pl.multiple_of(off + i * SUBLANES, SUBLANES), SUBLANES)

    def step(k, h):
        hr, hi = h
        r = rows(Tc - 1 - k if reverse else k)
        nr = ar * hr - ai * hi + xs[r, 0:ns]
        ni = ar * hi + ai * hr + xs[r, ns:2 * ns]
        xs[r, 0:ns] = nr
        xs[r, ns:2 * ns] = ni
        return nr, ni

    z = jnp.zeros((SUBLANES, ns), F32)
    er, ei = lax.fori_loop(0, Tc, step, (z, z), unroll=4)
    atr, ati = pw_ref[T - SUBLANES:T, 0:ns], sgn * pw_ref[T - SUBLANES:T, ns:2 * ns]
    rowid = lax.broadcasted_iota(jnp.int32, (SUBLANES, ns), 0)
    cr, ci = carry_ref[:, 0:ns], carry_ref[:, ns:2 * ns]
    ctr, cti = z, z
    for jj in range(SUBLANES):
        j = SUBLANES - 1 - jj if reverse else jj
        sel = rowid == j
        ctr, cti = jnp.where(sel, cr, ctr), jnp.where(sel, ci, cti)
        ejr = jnp.broadcast_to(jnp.sum(jnp.where(sel, er, 0.0), axis=0, keepdims=True), (SUBLANES, ns))
        eji = jnp.broadcast_to(jnp.sum(jnp.where(sel, ei, 0.0), axis=0, keepdims=True), (SUBLANES, ns))
        cr, ci = ejr + atr * cr - ati * ci, eji + atr * ci + ati * cr
    carry_ref[:, 0:ns] = cr
    carry_ref[:, ns:2 * ns] = ci

    def fix(i, _):
        r = rows(i)
        pr_rows = pl.ds(pl.multiple_of((Tc - 1 - i if reverse else i) * SUBLANES, SUBLANES), SUBLANES)
        pr, pi = pw_ref[pr_rows, 0:ns], sgn * pw_ref[pr_rows, ns:2 * ns]
        xs[r, 0:ns] += pr * ctr - pi * cti
        xs[r, ns:2 * ns] += pr * cti + pi * ctr
        return 0

    lax.fori_loop(0, Tc, fix, 0, unroll=4)
    return ctr, cti


def _ssm_fwd(ufp, bb, cc, lam_dt, drow, T):
    S, W = ufp.shape
    GB, cw, ns2 = bb.shape
    ns = ns2 // 2
    NCH = S // T

    def body(uf_ref, bb_ref, cc_ref, lam_ref, d_ref, y_ref, hs_ref, xs, pw, carry):
        @pl.when(pl.program_id(1) == 0)
        def _():
            _ssm_powers(lam_ref, pw, T, ns)
            carry[...] = jnp.zeros(carry.shape, F32)

        uf = uf_ref[...]
        xs[...] = _dot(uf.astype(BF16), bb_ref[0])
        hs_ref[0, 0] = carry[...]
        _ssm_scan(xs, 0, pw, carry, T, ns, reverse=False)
        y_ref[...] = _dot(xs[...].astype(BF16), cc_ref[0]) + d_ref[...] * uf

    return _pcall(
        body, name="ssm_fwd",
        grid=(GB, NCH),
        in_specs=[pl.BlockSpec((T, cw), lambda g, c: (c, g)),
                  pl.BlockSpec((1, cw, ns2), lambda g, c: (g, 0, 0)),
                  pl.BlockSpec((1, ns2, cw), lambda g, c: (g, 0, 0)),
                  pl.BlockSpec((1, 2, ns), lambda g, c: (g, 0, 0)),
                  pl.BlockSpec((1, cw), lambda g, c: (0, g))],
        out_specs=[pl.BlockSpec((T, cw), lambda g, c: (c, g)),
                   pl.BlockSpec((1, 1, SUBLANES, ns2), lambda g, c: (g, c, 0, 0))],
        out_shape=[jax.ShapeDtypeStruct((S, W), F32),
                   jax.ShapeDtypeStruct((GB, NCH, SUBLANES, ns2), F32)],
        scratch_shapes=[pltpu.VMEM((T, ns2), F32), pltpu.VMEM((T, ns2), F32), pltpu.VMEM((SUBLANES, ns2), F32)],
        compiler_params=_params("arbitrary", "arbitrary"),
    )(ufp, bb, cc, lam_dt, drow)


def _ssm_bwd(ufp, dyp, bb, bbt, cc, cct, lam_dt, drow, hstart, T):
    S, W = ufp.shape
    GB, cw, ns2 = bb.shape
    ns = ns2 // 2
    NCH = S // T

    def body(uf_ref, dy_ref, bb_ref, bbt_ref, cc_ref, cct_ref, lam_ref, d_ref, hs_ref,
             duf_ref, dbb_ref, dcc_ref, da_ref, dd_ref, hb, ls, pw, carry_f, carry_b):
        @pl.when(pl.program_id(1) == 0)
        def _():
            _ssm_powers(lam_ref, pw, T, ns)
            carry_b[...] = jnp.zeros(carry_b.shape, F32)
            dbb_ref[...] = jnp.zeros(dbb_ref.shape, F32)
            dcc_ref[...] = jnp.zeros(dcc_ref.shape, F32)
            da_ref[...] = jnp.zeros(da_ref.shape, F32)
            dd_ref[...] = jnp.zeros(dd_ref.shape, F32)

        uf, dy = uf_ref[...], dy_ref[...]
        ufb, dyb = uf.astype(BF16), dy.astype(BF16)
        hb[SUBLANES:T + SUBLANES, :] = _dot(ufb, bb_ref[0])
        carry_f[...] = hs_ref[0, 0]
        ctr, cti = _ssm_scan(hb, SUBLANES, pw, carry_f, T, ns, reverse=False)
        hb[0:SUBLANES, 0:ns] = ctr
        hb[0:SUBLANES, ns:ns2] = cti
        ls[...] = _dot(dyb, cct_ref[0])
        _ssm_scan(ls, 0, pw, carry_b, T, ns, reverse=True)
        lv = ls[...]
        lb = lv.astype(BF16)
        dbb_ref[0] += _dot_tn(ufb, lb)
        dcc_ref[0] += _dot_tn(hb[SUBLANES:T + SUBLANES, :].astype(BF16), dyb)
        lr, li = lv[:, 0:ns], lv[:, ns:ns2]
        hpr, hpi = hb[0:T, 0:ns], hb[0:T, ns:ns2]
        dar = jnp.sum(lr * hpr + li * hpi, axis=0, keepdims=True)
        dai = jnp.sum(li * hpr - lr * hpi, axis=0, keepdims=True)
        da_ref[0, 0:1, 0:ns] += dar
        da_ref[0, 0:1, ns:ns2] += dai
        duf_ref[...] = _dot(lb, bbt_ref[0]) + d_ref[...] * dy
        dd_ref[...] += jnp.sum(dy * uf, axis=0, keepdims=True)

    rc = lambda c: NCH - 1 - c
    return _pcall(
        body, name="ssm_bwd",
        grid=(GB, NCH),
        in_specs=[pl.BlockSpec((T, cw), lambda g, c: (rc(c), g)),
                  pl.BlockSpec((T, cw), lambda g, c: (rc(c), g)),
                  pl.BlockSpec((1, cw, ns2), lambda g, c: (g, 0, 0)),
                  pl.BlockSpec((1, ns2, cw), lambda g, c: (g, 0, 0)),
                  pl.BlockSpec((1, ns2, cw), lambda g, c: (g, 0, 0)),
                  pl.BlockSpec((1, cw, ns2), lambda g, c: (g, 0, 0)),
                  pl.BlockSpec((1, 2, ns), lambda g, c: (g, 0, 0)),
                  pl.BlockSpec((1, cw), lambda g, c: (0, g)),
                  pl.BlockSpec((1, 1, SUBLANES, ns2), lambda g, c: (g, rc(c), 0, 0))],
        out_specs=[pl.BlockSpec((T, cw), lambda g, c: (rc(c), g)),
                   pl.BlockSpec((1, cw, ns2), lambda g, c: (g, 0, 0)),
                   pl.BlockSpec((1, ns2, cw), lambda g, c: (g, 0, 0)),
                   pl.BlockSpec((1, SUBLANES, ns2), lambda g, c: (g, 0, 0)),
                   pl.BlockSpec((1, cw), lambda g, c: (0, g))],
        out_shape=[jax.ShapeDtypeStruct((S, W), F32),
                   jax.ShapeDtypeStruct((GB, cw, ns2), F32),
                   jax.ShapeDtypeStruct((GB, ns2, cw), F32),
                   jax.ShapeDtypeStruct((GB, SUBLANES, ns2), F32),
                   jax.ShapeDtypeStruct((1, W), F32)],
        scratch_shapes=[pltpu.VMEM((T + SUBLANES, ns2), F32), pltpu.VMEM((T, ns2), F32), pltpu.VMEM((T, ns2), F32),
                        pltpu.VMEM((SUBLANES, ns2), F32), pltpu.VMEM((SUBLANES, ns2), F32)],
        compiler_params=_params("arbitrary", "arbitrary"),
    )(ufp, dyp, bb, bbt, cc, cct, lam_dt, drow, hstart)


def _ssm_disc_math(lr, li, logdt, br, bi):
    dt = jnp.exp(logdt)
    mag = jnp.exp(lr * dt)
    ar = mag * jnp.cos(li * dt)
    ai = mag * jnp.sin(li * dt)
    nr, ni = ar - 1.0, ai
    den = lr * lr + li * li
    cr = (nr * lr + ni * li) / den
    ci = (ni * lr - nr * li) / den
    return ar, ai, cr * br - ci * bi, cr * bi + ci * br


def _ssm_disc(lr, li, logdt, br, bi):
    C = br.shape[1]

    def fn(ins, ps):
        _, _, bbr, bbi = _ssm_disc_math(*ins)
        dt = jnp.exp(ins[2])
        return [ins[0] * dt, ins[1] * dt, bbr, bbi], []

    return _rowwise("ssm_disc", fn, [lr, li, logdt, br, bi], [], [(1, F32), (1, F32), (C, F32), (C, F32)], ts=512)


def _ssm_disc_bwd(lr, li, logdt, br, bi, dar, dai, dbbr, dbbi):
    C = br.shape[1]

    def fn(ins, ps):
        _, vjp = jax.vjp(_ssm_disc_math, *ins[:5])
        return list(vjp(tuple(ins[5:]))), []

    return _rowwise("ssm_disc_bwd", fn, [lr, li, logdt, br, bi, dar, dai, dbbr, dbbi], [],
                    [(1, F32), (1, F32), (1, F32), (C, F32), (C, F32)], ts=512)


def _block_diag(t):
    GB, g, a, b = t.shape
    eye = jnp.eye(g, dtype=t.dtype)
    return (t[:, :, :, None, :] * eye[None, :, None, :, None]).reshape(GB, g * a, g * b)


def _block_diag_take(t, g):
    GB, ga, gb_ = t.shape
    a, b = ga // g, gb_ // g
    eye = jnp.eye(g, dtype=t.dtype)
    return (t.reshape(GB, g, a, g, b) * eye[None, :, None, :, None]).sum(axis=3)


def _loss_head(h4, tgt, gf):
    D = h4.shape[1]

    def fn(ins, ps):
        x, t = ins
        xh, r = _xhat(x)
        err = xh * ps[0] - t
        dn = err * (1.0 / D)
        dxh = dn * ps[0]
        dx = r * (dxh - xh * jnp.mean(dxh * xh, axis=-1, keepdims=True))
        return [dx], [jnp.sum(err * err, axis=0, keepdims=True), jnp.sum(dn * xh, axis=0, keepdims=True)]

    return _rowwise("loss_head", fn, [h4, tgt], [gf], [(D, F32)], accs=[D, D])


def _gelu(x):
    return 0.5 * x * (1.0 + jnp.tanh(GELU_C * (x + GELU_K * x * x * x)))


def _gelu_grad(x):
    t = jnp.tanh(GELU_C * (x + GELU_K * x * x * x))
    return 0.5 * (1.0 + t) + 0.5 * x * (1.0 - t * t) * GELU_C * (1.0 + 3.0 * GELU_K * x * x)


def _mesh_pos():
    return lax.axis_index("x"), lax.axis_index("y"), lax.axis_index("c")


def _other_chips(x, y):
    return [(1 - x, y), (x, 1 - y), (1 - x, 1 - y)]


def _remote(src, dst, send, recv, dev):
    return pltpu.make_async_remote_copy(src_ref=src, dst_ref=dst, send_sem=send, recv_sem=recv,
                                        device_id=dev, device_id_type=MESH)


ANY = pl.BlockSpec(memory_space=pl.ANY)


COMM_BLOCK_BYTES = 3 << 19


def _place():
    x, y, c = _mesh_pos()
    return jnp.stack([c] + [2 * cx + cy for cx, cy in _other_chips(x, y)] + [2 * x + y]).astype(jnp.int32)


def _send_chips(name, srcs, specs, tr, nth, cw):
    hr = nth * tr
    n = len(srcs)

    def body(*refs):
        got_ref, send, recv = refs[1 + n:]
        t = pl.program_id(0)
        x, y, c = _mesh_pos()
        cps = []
        for j, chip in enumerate(_other_chips(x, y)):
            dst = got_ref.at[pl.ds(pl.multiple_of(j * hr + t * tr, 16), tr), :]
            cp = _remote(refs[1 + j % n], dst, send.at[j], recv.at[j], (*chip, c))
            cp.start()
            cps.append(cp)
        for cp in cps:
            cp.wait_send()

        @pl.when(t == nth - 1)
        def _():
            for j in range(3):
                r_ = got_ref.at[pl.ds(j * hr, hr), :]
                _remote(r_, r_, send.at[j], recv.at[j], (x, y, c)).wait_recv()

    return _pcall(
        body, name=name,
        grid_spec=pltpu.PrefetchScalarGridSpec(
            num_scalar_prefetch=1, grid=(nth,), in_specs=specs, out_specs=ANY,
            scratch_shapes=[pltpu.SemaphoreType.DMA((3,)), pltpu.SemaphoreType.DMA((3,))]),
        out_shape=jax.ShapeDtypeStruct((3 * hr, cw), srcs[0].dtype),
        compiler_params=_params("arbitrary"),
    )(_place(), *srcs)


def _ag_assemble(name, shard, stage, axis, tr, nth):
    R, cc = shard.shape
    hr = nth * tr
    full = (R, N_CHIPS * cc) if axis == 1 else (N_CHIPS * R, cc)

    def body(pl_ref, s0, s1, s2, h0, h1, out_ref, send, recv, lsem):
        t = pl.program_id(0)
        x, y, c = _mesh_pos()

        def region(s, half):
            if axis == 1:
                return out_ref.at[pl.ds(pl.multiple_of(half * hr + t * tr, 16), tr), pl.ds(pl.multiple_of(s * cc, LANES), cc)]
            return out_ref.at[pl.ds(pl.multiple_of(s * R + half * hr + t * tr, 16), tr), :]

        cps = []
        for j, src in enumerate((s0, s1, s2)):
            dst = region(pl_ref[1 + j], c)
            cps.append(_remote(src, dst, send.at[j], recv, (x, y, 1 - c)))
            cps.append(pltpu.make_async_copy(src, dst, lsem.at[j]))
        for half, src in enumerate((h0, h1)):
            cps.append(pltpu.make_async_copy(src, region(pl_ref[4], half), lsem.at[3 + half]))
        for cp in cps:
            cp.start()
        for k, cp in enumerate(cps):
            if k < 6 and k % 2 == 0:
                cp.wait_send()
            else:
                cp.wait()

        @pl.when(t == nth - 1)
        def _():
            r_ = out_ref.at[pl.ds(0, hr), pl.ds(0, 3 * cc)] if axis == 1 else out_ref.at[pl.ds(0, 3 * hr), :]
            _remote(r_, r_, send.at[0], recv, (x, y, c)).wait_recv()

    blk = lambda f: pl.BlockSpec((tr, cc), f)
    return _pcall(
        body, name=name,
        grid_spec=pltpu.PrefetchScalarGridSpec(
            num_scalar_prefetch=1, grid=(nth,),
            in_specs=[blk(lambda t, p, j=j: (j * nth + t, 0)) for j in range(3)]
            + [blk(lambda t, p, h=h: (h * nth + t, 0)) for h in range(2)],
            out_specs=ANY,
            scratch_shapes=[pltpu.SemaphoreType.DMA((3,)), pltpu.SemaphoreType.DMA, pltpu.SemaphoreType.DMA((5,))]),
        out_shape=jax.ShapeDtypeStruct(full, shard.dtype),
        compiler_params=_params("arbitrary"),
    )(_place(), stage, stage, stage, shard, shard)


def _comm_rows(hr, row_bytes):
    return _tile(hr, max(16, COMM_BLOCK_BYTES // row_bytes // 16 * 16), 16)


def _host_send(items):
    def copies(ins, outs, send, recv):
        x, y, c = _mesh_pos()
        cps = []
        for w, (_, kind, hr, cw) in enumerate(items):
            for j, (cx, cy) in enumerate(_other_chips(x, y)):
                s = 2 * cx + cy
                if kind == "half":
                    src = ins[w].at[pl.ds(pl.multiple_of(c * hr, 16), hr), :]
                elif kind == "cols":
                    src = ins[w].at[:, pl.ds(pl.multiple_of(s * cw, LANES), cw)]
                else:
                    src = ins[w].at[pl.ds(pl.multiple_of(s * hr, 16), hr), :]
                cps.append(_remote(src, outs[w].at[pl.ds(j * hr, hr), :], send.at[3 * w + j], recv.at[3 * w + j], (cx, cy, c)))
        return cps

    def start(ins, outs, send, recv):
        for cp in copies(ins, outs, send, recv):
            cp.start()

    def wait(ins, outs, send, recv):
        for cp in copies(ins, outs, send, recv):
            cp.wait()

    return _Host([a for a, _, _, _ in items], [jax.ShapeDtypeStruct((3 * hr, cw), a.dtype) for a, _, hr, cw in items],
                 3 * len(items), start, wait)


def _ag_send(name, sh):
    R, cc = sh.shape
    tr = _comm_rows(R // 2, cc * 2)
    nth = R // 2 // tr
    return _send_chips(name, [sh], [pl.BlockSpec((tr, cc), lambda t, p: (p[0] * nth + t, 0))], tr, nth, cc)


def _ag_finish(name, sh, stage, axis):
    R, cc = sh.shape
    tr = _comm_rows(R // 2, cc * 2)
    return _ag_assemble(name, sh, stage, axis, tr, R // 2 // tr)


def _push_pair(name, src, tr, nblk, src_block):
    cw = src.shape[1]
    c_arr = lax.axis_index("c").astype(jnp.int32).reshape(1)

    def body(c_ref, src_ref, out_ref, send, recv):
        i = pl.program_id(0)
        x, y, c = _mesh_pos()
        cp = _remote(src_ref, out_ref.at[pl.ds(pl.multiple_of(i * tr, 16), tr), :], send, recv, (x, y, 1 - c))
        cp.start()
        cp.wait_send()

        @pl.when(i == nblk - 1)
        def _():
            _remote(out_ref, out_ref, send, recv, (x, y, c)).wait_recv()

    return _pcall(
        body, name=name,
        grid_spec=pltpu.PrefetchScalarGridSpec(
            num_scalar_prefetch=1, grid=(nblk,),
            in_specs=[pl.BlockSpec((tr, cw), lambda i, c_ref: (src_block(i, c_ref[0]), 0))],
            out_specs=ANY,
            scratch_shapes=[pltpu.SemaphoreType.DMA, pltpu.SemaphoreType.DMA]),
        out_shape=jax.ShapeDtypeStruct((nblk * tr, cw), src.dtype),
        compiler_params=_params("arbitrary"),
    )(c_arr, src)


def _sum_half(name, g, theirs, tr, nblk, src_block):
    cw = g.shape[1]
    c_arr = lax.axis_index("c").astype(jnp.int32).reshape(1)

    def body(c_ref, g_ref, t_ref, o_ref):
        o_ref[...] = (g_ref[...].astype(F32) + t_ref[...].astype(F32)).astype(BF16)

    return _pcall(
        body, name=name,
        grid_spec=pltpu.PrefetchScalarGridSpec(
            num_scalar_prefetch=1, grid=(nblk,),
            in_specs=[pl.BlockSpec((tr, cw), lambda i, c_ref: (src_block(i, c_ref[0]), 0)),
                      pl.BlockSpec((tr, cw), lambda i, c_ref: (i, 0))],
            out_specs=pl.BlockSpec((tr, cw), lambda i, c_ref: (i, 0))),
        out_shape=jax.ShapeDtypeStruct((nblk * tr, cw), BF16),
        compiler_params=_params("arbitrary"),
    )(c_arr, g, theirs)


def _sum_chips_swap(name, q, got, qspec, tr, nth, cw):
    def body(p_ref, q_ref, g0, g1, g2, out_ref, buf, send, recv, lsem):
        t = pl.program_id(0)
        x, y, c = _mesh_pos()
        buf[...] = q_ref[...].astype(F32) + g0[...].astype(F32) + g1[...].astype(F32) + g2[...].astype(F32)
        dst = out_ref.at[pl.ds(pl.multiple_of((p_ref[0] * nth + t) * tr, 16), tr), :]
        cp = _remote(buf, dst, send, recv, (x, y, 1 - c))
        lc = pltpu.make_async_copy(buf, dst, lsem)
        cp.start()
        lc.start()
        lc.wait()
        cp.wait_send()

        @pl.when(t == nth - 1)
        def _():
            theirs = out_ref.at[pl.ds(0, nth * tr), :]
            _remote(theirs, theirs, send, recv, (x, y, c)).wait_recv()

    return _pcall(
        body, name=name,
        grid_spec=pltpu.PrefetchScalarGridSpec(
            num_scalar_prefetch=1, grid=(nth,),
            in_specs=[qspec] + [pl.BlockSpec((tr, cw), lambda t, p, j=j: (j * nth + t, 0)) for j in range(3)],
            out_specs=ANY,
            scratch_shapes=[pltpu.VMEM((tr, cw), F32), pltpu.SemaphoreType.DMA, pltpu.SemaphoreType.DMA,
                            pltpu.SemaphoreType.DMA]),
        out_shape=jax.ShapeDtypeStruct((2 * nth * tr, cw), F32),
        compiler_params=_params("arbitrary"),
    )(_place(), q, got, got, got)


def _rs_geom(g, axis):
    rows, gw = g.shape
    return (rows // 2, gw // N_CHIPS) if axis == 1 else (rows // N_CHIPS // 2, gw)


def _rs_pair_sum(tag, g, axis):
    hr, _ = _rs_geom(g, axis)
    tr = _comm_rows(hr, g.shape[1] * 2)
    nth = hr // tr
    if axis == 1:
        nblk, blk = nth, (lambda i, half: half * nth + i)
    else:
        nblk, blk = N_CHIPS * nth, (lambda i, half: (i // nth) * (2 * nth) + half * nth + i % nth)
    theirs = _push_pair("rs_pair_" + tag, g, tr, nblk, lambda i, c: blk(i, 1 - c))
    return _sum_half("rs_sum_pair_" + tag, g, theirs, tr, nblk, blk)


def _rs_part(q, axis, hr, cw, tr):
    nth = hr // tr
    if axis == 1:
        return lambda k: pl.BlockSpec((tr, cw), lambda t, p: (t, p[k]))
    return lambda k: pl.BlockSpec((tr, cw), lambda t, p: (p[k] * nth + t, 0))


def _rs_send(tag, q, axis, hr, cw):
    tr = _comm_rows(hr, cw * 2)
    part = _rs_part(q, axis, hr, cw, tr)
    return _send_chips("rs_send_" + tag, [q, q, q], [part(1), part(2), part(3)], tr, hr // tr, cw)


def _rs_finish(tag, q, got, axis, hr, cw):
    tr = _comm_rows(hr, cw * 4)
    return _sum_chips_swap("rs_sum_swap_" + tag, q, got, _rs_part(q, axis, hr, cw, tr)(4), tr, hr // tr, cw)


class _Exchange:
    AG_PLAN = {"ffn1_up": ("ffn1_w_down", "ffn2_w_gate"), "ffn1_down": ("ffn2_w_up", "w_in"),
               "w_in_qkv": ("ssm_w_glu", "w_out", "ple_w_gate", "ple_w_proj"), "attn_fwd_d1": ("ffn2_w_down",)}
    RS_PLAN = {"ffn1_dact": ("ffn2_w_gate", "ple_w_gate", "ple_w_proj"), "ffn1_dwd": ("ffn2_w_up",),
               "ffn1_dwgu": ("ffn2_w_down", "w_in", "ssm_w_glu", "w_out", "ffn1_w_down"),
               "ffn1_dn": ("ffn1_w_gate", "ffn1_w_up")}

    def __init__(self, shards, axes):
        self.shards, self.axes = shards, axes
        self.stage, self.full, self.q, self.geom, self.got = {}, {}, {}, {}, {}

    def ag_host(self, kernel):
        item = lambda k: (self.shards[k], "half", self.shards[k].shape[0] // 2, self.shards[k].shape[1])
        return _host_send([item(k) for k in self.AG_PLAN[kernel]])

    def ag_done(self, kernel, stages):
        self.stage.update(zip(self.AG_PLAN[kernel], stages))

    def weight(self, k):
        if k not in self.full:
            stage = self.stage[k] if k in self.stage else _ag_send("ag_send_" + k, self.shards[k])
            self.full[k] = _ag_finish("ag_asm_" + k, self.shards[k], stage, self.axes[k])
        return self.full[k]

    def grad(self, k, g):
        self.q[k], self.geom[k] = _rs_pair_sum(k, g, self.axes[k]), _rs_geom(g, self.axes[k])

    def rs_host(self, kernel):
        item = lambda k: (self.q[k], "cols" if self.axes[k] == 1 else "rows") + self.geom[k]
        return _host_send([item(k) for k in self.RS_PLAN[kernel]])

    def rs_done(self, kernel, gots):
        self.got.update(zip(self.RS_PLAN[kernel], gots))

    def finish(self):
        return {k: _rs_finish(k, q, self.got[k] if k in self.got else _rs_send(k, q, self.axes[k], *self.geom[k]),
                              self.axes[k], *self.geom[k]) for k, q in self.q.items()}


def _all_reduce_small(v):
    n = v.shape[0]
    h = n // 2

    def body(v_ref, out_ref, pair_in, chips_in, send, recv):
        x, y, c = _mesh_pos()
        me, sib, my_chip = (x, y, c), (x, y, 1 - c), 2 * x + y
        mine = pl.ds(pl.multiple_of(c * h, SUBLANES), h)
        other = pl.ds(pl.multiple_of((1 - c) * h, SUBLANES), h)
        pair = _remote(v_ref.at[other], pair_in, send.at[0], recv.at[0], sib)
        pair.start()
        pair.wait()
        chips_in[my_chip] = v_ref[mine, :] + pair_in[...]
        cps = []
        for j, (cx, cy) in enumerate(_other_chips(x, y)):
            cp = _remote(chips_in.at[my_chip], chips_in.at[my_chip], send.at[1 + j], recv.at[1 + j], (cx, cy, c))
            cp.start()
            cps.append(cp)
        for j, (cx, cy) in enumerate(_other_chips(x, y)):
            slot = chips_in.at[2 * cx + cy]
            _remote(slot, slot, send.at[1 + j], recv.at[1 + j], me).wait_recv()
        out_ref[mine, :] = (chips_in[0] + chips_in[1]) + (chips_in[2] + chips_in[3])
        for cp in cps:
            cp.wait_send()
        swap = _remote(out_ref.at[mine, :], out_ref.at[mine, :], send.at[4], recv.at[4], sib)
        swap.start()
        _remote(out_ref.at[other, :], out_ref.at[other, :], send.at[4], recv.at[4], me).wait_recv()
        swap.wait_send()

    return _pcall(
        body, name="ar_small",
        in_specs=[pl.BlockSpec(memory_space=pltpu.VMEM)], out_specs=pl.BlockSpec(memory_space=pltpu.VMEM),
        out_shape=jax.ShapeDtypeStruct((n, LANES), F32),
        scratch_shapes=[pltpu.VMEM((h, LANES), F32), pltpu.VMEM((N_CHIPS, h, LANES), F32),
                        pltpu.SemaphoreType.DMA((5,)), pltpu.SemaphoreType.DMA((5,))],
        compiler_params=pltpu.CompilerParams(vmem_limit_bytes=V7X_VMEM_LIMIT_BYTES),
    )(v)


def _adamw(name, w, g, m, v):
    R, Cc = w.shape
    tr = _tile(R, max(8, (1 << 19) // Cc // 8 * 8), 8)
    c1 = 1.0 - ADAM_B1 ** ADAM_STEP
    c2 = 1.0 - ADAM_B2 ** ADAM_STEP

    def body(w_ref, g_ref, m_ref, v_ref, d_ref, nm_ref, nv_ref):
        g_ = g_ref[...]
        nm = ADAM_B1 * m_ref[...] + (1.0 - ADAM_B1) * g_
        nv = ADAM_B2 * v_ref[...] + (1.0 - ADAM_B2) * (g_ * g_)
        d_ref[...] = -ADAM_LR * ((nm / c1) / (jnp.sqrt(nv / c2) + ADAM_EPS) + ADAM_WD * w_ref[...])
        nm_ref[...] = nm
        nv_ref[...] = nv

    spec = pl.BlockSpec((tr, Cc), lambda i: (i, 0))
    return _pcall(
        body, name=name, grid=(R // tr,),
        in_specs=[spec] * 4, out_specs=[spec] * 3,
        out_shape=[jax.ShapeDtypeStruct((R, Cc), F32)] * 3,
        compiler_params=_params("parallel"),
    )(w, g, m, v)


def _pack(arrs, rows):
    flat = jnp.concatenate([a.reshape(-1) for a in arrs])
    return jnp.pad(flat, (0, rows * LANES - flat.shape[0])).reshape(rows, LANES)


def _unpack(packed, like):
    flat, out, o = packed.reshape(-1), [], 0
    for a in like:
        out.append(flat[o:o + a.size].reshape(a.shape))
        o += a.size
    return out


BIG = (
    ("ffn1_w_gate", 1), ("ffn1_w_up", 1), ("ffn1_w_down", 0), ("w_in", 1), ("ssm_w_glu", 0), ("w_out", 0),
    ("ffn2_w_gate", 1), ("ffn2_w_up", 1), ("ffn2_w_down", 0), ("ple_w_gate", 0), ("ple_w_proj", 1),
)
SMALL = ("ffn1_norm", "mix_norm", "attn_out_norm", "ssm_lambda_re", "ssm_lambda_im", "ssm_log_dt", "ssm_b_re", "ssm_b_im",
         "ssm_c_re", "ssm_c_im", "ssm_d", "ssm_b_glu", "ssm_out_norm", "ffn2_norm", "ple_norm", "final_norm")
WEIGHTS = ("ffn1_norm", "ffn1_w_gate", "ffn1_w_up", "ffn1_w_down", "mix_norm", "w_in", "attn_out_norm", "ssm_lambda_re",
           "ssm_lambda_im", "ssm_log_dt", "ssm_b_re", "ssm_b_im", "ssm_c_re", "ssm_c_im", "ssm_d", "ssm_w_glu", "ssm_b_glu",
           "ssm_out_norm", "w_out", "ffn2_norm", "ffn2_w_gate", "ffn2_w_up", "ffn2_w_down", "ple_norm", "ple_w_gate",
           "ple_w_proj", "final_norm")


def _pad_to(a, axis, n):
    pad = [(0, 0), (0, 0)]
    pad[axis] = (0, n - a.shape[axis])
    return jnp.pad(a, pad)


def _local_step(x, p, tgt, w, ex):
    S, D = x.shape
    A = w["attn_out_norm"].shape[-1]
    W = w["ssm_d"].shape[-1]
    G, P = w["ssm_lambda_re"].shape[-2:]
    C = w["ssm_b_re"].shape[-1]
    GB = G // SSM_BLOCK_GROUPS
    T = min(1024, S)
    row = lambda name: w[name].reshape(1, -1)
    gs = {}

    h1, ffn1_saved = _ffn_fwd("ffn1", x, row("ffn1_norm"), ex)
    n2 = _rms_fwd("mix_norm", h1, row("mix_norm"))
    w_in = ex.weight("w_in")
    n2p = _to_attn_order(n2)
    host, done = _carried(ex, "ag", "w_in_qkv")
    qkv, *outs = _mm("w_in_qkv", [n2p], [w_in[:, :3 * A]], [F32], tm=1024, tn=1024, host=host)
    done(outs)
    (s_in,) = _mm("w_in_ssm", [n2], [w_in[:, 3 * A:]], [F32], tm=1024, tn=1024)
    ya, lse = _attn_fwd(qkv, ex)

    col = lambda name: w[name].reshape(G * P, 1)
    logdt_x = jnp.repeat(w["ssm_log_dt"].reshape(G), P).reshape(G * P, 1)
    b_re, b_im = w["ssm_b_re"].reshape(G * P, C), w["ssm_b_im"].reshape(G * P, C)
    lrdt, lidt, bbr, bbi = _ssm_disc(col("ssm_lambda_re"), col("ssm_lambda_im"), logdt_x, b_re, b_im)
    gsz = SSM_BLOCK_GROUPS
    to_bb = lambda t: _block_diag(t.reshape(GB, gsz, P, C).transpose(0, 1, 3, 2))
    bb = jnp.concatenate([to_bb(bbr), to_bb(bbi)], axis=2).astype(BF16)
    to_cc = lambda t: _block_diag(t.reshape(GB, gsz, C, P).transpose(0, 1, 3, 2))
    cc = jnp.concatenate([to_cc(w["ssm_c_re"]), -to_cc(w["ssm_c_im"])], axis=1).astype(BF16)
    lam_dt = jnp.stack([lrdt.reshape(GB, gsz * P), lidt.reshape(GB, gsz * P)], axis=1)
    ufp = _ssm_perm(s_in, T)
    ypre, hstart = _ssm_fwd(ufp, bb, cc, lam_dt, row("ssm_d"), T)

    def glu_in(ins, ps):
        yg = _gelu(ins[0])
        return [yg, yg], []

    yg, ygb = _rowwise("ssm_gelu", glu_in, [ypre], [], [(W, F32), (W, BF16)])
    w_glu = ex.weight("ssm_w_glu")

    def glu_out(accs, ex):
        gl = accs[0] + ex[1]
        return [ex[0] * _sigmoid(gl), gl]

    ybp, gl = _mm("ssm_glu", [ygb], [w_glu], [F32, F32], extras=[(yg, "mn"), (row("ssm_b_glu"), "n")],
                  epilogue=glu_out, tm=1024, tn=1024)
    yb = _ssm_unperm(ybp, T)
    na = _from_attn_order(_rms_fwd("attn_out_norm", ya, row("attn_out_norm")))
    nb = _rms_fwd("ssm_out_norm", yb, row("ssm_out_norm"))
    w_out = ex.weight("w_out")
    (h2,) = _mm("w_out", [na, nb], [w_out[:A], w_out[A:]], [F32], pairs=((0, 0, 0), (1, 1, 0)), extras=[(h1, "mn")],
                epilogue=lambda accs, ex: [ex[0] + accs[0]], tm=1024, tn=1024)
    h3, ffn2_saved = _ffn_fwd("ffn2", h2, row("ffn2_norm"), ex)
    n4 = _rms_fwd("ple_norm", h3, row("ple_norm"))
    (pe,) = _mm("ple_proj", [p], [ex.weight("ple_w_proj")], [F32], tm=1024, tn=1024)

    def ple_out(accs, ex):
        gate = _sigmoid(accs[0])
        return [ex[1] + gate * ex[0], gate]

    h4, gate = _mm("ple_gate", [n4], [ex.weight("ple_w_gate")], [F32, F32], extras=[(pe, "mn"), (h3, "mn")],
                   epilogue=ple_out, tm=1024, tn=1024)

    dh4, err2, gs["final_norm"] = _loss_head(h4, tgt, row("final_norm"))
    loss = (0.5 / D) * jnp.sum(err2)

    def ple_bwd(ins, ps):
        dh, gt, pe_ = ins
        return [dh * gt, dh * pe_ * gt * (1.0 - gt)], []

    dpe, dpg = _rowwise("ple_bwd", ple_bwd, [dh4, gate, pe], [], [(D, BF16), (D, BF16)])
    (d_ple_proj,) = _mm("ple_dproj", [p], [dpe], [BF16], ta=True, tm=256, tn=2048, tk=1024)
    (d_ple_gate,) = _mm("ple_dgate", [n4], [dpg], [BF16], ta=True, tm=1024, tn=1024, tk=2048)
    (dn4,) = _mm("ple_dn", [dpg], [ex.weight("ple_w_gate")], [F32], tb=True, tm=1024, tn=1024)
    (dh3, dh3b), gs["ple_norm"] = _rms_bwd("ple_dnorm", dn4, h3, row("ple_norm"), dres=dh4, copy_scale=0.5)
    (dh2, dh2b), gs["ffn2_norm"] = _ffn_bwd("ffn2", dh3, dh3b, h2, row("ffn2_norm"), ex, ffn2_saved, copy_scale=1.0)
    (dna,) = _mm("w_out_dna", [_to_attn_order(dh2b)], [w_out[:A]], [F32], tb=True, tm=1024, tn=1024)
    (dnb,) = _mm("w_out_dnb", [dh2b], [w_out[A:]], [F32], tb=True, tm=1024, tn=1024)
    (d_wout_a,) = _mm("w_out_dwa", [na], [dh2b], [BF16], ta=True, tm=1024, tn=1024, tk=2048)
    (d_wout_b,) = _mm("w_out_dwb", [nb], [dh2b], [BF16], ta=True, tm=1024, tn=1024, tk=2048)
    d_w_out = jnp.concatenate([d_wout_a, d_wout_b], axis=0)
    (dya,), gs["attn_out_norm"] = _rms_bwd("attn_out_dnorm", dna, ya, row("attn_out_norm"))
    (dyb,), gs["ssm_out_norm"] = _rms_bwd("ssm_out_dnorm", dnb, yb, row("ssm_out_norm"))

    dybp = _ssm_perm(dyb, T)

    def glu_bwd(ins, ps):
        dy, yg_, gl_ = ins
        sg = _sigmoid(gl_)
        dgl = dy * yg_ * sg * (1.0 - sg)
        return [dgl, dy * sg], [jnp.sum(dgl, axis=0, keepdims=True)]

    dgl, dyg_direct, gs["ssm_b_glu"] = _rowwise("ssm_glu_bwd", glu_bwd, [dybp, yg, gl], [], [(W, BF16), (W, F32)], accs=[W])
    (d_w_glu,) = _mm("ssm_dwglu", [ygb], [dgl], [BF16], ta=True, tm=1024, tn=1024, tk=2048)
    (dypre,) = _mm("ssm_dyg", [dgl], [w_glu], [F32], tb=True, extras=[(dyg_direct, "mn"), (ypre, "mn")],
                   epilogue=lambda accs, ex: [(accs[0] + ex[0]) * _gelu_grad(ex[1])], tm=1024, tn=1024)
    dufp, dbb, dcc, da, gs["ssm_d"] = _ssm_bwd(ufp, dypre, bb, bb.transpose(0, 2, 1), cc, cc.transpose(0, 2, 1),
                                               lam_dt, row("ssm_d"), hstart, T)
    ns = gsz * P
    from_bb = lambda t: _block_diag_take(t, gsz).transpose(0, 1, 3, 2).reshape(G * P, C)
    from_cc = lambda t: _block_diag_take(t, gsz).transpose(0, 1, 3, 2).reshape(w["ssm_c_re"].shape)
    gs["ssm_c_re"], gs["ssm_c_im"] = from_cc(dcc[:, :ns]), -from_cc(dcc[:, ns:])
    da = da.sum(axis=1)
    dar, dai = da[:, :ns].reshape(G * P, 1), da[:, ns:].reshape(G * P, 1)
    dlr, dli, dlogdt, dbr, dbi = _ssm_disc_bwd(col("ssm_lambda_re"), col("ssm_lambda_im"), logdt_x, b_re, b_im,
                                               dar, dai, from_bb(dbb[:, :, :ns]), from_bb(dbb[:, :, ns:]))
    gs["ssm_lambda_re"], gs["ssm_lambda_im"] = dlr.reshape(w["ssm_lambda_re"].shape), dli.reshape(w["ssm_lambda_im"].shape)
    gs["ssm_log_dt"] = dlogdt.reshape(G, P).sum(axis=1).reshape(w["ssm_log_dt"].shape)
    gs["ssm_b_re"], gs["ssm_b_im"] = dbr.reshape(w["ssm_b_re"].shape), dbi.reshape(w["ssm_b_im"].shape)
    ds_in = _ssm_unperm(dufp, T)

    dq, dk, dv = _attn_bwd(qkv, dya, ya, lse)
    dqkv = jnp.concatenate([dq, dk, dv], axis=1).astype(BF16)
    (d_w_qkv,) = _mm("w_in_dw_qkv", [n2p], [dqkv], [BF16], ta=True, tm=1024, tn=1024, tk=2048)
    (d_w_s,) = _mm("w_in_dw_ssm", [n2], [ds_in], [BF16], ta=True, tm=1024, tn=1024, tk=2048)
    d_w_in = jnp.concatenate([d_w_qkv, d_w_s], axis=1)
    dz = jnp.concatenate([_from_attn_order(dqkv), ds_in.astype(BF16)], axis=1)
    (dn2,) = _mm("w_in_dn", [dz], [w_in], [F32], tb=True, tm=1024, tn=1024)
    (dh1, dh1b), gs["mix_norm"] = _rms_bwd("mix_dnorm", dn2, h1, row("mix_norm"), dres=dh2, copy_scale=0.5)
    for k, g in (("ple_w_gate", d_ple_gate), ("ple_w_proj", d_ple_proj), ("w_out", d_w_out), ("ssm_w_glu", d_w_glu),
                 ("w_in", d_w_in)):
        ex.grad(k, g)
    (dx,), gs["ffn1_norm"] = _ffn_bwd("ffn1", dh1, dh1b, x, row("ffn1_norm"), ex, ffn1_saved, copy_scale=None)
    small = {k: gs[k].reshape(w[k].shape) for k in SMALL}
    return loss, dx, ex.finish(), small


def kernel(x, p, ffn1_norm, ffn1_w_gate, ffn1_w_up, ffn1_w_down, mix_norm, w_in, attn_out_norm, ssm_lambda_re, ssm_lambda_im, ssm_log_dt, ssm_b_re, ssm_b_im, ssm_c_re, ssm_c_im, ssm_d, ssm_w_glu, ssm_b_glu, ssm_out_norm, w_out, ffn2_norm, ffn2_w_gate, ffn2_w_up, ffn2_w_down, ple_norm, ple_w_gate, ple_w_proj, final_norm, loss_target, m_ffn1_norm, m_ffn1_w_gate, m_ffn1_w_up, m_ffn1_w_down, m_mix_norm, m_w_in, m_attn_out_norm, m_ssm_lambda_re, m_ssm_lambda_im, m_ssm_log_dt, m_ssm_b_re, m_ssm_b_im, m_ssm_c_re, m_ssm_c_im, m_ssm_d, m_ssm_w_glu, m_ssm_b_glu, m_ssm_out_norm, m_w_out, m_ffn2_norm, m_ffn2_w_gate, m_ffn2_w_up, m_ffn2_w_down, m_ple_norm, m_ple_w_gate, m_ple_w_proj, m_final_norm, v_ffn1_norm, v_ffn1_w_gate, v_ffn1_w_up, v_ffn1_w_down, v_mix_norm, v_w_in, v_attn_out_norm, v_ssm_lambda_re, v_ssm_lambda_im, v_ssm_log_dt, v_ssm_b_re, v_ssm_b_im, v_ssm_c_re, v_ssm_c_im, v_ssm_d, v_ssm_w_glu, v_ssm_b_glu, v_ssm_out_norm, v_w_out, v_ffn2_norm, v_ffn2_w_gate, v_ffn2_w_up, v_ffn2_w_down, v_ple_norm, v_ple_w_gate, v_ple_w_proj, v_final_norm):
    args = locals()
    w = {k: args[k] for k in WEIGHTS}
    m = {k: args["m_" + k] for k in WEIGHTS}
    v = {k: args["v_" + k] for k in WEIGHTS}
    w2 = {k: w[k].reshape(w[k].shape[-2:]) for k, _ in BIG}

    axes = [ax for _, ax in BIG]
    padded = {k: -(-w2[k].shape[ax] // LANES) * LANES for k, ax in BIG}
    shards = [_pad_to(w2[k].astype(BF16), ax, padded[k]) for k, ax in BIG]
    ex = _Exchange(dict(zip([k for k, _ in BIG], shards)), dict(BIG))
    loss_local, dx, summed, gsmall = _local_step(x[0], p[0, 0], loss_target[0], w, ex)
    loss = lax.psum(loss_local, MESH_AXES)
    n_small = sum(w[k].size for k in SMALL)
    rows = -(-n_small // (2 * SUBLANES * LANES)) * 2 * SUBLANES
    gs_sum = _all_reduce_small(_pack([gsmall[k] for k in SMALL], rows))

    grads, delta, new_m, new_v = {}, {}, {}, {}
    for k, ax in BIG:
        gfull = summed[k]
        g2 = lax.slice_in_dim(gfull, 0, w2[k].shape[ax], axis=ax)
        d2, nm2, nv2 = _adamw("adamw_" + k, w2[k], g2, m[k].reshape(w2[k].shape), v[k].reshape(w2[k].shape))
        grads[k], delta[k], new_m[k], new_v[k] = (t.reshape(w[k].shape) for t in (g2, d2, nm2, nv2))
    small_like = [w[k] for k in SMALL]
    ds, nms, nvs = _adamw("adamw_small", _pack(small_like, rows), gs_sum, _pack([m[k] for k in SMALL], rows),
                          _pack([v[k] for k in SMALL], rows))
    for k, g_, d_, nm_, nv_ in zip(SMALL, _unpack(gs_sum, small_like), _unpack(ds, small_like),
                                   _unpack(nms, small_like), _unpack(nvs, small_like)):
        grads[k], delta[k], new_m[k], new_v[k] = g_, d_, nm_, nv_

    return (loss, dx[None], *[grads[k] for k in WEIGHTS], *[delta[k] for k in WEIGHTS],
            *[new_m[k] for k in WEIGHTS], *[new_v[k] for k in WEIGHTS])
```

```python
import functools
import math

import jax
import jax.numpy as jnp
from jax import lax
from jax.experimental import pallas as pl
from jax.experimental.pallas import tpu as pltpu

F32 = jnp.float32
BF16 = jnp.bfloat16
MESH = pl.DeviceIdType.MESH
MESH_AXES = ("x", "y", "c")
N_CHIPS = 4
N_DEV = 8

V7X_VMEM_LIMIT_BYTES = 56 << 20
LANES = 128
SUBLANES = 8

HEAD_DIM = 64
SWA_BLOCK = 128
DILATIONS = (1, 4, 16)
SSM_BLOCK_GROUPS = 8
NORM_EPS = 1e-6
MASK_VALUE = -1e30

ADAM_LR = 0.001
ADAM_B1 = 0.9
ADAM_B2 = 0.999
ADAM_EPS = 1e-08
ADAM_WD = 0.01
ADAM_STEP = 10

GELU_C = math.sqrt(2.0 / math.pi)
GELU_K = 0.044715


def _pcall(body, **kw):
    return pl.pallas_call(body, **kw)


def _params(*sem):
    return pltpu.CompilerParams(dimension_semantics=sem, vmem_limit_bytes=V7X_VMEM_LIMIT_BYTES)


def _tile(n, target, align):
    best = None
    for t in range(align, min(n, target) + 1, align):
        if n % t == 0:
            best = t
    return n if best is None else best


def _sigmoid(x):
    return 0.5 * jnp.tanh(0.5 * x) + 0.5


class _Host:
    def __init__(self, ins, out_shapes, n_sem, start, wait):
        self.ins, self.out_shapes, self.n_sem, self.start, self.wait = ins, out_shapes, n_sem, start, wait


def _mm(name, lhs, rhs, outs, pairs=((0, 0, 0),), epilogue=None, extras=(), ta=False, tb=False,
        tm=1024, tn=512, tk=2048, host=None):
    nl, nr, ne, no = len(lhs), len(rhs), len(extras), len(outs)
    nhi, nho = (len(host.ins), len(host.out_shapes)) if host else (0, 0)
    n_acc = 1 + max(p[2] for p in pairs)
    (K, M) = lhs[0].shape if ta else lhs[0].shape[::-1]
    (N, K2) = rhs[0].shape if tb else rhs[0].shape[::-1]
    assert K == K2, (name, lhs[0].shape, rhs[0].shape)
    tm, tn, tk = _tile(M, tm, LANES), _tile(N, tn, LANES), _tile(K, tk, LANES)
    ni, nj, nk = M // tm, N // tn, K // tk
    n_scr = n_acc if nk > 1 else 0
    if epilogue is None:
        epilogue = lambda accs, ex: accs
    dn = (((0 if ta else 1,), (1 if tb else 0,)), ((), ()))

    def body(*refs):
        refs = list(refs)
        take = lambda n: [refs.pop(0) for _ in range(n)]
        l, r, e, hin, o, hout, acc = take(nl), take(nr), take(ne), take(nhi), take(no), take(nho), take(n_scr)
        i, j, k = pl.program_id(0), pl.program_id(1), pl.program_id(2)
        if host:
            @pl.when((i == 0) & (j == 0) & (k == 0))
            def _():
                host.start(hin, hout, *refs)

        parts = [None] * n_acc
        for li, ri, ai in pairs:
            d = lax.dot_general(l[li][...].astype(BF16), r[ri][...].astype(BF16), dn,
                                preferred_element_type=F32)
            parts[ai] = d if parts[ai] is None else parts[ai] + d

        def finish(accs):
            res = epilogue(accs, [x[...] for x in e])
            for ref, val in zip(o, res):
                ref[...] = val.astype(ref.dtype)

        if nk == 1:
            finish(parts)
        else:
            @pl.when(k == 0)
            def _():
                for ai in range(n_acc):
                    acc[ai][...] = parts[ai]

            @pl.when(k > 0)
            def _():
                for ai in range(n_acc):
                    acc[ai][...] += parts[ai]

            @pl.when(k == nk - 1)
            def _():
                finish([a[...] for a in acc])

        if host:
            @pl.when((i == ni - 1) & (j == nj - 1) & (k == nk - 1))
            def _():
                host.wait(hin, hout, *refs)

    lspec = pl.BlockSpec((tk, tm), lambda i, j, k: (k, i)) if ta else pl.BlockSpec((tm, tk), lambda i, j, k: (i, k))
    rspec = pl.BlockSpec((tn, tk), lambda i, j, k: (j, k)) if tb else pl.BlockSpec((tk, tn), lambda i, j, k: (k, j))
    especs = []
    for arr, kind in extras:
        if kind == "mn":
            especs.append(pl.BlockSpec((tm, tn), lambda i, j, k: (i, j)))
        elif kind == "n":
            especs.append(pl.BlockSpec((1, tn), lambda i, j, k: (0, j)))
        else:
            especs.append(pl.BlockSpec((tm, 1), lambda i, j, k: (i, 0)))
    any_spec = pl.BlockSpec(memory_space=pl.ANY)
    sems = [pltpu.SemaphoreType.DMA((host.n_sem,)), pltpu.SemaphoreType.DMA((host.n_sem,))] if host else []
    res = _pcall(
        body, name=name,
        grid=(ni, nj, nk),
        in_specs=[lspec] * nl + [rspec] * nr + especs + [any_spec] * nhi,
        out_specs=[pl.BlockSpec((tm, tn), lambda i, j, k: (i, j))] * no + [any_spec] * nho,
        out_shape=[jax.ShapeDtypeStruct((M, N), dt) for dt in outs] + (list(host.out_shapes) if host else []),
        scratch_shapes=[pltpu.VMEM((tm, tn), F32)] * n_scr + sems,
        compiler_params=_params(*(("arbitrary",) * 3 if host else ("parallel", "parallel", "arbitrary"))),
    )(*lhs, *rhs, *[a for a, _ in extras], *(host.ins if host else []))
    return res


def _rowwise(name, fn, ins, params, outs, accs=(), ts=256):
    S = ins[0].shape[0]
    ts = _tile(S, ts, 16)
    ni, npar, no, na = len(ins), len(params), len(outs), len(accs)

    def body(*refs):
        i_refs, p_refs = refs[:ni], refs[ni:ni + npar]
        o_refs = refs[ni + npar:ni + npar + no]
        a_refs = refs[ni + npar + no:]
        res_o, res_a = fn([r[...] for r in i_refs], [r[...] for r in p_refs])
        for ref, val in zip(o_refs, res_o):
            ref[...] = val.astype(ref.dtype)
        if na:
            @pl.when(pl.program_id(0) == 0)
            def _():
                for ref in a_refs:
                    ref[...] = jnp.zeros(ref.shape, F32)

            for ref, val in zip(a_refs, res_a):
                ref[...] += val

    res = _pcall(
        body, name=name,
        grid=(S // ts,),
        in_specs=[pl.BlockSpec((ts, a.shape[1]), lambda i: (i, 0)) for a in ins]
        + [pl.BlockSpec(p.shape, lambda i: (0, 0)) for p in params],
        out_specs=[pl.BlockSpec((ts, w), lambda i: (i, 0)) for w, _ in outs]
        + [pl.BlockSpec((1, w), lambda i: (0, 0)) for w in accs],
        out_shape=[jax.ShapeDtypeStruct((S, w), dt) for w, dt in outs]
        + [jax.ShapeDtypeStruct((1, w), F32) for w in accs],
        compiler_params=_params("arbitrary"),
    )(*ins, *params)
    return res


def _xhat(x):
    r = lax.rsqrt(jnp.mean(x * x, axis=-1, keepdims=True) + NORM_EPS)
    return x * r, r


def _rms_fwd(name, x, g):
    def fn(ins, ps):
        xh, _ = _xhat(ins[0])
        return [xh * ps[0]], []

    return _rowwise(name, fn, [x], [g], [(x.shape[1], BF16)])[0]


def _rms_bwd(name, dn, x, g, dres=None, copy_scale=None):
    w = x.shape[1]

    def fn(ins, ps):
        dn_, x_ = ins[0], ins[1]
        xh, r = _xhat(x_)
        dxh = dn_ * ps[0]
        dx = r * (dxh - xh * jnp.mean(dxh * xh, axis=-1, keepdims=True))
        if dres is not None:
            dx = dx + ins[2]
        o = [dx] + ([dx * copy_scale] if copy_scale is not None else [])
        return o, [jnp.sum(dn_ * xh, axis=0, keepdims=True)]

    ins = [dn, x] + ([dres] if dres is not None else [])
    outs = [(w, F32)] + ([(w, BF16)] if copy_scale is not None else [])
    res = _rowwise(name, fn, ins, [g], outs, accs=[w])
    return res[:-1], res[-1]


def _swiglu_epilogue(accs, ex):
    g, u = accs
    sg = _sigmoid(g)
    s = g * sg
    return [u * (sg + s * (1.0 - sg)), s, s * u]


def _dswiglu_epilogue(accs, ex):
    da = accs[0]
    return [da * ex[0].astype(F32), da * ex[1].astype(F32)]


def _carried(ex, kind, kernel):
    if kernel not in (ex.AG_PLAN if kind == "ag" else ex.RS_PLAN):
        return None, lambda outs: None
    if kind == "ag":
        return ex.ag_host(kernel), lambda outs: ex.ag_done(kernel, outs)
    return ex.rs_host(kernel), lambda outs: ex.rs_done(kernel, outs)


def _ffn_fwd(tag, h, gnorm, ex):
    n = _rms_fwd(tag + "_norm", h, gnorm)
    host, done = _carried(ex, "ag", tag + "_up")
    g, u, a, *outs = _mm(tag + "_up", [n], [ex.weight(tag + "_w_gate"), ex.weight(tag + "_w_up")], [BF16, BF16, BF16],
                         pairs=((0, 0, 0), (0, 1, 1)), epilogue=_swiglu_epilogue, tm=1024, tn=512, host=host)
    done(outs)
    host, done = _carried(ex, "ag", tag + "_down")
    hout, *outs = _mm(tag + "_down", [a], [ex.weight(tag + "_w_down")], [F32], extras=[(h, "mn")],
                      epilogue=lambda accs, ex_: [ex_[0] + 0.5 * accs[0]], tm=512, tn=1024, tk=8192, host=host)
    done(outs)
    return hout, (n, g, u, a)


def _ffn_bwd(tag, dh, dhb_half, h, gnorm, ex, saved, copy_scale):
    n, g, u, a = saved
    wg, wu, wd = (ex.weight(tag + k) for k in ("_w_gate", "_w_up", "_w_down"))
    host, done = _carried(ex, "rs", tag + "_dact")
    dg, du, *outs = _mm(tag + "_dact", [dhb_half], [wd], [BF16, BF16], tb=True, extras=[(g, "mn"), (u, "mn")],
                        epilogue=_dswiglu_epilogue, tm=1024, tn=512, host=host)
    done(outs)
    host, done = _carried(ex, "rs", tag + "_dwd")
    dwd, *outs = _mm(tag + "_dwd", [a], [dhb_half], [BF16], ta=True, tm=512, tn=2048, tk=2048, host=host)
    done(outs)
    ex.grad(tag + "_w_down", dwd)
    host, done = _carried(ex, "rs", tag + "_dwgu")
    dwg, dwu, *outs = _mm(tag + "_dwgu", [n], [dg, du], [BF16, BF16], pairs=((0, 0, 0), (0, 1, 1)), ta=True,
                          tm=1024, tn=512, tk=2048, host=host)
    done(outs)
    ex.grad(tag + "_w_gate", dwg)
    ex.grad(tag + "_w_up", dwu)
    host, done = _carried(ex, "rs", tag + "_dn")
    dn, *outs = _mm(tag + "_dn", [dg, du], [wg, wu], [F32], pairs=((0, 0, 0), (1, 1, 0)), tb=True,
                    tm=1024, tn=1024, tk=1408, host=host)
    done(outs)
    return _rms_bwd(tag + "_dnorm", dn, h, gnorm, dres=dh, copy_scale=copy_scale)


ATTN_HEAD_PAIRS = 8


def _to_attn_order(a):
    S, w = a.shape
    return a.reshape(S // 16, 16, w).transpose(1, 0, 2).reshape(S, w)


def _from_attn_order(a):
    S, w = a.shape
    return a.reshape(16, S // 16, w).transpose(1, 0, 2).reshape(S, w)


def _attn_geom(S, d):
    s16 = S // 16
    if d == 16:
        return (16, s16), (1, SWA_BLOCK), (lambda r, b: (r, b)), 16, s16 // SWA_BLOCK
    if d == 4:
        return (4, 4, s16), (4, 1, SWA_BLOCK // 4), (lambda r, b: (0, r, b)), 4, s16 // (SWA_BLOCK // 4)
    return (16, s16), (16, SWA_BLOCK // 16), (lambda r, b: (0, b)), 1, s16 // (SWA_BLOCK // 16)


def _attn_pos(rho, d):
    if d == 16:
        return rho
    if d == 4:
        return 4 * (rho & 31) + (rho >> 5)
    return 16 * (rho & 7) + (rho >> 3)


def _attn_spec(S, d, lb, col, shift=0):
    _, blk, idx, _, nb = _attn_geom(S, d)
    return pl.BlockSpec(blk + (lb,), lambda r, cb, b: idx(r, jnp.clip(b + shift, 0, nb - 1)) + (col(cb),))


def _attn_view(a, d):
    return a.reshape(_attn_geom(a.shape[0], d)[0] + (a.shape[1],))


def _attn_valid(d):
    qp = _attn_pos(lax.broadcasted_iota(jnp.int32, (SWA_BLOCK, 2 * SWA_BLOCK), 0), d)
    kk = lax.broadcasted_iota(jnp.int32, (SWA_BLOCK, 2 * SWA_BLOCK), 1)
    kp = _attn_pos(kk & (SWA_BLOCK - 1), d)
    is_prev = kk < SWA_BLOCK
    return qp, kp, is_prev


def _head_masks(rows=SWA_BLOCK):
    lane = lax.broadcasted_iota(jnp.int32, (rows, LANES), 1)
    return [lane < HEAD_DIM, lane >= HEAD_DIM]


def _attn_ld(ref, sl):
    t = ref[(slice(None),) * (len(ref.shape) - 1) + (sl,)]
    return t.reshape(-1, t.shape[-1])


def _attn_st(ref, sl, val):
    ref[(slice(None),) * (len(ref.shape) - 1) + (sl,)] = val.reshape(ref.shape[:-1] + (val.shape[-1],))


def _per_head(t, first):
    sw = pltpu.roll(t, HEAD_DIM, 1)
    lo = lax.broadcasted_iota(jnp.int32, t.shape, 1) < HEAD_DIM
    return jnp.where(lo, t, sw) if first else jnp.where(lo, sw, t)


def _dot_nt(a, b):
    return lax.dot_general(a, b, (((1,), (1,)), ((), ())), preferred_element_type=F32)


def _dot_tn(a, b):
    return lax.dot_general(a, b, (((0,), (0,)), ((), ())), preferred_element_type=F32)


def _dot(a, b):
    return jnp.dot(a, b, preferred_element_type=F32)


def _keep(mask, t):
    return jnp.where(mask, t.astype(F32), 0.0).astype(BF16)


def _attn_cols(A):
    lb = min(A, LANES * ATTN_HEAD_PAIRS)
    ncol = A // lb
    return lb, ncol, [lambda cb, part=part: part * ncol + cb for part in range(3)], (lambda cb: cb)


def _attn_fwd_stage(name, qkv, d, prev, final, host=None):
    S, A3 = qkv.shape
    A = A3 // 3
    lb, ncol, (cq, ck, cv), ca = _attn_cols(A)
    view, _, _, nres, nb = _attn_geom(S, d)
    scale = HEAD_DIM ** -0.5
    has_prev = prev is not None
    n_out = 2 if final else 3
    nhi, nho = (len(host.ins), len(host.out_shapes)) if host else (0, 0)

    def body(*refs):
        q_ref, kp_ref, kc_ref, vp_ref, vc_ref = refs[:5]
        p_refs = refs[5:8] if has_prev else ()
        n_in = 5 + len(p_refs)
        hin, o_refs = refs[n_in:n_in + nhi], refs[n_in + nhi:n_in + nhi + n_out]
        hout, sems = refs[n_in + nhi + n_out:n_in + nhi + n_out + nho], refs[n_in + nhi + n_out + nho:]
        b = pl.program_id(2)
        if host:
            @pl.when((pl.program_id(0) == 0) & (pl.program_id(1) == 0) & (b == 0))
            def _():
                host.start(hin, hout, *sems)
        qp, kp_, is_prev = _attn_valid(d)
        valid = (is_prev & (kp_ >= qp) & (b > 0)) | (jnp.logical_not(is_prev) & (kp_ <= qp))
        hm, hm2 = _head_masks(), _head_masks(2 * SWA_BLOCK)
        for hp in range(lb // LANES):
            sl = slice(hp * LANES, (hp + 1) * LANES)
            q = _attn_ld(q_ref, sl)
            k2 = jnp.concatenate([_attn_ld(kp_ref, sl), _attn_ld(kc_ref, sl)], axis=0).astype(BF16)
            v2 = jnp.concatenate([_attn_ld(vp_ref, sl), _attn_ld(vc_ref, sl)], axis=0)
            o = jnp.zeros((SWA_BLOCK, LANES), F32)
            m = jnp.zeros((SWA_BLOCK, LANES), F32)
            l = jnp.zeros((SWA_BLOCK, LANES), F32)
            for hh in range(2):
                s = jnp.where(valid, _dot_nt(_keep(hm[hh], q), k2) * scale, MASK_VALUE)
                mh = jnp.max(s, axis=-1, keepdims=True)
                p = jnp.exp(s - mh)
                lh = jnp.sum(p, axis=-1, keepdims=True)
                o = o + _dot(p.astype(BF16), _keep(hm2[hh], v2))
                m = jnp.where(hm[hh], mh, m)
                l = jnp.where(hm[hh], lh, l)
            if has_prev:
                po, pm, pl_ = (_attn_ld(r, sl) for r in p_refs)
                mn = jnp.maximum(m, pm)
                w_new, w_old = jnp.exp(m - mn), jnp.exp(pm - mn)
                o = o * w_new + po * w_old
                l = l * w_new + pl_ * w_old
                m = mn
            if final:
                _attn_st(o_refs[0], sl, o / l)
                _attn_st(o_refs[1], sl, m + jnp.log(l))
            else:
                _attn_st(o_refs[0], sl, o)
                _attn_st(o_refs[1], sl, m)
                _attn_st(o_refs[2], sl, l)

        if host:
            @pl.when((pl.program_id(0) == nres - 1) & (pl.program_id(1) == ncol - 1) & (b == nb - 1))
            def _():
                host.wait(hin, hout, *sems)

    qk = _attn_view(qkv, d)
    prev_v = [_attn_view(t, d) for t in prev] if has_prev else []
    sp = functools.partial(_attn_spec, S, d, lb)
    any_spec = pl.BlockSpec(memory_space=pl.ANY)
    res = _pcall(
        body, name=name,
        grid=(nres, ncol, nb),
        in_specs=[sp(cq), sp(ck, -1), sp(ck), sp(cv, -1), sp(cv)] + [sp(ca)] * len(prev_v) + [any_spec] * nhi,
        out_specs=[sp(ca)] * n_out + [any_spec] * nho,
        out_shape=[jax.ShapeDtypeStruct(view + (A,), F32)] * n_out + (list(host.out_shapes) if host else []),
        scratch_shapes=[pltpu.SemaphoreType.DMA((host.n_sem,)), pltpu.SemaphoreType.DMA((host.n_sem,))] if host else [],
        compiler_params=_params(*(("arbitrary",) * 3 if host else ("parallel", "parallel", "arbitrary"))),
    )(qk, qk, qk, qk, qk, *prev_v, *(host.ins if host else []))
    return [t.reshape(S, A) for t in res[:n_out]], res[n_out:]


def _attn_fwd(qkv, ex):
    st = None
    for i, d in enumerate(DILATIONS):
        name = "attn_fwd_d%d" % d
        host, done = _carried(ex, "ag", name)
        st, outs = _attn_fwd_stage(name, qkv, d, st, final=(i == len(DILATIONS) - 1), host=host)
        done(outs)
    return st


def _attn_delta(dya, ya):
    S, A = ya.shape
    ri = lax.broadcasted_iota(jnp.int32, (A, A), 0) // HEAD_DIM
    ci = lax.broadcasted_iota(jnp.int32, (A, A), 1) // HEAD_DIM
    ones_bd = (ri == ci).astype(BF16)

    def fn(ins, ps):
        prod = ins[0] * ins[1]
        hi = prod.astype(BF16)
        lo = (prod - hi.astype(F32)).astype(BF16)
        return [_dot(hi, ps[0]) + _dot(lo, ps[0])], []

    return _rowwise("attn_delta", fn, [dya, ya], [ones_bd], [(A, F32)])[0]


def _attn_bwd_stage(name, qkv, do, lse, delta, d, prev):
    S, A3 = qkv.shape
    A = A3 // 3
    lb, ncol, (cq, ck, cv), ca = _attn_cols(A)
    view, _, _, nres, nb = _attn_geom(S, d)
    scale = HEAD_DIM ** -0.5
    has_prev = prev is not None
    lane_slices = [slice(hp * LANES, (hp + 1) * LANES) for hp in range(lb // LANES)]

    def body(*refs):
        q_ref, kp_ref, kc_ref, vp_ref, vc_ref, do_ref, lse_ref, dl_ref = refs[:8]
        p_refs = refs[8:11] if has_prev else ()
        dq_ref, dk_ref, dv_ref, dk_c, dv_c = refs[8 + len(p_refs):]
        b = pl.program_id(2)

        def put_keys(sl, dk, dv):
            if has_prev:
                dk, dv = dk + _attn_ld(p_refs[1], sl), dv + _attn_ld(p_refs[2], sl)
            _attn_st(dk_ref, sl, dk)
            _attn_st(dv_ref, sl, dv)

        @pl.when(b == 0)
        def _():
            dk_c[...] = jnp.zeros(dk_c.shape, F32)
            dv_c[...] = jnp.zeros(dv_c.shape, F32)

        @pl.when(b < nb)
        def _():
            qp, kp_, is_prev = _attn_valid(d)
            valid = (is_prev & (kp_ >= qp) & (b > 0)) | (jnp.logical_not(is_prev) & (kp_ <= qp))
            hm, hm2 = _head_masks(), _head_masks(2 * SWA_BLOCK)
            for sl in lane_slices:
                q, do_, lse_, dl_ = (_attn_ld(r, sl) for r in (q_ref, do_ref, lse_ref, dl_ref))
                k2 = jnp.concatenate([_attn_ld(kp_ref, sl), _attn_ld(kc_ref, sl)], axis=0)
                v2 = jnp.concatenate([_attn_ld(vp_ref, sl), _attn_ld(vc_ref, sl)], axis=0).astype(BF16)
                k2b = k2.astype(BF16)
                dq = jnp.zeros((SWA_BLOCK, LANES), F32)
                dk2 = jnp.zeros((2 * SWA_BLOCK, LANES), F32)
                dv2 = jnp.zeros((2 * SWA_BLOCK, LANES), F32)
                for hh in range(2):
                    qh, doh = _keep(hm[hh], q), _keep(hm[hh], do_)
                    lh, dh = _per_head(lse_, hh == 0), _per_head(dl_, hh == 0)
                    lh2, dh2 = jnp.concatenate([lh, lh], axis=1), jnp.concatenate([dh, dh], axis=1)
                    p = jnp.where(valid, jnp.exp(_dot_nt(qh, k2b) * scale - lh2), 0.0)
                    ds = (p * (_dot_nt(doh, v2) - dh2)).astype(BF16)
                    dq = dq + _dot(ds, _keep(hm2[hh], k2))
                    dk2 = dk2 + _dot_tn(ds, qh)
                    dv2 = dv2 + _dot_tn(p.astype(BF16), doh)
                dq, dk2 = dq * scale, dk2 * scale
                if has_prev:
                    dq = dq + _attn_ld(p_refs[0], sl)
                _attn_st(dq_ref, sl, dq)
                put_keys(sl, dk_c[:, sl] + dk2[:SWA_BLOCK], dv_c[:, sl] + dv2[:SWA_BLOCK])
                dk_c[:, sl] = dk2[SWA_BLOCK:]
                dv_c[:, sl] = dv2[SWA_BLOCK:]

        @pl.when(b == nb)
        def _():
            for sl in lane_slices:
                put_keys(sl, dk_c[:, sl], dv_c[:, sl])

    qk = _attn_view(qkv, d)
    acts = [_attn_view(t, d) for t in (do, lse, delta)] + ([_attn_view(t, d) for t in prev] if has_prev else [])
    sp = functools.partial(_attn_spec, S, d, lb)
    res = _pcall(
        body, name=name,
        grid=(nres, ncol, nb + 1),
        in_specs=[sp(cq), sp(ck, -1), sp(ck), sp(cv, -1), sp(cv), sp(ca), sp(ca), sp(ca)]
        + ([sp(ca), sp(ca, -1), sp(ca, -1)] if has_prev else []),
        out_specs=[sp(ca), sp(ca, -1), sp(ca, -1)],
        out_shape=[jax.ShapeDtypeStruct(view + (A,), F32)] * 3,
        scratch_shapes=[pltpu.VMEM((SWA_BLOCK, lb), F32)] * 2,
        compiler_params=_params("parallel", "parallel", "arbitrary"),
    )(qk, qk, qk, qk, qk, *acts)
    return [t.reshape(S, A) for t in res]


def _attn_bwd(qkv, dya, ya, lse):
    delta = _attn_delta(dya, ya)
    sums = None
    for d in DILATIONS:
        sums = _attn_bwd_stage("attn_bwd_d%d" % d, qkv, dya, lse, delta, d, sums)
    return sums


def _ssm_perm(a, T):
    S, w = a.shape
    return a.reshape(S // T, SUBLANES, T // SUBLANES, w).transpose(0, 2, 1, 3).reshape(S, w)


def _ssm_unperm(a, T):
    S, w = a.shape
    return a.reshape(S // T, T // SUBLANES, SUBLANES, w).transpose(0, 2, 1, 3).reshape(S, w)


def _ssm_powers(lam_ref, pw_ref, T, ns):
    tc = T // SUBLANES
    n = (lax.broadcasted_iota(jnp.int32, (tc, 1), 0) + 1).astype(F32)
    mag = jnp.exp(n * lam_ref[0, 0:1, :])
    ang = n * lam_ref[0, 1:2, :]
    rows8 = lambda t: jnp.broadcast_to(t[:, None, :], (tc, SUBLANES, ns)).reshape(T, ns)
    pw_ref[:, 0:ns] = rows8(mag * jnp.cos(ang))
    pw_ref[:, ns:2 * ns] = rows8(mag * jnp.sin(ang))


def _ssm_scan(xs, off, pw_ref, carry_ref, T, ns, reverse):
    Tc = T // SUBLANES
    sgn = -1.0 if reverse else 1.0
    ar, ai = pw_ref[0:SUBLANES, 0:ns], sgn * pw_ref[0:SUBLANES, ns:2 * ns]

    def rows(i):
        return pl.ds(pl.multiple_of(off + i * SUBLANES, SUBLANES), SUBLANES)

    def step(k, h):
        hr, hi = h
        r = rows(Tc - 1 - k if reverse else k)
        nr = ar * hr - ai * hi + xs[r, 0:ns]
        ni = ar * hi + ai * hr + xs[r, ns:2 * ns]
        xs[r, 0:ns] = nr
        xs[r, ns:2 * ns] = ni
        return nr, ni

    z = jnp.zeros((SUBLANES, ns), F32)
    er, ei = lax.fori_loop(0, Tc, step, (z, z), unroll=4)
    atr, ati = pw_ref[T - SUBLANES:T, 0:ns], sgn * pw_ref[T - SUBLANES:T, ns:2 * ns]
    rowid = lax.broadcasted_iota(jnp.int32, (SUBLANES, ns), 0)
    cr, ci = carry_ref[:, 0:ns], carry_ref[:, ns:2 * ns]
    ctr, cti = z, z
    for jj in range(SUBLANES):
        j = SUBLANES - 1 - jj if reverse else jj
        sel = rowid == j
        ctr, cti = jnp.where(sel, cr, ctr), jnp.where(sel, ci, cti)
        ejr = jnp.broadcast_to(jnp.sum(jnp.where(sel, er, 0.0), axis=0, keepdims=True), (SUBLANES, ns))
        eji = jnp.broadcast_to(jnp.sum(jnp.where(sel, ei, 0.0), axis=0, keepdims=True), (SUBLANES, ns))
        cr, ci = ejr + atr * cr - ati * ci, eji + atr * ci + ati * cr
    carry_ref[:, 0:ns] = cr
    carry_ref[:, ns:2 * ns] = ci

    def fix(i, _):
        r = rows(i)
        pr_rows = pl.ds(pl.multiple_of((Tc - 1 - i if reverse else i) * SUBLANES, SUBLANES), SUBLANES)
        pr, pi = pw_ref[pr_rows, 0:ns], sgn * pw_ref[pr_rows, ns:2 * ns]
        xs[r, 0:ns] += pr * ctr - pi * cti
        xs[r, ns:2 * ns] += pr * cti + pi * ctr
        return 0

    lax.fori_loop(0, Tc, fix, 0, unroll=4)
    return ctr, cti


def _ssm_fwd(ufp, bb, cc, lam_dt, drow, T):
    S, W = ufp.shape
    GB, cw, ns2 = bb.shape
    ns = ns2 // 2
    NCH = S // T

    def body(uf_ref, bb_ref, cc_ref, lam_ref, d_ref, y_ref, hs_ref, xs, pw, carry):
        @pl.when(pl.program_id(1) == 0)
        def _():
            _ssm_powers(lam_ref, pw, T, ns)
            carry[...] = jnp.zeros(carry.shape, F32)

        uf = uf_ref[...]
        xs[...] = _dot(uf.astype(BF16), bb_ref[0])
        hs_ref[0, 0] = carry[...]
        _ssm_scan(xs, 0, pw, carry, T, ns, reverse=False)
        y_ref[...] = _dot(xs[...].astype(BF16), cc_ref[0]) + d_ref[...] * uf

    return _pcall(
        body, name="ssm_fwd",
        grid=(GB, NCH),
        in_specs=[pl.BlockSpec((T, cw), lambda g, c: (c, g)),
                  pl.BlockSpec((1, cw, ns2), lambda g, c: (g, 0, 0)),
                  pl.BlockSpec((1, ns2, cw), lambda g, c: (g, 0, 0)),
                  pl.BlockSpec((1, 2, ns), lambda g, c: (g, 0, 0)),
                  pl.BlockSpec((1, cw), lambda g, c: (0, g))],
        out_specs=[pl.BlockSpec((T, cw), lambda g, c: (c, g)),
                   pl.BlockSpec((1, 1, SUBLANES, ns2), lambda g, c: (g, c, 0, 0))],
        out_shape=[jax.ShapeDtypeStruct((S, W), F32),
                   jax.ShapeDtypeStruct((GB, NCH, SUBLANES, ns2), F32)],
        scratch_shapes=[pltpu.VMEM((T, ns2), F32), pltpu.VMEM((T, ns2), F32), pltpu.VMEM((SUBLANES, ns2), F32)],
        compiler_params=_params("arbitrary", "arbitrary"),
    )(ufp, bb, cc, lam_dt, drow)


def _ssm_bwd(ufp, dyp, bb, bbt, cc, cct, lam_dt, drow, hstart, T):
    S, W = ufp.shape
    GB, cw, ns2 = bb.shape
    ns = ns2 // 2
    NCH = S // T

    def body(uf_ref, dy_ref, bb_ref, bbt_ref, cc_ref, cct_ref, lam_ref, d_ref, hs_ref,
             duf_ref, dbb_ref, dcc_ref, da_ref, dd_ref, hb, ls, pw, carry_f, carry_b):
        @pl.when(pl.program_id(1) == 0)
        def _():
            _ssm_powers(lam_ref, pw, T, ns)
            carry_b[...] = jnp.zeros(carry_b.shape, F32)
            dbb_ref[...] = jnp.zeros(dbb_ref.shape, F32)
            dcc_ref[...] = jnp.zeros(dcc_ref.shape, F32)
            da_ref[...] = jnp.zeros(da_ref.shape, F32)
            dd_ref[...] = jnp.zeros(dd_ref.shape, F32)

        uf, dy = uf_ref[...], dy_ref[...]
        ufb, dyb = uf.astype(BF16), dy.astype(BF16)
        hb[SUBLANES:T + SUBLANES, :] = _dot(ufb, bb_ref[0])
        carry_f[...] = hs_ref[0, 0]
        ctr, cti = _ssm_scan(hb, SUBLANES, pw, carry_f, T, ns, reverse=False)
        hb[0:SUBLANES, 0:ns] = ctr
        hb[0:SUBLANES, ns:ns2] = cti
        ls[...] = _dot(dyb, cct_ref[0])
        _ssm_scan(ls, 0, pw, carry_b, T, ns, reverse=True)
        lv = ls[...]
        lb = lv.astype(BF16)
        dbb_ref[0] += _dot_tn(ufb, lb)
        dcc_ref[0] += _dot_tn(hb[SUBLANES:T + SUBLANES, :].astype(BF16), dyb)
        lr, li = lv[:, 0:ns], lv[:, ns:ns2]
        hpr, hpi = hb[0:T, 0:ns], hb[0:T, ns:ns2]
        dar = jnp.sum(lr * hpr + li * hpi, axis=0, keepdims=True)
        dai = jnp.sum(li * hpr - lr * hpi, axis=0, keepdims=True)
        da_ref[0, 0:1, 0:ns] += dar
        da_ref[0, 0:1, ns:ns2] += dai
        duf_ref[...] = _dot(lb, bbt_ref[0]) + d_ref[...] * dy
        dd_ref[...] += jnp.sum(dy * uf, axis=0, keepdims=True)

    rc = lambda c: NCH - 1 - c
    return _pcall(
        body, name="ssm_bwd",
        grid=(GB, NCH),
        in_specs=[pl.BlockSpec((T, cw), lambda g, c: (rc(c), g)),
                  pl.BlockSpec((T, cw), lambda g, c: (rc(c), g)),
                  pl.BlockSpec((1, cw, ns2), lambda g, c: (g, 0, 0)),
                  pl.BlockSpec((1, ns2, cw), lambda g, c: (g, 0, 0)),
                  pl.BlockSpec((1, ns2, cw), lambda g, c: (g, 0, 0)),
                  pl.BlockSpec((1, cw, ns2), lambda g, c: (g, 0, 0)),
                  pl.BlockSpec((1, 2, ns), lambda g, c: (g, 0, 0)),
                  pl.BlockSpec((1, cw), lambda g, c: (0, g)),
                  pl.BlockSpec((1, 1, SUBLANES, ns2), lambda g, c: (g, rc(c), 0, 0))],
        out_specs=[pl.BlockSpec((T, cw), lambda g, c: (rc(c), g)),
                   pl.BlockSpec((1, cw, ns2), lambda g, c: (g, 0, 0)),
                   pl.BlockSpec((1, ns2, cw), lambda g, c: (g, 0, 0)),
                   pl.BlockSpec((1, SUBLANES, ns2), lambda g, c: (g, 0, 0)),
                   pl.BlockSpec((1, cw), lambda g, c: (0, g))],
        out_shape=[jax.ShapeDtypeStruct((S, W), F32),
                   jax.ShapeDtypeStruct((GB, cw, ns2), F32),
                   jax.ShapeDtypeStruct((GB, ns2, cw), F32),
                   jax.ShapeDtypeStruct((GB, SUBLANES, ns2), F32),
                   jax.ShapeDtypeStruct((1, W), F32)],
        scratch_shapes=[pltpu.VMEM((T + SUBLANES, ns2), F32), pltpu.VMEM((T, ns2), F32), pltpu.VMEM((T, ns2), F32),
                        pltpu.VMEM((SUBLANES, ns2), F32), pltpu.VMEM((SUBLANES, ns2), F32)],
        compiler_params=_params("arbitrary", "arbitrary"),
    )(ufp, dyp, bb, bbt, cc, cct, lam_dt, drow, hstart)


def _ssm_disc_math(lr, li, logdt, br, bi):
    dt = jnp.exp(logdt)
    mag = jnp.exp(lr * dt)
    ar = mag * jnp.cos(li * dt)
    ai = mag * jnp.sin(li * dt)
    nr, ni = ar - 1.0, ai
    den = lr * lr + li * li
    cr = (nr * lr + ni * li) / den
    ci = (ni * lr - nr * li) / den
    return ar, ai, cr * br - ci * bi, cr * bi + ci * br


def _ssm_disc(lr, li, logdt, br, bi):
    C = br.shape[1]

    def fn(ins, ps):
        _, _, bbr, bbi = _ssm_disc_math(*ins)
        dt = jnp.exp(ins[2])
        return [ins[0] * dt, ins[1] * dt, bbr, bbi], []

    return _rowwise("ssm_disc", fn, [lr, li, logdt, br, bi], [], [(1, F32), (1, F32), (C, F32), (C, F32)], ts=512)


def _ssm_disc_bwd(lr, li, logdt, br, bi, dar, dai, dbbr, dbbi):
    C = br.shape[1]

    def fn(ins, ps):
        _, vjp = jax.vjp(_ssm_disc_math, *ins[:5])
        return list(vjp(tuple(ins[5:]))), []

    return _rowwise("ssm_disc_bwd", fn, [lr, li, logdt, br, bi, dar, dai, dbbr, dbbi], [],
                    [(1, F32), (1, F32), (1, F32), (C, F32), (C, F32)], ts=512)


def _block_diag(t):
    GB, g, a, b = t.shape
    eye = jnp.eye(g, dtype=t.dtype)
    return (t[:, :, :, None, :] * eye[None, :, None, :, None]).reshape(GB, g * a, g * b)


def _block_diag_take(t, g):
    GB, ga, gb_ = t.shape
    a, b = ga // g, gb_ // g
    eye = jnp.eye(g, dtype=t.dtype)
    return (t.reshape(GB, g, a, g, b) * eye[None, :, None, :, None]).sum(axis=3)


def _loss_head(h4, tgt, gf):
    D = h4.shape[1]

    def fn(ins, ps):
        x, t = ins
        xh, r = _xhat(x)
        err = xh * ps[0] - t
        dn = err * (1.0 / D)
        dxh = dn * ps[0]
        dx = r * (dxh - xh * jnp.mean(dxh * xh, axis=-1, keepdims=True))
        return [dx], [jnp.sum(err * err, axis=0, keepdims=True), jnp.sum(dn * xh, axis=0, keepdims=True)]

    return _rowwise("loss_head", fn, [h4, tgt], [gf], [(D, F32)], accs=[D, D])


def _gelu(x):
    return 0.5 * x * (1.0 + jnp.tanh(GELU_C * (x + GELU_K * x * x * x)))


def _gelu_grad(x):
    t = jnp.tanh(GELU_C * (x + GELU_K * x * x * x))
    return 0.5 * (1.0 + t) + 0.5 * x * (1.0 - t * t) * GELU_C * (1.0 + 3.0 * GELU_K * x * x)


def _mesh_pos():
    return lax.axis_index("x"), lax.axis_index("y"), lax.axis_index("c")


def _other_chips(x, y):
    return [(1 - x, y), (x, 1 - y), (1 - x, 1 - y)]


def _remote(src, dst, send, recv, dev):
    return pltpu.make_async_remote_copy(src_ref=src, dst_ref=dst, send_sem=send, recv_sem=recv,
                                        device_id=dev, device_id_type=MESH)


ANY = pl.BlockSpec(memory_space=pl.ANY)


COMM_BLOCK_BYTES = 3 << 19


def _place():
    x, y, c = _mesh_pos()
    return jnp.stack([c] + [2 * cx + cy for cx, cy in _other_chips(x, y)] + [2 * x + y]).astype(jnp.int32)


def _send_chips(name, srcs, specs, tr, nth, cw):
    hr = nth * tr
    n = len(srcs)

    def body(*refs):
        got_ref, send, recv = refs[1 + n:]
        t = pl.program_id(0)
        x, y, c = _mesh_pos()
        cps = []
        for j, chip in enumerate(_other_chips(x, y)):
            dst = got_ref.at[pl.ds(pl.multiple_of(j * hr + t * tr, 16), tr), :]
            cp = _remote(refs[1 + j % n], dst, send.at[j], recv.at[j], (*chip, c))
            cp.start()
            cps.append(cp)
        for cp in cps:
            cp.wait_send()

        @pl.when(t == nth - 1)
        def _():
            for j in range(3):
                r_ = got_ref.at[pl.ds(j * hr, hr), :]
                _remote(r_, r_, send.at[j], recv.at[j], (x, y, c)).wait_recv()

    return _pcall(
        body, name=name,
        grid_spec=pltpu.PrefetchScalarGridSpec(
            num_scalar_prefetch=1, grid=(nth,), in_specs=specs, out_specs=ANY,
            scratch_shapes=[pltpu.SemaphoreType.DMA((3,)), pltpu.SemaphoreType.DMA((3,))]),
        out_shape=jax.ShapeDtypeStruct((3 * hr, cw), srcs[0].dtype),
        compiler_params=_params("arbitrary"),
    )(_place(), *srcs)


def _ag_assemble(name, shard, stage, axis, tr, nth):
    R, cc = shard.shape
    hr = nth * tr
    full = (R, N_CHIPS * cc) if axis == 1 else (N_CHIPS * R, cc)

    def body(pl_ref, s0, s1, s2, h0, h1, out_ref, send, recv, lsem):
        t = pl.program_id(0)
        x, y, c = _mesh_pos()

        def region(s, half):
            if axis == 1:
                return out_ref.at[pl.ds(pl.multiple_of(half * hr + t * tr, 16), tr), pl.ds(pl.multiple_of(s * cc, LANES), cc)]
            return out_ref.at[pl.ds(pl.multiple_of(s * R + half * hr + t * tr, 16), tr), :]

        cps = []
        for j, src in enumerate((s0, s1, s2)):
            dst = region(pl_ref[1 + j], c)
            cps.append(_remote(src, dst, send.at[j], recv, (x, y, 1 - c)))
            cps.append(pltpu.make_async_copy(src, dst, lsem.at[j]))
        for half, src in enumerate((h0, h1)):
            cps.append(pltpu.make_async_copy(src, region(pl_ref[4], half), lsem.at[3 + half]))
        for cp in cps:
            cp.start()
        for k, cp in enumerate(cps):
            if k < 6 and k % 2 == 0:
                cp.wait_send()
            else:
                cp.wait()

        @pl.when(t == nth - 1)
        def _():
            r_ = out_ref.at[pl.ds(0, hr), pl.ds(0, 3 * cc)] if axis == 1 else out_ref.at[pl.ds(0, 3 * hr), :]
            _remote(r_, r_, send.at[0], recv, (x, y, c)).wait_recv()

    blk = lambda f: pl.BlockSpec((tr, cc), f)
    return _pcall(
        body, name=name,
        grid_spec=pltpu.PrefetchScalarGridSpec(
            num_scalar_prefetch=1, grid=(nth,),
            in_specs=[blk(lambda t, p, j=j: (j * nth + t, 0)) for j in range(3)]
            + [blk(lambda t, p, h=h: (h * nth + t, 0)) for h in range(2)],
            out_specs=ANY,
            scratch_shapes=[pltpu.SemaphoreType.DMA((3,)), pltpu.SemaphoreType.DMA, pltpu.SemaphoreType.DMA((5,))]),
        out_shape=jax.ShapeDtypeStruct(full, shard.dtype),
        compiler_params=_params("arbitrary"),
    )(_place(), stage, stage, stage, shard, shard)


def _comm_rows(hr, row_bytes):
    return _tile(hr, max(16, COMM_BLOCK_BYTES // row_bytes // 16 * 16), 16)


def _host_send(items):
    def copies(ins, outs, send, recv):
        x, y, c = _mesh_pos()
        cps = []
        for w, (_, kind, hr, cw) in enumerate(items):
            for j, (cx, cy) in enumerate(_other_chips(x, y)):
                s = 2 * cx + cy
                if kind == "half":
                    src = ins[w].at[pl.ds(pl.multiple_of(c * hr, 16), hr), :]
                elif kind == "cols":
                    src = ins[w].at[:, pl.ds(pl.multiple_of(s * cw, LANES), cw)]
                else:
                    src = ins[w].at[pl.ds(pl.multiple_of(s * hr, 16), hr), :]
                cps.append(_remote(src, outs[w].at[pl.ds(j * hr, hr), :], send.at[3 * w + j], recv.at[3 * w + j], (cx, cy, c)))
        return cps

    def start(ins, outs, send, recv):
        for cp in copies(ins, outs, send, recv):
            cp.start()

    def wait(ins, outs, send, recv):
        for cp in copies(ins, outs, send, recv):
            cp.wait()

    return _Host([a for a, _, _, _ in items], [jax.ShapeDtypeStruct((3 * hr, cw), a.dtype) for a, _, hr, cw in items],
                 3 * len(items), start, wait)


def _ag_send(name, sh):
    R, cc = sh.shape
    tr = _comm_rows(R // 2, cc * 2)
    nth = R // 2 // tr
    return _send_chips(name, [sh], [pl.BlockSpec((tr, cc), lambda t, p: (p[0] * nth + t, 0))], tr, nth, cc)


def _ag_finish(name, sh, stage, axis):
    R, cc = sh.shape
    tr = _comm_rows(R // 2, cc * 2)
    return _ag_assemble(name, sh, stage, axis, tr, R // 2 // tr)


def _pair_sum(name, g, tr, nblk, blk):
    cw = g.shape[1]
    c_arr = lax.axis_index("c").astype(jnp.int32).reshape(1)

    def body(c_ref, keep_ref, send_ref, out_ref, land, send, recv):
        i = pl.program_id(0)
        slot = lax.rem(i, 2)
        x, y, c = _mesh_pos()
        cp = _remote(send_ref, land.at[slot], send.at[slot], recv.at[slot], (x, y, 1 - c))
        cp.start()
        cp.wait_recv()
        out_ref[...] = (keep_ref[...].astype(F32) + land[slot].astype(F32)).astype(BF16)
        cp.wait_send()

    return _pcall(
        body, name=name,
        grid_spec=pltpu.PrefetchScalarGridSpec(
            num_scalar_prefetch=1, grid=(nblk,),
            in_specs=[pl.BlockSpec((tr, cw), lambda i, c_ref: (blk(i, c_ref[0]), 0)),
                      pl.BlockSpec((tr, cw), lambda i, c_ref: (blk(i, 1 - c_ref[0]), 0))],
            out_specs=pl.BlockSpec((tr, cw), lambda i, c_ref: (i, 0)),
            scratch_shapes=[pltpu.VMEM((2, tr, cw), BF16), pltpu.SemaphoreType.DMA((2,)), pltpu.SemaphoreType.DMA((2,))]),
        out_shape=jax.ShapeDtypeStruct((nblk * tr, cw), BF16),
        compiler_params=_params("arbitrary"),
    )(c_arr, g, g)


def _push_pair(name, src, tr, nblk, src_block):
    cw = src.shape[1]
    c_arr = lax.axis_index("c").astype(jnp.int32).reshape(1)

    def body(c_ref, src_ref, out_ref, send, recv):
        i = pl.program_id(0)
        x, y, c = _mesh_pos()
        cp = _remote(src_ref, out_ref.at[pl.ds(pl.multiple_of(i * tr, 16), tr), :], send, recv, (x, y, 1 - c))
        cp.start()
        cp.wait_send()

        @pl.when(i == nblk - 1)
        def _():
            _remote(out_ref, out_ref, send, recv, (x, y, c)).wait_recv()

    return _pcall(
        body, name=name,
        grid_spec=pltpu.PrefetchScalarGridSpec(
            num_scalar_prefetch=1, grid=(nblk,),
            in_specs=[pl.BlockSpec((tr, cw), lambda i, c_ref: (src_block(i, c_ref[0]), 0))],
            out_specs=ANY,
            scratch_shapes=[pltpu.SemaphoreType.DMA, pltpu.SemaphoreType.DMA]),
        out_shape=jax.ShapeDtypeStruct((nblk * tr, cw), src.dtype),
        compiler_params=_params("arbitrary"),
    )(c_arr, src)


def _sum_half(name, g, theirs, tr, nblk, src_block):
    cw = g.shape[1]
    c_arr = lax.axis_index("c").astype(jnp.int32).reshape(1)

    def body(c_ref, g_ref, t_ref, o_ref):
        o_ref[...] = (g_ref[...].astype(F32) + t_ref[...].astype(F32)).astype(BF16)

    return _pcall(
        body, name=name,
        grid_spec=pltpu.PrefetchScalarGridSpec(
            num_scalar_prefetch=1, grid=(nblk,),
            in_specs=[pl.BlockSpec((tr, cw), lambda i, c_ref: (src_block(i, c_ref[0]), 0)),
                      pl.BlockSpec((tr, cw), lambda i, c_ref: (i, 0))],
            out_specs=pl.BlockSpec((tr, cw), lambda i, c_ref: (i, 0))),
        out_shape=jax.ShapeDtypeStruct((nblk * tr, cw), BF16),
        compiler_params=_params("arbitrary"),
    )(c_arr, g, theirs)


def _sum_chips_swap(name, q, got, qspec, tr, nth, cw):
    def body(p_ref, q_ref, g0, g1, g2, out_ref, buf, send, recv, lsem):
        t = pl.program_id(0)
        x, y, c = _mesh_pos()
        buf[...] = q_ref[...].astype(F32) + g0[...].astype(F32) + g1[...].astype(F32) + g2[...].astype(F32)
        dst = out_ref.at[pl.ds(pl.multiple_of((p_ref[0] * nth + t) * tr, 16), tr), :]
        cp = _remote(buf, dst, send, recv, (x, y, 1 - c))
        lc = pltpu.make_async_copy(buf, dst, lsem)
        cp.start()
        lc.start()
        lc.wait()
        cp.wait_send()

        @pl.when(t == nth - 1)
        def _():
            theirs = out_ref.at[pl.ds(0, nth * tr), :]
            _remote(theirs, theirs, send, recv, (x, y, c)).wait_recv()

    return _pcall(
        body, name=name,
        grid_spec=pltpu.PrefetchScalarGridSpec(
            num_scalar_prefetch=1, grid=(nth,),
            in_specs=[qspec] + [pl.BlockSpec((tr, cw), lambda t, p, j=j: (j * nth + t, 0)) for j in range(3)],
            out_specs=ANY,
            scratch_shapes=[pltpu.VMEM((tr, cw), F32), pltpu.SemaphoreType.DMA, pltpu.SemaphoreType.DMA,
                            pltpu.SemaphoreType.DMA]),
        out_shape=jax.ShapeDtypeStruct((2 * nth * tr, cw), F32),
        compiler_params=_params("arbitrary"),
    )(_place(), q, got, got, got)


def _rs_geom(g, axis):
    rows, gw = g.shape
    return (rows // 2, gw // N_CHIPS) if axis == 1 else (rows // N_CHIPS // 2, gw)


def _rs_pair_sum(tag, g, axis):
    hr, _ = _rs_geom(g, axis)
    tr = _comm_rows(hr, g.shape[1] * 2)
    nth = hr // tr
    if axis == 1:
        nblk, blk = nth, (lambda i, half: half * nth + i)
    else:
        nblk, blk = N_CHIPS * nth, (lambda i, half: (i // nth) * (2 * nth) + half * nth + i % nth)
    return _pair_sum("rs_pair_" + tag, g, tr, nblk, blk)


def _rs_part(q, axis, hr, cw, tr):
    nth = hr // tr
    if axis == 1:
        return lambda k: pl.BlockSpec((tr, cw), lambda t, p: (t, p[k]))
    return lambda k: pl.BlockSpec((tr, cw), lambda t, p: (p[k] * nth + t, 0))


def _rs_send(tag, q, axis, hr, cw):
    tr = _comm_rows(hr, cw * 2)
    part = _rs_part(q, axis, hr, cw, tr)
    return _send_chips("rs_send_" + tag, [q, q, q], [part(1), part(2), part(3)], tr, hr // tr, cw)


def _rs_finish(tag, q, got, axis, hr, cw):
    tr = _comm_rows(hr, cw * 4)
    return _sum_chips_swap("rs_sum_swap_" + tag, q, got, _rs_part(q, axis, hr, cw, tr)(4), tr, hr // tr, cw)


class _Exchange:
    AG_PLAN = {"ffn1_up": ("ffn1_w_down", "ffn2_w_gate"), "ffn1_down": ("ffn2_w_up", "w_in"),
               "w_in_qkv": ("ssm_w_glu", "w_out", "ple_w_gate", "ple_w_proj"), "attn_fwd_d1": ("ffn2_w_down",)}
    RS_PLAN = {"ffn1_dact": ("ffn2_w_gate", "ple_w_gate", "ple_w_proj"), "ffn1_dwd": ("ffn2_w_up",),
               "ffn1_dwgu": ("ffn2_w_down", "w_in", "ssm_w_glu", "w_out", "ffn1_w_down"),
               "ffn1_dn": ("ffn1_w_gate", "ffn1_w_up")}

    def __init__(self, shards, axes):
        self.shards, self.axes = shards, axes
        self.stage, self.full, self.q, self.geom, self.got = {}, {}, {}, {}, {}

    def ag_host(self, kernel):
        item = lambda k: (self.shards[k], "half", self.shards[k].shape[0] // 2, self.shards[k].shape[1])
        return _host_send([item(k) for k in self.AG_PLAN[kernel]])

    def ag_done(self, kernel, stages):
        self.stage.update(zip(self.AG_PLAN[kernel], stages))

    def weight(self, k):
        if k not in self.full:
            stage = self.stage[k] if k in self.stage else _ag_send("ag_send_" + k, self.shards[k])
            self.full[k] = _ag_finish("ag_asm_" + k, self.shards[k], stage, self.axes[k])
        return self.full[k]

    def grad(self, k, g):
        self.q[k], self.geom[k] = _rs_pair_sum(k, g, self.axes[k]), _rs_geom(g, self.axes[k])

    def rs_host(self, kernel):
        item = lambda k: (self.q[k], "cols" if self.axes[k] == 1 else "rows") + self.geom[k]
        return _host_send([item(k) for k in self.RS_PLAN[kernel]])

    def rs_done(self, kernel, gots):
        self.got.update(zip(self.RS_PLAN[kernel], gots))

    def finish(self):
        return {k: _rs_finish(k, q, self.got[k] if k in self.got else _rs_send(k, q, self.axes[k], *self.geom[k]),
                              self.axes[k], *self.geom[k]) for k, q in self.q.items()}


def _all_reduce_small(v):
    n = v.shape[0]
    h = n // 2

    def body(v_ref, out_ref, pair_in, chips_in, send, recv):
        x, y, c = _mesh_pos()
        me, sib, my_chip = (x, y, c), (x, y, 1 - c), 2 * x + y
        mine = pl.ds(pl.multiple_of(c * h, SUBLANES), h)
        other = pl.ds(pl.multiple_of((1 - c) * h, SUBLANES), h)
        pair = _remote(v_ref.at[other], pair_in, send.at[0], recv.at[0], sib)
        pair.start()
        pair.wait()
        chips_in[my_chip] = v_ref[mine, :] + pair_in[...]
        cps = []
        for j, (cx, cy) in enumerate(_other_chips(x, y)):
            cp = _remote(chips_in.at[my_chip], chips_in.at[my_chip], send.at[1 + j], recv.at[1 + j], (cx, cy, c))
            cp.start()
            cps.append(cp)
        for j, (cx, cy) in enumerate(_other_chips(x, y)):
            slot = chips_in.at[2 * cx + cy]
            _remote(slot, slot, send.at[1 + j], recv.at[1 + j], me).wait_recv()
        out_ref[mine, :] = (chips_in[0] + chips_in[1]) + (chips_in[2] + chips_in[3])
        for cp in cps:
            cp.wait_send()
        swap = _remote(out_ref.at[mine, :], out_ref.at[mine, :], send.at[4], recv.at[4], sib)
        swap.start()
        _remote(out_ref.at[other, :], out_ref.at[other, :], send.at[4], recv.at[4], me).wait_recv()
        swap.wait_send()

    return _pcall(
        body, name="ar_small",
        in_specs=[pl.BlockSpec(memory_space=pltpu.VMEM)], out_specs=pl.BlockSpec(memory_space=pltpu.VMEM),
        out_shape=jax.ShapeDtypeStruct((n, LANES), F32),
        scratch_shapes=[pltpu.VMEM((h, LANES), F32), pltpu.VMEM((N_CHIPS, h, LANES), F32),
                        pltpu.SemaphoreType.DMA((5,)), pltpu.SemaphoreType.DMA((5,))],
        compiler_params=pltpu.CompilerParams(vmem_limit_bytes=V7X_VMEM_LIMIT_BYTES),
    )(v)


def _adamw(name, w, g, m, v):
    R, Cc = w.shape
    tr = _tile(R, max(8, (1 << 19) // Cc // 8 * 8), 8)
    c1 = 1.0 - ADAM_B1 ** ADAM_STEP
    c2 = 1.0 - ADAM_B2 ** ADAM_STEP

    def body(w_ref, g_ref, m_ref, v_ref, d_ref, nm_ref, nv_ref):
        g_ = g_ref[...]
        nm = ADAM_B1 * m_ref[...] + (1.0 - ADAM_B1) * g_
        nv = ADAM_B2 * v_ref[...] + (1.0 - ADAM_B2) * (g_ * g_)
        d_ref[...] = -ADAM_LR * ((nm / c1) / (jnp.sqrt(nv / c2) + ADAM_EPS) + ADAM_WD * w_ref[...])
        nm_ref[...] = nm
        nv_ref[...] = nv

    spec = pl.BlockSpec((tr, Cc), lambda i: (i, 0))
    return _pcall(
        body, name=name, grid=(R // tr,),
        in_specs=[spec] * 4, out_specs=[spec] * 3,
        out_shape=[jax.ShapeDtypeStruct((R, Cc), F32)] * 3,
        compiler_params=_params("parallel"),
    )(w, g, m, v)


def _pack(arrs, rows):
    flat = jnp.concatenate([a.reshape(-1) for a in arrs])
    return jnp.pad(flat, (0, rows * LANES - flat.shape[0])).reshape(rows, LANES)


def _unpack(packed, like):
    flat, out, o = packed.reshape(-1), [], 0
    for a in like:
        out.append(flat[o:o + a.size].reshape(a.shape))
        o += a.size
    return out


BIG = (
    ("ffn1_w_gate", 1), ("ffn1_w_up", 1), ("ffn1_w_down", 0), ("w_in", 1), ("ssm_w_glu", 0), ("w_out", 0),
    ("ffn2_w_gate", 1), ("ffn2_w_up", 1), ("ffn2_w_down", 0), ("ple_w_gate", 0), ("ple_w_proj", 1),
)
SMALL = ("ffn1_norm", "mix_norm", "attn_out_norm", "ssm_lambda_re", "ssm_lambda_im", "ssm_log_dt", "ssm_b_re", "ssm_b_im",
         "ssm_c_re", "ssm_c_im", "ssm_d", "ssm_b_glu", "ssm_out_norm", "ffn2_norm", "ple_norm", "final_norm")
WEIGHTS = ("ffn1_norm", "ffn1_w_gate", "ffn1_w_up", "ffn1_w_down", "mix_norm", "w_in", "attn_out_norm", "ssm_lambda_re",
           "ssm_lambda_im", "ssm_log_dt", "ssm_b_re", "ssm_b_im", "ssm_c_re", "ssm_c_im", "ssm_d", "ssm_w_glu", "ssm_b_glu",
           "ssm_out_norm", "w_out", "ffn2_norm", "ffn2_w_gate", "ffn2_w_up", "ffn2_w_down", "ple_norm", "ple_w_gate",
           "ple_w_proj", "final_norm")


def _pad_to(a, axis, n):
    pad = [(0, 0), (0, 0)]
    pad[axis] = (0, n - a.shape[axis])
    return jnp.pad(a, pad)


def _local_step(x, p, tgt, w, ex):
    S, D = x.shape
    A = w["attn_out_norm"].shape[-1]
    W = w["ssm_d"].shape[-1]
    G, P = w["ssm_lambda_re"].shape[-2:]
    C = w["ssm_b_re"].shape[-1]
    GB = G // SSM_BLOCK_GROUPS
    T = min(1024, S)
    row = lambda name: w[name].reshape(1, -1)
    gs = {}

    h1, ffn1_saved = _ffn_fwd("ffn1", x, row("ffn1_norm"), ex)
    n2 = _rms_fwd("mix_norm", h1, row("mix_norm"))
    w_in = ex.weight("w_in")
    n2p = _to_attn_order(n2)
    host, done = _carried(ex, "ag", "w_in_qkv")
    qkv, *outs = _mm("w_in_qkv", [n2p], [w_in[:, :3 * A]], [F32], tm=1024, tn=1024, host=host)
    done(outs)
    (s_in,) = _mm("w_in_ssm", [n2], [w_in[:, 3 * A:]], [F32], tm=1024, tn=1024)
    ya, lse = _attn_fwd(qkv, ex)

    col = lambda name: w[name].reshape(G * P, 1)
    logdt_x = jnp.repeat(w["ssm_log_dt"].reshape(G), P).reshape(G * P, 1)
    b_re, b_im = w["ssm_b_re"].reshape(G * P, C), w["ssm_b_im"].reshape(G * P, C)
    lrdt, lidt, bbr, bbi = _ssm_disc(col("ssm_lambda_re"), col("ssm_lambda_im"), logdt_x, b_re, b_im)
    gsz = SSM_BLOCK_GROUPS
    to_bb = lambda t: _block_diag(t.reshape(GB, gsz, P, C).transpose(0, 1, 3, 2))
    bb = jnp.concatenate([to_bb(bbr), to_bb(bbi)], axis=2).astype(BF16)
    to_cc = lambda t: _block_diag(t.reshape(GB, gsz, C, P).transpose(0, 1, 3, 2))
    cc = jnp.concatenate([to_cc(w["ssm_c_re"]), -to_cc(w["ssm_c_im"])], axis=1).astype(BF16)
    lam_dt = jnp.stack([lrdt.reshape(GB, gsz * P), lidt.reshape(GB, gsz * P)], axis=1)
    ufp = _ssm_perm(s_in, T)
    ypre, hstart = _ssm_fwd(ufp, bb, cc, lam_dt, row("ssm_d"), T)

    def glu_in(ins, ps):
        yg = _gelu(ins[0])
        return [yg, yg], []

    yg, ygb = _rowwise("ssm_gelu", glu_in, [ypre], [], [(W, F32), (W, BF16)])
    w_glu = ex.weight("ssm_w_glu")

    def glu_out(accs, ex):
        gl = accs[0] + ex[1]
        return [ex[0] * _sigmoid(gl), gl]

    ybp, gl = _mm("ssm_glu", [ygb], [w_glu], [F32, F32], extras=[(yg, "mn"), (row("ssm_b_glu"), "n")],
                  epilogue=glu_out, tm=1024, tn=1024)
    yb = _ssm_unperm(ybp, T)
    na = _from_attn_order(_rms_fwd("attn_out_norm", ya, row("attn_out_norm")))
    nb = _rms_fwd("ssm_out_norm", yb, row("ssm_out_norm"))
    w_out = ex.weight("w_out")
    (h2,) = _mm("w_out", [na, nb], [w_out[:A], w_out[A:]], [F32], pairs=((0, 0, 0), (1, 1, 0)), extras=[(h1, "mn")],
                epilogue=lambda accs, ex: [ex[0] + accs[0]], tm=1024, tn=1024)
    h3, ffn2_saved = _ffn_fwd("ffn2", h2, row("ffn2_norm"), ex)
    n4 = _rms_fwd("ple_norm", h3, row("ple_norm"))
    (pe,) = _mm("ple_proj", [p], [ex.weight("ple_w_proj")], [F32], tm=1024, tn=1024)

    def ple_out(accs, ex):
        gate = _sigmoid(accs[0])
        return [ex[1] + gate * ex[0], gate]

    h4, gate = _mm("ple_gate", [n4], [ex.weight("ple_w_gate")], [F32, F32], extras=[(pe, "mn"), (h3, "mn")],
                   epilogue=ple_out, tm=1024, tn=1024)

    dh4, err2, gs["final_norm"] = _loss_head(h4, tgt, row("final_norm"))
    loss = (0.5 / D) * jnp.sum(err2)

    def ple_bwd(ins, ps):
        dh, gt, pe_ = ins
        return [dh * gt, dh * pe_ * gt * (1.0 - gt)], []

    dpe, dpg = _rowwise("ple_bwd", ple_bwd, [dh4, gate, pe], [], [(D, BF16), (D, BF16)])
    (d_ple_proj,) = _mm("ple_dproj", [p], [dpe], [BF16], ta=True, tm=256, tn=2048, tk=1024)
    (d_ple_gate,) = _mm("ple_dgate", [n4], [dpg], [BF16], ta=True, tm=1024, tn=1024, tk=2048)
    (dn4,) = _mm("ple_dn", [dpg], [ex.weight("ple_w_gate")], [F32], tb=True, tm=1024, tn=1024)
    (dh3, dh3b), gs["ple_norm"] = _rms_bwd("ple_dnorm", dn4, h3, row("ple_norm"), dres=dh4, copy_scale=0.5)
    (dh2, dh2b), gs["ffn2_norm"] = _ffn_bwd("ffn2", dh3, dh3b, h2, row("ffn2_norm"), ex, ffn2_saved, copy_scale=1.0)
    (dna,) = _mm("w_out_dna", [_to_attn_order(dh2b)], [w_out[:A]], [F32], tb=True, tm=1024, tn=1024)
    (dnb,) = _mm("w_out_dnb", [dh2b], [w_out[A:]], [F32], tb=True, tm=1024, tn=1024)
    (d_wout_a,) = _mm("w_out_dwa", [na], [dh2b], [BF16], ta=True, tm=1024, tn=1024, tk=2048)
    (d_wout_b,) = _mm("w_out_dwb", [nb], [dh2b], [BF16], ta=True, tm=1024, tn=1024, tk=2048)
    d_w_out = jnp.concatenate([d_wout_a, d_wout_b], axis=0)
    (dya,), gs["attn_out_norm"] = _rms_bwd("attn_out_dnorm", dna, ya, row("attn_out_norm"))
    (dyb,), gs["ssm_out_norm"] = _rms_bwd("ssm_out_dnorm", dnb, yb, row("ssm_out_norm"))

    dybp = _ssm_perm(dyb, T)

    def glu_bwd(ins, ps):
        dy, yg_, gl_ = ins
        sg = _sigmoid(gl_)
        dgl = dy * yg_ * sg * (1.0 - sg)
        return [dgl, dy * sg], [jnp.sum(dgl, axis=0, keepdims=True)]

    dgl, dyg_direct, gs["ssm_b_glu"] = _rowwise("ssm_glu_bwd", glu_bwd, [dybp, yg, gl], [], [(W, BF16), (W, F32)], accs=[W])
    (d_w_glu,) = _mm("ssm_dwglu", [ygb], [dgl], [BF16], ta=True, tm=1024, tn=1024, tk=2048)
    (dypre,) = _mm("ssm_dyg", [dgl], [w_glu], [F32], tb=True, extras=[(dyg_direct, "mn"), (ypre, "mn")],
                   epilogue=lambda accs, ex: [(accs[0] + ex[0]) * _gelu_grad(ex[1])], tm=1024, tn=1024)
    dufp, dbb, dcc, da, gs["ssm_d"] = _ssm_bwd(ufp, dypre, bb, bb.transpose(0, 2, 1), cc, cc.transpose(0, 2, 1),
                                               lam_dt, row("ssm_d"), hstart, T)
    ns = gsz * P
    from_bb = lambda t: _block_diag_take(t, gsz).transpose(0, 1, 3, 2).reshape(G * P, C)
    from_cc = lambda t: _block_diag_take(t, gsz).transpose(0, 1, 3, 2).reshape(w["ssm_c_re"].shape)
    gs["ssm_c_re"], gs["ssm_c_im"] = from_cc(dcc[:, :ns]), -from_cc(dcc[:, ns:])
    da = da.sum(axis=1)
    dar, dai = da[:, :ns].reshape(G * P, 1), da[:, ns:].reshape(G * P, 1)
    dlr, dli, dlogdt, dbr, dbi = _ssm_disc_bwd(col("ssm_lambda_re"), col("ssm_lambda_im"), logdt_x, b_re, b_im,
                                               dar, dai, from_bb(dbb[:, :, :ns]), from_bb(dbb[:, :, ns:]))
    gs["ssm_lambda_re"], gs["ssm_lambda_im"] = dlr.reshape(w["ssm_lambda_re"].shape), dli.reshape(w["ssm_lambda_im"].shape)
    gs["ssm_log_dt"] = dlogdt.reshape(G, P).sum(axis=1).reshape(w["ssm_log_dt"].shape)
    gs["ssm_b_re"], gs["ssm_b_im"] = dbr.reshape(w["ssm_b_re"].shape), dbi.reshape(w["ssm_b_im"].shape)
    ds_in = _ssm_unperm(dufp, T)

    dq, dk, dv = _attn_bwd(qkv, dya, ya, lse)
    dqkv = jnp.concatenate([dq, dk, dv], axis=1).astype(BF16)
    (d_w_qkv,) = _mm("w_in_dw_qkv", [n2p], [dqkv], [BF16], ta=True, tm=1024, tn=1024, tk=2048)
    (d_w_s,) = _mm("w_in_dw_ssm", [n2], [ds_in], [BF16], ta=True, tm=1024, tn=1024, tk=2048)
    d_w_in = jnp.concatenate([d_w_qkv, d_w_s], axis=1)
    dz = jnp.concatenate([_from_attn_order(dqkv), ds_in.astype(BF16)], axis=1)
    (dn2,) = _mm("w_in_dn", [dz], [w_in], [F32], tb=True, tm=1024, tn=1024)
    (dh1, dh1b), gs["mix_norm"] = _rms_bwd("mix_dnorm", dn2, h1, row("mix_norm"), dres=dh2, copy_scale=0.5)
    for k, g in (("ple_w_gate", d_ple_gate), ("ple_w_proj", d_ple_proj), ("w_out", d_w_out), ("ssm_w_glu", d_w_glu),
                 ("w_in", d_w_in)):
        ex.grad(k, g)
    (dx,), gs["ffn1_norm"] = _ffn_bwd("ffn1", dh1, dh1b, x, row("ffn1_norm"), ex, ffn1_saved, copy_scale=None)
    small = {k: gs[k].reshape(w[k].shape) for k in SMALL}
    return loss, dx, ex.finish(), small


def kernel(x, p, ffn1_norm, ffn1_w_gate, ffn1_w_up, ffn1_w_down, mix_norm, w_in, attn_out_norm, ssm_lambda_re, ssm_lambda_im, ssm_log_dt, ssm_b_re, ssm_b_im, ssm_c_re, ssm_c_im, ssm_d, ssm_w_glu, ssm_b_glu, ssm_out_norm, w_out, ffn2_norm, ffn2_w_gate, ffn2_w_up, ffn2_w_down, ple_norm, ple_w_gate, ple_w_proj, final_norm, loss_target, m_ffn1_norm, m_ffn1_w_gate, m_ffn1_w_up, m_ffn1_w_down, m_mix_norm, m_w_in, m_attn_out_norm, m_ssm_lambda_re, m_ssm_lambda_im, m_ssm_log_dt, m_ssm_b_re, m_ssm_b_im, m_ssm_c_re, m_ssm_c_im, m_ssm_d, m_ssm_w_glu, m_ssm_b_glu, m_ssm_out_norm, m_w_out, m_ffn2_norm, m_ffn2_w_gate, m_ffn2_w_up, m_ffn2_w_down, m_ple_norm, m_ple_w_gate, m_ple_w_proj, m_final_norm, v_ffn1_norm, v_ffn1_w_gate, v_ffn1_w_up, v_ffn1_w_down, v_mix_norm, v_w_in, v_attn_out_norm, v_ssm_lambda_re, v_ssm_lambda_im, v_ssm_log_dt, v_ssm_b_re, v_ssm_b_im, v_ssm_c_re, v_ssm_c_im, v_ssm_d, v_ssm_w_glu, v_ssm_b_glu, v_ssm_out_norm, v_w_out, v_ffn2_norm, v_ffn2_w_gate, v_ffn2_w_up, v_ffn2_w_down, v_ple_norm, v_ple_w_gate, v_ple_w_proj, v_final_norm):
    args = locals()
    w = {k: args[k] for k in WEIGHTS}
    m = {k: args["m_" + k] for k in WEIGHTS}
    v = {k: args["v_" + k] for k in WEIGHTS}
    w2 = {k: w[k].reshape(w[k].shape[-2:]) for k, _ in BIG}

    axes = [ax for _, ax in BIG]
    padded = {k: -(-w2[k].shape[ax] // LANES) * LANES for k, ax in BIG}
    shards = [_pad_to(w2[k].astype(BF16), ax, padded[k]) for k, ax in BIG]
    ex = _Exchange(dict(zip([k for k, _ in BIG], shards)), dict(BIG))
    loss_local, dx, summed, gsmall = _local_step(x[0], p[0, 0], loss_target[0], w, ex)
    loss = lax.psum(loss_local, MESH_AXES)
    n_small = sum(w[k].size for k in SMALL)
    rows = -(-n_small // (2 * SUBLANES * LANES)) * 2 * SUBLANES
    gs_sum = _all_reduce_small(_pack([gsmall[k] for k in SMALL], rows))

    grads, delta, new_m, new_v = {}, {}, {}, {}
    for k, ax in BIG:
        gfull = summed[k]
        g2 = lax.slice_in_dim(gfull, 0, w2[k].shape[ax], axis=ax)
        d2, nm2, nv2 = _adamw("adamw_" + k, w2[k], g2, m[k].reshape(w2[k].shape), v[k].reshape(w2[k].shape))
        grads[k], delta[k], new_m[k], new_v[k] = (t.reshape(w[k].shape) for t in (g2, d2, nm2, nv2))
    small_like = [w[k] for k in SMALL]
    ds, nms, nvs = _adamw("adamw_small", _pack(small_like, rows), gs_sum, _pack([m[k] for k in SMALL], rows),
                          _pack([v[k] for k in SMALL], rows))
    for k, g_, d_, nm_, nv_ in zip(SMALL, _unpack(gs_sum, small_like), _unpack(ds, small_like),
                                   _unpack(nms, small_like), _unpack(nvs, small_like)):
        grads[k], delta[k], new_m[k], new_v[k] = g_, d_, nm_, nv_

    return (loss, dx[None], *[grads[k] for k in WEIGHTS], *[delta[k] for k in WEIGHTS],
            *[new_m[k] for k in WEIGHTS], *[new_v[k] for k in WEIGHTS])
```

```python
import functools
import math

import jax
import jax.numpy as jnp
from jax import lax
from jax.experimental import pallas as pl
from jax.experimental.pallas import tpu as pltpu

F32 = jnp.float32
BF16 = jnp.bfloat16
MESH = pl.DeviceIdType.MESH
MESH_AXES = ("x", "y", "c")
N_CHIPS = 4
N_DEV = 8

V7X_VMEM_LIMIT_BYTES = 56 << 20
LANES = 128
SUBLANES = 8

HEAD_DIM = 64
SWA_BLOCK = 128
DILATIONS = (1, 4, 16)
SSM_BLOCK_GROUPS = 8
NORM_EPS = 1e-6
MASK_VALUE = -1e30

ADAM_LR = 0.001
ADAM_B1 = 0.9
ADAM_B2 = 0.999
ADAM_EPS = 1e-08
ADAM_WD = 0.01
ADAM_STEP = 10

GELU_C = math.sqrt(2.0 / math.pi)
GELU_K = 0.044715


def _pcall(body, **kw):
    return pl.pallas_call(body, **kw)


def _params(*sem):
    return pltpu.CompilerParams(dimension_semantics=sem, vmem_limit_bytes=V7X_VMEM_LIMIT_BYTES)


def _tile(n, target, align):
    best = None
    for t in range(align, min(n, target) + 1, align):
        if n % t == 0:
            best = t
    return n if best is None else best


def _sigmoid(x):
    return 0.5 * jnp.tanh(0.5 * x) + 0.5


class _Host:
    def __init__(self, ins, out_shapes, n_sem, start, wait):
        self.ins, self.out_shapes, self.n_sem, self.start, self.wait = ins, out_shapes, n_sem, start, wait


def _mm(name, lhs, rhs, outs, pairs=((0, 0, 0),), epilogue=None, extras=(), ta=False, tb=False,
        tm=1024, tn=512, tk=2048, host=None):
    nl, nr, ne, no = len(lhs), len(rhs), len(extras), len(outs)
    nhi, nho = (len(host.ins), len(host.out_shapes)) if host else (0, 0)
    n_acc = 1 + max(p[2] for p in pairs)
    (K, M) = lhs[0].shape if ta else lhs[0].shape[::-1]
    (N, K2) = rhs[0].shape if tb else rhs[0].shape[::-1]
    assert K == K2, (name, lhs[0].shape, rhs[0].shape)
    tm, tn, tk = _tile(M, tm, LANES), _tile(N, tn, LANES), _tile(K, tk, LANES)
    ni, nj, nk = M // tm, N // tn, K // tk
    n_scr = n_acc if nk > 1 else 0
    if epilogue is None:
        epilogue = lambda accs, ex: accs
    dn = (((0 if ta else 1,), (1 if tb else 0,)), ((), ()))

    def body(*refs):
        refs = list(refs)
        take = lambda n: [refs.pop(0) for _ in range(n)]
        l, r, e, hin, o, hout, acc = take(nl), take(nr), take(ne), take(nhi), take(no), take(nho), take(n_scr)
        i, j, k = pl.program_id(0), pl.program_id(1), pl.program_id(2)
        if host:
            @pl.when((i == 0) & (j == 0) & (k == 0))
            def _():
                host.start(hin, hout, *refs)

        parts = [None] * n_acc
        for li, ri, ai in pairs:
            d = lax.dot_general(l[li][...].astype(BF16), r[ri][...].astype(BF16), dn,
                                preferred_element_type=F32)
            parts[ai] = d if parts[ai] is None else parts[ai] + d

        def finish(accs):
            res = epilogue(accs, [x[...] for x in e])
            for ref, val in zip(o, res):
                ref[...] = val.astype(ref.dtype)

        if nk == 1:
            finish(parts)
        else:
            @pl.when(k == 0)
            def _():
                for ai in range(n_acc):
                    acc[ai][...] = parts[ai]

            @pl.when(k > 0)
            def _():
                for ai in range(n_acc):
                    acc[ai][...] += parts[ai]

            @pl.when(k == nk - 1)
            def _():
                finish([a[...] for a in acc])

        if host:
            @pl.when((i == ni - 1) & (j == nj - 1) & (k == nk - 1))
            def _():
                host.wait(hin, hout, *refs)

    lspec = pl.BlockSpec((tk, tm), lambda i, j, k: (k, i)) if ta else pl.BlockSpec((tm, tk), lambda i, j, k: (i, k))
    rspec = pl.BlockSpec((tn, tk), lambda i, j, k: (j, k)) if tb else pl.BlockSpec((tk, tn), lambda i, j, k: (k, j))
    especs = []
    for arr, kind in extras:
        if kind == "mn":
            especs.append(pl.BlockSpec((tm, tn), lambda i, j, k: (i, j)))
        elif kind == "n":
            especs.append(pl.BlockSpec((1, tn), lambda i, j, k: (0, j)))
        else:
            especs.append(pl.BlockSpec((tm, 1), lambda i, j, k: (i, 0)))
    any_spec = pl.BlockSpec(memory_space=pl.ANY)
    sems = [pltpu.SemaphoreType.DMA((host.n_sem,)), pltpu.SemaphoreType.DMA((host.n_sem,))] if host else []
    res = _pcall(
        body, name=name,
        grid=(ni, nj, nk),
        in_specs=[lspec] * nl + [rspec] * nr + especs + [any_spec] * nhi,
        out_specs=[pl.BlockSpec((tm, tn), lambda i, j, k: (i, j))] * no + [any_spec] * nho,
        out_shape=[jax.ShapeDtypeStruct((M, N), dt) for dt in outs] + (list(host.out_shapes) if host else []),
        scratch_shapes=[pltpu.VMEM((tm, tn), F32)] * n_scr + sems,
        compiler_params=_params(*(("arbitrary",) * 3 if host else ("parallel", "parallel", "arbitrary"))),
    )(*lhs, *rhs, *[a for a, _ in extras], *(host.ins if host else []))
    return res


ROWWISE_BLOCK_BYTES = 12 << 20


def _rowwise(name, fn, ins, params, outs, accs=(), ts=None):
    S = ins[0].shape[0]
    if ts is None:
        row_bytes = sum(a.shape[1] * a.dtype.itemsize for a in ins) + sum(w * jnp.dtype(dt).itemsize for w, dt in outs)
        ts = 512 if 512 * row_bytes <= ROWWISE_BLOCK_BYTES else 256
    ts = _tile(S, ts, 16)
    ni, npar, no, na = len(ins), len(params), len(outs), len(accs)

    def body(*refs):
        i_refs, p_refs = refs[:ni], refs[ni:ni + npar]
        o_refs = refs[ni + npar:ni + npar + no]
        a_refs = refs[ni + npar + no:]
        res_o, res_a = fn([r[...] for r in i_refs], [r[...] for r in p_refs])
        for ref, val in zip(o_refs, res_o):
            ref[...] = val.astype(ref.dtype)
        if na:
            @pl.when(pl.program_id(0) == 0)
            def _():
                for ref in a_refs:
                    ref[...] = jnp.zeros(ref.shape, F32)

            for ref, val in zip(a_refs, res_a):
                ref[...] += val

    res = _pcall(
        body, name=name,
        grid=(S // ts,),
        in_specs=[pl.BlockSpec((ts, a.shape[1]), lambda i: (i, 0)) for a in ins]
        + [pl.BlockSpec(p.shape, lambda i: (0, 0)) for p in params],
        out_specs=[pl.BlockSpec((ts, w), lambda i: (i, 0)) for w, _ in outs]
        + [pl.BlockSpec((1, w), lambda i: (0, 0)) for w in accs],
        out_shape=[jax.ShapeDtypeStruct((S, w), dt) for w, dt in outs]
        + [jax.ShapeDtypeStruct((1, w), F32) for w in accs],
        compiler_params=_params("arbitrary"),
    )(*ins, *params)
    return res


def _xhat(x):
    r = lax.rsqrt(jnp.mean(x * x, axis=-1, keepdims=True) + NORM_EPS)
    return x * r, r


def _rms_fwd(name, x, g):
    def fn(ins, ps):
        xh, _ = _xhat(ins[0])
        return [xh * ps[0]], []

    return _rowwise(name, fn, [x], [g], [(x.shape[1], BF16)])[0]


def _rms_bwd(name, dn, x, g, dres=None, copy_scale=None):
    w = x.shape[1]

    def fn(ins, ps):
        dn_, x_ = ins[0], ins[1]
        xh, r = _xhat(x_)
        dxh = dn_ * ps[0]
        dx = r * (dxh - xh * jnp.mean(dxh * xh, axis=-1, keepdims=True))
        if dres is not None:
            dx = dx + ins[2]
        o = [dx] + ([dx * copy_scale] if copy_scale is not None else [])
        return o, [jnp.sum(dn_ * xh, axis=0, keepdims=True)]

    ins = [dn, x] + ([dres] if dres is not None else [])
    outs = [(w, F32)] + ([(w, BF16)] if copy_scale is not None else [])
    res = _rowwise(name, fn, ins, [g], outs, accs=[w])
    return res[:-1], res[-1]


def _swiglu_epilogue(accs, ex):
    g, u = accs
    sg = _sigmoid(g)
    s = g * sg
    return [u * (sg + s * (1.0 - sg)), s, s * u]


def _dswiglu_epilogue(accs, ex):
    da = accs[0]
    return [da * ex[0].astype(F32), da * ex[1].astype(F32)]


def _carried(ex, kind, kernel):
    if kernel not in (ex.AG_PLAN if kind == "ag" else ex.RS_PLAN):
        return None, lambda outs: None
    if kind == "ag":
        return ex.ag_host(kernel), lambda outs: ex.ag_done(kernel, outs)
    return ex.rs_host(kernel), lambda outs: ex.rs_done(kernel, outs)


def _ffn_fwd(tag, h, gnorm, ex):
    n = _rms_fwd(tag + "_norm", h, gnorm)
    host, done = _carried(ex, "ag", tag + "_up")
    g, u, a, *outs = _mm(tag + "_up", [n], [ex.weight(tag + "_w_gate"), ex.weight(tag + "_w_up")], [BF16, BF16, BF16],
                         pairs=((0, 0, 0), (0, 1, 1)), epilogue=_swiglu_epilogue, tm=1024, tn=512, host=host)
    done(outs)
    host, done = _carried(ex, "ag", tag + "_down")
    hout, *outs = _mm(tag + "_down", [a], [ex.weight(tag + "_w_down")], [F32], extras=[(h, "mn")],
                      epilogue=lambda accs, ex_: [ex_[0] + 0.5 * accs[0]], tm=512, tn=1024, tk=8192, host=host)
    done(outs)
    return hout, (n, g, u, a)


def _ffn_bwd(tag, dh, dhb_half, h, gnorm, ex, saved, copy_scale):
    n, g, u, a = saved
    wg, wu, wd = (ex.weight(tag + k) for k in ("_w_gate", "_w_up", "_w_down"))
    host, done = _carried(ex, "rs", tag + "_dact")
    dg, du, *outs = _mm(tag + "_dact", [dhb_half], [wd], [BF16, BF16], tb=True, extras=[(g, "mn"), (u, "mn")],
                        epilogue=_dswiglu_epilogue, tm=1024, tn=512, host=host)
    done(outs)
    host, done = _carried(ex, "rs", tag + "_dwd")
    dwd, *outs = _mm(tag + "_dwd", [a], [dhb_half], [BF16], ta=True, tm=512, tn=2048, tk=2048, host=host)
    done(outs)
    ex.grad(tag + "_w_down", dwd)
    host, done = _carried(ex, "rs", tag + "_dwgu")
    dwg, dwu, *outs = _mm(tag + "_dwgu", [n], [dg, du], [BF16, BF16], pairs=((0, 0, 0), (0, 1, 1)), ta=True,
                          tm=1024, tn=512, tk=2048, host=host)
    done(outs)
    ex.grad(tag + "_w_gate", dwg)
    ex.grad(tag + "_w_up", dwu)
    host, done = _carried(ex, "rs", tag + "_dn")
    dn, *outs = _mm(tag + "_dn", [dg, du], [wg, wu], [F32], pairs=((0, 0, 0), (1, 1, 0)), tb=True,
                    tm=1024, tn=1024, tk=1408, host=host)
    done(outs)
    return _rms_bwd(tag + "_dnorm", dn, h, gnorm, dres=dh, copy_scale=copy_scale)


ATTN_HEAD_PAIRS = 8


def _to_attn_order(a):
    S, w = a.shape
    return a.reshape(S // 16, 16, w).transpose(1, 0, 2).reshape(S, w)


def _from_attn_order(a):
    S, w = a.shape
    return a.reshape(16, S // 16, w).transpose(1, 0, 2).reshape(S, w)


def _attn_geom(S, d):
    s16 = S // 16
    if d == 16:
        return (16, s16), (1, SWA_BLOCK), (lambda r, b: (r, b)), 16, s16 // SWA_BLOCK
    if d == 4:
        return (4, 4, s16), (4, 1, SWA_BLOCK // 4), (lambda r, b: (0, r, b)), 4, s16 // (SWA_BLOCK // 4)
    return (16, s16), (16, SWA_BLOCK // 16), (lambda r, b: (0, b)), 1, s16 // (SWA_BLOCK // 16)


def _attn_pos(rho, d):
    if d == 16:
        return rho
    if d == 4:
        return 4 * (rho & 31) + (rho >> 5)
    return 16 * (rho & 7) + (rho >> 3)


def _attn_spec(S, d, lb, col, shift=0):
    _, blk, idx, _, nb = _attn_geom(S, d)
    return pl.BlockSpec(blk + (lb,), lambda r, cb, b: idx(r, jnp.clip(b + shift, 0, nb - 1)) + (col(cb),))


def _attn_view(a, d):
    return a.reshape(_attn_geom(a.shape[0], d)[0] + (a.shape[1],))


def _attn_valid(d):
    qp = _attn_pos(lax.broadcasted_iota(jnp.int32, (SWA_BLOCK, 2 * SWA_BLOCK), 0), d)
    kk = lax.broadcasted_iota(jnp.int32, (SWA_BLOCK, 2 * SWA_BLOCK), 1)
    kp = _attn_pos(kk & (SWA_BLOCK - 1), d)
    is_prev = kk < SWA_BLOCK
    return qp, kp, is_prev


def _head_masks(rows=SWA_BLOCK):
    lane = lax.broadcasted_iota(jnp.int32, (rows, LANES), 1)
    return [lane < HEAD_DIM, lane >= HEAD_DIM]


def _attn_ld(ref, sl):
    t = ref[(slice(None),) * (len(ref.shape) - 1) + (sl,)]
    return t.reshape(-1, t.shape[-1])


def _attn_st(ref, sl, val):
    ref[(slice(None),) * (len(ref.shape) - 1) + (sl,)] = val.reshape(ref.shape[:-1] + (val.shape[-1],))


def _per_head(t, first):
    sw = pltpu.roll(t, HEAD_DIM, 1)
    lo = lax.broadcasted_iota(jnp.int32, t.shape, 1) < HEAD_DIM
    return jnp.where(lo, t, sw) if first else jnp.where(lo, sw, t)


def _dot_nt(a, b):
    return lax.dot_general(a, b, (((1,), (1,)), ((), ())), preferred_element_type=F32)


def _dot_tn(a, b):
    return lax.dot_general(a, b, (((0,), (0,)), ((), ())), preferred_element_type=F32)


def _dot(a, b):
    return jnp.dot(a, b, preferred_element_type=F32)


def _keep(mask, t):
    return jnp.where(mask, t.astype(F32), 0.0).astype(BF16)


def _attn_cols(A):
    lb = min(A, LANES * ATTN_HEAD_PAIRS)
    ncol = A // lb
    return lb, ncol, [lambda cb, part=part: part * ncol + cb for part in range(3)], (lambda cb: cb)


def _attn_fwd_stage(name, qkv, d, prev, final, host=None):
    S, A3 = qkv.shape
    A = A3 // 3
    lb, ncol, (cq, ck, cv), ca = _attn_cols(A)
    view, _, _, nres, nb = _attn_geom(S, d)
    scale = HEAD_DIM ** -0.5
    has_prev = prev is not None
    n_out = 2 if final else 3
    nhi, nho = (len(host.ins), len(host.out_shapes)) if host else (0, 0)

    def body(*refs):
        q_ref, kp_ref, kc_ref, vp_ref, vc_ref = refs[:5]
        p_refs = refs[5:8] if has_prev else ()
        n_in = 5 + len(p_refs)
        hin, o_refs = refs[n_in:n_in + nhi], refs[n_in + nhi:n_in + nhi + n_out]
        hout, sems = refs[n_in + nhi + n_out:n_in + nhi + n_out + nho], refs[n_in + nhi + n_out + nho:]
        b = pl.program_id(2)
        if host:
            @pl.when((pl.program_id(0) == 0) & (pl.program_id(1) == 0) & (b == 0))
            def _():
                host.start(hin, hout, *sems)
        qp, kp_, is_prev = _attn_valid(d)
        valid = (is_prev & (kp_ >= qp) & (b > 0)) | (jnp.logical_not(is_prev) & (kp_ <= qp))
        hm, hm2 = _head_masks(), _head_masks(2 * SWA_BLOCK)
        for hp in range(lb // LANES):
            sl = slice(hp * LANES, (hp + 1) * LANES)
            q = _attn_ld(q_ref, sl)
            k2 = jnp.concatenate([_attn_ld(kp_ref, sl), _attn_ld(kc_ref, sl)], axis=0).astype(BF16)
            v2 = jnp.concatenate([_attn_ld(vp_ref, sl), _attn_ld(vc_ref, sl)], axis=0)
            o = jnp.zeros((SWA_BLOCK, LANES), F32)
            m = jnp.zeros((SWA_BLOCK, LANES), F32)
            l = jnp.zeros((SWA_BLOCK, LANES), F32)
            for hh in range(2):
                s = jnp.where(valid, _dot_nt(_keep(hm[hh], q), k2) * scale, MASK_VALUE)
                mh = jnp.max(s, axis=-1, keepdims=True)
                p = jnp.exp(s - mh)
                lh = jnp.sum(p, axis=-1, keepdims=True)
                o = o + _dot(p.astype(BF16), _keep(hm2[hh], v2))
                m = jnp.where(hm[hh], mh, m)
                l = jnp.where(hm[hh], lh, l)
            if has_prev:
                po, pm, pl_ = (_attn_ld(r, sl) for r in p_refs)
                mn = jnp.maximum(m, pm)
                w_new, w_old = jnp.exp(m - mn), jnp.exp(pm - mn)
                o = o * w_new + po * w_old
                l = l * w_new + pl_ * w_old
                m = mn
            if final:
                _attn_st(o_refs[0], sl, o / l)
                _attn_st(o_refs[1], sl, m + jnp.log(l))
            else:
                _attn_st(o_refs[0], sl, o)
                _attn_st(o_refs[1], sl, m)
                _attn_st(o_refs[2], sl, l)

        if host:
            @pl.when((pl.program_id(0) == nres - 1) & (pl.program_id(1) == ncol - 1) & (b == nb - 1))
            def _():
                host.wait(hin, hout, *sems)

    qk = _attn_view(qkv, d)
    prev_v = [_attn_view(t, d) for t in prev] if has_prev else []
    sp = functools.partial(_attn_spec, S, d, lb)
    any_spec = pl.BlockSpec(memory_space=pl.ANY)
    res = _pcall(
        body, name=name,
        grid=(nres, ncol, nb),
        in_specs=[sp(cq), sp(ck, -1), sp(ck), sp(cv, -1), sp(cv)] + [sp(ca)] * len(prev_v) + [any_spec] * nhi,
        out_specs=[sp(ca)] * n_out + [any_spec] * nho,
        out_shape=[jax.ShapeDtypeStruct(view + (A,), F32)] * n_out + (list(host.out_shapes) if host else []),
        scratch_shapes=[pltpu.SemaphoreType.DMA((host.n_sem,)), pltpu.SemaphoreType.DMA((host.n_sem,))] if host else [],
        compiler_params=_params(*(("arbitrary",) * 3 if host else ("parallel", "parallel", "arbitrary"))),
    )(qk, qk, qk, qk, qk, *prev_v, *(host.ins if host else []))
    return [t.reshape(S, A) for t in res[:n_out]], res[n_out:]


def _attn_fwd(qkv, ex):
    st = None
    for i, d in enumerate(DILATIONS):
        name = "attn_fwd_d%d" % d
        host, done = _carried(ex, "ag", name)
        st, outs = _attn_fwd_stage(name, qkv, d, st, final=(i == len(DILATIONS) - 1), host=host)
        done(outs)
    return st


def _attn_delta(dya, ya):
    S, A = ya.shape
    ri = lax.broadcasted_iota(jnp.int32, (A, A), 0) // HEAD_DIM
    ci = lax.broadcasted_iota(jnp.int32, (A, A), 1) // HEAD_DIM
    ones_bd = (ri == ci).astype(BF16)

    def fn(ins, ps):
        prod = ins[0] * ins[1]
        hi = prod.astype(BF16)
        lo = (prod - hi.astype(F32)).astype(BF16)
        return [_dot(hi, ps[0]) + _dot(lo, ps[0])], []

    return _rowwise("attn_delta", fn, [dya, ya], [ones_bd], [(A, F32)])[0]


def _attn_bwd_stage(name, qkv, do, lse, delta, d, prev):
    S, A3 = qkv.shape
    A = A3 // 3
    lb, ncol, (cq, ck, cv), ca = _attn_cols(A)
    view, _, _, nres, nb = _attn_geom(S, d)
    scale = HEAD_DIM ** -0.5
    has_prev = prev is not None
    lane_slices = [slice(hp * LANES, (hp + 1) * LANES) for hp in range(lb // LANES)]

    def body(*refs):
        q_ref, kp_ref, kc_ref, vp_ref, vc_ref, do_ref, lse_ref, dl_ref = refs[:8]
        p_refs = refs[8:11] if has_prev else ()
        dq_ref, dk_ref, dv_ref, dk_c, dv_c = refs[8 + len(p_refs):]
        b = pl.program_id(2)

        def put_keys(sl, dk, dv):
            if has_prev:
                dk, dv = dk + _attn_ld(p_refs[1], sl), dv + _attn_ld(p_refs[2], sl)
            _attn_st(dk_ref, sl, dk)
            _attn_st(dv_ref, sl, dv)

        @pl.when(b == 0)
        def _():
            dk_c[...] = jnp.zeros(dk_c.shape, F32)
            dv_c[...] = jnp.zeros(dv_c.shape, F32)

        @pl.when(b < nb)
        def _():
            qp, kp_, is_prev = _attn_valid(d)
            valid = (is_prev & (kp_ >= qp) & (b > 0)) | (jnp.logical_not(is_prev) & (kp_ <= qp))
            hm, hm2 = _head_masks(), _head_masks(2 * SWA_BLOCK)
            for sl in lane_slices:
                q, do_, lse_, dl_ = (_attn_ld(r, sl) for r in (q_ref, do_ref, lse_ref, dl_ref))
                k2 = jnp.concatenate([_attn_ld(kp_ref, sl), _attn_ld(kc_ref, sl)], axis=0)
                v2 = jnp.concatenate([_attn_ld(vp_ref, sl), _attn_ld(vc_ref, sl)], axis=0).astype(BF16)
                k2b = k2.astype(BF16)
                dq = jnp.zeros((SWA_BLOCK, LANES), F32)
                dk2 = jnp.zeros((2 * SWA_BLOCK, LANES), F32)
                dv2 = jnp.zeros((2 * SWA_BLOCK, LANES), F32)
                for hh in range(2):
                    qh, doh = _keep(hm[hh], q), _keep(hm[hh], do_)
                    lh, dh = _per_head(lse_, hh == 0), _per_head(dl_, hh == 0)
                    lh2, dh2 = jnp.concatenate([lh, lh], axis=1), jnp.concatenate([dh, dh], axis=1)
                    p = jnp.where(valid, jnp.exp(_dot_nt(qh, k2b) * scale - lh2), 0.0)
                    ds = (p * (_dot_nt(doh, v2) - dh2)).astype(BF16)
                    dq = dq + _dot(ds, _keep(hm2[hh], k2))
                    dk2 = dk2 + _dot_tn(ds, qh)
                    dv2 = dv2 + _dot_tn(p.astype(BF16), doh)
                dq, dk2 = dq * scale, dk2 * scale
                if has_prev:
                    dq = dq + _attn_ld(p_refs[0], sl)
                _attn_st(dq_ref, sl, dq)
                put_keys(sl, dk_c[:, sl] + dk2[:SWA_BLOCK], dv_c[:, sl] + dv2[:SWA_BLOCK])
                dk_c[:, sl] = dk2[SWA_BLOCK:]
                dv_c[:, sl] = dv2[SWA_BLOCK:]

        @pl.when(b == nb)
        def _():
            for sl in lane_slices:
                put_keys(sl, dk_c[:, sl], dv_c[:, sl])

    qk = _attn_view(qkv, d)
    acts = [_attn_view(t, d) for t in (do, lse, delta)] + ([_attn_view(t, d) for t in prev] if has_prev else [])
    sp = functools.partial(_attn_spec, S, d, lb)
    res = _pcall(
        body, name=name,
        grid=(nres, ncol, nb + 1),
        in_specs=[sp(cq), sp(ck, -1), sp(ck), sp(cv, -1), sp(cv), sp(ca), sp(ca), sp(ca)]
        + ([sp(ca), sp(ca, -1), sp(ca, -1)] if has_prev else []),
        out_specs=[sp(ca), sp(ca, -1), sp(ca, -1)],
        out_shape=[jax.ShapeDtypeStruct(view + (A,), F32)] * 3,
        scratch_shapes=[pltpu.VMEM((SWA_BLOCK, lb), F32)] * 2,
        compiler_params=_params("parallel", "parallel", "arbitrary"),
    )(qk, qk, qk, qk, qk, *acts)
    return [t.reshape(S, A) for t in res]


def _attn_bwd(qkv, dya, ya, lse):
    delta = _attn_delta(dya, ya)
    sums = None
    for d in DILATIONS:
        sums = _attn_bwd_stage("attn_bwd_d%d" % d, qkv, dya, lse, delta, d, sums)
    return sums


def _ssm_perm(a, T):
    S, w = a.shape
    return a.reshape(S // T, SUBLANES, T // SUBLANES, w).transpose(0, 2, 1, 3).reshape(S, w)


def _ssm_unperm(a, T):
    S, w = a.shape
    return a.reshape(S // T, T // SUBLANES, SUBLANES, w).transpose(0, 2, 1, 3).reshape(S, w)


def _ssm_powers(lam_ref, pw_ref, T, ns):
    tc = T // SUBLANES
    n = (lax.broadcasted_iota(jnp.int32, (tc, 1), 0) + 1).astype(F32)
    mag = jnp.exp(n * lam_ref[0, 0:1, :])
    ang = n * lam_ref[0, 1:2, :]
    rows8 = lambda t: jnp.broadcast_to(t[:, None, :], (tc, SUBLANES, ns)).reshape(T, ns)
    pw_ref[:, 0:ns] = rows8(mag * jnp.cos(ang))
    pw_ref[:, ns:2 * ns] = rows8(mag * jnp.sin(ang))


def _ssm_scan(xs, off, pw_ref, carry_ref, T, ns, reverse):
    Tc = T // SUBLANES
    sgn = -1.0 if reverse else 1.0
    ar, ai = pw_ref[0:SUBLANES, 0:ns], sgn * pw_ref[0:SUBLANES, ns:2 * ns]

    def rows(i):
        return pl.ds(pl.multiple_of(off + i * SUBLANES, SUBLANES), SUBLANES)

    def step(k, h):
        hr, hi = h
        r = rows(Tc - 1 - k if reverse else k)
        nr = ar * hr - ai * hi + xs[r, 0:ns]
        ni = ar * hi + ai * hr + xs[r, ns:2 * ns]
        xs[r, 0:ns] = nr
        xs[r, ns:2 * ns] = ni
        return nr, ni

    z = jnp.zeros((SUBLANES, ns), F32)
    er, ei = lax.fori_loop(0, Tc, step, (z, z), unroll=4)
    atr, ati = pw_ref[T - SUBLANES:T, 0:ns], sgn * pw_ref[T - SUBLANES:T, ns:2 * ns]
    rowid = lax.broadcasted_iota(jnp.int32, (SUBLANES, ns), 0)
    cr, ci = carry_ref[:, 0:ns], carry_ref[:, ns:2 * ns]
    ctr, cti = z, z
    for jj in range(SUBLANES):
        j = SUBLANES - 1 - jj if reverse else jj
        sel = rowid == j
        ctr, cti = jnp.where(sel, cr, ctr), jnp.where(sel, ci, cti)
        ejr = jnp.broadcast_to(jnp.sum(jnp.where(sel, er, 0.0), axis=0, keepdims=True), (SUBLANES, ns))
        eji = jnp.broadcast_to(jnp.sum(jnp.where(sel, ei, 0.0), axis=0, keepdims=True), (SUBLANES, ns))
        cr, ci = ejr + atr * cr - ati * ci, eji + atr * ci + ati * cr
    carry_ref[:, 0:ns] = cr
    carry_ref[:, ns:2 * ns] = ci

    def fix(i, _):
        r = rows(i)
        pr_rows = pl.ds(pl.multiple_of((Tc - 1 - i if reverse else i) * SUBLANES, SUBLANES), SUBLANES)
        pr, pi = pw_ref[pr_rows, 0:ns], sgn * pw_ref[pr_rows, ns:2 * ns]
        xs[r, 0:ns] += pr * ctr - pi * cti
        xs[r, ns:2 * ns] += pr * cti + pi * ctr
        return 0

    lax.fori_loop(0, Tc, fix, 0, unroll=4)
    return ctr, cti


def _ssm_fwd(ufp, bb, cc, lam_dt, drow, T):
    S, W = ufp.shape
    GB, cw, ns2 = bb.shape
    ns = ns2 // 2
    NCH = S // T

    def body(uf_ref, bb_ref, cc_ref, lam_ref, d_ref, y_ref, hs_ref, xs, pw, carry):
        @pl.when(pl.program_id(1) == 0)
        def _():
            _ssm_powers(lam_ref, pw, T, ns)
            carry[...] = jnp.zeros(carry.shape, F32)

        uf = uf_ref[...]
        xs[...] = _dot(uf.astype(BF16), bb_ref[0])
        hs_ref[0, 0] = carry[...]
        _ssm_scan(xs, 0, pw, carry, T, ns, reverse=False)
        y_ref[...] = _dot(xs[...].astype(BF16), cc_ref[0]) + d_ref[...] * uf

    return _pcall(
        body, name="ssm_fwd",
        grid=(GB, NCH),
        in_specs=[pl.BlockSpec((T, cw), lambda g, c: (c, g)),
                  pl.BlockSpec((1, cw, ns2), lambda g, c: (g, 0, 0)),
                  pl.BlockSpec((1, ns2, cw), lambda g, c: (g, 0, 0)),
                  pl.BlockSpec((1, 2, ns), lambda g, c: (g, 0, 0)),
                  pl.BlockSpec((1, cw), lambda g, c: (0, g))],
        out_specs=[pl.BlockSpec((T, cw), lambda g, c: (c, g)),
                   pl.BlockSpec((1, 1, SUBLANES, ns2), lambda g, c: (g, c, 0, 0))],
        out_shape=[jax.ShapeDtypeStruct((S, W), F32),
                   jax.ShapeDtypeStruct((GB, NCH, SUBLANES, ns2), F32)],
        scratch_shapes=[pltpu.VMEM((T, ns2), F32), pltpu.VMEM((T, ns2), F32), pltpu.VMEM((SUBLANES, ns2), F32)],
        compiler_params=_params("arbitrary", "arbitrary"),
    )(ufp, bb, cc, lam_dt, drow)


def _ssm_bwd(ufp, dyp, bb, bbt, cc, cct, lam_dt, drow, hstart, T):
    S, W = ufp.shape
    GB, cw, ns2 = bb.shape
    ns = ns2 // 2
    NCH = S // T

    def body(uf_ref, dy_ref, bb_ref, bbt_ref, cc_ref, cct_ref, lam_ref, d_ref, hs_ref,
             duf_ref, dbb_ref, dcc_ref, da_ref, dd_ref, hb, ls, pw, carry_f, carry_b):
        @pl.when(pl.program_id(1) == 0)
        def _():
            _ssm_powers(lam_ref, pw, T, ns)
            carry_b[...] = jnp.zeros(carry_b.shape, F32)
            dbb_ref[...] = jnp.zeros(dbb_ref.shape, F32)
            dcc_ref[...] = jnp.zeros(dcc_ref.shape, F32)
            da_ref[...] = jnp.zeros(da_ref.shape, F32)
            dd_ref[...] = jnp.zeros(dd_ref.shape, F32)

        uf, dy = uf_ref[...], dy_ref[...]
        ufb, dyb = uf.astype(BF16), dy.astype(BF16)
        hb[SUBLANES:T + SUBLANES, :] = _dot(ufb, bb_ref[0])
        carry_f[...] = hs_ref[0, 0]
        ctr, cti = _ssm_scan(hb, SUBLANES, pw, carry_f, T, ns, reverse=False)
        hb[0:SUBLANES, 0:ns] = ctr
        hb[0:SUBLANES, ns:ns2] = cti
        ls[...] = _dot(dyb, cct_ref[0])
        _ssm_scan(ls, 0, pw, carry_b, T, ns, reverse=True)
        lv = ls[...]
        lb = lv.astype(BF16)
        dbb_ref[0] += _dot_tn(ufb, lb)
        dcc_ref[0] += _dot_tn(hb[SUBLANES:T + SUBLANES, :].astype(BF16), dyb)
        lr, li = lv[:, 0:ns], lv[:, ns:ns2]
        hpr, hpi = hb[0:T, 0:ns], hb[0:T, ns:ns2]
        dar = jnp.sum(lr * hpr + li * hpi, axis=0, keepdims=True)
        dai = jnp.sum(li * hpr - lr * hpi, axis=0, keepdims=True)
        da_ref[0, 0:1, 0:ns] += dar
        da_ref[0, 0:1, ns:ns2] += dai
        duf_ref[...] = _dot(lb, bbt_ref[0]) + d_ref[...] * dy
        dd_ref[...] += jnp.sum(dy * uf, axis=0, keepdims=True)

    rc = lambda c: NCH - 1 - c
    return _pcall(
        body, name="ssm_bwd",
        grid=(GB, NCH),
        in_specs=[pl.BlockSpec((T, cw), lambda g, c: (rc(c), g)),
                  pl.BlockSpec((T, cw), lambda g, c: (rc(c), g)),
                  pl.BlockSpec((1, cw, ns2), lambda g, c: (g, 0, 0)),
                  pl.BlockSpec((1, ns2, cw), lambda g, c: (g, 0, 0)),
                  pl.BlockSpec((1, ns2, cw), lambda g, c: (g, 0, 0)),
                  pl.BlockSpec((1, cw, ns2), lambda g, c: (g, 0, 0)),
                  pl.BlockSpec((1, 2, ns), lambda g, c: (g, 0, 0)),
                  pl.BlockSpec((1, cw), lambda g, c: (0, g)),
                  pl.BlockSpec((1, 1, SUBLANES, ns2), lambda g, c: (g, rc(c), 0, 0))],
        out_specs=[pl.BlockSpec((T, cw), lambda g, c: (rc(c), g)),
                   pl.BlockSpec((1, cw, ns2), lambda g, c: (g, 0, 0)),
                   pl.BlockSpec((1, ns2, cw), lambda g, c: (g, 0, 0)),
                   pl.BlockSpec((1, SUBLANES, ns2), lambda g, c: (g, 0, 0)),
                   pl.BlockSpec((1, cw), lambda g, c: (0, g))],
        out_shape=[jax.ShapeDtypeStruct((S, W), F32),
                   jax.ShapeDtypeStruct((GB, cw, ns2), F32),
                   jax.ShapeDtypeStruct((GB, ns2, cw), F32),
                   jax.ShapeDtypeStruct((GB, SUBLANES, ns2), F32),
                   jax.ShapeDtypeStruct((1, W), F32)],
        scratch_shapes=[pltpu.VMEM((T + SUBLANES, ns2), F32), pltpu.VMEM((T, ns2), F32), pltpu.VMEM((T, ns2), F32),
                        pltpu.VMEM((SUBLANES, ns2), F32), pltpu.VMEM((SUBLANES, ns2), F32)],
        compiler_params=_params("arbitrary", "arbitrary"),
    )(ufp, dyp, bb, bbt, cc, cct, lam_dt, drow, hstart)


def _ssm_disc_math(lr, li, logdt, br, bi):
    dt = jnp.exp(logdt)
    mag = jnp.exp(lr * dt)
    ar = mag * jnp.cos(li * dt)
    ai = mag * jnp.sin(li * dt)
    nr, ni = ar - 1.0, ai
    den = lr * lr + li * li
    cr = (nr * lr + ni * li) / den
    ci = (ni * lr - nr * li) / den
    return ar, ai, cr * br - ci * bi, cr * bi + ci * br


def _ssm_disc(lr, li, logdt, br, bi):
    C = br.shape[1]

    def fn(ins, ps):
        _, _, bbr, bbi = _ssm_disc_math(*ins)
        dt = jnp.exp(ins[2])
        return [ins[0] * dt, ins[1] * dt, bbr, bbi], []

    return _rowwise("ssm_disc", fn, [lr, li, logdt, br, bi], [], [(1, F32), (1, F32), (C, F32), (C, F32)], ts=512)


def _ssm_disc_bwd(lr, li, logdt, br, bi, dar, dai, dbbr, dbbi):
    C = br.shape[1]

    def fn(ins, ps):
        _, vjp = jax.vjp(_ssm_disc_math, *ins[:5])
        return list(vjp(tuple(ins[5:]))), []

    return _rowwise("ssm_disc_bwd", fn, [lr, li, logdt, br, bi, dar, dai, dbbr, dbbi], [],
                    [(1, F32), (1, F32), (1, F32), (C, F32), (C, F32)], ts=512)


def _block_diag(t):
    GB, g, a, b = t.shape
    eye = jnp.eye(g, dtype=t.dtype)
    return (t[:, :, :, None, :] * eye[None, :, None, :, None]).reshape(GB, g * a, g * b)


def _block_diag_take(t, g):
    GB, ga, gb_ = t.shape
    a, b = ga // g, gb_ // g
    eye = jnp.eye(g, dtype=t.dtype)
    return (t.reshape(GB, g, a, g, b) * eye[None, :, None, :, None]).sum(axis=3)


def _loss_head(h4, tgt, gf):
    D = h4.shape[1]

    def fn(ins, ps):
        x, t = ins
        xh, r = _xhat(x)
        err = xh * ps[0] - t
        dn = err * (1.0 / D)
        dxh = dn * ps[0]
        dx = r * (dxh - xh * jnp.mean(dxh * xh, axis=-1, keepdims=True))
        return [dx], [jnp.sum(err * err, axis=0, keepdims=True), jnp.sum(dn * xh, axis=0, keepdims=True)]

    return _rowwise("loss_head", fn, [h4, tgt], [gf], [(D, F32)], accs=[D, D])


def _gelu(x):
    return 0.5 * x * (1.0 + jnp.tanh(GELU_C * (x + GELU_K * x * x * x)))


def _gelu_grad(x):
    t = jnp.tanh(GELU_C * (x + GELU_K * x * x * x))
    return 0.5 * (1.0 + t) + 0.5 * x * (1.0 - t * t) * GELU_C * (1.0 + 3.0 * GELU_K * x * x)


def _mesh_pos():
    return lax.axis_index("x"), lax.axis_index("y"), lax.axis_index("c")


def _other_chips(x, y):
    return [(1 - x, y), (x, 1 - y), (1 - x, 1 - y)]


def _remote(src, dst, send, recv, dev):
    return pltpu.make_async_remote_copy(src_ref=src, dst_ref=dst, send_sem=send, recv_sem=recv,
                                        device_id=dev, device_id_type=MESH)


ANY = pl.BlockSpec(memory_space=pl.ANY)


COMM_BLOCK_BYTES = 3 << 19


def _place():
    x, y, c = _mesh_pos()
    return jnp.stack([c] + [2 * cx + cy for cx, cy in _other_chips(x, y)] + [2 * x + y]).astype(jnp.int32)


def _send_chips(name, srcs, specs, tr, nth, cw):
    hr = nth * tr
    n = len(srcs)

    def body(*refs):
        got_ref, send, recv = refs[1 + n:]
        t = pl.program_id(0)
        x, y, c = _mesh_pos()
        cps = []
        for j, chip in enumerate(_other_chips(x, y)):
            dst = got_ref.at[pl.ds(pl.multiple_of(j * hr + t * tr, 16), tr), :]
            cp = _remote(refs[1 + j % n], dst, send.at[j], recv.at[j], (*chip, c))
            cp.start()
            cps.append(cp)
        for cp in cps:
            cp.wait_send()

        @pl.when(t == nth - 1)
        def _():
            for j in range(3):
                r_ = got_ref.at[pl.ds(j * hr, hr), :]
                _remote(r_, r_, send.at[j], recv.at[j], (x, y, c)).wait_recv()

    return _pcall(
        body, name=name,
        grid_spec=pltpu.PrefetchScalarGridSpec(
            num_scalar_prefetch=1, grid=(nth,), in_specs=specs, out_specs=ANY,
            scratch_shapes=[pltpu.SemaphoreType.DMA((3,)), pltpu.SemaphoreType.DMA((3,))]),
        out_shape=jax.ShapeDtypeStruct((3 * hr, cw), srcs[0].dtype),
        compiler_params=_params("arbitrary"),
    )(_place(), *srcs)


def _ag_assemble(name, shard, stage, axis, tr, nth):
    R, cc = shard.shape
    hr = nth * tr
    full = (R, N_CHIPS * cc) if axis == 1 else (N_CHIPS * R, cc)

    def body(pl_ref, s0, s1, s2, h0, h1, out_ref, send, recv, lsem):
        t = pl.program_id(0)
        x, y, c = _mesh_pos()

        def region(s, half):
            if axis == 1:
                return out_ref.at[pl.ds(pl.multiple_of(half * hr + t * tr, 16), tr), pl.ds(pl.multiple_of(s * cc, LANES), cc)]
            return out_ref.at[pl.ds(pl.multiple_of(s * R + half * hr + t * tr, 16), tr), :]

        cps = []
        for j, src in enumerate((s0, s1, s2)):
            dst = region(pl_ref[1 + j], c)
            cps.append(_remote(src, dst, send.at[j], recv, (x, y, 1 - c)))
            cps.append(pltpu.make_async_copy(src, dst, lsem.at[j]))
        for half, src in enumerate((h0, h1)):
            cps.append(pltpu.make_async_copy(src, region(pl_ref[4], half), lsem.at[3 + half]))
        for cp in cps:
            cp.start()
        for k, cp in enumerate(cps):
            if k < 6 and k % 2 == 0:
                cp.wait_send()
            else:
                cp.wait()

        @pl.when(t == nth - 1)
        def _():
            r_ = out_ref.at[pl.ds(0, hr), pl.ds(0, 3 * cc)] if axis == 1 else out_ref.at[pl.ds(0, 3 * hr), :]
            _remote(r_, r_, send.at[0], recv, (x, y, c)).wait_recv()

    blk = lambda f: pl.BlockSpec((tr, cc), f)
    return _pcall(
        body, name=name,
        grid_spec=pltpu.PrefetchScalarGridSpec(
            num_scalar_prefetch=1, grid=(nth,),
            in_specs=[blk(lambda t, p, j=j: (j * nth + t, 0)) for j in range(3)]
            + [blk(lambda t, p, h=h: (h * nth + t, 0)) for h in range(2)],
            out_specs=ANY,
            scratch_shapes=[pltpu.SemaphoreType.DMA((3,)), pltpu.SemaphoreType.DMA, pltpu.SemaphoreType.DMA((5,))]),
        out_shape=jax.ShapeDtypeStruct(full, shard.dtype),
        compiler_params=_params("arbitrary"),
    )(_place(), stage, stage, stage, shard, shard)


def _comm_rows(hr, row_bytes):
    return _tile(hr, max(16, COMM_BLOCK_BYTES // row_bytes // 16 * 16), 16)


def _host_send(items):
    def copies(ins, outs, send, recv):
        x, y, c = _mesh_pos()
        cps = []
        for w, (_, kind, hr, cw) in enumerate(items):
            for j, (cx, cy) in enumerate(_other_chips(x, y)):
                s = 2 * cx + cy
                if kind == "half":
                    src = ins[w].at[pl.ds(pl.multiple_of(c * hr, 16), hr), :]
                elif kind == "cols":
                    src = ins[w].at[:, pl.ds(pl.multiple_of(s * cw, LANES), cw)]
                else:
                    src = ins[w].at[pl.ds(pl.multiple_of(s * hr, 16), hr), :]
                cps.append(_remote(src, outs[w].at[pl.ds(j * hr, hr), :], send.at[3 * w + j], recv.at[3 * w + j], (cx, cy, c)))
        return cps

    def start(ins, outs, send, recv):
        for cp in copies(ins, outs, send, recv):
            cp.start()

    def wait(ins, outs, send, recv):
        for cp in copies(ins, outs, send, recv):
            cp.wait()

    return _Host([a for a, _, _, _ in items], [jax.ShapeDtypeStruct((3 * hr, cw), a.dtype) for a, _, hr, cw in items],
                 3 * len(items), start, wait)


def _ag_send(name, sh):
    R, cc = sh.shape
    tr = _comm_rows(R // 2, cc * 2)
    nth = R // 2 // tr
    return _send_chips(name, [sh], [pl.BlockSpec((tr, cc), lambda t, p: (p[0] * nth + t, 0))], tr, nth, cc)


def _ag_finish(name, sh, stage, axis):
    R, cc = sh.shape
    tr = _comm_rows(R // 2, cc * 2)
    return _ag_assemble(name, sh, stage, axis, tr, R // 2 // tr)


def _pair_sum(name, g, tr, nblk, blk):
    cw = g.shape[1]
    c_arr = lax.axis_index("c").astype(jnp.int32).reshape(1)

    def body(c_ref, keep_ref, send_ref, out_ref, land, send, recv):
        i = pl.program_id(0)
        slot = lax.rem(i, 2)
        x, y, c = _mesh_pos()
        cp = _remote(send_ref, land.at[slot], send.at[slot], recv.at[slot], (x, y, 1 - c))
        cp.start()
        cp.wait_recv()
        out_ref[...] = (keep_ref[...].astype(F32) + land[slot].astype(F32)).astype(BF16)
        cp.wait_send()

    return _pcall(
        body, name=name,
        grid_spec=pltpu.PrefetchScalarGridSpec(
            num_scalar_prefetch=1, grid=(nblk,),
            in_specs=[pl.BlockSpec((tr, cw), lambda i, c_ref: (blk(i, c_ref[0]), 0)),
                      pl.BlockSpec((tr, cw), lambda i, c_ref: (blk(i, 1 - c_ref[0]), 0))],
            out_specs=pl.BlockSpec((tr, cw), lambda i, c_ref: (i, 0)),
            scratch_shapes=[pltpu.VMEM((2, tr, cw), BF16), pltpu.SemaphoreType.DMA((2,)), pltpu.SemaphoreType.DMA((2,))]),
        out_shape=jax.ShapeDtypeStruct((nblk * tr, cw), BF16),
        compiler_params=_params("arbitrary"),
    )(c_arr, g, g)


def _sum_chips_swap(name, q, got, qspec, tr, nth, cw):
    def body(p_ref, q_ref, g0, g1, g2, out_ref, buf, send, recv, lsem):
        t = pl.program_id(0)
        x, y, c = _mesh_pos()
        buf[...] = q_ref[...].astype(F32) + g0[...].astype(F32) + g1[...].astype(F32) + g2[...].astype(F32)
        dst = out_ref.at[pl.ds(pl.multiple_of((p_ref[0] * nth + t) * tr, 16), tr), :]
        cp = _remote(buf, dst, send, recv, (x, y, 1 - c))
        lc = pltpu.make_async_copy(buf, dst, lsem)
        cp.start()
        lc.start()
        lc.wait()
        cp.wait_send()

        @pl.when(t == nth - 1)
        def _():
            theirs = out_ref.at[pl.ds(0, nth * tr), :]
            _remote(theirs, theirs, send, recv, (x, y, c)).wait_recv()

    return _pcall(
        body, name=name,
        grid_spec=pltpu.PrefetchScalarGridSpec(
            num_scalar_prefetch=1, grid=(nth,),
            in_specs=[qspec] + [pl.BlockSpec((tr, cw), lambda t, p, j=j: (j * nth + t, 0)) for j in range(3)],
            out_specs=ANY,
            scratch_shapes=[pltpu.VMEM((tr, cw), F32), pltpu.SemaphoreType.DMA, pltpu.SemaphoreType.DMA,
                            pltpu.SemaphoreType.DMA]),
        out_shape=jax.ShapeDtypeStruct((2 * nth * tr, cw), F32),
        compiler_params=_params("arbitrary"),
    )(_place(), q, got, got, got)


def _rs_geom(g, axis):
    rows, gw = g.shape
    return (rows // 2, gw // N_CHIPS) if axis == 1 else (rows // N_CHIPS // 2, gw)


def _rs_pair_sum(tag, g, axis):
    hr, _ = _rs_geom(g, axis)
    tr = _comm_rows(hr, g.shape[1] * 2)
    nth = hr // tr
    if axis == 1:
        nblk, blk = nth, (lambda i, half: half * nth + i)
    else:
        nblk, blk = N_CHIPS * nth, (lambda i, half: (i // nth) * (2 * nth) + half * nth + i % nth)
    return _pair_sum("rs_pair_" + tag, g, tr, nblk, blk)


def _rs_part(q, axis, hr, cw, tr):
    nth = hr // tr
    if axis == 1:
        return lambda k: pl.BlockSpec((tr, cw), lambda t, p: (t, p[k]))
    return lambda k: pl.BlockSpec((tr, cw), lambda t, p: (p[k] * nth + t, 0))


def _rs_send(tag, q, axis, hr, cw):
    tr = _comm_rows(hr, cw * 2)
    part = _rs_part(q, axis, hr, cw, tr)
    return _send_chips("rs_send_" + tag, [q, q, q], [part(1), part(2), part(3)], tr, hr // tr, cw)


def _rs_finish(tag, q, got, axis, hr, cw):
    tr = _comm_rows(hr, cw * 4)
    return _sum_chips_swap("rs_sum_swap_" + tag, q, got, _rs_part(q, axis, hr, cw, tr)(4), tr, hr // tr, cw)


class _Exchange:
    AG_PLAN = {"ffn1_up": ("ffn1_w_down", "ffn2_w_gate"), "ffn1_down": ("ffn2_w_up", "w_in"),
               "w_in_qkv": ("ssm_w_glu", "w_out", "ple_w_gate", "ple_w_proj"), "attn_fwd_d1": ("ffn2_w_down",)}
    RS_PLAN = {"ffn1_dact": ("ffn2_w_gate", "ple_w_gate", "ple_w_proj"), "ffn1_dwd": ("ffn2_w_up",),
               "ffn1_dwgu": ("ffn2_w_down", "w_in", "ssm_w_glu", "w_out", "ffn1_w_down"),
               "ffn1_dn": ("ffn1_w_gate", "ffn1_w_up")}

    def __init__(self, shards, axes):
        self.shards, self.axes = shards, axes
        self.stage, self.full, self.q, self.geom, self.got = {}, {}, {}, {}, {}

    def ag_host(self, kernel):
        item = lambda k: (self.shards[k], "half", self.shards[k].shape[0] // 2, self.shards[k].shape[1])
        return _host_send([item(k) for k in self.AG_PLAN[kernel]])

    def ag_done(self, kernel, stages):
        self.stage.update(zip(self.AG_PLAN[kernel], stages))

    def weight(self, k):
        if k not in self.full:
            stage = self.stage[k] if k in self.stage else _ag_send("ag_send_" + k, self.shards[k])
            self.full[k] = _ag_finish("ag_asm_" + k, self.shards[k], stage, self.axes[k])
        return self.full[k]

    def grad(self, k, g):
        self.q[k], self.geom[k] = _rs_pair_sum(k, g, self.axes[k]), _rs_geom(g, self.axes[k])

    def rs_host(self, kernel):
        item = lambda k: (self.q[k], "cols" if self.axes[k] == 1 else "rows") + self.geom[k]
        return _host_send([item(k) for k in self.RS_PLAN[kernel]])

    def rs_done(self, kernel, gots):
        self.got.update(zip(self.RS_PLAN[kernel], gots))

    def finish(self):
        return {k: _rs_finish(k, q, self.got[k] if k in self.got else _rs_send(k, q, self.axes[k], *self.geom[k]),
                              self.axes[k], *self.geom[k]) for k, q in self.q.items()}


def _all_reduce_small(v):
    n = v.shape[0]
    h = n // 2

    def body(v_ref, out_ref, pair_in, chips_in, send, recv):
        x, y, c = _mesh_pos()
        me, sib, my_chip = (x, y, c), (x, y, 1 - c), 2 * x + y
        mine = pl.ds(pl.multiple_of(c * h, SUBLANES), h)
        other = pl.ds(pl.multiple_of((1 - c) * h, SUBLANES), h)
        pair = _remote(v_ref.at[other], pair_in, send.at[0], recv.at[0], sib)
        pair.start()
        pair.wait()
        chips_in[my_chip] = v_ref[mine, :] + pair_in[...]
        cps = []
        for j, (cx, cy) in enumerate(_other_chips(x, y)):
            cp = _remote(chips_in.at[my_chip], chips_in.at[my_chip], send.at[1 + j], recv.at[1 + j], (cx, cy, c))
            cp.start()
            cps.append(cp)
        for j, (cx, cy) in enumerate(_other_chips(x, y)):
            slot = chips_in.at[2 * cx + cy]
            _remote(slot, slot, send.at[1 + j], recv.at[1 + j], me).wait_recv()
        out_ref[mine, :] = (chips_in[0] + chips_in[1]) + (chips_in[2] + chips_in[3])
        for cp in cps:
            cp.wait_send()
        swap = _remote(out_ref.at[mine, :], out_ref.at[mine, :], send.at[4], recv.at[4], sib)
        swap.start()
        _remote(out_ref.at[other, :], out_ref.at[other, :], send.at[4], recv.at[4], me).wait_recv()
        swap.wait_send()

    return _pcall(
        body, name="ar_small",
        in_specs=[pl.BlockSpec(memory_space=pltpu.VMEM)], out_specs=pl.BlockSpec(memory_space=pltpu.VMEM),
        out_shape=jax.ShapeDtypeStruct((n, LANES), F32),
        scratch_shapes=[pltpu.VMEM((h, LANES), F32), pltpu.VMEM((N_CHIPS, h, LANES), F32),
                        pltpu.SemaphoreType.DMA((5,)), pltpu.SemaphoreType.DMA((5,))],
        compiler_params=pltpu.CompilerParams(vmem_limit_bytes=V7X_VMEM_LIMIT_BYTES),
    )(v)


def _adamw(name, w, g, m, v):
    R, Cc = w.shape
    tr = _tile(R, max(8, (1 << 19) // Cc // 8 * 8), 8)
    c1 = 1.0 - ADAM_B1 ** ADAM_STEP
    c2 = 1.0 - ADAM_B2 ** ADAM_STEP

    def body(w_ref, g_ref, m_ref, v_ref, d_ref, nm_ref, nv_ref):
        g_ = g_ref[...]
        nm = ADAM_B1 * m_ref[...] + (1.0 - ADAM_B1) * g_
        nv = ADAM_B2 * v_ref[...] + (1.0 - ADAM_B2) * (g_ * g_)
        d_ref[...] = -ADAM_LR * ((nm / c1) / (jnp.sqrt(nv / c2) + ADAM_EPS) + ADAM_WD * w_ref[...])
        nm_ref[...] = nm
        nv_ref[...] = nv

    spec = pl.BlockSpec((tr, Cc), lambda i: (i, 0))
    return _pcall(
        body, name=name, grid=(R // tr,),
        in_specs=[spec] * 4, out_specs=[spec] * 3,
        out_shape=[jax.ShapeDtypeStruct((R, Cc), F32)] * 3,
        compiler_params=_params("parallel"),
    )(w, g, m, v)


def _pack(arrs, rows):
    flat = jnp.concatenate([a.reshape(-1) for a in arrs])
    return jnp.pad(flat, (0, rows * LANES - flat.shape[0])).reshape(rows, LANES)


def _unpack(packed, like):
    flat, out, o = packed.reshape(-1), [], 0
    for a in like:
        out.append(flat[o:o + a.size].reshape(a.shape))
        o += a.size
    return out


BIG = (
    ("ffn1_w_gate", 1), ("ffn1_w_up", 1), ("ffn1_w_down", 0), ("w_in", 1), ("ssm_w_glu", 0), ("w_out", 0),
    ("ffn2_w_gate", 1), ("ffn2_w_up", 1), ("ffn2_w_down", 0), ("ple_w_gate", 0), ("ple_w_proj", 1),
)
SMALL = ("ffn1_norm", "mix_norm", "attn_out_norm", "ssm_lambda_re", "ssm_lambda_im", "ssm_log_dt", "ssm_b_re", "ssm_b_im",
         "ssm_c_re", "ssm_c_im", "ssm_d", "ssm_b_glu", "ssm_out_norm", "ffn2_norm", "ple_norm", "final_norm")
WEIGHTS = ("ffn1_norm", "ffn1_w_gate", "ffn1_w_up", "ffn1_w_down", "mix_norm", "w_in", "attn_out_norm", "ssm_lambda_re",
           "ssm_lambda_im", "ssm_log_dt", "ssm_b_re", "ssm_b_im", "ssm_c_re", "ssm_c_im", "ssm_d", "ssm_w_glu", "ssm_b_glu",
           "ssm_out_norm", "w_out", "ffn2_norm", "ffn2_w_gate", "ffn2_w_up", "ffn2_w_down", "ple_norm", "ple_w_gate",
           "ple_w_proj", "final_norm")


def _pad_to(a, axis, n):
    pad = [(0, 0), (0, 0)]
    pad[axis] = (0, n - a.shape[axis])
    return jnp.pad(a, pad)


def _local_step(x, p, tgt, w, ex):
    S, D = x.shape
    A = w["attn_out_norm"].shape[-1]
    W = w["ssm_d"].shape[-1]
    G, P = w["ssm_lambda_re"].shape[-2:]
    C = w["ssm_b_re"].shape[-1]
    GB = G // SSM_BLOCK_GROUPS
    T = min(1024, S)
    row = lambda name: w[name].reshape(1, -1)
    gs = {}

    h1, ffn1_saved = _ffn_fwd("ffn1", x, row("ffn1_norm"), ex)
    n2 = _rms_fwd("mix_norm", h1, row("mix_norm"))
    w_in = ex.weight("w_in")
    n2p = _to_attn_order(n2)
    host, done = _carried(ex, "ag", "w_in_qkv")
    qkv, *outs = _mm("w_in_qkv", [n2p], [w_in[:, :3 * A]], [F32], tm=1024, tn=1024, host=host)
    done(outs)
    (s_in,) = _mm("w_in_ssm", [n2], [w_in[:, 3 * A:]], [F32], tm=1024, tn=1024)
    ya, lse = _attn_fwd(qkv, ex)

    col = lambda name: w[name].reshape(G * P, 1)
    logdt_x = jnp.repeat(w["ssm_log_dt"].reshape(G), P).reshape(G * P, 1)
    b_re, b_im = w["ssm_b_re"].reshape(G * P, C), w["ssm_b_im"].reshape(G * P, C)
    lrdt, lidt, bbr, bbi = _ssm_disc(col("ssm_lambda_re"), col("ssm_lambda_im"), logdt_x, b_re, b_im)
    gsz = SSM_BLOCK_GROUPS
    to_bb = lambda t: _block_diag(t.reshape(GB, gsz, P, C).transpose(0, 1, 3, 2))
    bb = jnp.concatenate([to_bb(bbr), to_bb(bbi)], axis=2).astype(BF16)
    to_cc = lambda t: _block_diag(t.reshape(GB, gsz, C, P).transpose(0, 1, 3, 2))
    cc = jnp.concatenate([to_cc(w["ssm_c_re"]), -to_cc(w["ssm_c_im"])], axis=1).astype(BF16)
    lam_dt = jnp.stack([lrdt.reshape(GB, gsz * P), lidt.reshape(GB, gsz * P)], axis=1)
    ufp = _ssm_perm(s_in, T)
    ypre, hstart = _ssm_fwd(ufp, bb, cc, lam_dt, row("ssm_d"), T)

    def glu_in(ins, ps):
        yg = _gelu(ins[0])
        return [yg, yg], []

    yg, ygb = _rowwise("ssm_gelu", glu_in, [ypre], [], [(W, F32), (W, BF16)])
    w_glu = ex.weight("ssm_w_glu")

    def glu_out(accs, ex):
        gl = accs[0] + ex[1]
        return [ex[0] * _sigmoid(gl), gl]

    ybp, gl = _mm("ssm_glu", [ygb], [w_glu], [F32, F32], extras=[(yg, "mn"), (row("ssm_b_glu"), "n")],
                  epilogue=glu_out, tm=1024, tn=1024)
    yb = _ssm_unperm(ybp, T)
    na = _from_attn_order(_rms_fwd("attn_out_norm", ya, row("attn_out_norm")))
    nb = _rms_fwd("ssm_out_norm", yb, row("ssm_out_norm"))
    w_out = ex.weight("w_out")
    (h2,) = _mm("w_out", [na, nb], [w_out[:A], w_out[A:]], [F32], pairs=((0, 0, 0), (1, 1, 0)), extras=[(h1, "mn")],
                epilogue=lambda accs, ex: [ex[0] + accs[0]], tm=1024, tn=1024)
    h3, ffn2_saved = _ffn_fwd("ffn2", h2, row("ffn2_norm"), ex)
    n4 = _rms_fwd("ple_norm", h3, row("ple_norm"))
    (pe,) = _mm("ple_proj", [p], [ex.weight("ple_w_proj")], [F32], tm=1024, tn=1024)

    def ple_out(accs, ex):
        gate = _sigmoid(accs[0])
        return [ex[1] + gate * ex[0], gate]

    h4, gate = _mm("ple_gate", [n4], [ex.weight("ple_w_gate")], [F32, F32], extras=[(pe, "mn"), (h3, "mn")],
                   epilogue=ple_out, tm=1024, tn=1024)

    dh4, err2, gs["final_norm"] = _loss_head(h4, tgt, row("final_norm"))
    loss = (0.5 / D) * jnp.sum(err2)

    def ple_bwd(ins, ps):
        dh, gt, pe_ = ins
        return [dh * gt, dh * pe_ * gt * (1.0 - gt)], []

    dpe, dpg = _rowwise("ple_bwd", ple_bwd, [dh4, gate, pe], [], [(D, BF16), (D, BF16)])
    (d_ple_proj,) = _mm("ple_dproj", [p], [dpe], [BF16], ta=True, tm=256, tn=2048, tk=1024)
    (d_ple_gate,) = _mm("ple_dgate", [n4], [dpg], [BF16], ta=True, tm=1024, tn=1024, tk=2048)
    (dn4,) = _mm("ple_dn", [dpg], [ex.weight("ple_w_gate")], [F32], tb=True, tm=1024, tn=1024)
    (dh3, dh3b), gs["ple_norm"] = _rms_bwd("ple_dnorm", dn4, h3, row("ple_norm"), dres=dh4, copy_scale=0.5)
    (dh2, dh2b), gs["ffn2_norm"] = _ffn_bwd("ffn2", dh3, dh3b, h2, row("ffn2_norm"), ex, ffn2_saved, copy_scale=1.0)
    (dna,) = _mm("w_out_dna", [_to_attn_order(dh2b)], [w_out[:A]], [F32], tb=True, tm=1024, tn=1024)
    (dnb,) = _mm("w_out_dnb", [dh2b], [w_out[A:]], [F32], tb=True, tm=1024, tn=1024)
    (d_wout_a,) = _mm("w_out_dwa", [na], [dh2b], [BF16], ta=True, tm=1024, tn=1024, tk=2048)
    (d_wout_b,) = _mm("w_out_dwb", [nb], [dh2b], [BF16], ta=True, tm=1024, tn=1024, tk=2048)
    d_w_out = jnp.concatenate([d_wout_a, d_wout_b], axis=0)
    (dya,), gs["attn_out_norm"] = _rms_bwd("attn_out_dnorm", dna, ya, row("attn_out_norm"))
    (dyb,), gs["ssm_out_norm"] = _rms_bwd("ssm_out_dnorm", dnb, yb, row("ssm_out_norm"))

    dybp = _ssm_perm(dyb, T)

    def glu_bwd(ins, ps):
        dy, yg_, gl_ = ins
        sg = _sigmoid(gl_)
        dgl = dy * yg_ * sg * (1.0 - sg)
        return [dgl, dy * sg], [jnp.sum(dgl, axis=0, keepdims=True)]

    dgl, dyg_direct, gs["ssm_b_glu"] = _rowwise("ssm_glu_bwd", glu_bwd, [dybp, yg, gl], [], [(W, BF16), (W, F32)], accs=[W])
    (d_w_glu,) = _mm("ssm_dwglu", [ygb], [dgl], [BF16], ta=True, tm=1024, tn=1024, tk=2048)
    (dypre,) = _mm("ssm_dyg", [dgl], [w_glu], [F32], tb=True, extras=[(dyg_direct, "mn"), (ypre, "mn")],
                   epilogue=lambda accs, ex: [(accs[0] + ex[0]) * _gelu_grad(ex[1])], tm=1024, tn=1024)
    dufp, dbb, dcc, da, gs["ssm_d"] = _ssm_bwd(ufp, dypre, bb, bb.transpose(0, 2, 1), cc, cc.transpose(0, 2, 1),
                                               lam_dt, row("ssm_d"), hstart, T)
    ns = gsz * P
    from_bb = lambda t: _block_diag_take(t, gsz).transpose(0, 1, 3, 2).reshape(G * P, C)
    from_cc = lambda t: _block_diag_take(t, gsz).transpose(0, 1, 3, 2).reshape(w["ssm_c_re"].shape)
    gs["ssm_c_re"], gs["ssm_c_im"] = from_cc(dcc[:, :ns]), -from_cc(dcc[:, ns:])
    da = da.sum(axis=1)
    dar, dai = da[:, :ns].reshape(G * P, 1), da[:, ns:].reshape(G * P, 1)
    dlr, dli, dlogdt, dbr, dbi = _ssm_disc_bwd(col("ssm_lambda_re"), col("ssm_lambda_im"), logdt_x, b_re, b_im,
                                               dar, dai, from_bb(dbb[:, :, :ns]), from_bb(dbb[:, :, ns:]))
    gs["ssm_lambda_re"], gs["ssm_lambda_im"] = dlr.reshape(w["ssm_lambda_re"].shape), dli.reshape(w["ssm_lambda_im"].shape)
    gs["ssm_log_dt"] = dlogdt.reshape(G, P).sum(axis=1).reshape(w["ssm_log_dt"].shape)
    gs["ssm_b_re"], gs["ssm_b_im"] = dbr.reshape(w["ssm_b_re"].shape), dbi.reshape(w["ssm_b_im"].shape)
    ds_in = _ssm_unperm(dufp, T)

    dq, dk, dv = _attn_bwd(qkv, dya, ya, lse)
    dqkv = jnp.concatenate([dq, dk, dv], axis=1).astype(BF16)
    (d_w_qkv,) = _mm("w_in_dw_qkv", [n2p], [dqkv], [BF16], ta=True, tm=1024, tn=1024, tk=2048)
    (d_w_s,) = _mm("w_in_dw_ssm", [n2], [ds_in], [BF16], ta=True, tm=1024, tn=1024, tk=2048)
    d_w_in = jnp.concatenate([d_w_qkv, d_w_s], axis=1)
    dz = jnp.concatenate([_from_attn_order(dqkv), ds_in.astype(BF16)], axis=1)
    (dn2,) = _mm("w_in_dn", [dz], [w_in], [F32], tb=True, tm=1024, tn=1024)
    (dh1, dh1b), gs["mix_norm"] = _rms_bwd("mix_dnorm", dn2, h1, row("mix_norm"), dres=dh2, copy_scale=0.5)
    for k, g in (("ple_w_gate", d_ple_gate), ("ple_w_proj", d_ple_proj), ("w_out", d_w_out), ("ssm_w_glu", d_w_glu),
                 ("w_in", d_w_in)):
        ex.grad(k, g)
    (dx,), gs["ffn1_norm"] = _ffn_bwd("ffn1", dh1, dh1b, x, row("ffn1_norm"), ex, ffn1_saved, copy_scale=None)
    small = {k: gs[k].reshape(w[k].shape) for k in SMALL}
    return loss, dx, ex.finish(), small


def kernel(x, p, ffn1_norm, ffn1_w_gate, ffn1_w_up, ffn1_w_down, mix_norm, w_in, attn_out_norm, ssm_lambda_re, ssm_lambda_im, ssm_log_dt, ssm_b_re, ssm_b_im, ssm_c_re, ssm_c_im, ssm_d, ssm_w_glu, ssm_b_glu, ssm_out_norm, w_out, ffn2_norm, ffn2_w_gate, ffn2_w_up, ffn2_w_down, ple_norm, ple_w_gate, ple_w_proj, final_norm, loss_target, m_ffn1_norm, m_ffn1_w_gate, m_ffn1_w_up, m_ffn1_w_down, m_mix_norm, m_w_in, m_attn_out_norm, m_ssm_lambda_re, m_ssm_lambda_im, m_ssm_log_dt, m_ssm_b_re, m_ssm_b_im, m_ssm_c_re, m_ssm_c_im, m_ssm_d, m_ssm_w_glu, m_ssm_b_glu, m_ssm_out_norm, m_w_out, m_ffn2_norm, m_ffn2_w_gate, m_ffn2_w_up, m_ffn2_w_down, m_ple_norm, m_ple_w_gate, m_ple_w_proj, m_final_norm, v_ffn1_norm, v_ffn1_w_gate, v_ffn1_w_up, v_ffn1_w_down, v_mix_norm, v_w_in, v_attn_out_norm, v_ssm_lambda_re, v_ssm_lambda_im, v_ssm_log_dt, v_ssm_b_re, v_ssm_b_im, v_ssm_c_re, v_ssm_c_im, v_ssm_d, v_ssm_w_glu, v_ssm_b_glu, v_ssm_out_norm, v_w_out, v_ffn2_norm, v_ffn2_w_gate, v_ffn2_w_up, v_ffn2_w_down, v_ple_norm, v_ple_w_gate, v_ple_w_proj, v_final_norm):
    args = locals()
    w = {k: args[k] for k in WEIGHTS}
    m = {k: args["m_" + k] for k in WEIGHTS}
    v = {k: args["v_" + k] for k in WEIGHTS}
    w2 = {k: w[k].reshape(w[k].shape[-2:]) for k, _ in BIG}

    axes = [ax for _, ax in BIG]
    padded = {k: -(-w2[k].shape[ax] // LANES) * LANES for k, ax in BIG}
    shards = [_pad_to(w2[k].astype(BF16), ax, padded[k]) for k, ax in BIG]
    ex = _Exchange(dict(zip([k for k, _ in BIG], shards)), dict(BIG))
    loss_local, dx, summed, gsmall = _local_step(x[0], p[0, 0], loss_target[0], w, ex)
    loss = lax.psum(loss_local, MESH_AXES)
    n_small = sum(w[k].size for k in SMALL)
    rows = -(-n_small // (2 * SUBLANES * LANES)) * 2 * SUBLANES
    gs_sum = _all_reduce_small(_pack([gsmall[k] for k in SMALL], rows))

    grads, delta, new_m, new_v = {}, {}, {}, {}
    for k, ax in BIG:
        gfull = summed[k]
        g2 = lax.slice_in_dim(gfull, 0, w2[k].shape[ax], axis=ax)
        d2, nm2, nv2 = _adamw("adamw_" + k, w2[k], g2, m[k].reshape(w2[k].shape), v[k].reshape(w2[k].shape))
        grads[k], delta[k], new_m[k], new_v[k] = (t.reshape(w[k].shape) for t in (g2, d2, nm2, nv2))
    small_like = [w[k] for k in SMALL]
    ds, nms, nvs = _adamw("adamw_small", _pack(small_like, rows), gs_sum, _pack([m[k] for k in SMALL], rows),
                          _pack([v[k] for k in SMALL], rows))
    for k, g_, d_, nm_, nv_ in zip(SMALL, _unpack(gs_sum, small_like), _unpack(ds, small_like),
                                   _unpack(nms, small_like), _unpack(nvs, small_like)):
        grads[k], delta[k], new_m[k], new_v[k] = g_, d_, nm_, nv_

    return (loss, dx[None], *[grads[k] for k in WEIGHTS], *[delta[k] for k in WEIGHTS],
            *[new_m[k] for k in WEIGHTS], *[new_v[k] for k in WEIGHTS])
```

```python
import functools
import math

import jax
import jax.numpy as jnp
from jax import lax
from jax.experimental import pallas as pl
from jax.experimental.pallas import tpu as pltpu

F32 = jnp.float32
BF16 = jnp.bfloat16
MESH = pl.DeviceIdType.MESH
MESH_AXES = ("x", "y", "c")
N_CHIPS = 4
N_DEV = 8

V7X_VMEM_LIMIT_BYTES = 56 << 20
LANES = 128
SUBLANES = 8

HEAD_DIM = 64
SWA_BLOCK = 128
DILATIONS = (1, 4, 16)
SSM_BLOCK_GROUPS = 8
NORM_EPS = 1e-6
MASK_VALUE = -1e30

ADAM_LR = 0.001
ADAM_B1 = 0.9
ADAM_B2 = 0.999
ADAM_EPS = 1e-08
ADAM_WD = 0.01
ADAM_STEP = 10

GELU_C = math.sqrt(2.0 / math.pi)
GELU_K = 0.044715


def _pcall(body, **kw):
    return pl.pallas_call(body, **kw)


def _params(*sem):
    return pltpu.CompilerParams(dimension_semantics=sem, vmem_limit_bytes=V7X_VMEM_LIMIT_BYTES)


def _tile(n, target, align):
    best = None
    for t in range(align, min(n, target) + 1, align):
        if n % t == 0:
            best = t
    return n if best is None else best


def _sigmoid(x):
    return 0.5 * jnp.tanh(0.5 * x) + 0.5


class _Host:
    def __init__(self, ins, out_shapes, n_sem, start, wait):
        self.ins, self.out_shapes, self.n_sem, self.start, self.wait = ins, out_shapes, n_sem, start, wait


def _mm(name, lhs, rhs, outs, pairs=((0, 0, 0),), epilogue=None, extras=(), ta=False, tb=False,
        tm=1024, tn=512, tk=2048, host=None):
    nl, nr, ne, no = len(lhs), len(rhs), len(extras), len(outs)
    nhi, nho = (len(host.ins), len(host.out_shapes)) if host else (0, 0)
    n_acc = 1 + max(p[2] for p in pairs)
    (K, M) = lhs[0].shape if ta else lhs[0].shape[::-1]
    (N, K2) = rhs[0].shape if tb else rhs[0].shape[::-1]
    assert K == K2, (name, lhs[0].shape, rhs[0].shape)
    tm, tn, tk = _tile(M, tm, LANES), _tile(N, tn, LANES), _tile(K, tk, LANES)
    ni, nj, nk = M // tm, N // tn, K // tk
    n_scr = n_acc if nk > 1 else 0
    if epilogue is None:
        epilogue = lambda accs, ex: accs
    dn = (((0 if ta else 1,), (1 if tb else 0,)), ((), ()))

    def body(*refs):
        refs = list(refs)
        take = lambda n: [refs.pop(0) for _ in range(n)]
        l, r, e, hin, o, hout, acc = take(nl), take(nr), take(ne), take(nhi), take(no), take(nho), take(n_scr)
        i, j, k = pl.program_id(0), pl.program_id(1), pl.program_id(2)
        if host:
            @pl.when((i == 0) & (j == 0) & (k == 0))
            def _():
                host.start(hin, hout, *refs)

        parts = [None] * n_acc
        for li, ri, ai in pairs:
            d = lax.dot_general(l[li][...].astype(BF16), r[ri][...].astype(BF16), dn,
                                preferred_element_type=F32)
            parts[ai] = d if parts[ai] is None else parts[ai] + d

        def finish(accs):
            res = epilogue(accs, [x[...] for x in e])
            for ref, val in zip(o, res):
                ref[...] = val.astype(ref.dtype)

        if nk == 1:
            finish(parts)
        else:
            @pl.when(k == 0)
            def _():
                for ai in range(n_acc):
                    acc[ai][...] = parts[ai]

            @pl.when(k > 0)
            def _():
                for ai in range(n_acc):
                    acc[ai][...] += parts[ai]

            @pl.when(k == nk - 1)
            def _():
                finish([a[...] for a in acc])

        if host:
            @pl.when((i == ni - 1) & (j == nj - 1) & (k == nk - 1))
            def _():
                host.wait(hin, hout, *refs)

    lspec = pl.BlockSpec((tk, tm), lambda i, j, k: (k, i)) if ta else pl.BlockSpec((tm, tk), lambda i, j, k: (i, k))
    rspec = pl.BlockSpec((tn, tk), lambda i, j, k: (j, k)) if tb else pl.BlockSpec((tk, tn), lambda i, j, k: (k, j))
    especs = []
    for arr, kind in extras:
        if kind == "mn":
            especs.append(pl.BlockSpec((tm, tn), lambda i, j, k: (i, j)))
        elif kind == "n":
            especs.append(pl.BlockSpec((1, tn), lambda i, j, k: (0, j)))
        else:
            especs.append(pl.BlockSpec((tm, 1), lambda i, j, k: (i, 0)))
    any_spec = pl.BlockSpec(memory_space=pl.ANY)
    sems = [pltpu.SemaphoreType.DMA((host.n_sem,)), pltpu.SemaphoreType.DMA((host.n_sem,))] if host else []
    res = _pcall(
        body, name=name,
        grid=(ni, nj, nk),
        in_specs=[lspec] * nl + [rspec] * nr + especs + [any_spec] * nhi,
        out_specs=[pl.BlockSpec((tm, tn), lambda i, j, k: (i, j))] * no + [any_spec] * nho,
        out_shape=[jax.ShapeDtypeStruct((M, N), dt) for dt in outs] + (list(host.out_shapes) if host else []),
        scratch_shapes=[pltpu.VMEM((tm, tn), F32)] * n_scr + sems,
        compiler_params=_params(*(("arbitrary",) * 3 if host else ("parallel", "parallel", "arbitrary"))),
    )(*lhs, *rhs, *[a for a, _ in extras], *(host.ins if host else []))
    return res


ROWWISE_BLOCK_BYTES = 20 << 20


def _rowwise(name, fn, ins, params, outs, accs=(), ts=None):
    S = ins[0].shape[0]
    if ts is None:
        row_bytes = sum(a.shape[1] * a.dtype.itemsize for a in ins) + sum(w * jnp.dtype(dt).itemsize for w, dt in outs)
        ts = next((t for t in (1024, 512) if t * row_bytes <= ROWWISE_BLOCK_BYTES), 256)
    ts = _tile(S, ts, 16)
    ni, npar, no, na = len(ins), len(params), len(outs), len(accs)

    def body(*refs):
        i_refs, p_refs = refs[:ni], refs[ni:ni + npar]
        o_refs = refs[ni + npar:ni + npar + no]
        a_refs = refs[ni + npar + no:]
        res_o, res_a = fn([r[...] for r in i_refs], [r[...] for r in p_refs])
        for ref, val in zip(o_refs, res_o):
            ref[...] = val.astype(ref.dtype)
        if na:
            @pl.when(pl.program_id(0) == 0)
            def _():
                for ref in a_refs:
                    ref[...] = jnp.zeros(ref.shape, F32)

            for ref, val in zip(a_refs, res_a):
                ref[...] += val

    res = _pcall(
        body, name=name,
        grid=(S // ts,),
        in_specs=[pl.BlockSpec((ts, a.shape[1]), lambda i: (i, 0)) for a in ins]
        + [pl.BlockSpec(p.shape, lambda i: (0, 0)) for p in params],
        out_specs=[pl.BlockSpec((ts, w), lambda i: (i, 0)) for w, _ in outs]
        + [pl.BlockSpec((1, w), lambda i: (0, 0)) for w in accs],
        out_shape=[jax.ShapeDtypeStruct((S, w), dt) for w, dt in outs]
        + [jax.ShapeDtypeStruct((1, w), F32) for w in accs],
        compiler_params=_params("arbitrary"),
    )(*ins, *params)
    return res


def _xhat(x):
    r = lax.rsqrt(jnp.mean(x * x, axis=-1, keepdims=True) + NORM_EPS)
    return x * r, r


def _rms_fwd(name, x, g):
    def fn(ins, ps):
        xh, _ = _xhat(ins[0])
        return [xh * ps[0]], []

    return _rowwise(name, fn, [x], [g], [(x.shape[1], BF16)])[0]


def _rms_bwd(name, dn, x, g, dres=None, copy_scale=None):
    w = x.shape[1]

    def fn(ins, ps):
        dn_, x_ = ins[0], ins[1]
        xh, r = _xhat(x_)
        dxh = dn_ * ps[0]
        dx = r * (dxh - xh * jnp.mean(dxh * xh, axis=-1, keepdims=True))
        if dres is not None:
            dx = dx + ins[2]
        o = [dx] + ([dx * copy_scale] if copy_scale is not None else [])
        return o, [jnp.sum(dn_ * xh, axis=0, keepdims=True)]

    ins = [dn, x] + ([dres] if dres is not None else [])
    outs = [(w, F32)] + ([(w, BF16)] if copy_scale is not None else [])
    res = _rowwise(name, fn, ins, [g], outs, accs=[w])
    return res[:-1], res[-1]


def _swiglu_epilogue(accs, ex):
    g, u = accs
    sg = _sigmoid(g)
    s = g * sg
    return [u * (sg + s * (1.0 - sg)), s, s * u]


def _dswiglu_epilogue(accs, ex):
    da = accs[0]
    return [da * ex[0].astype(F32), da * ex[1].astype(F32)]


def _carried(ex, kind, kernel):
    if kernel not in (ex.AG_PLAN if kind == "ag" else ex.RS_PLAN):
        return None, lambda outs: None
    if kind == "ag":
        return ex.ag_host(kernel), lambda outs: ex.ag_done(kernel, outs)
    return ex.rs_host(kernel), lambda outs: ex.rs_done(kernel, outs)


def _ffn_fwd(tag, h, gnorm, ex):
    n = _rms_fwd(tag + "_norm", h, gnorm)
    host, done = _carried(ex, "ag", tag + "_up")
    g, u, a, *outs = _mm(tag + "_up", [n], [ex.weight(tag + "_w_gate"), ex.weight(tag + "_w_up")], [BF16, BF16, BF16],
                         pairs=((0, 0, 0), (0, 1, 1)), epilogue=_swiglu_epilogue, tm=1024, tn=512, host=host)
    done(outs)
    host, done = _carried(ex, "ag", tag + "_down")
    hout, *outs = _mm(tag + "_down", [a], [ex.weight(tag + "_w_down")], [F32], extras=[(h, "mn")],
                      epilogue=lambda accs, ex_: [ex_[0] + 0.5 * accs[0]], tm=512, tn=1024, tk=8192, host=host)
    done(outs)
    return hout, (n, g, u, a)


def _ffn_bwd(tag, dh, dhb_half, h, gnorm, ex, saved, copy_scale):
    n, g, u, a = saved
    wg, wu, wd = (ex.weight(tag + k) for k in ("_w_gate", "_w_up", "_w_down"))
    host, done = _carried(ex, "rs", tag + "_dact")
    dg, du, *outs = _mm(tag + "_dact", [dhb_half], [wd], [BF16, BF16], tb=True, extras=[(g, "mn"), (u, "mn")],
                        epilogue=_dswiglu_epilogue, tm=1024, tn=512, host=host)
    done(outs)
    host, done = _carried(ex, "rs", tag + "_dwd")
    dwd, *outs = _mm(tag + "_dwd", [a], [dhb_half], [BF16], ta=True, tm=512, tn=2048, tk=2048, host=host)
    done(outs)
    ex.grad(tag + "_w_down", dwd)
    host, done = _carried(ex, "rs", tag + "_dwgu")
    dwg, dwu, *outs = _mm(tag + "_dwgu", [n], [dg, du], [BF16, BF16], pairs=((0, 0, 0), (0, 1, 1)), ta=True,
                          tm=1024, tn=512, tk=2048, host=host)
    done(outs)
    ex.grad(tag + "_w_gate", dwg)
    ex.grad(tag + "_w_up", dwu)
    host, done = _carried(ex, "rs", tag + "_dn")
    dn, *outs = _mm(tag + "_dn", [dg, du], [wg, wu], [F32], pairs=((0, 0, 0), (1, 1, 0)), tb=True,
                    tm=1024, tn=1024, tk=1408, host=host)
    done(outs)
    return _rms_bwd(tag + "_dnorm", dn, h, gnorm, dres=dh, copy_scale=copy_scale)


ATTN_HEAD_PAIRS = 8


def _to_attn_order(a):
    S, w = a.shape
    return a.reshape(S // 16, 16, w).transpose(1, 0, 2).reshape(S, w)


def _from_attn_order(a):
    S, w = a.shape
    return a.reshape(16, S // 16, w).transpose(1, 0, 2).reshape(S, w)


def _attn_geom(S, d):
    s16 = S // 16
    if d == 16:
        return (16, s16), (1, SWA_BLOCK), (lambda r, b: (r, b)), 16, s16 // SWA_BLOCK
    if d == 4:
        return (4, 4, s16), (4, 1, SWA_BLOCK // 4), (lambda r, b: (0, r, b)), 4, s16 // (SWA_BLOCK // 4)
    return (16, s16), (16, SWA_BLOCK // 16), (lambda r, b: (0, b)), 1, s16 // (SWA_BLOCK // 16)


def _attn_pos(rho, d):
    if d == 16:
        return rho
    if d == 4:
        return 4 * (rho & 31) + (rho >> 5)
    return 16 * (rho & 7) + (rho >> 3)


def _attn_spec(S, d, lb, col, shift=0):
    _, blk, idx, _, nb = _attn_geom(S, d)
    return pl.BlockSpec(blk + (lb,), lambda r, cb, b: idx(r, jnp.clip(b + shift, 0, nb - 1)) + (col(cb),))


def _attn_view(a, d):
    return a.reshape(_attn_geom(a.shape[0], d)[0] + (a.shape[1],))


def _attn_valid(d):
    qp = _attn_pos(lax.broadcasted_iota(jnp.int32, (SWA_BLOCK, 2 * SWA_BLOCK), 0), d)
    kk = lax.broadcasted_iota(jnp.int32, (SWA_BLOCK, 2 * SWA_BLOCK), 1)
    kp = _attn_pos(kk & (SWA_BLOCK - 1), d)
    is_prev = kk < SWA_BLOCK
    return qp, kp, is_prev


def _head_masks(rows=SWA_BLOCK):
    lane = lax.broadcasted_iota(jnp.int32, (rows, LANES), 1)
    return [lane < HEAD_DIM, lane >= HEAD_DIM]


def _attn_ld(ref, sl):
    t = ref[(slice(None),) * (len(ref.shape) - 1) + (sl,)]
    return t.reshape(-1, t.shape[-1])


def _attn_st(ref, sl, val):
    ref[(slice(None),) * (len(ref.shape) - 1) + (sl,)] = val.reshape(ref.shape[:-1] + (val.shape[-1],))


def _per_head(t, first):
    sw = pltpu.roll(t, HEAD_DIM, 1)
    lo = lax.broadcasted_iota(jnp.int32, t.shape, 1) < HEAD_DIM
    return jnp.where(lo, t, sw) if first else jnp.where(lo, sw, t)


def _dot_nt(a, b):
    return lax.dot_general(a, b, (((1,), (1,)), ((), ())), preferred_element_type=F32)


def _dot_tn(a, b):
    return lax.dot_general(a, b, (((0,), (0,)), ((), ())), preferred_element_type=F32)


def _dot(a, b):
    return jnp.dot(a, b, preferred_element_type=F32)


def _keep(mask, t):
    return jnp.where(mask, t.astype(F32), 0.0).astype(BF16)


def _attn_cols(A):
    lb = min(A, LANES * ATTN_HEAD_PAIRS)
    ncol = A // lb
    return lb, ncol, [lambda cb, part=part: part * ncol + cb for part in range(3)], (lambda cb: cb)


def _attn_fwd_stage(name, qkv, d, prev, final, host=None):
    S, A3 = qkv.shape
    A = A3 // 3
    lb, ncol, (cq, ck, cv), ca = _attn_cols(A)
    view, _, _, nres, nb = _attn_geom(S, d)
    scale = HEAD_DIM ** -0.5
    has_prev = prev is not None
    n_out = 2 if final else 3
    nhi, nho = (len(host.ins), len(host.out_shapes)) if host else (0, 0)

    def body(*refs):
        q_ref, kp_ref, kc_ref, vp_ref, vc_ref = refs[:5]
        p_refs = refs[5:8] if has_prev else ()
        n_in = 5 + len(p_refs)
        hin, o_refs = refs[n_in:n_in + nhi], refs[n_in + nhi:n_in + nhi + n_out]
        hout, sems = refs[n_in + nhi + n_out:n_in + nhi + n_out + nho], refs[n_in + nhi + n_out + nho:]
        b = pl.program_id(2)
        if host:
            @pl.when((pl.program_id(0) == 0) & (pl.program_id(1) == 0) & (b == 0))
            def _():
                host.start(hin, hout, *sems)
        qp, kp_, is_prev = _attn_valid(d)
        valid = (is_prev & (kp_ >= qp) & (b > 0)) | (jnp.logical_not(is_prev) & (kp_ <= qp))
        hm, hm2 = _head_masks(), _head_masks(2 * SWA_BLOCK)
        for hp in range(lb // LANES):
            sl = slice(hp * LANES, (hp + 1) * LANES)
            q = _attn_ld(q_ref, sl)
            k2 = jnp.concatenate([_attn_ld(kp_ref, sl), _attn_ld(kc_ref, sl)], axis=0).astype(BF16)
            v2 = jnp.concatenate([_attn_ld(vp_ref, sl), _attn_ld(vc_ref, sl)], axis=0)
            o = jnp.zeros((SWA_BLOCK, LANES), F32)
            m = jnp.zeros((SWA_BLOCK, LANES), F32)
            l = jnp.zeros((SWA_BLOCK, LANES), F32)
            for hh in range(2):
                s = jnp.where(valid, _dot_nt(_keep(hm[hh], q), k2) * scale, MASK_VALUE)
                mh = jnp.max(s, axis=-1, keepdims=True)
                p = jnp.exp(s - mh)
                lh = jnp.sum(p, axis=-1, keepdims=True)
                o = o + _dot(p.astype(BF16), _keep(hm2[hh], v2))
                m = jnp.where(hm[hh], mh, m)
                l = jnp.where(hm[hh], lh, l)
            if has_prev:
                po, pm, pl_ = (_attn_ld(r, sl) for r in p_refs)
                mn = jnp.maximum(m, pm)
                w_new, w_old = jnp.exp(m - mn), jnp.exp(pm - mn)
                o = o * w_new + po * w_old
                l = l * w_new + pl_ * w_old
                m = mn
            if final:
                _attn_st(o_refs[0], sl, o / l)
                _attn_st(o_refs[1], sl, m + jnp.log(l))
            else:
                _attn_st(o_refs[0], sl, o)
                _attn_st(o_refs[1], sl, m)
                _attn_st(o_refs[2], sl, l)

        if host:
            @pl.when((pl.program_id(0) == nres - 1) & (pl.program_id(1) == ncol - 1) & (b == nb - 1))
            def _():
                host.wait(hin, hout, *sems)

    qk = _attn_view(qkv, d)
    prev_v = [_attn_view(t, d) for t in prev] if has_prev else []
    sp = functools.partial(_attn_spec, S, d, lb)
    any_spec = pl.BlockSpec(memory_space=pl.ANY)
    res = _pcall(
        body, name=name,
        grid=(nres, ncol, nb),
        in_specs=[sp(cq), sp(ck, -1), sp(ck), sp(cv, -1), sp(cv)] + [sp(ca)] * len(prev_v) + [any_spec] * nhi,
        out_specs=[sp(ca)] * n_out + [any_spec] * nho,
        out_shape=[jax.ShapeDtypeStruct(view + (A,), F32)] * n_out + (list(host.out_shapes) if host else []),
        scratch_shapes=[pltpu.SemaphoreType.DMA((host.n_sem,)), pltpu.SemaphoreType.DMA((host.n_sem,))] if host else [],
        compiler_params=_params(*(("arbitrary",) * 3 if host else ("parallel", "parallel", "arbitrary"))),
    )(qk, qk, qk, qk, qk, *prev_v, *(host.ins if host else []))
    return [t.reshape(S, A) for t in res[:n_out]], res[n_out:]


def _attn_fwd(qkv, ex):
    st = None
    for i, d in enumerate(DILATIONS):
        name = "attn_fwd_d%d" % d
        host, done = _carried(ex, "ag", name)
        st, outs = _attn_fwd_stage(name, qkv, d, st, final=(i == len(DILATIONS) - 1), host=host)
        done(outs)
    return st


def _attn_delta(dya, ya):
    S, A = ya.shape
    ri = lax.broadcasted_iota(jnp.int32, (A, A), 0) // HEAD_DIM
    ci = lax.broadcasted_iota(jnp.int32, (A, A), 1) // HEAD_DIM
    ones_bd = (ri == ci).astype(BF16)

    def fn(ins, ps):
        prod = ins[0] * ins[1]
        hi = prod.astype(BF16)
        lo = (prod - hi.astype(F32)).astype(BF16)
        return [_dot(hi, ps[0]) + _dot(lo, ps[0])], []

    return _rowwise("attn_delta", fn, [dya, ya], [ones_bd], [(A, F32)])[0]


def _attn_bwd_stage(name, qkv, do, lse, delta, d, prev):
    S, A3 = qkv.shape
    A = A3 // 3
    lb, ncol, (cq, ck, cv), ca = _attn_cols(A)
    view, _, _, nres, nb = _attn_geom(S, d)
    scale = HEAD_DIM ** -0.5
    has_prev = prev is not None
    lane_slices = [slice(hp * LANES, (hp + 1) * LANES) for hp in range(lb // LANES)]

    def body(*refs):
        q_ref, kp_ref, kc_ref, vp_ref, vc_ref, do_ref, lse_ref, dl_ref = refs[:8]
        p_refs = refs[8:11] if has_prev else ()
        dq_ref, dk_ref, dv_ref, dk_c, dv_c = refs[8 + len(p_refs):]
        b = pl.program_id(2)

        def put_keys(sl, dk, dv):
            if has_prev:
                dk, dv = dk + _attn_ld(p_refs[1], sl), dv + _attn_ld(p_refs[2], sl)
            _attn_st(dk_ref, sl, dk)
            _attn_st(dv_ref, sl, dv)

        @pl.when(b == 0)
        def _():
            dk_c[...] = jnp.zeros(dk_c.shape, F32)
            dv_c[...] = jnp.zeros(dv_c.shape, F32)

        @pl.when(b < nb)
        def _():
            qp, kp_, is_prev = _attn_valid(d)
            valid = (is_prev & (kp_ >= qp) & (b > 0)) | (jnp.logical_not(is_prev) & (kp_ <= qp))
            hm, hm2 = _head_masks(), _head_masks(2 * SWA_BLOCK)
            for sl in lane_slices:
                q, do_, lse_, dl_ = (_attn_ld(r, sl) for r in (q_ref, do_ref, lse_ref, dl_ref))
                k2 = jnp.concatenate([_attn_ld(kp_ref, sl), _attn_ld(kc_ref, sl)], axis=0)
                v2 = jnp.concatenate([_attn_ld(vp_ref, sl), _attn_ld(vc_ref, sl)], axis=0).astype(BF16)
                k2b = k2.astype(BF16)
                dq = jnp.zeros((SWA_BLOCK, LANES), F32)
                dk2 = jnp.zeros((2 * SWA_BLOCK, LANES), F32)
                dv2 = jnp.zeros((2 * SWA_BLOCK, LANES), F32)
                for hh in range(2):
                    qh, doh = _keep(hm[hh], q), _keep(hm[hh], do_)
                    lh, dh = _per_head(lse_, hh == 0), _per_head(dl_, hh == 0)
                    lh2, dh2 = jnp.concatenate([lh, lh], axis=1), jnp.concatenate([dh, dh], axis=1)
                    p = jnp.where(valid, jnp.exp(_dot_nt(qh, k2b) * scale - lh2), 0.0)
                    ds = (p * (_dot_nt(doh, v2) - dh2)).astype(BF16)
                    dq = dq + _dot(ds, _keep(hm2[hh], k2))
                    dk2 = dk2 + _dot_tn(ds, qh)
                    dv2 = dv2 + _dot_tn(p.astype(BF16), doh)
                dq, dk2 = dq * scale, dk2 * scale
                if has_prev:
                    dq = dq + _attn_ld(p_refs[0], sl)
                _attn_st(dq_ref, sl, dq)
                put_keys(sl, dk_c[:, sl] + dk2[:SWA_BLOCK], dv_c[:, sl] + dv2[:SWA_BLOCK])
                dk_c[:, sl] = dk2[SWA_BLOCK:]
                dv_c[:, sl] = dv2[SWA_BLOCK:]

        @pl.when(b == nb)
        def _():
            for sl in lane_slices:
                put_keys(sl, dk_c[:, sl], dv_c[:, sl])

    qk = _attn_view(qkv, d)
    acts = [_attn_view(t, d) for t in (do, lse, delta)] + ([_attn_view(t, d) for t in prev] if has_prev else [])
    sp = functools.partial(_attn_spec, S, d, lb)
    res = _pcall(
        body, name=name,
        grid=(nres, ncol, nb + 1),
        in_specs=[sp(cq), sp(ck, -1), sp(ck), sp(cv, -1), sp(cv), sp(ca), sp(ca), sp(ca)]
        + ([sp(ca), sp(ca, -1), sp(ca, -1)] if has_prev else []),
        out_specs=[sp(ca), sp(ca, -1), sp(ca, -1)],
        out_shape=[jax.ShapeDtypeStruct(view + (A,), F32)] * 3,
        scratch_shapes=[pltpu.VMEM((SWA_BLOCK, lb), F32)] * 2,
        compiler_params=_params("parallel", "parallel", "arbitrary"),
    )(qk, qk, qk, qk, qk, *acts)
    return [t.reshape(S, A) for t in res]


def _attn_bwd(qkv, dya, ya, lse):
    delta = _attn_delta(dya, ya)
    sums = None
    for d in DILATIONS:
        sums = _attn_bwd_stage("attn_bwd_d%d" % d, qkv, dya, lse, delta, d, sums)
    return sums


def _ssm_perm(a, T):
    S, w = a.shape
    return a.reshape(S // T, SUBLANES, T // SUBLANES, w).transpose(0, 2, 1, 3).reshape(S, w)


def _ssm_unperm(a, T):
    S, w = a.shape
    return a.reshape(S // T, T // SUBLANES, SUBLANES, w).transpose(0, 2, 1, 3).reshape(S, w)


def _ssm_powers(lam_ref, pw_ref, T, ns):
    tc = T // SUBLANES
    n = (lax.broadcasted_iota(jnp.int32, (tc, 1), 0) + 1).astype(F32)
    mag = jnp.exp(n * lam_ref[0, 0:1, :])
    ang = n * lam_ref[0, 1:2, :]
    rows8 = lambda t: jnp.broadcast_to(t[:, None, :], (tc, SUBLANES, ns)).reshape(T, ns)
    pw_ref[:, 0:ns] = rows8(mag * jnp.cos(ang))
    pw_ref[:, ns:2 * ns] = rows8(mag * jnp.sin(ang))


def _ssm_scan(xs, off, pw_ref, carry_ref, T, ns, reverse):
    Tc = T // SUBLANES
    sgn = -1.0 if reverse else 1.0
    ar, ai = pw_ref[0:SUBLANES, 0:ns], sgn * pw_ref[0:SUBLANES, ns:2 * ns]

    def rows(i):
        return pl.ds(pl.multiple_of(off + i * SUBLANES, SUBLANES), SUBLANES)

    def step(k, h):
        hr, hi = h
        r = rows(Tc - 1 - k if reverse else k)
        nr = ar * hr - ai * hi + xs[r, 0:ns]
        ni = ar * hi + ai * hr + xs[r, ns:2 * ns]
        xs[r, 0:ns] = nr
        xs[r, ns:2 * ns] = ni
        return nr, ni

    z = jnp.zeros((SUBLANES, ns), F32)
    er, ei = lax.fori_loop(0, Tc, step, (z, z), unroll=4)
    atr, ati = pw_ref[T - SUBLANES:T, 0:ns], sgn * pw_ref[T - SUBLANES:T, ns:2 * ns]
    rowid = lax.broadcasted_iota(jnp.int32, (SUBLANES, ns), 0)
    cr, ci = carry_ref[:, 0:ns], carry_ref[:, ns:2 * ns]
    ctr, cti = z, z
    for jj in range(SUBLANES):
        j = SUBLANES - 1 - jj if reverse else jj
        sel = rowid == j
        ctr, cti = jnp.where(sel, cr, ctr), jnp.where(sel, ci, cti)
        ejr = jnp.broadcast_to(jnp.sum(jnp.where(sel, er, 0.0), axis=0, keepdims=True), (SUBLANES, ns))
        eji = jnp.broadcast_to(jnp.sum(jnp.where(sel, ei, 0.0), axis=0, keepdims=True), (SUBLANES, ns))
        cr, ci = ejr + atr * cr - ati * ci, eji + atr * ci + ati * cr
    carry_ref[:, 0:ns] = cr
    carry_ref[:, ns:2 * ns] = ci

    def fix(i, _):
        r = rows(i)
        pr_rows = pl.ds(pl.multiple_of((Tc - 1 - i if reverse else i) * SUBLANES, SUBLANES), SUBLANES)
        pr, pi = pw_ref[pr_rows, 0:ns], sgn * pw_ref[pr_rows, ns:2 * ns]
        xs[r, 0:ns] += pr * ctr - pi * cti
        xs[r, ns:2 * ns] += pr * cti + pi * ctr
        return 0

    lax.fori_loop(0, Tc, fix, 0, unroll=4)
    return ctr, cti


def _ssm_fwd(ufp, bb, cc, lam_dt, drow, T):
    S, W = ufp.shape
    GB, cw, ns2 = bb.shape
    ns = ns2 // 2
    NCH = S // T

    def body(uf_ref, bb_ref, cc_ref, lam_ref, d_ref, y_ref, hs_ref, xs, pw, carry):
        @pl.when(pl.program_id(1) == 0)
        def _():
            _ssm_powers(lam_ref, pw, T, ns)
            carry[...] = jnp.zeros(carry.shape, F32)

        uf = uf_ref[...]
        xs[...] = _dot(uf.astype(BF16), bb_ref[0])
        hs_ref[0, 0] = carry[...]
        _ssm_scan(xs, 0, pw, carry, T, ns, reverse=False)
        y_ref[...] = _dot(xs[...].astype(BF16), cc_ref[0]) + d_ref[...] * uf

    return _pcall(
        body, name="ssm_fwd",
        grid=(GB, NCH),
        in_specs=[pl.BlockSpec((T, cw), lambda g, c: (c, g)),
                  pl.BlockSpec((1, cw, ns2), lambda g, c: (g, 0, 0)),
                  pl.BlockSpec((1, ns2, cw), lambda g, c: (g, 0, 0)),
                  pl.BlockSpec((1, 2, ns), lambda g, c: (g, 0, 0)),
                  pl.BlockSpec((1, cw), lambda g, c: (0, g))],
        out_specs=[pl.BlockSpec((T, cw), lambda g, c: (c, g)),
                   pl.BlockSpec((1, 1, SUBLANES, ns2), lambda g, c: (g, c, 0, 0))],
        out_shape=[jax.ShapeDtypeStruct((S, W), F32),
                   jax.ShapeDtypeStruct((GB, NCH, SUBLANES, ns2), F32)],
        scratch_shapes=[pltpu.VMEM((T, ns2), F32), pltpu.VMEM((T, ns2), F32), pltpu.VMEM((SUBLANES, ns2), F32)],
        compiler_params=_params("arbitrary", "arbitrary"),
    )(ufp, bb, cc, lam_dt, drow)


def _ssm_bwd(ufp, dyp, bb, bbt, cc, cct, lam_dt, drow, hstart, T):
    S, W = ufp.shape
    GB, cw, ns2 = bb.shape
    ns = ns2 // 2
    NCH = S // T

    def body(uf_ref, dy_ref, bb_ref, bbt_ref, cc_ref, cct_ref, lam_ref, d_ref, hs_ref,
             duf_ref, dbb_ref, dcc_ref, da_ref, dd_ref, hb, ls, pw, carry_f, carry_b):
        @pl.when(pl.program_id(1) == 0)
        def _():
            _ssm_powers(lam_ref, pw, T, ns)
            carry_b[...] = jnp.zeros(carry_b.shape, F32)
            dbb_ref[...] = jnp.zeros(dbb_ref.shape, F32)
            dcc_ref[...] = jnp.zeros(dcc_ref.shape, F32)
            da_ref[...] = jnp.zeros(da_ref.shape, F32)
            dd_ref[...] = jnp.zeros(dd_ref.shape, F32)

        uf, dy = uf_ref[...], dy_ref[...]
        ufb, dyb = uf.astype(BF16), dy.astype(BF16)
        hb[SUBLANES:T + SUBLANES, :] = _dot(ufb, bb_ref[0])
        carry_f[...] = hs_ref[0, 0]
        ctr, cti = _ssm_scan(hb, SUBLANES, pw, carry_f, T, ns, reverse=False)
        hb[0:SUBLANES, 0:ns] = ctr
        hb[0:SUBLANES, ns:ns2] = cti
        ls[...] = _dot(dyb, cct_ref[0])
        _ssm_scan(ls, 0, pw, carry_b, T, ns, reverse=True)
        lv = ls[...]
        lb = lv.astype(BF16)
        dbb_ref[0] += _dot_tn(ufb, lb)
        dcc_ref[0] += _dot_tn(hb[SUBLANES:T + SUBLANES, :].astype(BF16), dyb)
        lr, li = lv[:, 0:ns], lv[:, ns:ns2]
        hpr, hpi = hb[0:T, 0:ns], hb[0:T, ns:ns2]
        dar = jnp.sum(lr * hpr + li * hpi, axis=0, keepdims=True)
        dai = jnp.sum(li * hpr - lr * hpi, axis=0, keepdims=True)
        da_ref[0, 0:1, 0:ns] += dar
        da_ref[0, 0:1, ns:ns2] += dai
        duf_ref[...] = _dot(lb, bbt_ref[0]) + d_ref[...] * dy
        dd_ref[...] += jnp.sum(dy * uf, axis=0, keepdims=True)

    rc = lambda c: NCH - 1 - c
    return _pcall(
        body, name="ssm_bwd",
        grid=(GB, NCH),
        in_specs=[pl.BlockSpec((T, cw), lambda g, c: (rc(c), g)),
                  pl.BlockSpec((T, cw), lambda g, c: (rc(c), g)),
                  pl.BlockSpec((1, cw, ns2), lambda g, c: (g, 0, 0)),
                  pl.BlockSpec((1, ns2, cw), lambda g, c: (g, 0, 0)),
                  pl.BlockSpec((1, ns2, cw), lambda g, c: (g, 0, 0)),
                  pl.BlockSpec((1, cw, ns2), lambda g, c: (g, 0, 0)),
                  pl.BlockSpec((1, 2, ns), lambda g, c: (g, 0, 0)),
                  pl.BlockSpec((1, cw), lambda g, c: (0, g)),
                  pl.BlockSpec((1, 1, SUBLANES, ns2), lambda g, c: (g, rc(c), 0, 0))],
        out_specs=[pl.BlockSpec((T, cw), lambda g, c: (rc(c), g)),
                   pl.BlockSpec((1, cw, ns2), lambda g, c: (g, 0, 0)),
                   pl.BlockSpec((1, ns2, cw), lambda g, c: (g, 0, 0)),
                   pl.BlockSpec((1, SUBLANES, ns2), lambda g, c: (g, 0, 0)),
                   pl.BlockSpec((1, cw), lambda g, c: (0, g))],
        out_shape=[jax.ShapeDtypeStruct((S, W), F32),
                   jax.ShapeDtypeStruct((GB, cw, ns2), F32),
                   jax.ShapeDtypeStruct((GB, ns2, cw), F32),
                   jax.ShapeDtypeStruct((GB, SUBLANES, ns2), F32),
                   jax.ShapeDtypeStruct((1, W), F32)],
        scratch_shapes=[pltpu.VMEM((T + SUBLANES, ns2), F32), pltpu.VMEM((T, ns2), F32), pltpu.VMEM((T, ns2), F32),
                        pltpu.VMEM((SUBLANES, ns2), F32), pltpu.VMEM((SUBLANES, ns2), F32)],
        compiler_params=_params("arbitrary", "arbitrary"),
    )(ufp, dyp, bb, bbt, cc, cct, lam_dt, drow, hstart)


def _ssm_disc_math(lr, li, logdt, br, bi):
    dt = jnp.exp(logdt)
    mag = jnp.exp(lr * dt)
    ar = mag * jnp.cos(li * dt)
    ai = mag * jnp.sin(li * dt)
    nr, ni = ar - 1.0, ai
    den = lr * lr + li * li
    cr = (nr * lr + ni * li) / den
    ci = (ni * lr - nr * li) / den
    return ar, ai, cr * br - ci * bi, cr * bi + ci * br


def _ssm_disc(lr, li, logdt, br, bi):
    C = br.shape[1]

    def fn(ins, ps):
        _, _, bbr, bbi = _ssm_disc_math(*ins)
        dt = jnp.exp(ins[2])
        return [ins[0] * dt, ins[1] * dt, bbr, bbi], []

    return _rowwise("ssm_disc", fn, [lr, li, logdt, br, bi], [], [(1, F32), (1, F32), (C, F32), (C, F32)], ts=512)


def _ssm_disc_bwd(lr, li, logdt, br, bi, dar, dai, dbbr, dbbi):
    C = br.shape[1]

    def fn(ins, ps):
        _, vjp = jax.vjp(_ssm_disc_math, *ins[:5])
        return list(vjp(tuple(ins[5:]))), []

    return _rowwise("ssm_disc_bwd", fn, [lr, li, logdt, br, bi, dar, dai, dbbr, dbbi], [],
                    [(1, F32), (1, F32), (1, F32), (C, F32), (C, F32)], ts=512)


def _block_diag(t):
    GB, g, a, b = t.shape
    eye = jnp.eye(g, dtype=t.dtype)
    return (t[:, :, :, None, :] * eye[None, :, None, :, None]).reshape(GB, g * a, g * b)


def _block_diag_take(t, g):
    GB, ga, gb_ = t.shape
    a, b = ga // g, gb_ // g
    eye = jnp.eye(g, dtype=t.dtype)
    return (t.reshape(GB, g, a, g, b) * eye[None, :, None, :, None]).sum(axis=3)


def _loss_head(h4, tgt, gf):
    D = h4.shape[1]

    def fn(ins, ps):
        x, t = ins
        xh, r = _xhat(x)
        err = xh * ps[0] - t
        dn = err * (1.0 / D)
        dxh = dn * ps[0]
        dx = r * (dxh - xh * jnp.mean(dxh * xh, axis=-1, keepdims=True))
        return [dx], [jnp.sum(err * err, axis=0, keepdims=True), jnp.sum(dn * xh, axis=0, keepdims=True)]

    return _rowwise("loss_head", fn, [h4, tgt], [gf], [(D, F32)], accs=[D, D])


def _gelu(x):
    return 0.5 * x * (1.0 + jnp.tanh(GELU_C * (x + GELU_K * x * x * x)))


def _gelu_grad(x):
    t = jnp.tanh(GELU_C * (x + GELU_K * x * x * x))
    return 0.5 * (1.0 + t) + 0.5 * x * (1.0 - t * t) * GELU_C * (1.0 + 3.0 * GELU_K * x * x)


def _mesh_pos():
    return lax.axis_index("x"), lax.axis_index("y"), lax.axis_index("c")


def _other_chips(x, y):
    return [(1 - x, y), (x, 1 - y), (1 - x, 1 - y)]


def _remote(src, dst, send, recv, dev):
    return pltpu.make_async_remote_copy(src_ref=src, dst_ref=dst, send_sem=send, recv_sem=recv,
                                        device_id=dev, device_id_type=MESH)


ANY = pl.BlockSpec(memory_space=pl.ANY)


COMM_BLOCK_BYTES = 3 << 19


def _place():
    x, y, c = _mesh_pos()
    return jnp.stack([c] + [2 * cx + cy for cx, cy in _other_chips(x, y)] + [2 * x + y]).astype(jnp.int32)


def _send_chips(name, srcs, specs, tr, nth, cw):
    hr = nth * tr
    n = len(srcs)

    def body(*refs):
        got_ref, send, recv = refs[1 + n:]
        t = pl.program_id(0)
        x, y, c = _mesh_pos()
        cps = []
        for j, chip in enumerate(_other_chips(x, y)):
            dst = got_ref.at[pl.ds(pl.multiple_of(j * hr + t * tr, 16), tr), :]
            cp = _remote(refs[1 + j % n], dst, send.at[j], recv.at[j], (*chip, c))
            cp.start()
            cps.append(cp)
        for cp in cps:
            cp.wait_send()

        @pl.when(t == nth - 1)
        def _():
            for j in range(3):
                r_ = got_ref.at[pl.ds(j * hr, hr), :]
                _remote(r_, r_, send.at[j], recv.at[j], (x, y, c)).wait_recv()

    return _pcall(
        body, name=name,
        grid_spec=pltpu.PrefetchScalarGridSpec(
            num_scalar_prefetch=1, grid=(nth,), in_specs=specs, out_specs=ANY,
            scratch_shapes=[pltpu.SemaphoreType.DMA((3,)), pltpu.SemaphoreType.DMA((3,))]),
        out_shape=jax.ShapeDtypeStruct((3 * hr, cw), srcs[0].dtype),
        compiler_params=_params("arbitrary"),
    )(_place(), *srcs)


def _ag_assemble(name, shard, stage, axis, tr, nth):
    R, cc = shard.shape
    hr = nth * tr
    full = (R, N_CHIPS * cc) if axis == 1 else (N_CHIPS * R, cc)

    def body(pl_ref, s0, s1, s2, h0, h1, out_ref, send, recv, lsem):
        t = pl.program_id(0)
        x, y, c = _mesh_pos()

        def region(s, half):
            if axis == 1:
                return out_ref.at[pl.ds(pl.multiple_of(half * hr + t * tr, 16), tr), pl.ds(pl.multiple_of(s * cc, LANES), cc)]
            return out_ref.at[pl.ds(pl.multiple_of(s * R + half * hr + t * tr, 16), tr), :]

        cps = []
        for j, src in enumerate((s0, s1, s2)):
            dst = region(pl_ref[1 + j], c)
            cps.append(_remote(src, dst, send.at[j], recv, (x, y, 1 - c)))
            cps.append(pltpu.make_async_copy(src, dst, lsem.at[j]))
        for half, src in enumerate((h0, h1)):
            cps.append(pltpu.make_async_copy(src, region(pl_ref[4], half), lsem.at[3 + half]))
        for cp in cps:
            cp.start()
        for k, cp in enumerate(cps):
            if k < 6 and k % 2 == 0:
                cp.wait_send()
            else:
                cp.wait()

        @pl.when(t == nth - 1)
        def _():
            r_ = out_ref.at[pl.ds(0, hr), pl.ds(0, 3 * cc)] if axis == 1 else out_ref.at[pl.ds(0, 3 * hr), :]
            _remote(r_, r_, send.at[0], recv, (x, y, c)).wait_recv()

    blk = lambda f: pl.BlockSpec((tr, cc), f)
    return _pcall(
        body, name=name,
        grid_spec=pltpu.PrefetchScalarGridSpec(
            num_scalar_prefetch=1, grid=(nth,),
            in_specs=[blk(lambda t, p, j=j: (j * nth + t, 0)) for j in range(3)]
            + [blk(lambda t, p, h=h: (h * nth + t, 0)) for h in range(2)],
            out_specs=ANY,
            scratch_shapes=[pltpu.SemaphoreType.DMA((3,)), pltpu.SemaphoreType.DMA, pltpu.SemaphoreType.DMA((5,))]),
        out_shape=jax.ShapeDtypeStruct(full, shard.dtype),
        compiler_params=_params("arbitrary"),
    )(_place(), stage, stage, stage, shard, shard)


def _comm_rows(hr, row_bytes):
    return _tile(hr, max(16, COMM_BLOCK_BYTES // row_bytes // 16 * 16), 16)


def _host_send(items):
    def copies(ins, outs, send, recv):
        x, y, c = _mesh_pos()
        cps = []
        for w, (_, kind, hr, cw) in enumerate(items):
            for j, (cx, cy) in enumerate(_other_chips(x, y)):
                s = 2 * cx + cy
                if kind == "half":
                    src = ins[w].at[pl.ds(pl.multiple_of(c * hr, 16), hr), :]
                elif kind == "cols":
                    src = ins[w].at[:, pl.ds(pl.multiple_of(s * cw, LANES), cw)]
                else:
                    src = ins[w].at[pl.ds(pl.multiple_of(s * hr, 16), hr), :]
                cps.append(_remote(src, outs[w].at[pl.ds(j * hr, hr), :], send.at[3 * w + j], recv.at[3 * w + j], (cx, cy, c)))
        return cps

    def start(ins, outs, send, recv):
        for cp in copies(ins, outs, send, recv):
            cp.start()

    def wait(ins, outs, send, recv):
        for cp in copies(ins, outs, send, recv):
            cp.wait()

    return _Host([a for a, _, _, _ in items], [jax.ShapeDtypeStruct((3 * hr, cw), a.dtype) for a, _, hr, cw in items],
                 3 * len(items), start, wait)


def _ag_send(name, sh):
    R, cc = sh.shape
    tr = _comm_rows(R // 2, cc * 2)
    nth = R // 2 // tr
    return _send_chips(name, [sh], [pl.BlockSpec((tr, cc), lambda t, p: (p[0] * nth + t, 0))], tr, nth, cc)


def _ag_finish(name, sh, stage, axis):
    R, cc = sh.shape
    tr = _comm_rows(R // 2, cc * 2)
    return _ag_assemble(name, sh, stage, axis, tr, R // 2 // tr)


def _pair_sum(name, g, tr, nblk, blk):
    cw = g.shape[1]
    c_arr = lax.axis_index("c").astype(jnp.int32).reshape(1)

    def body(c_ref, keep_ref, send_ref, out_ref, land, send, recv):
        i = pl.program_id(0)
        slot = lax.rem(i, 2)
        x, y, c = _mesh_pos()
        cp = _remote(send_ref, land.at[slot], send.at[slot], recv.at[slot], (x, y, 1 - c))
        cp.start()
        cp.wait_recv()
        out_ref[...] = (keep_ref[...].astype(F32) + land[slot].astype(F32)).astype(BF16)
        cp.wait_send()

    return _pcall(
        body, name=name,
        grid_spec=pltpu.PrefetchScalarGridSpec(
            num_scalar_prefetch=1, grid=(nblk,),
            in_specs=[pl.BlockSpec((tr, cw), lambda i, c_ref: (blk(i, c_ref[0]), 0)),
                      pl.BlockSpec((tr, cw), lambda i, c_ref: (blk(i, 1 - c_ref[0]), 0))],
            out_specs=pl.BlockSpec((tr, cw), lambda i, c_ref: (i, 0)),
            scratch_shapes=[pltpu.VMEM((2, tr, cw), BF16), pltpu.SemaphoreType.DMA((2,)), pltpu.SemaphoreType.DMA((2,))]),
        out_shape=jax.ShapeDtypeStruct((nblk * tr, cw), BF16),
        compiler_params=_params("arbitrary"),
    )(c_arr, g, g)


def _sum_chips_swap(name, q, got, qspec, tr, nth, cw):
    def body(p_ref, q_ref, g0, g1, g2, out_ref, buf, send, recv, lsem):
        t = pl.program_id(0)
        x, y, c = _mesh_pos()
        buf[...] = q_ref[...].astype(F32) + g0[...].astype(F32) + g1[...].astype(F32) + g2[...].astype(F32)
        dst = out_ref.at[pl.ds(pl.multiple_of((p_ref[0] * nth + t) * tr, 16), tr), :]
        cp = _remote(buf, dst, send, recv, (x, y, 1 - c))
        lc = pltpu.make_async_copy(buf, dst, lsem)
        cp.start()
        lc.start()
        lc.wait()
        cp.wait_send()

        @pl.when(t == nth - 1)
        def _():
            theirs = out_ref.at[pl.ds(0, nth * tr), :]
            _remote(theirs, theirs, send, recv, (x, y, c)).wait_recv()

    return _pcall(
        body, name=name,
        grid_spec=pltpu.PrefetchScalarGridSpec(
            num_scalar_prefetch=1, grid=(nth,),
            in_specs=[qspec] + [pl.BlockSpec((tr, cw), lambda t, p, j=j: (j * nth + t, 0)) for j in range(3)],
            out_specs=ANY,
            scratch_shapes=[pltpu.VMEM((tr, cw), F32), pltpu.SemaphoreType.DMA, pltpu.SemaphoreType.DMA,
                            pltpu.SemaphoreType.DMA]),
        out_shape=jax.ShapeDtypeStruct((2 * nth * tr, cw), F32),
        compiler_params=_params("arbitrary"),
    )(_place(), q, got, got, got)


def _rs_geom(g, axis):
    rows, gw = g.shape
    return (rows // 2, gw // N_CHIPS) if axis == 1 else (rows // N_CHIPS // 2, gw)


def _rs_pair_sum(tag, g, axis):
    hr, _ = _rs_geom(g, axis)
    tr = _comm_rows(hr, g.shape[1] * 2)
    nth = hr // tr
    if axis == 1:
        nblk, blk = nth, (lambda i, half: half * nth + i)
    else:
        nblk, blk = N_CHIPS * nth, (lambda i, half: (i // nth) * (2 * nth) + half * nth + i % nth)
    return _pair_sum("rs_pair_" + tag, g, tr, nblk, blk)


def _rs_part(q, axis, hr, cw, tr):
    nth = hr // tr
    if axis == 1:
        return lambda k: pl.BlockSpec((tr, cw), lambda t, p: (t, p[k]))
    return lambda k: pl.BlockSpec((tr, cw), lambda t, p: (p[k] * nth + t, 0))


def _rs_send(tag, q, axis, hr, cw):
    tr = _comm_rows(hr, cw * 2)
    part = _rs_part(q, axis, hr, cw, tr)
    return _send_chips("rs_send_" + tag, [q, q, q], [part(1), part(2), part(3)], tr, hr // tr, cw)


def _rs_finish(tag, q, got, axis, hr, cw):
    tr = _comm_rows(hr, cw * 4)
    return _sum_chips_swap("rs_sum_swap_" + tag, q, got, _rs_part(q, axis, hr, cw, tr)(4), tr, hr // tr, cw)


class _Exchange:
    AG_PLAN = {"ffn1_up": ("ffn1_w_down", "ffn2_w_gate"), "ffn1_down": ("ffn2_w_up", "w_in"),
               "w_in_qkv": ("ssm_w_glu", "w_out", "ple_w_gate", "ple_w_proj"), "attn_fwd_d1": ("ffn2_w_down",)}
    RS_PLAN = {"ffn1_dact": ("ffn2_w_gate", "ple_w_gate", "ple_w_proj"), "ffn1_dwd": ("ffn2_w_up",),
               "ffn1_dwgu": ("ffn2_w_down", "w_in", "ssm_w_glu", "w_out", "ffn1_w_down"),
               "ffn1_dn": ("ffn1_w_gate", "ffn1_w_up")}

    def __init__(self, shards, axes):
        self.shards, self.axes = shards, axes
        self.stage, self.full, self.q, self.geom, self.got = {}, {}, {}, {}, {}

    def ag_host(self, kernel):
        item = lambda k: (self.shards[k], "half", self.shards[k].shape[0] // 2, self.shards[k].shape[1])
        return _host_send([item(k) for k in self.AG_PLAN[kernel]])

    def ag_done(self, kernel, stages):
        self.stage.update(zip(self.AG_PLAN[kernel], stages))

    def weight(self, k):
        if k not in self.full:
            stage = self.stage[k] if k in self.stage else _ag_send("ag_send_" + k, self.shards[k])
            self.full[k] = _ag_finish("ag_asm_" + k, self.shards[k], stage, self.axes[k])
        return self.full[k]

    def grad(self, k, g):
        self.q[k], self.geom[k] = _rs_pair_sum(k, g, self.axes[k]), _rs_geom(g, self.axes[k])

    def rs_host(self, kernel):
        item = lambda k: (self.q[k], "cols" if self.axes[k] == 1 else "rows") + self.geom[k]
        return _host_send([item(k) for k in self.RS_PLAN[kernel]])

    def rs_done(self, kernel, gots):
        self.got.update(zip(self.RS_PLAN[kernel], gots))

    def finish(self):
        return {k: _rs_finish(k, q, self.got[k] if k in self.got else _rs_send(k, q, self.axes[k], *self.geom[k]),
                              self.axes[k], *self.geom[k]) for k, q in self.q.items()}


def _all_reduce_small(v):
    n = v.shape[0]
    h = n // 2

    def body(v_ref, out_ref, pair_in, chips_in, send, recv):
        x, y, c = _mesh_pos()
        me, sib, my_chip = (x, y, c), (x, y, 1 - c), 2 * x + y
        mine = pl.ds(pl.multiple_of(c * h, SUBLANES), h)
        other = pl.ds(pl.multiple_of((1 - c) * h, SUBLANES), h)
        pair = _remote(v_ref.at[other], pair_in, send.at[0], recv.at[0], sib)
        pair.start()
        pair.wait()
        chips_in[my_chip] = v_ref[mine, :] + pair_in[...]
        cps = []
        for j, (cx, cy) in enumerate(_other_chips(x, y)):
            cp = _remote(chips_in.at[my_chip], chips_in.at[my_chip], send.at[1 + j], recv.at[1 + j], (cx, cy, c))
            cp.start()
            cps.append(cp)
        for j, (cx, cy) in enumerate(_other_chips(x, y)):
            slot = chips_in.at[2 * cx + cy]
            _remote(slot, slot, send.at[1 + j], recv.at[1 + j], me).wait_recv()
        out_ref[mine, :] = (chips_in[0] + chips_in[1]) + (chips_in[2] + chips_in[3])
        for cp in cps:
            cp.wait_send()
        swap = _remote(out_ref.at[mine, :], out_ref.at[mine, :], send.at[4], recv.at[4], sib)
        swap.start()
        _remote(out_ref.at[other, :], out_ref.at[other, :], send.at[4], recv.at[4], me).wait_recv()
        swap.wait_send()

    return _pcall(
        body, name="ar_small",
        in_specs=[pl.BlockSpec(memory_space=pltpu.VMEM)], out_specs=pl.BlockSpec(memory_space=pltpu.VMEM),
        out_shape=jax.ShapeDtypeStruct((n, LANES), F32),
        scratch_shapes=[pltpu.VMEM((h, LANES), F32), pltpu.VMEM((N_CHIPS, h, LANES), F32),
                        pltpu.SemaphoreType.DMA((5,)), pltpu.SemaphoreType.DMA((5,))],
        compiler_params=pltpu.CompilerParams(vmem_limit_bytes=V7X_VMEM_LIMIT_BYTES),
    )(v)


def _adamw(name, w, g, m, v):
    R, Cc = w.shape
    tr = _tile(R, max(8, (3 << 18) // Cc // 8 * 8), 8)
    c1 = 1.0 - ADAM_B1 ** ADAM_STEP
    c2 = 1.0 - ADAM_B2 ** ADAM_STEP

    def body(w_ref, g_ref, m_ref, v_ref, d_ref, nm_ref, nv_ref):
        g_ = g_ref[...]
        nm = ADAM_B1 * m_ref[...] + (1.0 - ADAM_B1) * g_
        nv = ADAM_B2 * v_ref[...] + (1.0 - ADAM_B2) * (g_ * g_)
        d_ref[...] = -ADAM_LR * ((nm / c1) / (jnp.sqrt(nv / c2) + ADAM_EPS) + ADAM_WD * w_ref[...])
        nm_ref[...] = nm
        nv_ref[...] = nv

    spec = pl.BlockSpec((tr, Cc), lambda i: (i, 0))
    return _pcall(
        body, name=name, grid=(R // tr,),
        in_specs=[spec] * 4, out_specs=[spec] * 3,
        out_shape=[jax.ShapeDtypeStruct((R, Cc), F32)] * 3,
        compiler_params=_params("parallel"),
    )(w, g, m, v)


def _pack(arrs, rows):
    flat = jnp.concatenate([a.reshape(-1) for a in arrs])
    return jnp.pad(flat, (0, rows * LANES - flat.shape[0])).reshape(rows, LANES)


def _unpack(packed, like):
    flat, out, o = packed.reshape(-1), [], 0
    for a in like:
        out.append(flat[o:o + a.size].reshape(a.shape))
        o += a.size
    return out


BIG = (
    ("ffn1_w_gate", 1), ("ffn1_w_up", 1), ("ffn1_w_down", 0), ("w_in", 1), ("ssm_w_glu", 0), ("w_out", 0),
    ("ffn2_w_gate", 1), ("ffn2_w_up", 1), ("ffn2_w_down", 0), ("ple_w_gate", 0), ("ple_w_proj", 1),
)
SMALL = ("ffn1_norm", "mix_norm", "attn_out_norm", "ssm_lambda_re", "ssm_lambda_im", "ssm_log_dt", "ssm_b_re", "ssm_b_im",
         "ssm_c_re", "ssm_c_im", "ssm_d", "ssm_b_glu", "ssm_out_norm", "ffn2_norm", "ple_norm", "final_norm")
WEIGHTS = ("ffn1_norm", "ffn1_w_gate", "ffn1_w_up", "ffn1_w_down", "mix_norm", "w_in", "attn_out_norm", "ssm_lambda_re",
           "ssm_lambda_im", "ssm_log_dt", "ssm_b_re", "ssm_b_im", "ssm_c_re", "ssm_c_im", "ssm_d", "ssm_w_glu", "ssm_b_glu",
           "ssm_out_norm", "w_out", "ffn2_norm", "ffn2_w_gate", "ffn2_w_up", "ffn2_w_down", "ple_norm", "ple_w_gate",
           "ple_w_proj", "final_norm")


def _pad_to(a, axis, n):
    pad = [(0, 0), (0, 0)]
    pad[axis] = (0, n - a.shape[axis])
    return jnp.pad(a, pad)


def _local_step(x, p, tgt, w, ex):
    S, D = x.shape
    A = w["attn_out_norm"].shape[-1]
    W = w["ssm_d"].shape[-1]
    G, P = w["ssm_lambda_re"].shape[-2:]
    C = w["ssm_b_re"].shape[-1]
    GB = G // SSM_BLOCK_GROUPS
    T = min(1024, S)
    row = lambda name: w[name].reshape(1, -1)
    gs = {}

    h1, ffn1_saved = _ffn_fwd("ffn1", x, row("ffn1_norm"), ex)
    n2 = _rms_fwd("mix_norm", h1, row("mix_norm"))
    w_in = ex.weight("w_in")
    n2p = _to_attn_order(n2)
    host, done = _carried(ex, "ag", "w_in_qkv")
    qkv, *outs = _mm("w_in_qkv", [n2p], [w_in[:, :3 * A]], [F32], tm=1024, tn=1024, host=host)
    done(outs)
    (s_in,) = _mm("w_in_ssm", [n2], [w_in[:, 3 * A:]], [F32], tm=1024, tn=1024)
    ya, lse = _attn_fwd(qkv, ex)

    col = lambda name: w[name].reshape(G * P, 1)
    logdt_x = jnp.repeat(w["ssm_log_dt"].reshape(G), P).reshape(G * P, 1)
    b_re, b_im = w["ssm_b_re"].reshape(G * P, C), w["ssm_b_im"].reshape(G * P, C)
    lrdt, lidt, bbr, bbi = _ssm_disc(col("ssm_lambda_re"), col("ssm_lambda_im"), logdt_x, b_re, b_im)
    gsz = SSM_BLOCK_GROUPS
    to_bb = lambda t: _block_diag(t.reshape(GB, gsz, P, C).transpose(0, 1, 3, 2))
    bb = jnp.concatenate([to_bb(bbr), to_bb(bbi)], axis=2).astype(BF16)
    to_cc = lambda t: _block_diag(t.reshape(GB, gsz, C, P).transpose(0, 1, 3, 2))
    cc = jnp.concatenate([to_cc(w["ssm_c_re"]), -to_cc(w["ssm_c_im"])], axis=1).astype(BF16)
    lam_dt = jnp.stack([lrdt.reshape(GB, gsz * P), lidt.reshape(GB, gsz * P)], axis=1)
    ufp = _ssm_perm(s_in, T)
    ypre, hstart = _ssm_fwd(ufp, bb, cc, lam_dt, row("ssm_d"), T)

    def glu_in(ins, ps):
        yg = _gelu(ins[0])
        return [yg, yg], []

    yg, ygb = _rowwise("ssm_gelu", glu_in, [ypre], [], [(W, F32), (W, BF16)])
    w_glu = ex.weight("ssm_w_glu")

    def glu_out(accs, ex):
        gl = accs[0] + ex[1]
        return [ex[0] * _sigmoid(gl), gl]

    ybp, gl = _mm("ssm_glu", [ygb], [w_glu], [F32, F32], extras=[(yg, "mn"), (row("ssm_b_glu"), "n")],
                  epilogue=glu_out, tm=1024, tn=1024)
    yb = _ssm_unperm(ybp, T)
    na = _from_attn_order(_rms_fwd("attn_out_norm", ya, row("attn_out_norm")))
    nb = _rms_fwd("ssm_out_norm", yb, row("ssm_out_norm"))
    w_out = ex.weight("w_out")
    (h2,) = _mm("w_out", [na, nb], [w_out[:A], w_out[A:]], [F32], pairs=((0, 0, 0), (1, 1, 0)), extras=[(h1, "mn")],
                epilogue=lambda accs, ex: [ex[0] + accs[0]], tm=1024, tn=1024)
    h3, ffn2_saved = _ffn_fwd("ffn2", h2, row("ffn2_norm"), ex)
    n4 = _rms_fwd("ple_norm", h3, row("ple_norm"))
    (pe,) = _mm("ple_proj", [p], [ex.weight("ple_w_proj")], [F32], tm=1024, tn=1024)

    def ple_out(accs, ex):
        gate = _sigmoid(accs[0])
        return [ex[1] + gate * ex[0], gate]

    h4, gate = _mm("ple_gate", [n4], [ex.weight("ple_w_gate")], [F32, F32], extras=[(pe, "mn"), (h3, "mn")],
                   epilogue=ple_out, tm=1024, tn=1024)

    dh4, err2, gs["final_norm"] = _loss_head(h4, tgt, row("final_norm"))
    loss = (0.5 / D) * jnp.sum(err2)

    def ple_bwd(ins, ps):
        dh, gt, pe_ = ins
        return [dh * gt, dh * pe_ * gt * (1.0 - gt)], []

    dpe, dpg = _rowwise("ple_bwd", ple_bwd, [dh4, gate, pe], [], [(D, BF16), (D, BF16)])
    (d_ple_proj,) = _mm("ple_dproj", [p], [dpe], [BF16], ta=True, tm=256, tn=2048, tk=1024)
    (d_ple_gate,) = _mm("ple_dgate", [n4], [dpg], [BF16], ta=True, tm=1024, tn=1024, tk=2048)
    (dn4,) = _mm("ple_dn", [dpg], [ex.weight("ple_w_gate")], [F32], tb=True, tm=1024, tn=1024)
    (dh3, dh3b), gs["ple_norm"] = _rms_bwd("ple_dnorm", dn4, h3, row("ple_norm"), dres=dh4, copy_scale=0.5)
    (dh2, dh2b), gs["ffn2_norm"] = _ffn_bwd("ffn2", dh3, dh3b, h2, row("ffn2_norm"), ex, ffn2_saved, copy_scale=1.0)
    (dna,) = _mm("w_out_dna", [_to_attn_order(dh2b)], [w_out[:A]], [F32], tb=True, tm=1024, tn=1024)
    (dnb,) = _mm("w_out_dnb", [dh2b], [w_out[A:]], [F32], tb=True, tm=1024, tn=1024)
    (d_wout_a,) = _mm("w_out_dwa", [na], [dh2b], [BF16], ta=True, tm=1024, tn=1024, tk=2048)
    (d_wout_b,) = _mm("w_out_dwb", [nb], [dh2b], [BF16], ta=True, tm=1024, tn=1024, tk=2048)
    d_w_out = jnp.concatenate([d_wout_a, d_wout_b], axis=0)
    (dya,), gs["attn_out_norm"] = _rms_bwd("attn_out_dnorm", dna, ya, row("attn_out_norm"))
    (dyb,), gs["ssm_out_norm"] = _rms_bwd("ssm_out_dnorm", dnb, yb, row("ssm_out_norm"))

    dybp = _ssm_perm(dyb, T)

    def glu_bwd(ins, ps):
        dy, yg_, gl_ = ins
        sg = _sigmoid(gl_)
        dgl = dy * yg_ * sg * (1.0 - sg)
        return [dgl, dy * sg], [jnp.sum(dgl, axis=0, keepdims=True)]

    dgl, dyg_direct, gs["ssm_b_glu"] = _rowwise("ssm_glu_bwd", glu_bwd, [dybp, yg, gl], [], [(W, BF16), (W, F32)], accs=[W])
    (d_w_glu,) = _mm("ssm_dwglu", [ygb], [dgl], [BF16], ta=True, tm=1024, tn=1024, tk=2048)
    (dypre,) = _mm("ssm_dyg", [dgl], [w_glu], [F32], tb=True, extras=[(dyg_direct, "mn"), (ypre, "mn")],
                   epilogue=lambda accs, ex: [(accs[0] + ex[0]) * _gelu_grad(ex[1])], tm=1024, tn=1024)
    dufp, dbb, dcc, da, gs["ssm_d"] = _ssm_bwd(ufp, dypre, bb, bb.transpose(0, 2, 1), cc, cc.transpose(0, 2, 1),
                                               lam_dt, row("ssm_d"), hstart, T)
    ns = gsz * P
    from_bb = lambda t: _block_diag_take(t, gsz).transpose(0, 1, 3, 2).reshape(G * P, C)
    from_cc = lambda t: _block_diag_take(t, gsz).transpose(0, 1, 3, 2).reshape(w["ssm_c_re"].shape)
    gs["ssm_c_re"], gs["ssm_c_im"] = from_cc(dcc[:, :ns]), -from_cc(dcc[:, ns:])
    da = da.sum(axis=1)
    dar, dai = da[:, :ns].reshape(G * P, 1), da[:, ns:].reshape(G * P, 1)
    dlr, dli, dlogdt, dbr, dbi = _ssm_disc_bwd(col("ssm_lambda_re"), col("ssm_lambda_im"), logdt_x, b_re, b_im,
                                               dar, dai, from_bb(dbb[:, :, :ns]), from_bb(dbb[:, :, ns:]))
    gs["ssm_lambda_re"], gs["ssm_lambda_im"] = dlr.reshape(w["ssm_lambda_re"].shape), dli.reshape(w["ssm_lambda_im"].shape)
    gs["ssm_log_dt"] = dlogdt.reshape(G, P).sum(axis=1).reshape(w["ssm_log_dt"].shape)
    gs["ssm_b_re"], gs["ssm_b_im"] = dbr.reshape(w["ssm_b_re"].shape), dbi.reshape(w["ssm_b_im"].shape)
    ds_in = _ssm_unperm(dufp, T)

    dq, dk, dv = _attn_bwd(qkv, dya, ya, lse)
    dqkv = jnp.concatenate([dq, dk, dv], axis=1).astype(BF16)
    (d_w_qkv,) = _mm("w_in_dw_qkv", [n2p], [dqkv], [BF16], ta=True, tm=1024, tn=1024, tk=2048)
    (d_w_s,) = _mm("w_in_dw_ssm", [n2], [ds_in], [BF16], ta=True, tm=1024, tn=1024, tk=2048)
    d_w_in = jnp.concatenate([d_w_qkv, d_w_s], axis=1)
    dz = jnp.concatenate([_from_attn_order(dqkv), ds_in.astype(BF16)], axis=1)
    (dn2,) = _mm("w_in_dn", [dz], [w_in], [F32], tb=True, tm=1024, tn=1024)
    (dh1, dh1b), gs["mix_norm"] = _rms_bwd("mix_dnorm", dn2, h1, row("mix_norm"), dres=dh2, copy_scale=0.5)
    for k, g in (("ple_w_gate", d_ple_gate), ("ple_w_proj", d_ple_proj), ("w_out", d_w_out), ("ssm_w_glu", d_w_glu),
                 ("w_in", d_w_in)):
        ex.grad(k, g)
    (dx,), gs["ffn1_norm"] = _ffn_bwd("ffn1", dh1, dh1b, x, row("ffn1_norm"), ex, ffn1_saved, copy_scale=None)
    small = {k: gs[k].reshape(w[k].shape) for k in SMALL}
    return loss, dx, ex.finish(), small


def kernel(x, p, ffn1_norm, ffn1_w_gate, ffn1_w_up, ffn1_w_down, mix_norm, w_in, attn_out_norm, ssm_lambda_re, ssm_lambda_im, ssm_log_dt, ssm_b_re, ssm_b_im, ssm_c_re, ssm_c_im, ssm_d, ssm_w_glu, ssm_b_glu, ssm_out_norm, w_out, ffn2_norm, ffn2_w_gate, ffn2_w_up, ffn2_w_down, ple_norm, ple_w_gate, ple_w_proj, final_norm, loss_target, m_ffn1_norm, m_ffn1_w_gate, m_ffn1_w_up, m_ffn1_w_down, m_mix_norm, m_w_in, m_attn_out_norm, m_ssm_lambda_re, m_ssm_lambda_im, m_ssm_log_dt, m_ssm_b_re, m_ssm_b_im, m_ssm_c_re, m_ssm_c_im, m_ssm_d, m_ssm_w_glu, m_ssm_b_glu, m_ssm_out_norm, m_w_out, m_ffn2_norm, m_ffn2_w_gate, m_ffn2_w_up, m_ffn2_w_down, m_ple_norm, m_ple_w_gate, m_ple_w_proj, m_final_norm, v_ffn1_norm, v_ffn1_w_gate, v_ffn1_w_up, v_ffn1_w_down, v_mix_norm, v_w_in, v_attn_out_norm, v_ssm_lambda_re, v_ssm_lambda_im, v_ssm_log_dt, v_ssm_b_re, v_ssm_b_im, v_ssm_c_re, v_ssm_c_im, v_ssm_d, v_ssm_w_glu, v_ssm_b_glu, v_ssm_out_norm, v_w_out, v_ffn2_norm, v_ffn2_w_gate, v_ffn2_w_up, v_ffn2_w_down, v_ple_norm, v_ple_w_gate, v_ple_w_proj, v_final_norm):
    args = locals()
    w = {k: args[k] for k in WEIGHTS}
    m = {k: args["m_" + k] for k in WEIGHTS}
    v = {k: args["v_" + k] for k in WEIGHTS}
    w2 = {k: w[k].reshape(w[k].shape[-2:]) for k, _ in BIG}

    axes = [ax for _, ax in BIG]
    padded = {k: -(-w2[k].shape[ax] // LANES) * LANES for k, ax in BIG}
    shards = [_pad_to(w2[k].astype(BF16), ax, padded[k]) for k, ax in BIG]
    ex = _Exchange(dict(zip([k for k, _ in BIG], shards)), dict(BIG))
    loss_local, dx, summed, gsmall = _local_step(x[0], p[0, 0], loss_target[0], w, ex)
    loss = lax.psum(loss_local, MESH_AXES)
    n_small = sum(w[k].size for k in SMALL)
    rows = -(-n_small // (2 * SUBLANES * LANES)) * 2 * SUBLANES
    gs_sum = _all_reduce_small(_pack([gsmall[k] for k in SMALL], rows))

    grads, delta, new_m, new_v = {}, {}, {}, {}
    for k, ax in BIG:
        gfull = summed[k]
        g2 = lax.slice_in_dim(gfull, 0, w2[k].shape[ax], axis=ax)
        d2, nm2, nv2 = _adamw("adamw_" + k, w2[k], g2, m[k].reshape(w2[k].shape), v[k].reshape(w2[k].shape))
        grads[k], delta[k], new_m[k], new_v[k] = (t.reshape(w[k].shape) for t in (g2, d2, nm2, nv2))
    small_like = [w[k] for k in SMALL]
    ds, nms, nvs = _adamw("adamw_small", _pack(small_like, rows), gs_sum, _pack([m[k] for k in SMALL], rows),
                          _pack([v[k] for k in SMALL], rows))
    for k, g_, d_, nm_, nv_ in zip(SMALL, _unpack(gs_sum, small_like), _unpack(ds, small_like),
                                   _unpack(nms, small_like), _unpack(nvs, small_like)):
        grads[k], delta[k], new_m[k], new_v[k] = g_, d_, nm_, nv_

    return (loss, dx[None], *[grads[k] for k in WEIGHTS], *[delta[k] for k in WEIGHTS],
            *[new_m[k] for k in WEIGHTS], *[new_v[k] for k in WEIGHTS])
```

```python
import functools
import math

import jax
import jax.numpy as jnp
from jax import lax
from jax.experimental import pallas as pl
from jax.experimental.pallas import tpu as pltpu

F32 = jnp.float32
BF16 = jnp.bfloat16
MESH = pl.DeviceIdType.MESH
MESH_AXES = ("x", "y", "c")
N_CHIPS = 4
N_DEV = 8

V7X_VMEM_LIMIT_BYTES = 56 << 20
LANES = 128
SUBLANES = 8

HEAD_DIM = 64
SWA_BLOCK = 128
DILATIONS = (1, 4, 16)
SSM_BLOCK_GROUPS = 8
NORM_EPS = 1e-6
MASK_VALUE = -1e30

ADAM_LR = 0.001
ADAM_B1 = 0.9
ADAM_B2 = 0.999
ADAM_EPS = 1e-08
ADAM_WD = 0.01
ADAM_STEP = 10

GELU_C = math.sqrt(2.0 / math.pi)
GELU_K = 0.044715


def _pcall(body, **kw):
    return pl.pallas_call(body, **kw)


def _params(*sem):
    return pltpu.CompilerParams(dimension_semantics=sem, vmem_limit_bytes=V7X_VMEM_LIMIT_BYTES)


def _tile(n, target, align):
    best = None
    for t in range(align, min(n, target) + 1, align):
        if n % t == 0:
            best = t
    return n if best is None else best


def _sigmoid(x):
    return 0.5 * jnp.tanh(0.5 * x) + 0.5


class _Host:
    def __init__(self, ins, out_shapes, n_sem, start, wait):
        self.ins, self.out_shapes, self.n_sem, self.start, self.wait = ins, out_shapes, n_sem, start, wait


def _mm(name, lhs, rhs, outs, pairs=((0, 0, 0),), epilogue=None, extras=(), ta=False, tb=False,
        tm=1024, tn=512, tk=2048, host=None):
    nl, nr, ne, no = len(lhs), len(rhs), len(extras), len(outs)
    nhi, nho = (len(host.ins), len(host.out_shapes)) if host else (0, 0)
    n_acc = 1 + max(p[2] for p in pairs)
    (K, M) = lhs[0].shape if ta else lhs[0].shape[::-1]
    (N, K2) = rhs[0].shape if tb else rhs[0].shape[::-1]
    assert K == K2, (name, lhs[0].shape, rhs[0].shape)
    tm, tn, tk = _tile(M, tm, LANES), _tile(N, tn, LANES), _tile(K, tk, LANES)
    ni, nj, nk = M // tm, N // tn, K // tk
    n_scr = n_acc if nk > 1 else 0
    if epilogue is None:
        epilogue = lambda accs, ex: accs
    dn = (((0 if ta else 1,), (1 if tb else 0,)), ((), ()))

    def body(*refs):
        refs = list(refs)
        take = lambda n: [refs.pop(0) for _ in range(n)]
        l, r, e, hin, o, hout, acc = take(nl), take(nr), take(ne), take(nhi), take(no), take(nho), take(n_scr)
        i, j, k = pl.program_id(0), pl.program_id(1), pl.program_id(2)
        if host:
            @pl.when((i == 0) & (j == 0) & (k == 0))
            def _():
                host.start(hin, hout, *refs)

        parts = [None] * n_acc
        for li, ri, ai in pairs:
            d = lax.dot_general(l[li][...].astype(BF16), r[ri][...].astype(BF16), dn,
                                preferred_element_type=F32)
            parts[ai] = d if parts[ai] is None else parts[ai] + d

        def finish(accs):
            res = epilogue(accs, [x[...] for x in e])
            for ref, val in zip(o, res):
                ref[...] = val.astype(ref.dtype)

        if nk == 1:
            finish(parts)
        else:
            @pl.when(k == 0)
            def _():
                for ai in range(n_acc):
                    acc[ai][...] = parts[ai]

            @pl.when(k > 0)
            def _():
                for ai in range(n_acc):
                    acc[ai][...] += parts[ai]

            @pl.when(k == nk - 1)
            def _():
                finish([a[...] for a in acc])

        if host:
            @pl.when((i == ni - 1) & (j == nj - 1) & (k == nk - 1))
            def _():
                host.wait(hin, hout, *refs)

    lspec = pl.BlockSpec((tk, tm), lambda i, j, k: (k, i)) if ta else pl.BlockSpec((tm, tk), lambda i, j, k: (i, k))
    rspec = pl.BlockSpec((tn, tk), lambda i, j, k: (j, k)) if tb else pl.BlockSpec((tk, tn), lambda i, j, k: (k, j))
    especs = []
    for arr, kind in extras:
        if kind == "mn":
            especs.append(pl.BlockSpec((tm, tn), lambda i, j, k: (i, j)))
        elif kind == "n":
            especs.append(pl.BlockSpec((1, tn), lambda i, j, k: (0, j)))
        else:
            especs.append(pl.BlockSpec((tm, 1), lambda i, j, k: (i, 0)))
    any_spec = pl.BlockSpec(memory_space=pl.ANY)
    sems = [pltpu.SemaphoreType.DMA((host.n_sem,)), pltpu.SemaphoreType.DMA((host.n_sem,))] if host else []
    res = _pcall(
        body, name=name,
        grid=(ni, nj, nk),
        in_specs=[lspec] * nl + [rspec] * nr + especs + [any_spec] * nhi,
        out_specs=[pl.BlockSpec((tm, tn), lambda i, j, k: (i, j))] * no + [any_spec] * nho,
        out_shape=[jax.ShapeDtypeStruct((M, N), dt) for dt in outs] + (list(host.out_shapes) if host else []),
        scratch_shapes=[pltpu.VMEM((tm, tn), F32)] * n_scr + sems,
        compiler_params=_params(*(("arbitrary",) * 3 if host else ("parallel", "parallel", "arbitrary"))),
    )(*lhs, *rhs, *[a for a, _ in extras], *(host.ins if host else []))
    return res


ROWWISE_BLOCK_BYTES = 20 << 20


def _rowwise(name, fn, ins, params, outs, accs=(), ts=None):
    S = ins[0].shape[0]
    if ts is None:
        row_bytes = sum(a.shape[1] * a.dtype.itemsize for a in ins) + sum(w * jnp.dtype(dt).itemsize for w, dt in outs)
        ts = next((t for t in (1024, 512) if t * row_bytes <= ROWWISE_BLOCK_BYTES), 256)
    ts = _tile(S, ts, 16)
    ni, npar, no, na = len(ins), len(params), len(outs), len(accs)

    def body(*refs):
        i_refs, p_refs = refs[:ni], refs[ni:ni + npar]
        o_refs = refs[ni + npar:ni + npar + no]
        a_refs = refs[ni + npar + no:]
        res_o, res_a = fn([r[...] for r in i_refs], [r[...] for r in p_refs])
        for ref, val in zip(o_refs, res_o):
            ref[...] = val.astype(ref.dtype)
        if na:
            @pl.when(pl.program_id(0) == 0)
            def _():
                for ref in a_refs:
                    ref[...] = jnp.zeros(ref.shape, F32)

            for ref, val in zip(a_refs, res_a):
                ref[...] += val

    res = _pcall(
        body, name=name,
        grid=(S // ts,),
        in_specs=[pl.BlockSpec((ts, a.shape[1]), lambda i: (i, 0)) for a in ins]
        + [pl.BlockSpec(p.shape, lambda i: (0, 0)) for p in params],
        out_specs=[pl.BlockSpec((ts, w), lambda i: (i, 0)) for w, _ in outs]
        + [pl.BlockSpec((1, w), lambda i: (0, 0)) for w in accs],
        out_shape=[jax.ShapeDtypeStruct((S, w), dt) for w, dt in outs]
        + [jax.ShapeDtypeStruct((1, w), F32) for w in accs],
        compiler_params=_params("arbitrary"),
    )(*ins, *params)
    return res


def _xhat(x):
    r = lax.rsqrt(jnp.mean(x * x, axis=-1, keepdims=True) + NORM_EPS)
    return x * r, r


def _rms_fwd(name, x, g):
    def fn(ins, ps):
        xh, _ = _xhat(ins[0])
        return [xh * ps[0]], []

    return _rowwise(name, fn, [x], [g], [(x.shape[1], BF16)])[0]


def _rms_bwd(name, dn, x, g, dres=None, copy_scale=None):
    w = x.shape[1]

    def fn(ins, ps):
        dn_, x_ = ins[0], ins[1]
        xh, r = _xhat(x_)
        dxh = dn_ * ps[0]
        dx = r * (dxh - xh * jnp.mean(dxh * xh, axis=-1, keepdims=True))
        if dres is not None:
            dx = dx + ins[2]
        o = [dx] + ([dx * copy_scale] if copy_scale is not None else [])
        return o, [jnp.sum(dn_ * xh, axis=0, keepdims=True)]

    ins = [dn, x] + ([dres] if dres is not None else [])
    outs = [(w, F32)] + ([(w, BF16)] if copy_scale is not None else [])
    res = _rowwise(name, fn, ins, [g], outs, accs=[w])
    return res[:-1], res[-1]


def _swiglu_epilogue(accs, ex):
    g, u = accs
    sg = _sigmoid(g)
    s = g * sg
    return [u * (sg + s * (1.0 - sg)), s, s * u]


def _dswiglu_epilogue(accs, ex):
    da = accs[0]
    return [da * ex[0].astype(F32), da * ex[1].astype(F32)]


def _carried(ex, kind, kernel):
    if kernel not in (ex.AG_PLAN if kind == "ag" else ex.RS_PLAN):
        return None, lambda outs: None
    if kind == "ag":
        return ex.ag_host(kernel), lambda outs: ex.ag_done(kernel, outs)
    return ex.rs_host(kernel), lambda outs: ex.rs_done(kernel, outs)


def _ffn_fwd(tag, h, gnorm, ex):
    n = _rms_fwd(tag + "_norm", h, gnorm)
    host, done = _carried(ex, "ag", tag + "_up")
    g, u, a, *outs = _mm(tag + "_up", [n], [ex.weight(tag + "_w_gate"), ex.weight(tag + "_w_up")], [BF16, BF16, BF16],
                         pairs=((0, 0, 0), (0, 1, 1)), epilogue=_swiglu_epilogue, tm=1024, tn=512, host=host)
    done(outs)
    host, done = _carried(ex, "ag", tag + "_down")
    hout, *outs = _mm(tag + "_down", [a], [ex.weight(tag + "_w_down")], [F32], extras=[(h, "mn")],
                      epilogue=lambda accs, ex_: [ex_[0] + 0.5 * accs[0]], tm=512, tn=1024, tk=8192, host=host)
    done(outs)
    return hout, (n, g, u, a)


def _ffn_bwd(tag, dh, dhb_half, h, gnorm, ex, saved, copy_scale):
    n, g, u, a = saved
    wg, wu, wd = (ex.weight(tag + k) for k in ("_w_gate", "_w_up", "_w_down"))
    host, done = _carried(ex, "rs", tag + "_dact")
    dg, du, *outs = _mm(tag + "_dact", [dhb_half], [wd], [BF16, BF16], tb=True, extras=[(g, "mn"), (u, "mn")],
                        epilogue=_dswiglu_epilogue, tm=1024, tn=512, host=host)
    done(outs)
    host, done = _carried(ex, "rs", tag + "_dwd")
    dwd, *outs = _mm(tag + "_dwd", [a], [dhb_half], [BF16], ta=True, tm=512, tn=2048, tk=2048, host=host)
    done(outs)
    ex.grad(tag + "_w_down", dwd)
    host, done = _carried(ex, "rs", tag + "_dwgu")
    dwg, dwu, *outs = _mm(tag + "_dwgu", [n], [dg, du], [BF16, BF16], pairs=((0, 0, 0), (0, 1, 1)), ta=True,
                          tm=1024, tn=512, tk=2048, host=host)
    done(outs)
    ex.grad(tag + "_w_gate", dwg)
    ex.grad(tag + "_w_up", dwu)
    host, done = _carried(ex, "rs", tag + "_dn")
    dn, *outs = _mm(tag + "_dn", [dg, du], [wg, wu], [F32], pairs=((0, 0, 0), (1, 1, 0)), tb=True,
                    tm=1024, tn=1024, tk=1408, host=host)
    done(outs)
    return _rms_bwd(tag + "_dnorm", dn, h, gnorm, dres=dh, copy_scale=copy_scale)


ATTN_HEAD_PAIRS = 8


def _to_attn_order(a):
    S, w = a.shape
    return a.reshape(S // 16, 16, w).transpose(1, 0, 2).reshape(S, w)


def _from_attn_order(a):
    S, w = a.shape
    return a.reshape(16, S // 16, w).transpose(1, 0, 2).reshape(S, w)


def _attn_geom(S, d):
    s16 = S // 16
    if d == 16:
        return (16, s16), (1, SWA_BLOCK), (lambda r, b: (r, b)), 16, s16 // SWA_BLOCK
    if d == 4:
        return (4, 4, s16), (4, 1, SWA_BLOCK // 4), (lambda r, b: (0, r, b)), 4, s16 // (SWA_BLOCK // 4)
    return (16, s16), (16, SWA_BLOCK // 16), (lambda r, b: (0, b)), 1, s16 // (SWA_BLOCK // 16)


def _attn_pos(rho, d):
    if d == 16:
        return rho
    if d == 4:
        return 4 * (rho & 31) + (rho >> 5)
    return 16 * (rho & 7) + (rho >> 3)


def _attn_spec(S, d, lb, col, shift=0):
    _, blk, idx, _, nb = _attn_geom(S, d)
    return pl.BlockSpec(blk + (lb,), lambda r, cb, b: idx(r, jnp.clip(b + shift, 0, nb - 1)) + (col(cb),))


def _attn_view(a, d):
    return a.reshape(_attn_geom(a.shape[0], d)[0] + (a.shape[1],))


def _attn_valid(d):
    qp = _attn_pos(lax.broadcasted_iota(jnp.int32, (SWA_BLOCK, 2 * SWA_BLOCK), 0), d)
    kk = lax.broadcasted_iota(jnp.int32, (SWA_BLOCK, 2 * SWA_BLOCK), 1)
    kp = _attn_pos(kk & (SWA_BLOCK - 1), d)
    is_prev = kk < SWA_BLOCK
    return qp, kp, is_prev


def _head_masks(rows=SWA_BLOCK):
    lane = lax.broadcasted_iota(jnp.int32, (rows, LANES), 1)
    return [lane < HEAD_DIM, lane >= HEAD_DIM]


def _attn_ld(ref, sl):
    t = ref[(slice(None),) * (len(ref.shape) - 1) + (sl,)]
    return t.reshape(-1, t.shape[-1])


def _attn_st(ref, sl, val):
    ref[(slice(None),) * (len(ref.shape) - 1) + (sl,)] = val.reshape(ref.shape[:-1] + (val.shape[-1],))


def _per_head(t, first):
    sw = pltpu.roll(t, HEAD_DIM, 1)
    lo = lax.broadcasted_iota(jnp.int32, t.shape, 1) < HEAD_DIM
    return jnp.where(lo, t, sw) if first else jnp.where(lo, sw, t)


def _dot_nt(a, b):
    return lax.dot_general(a, b, (((1,), (1,)), ((), ())), preferred_element_type=F32)


def _dot_tn(a, b):
    return lax.dot_general(a, b, (((0,), (0,)), ((), ())), preferred_element_type=F32)


def _dot(a, b):
    return jnp.dot(a, b, preferred_element_type=F32)


def _keep(mask, t):
    return jnp.where(mask, t.astype(F32), 0.0).astype(BF16)


def _attn_cols(A):
    lb = min(A, LANES * ATTN_HEAD_PAIRS)
    ncol = A // lb
    return lb, ncol, [lambda cb, part=part: part * ncol + cb for part in range(3)], (lambda cb: cb)


def _attn_fwd_stage(name, qkv, d, prev, final, host=None):
    S, A3 = qkv.shape
    A = A3 // 3
    lb, ncol, (cq, ck, cv), ca = _attn_cols(A)
    view, _, _, nres, nb = _attn_geom(S, d)
    scale = HEAD_DIM ** -0.5
    has_prev = prev is not None
    n_out = 2 if final else 3
    nhi, nho = (len(host.ins), len(host.out_shapes)) if host else (0, 0)

    def body(*refs):
        q_ref, kp_ref, kc_ref, vp_ref, vc_ref = refs[:5]
        p_refs = refs[5:8] if has_prev else ()
        n_in = 5 + len(p_refs)
        hin, o_refs = refs[n_in:n_in + nhi], refs[n_in + nhi:n_in + nhi + n_out]
        hout, sems = refs[n_in + nhi + n_out:n_in + nhi + n_out + nho], refs[n_in + nhi + n_out + nho:]
        b = pl.program_id(2)
        if host:
            @pl.when((pl.program_id(0) == 0) & (pl.program_id(1) == 0) & (b == 0))
            def _():
                host.start(hin, hout, *sems)
        qp, kp_, is_prev = _attn_valid(d)
        valid = (is_prev & (kp_ >= qp) & (b > 0)) | (jnp.logical_not(is_prev) & (kp_ <= qp))
        hm, hm2 = _head_masks(), _head_masks(2 * SWA_BLOCK)
        for hp in range(lb // LANES):
            sl = slice(hp * LANES, (hp + 1) * LANES)
            q = _attn_ld(q_ref, sl)
            k2 = jnp.concatenate([_attn_ld(kp_ref, sl), _attn_ld(kc_ref, sl)], axis=0).astype(BF16)
            v2 = jnp.concatenate([_attn_ld(vp_ref, sl), _attn_ld(vc_ref, sl)], axis=0)
            o = jnp.zeros((SWA_BLOCK, LANES), F32)
            m = jnp.zeros((SWA_BLOCK, LANES), F32)
            l = jnp.zeros((SWA_BLOCK, LANES), F32)
            for hh in range(2):
                s = jnp.where(valid, _dot_nt(_keep(hm[hh], q), k2) * scale, MASK_VALUE)
                mh = jnp.max(s, axis=-1, keepdims=True)
                p = jnp.exp(s - mh)
                lh = jnp.sum(p, axis=-1, keepdims=True)
                o = o + _dot(p.astype(BF16), _keep(hm2[hh], v2))
                m = jnp.where(hm[hh], mh, m)
                l = jnp.where(hm[hh], lh, l)
            if has_prev:
                po, pm, pl_ = (_attn_ld(r, sl) for r in p_refs)
                mn = jnp.maximum(m, pm)
                w_new, w_old = jnp.exp(m - mn), jnp.exp(pm - mn)
                o = o * w_new + po * w_old
                l = l * w_new + pl_ * w_old
                m = mn
            if final:
                _attn_st(o_refs[0], sl, o / l)
                _attn_st(o_refs[1], sl, m + jnp.log(l))
            else:
                _attn_st(o_refs[0], sl, o)
                _attn_st(o_refs[1], sl, m)
                _attn_st(o_refs[2], sl, l)

        if host:
            @pl.when((pl.program_id(0) == nres - 1) & (pl.program_id(1) == ncol - 1) & (b == nb - 1))
            def _():
                host.wait(hin, hout, *sems)

    qk = _attn_view(qkv, d)
    prev_v = [_attn_view(t, d) for t in prev] if has_prev else []
    sp = functools.partial(_attn_spec, S, d, lb)
    any_spec = pl.BlockSpec(memory_space=pl.ANY)
    res = _pcall(
        body, name=name,
        grid=(nres, ncol, nb),
        in_specs=[sp(cq), sp(ck, -1), sp(ck), sp(cv, -1), sp(cv)] + [sp(ca)] * len(prev_v) + [any_spec] * nhi,
        out_specs=[sp(ca)] * n_out + [any_spec] * nho,
        out_shape=[jax.ShapeDtypeStruct(view + (A,), F32)] * n_out + (list(host.out_shapes) if host else []),
        scratch_shapes=[pltpu.SemaphoreType.DMA((host.n_sem,)), pltpu.SemaphoreType.DMA((host.n_sem,))] if host else [],
        compiler_params=_params(*(("arbitrary",) * 3 if host else ("parallel", "parallel", "arbitrary"))),
    )(qk, qk, qk, qk, qk, *prev_v, *(host.ins if host else []))
    return [t.reshape(S, A) for t in res[:n_out]], res[n_out:]


def _attn_fwd(qkv, ex):
    st = None
    for i, d in enumerate(DILATIONS):
        name = "attn_fwd_d%d" % d
        host, done = _carried(ex, "ag", name)
        st, outs = _attn_fwd_stage(name, qkv, d, st, final=(i == len(DILATIONS) - 1), host=host)
        done(outs)
    return st


def _attn_delta(dya, ya):
    S, A = ya.shape
    ri = lax.broadcasted_iota(jnp.int32, (A, A), 0) // HEAD_DIM
    ci = lax.broadcasted_iota(jnp.int32, (A, A), 1) // HEAD_DIM
    ones_bd = (ri == ci).astype(BF16)

    def fn(ins, ps):
        prod = ins[0] * ins[1]
        hi = prod.astype(BF16)
        lo = (prod - hi.astype(F32)).astype(BF16)
        return [_dot(hi, ps[0]) + _dot(lo, ps[0])], []

    return _rowwise("attn_delta", fn, [dya, ya], [ones_bd], [(A, F32)])[0]


def _attn_bwd_stage(name, qkv, do, lse, delta, d, prev):
    S, A3 = qkv.shape
    A = A3 // 3
    lb, ncol, (cq, ck, cv), ca = _attn_cols(A)
    view, _, _, nres, nb = _attn_geom(S, d)
    scale = HEAD_DIM ** -0.5
    has_prev = prev is not None
    lane_slices = [slice(hp * LANES, (hp + 1) * LANES) for hp in range(lb // LANES)]

    def body(*refs):
        q_ref, kp_ref, kc_ref, vp_ref, vc_ref, do_ref, lse_ref, dl_ref = refs[:8]
        p_refs = refs[8:11] if has_prev else ()
        dq_ref, dk_ref, dv_ref, dk_c, dv_c = refs[8 + len(p_refs):]
        b = pl.program_id(2)

        def put_keys(sl, dk, dv):
            if has_prev:
                dk, dv = dk + _attn_ld(p_refs[1], sl), dv + _attn_ld(p_refs[2], sl)
            _attn_st(dk_ref, sl, dk)
            _attn_st(dv_ref, sl, dv)

        @pl.when(b == 0)
        def _():
            dk_c[...] = jnp.zeros(dk_c.shape, F32)
            dv_c[...] = jnp.zeros(dv_c.shape, F32)

        @pl.when(b < nb)
        def _():
            qp, kp_, is_prev = _attn_valid(d)
            valid = (is_prev & (kp_ >= qp) & (b > 0)) | (jnp.logical_not(is_prev) & (kp_ <= qp))
            hm, hm2 = _head_masks(), _head_masks(2 * SWA_BLOCK)
            for sl in lane_slices:
                q, do_, lse_, dl_ = (_attn_ld(r, sl) for r in (q_ref, do_ref, lse_ref, dl_ref))
                k2 = jnp.concatenate([_attn_ld(kp_ref, sl), _attn_ld(kc_ref, sl)], axis=0)
                v2 = jnp.concatenate([_attn_ld(vp_ref, sl), _attn_ld(vc_ref, sl)], axis=0).astype(BF16)
                k2b = k2.astype(BF16)
                dq = jnp.zeros((SWA_BLOCK, LANES), F32)
                dk2 = jnp.zeros((2 * SWA_BLOCK, LANES), F32)
                dv2 = jnp.zeros((2 * SWA_BLOCK, LANES), F32)
                for hh in range(2):
                    qh, doh = _keep(hm[hh], q), _keep(hm[hh], do_)
                    lh, dh = _per_head(lse_, hh == 0), _per_head(dl_, hh == 0)
                    lh2, dh2 = jnp.concatenate([lh, lh], axis=1), jnp.concatenate([dh, dh], axis=1)
                    p = jnp.where(valid, jnp.exp(_dot_nt(qh, k2b) * scale - lh2), 0.0)
                    ds = (p * (_dot_nt(doh, v2) - dh2)).astype(BF16)
                    dq = dq + _dot(ds, _keep(hm2[hh], k2))
                    dk2 = dk2 + _dot_tn(ds, qh)
                    dv2 = dv2 + _dot_tn(p.astype(BF16), doh)
                dq, dk2 = dq * scale, dk2 * scale
                if has_prev:
                    dq = dq + _attn_ld(p_refs[0], sl)
                _attn_st(dq_ref, sl, dq)
                put_keys(sl, dk_c[:, sl] + dk2[:SWA_BLOCK], dv_c[:, sl] + dv2[:SWA_BLOCK])
                dk_c[:, sl] = dk2[SWA_BLOCK:]
                dv_c[:, sl] = dv2[SWA_BLOCK:]

        @pl.when(b == nb)
        def _():
            for sl in lane_slices:
                put_keys(sl, dk_c[:, sl], dv_c[:, sl])

    qk = _attn_view(qkv, d)
    acts = [_attn_view(t, d) for t in (do, lse, delta)] + ([_attn_view(t, d) for t in prev] if has_prev else [])
    sp = functools.partial(_attn_spec, S, d, lb)
    res = _pcall(
        body, name=name,
        grid=(nres, ncol, nb + 1),
        in_specs=[sp(cq), sp(ck, -1), sp(ck), sp(cv, -1), sp(cv), sp(ca), sp(ca), sp(ca)]
        + ([sp(ca), sp(ca, -1), sp(ca, -1)] if has_prev else []),
        out_specs=[sp(ca), sp(ca, -1), sp(ca, -1)],
        out_shape=[jax.ShapeDtypeStruct(view + (A,), F32)] * 3,
        scratch_shapes=[pltpu.VMEM((SWA_BLOCK, lb), F32)] * 2,
        compiler_params=_params("parallel", "parallel", "arbitrary"),
    )(qk, qk, qk, qk, qk, *acts)
    return [t.reshape(S, A) for t in res]


def _attn_bwd(qkv, dya, ya, lse):
    delta = _attn_delta(dya, ya)
    sums = None
    for d in DILATIONS:
        sums = _attn_bwd_stage("attn_bwd_d%d" % d, qkv, dya, lse, delta, d, sums)
    return sums


def _ssm_perm(a, T):
    S, w = a.shape
    return a.reshape(S // T, SUBLANES, T // SUBLANES, w).transpose(0, 2, 1, 3).reshape(S, w)


def _ssm_unperm(a, T):
    S, w = a.shape
    return a.reshape(S // T, T // SUBLANES, SUBLANES, w).transpose(0, 2, 1, 3).reshape(S, w)


def _ssm_powers(lam_ref, pw_ref, T, ns):
    tc = T // SUBLANES
    n = (lax.broadcasted_iota(jnp.int32, (tc, 1), 0) + 1).astype(F32)
    mag = jnp.exp(n * lam_ref[0, 0:1, :])
    ang = n * lam_ref[0, 1:2, :]
    rows8 = lambda t: jnp.broadcast_to(t[:, None, :], (tc, SUBLANES, ns)).reshape(T, ns)
    pw_ref[:, 0:ns] = rows8(mag * jnp.cos(ang))
    pw_ref[:, ns:2 * ns] = rows8(mag * jnp.sin(ang))


def _ssm_scan(xs, off, pw_ref, carry_ref, T, ns, reverse):
    Tc = T // SUBLANES
    sgn = -1.0 if reverse else 1.0
    ar, ai = pw_ref[0:SUBLANES, 0:ns], sgn * pw_ref[0:SUBLANES, ns:2 * ns]

    def rows(i):
        return pl.ds(pl.multiple_of(off + i * SUBLANES, SUBLANES), SUBLANES)

    def step(k, h):
        hr, hi = h
        r = rows(Tc - 1 - k if reverse else k)
        nr = ar * hr - ai * hi + xs[r, 0:ns]
        ni = ar * hi + ai * hr + xs[r, ns:2 * ns]
        xs[r, 0:ns] = nr
        xs[r, ns:2 * ns] = ni
        return nr, ni

    z = jnp.zeros((SUBLANES, ns), F32)
    er, ei = lax.fori_loop(0, Tc, step, (z, z), unroll=4)
    atr, ati = pw_ref[T - SUBLANES:T, 0:ns], sgn * pw_ref[T - SUBLANES:T, ns:2 * ns]
    rowid = lax.broadcasted_iota(jnp.int32, (SUBLANES, ns), 0)
    cr, ci = carry_ref[:, 0:ns], carry_ref[:, ns:2 * ns]
    ctr, cti = z, z
    for jj in range(SUBLANES):
        j = SUBLANES - 1 - jj if reverse else jj
        sel = rowid == j
        ctr, cti = jnp.where(sel, cr, ctr), jnp.where(sel, ci, cti)
        ejr = jnp.broadcast_to(jnp.sum(jnp.where(sel, er, 0.0), axis=0, keepdims=True), (SUBLANES, ns))
        eji = jnp.broadcast_to(jnp.sum(jnp.where(sel, ei, 0.0), axis=0, keepdims=True), (SUBLANES, ns))
        cr, ci = ejr + atr * cr - ati * ci, eji + atr * ci + ati * cr
    carry_ref[:, 0:ns] = cr
    carry_ref[:, ns:2 * ns] = ci

    def fix(i, _):
        r = rows(i)
        pr_rows = pl.ds(pl.multiple_of((Tc - 1 - i if reverse else i) * SUBLANES, SUBLANES), SUBLANES)
        pr, pi = pw_ref[pr_rows, 0:ns], sgn * pw_ref[pr_rows, ns:2 * ns]
        xs[r, 0:ns] += pr * ctr - pi * cti
        xs[r, ns:2 * ns] += pr * cti + pi * ctr
        return 0

    lax.fori_loop(0, Tc, fix, 0, unroll=4)
    return ctr, cti


def _ssm_fwd(ufp, bb, cc, lam_dt, drow, T):
    S, W = ufp.shape
    GB, cw, ns2 = bb.shape
    ns = ns2 // 2
    NCH = S // T

    def body(uf_ref, bb_ref, cc_ref, lam_ref, d_ref, y_ref, hs_ref, xs, pw, carry):
        @pl.when(pl.program_id(1) == 0)
        def _():
            _ssm_powers(lam_ref, pw, T, ns)
            carry[...] = jnp.zeros(carry.shape, F32)

        uf = uf_ref[...]
        xs[...] = _dot(uf.astype(BF16), bb_ref[0])
        hs_ref[0, 0] = carry[...]
        _ssm_scan(xs, 0, pw, carry, T, ns, reverse=False)
        y_ref[...] = _dot(xs[...].astype(BF16), cc_ref[0]) + d_ref[...] * uf

    return _pcall(
        body, name="ssm_fwd",
        grid=(GB, NCH),
        in_specs=[pl.BlockSpec((T, cw), lambda g, c: (c, g)),
                  pl.BlockSpec((1, cw, ns2), lambda g, c: (g, 0, 0)),
                  pl.BlockSpec((1, ns2, cw), lambda g, c: (g, 0, 0)),
                  pl.BlockSpec((1, 2, ns), lambda g, c: (g, 0, 0)),
                  pl.BlockSpec((1, cw), lambda g, c: (0, g))],
        out_specs=[pl.BlockSpec((T, cw), lambda g, c: (c, g)),
                   pl.BlockSpec((1, 1, SUBLANES, ns2), lambda g, c: (g, c, 0, 0))],
        out_shape=[jax.ShapeDtypeStruct((S, W), F32),
                   jax.ShapeDtypeStruct((GB, NCH, SUBLANES, ns2), F32)],
        scratch_shapes=[pltpu.VMEM((T, ns2), F32), pltpu.VMEM((T, ns2), F32), pltpu.VMEM((SUBLANES, ns2), F32)],
        compiler_params=_params("arbitrary", "arbitrary"),
    )(ufp, bb, cc, lam_dt, drow)


def _ssm_bwd(ufp, dyp, bb, bbt, cc, cct, lam_dt, drow, hstart, T):
    S, W = ufp.shape
    GB, cw, ns2 = bb.shape
    ns = ns2 // 2
    NCH = S // T

    def body(uf_ref, dy_ref, bb_ref, bbt_ref, cc_ref, cct_ref, lam_ref, d_ref, hs_ref,
             duf_ref, dbb_ref, dcc_ref, da_ref, dd_ref, hb, ls, pw, carry_f, carry_b):
        @pl.when(pl.program_id(1) == 0)
        def _():
            _ssm_powers(lam_ref, pw, T, ns)
            carry_b[...] = jnp.zeros(carry_b.shape, F32)
            dbb_ref[...] = jnp.zeros(dbb_ref.shape, F32)
            dcc_ref[...] = jnp.zeros(dcc_ref.shape, F32)
            da_ref[...] = jnp.zeros(da_ref.shape, F32)
            dd_ref[...] = jnp.zeros(dd_ref.shape, F32)

        uf, dy = uf_ref[...], dy_ref[...]
        ufb, dyb = uf.astype(BF16), dy.astype(BF16)
        hb[SUBLANES:T + SUBLANES, :] = _dot(ufb, bb_ref[0])
        carry_f[...] = hs_ref[0, 0]
        ctr, cti = _ssm_scan(hb, SUBLANES, pw, carry_f, T, ns, reverse=False)
        hb[0:SUBLANES, 0:ns] = ctr
        hb[0:SUBLANES, ns:ns2] = cti
        ls[...] = _dot(dyb, cct_ref[0])
        _ssm_scan(ls, 0, pw, carry_b, T, ns, reverse=True)
        lv = ls[...]
        lb = lv.astype(BF16)
        dbb_ref[0] += _dot_tn(ufb, lb)
        dcc_ref[0] += _dot_tn(hb[SUBLANES:T + SUBLANES, :].astype(BF16), dyb)
        lr, li = lv[:, 0:ns], lv[:, ns:ns2]
        hpr, hpi = hb[0:T, 0:ns], hb[0:T, ns:ns2]
        dar = jnp.sum(lr * hpr + li * hpi, axis=0, keepdims=True)
        dai = jnp.sum(li * hpr - lr * hpi, axis=0, keepdims=True)
        da_ref[0, 0:1, 0:ns] += dar
        da_ref[0, 0:1, ns:ns2] += dai
        duf_ref[...] = _dot(lb, bbt_ref[0]) + d_ref[...] * dy
        dd_ref[...] += jnp.sum(dy * uf, axis=0, keepdims=True)

    rc = lambda c: NCH - 1 - c
    return _pcall(
        body, name="ssm_bwd",
        grid=(GB, NCH),
        in_specs=[pl.BlockSpec((T, cw), lambda g, c: (rc(c), g)),
                  pl.BlockSpec((T, cw), lambda g, c: (rc(c), g)),
                  pl.BlockSpec((1, cw, ns2), lambda g, c: (g, 0, 0)),
                  pl.BlockSpec((1, ns2, cw), lambda g, c: (g, 0, 0)),
                  pl.BlockSpec((1, ns2, cw), lambda g, c: (g, 0, 0)),
                  pl.BlockSpec((1, cw, ns2), lambda g, c: (g, 0, 0)),
                  pl.BlockSpec((1, 2, ns), lambda g, c: (g, 0, 0)),
                  pl.BlockSpec((1, cw), lambda g, c: (0, g)),
                  pl.BlockSpec((1, 1, SUBLANES, ns2), lambda g, c: (g, rc(c), 0, 0))],
        out_specs=[pl.BlockSpec((T, cw), lambda g, c: (rc(c), g)),
                   pl.BlockSpec((1, cw, ns2), lambda g, c: (g, 0, 0)),
                   pl.BlockSpec((1, ns2, cw), lambda g, c: (g, 0, 0)),
                   pl.BlockSpec((1, SUBLANES, ns2), lambda g, c: (g, 0, 0)),
                   pl.BlockSpec((1, cw), lambda g, c: (0, g))],
        out_shape=[jax.ShapeDtypeStruct((S, W), F32),
                   jax.ShapeDtypeStruct((GB, cw, ns2), F32),
                   jax.ShapeDtypeStruct((GB, ns2, cw), F32),
                   jax.ShapeDtypeStruct((GB, SUBLANES, ns2), F32),
                   jax.ShapeDtypeStruct((1, W), F32)],
        scratch_shapes=[pltpu.VMEM((T + SUBLANES, ns2), F32), pltpu.VMEM((T, ns2), F32), pltpu.VMEM((T, ns2), F32),
                        pltpu.VMEM((SUBLANES, ns2), F32), pltpu.VMEM((SUBLANES, ns2), F32)],
        compiler_params=_params("arbitrary", "arbitrary"),
    )(ufp, dyp, bb, bbt, cc, cct, lam_dt, drow, hstart)


def _ssm_disc_math(lr, li, logdt, br, bi):
    dt = jnp.exp(logdt)
    mag = jnp.exp(lr * dt)
    ar = mag * jnp.cos(li * dt)
    ai = mag * jnp.sin(li * dt)
    nr, ni = ar - 1.0, ai
    den = lr * lr + li * li
    cr = (nr * lr + ni * li) / den
    ci = (ni * lr - nr * li) / den
    return ar, ai, cr * br - ci * bi, cr * bi + ci * br


def _ssm_disc(lr, li, logdt, br, bi):
    C = br.shape[1]

    def fn(ins, ps):
        _, _, bbr, bbi = _ssm_disc_math(*ins)
        dt = jnp.exp(ins[2])
        return [ins[0] * dt, ins[1] * dt, bbr, bbi], []

    return _rowwise("ssm_disc", fn, [lr, li, logdt, br, bi], [], [(1, F32), (1, F32), (C, F32), (C, F32)], ts=512)


def _ssm_disc_bwd(lr, li, logdt, br, bi, dar, dai, dbbr, dbbi):
    C = br.shape[1]

    def fn(ins, ps):
        _, vjp = jax.vjp(_ssm_disc_math, *ins[:5])
        return list(vjp(tuple(ins[5:]))), []

    return _rowwise("ssm_disc_bwd", fn, [lr, li, logdt, br, bi, dar, dai, dbbr, dbbi], [],
                    [(1, F32), (1, F32), (1, F32), (C, F32), (C, F32)], ts=512)


def _block_diag(t):
    GB, g, a, b = t.shape
    eye = jnp.eye(g, dtype=t.dtype)
    return (t[:, :, :, None, :] * eye[None, :, None, :, None]).reshape(GB, g * a, g * b)


def _block_diag_take(t, g):
    GB, ga, gb_ = t.shape
    a, b = ga // g, gb_ // g
    eye = jnp.eye(g, dtype=t.dtype)
    return (t.reshape(GB, g, a, g, b) * eye[None, :, None, :, None]).sum(axis=3)


def _loss_head(h4, tgt, gf):
    D = h4.shape[1]

    def fn(ins, ps):
        x, t = ins
        xh, r = _xhat(x)
        err = xh * ps[0] - t
        dn = err * (1.0 / D)
        dxh = dn * ps[0]
        dx = r * (dxh - xh * jnp.mean(dxh * xh, axis=-1, keepdims=True))
        return [dx], [jnp.sum(err * err, axis=0, keepdims=True), jnp.sum(dn * xh, axis=0, keepdims=True)]

    return _rowwise("loss_head", fn, [h4, tgt], [gf], [(D, F32)], accs=[D, D])


def _gelu(x):
    return 0.5 * x * (1.0 + jnp.tanh(GELU_C * (x + GELU_K * x * x * x)))


def _gelu_grad(x):
    t = jnp.tanh(GELU_C * (x + GELU_K * x * x * x))
    return 0.5 * (1.0 + t) + 0.5 * x * (1.0 - t * t) * GELU_C * (1.0 + 3.0 * GELU_K * x * x)


def _mesh_pos():
    return lax.axis_index("x"), lax.axis_index("y"), lax.axis_index("c")


def _other_chips(x, y):
    return [(1 - x, y), (x, 1 - y), (1 - x, 1 - y)]


def _remote(src, dst, send, recv, dev):
    return pltpu.make_async_remote_copy(src_ref=src, dst_ref=dst, send_sem=send, recv_sem=recv,
                                        device_id=dev, device_id_type=MESH)


ANY = pl.BlockSpec(memory_space=pl.ANY)


COMM_BLOCK_BYTES = 3 << 20


def _place():
    x, y, c = _mesh_pos()
    return jnp.stack([c] + [2 * cx + cy for cx, cy in _other_chips(x, y)] + [2 * x + y]).astype(jnp.int32)


def _send_chips(name, srcs, specs, tr, nth, cw):
    hr = nth * tr
    n = len(srcs)

    def body(*refs):
        got_ref, send, recv = refs[1 + n:]
        t = pl.program_id(0)
        x, y, c = _mesh_pos()
        cps = []
        for j, chip in enumerate(_other_chips(x, y)):
            dst = got_ref.at[pl.ds(pl.multiple_of(j * hr + t * tr, 16), tr), :]
            cp = _remote(refs[1 + j % n], dst, send.at[j], recv.at[j], (*chip, c))
            cp.start()
            cps.append(cp)
        for cp in cps:
            cp.wait_send()

        @pl.when(t == nth - 1)
        def _():
            for j in range(3):
                r_ = got_ref.at[pl.ds(j * hr, hr), :]
                _remote(r_, r_, send.at[j], recv.at[j], (x, y, c)).wait_recv()

    return _pcall(
        body, name=name,
        grid_spec=pltpu.PrefetchScalarGridSpec(
            num_scalar_prefetch=1, grid=(nth,), in_specs=specs, out_specs=ANY,
            scratch_shapes=[pltpu.SemaphoreType.DMA((3,)), pltpu.SemaphoreType.DMA((3,))]),
        out_shape=jax.ShapeDtypeStruct((3 * hr, cw), srcs[0].dtype),
        compiler_params=_params("arbitrary"),
    )(_place(), *srcs)


def _ag_assemble(name, shard, stage, axis, tr, nth):
    R, cc = shard.shape
    hr = nth * tr
    full = (R, N_CHIPS * cc) if axis == 1 else (N_CHIPS * R, cc)

    def body(pl_ref, s0, s1, s2, h0, h1, out_ref, send, recv, lsem):
        t = pl.program_id(0)
        x, y, c = _mesh_pos()

        def region(s, half):
            if axis == 1:
                return out_ref.at[pl.ds(pl.multiple_of(half * hr + t * tr, 16), tr), pl.ds(pl.multiple_of(s * cc, LANES), cc)]
            return out_ref.at[pl.ds(pl.multiple_of(s * R + half * hr + t * tr, 16), tr), :]

        cps = []
        for j, src in enumerate((s0, s1, s2)):
            dst = region(pl_ref[1 + j], c)
            cps.append(_remote(src, dst, send.at[j], recv, (x, y, 1 - c)))
            cps.append(pltpu.make_async_copy(src, dst, lsem.at[j]))
        for half, src in enumerate((h0, h1)):
            cps.append(pltpu.make_async_copy(src, region(pl_ref[4], half), lsem.at[3 + half]))
        for cp in cps:
            cp.start()
        for k, cp in enumerate(cps):
            if k < 6 and k % 2 == 0:
                cp.wait_send()
            else:
                cp.wait()

        @pl.when(t == nth - 1)
        def _():
            r_ = out_ref.at[pl.ds(0, hr), pl.ds(0, 3 * cc)] if axis == 1 else out_ref.at[pl.ds(0, 3 * hr), :]
            _remote(r_, r_, send.at[0], recv, (x, y, c)).wait_recv()

    blk = lambda f: pl.BlockSpec((tr, cc), f)
    return _pcall(
        body, name=name,
        grid_spec=pltpu.PrefetchScalarGridSpec(
            num_scalar_prefetch=1, grid=(nth,),
            in_specs=[blk(lambda t, p, j=j: (j * nth + t, 0)) for j in range(3)]
            + [blk(lambda t, p, h=h: (h * nth + t, 0)) for h in range(2)],
            out_specs=ANY,
            scratch_shapes=[pltpu.SemaphoreType.DMA((3,)), pltpu.SemaphoreType.DMA, pltpu.SemaphoreType.DMA((5,))]),
        out_shape=jax.ShapeDtypeStruct(full, shard.dtype),
        compiler_params=_params("arbitrary"),
    )(_place(), stage, stage, stage, shard, shard)


def _comm_rows(hr, row_bytes):
    return _tile(hr, max(16, COMM_BLOCK_BYTES // row_bytes // 16 * 16), 16)


def _host_send(items):
    def copies(ins, outs, send, recv):
        x, y, c = _mesh_pos()
        cps = []
        for w, (_, kind, hr, cw) in enumerate(items):
            for j, (cx, cy) in enumerate(_other_chips(x, y)):
                s = 2 * cx + cy
                if kind == "half":
                    src = ins[w].at[pl.ds(pl.multiple_of(c * hr, 16), hr), :]
                elif kind == "cols":
                    src = ins[w].at[:, pl.ds(pl.multiple_of(s * cw, LANES), cw)]
                else:
                    src = ins[w].at[pl.ds(pl.multiple_of(s * hr, 16), hr), :]
                cps.append(_remote(src, outs[w].at[pl.ds(j * hr, hr), :], send.at[3 * w + j], recv.at[3 * w + j], (cx, cy, c)))
        return cps

    def start(ins, outs, send, recv):
        for cp in copies(ins, outs, send, recv):
            cp.start()

    def wait(ins, outs, send, recv):
        for cp in copies(ins, outs, send, recv):
            cp.wait()

    return _Host([a for a, _, _, _ in items], [jax.ShapeDtypeStruct((3 * hr, cw), a.dtype) for a, _, hr, cw in items],
                 3 * len(items), start, wait)


def _ag_send(name, sh):
    R, cc = sh.shape
    tr = _comm_rows(R // 2, cc * 2)
    nth = R // 2 // tr
    return _send_chips(name, [sh], [pl.BlockSpec((tr, cc), lambda t, p: (p[0] * nth + t, 0))], tr, nth, cc)


def _ag_finish(name, sh, stage, axis):
    R, cc = sh.shape
    tr = _comm_rows(R // 2, cc * 2)
    return _ag_assemble(name, sh, stage, axis, tr, R // 2 // tr)


def _pair_sum(name, g, tr, nblk, blk):
    cw = g.shape[1]
    c_arr = lax.axis_index("c").astype(jnp.int32).reshape(1)

    def body(c_ref, keep_ref, send_ref, out_ref, land, send, recv):
        i = pl.program_id(0)
        slot = lax.rem(i, 2)
        x, y, c = _mesh_pos()
        cp = _remote(send_ref, land.at[slot], send.at[slot], recv.at[slot], (x, y, 1 - c))
        cp.start()
        cp.wait_recv()
        out_ref[...] = (keep_ref[...].astype(F32) + land[slot].astype(F32)).astype(BF16)
        cp.wait_send()

    return _pcall(
        body, name=name,
        grid_spec=pltpu.PrefetchScalarGridSpec(
            num_scalar_prefetch=1, grid=(nblk,),
            in_specs=[pl.BlockSpec((tr, cw), lambda i, c_ref: (blk(i, c_ref[0]), 0)),
                      pl.BlockSpec((tr, cw), lambda i, c_ref: (blk(i, 1 - c_ref[0]), 0))],
            out_specs=pl.BlockSpec((tr, cw), lambda i, c_ref: (i, 0)),
            scratch_shapes=[pltpu.VMEM((2, tr, cw), BF16), pltpu.SemaphoreType.DMA((2,)), pltpu.SemaphoreType.DMA((2,))]),
        out_shape=jax.ShapeDtypeStruct((nblk * tr, cw), BF16),
        compiler_params=_params("arbitrary"),
    )(c_arr, g, g)


def _sum_chips_swap(name, q, got, qspec, tr, nth, cw):
    def body(p_ref, q_ref, g0, g1, g2, out_ref, buf, send, recv, lsem):
        t = pl.program_id(0)
        x, y, c = _mesh_pos()
        buf[...] = q_ref[...].astype(F32) + g0[...].astype(F32) + g1[...].astype(F32) + g2[...].astype(F32)
        dst = out_ref.at[pl.ds(pl.multiple_of((p_ref[0] * nth + t) * tr, 16), tr), :]
        cp = _remote(buf, dst, send, recv, (x, y, 1 - c))
        lc = pltpu.make_async_copy(buf, dst, lsem)
        cp.start()
        lc.start()
        lc.wait()
        cp.wait_send()

        @pl.when(t == nth - 1)
        def _():
            theirs = out_ref.at[pl.ds(0, nth * tr), :]
            _remote(theirs, theirs, send, recv, (x, y, c)).wait_recv()

    return _pcall(
        body, name=name,
        grid_spec=pltpu.PrefetchScalarGridSpec(
            num_scalar_prefetch=1, grid=(nth,),
            in_specs=[qspec] + [pl.BlockSpec((tr, cw), lambda t, p, j=j: (j * nth + t, 0)) for j in range(3)],
            out_specs=ANY,
            scratch_shapes=[pltpu.VMEM((tr, cw), F32), pltpu.SemaphoreType.DMA, pltpu.SemaphoreType.DMA,
                            pltpu.SemaphoreType.DMA]),
        out_shape=jax.ShapeDtypeStruct((2 * nth * tr, cw), F32),
        compiler_params=_params("arbitrary"),
    )(_place(), q, got, got, got)


def _rs_geom(g, axis):
    rows, gw = g.shape
    return (rows // 2, gw // N_CHIPS) if axis == 1 else (rows // N_CHIPS // 2, gw)


def _rs_pair_sum(tag, g, axis):
    hr, _ = _rs_geom(g, axis)
    tr = _comm_rows(hr, g.shape[1] * 2)
    nth = hr // tr
    if axis == 1:
        nblk, blk = nth, (lambda i, half: half * nth + i)
    else:
        nblk, blk = N_CHIPS * nth, (lambda i, half: (i // nth) * (2 * nth) + half * nth + i % nth)
    return _pair_sum("rs_pair_" + tag, g, tr, nblk, blk)


def _rs_part(q, axis, hr, cw, tr):
    nth = hr // tr
    if axis == 1:
        return lambda k: pl.BlockSpec((tr, cw), lambda t, p: (t, p[k]))
    return lambda k: pl.BlockSpec((tr, cw), lambda t, p: (p[k] * nth + t, 0))


def _rs_send(tag, q, axis, hr, cw):
    tr = _comm_rows(hr, cw * 2)
    part = _rs_part(q, axis, hr, cw, tr)
    return _send_chips("rs_send_" + tag, [q, q, q], [part(1), part(2), part(3)], tr, hr // tr, cw)


def _rs_finish(tag, q, got, axis, hr, cw):
    tr = _comm_rows(hr, cw * 4)
    return _sum_chips_swap("rs_sum_swap_" + tag, q, got, _rs_part(q, axis, hr, cw, tr)(4), tr, hr // tr, cw)


class _Exchange:
    AG_PLAN = {"ffn1_up": ("ffn1_w_down", "ffn2_w_gate"), "ffn1_down": ("ffn2_w_up", "w_in"),
               "w_in_qkv": ("ssm_w_glu", "w_out", "ple_w_gate", "ple_w_proj"), "attn_fwd_d1": ("ffn2_w_down",)}
    RS_PLAN = {"ffn1_dact": ("ffn2_w_gate", "ple_w_gate", "ple_w_proj"), "ffn1_dwd": ("ffn2_w_up",),
               "ffn1_dwgu": ("ffn2_w_down", "w_in", "ssm_w_glu", "w_out", "ffn1_w_down"),
               "ffn1_dn": ("ffn1_w_gate", "ffn1_w_up")}

    def __init__(self, shards, axes):
        self.shards, self.axes = shards, axes
        self.stage, self.full, self.q, self.geom, self.got = {}, {}, {}, {}, {}

    def ag_host(self, kernel):
        item = lambda k: (self.shards[k], "half", self.shards[k].shape[0] // 2, self.shards[k].shape[1])
        return _host_send([item(k) for k in self.AG_PLAN[kernel]])

    def ag_done(self, kernel, stages):
        self.stage.update(zip(self.AG_PLAN[kernel], stages))

    def weight(self, k):
        if k not in self.full:
            stage = self.stage[k] if k in self.stage else _ag_send("ag_send_" + k, self.shards[k])
            self.full[k] = _ag_finish("ag_asm_" + k, self.shards[k], stage, self.axes[k])
        return self.full[k]

    def grad(self, k, g):
        self.q[k], self.geom[k] = _rs_pair_sum(k, g, self.axes[k]), _rs_geom(g, self.axes[k])

    def rs_host(self, kernel):
        item = lambda k: (self.q[k], "cols" if self.axes[k] == 1 else "rows") + self.geom[k]
        return _host_send([item(k) for k in self.RS_PLAN[kernel]])

    def rs_done(self, kernel, gots):
        self.got.update(zip(self.RS_PLAN[kernel], gots))

    def finish(self):
        return {k: _rs_finish(k, q, self.got[k] if k in self.got else _rs_send(k, q, self.axes[k], *self.geom[k]),
                              self.axes[k], *self.geom[k]) for k, q in self.q.items()}


def _all_reduce_small(v):
    n = v.shape[0]
    h = n // 2

    def body(v_ref, out_ref, pair_in, chips_in, send, recv):
        x, y, c = _mesh_pos()
        me, sib, my_chip = (x, y, c), (x, y, 1 - c), 2 * x + y
        mine = pl.ds(pl.multiple_of(c * h, SUBLANES), h)
        other = pl.ds(pl.multiple_of((1 - c) * h, SUBLANES), h)
        pair = _remote(v_ref.at[other], pair_in, send.at[0], recv.at[0], sib)
        pair.start()
        pair.wait()
        chips_in[my_chip] = v_ref[mine, :] + pair_in[...]
        cps = []
        for j, (cx, cy) in enumerate(_other_chips(x, y)):
            cp = _remote(chips_in.at[my_chip], chips_in.at[my_chip], send.at[1 + j], recv.at[1 + j], (cx, cy, c))
            cp.start()
            cps.append(cp)
        for j, (cx, cy) in enumerate(_other_chips(x, y)):
            slot = chips_in.at[2 * cx + cy]
            _remote(slot, slot, send.at[1 + j], recv.at[1 + j], me).wait_recv()
        out_ref[mine, :] = (chips_in[0] + chips_in[1]) + (chips_in[2] + chips_in[3])
        for cp in cps:
            cp.wait_send()
        swap = _remote(out_ref.at[mine, :], out_ref.at[mine, :], send.at[4], recv.at[4], sib)
        swap.start()
        _remote(out_ref.at[other, :], out_ref.at[other, :], send.at[4], recv.at[4], me).wait_recv()
        swap.wait_send()

    return _pcall(
        body, name="ar_small",
        in_specs=[pl.BlockSpec(memory_space=pltpu.VMEM)], out_specs=pl.BlockSpec(memory_space=pltpu.VMEM),
        out_shape=jax.ShapeDtypeStruct((n, LANES), F32),
        scratch_shapes=[pltpu.VMEM((h, LANES), F32), pltpu.VMEM((N_CHIPS, h, LANES), F32),
                        pltpu.SemaphoreType.DMA((5,)), pltpu.SemaphoreType.DMA((5,))],
        compiler_params=pltpu.CompilerParams(vmem_limit_bytes=V7X_VMEM_LIMIT_BYTES),
    )(v)


def _adamw(name, w, g, m, v):
    R, Cc = w.shape
    tr = _tile(R, max(8, (3 << 18) // Cc // 8 * 8), 8)
    c1 = 1.0 - ADAM_B1 ** ADAM_STEP
    c2 = 1.0 - ADAM_B2 ** ADAM_STEP

    def body(w_ref, g_ref, m_ref, v_ref, d_ref, nm_ref, nv_ref):
        g_ = g_ref[...]
        nm = ADAM_B1 * m_ref[...] + (1.0 - ADAM_B1) * g_
        nv = ADAM_B2 * v_ref[...] + (1.0 - ADAM_B2) * (g_ * g_)
        d_ref[...] = -ADAM_LR * ((nm / c1) / (jnp.sqrt(nv / c2) + ADAM_EPS) + ADAM_WD * w_ref[...])
        nm_ref[...] = nm
        nv_ref[...] = nv

    spec = pl.BlockSpec((tr, Cc), lambda i: (i, 0))
    return _pcall(
        body, name=name, grid=(R // tr,),
        in_specs=[spec] * 4, out_specs=[spec] * 3,
        out_shape=[jax.ShapeDtypeStruct((R, Cc), F32)] * 3,
        compiler_params=_params("parallel"),
    )(w, g, m, v)


def _pack(arrs, rows):
    flat = jnp.concatenate([a.reshape(-1) for a in arrs])
    return jnp.pad(flat, (0, rows * LANES - flat.shape[0])).reshape(rows, LANES)


def _unpack(packed, like):
    flat, out, o = packed.reshape(-1), [], 0
    for a in like:
        out.append(flat[o:o + a.size].reshape(a.shape))
        o += a.size
    return out


BIG = (
    ("ffn1_w_gate", 1), ("ffn1_w_up", 1), ("ffn1_w_down", 0), ("w_in", 1), ("ssm_w_glu", 0), ("w_out", 0),
    ("ffn2_w_gate", 1), ("ffn2_w_up", 1), ("ffn2_w_down", 0), ("ple_w_gate", 0), ("ple_w_proj", 1),
)
SMALL = ("ffn1_norm", "mix_norm", "attn_out_norm", "ssm_lambda_re", "ssm_lambda_im", "ssm_log_dt", "ssm_b_re", "ssm_b_im",
         "ssm_c_re", "ssm_c_im", "ssm_d", "ssm_b_glu", "ssm_out_norm", "ffn2_norm", "ple_norm", "final_norm")
WEIGHTS = ("ffn1_norm", "ffn1_w_gate", "ffn1_w_up", "ffn1_w_down", "mix_norm", "w_in", "attn_out_norm", "ssm_lambda_re",
           "ssm_lambda_im", "ssm_log_dt", "ssm_b_re", "ssm_b_im", "ssm_c_re", "ssm_c_im", "ssm_d", "ssm_w_glu", "ssm_b_glu",
           "ssm_out_norm", "w_out", "ffn2_norm", "ffn2_w_gate", "ffn2_w_up", "ffn2_w_down", "ple_norm", "ple_w_gate",
           "ple_w_proj", "final_norm")


def _pad_to(a, axis, n):
    pad = [(0, 0), (0, 0)]
    pad[axis] = (0, n - a.shape[axis])
    return jnp.pad(a, pad)


def _local_step(x, p, tgt, w, ex):
    S, D = x.shape
    A = w["attn_out_norm"].shape[-1]
    W = w["ssm_d"].shape[-1]
    G, P = w["ssm_lambda_re"].shape[-2:]
    C = w["ssm_b_re"].shape[-1]
    GB = G // SSM_BLOCK_GROUPS
    T = min(1024, S)
    row = lambda name: w[name].reshape(1, -1)
    gs = {}

    h1, ffn1_saved = _ffn_fwd("ffn1", x, row("ffn1_norm"), ex)
    n2 = _rms_fwd("mix_norm", h1, row("mix_norm"))
    w_in = ex.weight("w_in")
    n2p = _to_attn_order(n2)
    host, done = _carried(ex, "ag", "w_in_qkv")
    qkv, *outs = _mm("w_in_qkv", [n2p], [w_in[:, :3 * A]], [F32], tm=1024, tn=1024, host=host)
    done(outs)
    (s_in,) = _mm("w_in_ssm", [n2], [w_in[:, 3 * A:]], [F32], tm=1024, tn=1024)
    ya, lse = _attn_fwd(qkv, ex)

    col = lambda name: w[name].reshape(G * P, 1)
    logdt_x = jnp.repeat(w["ssm_log_dt"].reshape(G), P).reshape(G * P, 1)
    b_re, b_im = w["ssm_b_re"].reshape(G * P, C), w["ssm_b_im"].reshape(G * P, C)
    lrdt, lidt, bbr, bbi = _ssm_disc(col("ssm_lambda_re"), col("ssm_lambda_im"), logdt_x, b_re, b_im)
    gsz = SSM_BLOCK_GROUPS
    to_bb = lambda t: _block_diag(t.reshape(GB, gsz, P, C).transpose(0, 1, 3, 2))
    bb = jnp.concatenate([to_bb(bbr), to_bb(bbi)], axis=2).astype(BF16)
    to_cc = lambda t: _block_diag(t.reshape(GB, gsz, C, P).transpose(0, 1, 3, 2))
    cc = jnp.concatenate([to_cc(w["ssm_c_re"]), -to_cc(w["ssm_c_im"])], axis=1).astype(BF16)
    lam_dt = jnp.stack([lrdt.reshape(GB, gsz * P), lidt.reshape(GB, gsz * P)], axis=1)
    ufp = _ssm_perm(s_in, T)
    ypre, hstart = _ssm_fwd(ufp, bb, cc, lam_dt, row("ssm_d"), T)

    def glu_in(ins, ps):
        yg = _gelu(ins[0])
        return [yg, yg], []

    yg, ygb = _rowwise("ssm_gelu", glu_in, [ypre], [], [(W, F32), (W, BF16)])
    w_glu = ex.weight("ssm_w_glu")

    def glu_out(accs, ex):
        gl = accs[0] + ex[1]
        return [ex[0] * _sigmoid(gl), gl]

    ybp, gl = _mm("ssm_glu", [ygb], [w_glu], [F32, F32], extras=[(yg, "mn"), (row("ssm_b_glu"), "n")],
                  epilogue=glu_out, tm=1024, tn=1024)
    yb = _ssm_unperm(ybp, T)
    na = _from_attn_order(_rms_fwd("attn_out_norm", ya, row("attn_out_norm")))
    nb = _rms_fwd("ssm_out_norm", yb, row("ssm_out_norm"))
    w_out = ex.weight("w_out")
    (h2,) = _mm("w_out", [na, nb], [w_out[:A], w_out[A:]], [F32], pairs=((0, 0, 0), (1, 1, 0)), extras=[(h1, "mn")],
                epilogue=lambda accs, ex: [ex[0] + accs[0]], tm=1024, tn=1024)
    h3, ffn2_saved = _ffn_fwd("ffn2", h2, row("ffn2_norm"), ex)
    n4 = _rms_fwd("ple_norm", h3, row("ple_norm"))
    (pe,) = _mm("ple_proj", [p], [ex.weight("ple_w_proj")], [F32], tm=1024, tn=1024)

    def ple_out(accs, ex):
        gate = _sigmoid(accs[0])
        return [ex[1] + gate * ex[0], gate]

    h4, gate = _mm("ple_gate", [n4], [ex.weight("ple_w_gate")], [F32, F32], extras=[(pe, "mn"), (h3, "mn")],
                   epilogue=ple_out, tm=1024, tn=1024)

    dh4, err2, gs["final_norm"] = _loss_head(h4, tgt, row("final_norm"))
    loss = (0.5 / D) * jnp.sum(err2)

    def ple_bwd(ins, ps):
        dh, gt, pe_ = ins
        return [dh * gt, dh * pe_ * gt * (1.0 - gt)], []

    dpe, dpg = _rowwise("ple_bwd", ple_bwd, [dh4, gate, pe], [], [(D, BF16), (D, BF16)])
    (d_ple_proj,) = _mm("ple_dproj", [p], [dpe], [BF16], ta=True, tm=256, tn=2048, tk=1024)
    (d_ple_gate,) = _mm("ple_dgate", [n4], [dpg], [BF16], ta=True, tm=1024, tn=1024, tk=2048)
    (dn4,) = _mm("ple_dn", [dpg], [ex.weight("ple_w_gate")], [F32], tb=True, tm=1024, tn=1024)
    (dh3, dh3b), gs["ple_norm"] = _rms_bwd("ple_dnorm", dn4, h3, row("ple_norm"), dres=dh4, copy_scale=0.5)
    (dh2, dh2b), gs["ffn2_norm"] = _ffn_bwd("ffn2", dh3, dh3b, h2, row("ffn2_norm"), ex, ffn2_saved, copy_scale=1.0)
    (dna,) = _mm("w_out_dna", [_to_attn_order(dh2b)], [w_out[:A]], [F32], tb=True, tm=1024, tn=1024)
    (dnb,) = _mm("w_out_dnb", [dh2b], [w_out[A:]], [F32], tb=True, tm=1024, tn=1024)
    (d_wout_a,) = _mm("w_out_dwa", [na], [dh2b], [BF16], ta=True, tm=1024, tn=1024, tk=2048)
    (d_wout_b,) = _mm("w_out_dwb", [nb], [dh2b], [BF16], ta=True, tm=1024, tn=1024, tk=2048)
    d_w_out = jnp.concatenate([d_wout_a, d_wout_b], axis=0)
    (dya,), gs["attn_out_norm"] = _rms_bwd("attn_out_dnorm", dna, ya, row("attn_out_norm"))
    (dyb,), gs["ssm_out_norm"] = _rms_bwd("ssm_out_dnorm", dnb, yb, row("ssm_out_norm"))

    dybp = _ssm_perm(dyb, T)

    def glu_bwd(ins, ps):
        dy, yg_, gl_ = ins
        sg = _sigmoid(gl_)
        dgl = dy * yg_ * sg * (1.0 - sg)
        return [dgl, dy * sg], [jnp.sum(dgl, axis=0, keepdims=True)]

    dgl, dyg_direct, gs["ssm_b_glu"] = _rowwise("ssm_glu_bwd", glu_bwd, [dybp, yg, gl], [], [(W, BF16), (W, F32)], accs=[W])
    (d_w_glu,) = _mm("ssm_dwglu", [ygb], [dgl], [BF16], ta=True, tm=1024, tn=1024, tk=2048)
    (dypre,) = _mm("ssm_dyg", [dgl], [w_glu], [F32], tb=True, extras=[(dyg_direct, "mn"), (ypre, "mn")],
                   epilogue=lambda accs, ex: [(accs[0] + ex[0]) * _gelu_grad(ex[1])], tm=1024, tn=1024)
    dufp, dbb, dcc, da, gs["ssm_d"] = _ssm_bwd(ufp, dypre, bb, bb.transpose(0, 2, 1), cc, cc.transpose(0, 2, 1),
                                               lam_dt, row("ssm_d"), hstart, T)
    ns = gsz * P
    from_bb = lambda t: _block_diag_take(t, gsz).transpose(0, 1, 3, 2).reshape(G * P, C)
    from_cc = lambda t: _block_diag_take(t, gsz).transpose(0, 1, 3, 2).reshape(w["ssm_c_re"].shape)
    gs["ssm_c_re"], gs["ssm_c_im"] = from_cc(dcc[:, :ns]), -from_cc(dcc[:, ns:])
    da = da.sum(axis=1)
    dar, dai = da[:, :ns].reshape(G * P, 1), da[:, ns:].reshape(G * P, 1)
    dlr, dli, dlogdt, dbr, dbi = _ssm_disc_bwd(col("ssm_lambda_re"), col("ssm_lambda_im"), logdt_x, b_re, b_im,
                                               dar, dai, from_bb(dbb[:, :, :ns]), from_bb(dbb[:, :, ns:]))
    gs["ssm_lambda_re"], gs["ssm_lambda_im"] = dlr.reshape(w["ssm_lambda_re"].shape), dli.reshape(w["ssm_lambda_im"].shape)
    gs["ssm_log_dt"] = dlogdt.reshape(G, P).sum(axis=1).reshape(w["ssm_log_dt"].shape)
    gs["ssm_b_re"], gs["ssm_b_im"] = dbr.reshape(w["ssm_b_re"].shape), dbi.reshape(w["ssm_b_im"].shape)
    ds_in = _ssm_unperm(dufp, T)

    dq, dk, dv = _attn_bwd(qkv, dya, ya, lse)
    dqkv = jnp.concatenate([dq, dk, dv], axis=1).astype(BF16)
    (d_w_qkv,) = _mm("w_in_dw_qkv", [n2p], [dqkv], [BF16], ta=True, tm=1024, tn=1024, tk=2048)
    (d_w_s,) = _mm("w_in_dw_ssm", [n2], [ds_in], [BF16], ta=True, tm=1024, tn=1024, tk=2048)
    d_w_in = jnp.concatenate([d_w_qkv, d_w_s], axis=1)
    dz = jnp.concatenate([_from_attn_order(dqkv), ds_in.astype(BF16)], axis=1)
    (dn2,) = _mm("w_in_dn", [dz], [w_in], [F32], tb=True, tm=1024, tn=1024)
    (dh1, dh1b), gs["mix_norm"] = _rms_bwd("mix_dnorm", dn2, h1, row("mix_norm"), dres=dh2, copy_scale=0.5)
    for k, g in (("ple_w_gate", d_ple_gate), ("ple_w_proj", d_ple_proj), ("w_out", d_w_out), ("ssm_w_glu", d_w_glu),
                 ("w_in", d_w_in)):
        ex.grad(k, g)
    (dx,), gs["ffn1_norm"] = _ffn_bwd("ffn1", dh1, dh1b, x, row("ffn1_norm"), ex, ffn1_saved, copy_scale=None)
    small = {k: gs[k].reshape(w[k].shape) for k in SMALL}
    return loss, dx, ex.finish(), small


def kernel(x, p, ffn1_norm, ffn1_w_gate, ffn1_w_up, ffn1_w_down, mix_norm, w_in, attn_out_norm, ssm_lambda_re, ssm_lambda_im, ssm_log_dt, ssm_b_re, ssm_b_im, ssm_c_re, ssm_c_im, ssm_d, ssm_w_glu, ssm_b_glu, ssm_out_norm, w_out, ffn2_norm, ffn2_w_gate, ffn2_w_up, ffn2_w_down, ple_norm, ple_w_gate, ple_w_proj, final_norm, loss_target, m_ffn1_norm, m_ffn1_w_gate, m_ffn1_w_up, m_ffn1_w_down, m_mix_norm, m_w_in, m_attn_out_norm, m_ssm_lambda_re, m_ssm_lambda_im, m_ssm_log_dt, m_ssm_b_re, m_ssm_b_im, m_ssm_c_re, m_ssm_c_im, m_ssm_d, m_ssm_w_glu, m_ssm_b_glu, m_ssm_out_norm, m_w_out, m_ffn2_norm, m_ffn2_w_gate, m_ffn2_w_up, m_ffn2_w_down, m_ple_norm, m_ple_w_gate, m_ple_w_proj, m_final_norm, v_ffn1_norm, v_ffn1_w_gate, v_ffn1_w_up, v_ffn1_w_down, v_mix_norm, v_w_in, v_attn_out_norm, v_ssm_lambda_re, v_ssm_lambda_im, v_ssm_log_dt, v_ssm_b_re, v_ssm_b_im, v_ssm_c_re, v_ssm_c_im, v_ssm_d, v_ssm_w_glu, v_ssm_b_glu, v_ssm_out_norm, v_w_out, v_ffn2_norm, v_ffn2_w_gate, v_ffn2_w_up, v_ffn2_w_down, v_ple_norm, v_ple_w_gate, v_ple_w_proj, v_final_norm):
    args = locals()
    w = {k: args[k] for k in WEIGHTS}
    m = {k: args["m_" + k] for k in WEIGHTS}
    v = {k: args["v_" + k] for k in WEIGHTS}
    w2 = {k: w[k].reshape(w[k].shape[-2:]) for k, _ in BIG}

    axes = [ax for _, ax in BIG]
    padded = {k: -(-w2[k].shape[ax] // LANES) * LANES for k, ax in BIG}
    shards = [_pad_to(w2[k].astype(BF16), ax, padded[k]) for k, ax in BIG]
    ex = _Exchange(dict(zip([k for k, _ in BIG], shards)), dict(BIG))
    loss_local, dx, summed, gsmall = _local_step(x[0], p[0, 0], loss_target[0], w, ex)
    loss = lax.psum(loss_local, MESH_AXES)
    n_small = sum(w[k].size for k in SMALL)
    rows = -(-n_small // (2 * SUBLANES * LANES)) * 2 * SUBLANES
    gs_sum = _all_reduce_small(_pack([gsmall[k] for k in SMALL], rows))

    grads, delta, new_m, new_v = {}, {}, {}, {}
    for k, ax in BIG:
        gfull = summed[k]
        g2 = lax.slice_in_dim(gfull, 0, w2[k].shape[ax], axis=ax)
        d2, nm2, nv2 = _adamw("adamw_" + k, w2[k], g2, m[k].reshape(w2[k].shape), v[k].reshape(w2[k].shape))
        grads[k], delta[k], new_m[k], new_v[k] = (t.reshape(w[k].shape) for t in (g2, d2, nm2, nv2))
    small_like = [w[k] for k in SMALL]
    ds, nms, nvs = _adamw("adamw_small", _pack(small_like, rows), gs_sum, _pack([m[k] for k in SMALL], rows),
                          _pack([v[k] for k in SMALL], rows))
    for k, g_, d_, nm_, nv_ in zip(SMALL, _unpack(gs_sum, small_like), _unpack(ds, small_like),
                                   _unpack(nms, small_like), _unpack(nvs, small_like)):
        grads[k], delta[k], new_m[k], new_v[k] = g_, d_, nm_, nv_

    return (loss, dx[None], *[grads[k] for k in WEIGHTS], *[delta[k] for k in WEIGHTS],
            *[new_m[k] for k in WEIGHTS], *[new_v[k] for k in WEIGHTS])
```

```python
import functools
import math

import jax
import jax.numpy as jnp
from jax import lax
from jax.experimental import pallas as pl
from jax.experimental.pallas import tpu as pltpu

F32 = jnp.float32
BF16 = jnp.bfloat16
MESH = pl.DeviceIdType.MESH
MESH_AXES = ("x", "y", "c")
N_CHIPS = 4
N_DEV = 8

V7X_VMEM_LIMIT_BYTES = 56 << 20
LANES = 128
SUBLANES = 8

HEAD_DIM = 64
SWA_BLOCK = 128
DILATIONS = (1, 4, 16)
SSM_BLOCK_GROUPS = 8
NORM_EPS = 1e-6
MASK_VALUE = -1e30

ADAM_LR = 0.001
ADAM_B1 = 0.9
ADAM_B2 = 0.999
ADAM_EPS = 1e-08
ADAM_WD = 0.01
ADAM_STEP = 10

GELU_C = math.sqrt(2.0 / math.pi)
GELU_K = 0.044715


def _pcall(body, **kw):
    return pl.pallas_call(body, **kw)


def _params(*sem):
    return pltpu.CompilerParams(dimension_semantics=sem, vmem_limit_bytes=V7X_VMEM_LIMIT_BYTES)


def _tile(n, target, align):
    best = None
    for t in range(align, min(n, target) + 1, align):
        if n % t == 0:
            best = t
    return n if best is None else best


def _sigmoid(x):
    return 0.5 * jnp.tanh(0.5 * x) + 0.5


class _Host:
    def __init__(self, ins, out_shapes, n_sem, start, wait):
        self.ins, self.out_shapes, self.n_sem, self.start, self.wait = ins, out_shapes, n_sem, start, wait


def _mm(name, lhs, rhs, outs, pairs=((0, 0, 0),), epilogue=None, extras=(), ta=False, tb=False,
        tm=1024, tn=512, tk=2048, host=None):
    nl, nr, ne, no = len(lhs), len(rhs), len(extras), len(outs)
    nhi, nho = (len(host.ins), len(host.out_shapes)) if host else (0, 0)
    n_acc = 1 + max(p[2] for p in pairs)
    (K, M) = lhs[0].shape if ta else lhs[0].shape[::-1]
    (N, K2) = rhs[0].shape if tb else rhs[0].shape[::-1]
    assert K == K2, (name, lhs[0].shape, rhs[0].shape)
    tm, tn, tk = _tile(M, tm, LANES), _tile(N, tn, LANES), _tile(K, tk, LANES)
    ni, nj, nk = M // tm, N // tn, K // tk
    n_scr = n_acc if nk > 1 else 0
    if epilogue is None:
        epilogue = lambda accs, ex: accs
    dn = (((0 if ta else 1,), (1 if tb else 0,)), ((), ()))

    def body(*refs):
        refs = list(refs)
        take = lambda n: [refs.pop(0) for _ in range(n)]
        l, r, e, hin, o, hout, acc = take(nl), take(nr), take(ne), take(nhi), take(no), take(nho), take(n_scr)
        i, j, k = pl.program_id(0), pl.program_id(1), pl.program_id(2)
        if host:
            @pl.when((i == 0) & (j == 0) & (k == 0))
            def _():
                host.start(hin, hout, *refs)

        parts = [None] * n_acc
        for li, ri, ai in pairs:
            d = lax.dot_general(l[li][...].astype(BF16), r[ri][...].astype(BF16), dn,
                                preferred_element_type=F32)
            parts[ai] = d if parts[ai] is None else parts[ai] + d

        def finish(accs):
            res = epilogue(accs, [x[...] for x in e])
            for ref, val in zip(o, res):
                ref[...] = val.astype(ref.dtype)

        if nk == 1:
            finish(parts)
        else:
            @pl.when(k == 0)
            def _():
                for ai in range(n_acc):
                    acc[ai][...] = parts[ai]

            @pl.when(k > 0)
            def _():
                for ai in range(n_acc):
                    acc[ai][...] += parts[ai]

            @pl.when(k == nk - 1)
            def _():
                finish([a[...] for a in acc])

        if host:
            @pl.when((i == ni - 1) & (j == nj - 1) & (k == nk - 1))
            def _():
                host.wait(hin, hout, *refs)

    lspec = pl.BlockSpec((tk, tm), lambda i, j, k: (k, i)) if ta else pl.BlockSpec((tm, tk), lambda i, j, k: (i, k))
    rspec = pl.BlockSpec((tn, tk), lambda i, j, k: (j, k)) if tb else pl.BlockSpec((tk, tn), lambda i, j, k: (k, j))
    especs = []
    for arr, kind in extras:
        if kind == "mn":
            especs.append(pl.BlockSpec((tm, tn), lambda i, j, k: (i, j)))
        elif kind == "n":
            especs.append(pl.BlockSpec((1, tn), lambda i, j, k: (0, j)))
        else:
            especs.append(pl.BlockSpec((tm, 1), lambda i, j, k: (i, 0)))
    any_spec = pl.BlockSpec(memory_space=pl.ANY)
    sems = [pltpu.SemaphoreType.DMA((host.n_sem,)), pltpu.SemaphoreType.DMA((host.n_sem,))] if host else []
    res = _pcall(
        body, name=name,
        grid=(ni, nj, nk),
        in_specs=[lspec] * nl + [rspec] * nr + especs + [any_spec] * nhi,
        out_specs=[pl.BlockSpec((tm, tn), lambda i, j, k: (i, j))] * no + [any_spec] * nho,
        out_shape=[jax.ShapeDtypeStruct((M, N), dt) for dt in outs] + (list(host.out_shapes) if host else []),
        scratch_shapes=[pltpu.VMEM((tm, tn), F32)] * n_scr + sems,
        compiler_params=_params(*(("arbitrary",) * 3 if host else ("parallel", "parallel", "arbitrary"))),
    )(*lhs, *rhs, *[a for a, _ in extras], *(host.ins if host else []))
    return res


ROWWISE_BLOCK_BYTES = 20 << 20


def _rowwise(name, fn, ins, params, outs, accs=(), ts=None):
    S = ins[0].shape[0]
    if ts is None:
        row_bytes = sum(a.shape[1] * a.dtype.itemsize for a in ins) + sum(w * jnp.dtype(dt).itemsize for w, dt in outs)
        ts = next((t for t in (1024, 512) if t * row_bytes <= ROWWISE_BLOCK_BYTES), 256)
    ts = _tile(S, ts, 16)
    ni, npar, no, na = len(ins), len(params), len(outs), len(accs)

    def body(*refs):
        i_refs, p_refs = refs[:ni], refs[ni:ni + npar]
        o_refs = refs[ni + npar:ni + npar + no]
        a_refs = refs[ni + npar + no:]
        res_o, res_a = fn([r[...] for r in i_refs], [r[...] for r in p_refs])
        for ref, val in zip(o_refs, res_o):
            ref[...] = val.astype(ref.dtype)
        if na:
            @pl.when(pl.program_id(0) == 0)
            def _():
                for ref in a_refs:
                    ref[...] = jnp.zeros(ref.shape, F32)

            for ref, val in zip(a_refs, res_a):
                ref[...] += val

    res = _pcall(
        body, name=name,
        grid=(S // ts,),
        in_specs=[pl.BlockSpec((ts, a.shape[1]), lambda i: (i, 0)) for a in ins]
        + [pl.BlockSpec(p.shape, lambda i: (0, 0)) for p in params],
        out_specs=[pl.BlockSpec((ts, w), lambda i: (i, 0)) for w, _ in outs]
        + [pl.BlockSpec((1, w), lambda i: (0, 0)) for w in accs],
        out_shape=[jax.ShapeDtypeStruct((S, w), dt) for w, dt in outs]
        + [jax.ShapeDtypeStruct((1, w), F32) for w in accs],
        compiler_params=_params("arbitrary"),
    )(*ins, *params)
    return res


def _xhat(x):
    r = lax.rsqrt(jnp.mean(x * x, axis=-1, keepdims=True) + NORM_EPS)
    return x * r, r


def _rms_fwd(name, x, g):
    def fn(ins, ps):
        xh, _ = _xhat(ins[0])
        return [xh * ps[0]], []

    return _rowwise(name, fn, [x], [g], [(x.shape[1], BF16)])[0]


def _rms_bwd(name, dn, x, g, dres=None, copy_scale=None):
    w = x.shape[1]

    def fn(ins, ps):
        dn_, x_ = ins[0], ins[1]
        xh, r = _xhat(x_)
        dxh = dn_ * ps[0]
        dx = r * (dxh - xh * jnp.mean(dxh * xh, axis=-1, keepdims=True))
        if dres is not None:
            dx = dx + ins[2]
        o = [dx] + ([dx * copy_scale] if copy_scale is not None else [])
        return o, [jnp.sum(dn_ * xh, axis=0, keepdims=True)]

    ins = [dn, x] + ([dres] if dres is not None else [])
    outs = [(w, F32)] + ([(w, BF16)] if copy_scale is not None else [])
    res = _rowwise(name, fn, ins, [g], outs, accs=[w])
    return res[:-1], res[-1]


def _swiglu_epilogue(accs, ex):
    g, u = accs
    sg = _sigmoid(g)
    s = g * sg
    return [u * (sg + s * (1.0 - sg)), s, s * u]


def _dswiglu_epilogue(accs, ex):
    da = accs[0]
    return [da * ex[0].astype(F32), da * ex[1].astype(F32)]


def _carried(ex, kind, kernel):
    if kernel not in (ex.AG_PLAN if kind == "ag" else ex.RS_PLAN):
        return None, lambda outs: None
    if kind == "ag":
        return ex.ag_host(kernel), lambda outs: ex.ag_done(kernel, outs)
    return ex.rs_host(kernel), lambda outs: ex.rs_done(kernel, outs)


def _ffn_fwd(tag, h, gnorm, ex):
    n = _rms_fwd(tag + "_norm", h, gnorm)
    host, done = _carried(ex, "ag", tag + "_up")
    g, u, a, *outs = _mm(tag + "_up", [n], [ex.weight(tag + "_w_gate"), ex.weight(tag + "_w_up")], [BF16, BF16, BF16],
                         pairs=((0, 0, 0), (0, 1, 1)), epilogue=_swiglu_epilogue, tm=1024, tn=512, host=host)
    done(outs)
    host, done = _carried(ex, "ag", tag + "_down")
    hout, *outs = _mm(tag + "_down", [a], [ex.weight(tag + "_w_down")], [F32], extras=[(h, "mn")],
                      epilogue=lambda accs, ex_: [ex_[0] + 0.5 * accs[0]], tm=512, tn=1024, tk=8192, host=host)
    done(outs)
    return hout, (n, g, u, a)


def _ffn_bwd(tag, dh, dhb_half, h, gnorm, ex, saved, copy_scale):
    n, g, u, a = saved
    wg, wu, wd = (ex.weight(tag + k) for k in ("_w_gate", "_w_up", "_w_down"))
    host, done = _carried(ex, "rs", tag + "_dact")
    dg, du, *outs = _mm(tag + "_dact", [dhb_half], [wd], [BF16, BF16], tb=True, extras=[(g, "mn"), (u, "mn")],
                        epilogue=_dswiglu_epilogue, tm=2048, tn=512, host=host)
    done(outs)
    host, done = _carried(ex, "rs", tag + "_dwd")
    dwd, *outs = _mm(tag + "_dwd", [a], [dhb_half], [BF16], ta=True, tm=512, tn=2048, tk=2048, host=host)
    done(outs)
    ex.grad(tag + "_w_down", dwd)
    host, done = _carried(ex, "rs", tag + "_dwgu")
    dwg, dwu, *outs = _mm(tag + "_dwgu", [n], [dg, du], [BF16, BF16], pairs=((0, 0, 0), (0, 1, 1)), ta=True,
                          tm=1024, tn=512, tk=2048, host=host)
    done(outs)
    ex.grad(tag + "_w_gate", dwg)
    ex.grad(tag + "_w_up", dwu)
    host, done = _carried(ex, "rs", tag + "_dn")
    dn, *outs = _mm(tag + "_dn", [dg, du], [wg, wu], [F32], pairs=((0, 0, 0), (1, 1, 0)), tb=True,
                    tm=1024, tn=1024, tk=1408, host=host)
    done(outs)
    return _rms_bwd(tag + "_dnorm", dn, h, gnorm, dres=dh, copy_scale=copy_scale)


ATTN_HEAD_PAIRS = 8


def _to_attn_order(a):
    S, w = a.shape
    return a.reshape(S // 16, 16, w).transpose(1, 0, 2).reshape(S, w)


def _from_attn_order(a):
    S, w = a.shape
    return a.reshape(16, S // 16, w).transpose(1, 0, 2).reshape(S, w)


def _attn_geom(S, d):
    s16 = S // 16
    if d == 16:
        return (16, s16), (1, SWA_BLOCK), (lambda r, b: (r, b)), 16, s16 // SWA_BLOCK
    if d == 4:
        return (4, 4, s16), (4, 1, SWA_BLOCK // 4), (lambda r, b: (0, r, b)), 4, s16 // (SWA_BLOCK // 4)
    return (16, s16), (16, SWA_BLOCK // 16), (lambda r, b: (0, b)), 1, s16 // (SWA_BLOCK // 16)


def _attn_pos(rho, d):
    if d == 16:
        return rho
    if d == 4:
        return 4 * (rho & 31) + (rho >> 5)
    return 16 * (rho & 7) + (rho >> 3)


def _attn_spec(S, d, lb, col, shift=0):
    _, blk, idx, _, nb = _attn_geom(S, d)
    return pl.BlockSpec(blk + (lb,), lambda r, cb, b: idx(r, jnp.clip(b + shift, 0, nb - 1)) + (col(cb),))


def _attn_view(a, d):
    return a.reshape(_attn_geom(a.shape[0], d)[0] + (a.shape[1],))


def _attn_valid(d):
    qp = _attn_pos(lax.broadcasted_iota(jnp.int32, (SWA_BLOCK, 2 * SWA_BLOCK), 0), d)
    kk = lax.broadcasted_iota(jnp.int32, (SWA_BLOCK, 2 * SWA_BLOCK), 1)
    kp = _attn_pos(kk & (SWA_BLOCK - 1), d)
    is_prev = kk < SWA_BLOCK
    return qp, kp, is_prev


def _head_masks(rows=SWA_BLOCK):
    lane = lax.broadcasted_iota(jnp.int32, (rows, LANES), 1)
    return [lane < HEAD_DIM, lane >= HEAD_DIM]


def _attn_ld(ref, sl):
    t = ref[(slice(None),) * (len(ref.shape) - 1) + (sl,)]
    return t.reshape(-1, t.shape[-1])


def _attn_st(ref, sl, val):
    ref[(slice(None),) * (len(ref.shape) - 1) + (sl,)] = val.reshape(ref.shape[:-1] + (val.shape[-1],))


def _per_head(t, first):
    sw = pltpu.roll(t, HEAD_DIM, 1)
    lo = lax.broadcasted_iota(jnp.int32, t.shape, 1) < HEAD_DIM
    return jnp.where(lo, t, sw) if first else jnp.where(lo, sw, t)


def _dot_nt(a, b):
    return lax.dot_general(a, b, (((1,), (1,)), ((), ())), preferred_element_type=F32)


def _dot_tn(a, b):
    return lax.dot_general(a, b, (((0,), (0,)), ((), ())), preferred_element_type=F32)


def _dot(a, b):
    return jnp.dot(a, b, preferred_element_type=F32)


def _keep(mask, t):
    return jnp.where(mask, t.astype(F32), 0.0).astype(BF16)


def _attn_cols(A):
    lb = min(A, LANES * ATTN_HEAD_PAIRS)
    ncol = A // lb
    return lb, ncol, [lambda cb, part=part: part * ncol + cb for part in range(3)], (lambda cb: cb)


def _attn_fwd_stage(name, qkv, d, prev, final, host=None):
    S, A3 = qkv.shape
    A = A3 // 3
    lb, ncol, (cq, ck, cv), ca = _attn_cols(A)
    view, _, _, nres, nb = _attn_geom(S, d)
    scale = HEAD_DIM ** -0.5
    has_prev = prev is not None
    n_out = 2 if final else 3
    nhi, nho = (len(host.ins), len(host.out_shapes)) if host else (0, 0)

    def body(*refs):
        q_ref, kp_ref, kc_ref, vp_ref, vc_ref = refs[:5]
        p_refs = refs[5:8] if has_prev else ()
        n_in = 5 + len(p_refs)
        hin, o_refs = refs[n_in:n_in + nhi], refs[n_in + nhi:n_in + nhi + n_out]
        hout, sems = refs[n_in + nhi + n_out:n_in + nhi + n_out + nho], refs[n_in + nhi + n_out + nho:]
        b = pl.program_id(2)
        if host:
            @pl.when((pl.program_id(0) == 0) & (pl.program_id(1) == 0) & (b == 0))
            def _():
                host.start(hin, hout, *sems)
        qp, kp_, is_prev = _attn_valid(d)
        valid = (is_prev & (kp_ >= qp) & (b > 0)) | (jnp.logical_not(is_prev) & (kp_ <= qp))
        hm, hm2 = _head_masks(), _head_masks(2 * SWA_BLOCK)
        for hp in range(lb // LANES):
            sl = slice(hp * LANES, (hp + 1) * LANES)
            q = _attn_ld(q_ref, sl)
            k2 = jnp.concatenate([_attn_ld(kp_ref, sl), _attn_ld(kc_ref, sl)], axis=0).astype(BF16)
            v2 = jnp.concatenate([_attn_ld(vp_ref, sl), _attn_ld(vc_ref, sl)], axis=0)
            o = jnp.zeros((SWA_BLOCK, LANES), F32)
            m = jnp.zeros((SWA_BLOCK, LANES), F32)
            l = jnp.zeros((SWA_BLOCK, LANES), F32)
            for hh in range(2):
                s = jnp.where(valid, _dot_nt(_keep(hm[hh], q), k2) * scale, MASK_VALUE)
                mh = jnp.max(s, axis=-1, keepdims=True)
                p = jnp.exp(s - mh)
                lh = jnp.sum(p, axis=-1, keepdims=True)
                o = o + _dot(p.astype(BF16), _keep(hm2[hh], v2))
                m = jnp.where(hm[hh], mh, m)
                l = jnp.where(hm[hh], lh, l)
            if has_prev:
                po, pm, pl_ = (_attn_ld(r, sl) for r in p_refs)
                mn = jnp.maximum(m, pm)
                w_new, w_old = jnp.exp(m - mn), jnp.exp(pm - mn)
                o = o * w_new + po * w_old
                l = l * w_new + pl_ * w_old
                m = mn
            if final:
                _attn_st(o_refs[0], sl, o / l)
                _attn_st(o_refs[1], sl, m + jnp.log(l))
            else:
                _attn_st(o_refs[0], sl, o)
                _attn_st(o_refs[1], sl, m)
                _attn_st(o_refs[2], sl, l)

        if host:
            @pl.when((pl.program_id(0) == nres - 1) & (pl.program_id(1) == ncol - 1) & (b == nb - 1))
            def _():
                host.wait(hin, hout, *sems)

    qk = _attn_view(qkv, d)
    prev_v = [_attn_view(t, d) for t in prev] if has_prev else []
    sp = functools.partial(_attn_spec, S, d, lb)
    any_spec = pl.BlockSpec(memory_space=pl.ANY)
    res = _pcall(
        body, name=name,
        grid=(nres, ncol, nb),
        in_specs=[sp(cq), sp(ck, -1), sp(ck), sp(cv, -1), sp(cv)] + [sp(ca)] * len(prev_v) + [any_spec] * nhi,
        out_specs=[sp(ca)] * n_out + [any_spec] * nho,
        out_shape=[jax.ShapeDtypeStruct(view + (A,), F32)] * n_out + (list(host.out_shapes) if host else []),
        scratch_shapes=[pltpu.SemaphoreType.DMA((host.n_sem,)), pltpu.SemaphoreType.DMA((host.n_sem,))] if host else [],
        compiler_params=_params(*(("arbitrary",) * 3 if host else ("parallel", "parallel", "arbitrary"))),
    )(qk, qk, qk, qk, qk, *prev_v, *(host.ins if host else []))
    return [t.reshape(S, A) for t in res[:n_out]], res[n_out:]


def _attn_fwd(qkv, ex):
    st = None
    for i, d in enumerate(DILATIONS):
        name = "attn_fwd_d%d" % d
        host, done = _carried(ex, "ag", name)
        st, outs = _attn_fwd_stage(name, qkv, d, st, final=(i == len(DILATIONS) - 1), host=host)
        done(outs)
    return st


def _attn_delta(dya, ya):
    S, A = ya.shape
    ri = lax.broadcasted_iota(jnp.int32, (A, A), 0) // HEAD_DIM
    ci = lax.broadcasted_iota(jnp.int32, (A, A), 1) // HEAD_DIM
    ones_bd = (ri == ci).astype(BF16)

    def fn(ins, ps):
        prod = ins[0] * ins[1]
        hi = prod.astype(BF16)
        lo = (prod - hi.astype(F32)).astype(BF16)
        return [_dot(hi, ps[0]) + _dot(lo, ps[0])], []

    return _rowwise("attn_delta", fn, [dya, ya], [ones_bd], [(A, F32)])[0]


def _attn_bwd_stage(name, qkv, do, lse, delta, d, prev):
    S, A3 = qkv.shape
    A = A3 // 3
    lb, ncol, (cq, ck, cv), ca = _attn_cols(A)
    view, _, _, nres, nb = _attn_geom(S, d)
    scale = HEAD_DIM ** -0.5
    has_prev = prev is not None
    lane_slices = [slice(hp * LANES, (hp + 1) * LANES) for hp in range(lb // LANES)]

    def body(*refs):
        q_ref, kp_ref, kc_ref, vp_ref, vc_ref, do_ref, lse_ref, dl_ref = refs[:8]
        p_refs = refs[8:11] if has_prev else ()
        dq_ref, dk_ref, dv_ref, dk_c, dv_c = refs[8 + len(p_refs):]
        b = pl.program_id(2)

        def put_keys(sl, dk, dv):
            if has_prev:
                dk, dv = dk + _attn_ld(p_refs[1], sl), dv + _attn_ld(p_refs[2], sl)
            _attn_st(dk_ref, sl, dk)
            _attn_st(dv_ref, sl, dv)

        @pl.when(b == 0)
        def _():
            dk_c[...] = jnp.zeros(dk_c.shape, F32)
            dv_c[...] = jnp.zeros(dv_c.shape, F32)

        @pl.when(b < nb)
        def _():
            qp, kp_, is_prev = _attn_valid(d)
            valid = (is_prev & (kp_ >= qp) & (b > 0)) | (jnp.logical_not(is_prev) & (kp_ <= qp))
            hm, hm2 = _head_masks(), _head_masks(2 * SWA_BLOCK)
            for sl in lane_slices:
                q, do_, lse_, dl_ = (_attn_ld(r, sl) for r in (q_ref, do_ref, lse_ref, dl_ref))
                k2 = jnp.concatenate([_attn_ld(kp_ref, sl), _attn_ld(kc_ref, sl)], axis=0)
                v2 = jnp.concatenate([_attn_ld(vp_ref, sl), _attn_ld(vc_ref, sl)], axis=0).astype(BF16)
                k2b = k2.astype(BF16)
                dq = jnp.zeros((SWA_BLOCK, LANES), F32)
                dk2 = jnp.zeros((2 * SWA_BLOCK, LANES), F32)
                dv2 = jnp.zeros((2 * SWA_BLOCK, LANES), F32)
                for hh in range(2):
                    qh, doh = _keep(hm[hh], q), _keep(hm[hh], do_)
                    lh, dh = _per_head(lse_, hh == 0), _per_head(dl_, hh == 0)
                    lh2, dh2 = jnp.concatenate([lh, lh], axis=1), jnp.concatenate([dh, dh], axis=1)
                    p = jnp.where(valid, jnp.exp(_dot_nt(qh, k2b) * scale - lh2), 0.0)
                    ds = (p * (_dot_nt(doh, v2) - dh2)).astype(BF16)
                    dq = dq + _dot(ds, _keep(hm2[hh], k2))
                    dk2 = dk2 + _dot_tn(ds, qh)
                    dv2 = dv2 + _dot_tn(p.astype(BF16), doh)
                dq, dk2 = dq * scale, dk2 * scale
                if has_prev:
                    dq = dq + _attn_ld(p_refs[0], sl)
                _attn_st(dq_ref, sl, dq)
                put_keys(sl, dk_c[:, sl] + dk2[:SWA_BLOCK], dv_c[:, sl] + dv2[:SWA_BLOCK])
                dk_c[:, sl] = dk2[SWA_BLOCK:]
                dv_c[:, sl] = dv2[SWA_BLOCK:]

        @pl.when(b == nb)
        def _():
            for sl in lane_slices:
                put_keys(sl, dk_c[:, sl], dv_c[:, sl])

    qk = _attn_view(qkv, d)
    acts = [_attn_view(t, d) for t in (do, lse, delta)] + ([_attn_view(t, d) for t in prev] if has_prev else [])
    sp = functools.partial(_attn_spec, S, d, lb)
    res = _pcall(
        body, name=name,
        grid=(nres, ncol, nb + 1),
        in_specs=[sp(cq), sp(ck, -1), sp(ck), sp(cv, -1), sp(cv), sp(ca), sp(ca), sp(ca)]
        + ([sp(ca), sp(ca, -1), sp(ca, -1)] if has_prev else []),
        out_specs=[sp(ca), sp(ca, -1), sp(ca, -1)],
        out_shape=[jax.ShapeDtypeStruct(view + (A,), F32)] * 3,
        scratch_shapes=[pltpu.VMEM((SWA_BLOCK, lb), F32)] * 2,
        compiler_params=_params("parallel", "parallel", "arbitrary"),
    )(qk, qk, qk, qk, qk, *acts)
    return [t.reshape(S, A) for t in res]


def _attn_bwd(qkv, dya, ya, lse):
    delta = _attn_delta(dya, ya)
    sums = None
    for d in DILATIONS:
        sums = _attn_bwd_stage("attn_bwd_d%d" % d, qkv, dya, lse, delta, d, sums)
    return sums


def _ssm_perm(a, T):
    S, w = a.shape
    return a.reshape(S // T, SUBLANES, T // SUBLANES, w).transpose(0, 2, 1, 3).reshape(S, w)


def _ssm_unperm(a, T):
    S, w = a.shape
    return a.reshape(S // T, T // SUBLANES, SUBLANES, w).transpose(0, 2, 1, 3).reshape(S, w)


def _ssm_powers(lam_ref, pw_ref, T, ns):
    tc = T // SUBLANES
    n = (lax.broadcasted_iota(jnp.int32, (tc, 1), 0) + 1).astype(F32)
    mag = jnp.exp(n * lam_ref[0, 0:1, :])
    ang = n * lam_ref[0, 1:2, :]
    rows8 = lambda t: jnp.broadcast_to(t[:, None, :], (tc, SUBLANES, ns)).reshape(T, ns)
    pw_ref[:, 0:ns] = rows8(mag * jnp.cos(ang))
    pw_ref[:, ns:2 * ns] = rows8(mag * jnp.sin(ang))


def _ssm_scan(xs, off, pw_ref, carry_ref, T, ns, reverse):
    Tc = T // SUBLANES
    sgn = -1.0 if reverse else 1.0
    ar, ai = pw_ref[0:SUBLANES, 0:ns], sgn * pw_ref[0:SUBLANES, ns:2 * ns]

    def rows(i):
        return pl.ds(pl.multiple_of(off + i * SUBLANES, SUBLANES), SUBLANES)

    def step(k, h):
        hr, hi = h
        r = rows(Tc - 1 - k if reverse else k)
        nr = ar * hr - ai * hi + xs[r, 0:ns]
        ni = ar * hi + ai * hr + xs[r, ns:2 * ns]
        xs[r, 0:ns] = nr
        xs[r, ns:2 * ns] = ni
        return nr, ni

    z = jnp.zeros((SUBLANES, ns), F32)
    er, ei = lax.fori_loop(0, Tc, step, (z, z), unroll=4)
    atr, ati = pw_ref[T - SUBLANES:T, 0:ns], sgn * pw_ref[T - SUBLANES:T, ns:2 * ns]
    rowid = lax.broadcasted_iota(jnp.int32, (SUBLANES, ns), 0)
    cr, ci = carry_ref[:, 0:ns], carry_ref[:, ns:2 * ns]
    ctr, cti = z, z
    for jj in range(SUBLANES):
        j = SUBLANES - 1 - jj if reverse else jj
        sel = rowid == j
        ctr, cti = jnp.where(sel, cr, ctr), jnp.where(sel, ci, cti)
        ejr = jnp.broadcast_to(jnp.sum(jnp.where(sel, er, 0.0), axis=0, keepdims=True), (SUBLANES, ns))
        eji = jnp.broadcast_to(jnp.sum(jnp.where(sel, ei, 0.0), axis=0, keepdims=True), (SUBLANES, ns))
        cr, ci = ejr + atr * cr - ati * ci, eji + atr * ci + ati * cr
    carry_ref[:, 0:ns] = cr
    carry_ref[:, ns:2 * ns] = ci

    def fix(i, _):
        r = rows(i)
        pr_rows = pl.ds(pl.multiple_of((Tc - 1 - i if reverse else i) * SUBLANES, SUBLANES), SUBLANES)
        pr, pi = pw_ref[pr_rows, 0:ns], sgn * pw_ref[pr_rows, ns:2 * ns]
        xs[r, 0:ns] += pr * ctr - pi * cti
        xs[r, ns:2 * ns] += pr * cti + pi * ctr
        return 0

    lax.fori_loop(0, Tc, fix, 0, unroll=4)
    return ctr, cti


def _ssm_fwd(ufp, bb, cc, lam_dt, drow, T):
    S, W = ufp.shape
    GB, cw, ns2 = bb.shape
    ns = ns2 // 2
    NCH = S // T

    def body(uf_ref, bb_ref, cc_ref, lam_ref, d_ref, y_ref, hs_ref, xs, pw, carry):
        @pl.when(pl.program_id(1) == 0)
        def _():
            _ssm_powers(lam_ref, pw, T, ns)
            carry[...] = jnp.zeros(carry.shape, F32)

        uf = uf_ref[...]
        xs[...] = _dot(uf.astype(BF16), bb_ref[0])
        hs_ref[0, 0] = carry[...]
        _ssm_scan(xs, 0, pw, carry, T, ns, reverse=False)
        y_ref[...] = _dot(xs[...].astype(BF16), cc_ref[0]) + d_ref[...] * uf

    return _pcall(
        body, name="ssm_fwd",
        grid=(GB, NCH),
        in_specs=[pl.BlockSpec((T, cw), lambda g, c: (c, g)),
                  pl.BlockSpec((1, cw, ns2), lambda g, c: (g, 0, 0)),
                  pl.BlockSpec((1, ns2, cw), lambda g, c: (g, 0, 0)),
                  pl.BlockSpec((1, 2, ns), lambda g, c: (g, 0, 0)),
                  pl.BlockSpec((1, cw), lambda g, c: (0, g))],
        out_specs=[pl.BlockSpec((T, cw), lambda g, c: (c, g)),
                   pl.BlockSpec((1, 1, SUBLANES, ns2), lambda g, c: (g, c, 0, 0))],
        out_shape=[jax.ShapeDtypeStruct((S, W), F32),
                   jax.ShapeDtypeStruct((GB, NCH, SUBLANES, ns2), F32)],
        scratch_shapes=[pltpu.VMEM((T, ns2), F32), pltpu.VMEM((T, ns2), F32), pltpu.VMEM((SUBLANES, ns2), F32)],
        compiler_params=_params("arbitrary", "arbitrary"),
    )(ufp, bb, cc, lam_dt, drow)


def _ssm_bwd(ufp, dyp, bb, bbt, cc, cct, lam_dt, drow, hstart, T):
    S, W = ufp.shape
    GB, cw, ns2 = bb.shape
    ns = ns2 // 2
    NCH = S // T

    def body(uf_ref, dy_ref, bb_ref, bbt_ref, cc_ref, cct_ref, lam_ref, d_ref, hs_ref,
             duf_ref, dbb_ref, dcc_ref, da_ref, dd_ref, hb, ls, pw, carry_f, carry_b):
        @pl.when(pl.program_id(1) == 0)
        def _():
            _ssm_powers(lam_ref, pw, T, ns)
            carry_b[...] = jnp.zeros(carry_b.shape, F32)
            dbb_ref[...] = jnp.zeros(dbb_ref.shape, F32)
            dcc_ref[...] = jnp.zeros(dcc_ref.shape, F32)
            da_ref[...] = jnp.zeros(da_ref.shape, F32)
            dd_ref[...] = jnp.zeros(dd_ref.shape, F32)

        uf, dy = uf_ref[...], dy_ref[...]
        ufb, dyb = uf.astype(BF16), dy.astype(BF16)
        hb[SUBLANES:T + SUBLANES, :] = _dot(ufb, bb_ref[0])
        carry_f[...] = hs_ref[0, 0]
        ctr, cti = _ssm_scan(hb, SUBLANES, pw, carry_f, T, ns, reverse=False)
        hb[0:SUBLANES, 0:ns] = ctr
        hb[0:SUBLANES, ns:ns2] = cti
        ls[...] = _dot(dyb, cct_ref[0])
        _ssm_scan(ls, 0, pw, carry_b, T, ns, reverse=True)
        lv = ls[...]
        lb = lv.astype(BF16)
        dbb_ref[0] += _dot_tn(ufb, lb)
        dcc_ref[0] += _dot_tn(hb[SUBLANES:T + SUBLANES, :].astype(BF16), dyb)
        lr, li = lv[:, 0:ns], lv[:, ns:ns2]
        hpr, hpi = hb[0:T, 0:ns], hb[0:T, ns:ns2]
        dar = jnp.sum(lr * hpr + li * hpi, axis=0, keepdims=True)
        dai = jnp.sum(li * hpr - lr * hpi, axis=0, keepdims=True)
        da_ref[0, 0:1, 0:ns] += dar
        da_ref[0, 0:1, ns:ns2] += dai
        duf_ref[...] = _dot(lb, bbt_ref[0]) + d_ref[...] * dy
        dd_ref[...] += jnp.sum(dy * uf, axis=0, keepdims=True)

    rc = lambda c: NCH - 1 - c
    return _pcall(
        body, name="ssm_bwd",
        grid=(GB, NCH),
        in_specs=[pl.BlockSpec((T, cw), lambda g, c: (rc(c), g)),
                  pl.BlockSpec((T, cw), lambda g, c: (rc(c), g)),
                  pl.BlockSpec((1, cw, ns2), lambda g, c: (g, 0, 0)),
                  pl.BlockSpec((1, ns2, cw), lambda g, c: (g, 0, 0)),
                  pl.BlockSpec((1, ns2, cw), lambda g, c: (g, 0, 0)),
                  pl.BlockSpec((1, cw, ns2), lambda g, c: (g, 0, 0)),
                  pl.BlockSpec((1, 2, ns), lambda g, c: (g, 0, 0)),
                  pl.BlockSpec((1, cw), lambda g, c: (0, g)),
                  pl.BlockSpec((1, 1, SUBLANES, ns2), lambda g, c: (g, rc(c), 0, 0))],
        out_specs=[pl.BlockSpec((T, cw), lambda g, c: (rc(c), g)),
                   pl.BlockSpec((1, cw, ns2), lambda g, c: (g, 0, 0)),
                   pl.BlockSpec((1, ns2, cw), lambda g, c: (g, 0, 0)),
                   pl.BlockSpec((1, SUBLANES, ns2), lambda g, c: (g, 0, 0)),
                   pl.BlockSpec((1, cw), lambda g, c: (0, g))],
        out_shape=[jax.ShapeDtypeStruct((S, W), F32),
                   jax.ShapeDtypeStruct((GB, cw, ns2), F32),
                   jax.ShapeDtypeStruct((GB, ns2, cw), F32),
                   jax.ShapeDtypeStruct((GB, SUBLANES, ns2), F32),
                   jax.ShapeDtypeStruct((1, W), F32)],
        scratch_shapes=[pltpu.VMEM((T + SUBLANES, ns2), F32), pltpu.VMEM((T, ns2), F32), pltpu.VMEM((T, ns2), F32),
                        pltpu.VMEM((SUBLANES, ns2), F32), pltpu.VMEM((SUBLANES, ns2), F32)],
        compiler_params=_params("arbitrary", "arbitrary"),
    )(ufp, dyp, bb, bbt, cc, cct, lam_dt, drow, hstart)


def _ssm_disc_math(lr, li, logdt, br, bi):
    dt = jnp.exp(logdt)
    mag = jnp.exp(lr * dt)
    ar = mag * jnp.cos(li * dt)
    ai = mag * jnp.sin(li * dt)
    nr, ni = ar - 1.0, ai
    den = lr * lr + li * li
    cr = (nr * lr + ni * li) / den
    ci = (ni * lr - nr * li) / den
    return ar, ai, cr * br - ci * bi, cr * bi + ci * br


def _ssm_disc(lr, li, logdt, br, bi):
    C = br.shape[1]

    def fn(ins, ps):
        _, _, bbr, bbi = _ssm_disc_math(*ins)
        dt = jnp.exp(ins[2])
        return [ins[0] * dt, ins[1] * dt, bbr, bbi], []

    return _rowwise("ssm_disc", fn, [lr, li, logdt, br, bi], [], [(1, F32), (1, F32), (C, F32), (C, F32)], ts=512)


def _ssm_disc_bwd(lr, li, logdt, br, bi, dar, dai, dbbr, dbbi):
    C = br.shape[1]

    def fn(ins, ps):
        _, vjp = jax.vjp(_ssm_disc_math, *ins[:5])
        return list(vjp(tuple(ins[5:]))), []

    return _rowwise("ssm_disc_bwd", fn, [lr, li, logdt, br, bi, dar, dai, dbbr, dbbi], [],
                    [(1, F32), (1, F32), (1, F32), (C, F32), (C, F32)], ts=512)


def _block_diag(t):
    GB, g, a, b = t.shape
    eye = jnp.eye(g, dtype=t.dtype)
    return (t[:, :, :, None, :] * eye[None, :, None, :, None]).reshape(GB, g * a, g * b)


def _block_diag_take(t, g):
    GB, ga, gb_ = t.shape
    a, b = ga // g, gb_ // g
    eye = jnp.eye(g, dtype=t.dtype)
    return (t.reshape(GB, g, a, g, b) * eye[None, :, None, :, None]).sum(axis=3)


def _loss_head(h4, tgt, gf):
    D = h4.shape[1]

    def fn(ins, ps):
        x, t = ins
        xh, r = _xhat(x)
        err = xh * ps[0] - t
        dn = err * (1.0 / D)
        dxh = dn * ps[0]
        dx = r * (dxh - xh * jnp.mean(dxh * xh, axis=-1, keepdims=True))
        return [dx], [jnp.sum(err * err, axis=0, keepdims=True), jnp.sum(dn * xh, axis=0, keepdims=True)]

    return _rowwise("loss_head", fn, [h4, tgt], [gf], [(D, F32)], accs=[D, D])


def _gelu(x):
    return 0.5 * x * (1.0 + jnp.tanh(GELU_C * (x + GELU_K * x * x * x)))


def _gelu_grad(x):
    t = jnp.tanh(GELU_C * (x + GELU_K * x * x * x))
    return 0.5 * (1.0 + t) + 0.5 * x * (1.0 - t * t) * GELU_C * (1.0 + 3.0 * GELU_K * x * x)


def _mesh_pos():
    return lax.axis_index("x"), lax.axis_index("y"), lax.axis_index("c")


def _other_chips(x, y):
    return [(1 - x, y), (x, 1 - y), (1 - x, 1 - y)]


def _remote(src, dst, send, recv, dev):
    return pltpu.make_async_remote_copy(src_ref=src, dst_ref=dst, send_sem=send, recv_sem=recv,
                                        device_id=dev, device_id_type=MESH)


ANY = pl.BlockSpec(memory_space=pl.ANY)


COMM_BLOCK_BYTES = 3 << 19


def _place():
    x, y, c = _mesh_pos()
    return jnp.stack([c] + [2 * cx + cy for cx, cy in _other_chips(x, y)] + [2 * x + y]).astype(jnp.int32)


def _send_chips(name, srcs, specs, tr, nth, cw):
    hr = nth * tr
    n = len(srcs)

    def body(*refs):
        got_ref, send, recv = refs[1 + n:]
        t = pl.program_id(0)
        x, y, c = _mesh_pos()
        cps = []
        for j, chip in enumerate(_other_chips(x, y)):
            dst = got_ref.at[pl.ds(pl.multiple_of(j * hr + t * tr, 16), tr), :]
            cp = _remote(refs[1 + j % n], dst, send.at[j], recv.at[j], (*chip, c))
            cp.start()
            cps.append(cp)
        for cp in cps:
            cp.wait_send()

        @pl.when(t == nth - 1)
        def _():
            for j in range(3):
                r_ = got_ref.at[pl.ds(j * hr, hr), :]
                _remote(r_, r_, send.at[j], recv.at[j], (x, y, c)).wait_recv()

    return _pcall(
        body, name=name,
        grid_spec=pltpu.PrefetchScalarGridSpec(
            num_scalar_prefetch=1, grid=(nth,), in_specs=specs, out_specs=ANY,
            scratch_shapes=[pltpu.SemaphoreType.DMA((3,)), pltpu.SemaphoreType.DMA((3,))]),
        out_shape=jax.ShapeDtypeStruct((3 * hr, cw), srcs[0].dtype),
        compiler_params=_params("arbitrary"),
    )(_place(), *srcs)


def _ag_assemble(name, shard, stage, axis, tr, nth):
    R, cc = shard.shape
    hr = nth * tr
    full = (R, N_CHIPS * cc) if axis == 1 else (N_CHIPS * R, cc)

    def body(pl_ref, s0, s1, s2, h0, h1, out_ref, send, recv, lsem):
        t = pl.program_id(0)
        x, y, c = _mesh_pos()

        def region(s, half):
            if axis == 1:
                return out_ref.at[pl.ds(pl.multiple_of(half * hr + t * tr, 16), tr), pl.ds(pl.multiple_of(s * cc, LANES), cc)]
            return out_ref.at[pl.ds(pl.multiple_of(s * R + half * hr + t * tr, 16), tr), :]

        cps = []
        for j, src in enumerate((s0, s1, s2)):
            dst = region(pl_ref[1 + j], c)
            cps.append(_remote(src, dst, send.at[j], recv, (x, y, 1 - c)))
            cps.append(pltpu.make_async_copy(src, dst, lsem.at[j]))
        for half, src in enumerate((h0, h1)):
            cps.append(pltpu.make_async_copy(src, region(pl_ref[4], half), lsem.at[3 + half]))
        for cp in cps:
            cp.start()
        for k, cp in enumerate(cps):
            if k < 6 and k % 2 == 0:
                cp.wait_send()
            else:
                cp.wait()

        @pl.when(t == nth - 1)
        def _():
            r_ = out_ref.at[pl.ds(0, hr), pl.ds(0, 3 * cc)] if axis == 1 else out_ref.at[pl.ds(0, 3 * hr), :]
            _remote(r_, r_, send.at[0], recv, (x, y, c)).wait_recv()

    blk = lambda f: pl.BlockSpec((tr, cc), f)
    return _pcall(
        body, name=name,
        grid_spec=pltpu.PrefetchScalarGridSpec(
            num_scalar_prefetch=1, grid=(nth,),
            in_specs=[blk(lambda t, p, j=j: (j * nth + t, 0)) for j in range(3)]
            + [blk(lambda t, p, h=h: (h * nth + t, 0)) for h in range(2)],
            out_specs=ANY,
            scratch_shapes=[pltpu.SemaphoreType.DMA((3,)), pltpu.SemaphoreType.DMA, pltpu.SemaphoreType.DMA((5,))]),
        out_shape=jax.ShapeDtypeStruct(full, shard.dtype),
        compiler_params=_params("arbitrary"),
    )(_place(), stage, stage, stage, shard, shard)


def _comm_rows(hr, row_bytes):
    return _tile(hr, max(16, COMM_BLOCK_BYTES // row_bytes // 16 * 16), 16)


def _host_send(items):
    def copies(ins, outs, send, recv):
        x, y, c = _mesh_pos()
        cps = []
        for w, (_, kind, hr, cw) in enumerate(items):
            for j, (cx, cy) in enumerate(_other_chips(x, y)):
                s = 2 * cx + cy
                if kind == "half":
                    src = ins[w].at[pl.ds(pl.multiple_of(c * hr, 16), hr), :]
                elif kind == "cols":
                    src = ins[w].at[:, pl.ds(pl.multiple_of(s * cw, LANES), cw)]
                else:
                    src = ins[w].at[pl.ds(pl.multiple_of(s * hr, 16), hr), :]
                cps.append(_remote(src, outs[w].at[pl.ds(j * hr, hr), :], send.at[3 * w + j], recv.at[3 * w + j], (cx, cy, c)))
        return cps

    def start(ins, outs, send, recv):
        for cp in copies(ins, outs, send, recv):
            cp.start()

    def wait(ins, outs, send, recv):
        for cp in copies(ins, outs, send, recv):
            cp.wait()

    return _Host([a for a, _, _, _ in items], [jax.ShapeDtypeStruct((3 * hr, cw), a.dtype) for a, _, hr, cw in items],
                 3 * len(items), start, wait)


def _ag_send(name, sh):
    R, cc = sh.shape
    tr = _comm_rows(R // 2, cc * 2)
    nth = R // 2 // tr
    return _send_chips(name, [sh], [pl.BlockSpec((tr, cc), lambda t, p: (p[0] * nth + t, 0))], tr, nth, cc)


def _ag_finish(name, sh, stage, axis):
    R, cc = sh.shape
    tr = _comm_rows(R // 2, cc * 2)
    return _ag_assemble(name, sh, stage, axis, tr, R // 2 // tr)


def _pair_sum(name, g, tr, nblk, blk):
    cw = g.shape[1]
    c_arr = lax.axis_index("c").astype(jnp.int32).reshape(1)

    def body(c_ref, keep_ref, send_ref, out_ref, land, send, recv):
        i = pl.program_id(0)
        slot = lax.rem(i, 2)
        x, y, c = _mesh_pos()
        cp = _remote(send_ref, land.at[slot], send.at[slot], recv.at[slot], (x, y, 1 - c))
        cp.start()
        cp.wait_recv()
        out_ref[...] = (keep_ref[...].astype(F32) + land[slot].astype(F32)).astype(BF16)
        cp.wait_send()

    return _pcall(
        body, name=name,
        grid_spec=pltpu.PrefetchScalarGridSpec(
            num_scalar_prefetch=1, grid=(nblk,),
            in_specs=[pl.BlockSpec((tr, cw), lambda i, c_ref: (blk(i, c_ref[0]), 0)),
                      pl.BlockSpec((tr, cw), lambda i, c_ref: (blk(i, 1 - c_ref[0]), 0))],
            out_specs=pl.BlockSpec((tr, cw), lambda i, c_ref: (i, 0)),
            scratch_shapes=[pltpu.VMEM((2, tr, cw), BF16), pltpu.SemaphoreType.DMA((2,)), pltpu.SemaphoreType.DMA((2,))]),
        out_shape=jax.ShapeDtypeStruct((nblk * tr, cw), BF16),
        compiler_params=_params("arbitrary"),
    )(c_arr, g, g)


def _sum_chips_swap(name, q, got, qspec, tr, nth, cw):
    def body(p_ref, q_ref, g0, g1, g2, out_ref, buf, send, recv, lsem):
        t = pl.program_id(0)
        x, y, c = _mesh_pos()
        buf[...] = q_ref[...].astype(F32) + g0[...].astype(F32) + g1[...].astype(F32) + g2[...].astype(F32)
        dst = out_ref.at[pl.ds(pl.multiple_of((p_ref[0] * nth + t) * tr, 16), tr), :]
        cp = _remote(buf, dst, send, recv, (x, y, 1 - c))
        lc = pltpu.make_async_copy(buf, dst, lsem)
        cp.start()
        lc.start()
        lc.wait()
        cp.wait_send()

        @pl.when(t == nth - 1)
        def _():
            theirs = out_ref.at[pl.ds(0, nth * tr), :]
            _remote(theirs, theirs, send, recv, (x, y, c)).wait_recv()

    return _pcall(
        body, name=name,
        grid_spec=pltpu.PrefetchScalarGridSpec(
            num_scalar_prefetch=1, grid=(nth,),
            in_specs=[qspec] + [pl.BlockSpec((tr, cw), lambda t, p, j=j: (j * nth + t, 0)) for j in range(3)],
            out_specs=ANY,
            scratch_shapes=[pltpu.VMEM((tr, cw), F32), pltpu.SemaphoreType.DMA, pltpu.SemaphoreType.DMA,
                            pltpu.SemaphoreType.DMA]),
        out_shape=jax.ShapeDtypeStruct((2 * nth * tr, cw), F32),
        compiler_params=_params("arbitrary"),
    )(_place(), q, got, got, got)


def _rs_geom(g, axis):
    rows, gw = g.shape
    return (rows // 2, gw // N_CHIPS) if axis == 1 else (rows // N_CHIPS // 2, gw)


def _rs_pair_sum(tag, g, axis):
    hr, _ = _rs_geom(g, axis)
    tr = _comm_rows(hr, g.shape[1] * 2)
    nth = hr // tr
    if axis == 1:
        nblk, blk = nth, (lambda i, half: half * nth + i)
    else:
        nblk, blk = N_CHIPS * nth, (lambda i, half: (i // nth) * (2 * nth) + half * nth + i % nth)
    return _pair_sum("rs_pair_" + tag, g, tr, nblk, blk)


def _rs_part(q, axis, hr, cw, tr):
    nth = hr // tr
    if axis == 1:
        return lambda k: pl.BlockSpec((tr, cw), lambda t, p: (t, p[k]))
    return lambda k: pl.BlockSpec((tr, cw), lambda t, p: (p[k] * nth + t, 0))


def _rs_send(tag, q, axis, hr, cw):
    tr = _comm_rows(hr, cw * 2)
    part = _rs_part(q, axis, hr, cw, tr)
    return _send_chips("rs_send_" + tag, [q, q, q], [part(1), part(2), part(3)], tr, hr // tr, cw)


def _rs_finish(tag, q, got, axis, hr, cw):
    tr = _comm_rows(hr, cw * 4)
    return _sum_chips_swap("rs_sum_swap_" + tag, q, got, _rs_part(q, axis, hr, cw, tr)(4), tr, hr // tr, cw)


class _Exchange:
    AG_PLAN = {"ffn1_up": ("ffn1_w_down", "ffn2_w_gate"), "ffn1_down": ("ffn2_w_up", "w_in"),
               "w_in_qkv": ("ssm_w_glu", "w_out", "ple_w_gate", "ple_w_proj"), "attn_fwd_d1": ("ffn2_w_down",)}
    RS_PLAN = {"ffn1_dact": ("ffn2_w_gate", "ple_w_gate", "ple_w_proj"), "ffn1_dwd": ("ffn2_w_up",),
               "ffn1_dwgu": ("ffn2_w_down", "w_in", "ssm_w_glu", "w_out", "ffn1_w_down"),
               "ffn1_dn": ("ffn1_w_gate", "ffn1_w_up")}

    def __init__(self, shards, axes):
        self.shards, self.axes = shards, axes
        self.stage, self.full, self.q, self.geom, self.got = {}, {}, {}, {}, {}

    def ag_host(self, kernel):
        item = lambda k: (self.shards[k], "half", self.shards[k].shape[0] // 2, self.shards[k].shape[1])
        return _host_send([item(k) for k in self.AG_PLAN[kernel]])

    def ag_done(self, kernel, stages):
        self.stage.update(zip(self.AG_PLAN[kernel], stages))

    def weight(self, k):
        if k not in self.full:
            stage = self.stage[k] if k in self.stage else _ag_send("ag_send_" + k, self.shards[k])
            self.full[k] = _ag_finish("ag_asm_" + k, self.shards[k], stage, self.axes[k])
        return self.full[k]

    def grad(self, k, g):
        self.q[k], self.geom[k] = _rs_pair_sum(k, g, self.axes[k]), _rs_geom(g, self.axes[k])

    def rs_host(self, kernel):
        item = lambda k: (self.q[k], "cols" if self.axes[k] == 1 else "rows") + self.geom[k]
        return _host_send([item(k) for k in self.RS_PLAN[kernel]])

    def rs_done(self, kernel, gots):
        self.got.update(zip(self.RS_PLAN[kernel], gots))

    def finish(self):
        return {k: _rs_finish(k, q, self.got[k] if k in self.got else _rs_send(k, q, self.axes[k], *self.geom[k]),
                              self.axes[k], *self.geom[k]) for k, q in self.q.items()}


def _all_reduce_small(v):
    n = v.shape[0]
    h = n // 2

    def body(v_ref, out_ref, pair_in, chips_in, send, recv):
        x, y, c = _mesh_pos()
        me, sib, my_chip = (x, y, c), (x, y, 1 - c), 2 * x + y
        mine = pl.ds(pl.multiple_of(c * h, SUBLANES), h)
        other = pl.ds(pl.multiple_of((1 - c) * h, SUBLANES), h)
        pair = _remote(v_ref.at[other], pair_in, send.at[0], recv.at[0], sib)
        pair.start()
        pair.wait()
        chips_in[my_chip] = v_ref[mine, :] + pair_in[...]
        cps = []
        for j, (cx, cy) in enumerate(_other_chips(x, y)):
            cp = _remote(chips_in.at[my_chip], chips_in.at[my_chip], send.at[1 + j], recv.at[1 + j], (cx, cy, c))
            cp.start()
            cps.append(cp)
        for j, (cx, cy) in enumerate(_other_chips(x, y)):
            slot = chips_in.at[2 * cx + cy]
            _remote(slot, slot, send.at[1 + j], recv.at[1 + j], me).wait_recv()
        out_ref[mine, :] = (chips_in[0] + chips_in[1]) + (chips_in[2] + chips_in[3])
        for cp in cps:
            cp.wait_send()
        swap = _remote(out_ref.at[mine, :], out_ref.at[mine, :], send.at[4], recv.at[4], sib)
        swap.start()
        _remote(out_ref.at[other, :], out_ref.at[other, :], send.at[4], recv.at[4], me).wait_recv()
        swap.wait_send()

    return _pcall(
        body, name="ar_small",
        in_specs=[pl.BlockSpec(memory_space=pltpu.VMEM)], out_specs=pl.BlockSpec(memory_space=pltpu.VMEM),
        out_shape=jax.ShapeDtypeStruct((n, LANES), F32),
        scratch_shapes=[pltpu.VMEM((h, LANES), F32), pltpu.VMEM((N_CHIPS, h, LANES), F32),
                        pltpu.SemaphoreType.DMA((5,)), pltpu.SemaphoreType.DMA((5,))],
        compiler_params=pltpu.CompilerParams(vmem_limit_bytes=V7X_VMEM_LIMIT_BYTES),
    )(v)


def _adamw(name, w, g, m, v):
    R, Cc = w.shape
    tr = _tile(R, max(8, (3 << 18) // Cc // 8 * 8), 8)
    c1 = 1.0 - ADAM_B1 ** ADAM_STEP
    c2 = 1.0 - ADAM_B2 ** ADAM_STEP

    def body(w_ref, g_ref, m_ref, v_ref, d_ref, nm_ref, nv_ref):
        g_ = g_ref[...]
        nm = ADAM_B1 * m_ref[...] + (1.0 - ADAM_B1) * g_
        nv = ADAM_B2 * v_ref[...] + (1.0 - ADAM_B2) * (g_ * g_)
        d_ref[...] = -ADAM_LR * ((nm / c1) / (jnp.sqrt(nv / c2) + ADAM_EPS) + ADAM_WD * w_ref[...])
        nm_ref[...] = nm
        nv_ref[...] = nv

    spec = pl.BlockSpec((tr, Cc), lambda i: (i, 0))
    return _pcall(
        body, name=name, grid=(R // tr,),
        in_specs=[spec] * 4, out_specs=[spec] * 3,
        out_shape=[jax.ShapeDtypeStruct((R, Cc), F32)] * 3,
        compiler_params=_params("parallel"),
    )(w, g, m, v)


def _pack(arrs, rows):
    flat = jnp.concatenate([a.reshape(-1) for a in arrs])
    return jnp.pad(flat, (0, rows * LANES - flat.shape[0])).reshape(rows, LANES)


def _unpack(packed, like):
    flat, out, o = packed.reshape(-1), [], 0
    for a in like:
        out.append(flat[o:o + a.size].reshape(a.shape))
        o += a.size
    return out


BIG = (
    ("ffn1_w_gate", 1), ("ffn1_w_up", 1), ("ffn1_w_down", 0), ("w_in", 1), ("ssm_w_glu", 0), ("w_out", 0),
    ("ffn2_w_gate", 1), ("ffn2_w_up", 1), ("ffn2_w_down", 0), ("ple_w_gate", 0), ("ple_w_proj", 1),
)
SMALL = ("ffn1_norm", "mix_norm", "attn_out_norm", "ssm_lambda_re", "ssm_lambda_im", "ssm_log_dt", "ssm_b_re", "ssm_b_im",
         "ssm_c_re", "ssm_c_im", "ssm_d", "ssm_b_glu", "ssm_out_norm", "ffn2_norm", "ple_norm", "final_norm")
WEIGHTS = ("ffn1_norm", "ffn1_w_gate", "ffn1_w_up", "ffn1_w_down", "mix_norm", "w_in", "attn_out_norm", "ssm_lambda_re",
           "ssm_lambda_im", "ssm_log_dt", "ssm_b_re", "ssm_b_im", "ssm_c_re", "ssm_c_im", "ssm_d", "ssm_w_glu", "ssm_b_glu",
           "ssm_out_norm", "w_out", "ffn2_norm", "ffn2_w_gate", "ffn2_w_up", "ffn2_w_down", "ple_norm", "ple_w_gate",
           "ple_w_proj", "final_norm")


def _pad_to(a, axis, n):
    pad = [(0, 0), (0, 0)]
    pad[axis] = (0, n - a.shape[axis])
    return jnp.pad(a, pad)


def _local_step(x, p, tgt, w, ex):
    S, D = x.shape
    A = w["attn_out_norm"].shape[-1]
    W = w["ssm_d"].shape[-1]
    G, P = w["ssm_lambda_re"].shape[-2:]
    C = w["ssm_b_re"].shape[-1]
    GB = G // SSM_BLOCK_GROUPS
    T = min(1024, S)
    row = lambda name: w[name].reshape(1, -1)
    gs = {}

    h1, ffn1_saved = _ffn_fwd("ffn1", x, row("ffn1_norm"), ex)
    n2 = _rms_fwd("mix_norm", h1, row("mix_norm"))
    w_in = ex.weight("w_in")
    n2p = _to_attn_order(n2)
    host, done = _carried(ex, "ag", "w_in_qkv")
    qkv, *outs = _mm("w_in_qkv", [n2p], [w_in[:, :3 * A]], [F32], tm=1024, tn=1024, host=host)
    done(outs)
    (s_in,) = _mm("w_in_ssm", [n2], [w_in[:, 3 * A:]], [F32], tm=1024, tn=1024)
    ya, lse = _attn_fwd(qkv, ex)

    col = lambda name: w[name].reshape(G * P, 1)
    logdt_x = jnp.repeat(w["ssm_log_dt"].reshape(G), P).reshape(G * P, 1)
    b_re, b_im = w["ssm_b_re"].reshape(G * P, C), w["ssm_b_im"].reshape(G * P, C)
    lrdt, lidt, bbr, bbi = _ssm_disc(col("ssm_lambda_re"), col("ssm_lambda_im"), logdt_x, b_re, b_im)
    gsz = SSM_BLOCK_GROUPS
    to_bb = lambda t: _block_diag(t.reshape(GB, gsz, P, C).transpose(0, 1, 3, 2))
    bb = jnp.concatenate([to_bb(bbr), to_bb(bbi)], axis=2).astype(BF16)
    to_cc = lambda t: _block_diag(t.reshape(GB, gsz, C, P).transpose(0, 1, 3, 2))
    cc = jnp.concatenate([to_cc(w["ssm_c_re"]), -to_cc(w["ssm_c_im"])], axis=1).astype(BF16)
    lam_dt = jnp.stack([lrdt.reshape(GB, gsz * P), lidt.reshape(GB, gsz * P)], axis=1)
    ufp = _ssm_perm(s_in, T)
    ypre, hstart = _ssm_fwd(ufp, bb, cc, lam_dt, row("ssm_d"), T)

    def glu_in(ins, ps):
        yg = _gelu(ins[0])
        return [yg, yg], []

    yg, ygb = _rowwise("ssm_gelu", glu_in, [ypre], [], [(W, F32), (W, BF16)])
    w_glu = ex.weight("ssm_w_glu")

    def glu_out(accs, ex):
        gl = accs[0] + ex[1]
        return [ex[0] * _sigmoid(gl), gl]

    ybp, gl = _mm("ssm_glu", [ygb], [w_glu], [F32, F32], extras=[(yg, "mn"), (row("ssm_b_glu"), "n")],
                  epilogue=glu_out, tm=1024, tn=1024)
    yb = _ssm_unperm(ybp, T)
    na = _from_attn_order(_rms_fwd("attn_out_norm", ya, row("attn_out_norm")))
    nb = _rms_fwd("ssm_out_norm", yb, row("ssm_out_norm"))
    w_out = ex.weight("w_out")
    (h2,) = _mm("w_out", [na, nb], [w_out[:A], w_out[A:]], [F32], pairs=((0, 0, 0), (1, 1, 0)), extras=[(h1, "mn")],
                epilogue=lambda accs, ex: [ex[0] + accs[0]], tm=1024, tn=1024)
    h3, ffn2_saved = _ffn_fwd("ffn2", h2, row("ffn2_norm"), ex)
    n4 = _rms_fwd("ple_norm", h3, row("ple_norm"))
    (pe,) = _mm("ple_proj", [p], [ex.weight("ple_w_proj")], [F32], tm=1024, tn=1024)

    def ple_out(accs, ex):
        gate = _sigmoid(accs[0])
        return [ex[1] + gate * ex[0], gate]

    h4, gate = _mm("ple_gate", [n4], [ex.weight("ple_w_gate")], [F32, F32], extras=[(pe, "mn"), (h3, "mn")],
                   epilogue=ple_out, tm=1024, tn=1024)

    dh4, err2, gs["final_norm"] = _loss_head(h4, tgt, row("final_norm"))
    loss = (0.5 / D) * jnp.sum(err2)

    def ple_bwd(ins, ps):
        dh, gt, pe_ = ins
        return [dh * gt, dh * pe_ * gt * (1.0 - gt)], []

    dpe, dpg = _rowwise("ple_bwd", ple_bwd, [dh4, gate, pe], [], [(D, BF16), (D, BF16)])
    (d_ple_proj,) = _mm("ple_dproj", [p], [dpe], [BF16], ta=True, tm=256, tn=2048, tk=1024)
    (d_ple_gate,) = _mm("ple_dgate", [n4], [dpg], [BF16], ta=True, tm=1024, tn=1024, tk=2048)
    (dn4,) = _mm("ple_dn", [dpg], [ex.weight("ple_w_gate")], [F32], tb=True, tm=1024, tn=1024)
    (dh3, dh3b), gs["ple_norm"] = _rms_bwd("ple_dnorm", dn4, h3, row("ple_norm"), dres=dh4, copy_scale=0.5)
    (dh2, dh2b), gs["ffn2_norm"] = _ffn_bwd("ffn2", dh3, dh3b, h2, row("ffn2_norm"), ex, ffn2_saved, copy_scale=1.0)
    (dna,) = _mm("w_out_dna", [_to_attn_order(dh2b)], [w_out[:A]], [F32], tb=True, tm=1024, tn=1024)
    (dnb,) = _mm("w_out_dnb", [dh2b], [w_out[A:]], [F32], tb=True, tm=1024, tn=1024)
    (d_wout_a,) = _mm("w_out_dwa", [na], [dh2b], [BF16], ta=True, tm=1024, tn=1024, tk=2048)
    (d_wout_b,) = _mm("w_out_dwb", [nb], [dh2b], [BF16], ta=True, tm=1024, tn=1024, tk=2048)
    d_w_out = jnp.concatenate([d_wout_a, d_wout_b], axis=0)
    (dya,), gs["attn_out_norm"] = _rms_bwd("attn_out_dnorm", dna, ya, row("attn_out_norm"))
    (dyb,), gs["ssm_out_norm"] = _rms_bwd("ssm_out_dnorm", dnb, yb, row("ssm_out_norm"))

    dybp = _ssm_perm(dyb, T)

    def glu_bwd(ins, ps):
        dy, yg_, gl_ = ins
        sg = _sigmoid(gl_)
        dgl = dy * yg_ * sg * (1.0 - sg)
        return [dgl, dy * sg], [jnp.sum(dgl, axis=0, keepdims=True)]

    dgl, dyg_direct, gs["ssm_b_glu"] = _rowwise("ssm_glu_bwd", glu_bwd, [dybp, yg, gl], [], [(W, BF16), (W, F32)], accs=[W])
    (d_w_glu,) = _mm("ssm_dwglu", [ygb], [dgl], [BF16], ta=True, tm=1024, tn=1024, tk=2048)
    (dypre,) = _mm("ssm_dyg", [dgl], [w_glu], [F32], tb=True, extras=[(dyg_direct, "mn"), (ypre, "mn")],
                   epilogue=lambda accs, ex: [(accs[0] + ex[0]) * _gelu_grad(ex[1])], tm=1024, tn=1024)
    dufp, dbb, dcc, da, gs["ssm_d"] = _ssm_bwd(ufp, dypre, bb, bb.transpose(0, 2, 1), cc, cc.transpose(0, 2, 1),
                                               lam_dt, row("ssm_d"), hstart, T)
    ns = gsz * P
    from_bb = lambda t: _block_diag_take(t, gsz).transpose(0, 1, 3, 2).reshape(G * P, C)
    from_cc = lambda t: _block_diag_take(t, gsz).transpose(0, 1, 3, 2).reshape(w["ssm_c_re"].shape)
    gs["ssm_c_re"], gs["ssm_c_im"] = from_cc(dcc[:, :ns]), -from_cc(dcc[:, ns:])
    da = da.sum(axis=1)
    dar, dai = da[:, :ns].reshape(G * P, 1), da[:, ns:].reshape(G * P, 1)
    dlr, dli, dlogdt, dbr, dbi = _ssm_disc_bwd(col("ssm_lambda_re"), col("ssm_lambda_im"), logdt_x, b_re, b_im,
                                               dar, dai, from_bb(dbb[:, :, :ns]), from_bb(dbb[:, :, ns:]))
    gs["ssm_lambda_re"], gs["ssm_lambda_im"] = dlr.reshape(w["ssm_lambda_re"].shape), dli.reshape(w["ssm_lambda_im"].shape)
    gs["ssm_log_dt"] = dlogdt.reshape(G, P).sum(axis=1).reshape(w["ssm_log_dt"].shape)
    gs["ssm_b_re"], gs["ssm_b_im"] = dbr.reshape(w["ssm_b_re"].shape), dbi.reshape(w["ssm_b_im"].shape)
    ds_in = _ssm_unperm(dufp, T)

    dq, dk, dv = _attn_bwd(qkv, dya, ya, lse)
    dqkv = jnp.concatenate([dq, dk, dv], axis=1).astype(BF16)
    (d_w_qkv,) = _mm("w_in_dw_qkv", [n2p], [dqkv], [BF16], ta=True, tm=1024, tn=1024, tk=2048)
    (d_w_s,) = _mm("w_in_dw_ssm", [n2], [ds_in], [BF16], ta=True, tm=1024, tn=1024, tk=2048)
    d_w_in = jnp.concatenate([d_w_qkv, d_w_s], axis=1)
    dz = jnp.concatenate([_from_attn_order(dqkv), ds_in.astype(BF16)], axis=1)
    (dn2,) = _mm("w_in_dn", [dz], [w_in], [F32], tb=True, tm=1024, tn=1024)
    (dh1, dh1b), gs["mix_norm"] = _rms_bwd("mix_dnorm", dn2, h1, row("mix_norm"), dres=dh2, copy_scale=0.5)
    for k, g in (("ple_w_gate", d_ple_gate), ("ple_w_proj", d_ple_proj), ("w_out", d_w_out), ("ssm_w_glu", d_w_glu),
                 ("w_in", d_w_in)):
        ex.grad(k, g)
    (dx,), gs["ffn1_norm"] = _ffn_bwd("ffn1", dh1, dh1b, x, row("ffn1_norm"), ex, ffn1_saved, copy_scale=None)
    small = {k: gs[k].reshape(w[k].shape) for k in SMALL}
    return loss, dx, ex.finish(), small


def kernel(x, p, ffn1_norm, ffn1_w_gate, ffn1_w_up, ffn1_w_down, mix_norm, w_in, attn_out_norm, ssm_lambda_re, ssm_lambda_im, ssm_log_dt, ssm_b_re, ssm_b_im, ssm_c_re, ssm_c_im, ssm_d, ssm_w_glu, ssm_b_glu, ssm_out_norm, w_out, ffn2_norm, ffn2_w_gate, ffn2_w_up, ffn2_w_down, ple_norm, ple_w_gate, ple_w_proj, final_norm, loss_target, m_ffn1_norm, m_ffn1_w_gate, m_ffn1_w_up, m_ffn1_w_down, m_mix_norm, m_w_in, m_attn_out_norm, m_ssm_lambda_re, m_ssm_lambda_im, m_ssm_log_dt, m_ssm_b_re, m_ssm_b_im, m_ssm_c_re, m_ssm_c_im, m_ssm_d, m_ssm_w_glu, m_ssm_b_glu, m_ssm_out_norm, m_w_out, m_ffn2_norm, m_ffn2_w_gate, m_ffn2_w_up, m_ffn2_w_down, m_ple_norm, m_ple_w_gate, m_ple_w_proj, m_final_norm, v_ffn1_norm, v_ffn1_w_gate, v_ffn1_w_up, v_ffn1_w_down, v_mix_norm, v_w_in, v_attn_out_norm, v_ssm_lambda_re, v_ssm_lambda_im, v_ssm_log_dt, v_ssm_b_re, v_ssm_b_im, v_ssm_c_re, v_ssm_c_im, v_ssm_d, v_ssm_w_glu, v_ssm_b_glu, v_ssm_out_norm, v_w_out, v_ffn2_norm, v_ffn2_w_gate, v_ffn2_w_up, v_ffn2_w_down, v_ple_norm, v_ple_w_gate, v_ple_w_proj, v_final_norm):
    args = locals()
    w = {k: args[k] for k in WEIGHTS}
    m = {k: args["m_" + k] for k in WEIGHTS}
    v = {k: args["v_" + k] for k in WEIGHTS}
    w2 = {k: w[k].reshape(w[k].shape[-2:]) for k, _ in BIG}

    axes = [ax for _, ax in BIG]
    padded = {k: -(-w2[k].shape[ax] // LANES) * LANES for k, ax in BIG}
    shards = [_pad_to(w2[k].astype(BF16), ax, padded[k]) for k, ax in BIG]
    ex = _Exchange(dict(zip([k for k, _ in BIG], shards)), dict(BIG))
    loss_local, dx, summed, gsmall = _local_step(x[0], p[0, 0], loss_target[0], w, ex)
    loss = lax.psum(loss_local, MESH_AXES)
    n_small = sum(w[k].size for k in SMALL)
    rows = -(-n_small // (2 * SUBLANES * LANES)) * 2 * SUBLANES
    gs_sum = _all_reduce_small(_pack([gsmall[k] for k in SMALL], rows))

    grads, delta, new_m, new_v = {}, {}, {}, {}
    for k, ax in BIG:
        gfull = summed[k]
        g2 = lax.slice_in_dim(gfull, 0, w2[k].shape[ax], axis=ax)
        d2, nm2, nv2 = _adamw("adamw_" + k, w2[k], g2, m[k].reshape(w2[k].shape), v[k].reshape(w2[k].shape))
        grads[k], delta[k], new_m[k], new_v[k] = (t.reshape(w[k].shape) for t in (g2, d2, nm2, nv2))
    small_like = [w[k] for k in SMALL]
    ds, nms, nvs = _adamw("adamw_small", _pack(small_like, rows), gs_sum, _pack([m[k] for k in SMALL], rows),
                          _pack([v[k] for k in SMALL], rows))
    for k, g_, d_, nm_, nv_ in zip(SMALL, _unpack(gs_sum, small_like), _unpack(ds, small_like),
                                   _unpack(nms, small_like), _unpack(nvs, small_like)):
        grads[k], delta[k], new_m[k], new_v[k] = g_, d_, nm_, nv_

    return (loss, dx[None], *[grads[k] for k in WEIGHTS], *[delta[k] for k in WEIGHTS],
            *[new_m[k] for k in WEIGHTS], *[new_v[k] for k in WEIGHTS])
```

```python
import functools
import math

import jax
import jax.numpy as jnp
from jax import lax
from jax.experimental import pallas as pl
from jax.experimental.pallas import tpu as pltpu

F32 = jnp.float32
BF16 = jnp.bfloat16
MESH = pl.DeviceIdType.MESH
MESH_AXES = ("x", "y", "c")
N_CHIPS = 4
N_DEV = 8

V7X_VMEM_LIMIT_BYTES = 56 << 20
LANES = 128
SUBLANES = 8

HEAD_DIM = 64
SWA_BLOCK = 128
DILATIONS = (1, 4, 16)
SSM_BLOCK_GROUPS = 8
NORM_EPS = 1e-6
MASK_VALUE = -1e30

ADAM_LR = 0.001
ADAM_B1 = 0.9
ADAM_B2 = 0.999
ADAM_EPS = 1e-08
ADAM_WD = 0.01
ADAM_STEP = 10

GELU_C = math.sqrt(2.0 / math.pi)
GELU_K = 0.044715


def _pcall(body, **kw):
    return pl.pallas_call(body, **kw)


def _params(*sem):
    return pltpu.CompilerParams(dimension_semantics=sem, vmem_limit_bytes=V7X_VMEM_LIMIT_BYTES)


def _tile(n, target, align):
    best = None
    for t in range(align, min(n, target) + 1, align):
        if n % t == 0:
            best = t
    return n if best is None else best


def _sigmoid(x):
    return 0.5 * jnp.tanh(0.5 * x) + 0.5


class _Host:
    def __init__(self, ins, out_shapes, n_sem, start, wait):
        self.ins, self.out_shapes, self.n_sem, self.start, self.wait = ins, out_shapes, n_sem, start, wait


def _mm(name, lhs, rhs, outs, pairs=((0, 0, 0),), epilogue=None, extras=(), ta=False, tb=False,
        tm=1024, tn=512, tk=2048, host=None):
    nl, nr, ne, no = len(lhs), len(rhs), len(extras), len(outs)
    nhi, nho = (len(host.ins), len(host.out_shapes)) if host else (0, 0)
    n_acc = 1 + max(p[2] for p in pairs)
    (K, M) = lhs[0].shape if ta else lhs[0].shape[::-1]
    (N, K2) = rhs[0].shape if tb else rhs[0].shape[::-1]
    assert K == K2, (name, lhs[0].shape, rhs[0].shape)
    tm, tn, tk = _tile(M, tm, LANES), _tile(N, tn, LANES), _tile(K, tk, LANES)
    ni, nj, nk = M // tm, N // tn, K // tk
    n_scr = n_acc if nk > 1 else 0
    if epilogue is None:
        epilogue = lambda accs, ex: accs
    dn = (((0 if ta else 1,), (1 if tb else 0,)), ((), ()))

    def body(*refs):
        refs = list(refs)
        take = lambda n: [refs.pop(0) for _ in range(n)]
        l, r, e, hin, o, hout, acc = take(nl), take(nr), take(ne), take(nhi), take(no), take(nho), take(n_scr)
        i, j, k = pl.program_id(0), pl.program_id(1), pl.program_id(2)
        if host:
            @pl.when((i == 0) & (j == 0) & (k == 0))
            def _():
                host.start(hin, hout, *refs)

        parts = [None] * n_acc
        for li, ri, ai in pairs:
            d = lax.dot_general(l[li][...].astype(BF16), r[ri][...].astype(BF16), dn,
                                preferred_element_type=F32)
            parts[ai] = d if parts[ai] is None else parts[ai] + d

        def finish(accs):
            res = epilogue(accs, [x[...] for x in e])
            for ref, val in zip(o, res):
                ref[...] = val.astype(ref.dtype)

        if nk == 1:
            finish(parts)
        else:
            @pl.when(k == 0)
            def _():
                for ai in range(n_acc):
                    acc[ai][...] = parts[ai]

            @pl.when(k > 0)
            def _():
                for ai in range(n_acc):
                    acc[ai][...] += parts[ai]

            @pl.when(k == nk - 1)
            def _():
                finish([a[...] for a in acc])

        if host:
            @pl.when((i == ni - 1) & (j == nj - 1) & (k == nk - 1))
            def _():
                host.wait(hin, hout, *refs)

    lspec = pl.BlockSpec((tk, tm), lambda i, j, k: (k, i)) if ta else pl.BlockSpec((tm, tk), lambda i, j, k: (i, k))
    rspec = pl.BlockSpec((tn, tk), lambda i, j, k: (j, k)) if tb else pl.BlockSpec((tk, tn), lambda i, j, k: (k, j))
    especs = []
    for arr, kind in extras:
        if kind == "mn":
            especs.append(pl.BlockSpec((tm, tn), lambda i, j, k: (i, j)))
        elif kind == "n":
            especs.append(pl.BlockSpec((1, tn), lambda i, j, k: (0, j)))
        else:
            especs.append(pl.BlockSpec((tm, 1), lambda i, j, k: (i, 0)))
    any_spec = pl.BlockSpec(memory_space=pl.ANY)
    sems = [pltpu.SemaphoreType.DMA((host.n_sem,)), pltpu.SemaphoreType.DMA((host.n_sem,))] if host else []
    res = _pcall(
        body, name=name,
        grid=(ni, nj, nk),
        in_specs=[lspec] * nl + [rspec] * nr + especs + [any_spec] * nhi,
        out_specs=[pl.BlockSpec((tm, tn), lambda i, j, k: (i, j))] * no + [any_spec] * nho,
        out_shape=[jax.ShapeDtypeStruct((M, N), dt) for dt in outs] + (list(host.out_shapes) if host else []),
        scratch_shapes=[pltpu.VMEM((tm, tn), F32)] * n_scr + sems,
        compiler_params=_params(*(("arbitrary",) * 3 if host else ("parallel", "parallel", "arbitrary"))),
    )(*lhs, *rhs, *[a for a, _ in extras], *(host.ins if host else []))
    return res


ROWWISE_BLOCK_BYTES = 20 << 20


def _rowwise(name, fn, ins, params, outs, accs=(), ts=None):
    S = ins[0].shape[0]
    if ts is None:
        row_bytes = sum(a.shape[1] * a.dtype.itemsize for a in ins) + sum(w * jnp.dtype(dt).itemsize for w, dt in outs)
        ts = next((t for t in (1024, 512) if t * row_bytes <= ROWWISE_BLOCK_BYTES), 256)
    ts = _tile(S, ts, 16)
    ni, npar, no, na = len(ins), len(params), len(outs), len(accs)

    def body(*refs):
        i_refs, p_refs = refs[:ni], refs[ni:ni + npar]
        o_refs = refs[ni + npar:ni + npar + no]
        a_refs = refs[ni + npar + no:]
        res_o, res_a = fn([r[...] for r in i_refs], [r[...] for r in p_refs])
        for ref, val in zip(o_refs, res_o):
            ref[...] = val.astype(ref.dtype)
        if na:
            @pl.when(pl.program_id(0) == 0)
            def _():
                for ref in a_refs:
                    ref[...] = jnp.zeros(ref.shape, F32)

            for ref, val in zip(a_refs, res_a):
                ref[...] += val

    res = _pcall(
        body, name=name,
        grid=(S // ts,),
        in_specs=[pl.BlockSpec((ts, a.shape[1]), lambda i: (i, 0)) for a in ins]
        + [pl.BlockSpec(p.shape, lambda i: (0, 0)) for p in params],
        out_specs=[pl.BlockSpec((ts, w), lambda i: (i, 0)) for w, _ in outs]
        + [pl.BlockSpec((1, w), lambda i: (0, 0)) for w in accs],
        out_shape=[jax.ShapeDtypeStruct((S, w), dt) for w, dt in outs]
        + [jax.ShapeDtypeStruct((1, w), F32) for w in accs],
        compiler_params=_params("arbitrary"),
    )(*ins, *params)
    return res


def _xhat(x):
    r = lax.rsqrt(jnp.mean(x * x, axis=-1, keepdims=True) + NORM_EPS)
    return x * r, r


def _rms_fwd(name, x, g):
    def fn(ins, ps):
        xh, _ = _xhat(ins[0])
        return [xh * ps[0]], []

    return _rowwise(name, fn, [x], [g], [(x.shape[1], BF16)])[0]


def _rms_bwd(name, dn, x, g, dres=None, copy_scale=None):
    w = x.shape[1]

    def fn(ins, ps):
        dn_, x_ = ins[0], ins[1]
        xh, r = _xhat(x_)
        dxh = dn_ * ps[0]
        dx = r * (dxh - xh * jnp.mean(dxh * xh, axis=-1, keepdims=True))
        if dres is not None:
            dx = dx + ins[2]
        o = [dx] + ([dx * copy_scale] if copy_scale is not None else [])
        return o, [jnp.sum(dn_ * xh, axis=0, keepdims=True)]

    ins = [dn, x] + ([dres] if dres is not None else [])
    outs = [(w, F32)] + ([(w, BF16)] if copy_scale is not None else [])
    res = _rowwise(name, fn, ins, [g], outs, accs=[w])
    return res[:-1], res[-1]


def _swiglu_epilogue(accs, ex):
    g, u = accs
    sg = _sigmoid(g)
    s = g * sg
    return [u * (sg + s * (1.0 - sg)), s, s * u]


def _dswiglu_epilogue(accs, ex):
    da = accs[0]
    return [da * ex[0].astype(F32), da * ex[1].astype(F32)]


def _carried(ex, kind, kernel):
    if kernel not in (ex.AG_PLAN if kind == "ag" else ex.RS_PLAN):
        return None, lambda outs: None
    if kind == "ag":
        return ex.ag_host(kernel), lambda outs: ex.ag_done(kernel, outs)
    return ex.rs_host(kernel), lambda outs: ex.rs_done(kernel, outs)


def _ffn_fwd(tag, h, gnorm, ex):
    n = _rms_fwd(tag + "_norm", h, gnorm)
    host, done = _carried(ex, "ag", tag + "_up")
    g, u, a, *outs = _mm(tag + "_up", [n], [ex.weight(tag + "_w_gate"), ex.weight(tag + "_w_up")], [BF16, BF16, BF16],
                         pairs=((0, 0, 0), (0, 1, 1)), epilogue=_swiglu_epilogue, tm=1024, tn=512, host=host)
    done(outs)
    host, done = _carried(ex, "ag", tag + "_down")
    hout, *outs = _mm(tag + "_down", [a], [ex.weight(tag + "_w_down")], [F32], extras=[(h, "mn")],
                      epilogue=lambda accs, ex_: [ex_[0] + 0.5 * accs[0]], tm=512, tn=1024, tk=8192, host=host)
    done(outs)
    return hout, (n, g, u, a)


def _ffn_bwd(tag, dh, dhb_half, h, gnorm, ex, saved, copy_scale):
    n, g, u, a = saved
    wg, wu, wd = (ex.weight(tag + k) for k in ("_w_gate", "_w_up", "_w_down"))
    host, done = _carried(ex, "rs", tag + "_dact")
    dg, du, *outs = _mm(tag + "_dact", [dhb_half], [wd], [BF16, BF16], tb=True, extras=[(g, "mn"), (u, "mn")],
                        epilogue=_dswiglu_epilogue, tm=2048, tn=512, host=host)
    done(outs)
    host, done = _carried(ex, "rs", tag + "_dwd")
    dwd, *outs = _mm(tag + "_dwd", [a], [dhb_half], [BF16], ta=True, tm=512, tn=2048, tk=2048, host=host)
    done(outs)
    ex.grad(tag + "_w_down", dwd)
    host, done = _carried(ex, "rs", tag + "_dwgu")
    dwg, dwu, *outs = _mm(tag + "_dwgu", [n], [dg, du], [BF16, BF16], pairs=((0, 0, 0), (0, 1, 1)), ta=True,
                          tm=1024, tn=512, tk=2048, host=host)
    done(outs)
    ex.grad(tag + "_w_gate", dwg)
    ex.grad(tag + "_w_up", dwu)
    host, done = _carried(ex, "rs", tag + "_dn")
    dn, *outs = _mm(tag + "_dn", [dg, du], [wg, wu], [F32], pairs=((0, 0, 0), (1, 1, 0)), tb=True,
                    tm=1024, tn=1024, tk=1408, host=host)
    done(outs)
    return _rms_bwd(tag + "_dnorm", dn, h, gnorm, dres=dh, copy_scale=copy_scale)


ATTN_HEAD_PAIRS = 8


def _to_attn_order(a):
    S, w = a.shape
    return a.reshape(S // 16, 16, w).transpose(1, 0, 2).reshape(S, w)


def _from_attn_order(a):
    S, w = a.shape
    return a.reshape(16, S // 16, w).transpose(1, 0, 2).reshape(S, w)


def _attn_geom(S, d):
    s16 = S // 16
    if d == 16:
        return (16, s16), (1, SWA_BLOCK), (lambda r, b: (r, b)), 16, s16 // SWA_BLOCK
    if d == 4:
        return (4, 4, s16), (4, 1, SWA_BLOCK // 4), (lambda r, b: (0, r, b)), 4, s16 // (SWA_BLOCK // 4)
    return (16, s16), (16, SWA_BLOCK // 16), (lambda r, b: (0, b)), 1, s16 // (SWA_BLOCK // 16)


def _attn_pos(rho, d):
    if d == 16:
        return rho
    if d == 4:
        return 4 * (rho & 31) + (rho >> 5)
    return 16 * (rho & 7) + (rho >> 3)


def _attn_spec(S, d, lb, col, shift=0):
    _, blk, idx, _, nb = _attn_geom(S, d)
    return pl.BlockSpec(blk + (lb,), lambda r, cb, b: idx(r, jnp.clip(b + shift, 0, nb - 1)) + (col(cb),))


def _attn_view(a, d):
    return a.reshape(_attn_geom(a.shape[0], d)[0] + (a.shape[1],))


def _attn_valid(d):
    qp = _attn_pos(lax.broadcasted_iota(jnp.int32, (SWA_BLOCK, 2 * SWA_BLOCK), 0), d)
    kk = lax.broadcasted_iota(jnp.int32, (SWA_BLOCK, 2 * SWA_BLOCK), 1)
    kp = _attn_pos(kk & (SWA_BLOCK - 1), d)
    is_prev = kk < SWA_BLOCK
    return qp, kp, is_prev


def _head_masks(rows=SWA_BLOCK):
    lane = lax.broadcasted_iota(jnp.int32, (rows, LANES), 1)
    return [lane < HEAD_DIM, lane >= HEAD_DIM]


def _attn_ld(ref, sl):
    t = ref[(slice(None),) * (len(ref.shape) - 1) + (sl,)]
    return t.reshape(-1, t.shape[-1])


def _attn_st(ref, sl, val):
    ref[(slice(None),) * (len(ref.shape) - 1) + (sl,)] = val.reshape(ref.shape[:-1] + (val.shape[-1],))


def _per_head(t, first):
    sw = pltpu.roll(t, HEAD_DIM, 1)
    lo = lax.broadcasted_iota(jnp.int32, t.shape, 1) < HEAD_DIM
    return jnp.where(lo, t, sw) if first else jnp.where(lo, sw, t)


def _dot_nt(a, b):
    return lax.dot_general(a, b, (((1,), (1,)), ((), ())), preferred_element_type=F32)


def _dot_tn(a, b):
    return lax.dot_general(a, b, (((0,), (0,)), ((), ())), preferred_element_type=F32)


def _dot(a, b):
    return jnp.dot(a, b, preferred_element_type=F32)


def _keep(mask, t):
    return jnp.where(mask, t.astype(F32), 0.0).astype(BF16)


def _attn_cols(A):
    lb = min(A, LANES * ATTN_HEAD_PAIRS)
    ncol = A // lb
    return lb, ncol, [lambda cb, part=part: part * ncol + cb for part in range(3)], (lambda cb: cb)


def _attn_fwd_stage(name, qkv, d, prev, final, host=None):
    S, A3 = qkv.shape
    A = A3 // 3
    lb, ncol, (cq, ck, cv), ca = _attn_cols(A)
    view, _, _, nres, nb = _attn_geom(S, d)
    scale = HEAD_DIM ** -0.5
    has_prev = prev is not None
    n_out = 2 if final else 3
    nhi, nho = (len(host.ins), len(host.out_shapes)) if host else (0, 0)

    def body(*refs):
        q_ref, kp_ref, kc_ref, vp_ref, vc_ref = refs[:5]
        p_refs = refs[5:8] if has_prev else ()
        n_in = 5 + len(p_refs)
        hin, o_refs = refs[n_in:n_in + nhi], refs[n_in + nhi:n_in + nhi + n_out]
        hout, sems = refs[n_in + nhi + n_out:n_in + nhi + n_out + nho], refs[n_in + nhi + n_out + nho:]
        b = pl.program_id(2)
        if host:
            @pl.when((pl.program_id(0) == 0) & (pl.program_id(1) == 0) & (b == 0))
            def _():
                host.start(hin, hout, *sems)
        qp, kp_, is_prev = _attn_valid(d)
        valid = (is_prev & (kp_ >= qp) & (b > 0)) | (jnp.logical_not(is_prev) & (kp_ <= qp))
        hm, hm2 = _head_masks(), _head_masks(2 * SWA_BLOCK)
        for hp in range(lb // LANES):
            sl = slice(hp * LANES, (hp + 1) * LANES)
            q = _attn_ld(q_ref, sl)
            k2 = jnp.concatenate([_attn_ld(kp_ref, sl), _attn_ld(kc_ref, sl)], axis=0).astype(BF16)
            v2 = jnp.concatenate([_attn_ld(vp_ref, sl), _attn_ld(vc_ref, sl)], axis=0)
            o = jnp.zeros((SWA_BLOCK, LANES), F32)
            m = jnp.zeros((SWA_BLOCK, LANES), F32)
            l = jnp.zeros((SWA_BLOCK, LANES), F32)
            for hh in range(2):
                s = jnp.where(valid, _dot_nt(_keep(hm[hh], q), k2) * scale, MASK_VALUE)
                mh = jnp.max(s, axis=-1, keepdims=True)
                p = jnp.exp(s - mh)
                lh = jnp.sum(p, axis=-1, keepdims=True)
                o = o + _dot(p.astype(BF16), _keep(hm2[hh], v2))
                m = jnp.where(hm[hh], mh, m)
                l = jnp.where(hm[hh], lh, l)
            if has_prev:
                po, pm, pl_ = (_attn_ld(r, sl) for r in p_refs)
                mn = jnp.maximum(m, pm)
                w_new, w_old = jnp.exp(m - mn), jnp.exp(pm - mn)
                o = o * w_new + po * w_old
                l = l * w_new + pl_ * w_old
                m = mn
            if final:
                _attn_st(o_refs[0], sl, o / l)
                _attn_st(o_refs[1], sl, m + jnp.log(l))
            else:
                _attn_st(o_refs[0], sl, o)
                _attn_st(o_refs[1], sl, m)
                _attn_st(o_refs[2], sl, l)

        if host:
            @pl.when((pl.program_id(0) == nres - 1) & (pl.program_id(1) == ncol - 1) & (b == nb - 1))
            def _():
                host.wait(hin, hout, *sems)

    qk = _attn_view(qkv, d)
    prev_v = [_attn_view(t, d) for t in prev] if has_prev else []
    sp = functools.partial(_attn_spec, S, d, lb)
    any_spec = pl.BlockSpec(memory_space=pl.ANY)
    res = _pcall(
        body, name=name,
        grid=(nres, ncol, nb),
        in_specs=[sp(cq), sp(ck, -1), sp(ck), sp(cv, -1), sp(cv)] + [sp(ca)] * len(prev_v) + [any_spec] * nhi,
        out_specs=[sp(ca)] * n_out + [any_spec] * nho,
        out_shape=[jax.ShapeDtypeStruct(view + (A,), F32)] * n_out + (list(host.out_shapes) if host else []),
        scratch_shapes=[pltpu.SemaphoreType.DMA((host.n_sem,)), pltpu.SemaphoreType.DMA((host.n_sem,))] if host else [],
        compiler_params=_params(*(("arbitrary",) * 3 if host else ("parallel", "parallel", "arbitrary"))),
    )(qk, qk, qk, qk, qk, *prev_v, *(host.ins if host else []))
    return [t.reshape(S, A) for t in res[:n_out]], res[n_out:]


def _attn_fwd(qkv, ex):
    st = None
    for i, d in enumerate(DILATIONS):
        name = "attn_fwd_d%d" % d
        host, done = _carried(ex, "ag", name)
        st, outs = _attn_fwd_stage(name, qkv, d, st, final=(i == len(DILATIONS) - 1), host=host)
        done(outs)
    return st


def _attn_delta(dya, ya):
    S, A = ya.shape
    ri = lax.broadcasted_iota(jnp.int32, (A, A), 0) // HEAD_DIM
    ci = lax.broadcasted_iota(jnp.int32, (A, A), 1) // HEAD_DIM
    ones_bd = (ri == ci).astype(BF16)

    def fn(ins, ps):
        prod = ins[0] * ins[1]
        hi = prod.astype(BF16)
        lo = (prod - hi.astype(F32)).astype(BF16)
        return [_dot(hi, ps[0]) + _dot(lo, ps[0])], []

    return _rowwise("attn_delta", fn, [dya, ya], [ones_bd], [(A, F32)])[0]


def _attn_bwd_stage(name, qkv, do, lse, delta, d, prev):
    S, A3 = qkv.shape
    A = A3 // 3
    lb, ncol, (cq, ck, cv), ca = _attn_cols(A)
    view, _, _, nres, nb = _attn_geom(S, d)
    scale = HEAD_DIM ** -0.5
    has_prev = prev is not None
    lane_slices = [slice(hp * LANES, (hp + 1) * LANES) for hp in range(lb // LANES)]

    def body(*refs):
        q_ref, kp_ref, kc_ref, vp_ref, vc_ref, do_ref, lse_ref, dl_ref = refs[:8]
        p_refs = refs[8:11] if has_prev else ()
        dq_ref, dk_ref, dv_ref, dk_c, dv_c = refs[8 + len(p_refs):]
        b = pl.program_id(2)

        def put_keys(sl, dk, dv):
            if has_prev:
                dk, dv = dk + _attn_ld(p_refs[1], sl), dv + _attn_ld(p_refs[2], sl)
            _attn_st(dk_ref, sl, dk)
            _attn_st(dv_ref, sl, dv)

        @pl.when(b == 0)
        def _():
            dk_c[...] = jnp.zeros(dk_c.shape, F32)
            dv_c[...] = jnp.zeros(dv_c.shape, F32)

        @pl.when(b < nb)
        def _():
            qp, kp_, is_prev = _attn_valid(d)
            valid = (is_prev & (kp_ >= qp) & (b > 0)) | (jnp.logical_not(is_prev) & (kp_ <= qp))
            hm, hm2 = _head_masks(), _head_masks(2 * SWA_BLOCK)
            for sl in lane_slices:
                q, do_, lse_, dl_ = (_attn_ld(r, sl) for r in (q_ref, do_ref, lse_ref, dl_ref))
                k2 = jnp.concatenate([_attn_ld(kp_ref, sl), _attn_ld(kc_ref, sl)], axis=0)
                v2 = jnp.concatenate([_attn_ld(vp_ref, sl), _attn_ld(vc_ref, sl)], axis=0).astype(BF16)
                k2b = k2.astype(BF16)
                dq = jnp.zeros((SWA_BLOCK, LANES), F32)
                dk2 = jnp.zeros((2 * SWA_BLOCK, LANES), F32)
                dv2 = jnp.zeros((2 * SWA_BLOCK, LANES), F32)
                for hh in range(2):
                    qh, doh = _keep(hm[hh], q), _keep(hm[hh], do_)
                    lh, dh = _per_head(lse_, hh == 0), _per_head(dl_, hh == 0)
                    lh2, dh2 = jnp.concatenate([lh, lh], axis=1), jnp.concatenate([dh, dh], axis=1)
                    p = jnp.where(valid, jnp.exp(_dot_nt(qh, k2b) * scale - lh2), 0.0)
                    ds = (p * (_dot_nt(doh, v2) - dh2)).astype(BF16)
                    dq = dq + _dot(ds, _keep(hm2[hh], k2))
                    dk2 = dk2 + _dot_tn(ds, qh)
                    dv2 = dv2 + _dot_tn(p.astype(BF16), doh)
                dq, dk2 = dq * scale, dk2 * scale
                if has_prev:
                    dq = dq + _attn_ld(p_refs[0], sl)
                _attn_st(dq_ref, sl, dq)
                put_keys(sl, dk_c[:, sl] + dk2[:SWA_BLOCK], dv_c[:, sl] + dv2[:SWA_BLOCK])
                dk_c[:, sl] = dk2[SWA_BLOCK:]
                dv_c[:, sl] = dv2[SWA_BLOCK:]

        @pl.when(b == nb)
        def _():
            for sl in lane_slices:
                put_keys(sl, dk_c[:, sl], dv_c[:, sl])

    qk = _attn_view(qkv, d)
    acts = [_attn_view(t, d) for t in (do, lse, delta)] + ([_attn_view(t, d) for t in prev] if has_prev else [])
    sp = functools.partial(_attn_spec, S, d, lb)
    res = _pcall(
        body, name=name,
        grid=(nres, ncol, nb + 1),
        in_specs=[sp(cq), sp(ck, -1), sp(ck), sp(cv, -1), sp(cv), sp(ca), sp(ca), sp(ca)]
        + ([sp(ca), sp(ca, -1), sp(ca, -1)] if has_prev else []),
        out_specs=[sp(ca), sp(ca, -1), sp(ca, -1)],
        out_shape=[jax.ShapeDtypeStruct(view + (A,), F32)] * 3,
        scratch_shapes=[pltpu.VMEM((SWA_BLOCK, lb), F32)] * 2,
        compiler_params=_params("parallel", "parallel", "arbitrary"),
    )(qk, qk, qk, qk, qk, *acts)
    return [t.reshape(S, A) for t in res]


def _attn_bwd(qkv, dya, ya, lse):
    delta = _attn_delta(dya, ya)
    sums = None
    for d in DILATIONS:
        sums = _attn_bwd_stage("attn_bwd_d%d" % d, qkv, dya, lse, delta, d, sums)
    return sums


def _ssm_perm(a, T):
    S, w = a.shape
    return a.reshape(S // T, SUBLANES, T // SUBLANES, w).transpose(0, 2, 1, 3).reshape(S, w)


def _ssm_unperm(a, T):
    S, w = a.shape
    return a.reshape(S // T, T // SUBLANES, SUBLANES, w).transpose(0, 2, 1, 3).reshape(S, w)


def _ssm_powers(lam_ref, pw_ref, T, ns):
    tc = T // SUBLANES
    n = (lax.broadcasted_iota(jnp.int32, (tc, 1), 0) + 1).astype(F32)
    mag = jnp.exp(n * lam_ref[0, 0:1, :])
    ang = n * lam_ref[0, 1:2, :]
    rows8 = lambda t: jnp.broadcast_to(t[:, None, :], (tc, SUBLANES, ns)).reshape(T, ns)
    pw_ref[:, 0:ns] = rows8(mag * jnp.cos(ang))
    pw_ref[:, ns:2 * ns] = rows8(mag * jnp.sin(ang))


def _ssm_scan(xs, off, pw_ref, carry_ref, T, ns, reverse):
    Tc = T // SUBLANES
    sgn = -1.0 if reverse else 1.0
    ar, ai = pw_ref[0:SUBLANES, 0:ns], sgn * pw_ref[0:SUBLANES, ns:2 * ns]

    def rows(i):
        return pl.ds(pl.multiple_of(off + i * SUBLANES, SUBLANES), SUBLANES)

    def step(k, h):
        hr, hi = h
        r = rows(Tc - 1 - k if reverse else k)
        nr = ar * hr - ai * hi + xs[r, 0:ns]
        ni = ar * hi + ai * hr + xs[r, ns:2 * ns]
        xs[r, 0:ns] = nr
        xs[r, ns:2 * ns] = ni
        return nr, ni

    z = jnp.zeros((SUBLANES, ns), F32)
    er, ei = lax.fori_loop(0, Tc, step, (z, z), unroll=4)
    atr, ati = pw_ref[T - SUBLANES:T, 0:ns], sgn * pw_ref[T - SUBLANES:T, ns:2 * ns]
    rowid = lax.broadcasted_iota(jnp.int32, (SUBLANES, ns), 0)
    cr, ci = carry_ref[:, 0:ns], carry_ref[:, ns:2 * ns]
    ctr, cti = z, z
    for jj in range(SUBLANES):
        j = SUBLANES - 1 - jj if reverse else jj
        sel = rowid == j
        ctr, cti = jnp.where(sel, cr, ctr), jnp.where(sel, ci, cti)
        ejr = jnp.broadcast_to(jnp.sum(jnp.where(sel, er, 0.0), axis=0, keepdims=True), (SUBLANES, ns))
        eji = jnp.broadcast_to(jnp.sum(jnp.where(sel, ei, 0.0), axis=0, keepdims=True), (SUBLANES, ns))
        cr, ci = ejr + atr * cr - ati * ci, eji + atr * ci + ati * cr
    carry_ref[:, 0:ns] = cr
    carry_ref[:, ns:2 * ns] = ci

    def fix(i, _):
        r = rows(i)
        pr_rows = pl.ds(pl.multiple_of((Tc - 1 - i if reverse else i) * SUBLANES, SUBLANES), SUBLANES)
        pr, pi = pw_ref[pr_rows, 0:ns], sgn * pw_ref[pr_rows, ns:2 * ns]
        xs[r, 0:ns] += pr * ctr - pi * cti
        xs[r, ns:2 * ns] += pr * cti + pi * ctr
        return 0

    lax.fori_loop(0, Tc, fix, 0, unroll=4)
    return ctr, cti


def _ssm_fwd(ufp, bb, cc, lam_dt, drow, T):
    S, W = ufp.shape
    GB, cw, ns2 = bb.shape
    ns = ns2 // 2
    NCH = S // T

    def body(uf_ref, bb_ref, cc_ref, lam_ref, d_ref, y_ref, hs_ref, xs, pw, carry):
        @pl.when(pl.program_id(1) == 0)
        def _():
            _ssm_powers(lam_ref, pw, T, ns)
            carry[...] = jnp.zeros(carry.shape, F32)

        uf = uf_ref[...]
        xs[...] = _dot(uf.astype(BF16), bb_ref[0])
        hs_ref[0, 0] = carry[...]
        _ssm_scan(xs, 0, pw, carry, T, ns, reverse=False)
        y_ref[...] = _dot(xs[...].astype(BF16), cc_ref[0]) + d_ref[...] * uf

    return _pcall(
        body, name="ssm_fwd",
        grid=(GB, NCH),
        in_specs=[pl.BlockSpec((T, cw), lambda g, c: (c, g)),
                  pl.BlockSpec((1, cw, ns2), lambda g, c: (g, 0, 0)),
                  pl.BlockSpec((1, ns2, cw), lambda g, c: (g, 0, 0)),
                  pl.BlockSpec((1, 2, ns), lambda g, c: (g, 0, 0)),
                  pl.BlockSpec((1, cw), lambda g, c: (0, g))],
        out_specs=[pl.BlockSpec((T, cw), lambda g, c: (c, g)),
                   pl.BlockSpec((1, 1, SUBLANES, ns2), lambda g, c: (g, c, 0, 0))],
        out_shape=[jax.ShapeDtypeStruct((S, W), F32),
                   jax.ShapeDtypeStruct((GB, NCH, SUBLANES, ns2), F32)],
        scratch_shapes=[pltpu.VMEM((T, ns2), F32), pltpu.VMEM((T, ns2), F32), pltpu.VMEM((SUBLANES, ns2), F32)],
        compiler_params=_params("arbitrary", "arbitrary"),
    )(ufp, bb, cc, lam_dt, drow)


def _ssm_bwd(ufp, dyp, bb, bbt, cc, cct, lam_dt, drow, hstart, T):
    S, W = ufp.shape
    GB, cw, ns2 = bb.shape
    ns = ns2 // 2
    NCH = S // T

    def body(uf_ref, dy_ref, bb_ref, bbt_ref, cc_ref, cct_ref, lam_ref, d_ref, hs_ref,
             duf_ref, dbb_ref, dcc_ref, da_ref, dd_ref, hb, ls, pw, carry_f, carry_b):
        @pl.when(pl.program_id(1) == 0)
        def _():
            _ssm_powers(lam_ref, pw, T, ns)
            carry_b[...] = jnp.zeros(carry_b.shape, F32)
            dbb_ref[...] = jnp.zeros(dbb_ref.shape, F32)
            dcc_ref[...] = jnp.zeros(dcc_ref.shape, F32)
            da_ref[...] = jnp.zeros(da_ref.shape, F32)
            dd_ref[...] = jnp.zeros(dd_ref.shape, F32)

        uf, dy = uf_ref[...], dy_ref[...]
        ufb, dyb = uf.astype(BF16), dy.astype(BF16)
        hb[SUBLANES:T + SUBLANES, :] = _dot(ufb, bb_ref[0])
        carry_f[...] = hs_ref[0, 0]
        ctr, cti = _ssm_scan(hb, SUBLANES, pw, carry_f, T, ns, reverse=False)
        hb[0:SUBLANES, 0:ns] = ctr
        hb[0:SUBLANES, ns:ns2] = cti
        ls[...] = _dot(dyb, cct_ref[0])
        _ssm_scan(ls, 0, pw, carry_b, T, ns, reverse=True)
        lv = ls[...]
        lb = lv.astype(BF16)
        dbb_ref[0] += _dot_tn(ufb, lb)
        dcc_ref[0] += _dot_tn(hb[SUBLANES:T + SUBLANES, :].astype(BF16), dyb)
        lr, li = lv[:, 0:ns], lv[:, ns:ns2]
        hpr, hpi = hb[0:T, 0:ns], hb[0:T, ns:ns2]
        dar = jnp.sum(lr * hpr + li * hpi, axis=0, keepdims=True)
        dai = jnp.sum(li * hpr - lr * hpi, axis=0, keepdims=True)
        da_ref[0, 0:1, 0:ns] += dar
        da_ref[0, 0:1, ns:ns2] += dai
        duf_ref[...] = _dot(lb, bbt_ref[0]) + d_ref[...] * dy
        dd_ref[...] += jnp.sum(dy * uf, axis=0, keepdims=True)

    rc = lambda c: NCH - 1 - c
    return _pcall(
        body, name="ssm_bwd",
        grid=(GB, NCH),
        in_specs=[pl.BlockSpec((T, cw), lambda g, c: (rc(c), g)),
                  pl.BlockSpec((T, cw), lambda g, c: (rc(c), g)),
                  pl.BlockSpec((1, cw, ns2), lambda g, c: (g, 0, 0)),
                  pl.BlockSpec((1, ns2, cw), lambda g, c: (g, 0, 0)),
                  pl.BlockSpec((1, ns2, cw), lambda g, c: (g, 0, 0)),
                  pl.BlockSpec((1, cw, ns2), lambda g, c: (g, 0, 0)),
                  pl.BlockSpec((1, 2, ns), lambda g, c: (g, 0, 0)),
                  pl.BlockSpec((1, cw), lambda g, c: (0, g)),
                  pl.BlockSpec((1, 1, SUBLANES, ns2), lambda g, c: (g, rc(c), 0, 0))],
        out_specs=[pl.BlockSpec((T, cw), lambda g, c: (rc(c), g)),
                   pl.BlockSpec((1, cw, ns2), lambda g, c: (g, 0, 0)),
                   pl.BlockSpec((1, ns2, cw), lambda g, c: (g, 0, 0)),
                   pl.BlockSpec((1, SUBLANES, ns2), lambda g, c: (g, 0, 0)),
                   pl.BlockSpec((1, cw), lambda g, c: (0, g))],
        out_shape=[jax.ShapeDtypeStruct((S, W), F32),
                   jax.ShapeDtypeStruct((GB, cw, ns2), F32),
                   jax.ShapeDtypeStruct((GB, ns2, cw), F32),
                   jax.ShapeDtypeStruct((GB, SUBLANES, ns2), F32),
                   jax.ShapeDtypeStruct((1, W), F32)],
        scratch_shapes=[pltpu.VMEM((T + SUBLANES, ns2), F32), pltpu.VMEM((T, ns2), F32), pltpu.VMEM((T, ns2), F32),
                        pltpu.VMEM((SUBLANES, ns2), F32), pltpu.VMEM((SUBLANES, ns2), F32)],
        compiler_params=_params("arbitrary", "arbitrary"),
    )(ufp, dyp, bb, bbt, cc, cct, lam_dt, drow, hstart)


def _ssm_disc_math(lr, li, logdt, br, bi):
    dt = jnp.exp(logdt)
    mag = jnp.exp(lr * dt)
    ar = mag * jnp.cos(li * dt)
    ai = mag * jnp.sin(li * dt)
    nr, ni = ar - 1.0, ai
    den = lr * lr + li * li
    cr = (nr * lr + ni * li) / den
    ci = (ni * lr - nr * li) / den
    return ar, ai, cr * br - ci * bi, cr * bi + ci * br


def _ssm_disc(lr, li, logdt, br, bi):
    C = br.shape[1]

    def fn(ins, ps):
        _, _, bbr, bbi = _ssm_disc_math(*ins)
        dt = jnp.exp(ins[2])
        return [ins[0] * dt, ins[1] * dt, bbr, bbi], []

    return _rowwise("ssm_disc", fn, [lr, li, logdt, br, bi], [], [(1, F32), (1, F32), (C, F32), (C, F32)], ts=512)


def _ssm_disc_bwd(lr, li, logdt, br, bi, dar, dai, dbbr, dbbi):
    C = br.shape[1]

    def fn(ins, ps):
        _, vjp = jax.vjp(_ssm_disc_math, *ins[:5])
        return list(vjp(tuple(ins[5:]))), []

    return _rowwise("ssm_disc_bwd", fn, [lr, li, logdt, br, bi, dar, dai, dbbr, dbbi], [],
                    [(1, F32), (1, F32), (1, F32), (C, F32), (C, F32)], ts=512)


def _block_diag(t):
    GB, g, a, b = t.shape
    eye = jnp.eye(g, dtype=t.dtype)
    return (t[:, :, :, None, :] * eye[None, :, None, :, None]).reshape(GB, g * a, g * b)


def _block_diag_take(t, g):
    GB, ga, gb_ = t.shape
    a, b = ga // g, gb_ // g
    eye = jnp.eye(g, dtype=t.dtype)
    return (t.reshape(GB, g, a, g, b) * eye[None, :, None, :, None]).sum(axis=3)


def _loss_head(h4, tgt, gf):
    D = h4.shape[1]

    def fn(ins, ps):
        x, t = ins
        xh, r = _xhat(x)
        err = xh * ps[0] - t
        dn = err * (1.0 / D)
        dxh = dn * ps[0]
        dx = r * (dxh - xh * jnp.mean(dxh * xh, axis=-1, keepdims=True))
        return [dx], [jnp.sum(err * err, axis=0, keepdims=True), jnp.sum(dn * xh, axis=0, keepdims=True)]

    return _rowwise("loss_head", fn, [h4, tgt], [gf], [(D, F32)], accs=[D, D])


def _gelu(x):
    return 0.5 * x * (1.0 + jnp.tanh(GELU_C * (x + GELU_K * x * x * x)))


def _gelu_grad(x):
    t = jnp.tanh(GELU_C * (x + GELU_K * x * x * x))
    return 0.5 * (1.0 + t) + 0.5 * x * (1.0 - t * t) * GELU_C * (1.0 + 3.0 * GELU_K * x * x)


def _mesh_pos():
    return lax.axis_index("x"), lax.axis_index("y"), lax.axis_index("c")


def _other_chips(x, y):
    return [(1 - x, y), (x, 1 - y), (1 - x, 1 - y)]


def _remote(src, dst, send, recv, dev):
    return pltpu.make_async_remote_copy(src_ref=src, dst_ref=dst, send_sem=send, recv_sem=recv,
                                        device_id=dev, device_id_type=MESH)


ANY = pl.BlockSpec(memory_space=pl.ANY)


COMM_BLOCK_BYTES = 3 << 19


def _place():
    x, y, c = _mesh_pos()
    return jnp.stack([c] + [2 * cx + cy for cx, cy in _other_chips(x, y)] + [2 * x + y]).astype(jnp.int32)


def _send_chips(name, srcs, specs, tr, nth, cw):
    hr = nth * tr
    n = len(srcs)

    def body(*refs):
        got_ref, send, recv = refs[1 + n:]
        t = pl.program_id(0)
        x, y, c = _mesh_pos()
        cps = []
        for j, chip in enumerate(_other_chips(x, y)):
            dst = got_ref.at[pl.ds(pl.multiple_of(j * hr + t * tr, 16), tr), :]
            cp = _remote(refs[1 + j % n], dst, send.at[j], recv.at[j], (*chip, c))
            cp.start()
            cps.append(cp)
        for cp in cps:
            cp.wait_send()

        @pl.when(t == nth - 1)
        def _():
            for j in range(3):
                r_ = got_ref.at[pl.ds(j * hr, hr), :]
                _remote(r_, r_, send.at[j], recv.at[j], (x, y, c)).wait_recv()

    return _pcall(
        body, name=name,
        grid_spec=pltpu.PrefetchScalarGridSpec(
            num_scalar_prefetch=1, grid=(nth,), in_specs=specs, out_specs=ANY,
            scratch_shapes=[pltpu.SemaphoreType.DMA((3,)), pltpu.SemaphoreType.DMA((3,))]),
        out_shape=jax.ShapeDtypeStruct((3 * hr, cw), srcs[0].dtype),
        compiler_params=_params("arbitrary"),
    )(_place(), *srcs)


def _ag_assemble(name, shard, stage, axis, tr, nth):
    R, cc = shard.shape
    hr = nth * tr
    full = (R, N_CHIPS * cc) if axis == 1 else (N_CHIPS * R, cc)

    def body(pl_ref, s0, s1, s2, h0, h1, out_ref, send, recv, lsem):
        t = pl.program_id(0)
        x, y, c = _mesh_pos()

        def region(s, half):
            if axis == 1:
                return out_ref.at[pl.ds(pl.multiple_of(half * hr + t * tr, 16), tr), pl.ds(pl.multiple_of(s * cc, LANES), cc)]
            return out_ref.at[pl.ds(pl.multiple_of(s * R + half * hr + t * tr, 16), tr), :]

        cps = []
        for j, src in enumerate((s0, s1, s2)):
            dst = region(pl_ref[1 + j], c)
            cps.append(_remote(src, dst, send.at[j], recv, (x, y, 1 - c)))
            cps.append(pltpu.make_async_copy(src, dst, lsem.at[j]))
        for half, src in enumerate((h0, h1)):
            cps.append(pltpu.make_async_copy(src, region(pl_ref[4], half), lsem.at[3 + half]))
        for cp in cps:
            cp.start()
        for k, cp in enumerate(cps):
            if k < 6 and k % 2 == 0:
                cp.wait_send()
            else:
                cp.wait()

        @pl.when(t == nth - 1)
        def _():
            r_ = out_ref.at[pl.ds(0, hr), pl.ds(0, 3 * cc)] if axis == 1 else out_ref.at[pl.ds(0, 3 * hr), :]
            _remote(r_, r_, send.at[0], recv, (x, y, c)).wait_recv()

    blk = lambda f: pl.BlockSpec((tr, cc), f)
    return _pcall(
        body, name=name,
        grid_spec=pltpu.PrefetchScalarGridSpec(
            num_scalar_prefetch=1, grid=(nth,),
            in_specs=[blk(lambda t, p, j=j: (j * nth + t, 0)) for j in range(3)]
            + [blk(lambda t, p, h=h: (h * nth + t, 0)) for h in range(2)],
            out_specs=ANY,
            scratch_shapes=[pltpu.SemaphoreType.DMA((3,)), pltpu.SemaphoreType.DMA, pltpu.SemaphoreType.DMA((5,))]),
        out_shape=jax.ShapeDtypeStruct(full, shard.dtype),
        compiler_params=_params("arbitrary"),
    )(_place(), stage, stage, stage, shard, shard)


def _comm_rows(hr, row_bytes):
    return _tile(hr, max(16, COMM_BLOCK_BYTES // row_bytes // 16 * 16), 16)


def _host_send(items):
    def copies(ins, outs, send, recv):
        x, y, c = _mesh_pos()
        cps = []
        for w, (_, kind, hr, cw) in enumerate(items):
            for j, (cx, cy) in enumerate(_other_chips(x, y)):
                s = 2 * cx + cy
                if kind == "half":
                    src = ins[w].at[pl.ds(pl.multiple_of(c * hr, 16), hr), :]
                elif kind == "cols":
                    src = ins[w].at[:, pl.ds(pl.multiple_of(s * cw, LANES), cw)]
                else:
                    src = ins[w].at[pl.ds(pl.multiple_of(s * hr, 16), hr), :]
                cps.append(_remote(src, outs[w].at[pl.ds(j * hr, hr), :], send.at[3 * w + j], recv.at[3 * w + j], (cx, cy, c)))
        return cps

    def start(ins, outs, send, recv):
        for cp in copies(ins, outs, send, recv):
            cp.start()

    def wait(ins, outs, send, recv):
        for cp in copies(ins, outs, send, recv):
            cp.wait()

    return _Host([a for a, _, _, _ in items], [jax.ShapeDtypeStruct((3 * hr, cw), a.dtype) for a, _, hr, cw in items],
                 3 * len(items), start, wait)


def _ag_send(name, sh):
    R, cc = sh.shape
    tr = _comm_rows(R // 2, cc * 2)
    nth = R // 2 // tr
    return _send_chips(name, [sh], [pl.BlockSpec((tr, cc), lambda t, p: (p[0] * nth + t, 0))], tr, nth, cc)


def _ag_finish(name, sh, stage, axis):
    R, cc = sh.shape
    tr = _comm_rows(R // 2, cc * 2)
    return _ag_assemble(name, sh, stage, axis, tr, R // 2 // tr)


def _pair_sum(name, g, tr, nblk, blk):
    cw = g.shape[1]
    c_arr = lax.axis_index("c").astype(jnp.int32).reshape(1)

    def body(c_ref, keep_ref, send_ref, out_ref, land, send, recv):
        i = pl.program_id(0)
        slot = lax.rem(i, 2)
        x, y, c = _mesh_pos()
        cp = _remote(send_ref, land.at[slot], send.at[slot], recv.at[slot], (x, y, 1 - c))
        cp.start()
        cp.wait_recv()
        out_ref[...] = (keep_ref[...].astype(F32) + land[slot].astype(F32)).astype(BF16)
        cp.wait_send()

    return _pcall(
        body, name=name,
        grid_spec=pltpu.PrefetchScalarGridSpec(
            num_scalar_prefetch=1, grid=(nblk,),
            in_specs=[pl.BlockSpec((tr, cw), lambda i, c_ref: (blk(i, c_ref[0]), 0)),
                      pl.BlockSpec((tr, cw), lambda i, c_ref: (blk(i, 1 - c_ref[0]), 0))],
            out_specs=pl.BlockSpec((tr, cw), lambda i, c_ref: (i, 0)),
            scratch_shapes=[pltpu.VMEM((2, tr, cw), BF16), pltpu.SemaphoreType.DMA((2,)), pltpu.SemaphoreType.DMA((2,))]),
        out_shape=jax.ShapeDtypeStruct((nblk * tr, cw), BF16),
        compiler_params=_params("arbitrary"),
    )(c_arr, g, g)


def _sum_chips_swap(name, q, got, qspec, tr, nth, cw):
    def body(p_ref, q_ref, g0, g1, g2, out_ref, buf, send, recv, lsem):
        t = pl.program_id(0)
        x, y, c = _mesh_pos()
        buf[...] = q_ref[...].astype(F32) + g0[...].astype(F32) + g1[...].astype(F32) + g2[...].astype(F32)
        dst = out_ref.at[pl.ds(pl.multiple_of((p_ref[0] * nth + t) * tr, 16), tr), :]
        cp = _remote(buf, dst, send, recv, (x, y, 1 - c))
        lc = pltpu.make_async_copy(buf, dst, lsem)
        cp.start()
        lc.start()
        lc.wait()
        cp.wait_send()

        @pl.when(t == nth - 1)
        def _():
            theirs = out_ref.at[pl.ds(0, nth * tr), :]
            _remote(theirs, theirs, send, recv, (x, y, c)).wait_recv()

    return _pcall(
        body, name=name,
        grid_spec=pltpu.PrefetchScalarGridSpec(
            num_scalar_prefetch=1, grid=(nth,),
            in_specs=[qspec] + [pl.BlockSpec((tr, cw), lambda t, p, j=j: (j * nth + t, 0)) for j in range(3)],
            out_specs=ANY,
            scratch_shapes=[pltpu.VMEM((tr, cw), F32), pltpu.SemaphoreType.DMA, pltpu.SemaphoreType.DMA,
                            pltpu.SemaphoreType.DMA]),
        out_shape=jax.ShapeDtypeStruct((2 * nth * tr, cw), F32),
        compiler_params=_params("arbitrary"),
    )(_place(), q, got, got, got)


def _rs_geom(g, axis):
    rows, gw = g.shape
    return (rows // 2, gw // N_CHIPS) if axis == 1 else (rows // N_CHIPS // 2, gw)


def _rs_pair_sum(tag, g, axis):
    hr, _ = _rs_geom(g, axis)
    tr = _comm_rows(hr, g.shape[1] * 2)
    nth = hr // tr
    if axis == 1:
        nblk, blk = nth, (lambda i, half: half * nth + i)
    else:
        nblk, blk = N_CHIPS * nth, (lambda i, half: (i // nth) * (2 * nth) + half * nth + i % nth)
    return _pair_sum("rs_pair_" + tag, g, tr, nblk, blk)


def _rs_part(q, axis, hr, cw, tr):
    nth = hr // tr
    if axis == 1:
        return lambda k: pl.BlockSpec((tr, cw), lambda t, p: (t, p[k]))
    return lambda k: pl.BlockSpec((tr, cw), lambda t, p: (p[k] * nth + t, 0))


def _rs_send(tag, q, axis, hr, cw):
    tr = _comm_rows(hr, cw * 2)
    part = _rs_part(q, axis, hr, cw, tr)
    return _send_chips("rs_send_" + tag, [q, q, q], [part(1), part(2), part(3)], tr, hr // tr, cw)


def _rs_finish(tag, q, got, axis, hr, cw):
    tr = _comm_rows(hr, cw * 4)
    return _sum_chips_swap("rs_sum_swap_" + tag, q, got, _rs_part(q, axis, hr, cw, tr)(4), tr, hr // tr, cw)


class _Exchange:
    AG_PLAN = {"ffn1_up": ("ffn1_w_down", "ffn2_w_gate"), "ffn1_down": ("ffn2_w_up", "w_in"),
               "w_in_qkv": ("ssm_w_glu", "w_out", "ple_w_gate", "ple_w_proj"), "attn_fwd_d1": ("ffn2_w_down",)}
    RS_PLAN = {"ffn1_dact": ("ffn2_w_gate", "ple_w_gate", "ple_w_proj"), "ffn1_dwd": ("ffn2_w_up",),
               "ffn1_dwgu": ("ffn2_w_down", "w_in", "ssm_w_glu", "w_out", "ffn1_w_down"),
               "ffn1_dn": ("ffn1_w_gate", "ffn1_w_up")}

    def __init__(self, shards, axes):
        self.shards, self.axes = shards, axes
        self.stage, self.full, self.q, self.geom, self.got = {}, {}, {}, {}, {}

    def ag_host(self, kernel):
        item = lambda k: (self.shards[k], "half", self.shards[k].shape[0] // 2, self.shards[k].shape[1])
        return _host_send([item(k) for k in self.AG_PLAN[kernel]])

    def ag_done(self, kernel, stages):
        self.stage.update(zip(self.AG_PLAN[kernel], stages))

    def weight(self, k):
        if k not in self.full:
            stage = self.stage[k] if k in self.stage else _ag_send("ag_send_" + k, self.shards[k])
            self.full[k] = _ag_finish("ag_asm_" + k, self.shards[k], stage, self.axes[k])
        return self.full[k]

    def grad(self, k, g):
        self.q[k], self.geom[k] = _rs_pair_sum(k, g, self.axes[k]), _rs_geom(g, self.axes[k])

    def rs_host(self, kernel):
        item = lambda k: (self.q[k], "cols" if self.axes[k] == 1 else "rows") + self.geom[k]
        return _host_send([item(k) for k in self.RS_PLAN[kernel]])

    def rs_done(self, kernel, gots):
        self.got.update(zip(self.RS_PLAN[kernel], gots))

    def finish(self):
        return {k: _rs_finish(k, q, self.got[k] if k in self.got else _rs_send(k, q, self.axes[k], *self.geom[k]),
                              self.axes[k], *self.geom[k]) for k, q in self.q.items()}


def _all_reduce_small(v):
    n = v.shape[0]
    h = n // 2

    def body(v_ref, out_ref, pair_in, chips_in, send, recv):
        x, y, c = _mesh_pos()
        me, sib, my_chip = (x, y, c), (x, y, 1 - c), 2 * x + y
        mine = pl.ds(pl.multiple_of(c * h, SUBLANES), h)
        other = pl.ds(pl.multiple_of((1 - c) * h, SUBLANES), h)
        pair = _remote(v_ref.at[other], pair_in, send.at[0], recv.at[0], sib)
        pair.start()
        pair.wait()
        chips_in[my_chip] = v_ref[mine, :] + pair_in[...]
        cps = []
        for j, (cx, cy) in enumerate(_other_chips(x, y)):
            cp = _remote(chips_in.at[my_chip], chips_in.at[my_chip], send.at[1 + j], recv.at[1 + j], (cx, cy, c))
            cp.start()
            cps.append(cp)
        for j, (cx, cy) in enumerate(_other_chips(x, y)):
            slot = chips_in.at[2 * cx + cy]
            _remote(slot, slot, send.at[1 + j], recv.at[1 + j], me).wait_recv()
        out_ref[mine, :] = (chips_in[0] + chips_in[1]) + (chips_in[2] + chips_in[3])
        for cp in cps:
            cp.wait_send()
        swap = _remote(out_ref.at[mine, :], out_ref.at[mine, :], send.at[4], recv.at[4], sib)
        swap.start()
        _remote(out_ref.at[other, :], out_ref.at[other, :], send.at[4], recv.at[4], me).wait_recv()
        swap.wait_send()

    return _pcall(
        body, name="ar_small",
        in_specs=[pl.BlockSpec(memory_space=pltpu.VMEM)], out_specs=pl.BlockSpec(memory_space=pltpu.VMEM),
        out_shape=jax.ShapeDtypeStruct((n, LANES), F32),
        scratch_shapes=[pltpu.VMEM((h, LANES), F32), pltpu.VMEM((N_CHIPS, h, LANES), F32),
                        pltpu.SemaphoreType.DMA((5,)), pltpu.SemaphoreType.DMA((5,))],
        compiler_params=pltpu.CompilerParams(vmem_limit_bytes=V7X_VMEM_LIMIT_BYTES),
    )(v)


def _adamw(name, w, g, m, v):
    R, Cc = w.shape
    tr = _tile(R, max(8, (3 << 18) // Cc // 8 * 8), 8)
    c1 = 1.0 - ADAM_B1 ** ADAM_STEP
    c2 = 1.0 - ADAM_B2 ** ADAM_STEP

    def body(w_ref, g_ref, m_ref, v_ref, d_ref, nm_ref, nv_ref):
        g_ = g_ref[...]
        nm = ADAM_B1 * m_ref[...] + (1.0 - ADAM_B1) * g_
        nv = ADAM_B2 * v_ref[...] + (1.0 - ADAM_B2) * (g_ * g_)
        d_ref[...] = -ADAM_LR * ((nm / c1) / (jnp.sqrt(nv / c2) + ADAM_EPS) + ADAM_WD * w_ref[...])
        nm_ref[...] = nm
        nv_ref[...] = nv

    spec = pl.BlockSpec((tr, Cc), lambda i: (i, 0))
    return _pcall(
        body, name=name, grid=(R // tr,),
        in_specs=[spec] * 4, out_specs=[spec] * 3,
        out_shape=[jax.ShapeDtypeStruct((R, Cc), F32)] * 3,
        compiler_params=_params("parallel"),
    )(w, g, m, v)


def _pack(arrs, rows):
    flat = jnp.concatenate([a.reshape(-1) for a in arrs])
    return jnp.pad(flat, (0, rows * LANES - flat.shape[0])).reshape(rows, LANES)


def _unpack(packed, like):
    flat, out, o = packed.reshape(-1), [], 0
    for a in like:
        out.append(flat[o:o + a.size].reshape(a.shape))
        o += a.size
    return out


BIG = (
    ("ffn1_w_gate", 1), ("ffn1_w_up", 1), ("ffn1_w_down", 0), ("w_in", 1), ("ssm_w_glu", 0), ("w_out", 0),
    ("ffn2_w_gate", 1), ("ffn2_w_up", 1), ("ffn2_w_down", 0), ("ple_w_gate", 0), ("ple_w_proj", 1),
)
SMALL = ("ffn1_norm", "mix_norm", "attn_out_norm", "ssm_lambda_re", "ssm_lambda_im", "ssm_log_dt", "ssm_b_re", "ssm_b_im",
         "ssm_c_re", "ssm_c_im", "ssm_d", "ssm_b_glu", "ssm_out_norm", "ffn2_norm", "ple_norm", "final_norm")
WEIGHTS = ("ffn1_norm", "ffn1_w_gate", "ffn1_w_up", "ffn1_w_down", "mix_norm", "w_in", "attn_out_norm", "ssm_lambda_re",
           "ssm_lambda_im", "ssm_log_dt", "ssm_b_re", "ssm_b_im", "ssm_c_re", "ssm_c_im", "ssm_d", "ssm_w_glu", "ssm_b_glu",
           "ssm_out_norm", "w_out", "ffn2_norm", "ffn2_w_gate", "ffn2_w_up", "ffn2_w_down", "ple_norm", "ple_w_gate",
           "ple_w_proj", "final_norm")


def _pad_to(a, axis, n):
    pad = [(0, 0), (0, 0)]
    pad[axis] = (0, n - a.shape[axis])
    return jnp.pad(a, pad)


def _local_step(x, p, tgt, w, ex):
    S, D = x.shape
    A = w["attn_out_norm"].shape[-1]
    W = w["ssm_d"].shape[-1]
    G, P = w["ssm_lambda_re"].shape[-2:]
    C = w["ssm_b_re"].shape[-1]
    GB = G // SSM_BLOCK_GROUPS
    T = min(512, S)
    row = lambda name: w[name].reshape(1, -1)
    gs = {}

    h1, ffn1_saved = _ffn_fwd("ffn1", x, row("ffn1_norm"), ex)
    n2 = _rms_fwd("mix_norm", h1, row("mix_norm"))
    w_in = ex.weight("w_in")
    n2p = _to_attn_order(n2)
    host, done = _carried(ex, "ag", "w_in_qkv")
    qkv, *outs = _mm("w_in_qkv", [n2p], [w_in[:, :3 * A]], [F32], tm=1024, tn=1024, host=host)
    done(outs)
    (s_in,) = _mm("w_in_ssm", [n2], [w_in[:, 3 * A:]], [F32], tm=1024, tn=1024)
    ya, lse = _attn_fwd(qkv, ex)

    col = lambda name: w[name].reshape(G * P, 1)
    logdt_x = jnp.repeat(w["ssm_log_dt"].reshape(G), P).reshape(G * P, 1)
    b_re, b_im = w["ssm_b_re"].reshape(G * P, C), w["ssm_b_im"].reshape(G * P, C)
    lrdt, lidt, bbr, bbi = _ssm_disc(col("ssm_lambda_re"), col("ssm_lambda_im"), logdt_x, b_re, b_im)
    gsz = SSM_BLOCK_GROUPS
    to_bb = lambda t: _block_diag(t.reshape(GB, gsz, P, C).transpose(0, 1, 3, 2))
    bb = jnp.concatenate([to_bb(bbr), to_bb(bbi)], axis=2).astype(BF16)
    to_cc = lambda t: _block_diag(t.reshape(GB, gsz, C, P).transpose(0, 1, 3, 2))
    cc = jnp.concatenate([to_cc(w["ssm_c_re"]), -to_cc(w["ssm_c_im"])], axis=1).astype(BF16)
    lam_dt = jnp.stack([lrdt.reshape(GB, gsz * P), lidt.reshape(GB, gsz * P)], axis=1)
    ufp = _ssm_perm(s_in, T)
    ypre, hstart = _ssm_fwd(ufp, bb, cc, lam_dt, row("ssm_d"), T)

    def glu_in(ins, ps):
        yg = _gelu(ins[0])
        return [yg, yg], []

    yg, ygb = _rowwise("ssm_gelu", glu_in, [ypre], [], [(W, F32), (W, BF16)])
    w_glu = ex.weight("ssm_w_glu")

    def glu_out(accs, ex):
        gl = accs[0] + ex[1]
        return [ex[0] * _sigmoid(gl), gl]

    ybp, gl = _mm("ssm_glu", [ygb], [w_glu], [F32, F32], extras=[(yg, "mn"), (row("ssm_b_glu"), "n")],
                  epilogue=glu_out, tm=1024, tn=1024)
    yb = _ssm_unperm(ybp, T)
    na = _from_attn_order(_rms_fwd("attn_out_norm", ya, row("attn_out_norm")))
    nb = _rms_fwd("ssm_out_norm", yb, row("ssm_out_norm"))
    w_out = ex.weight("w_out")
    (h2,) = _mm("w_out", [na, nb], [w_out[:A], w_out[A:]], [F32], pairs=((0, 0, 0), (1, 1, 0)), extras=[(h1, "mn")],
                epilogue=lambda accs, ex: [ex[0] + accs[0]], tm=1024, tn=1024)
    h3, ffn2_saved = _ffn_fwd("ffn2", h2, row("ffn2_norm"), ex)
    n4 = _rms_fwd("ple_norm", h3, row("ple_norm"))
    (pe,) = _mm("ple_proj", [p], [ex.weight("ple_w_proj")], [F32], tm=1024, tn=1024)

    def ple_out(accs, ex):
        gate = _sigmoid(accs[0])
        return [ex[1] + gate * ex[0], gate]

    h4, gate = _mm("ple_gate", [n4], [ex.weight("ple_w_gate")], [F32, F32], extras=[(pe, "mn"), (h3, "mn")],
                   epilogue=ple_out, tm=1024, tn=1024)

    dh4, err2, gs["final_norm"] = _loss_head(h4, tgt, row("final_norm"))
    loss = (0.5 / D) * jnp.sum(err2)

    def ple_bwd(ins, ps):
        dh, gt, pe_ = ins
        return [dh * gt, dh * pe_ * gt * (1.0 - gt)], []

    dpe, dpg = _rowwise("ple_bwd", ple_bwd, [dh4, gate, pe], [], [(D, BF16), (D, BF16)])
    (d_ple_proj,) = _mm("ple_dproj", [p], [dpe], [BF16], ta=True, tm=256, tn=2048, tk=1024)
    (d_ple_gate,) = _mm("ple_dgate", [n4], [dpg], [BF16], ta=True, tm=1024, tn=1024, tk=2048)
    (dn4,) = _mm("ple_dn", [dpg], [ex.weight("ple_w_gate")], [F32], tb=True, tm=1024, tn=1024)
    (dh3, dh3b), gs["ple_norm"] = _rms_bwd("ple_dnorm", dn4, h3, row("ple_norm"), dres=dh4, copy_scale=0.5)
    (dh2, dh2b), gs["ffn2_norm"] = _ffn_bwd("ffn2", dh3, dh3b, h2, row("ffn2_norm"), ex, ffn2_saved, copy_scale=1.0)
    (dna,) = _mm("w_out_dna", [_to_attn_order(dh2b)], [w_out[:A]], [F32], tb=True, tm=1024, tn=1024)
    (dnb,) = _mm("w_out_dnb", [dh2b], [w_out[A:]], [F32], tb=True, tm=1024, tn=1024)
    (d_wout_a,) = _mm("w_out_dwa", [na], [dh2b], [BF16], ta=True, tm=1024, tn=1024, tk=2048)
    (d_wout_b,) = _mm("w_out_dwb", [nb], [dh2b], [BF16], ta=True, tm=1024, tn=1024, tk=2048)
    d_w_out = jnp.concatenate([d_wout_a, d_wout_b], axis=0)
    (dya,), gs["attn_out_norm"] = _rms_bwd("attn_out_dnorm", dna, ya, row("attn_out_norm"))
    (dyb,), gs["ssm_out_norm"] = _rms_bwd("ssm_out_dnorm", dnb, yb, row("ssm_out_norm"))

    dybp = _ssm_perm(dyb, T)

    def glu_bwd(ins, ps):
        dy, yg_, gl_ = ins
        sg = _sigmoid(gl_)
        dgl = dy * yg_ * sg * (1.0 - sg)
        return [dgl, dy * sg], [jnp.sum(dgl, axis=0, keepdims=True)]

    dgl, dyg_direct, gs["ssm_b_glu"] = _rowwise("ssm_glu_bwd", glu_bwd, [dybp, yg, gl], [], [(W, BF16), (W, F32)], accs=[W])
    (d_w_glu,) = _mm("ssm_dwglu", [ygb], [dgl], [BF16], ta=True, tm=1024, tn=1024, tk=2048)
    (dypre,) = _mm("ssm_dyg", [dgl], [w_glu], [F32], tb=True, extras=[(dyg_direct, "mn"), (ypre, "mn")],
                   epilogue=lambda accs, ex: [(accs[0] + ex[0]) * _gelu_grad(ex[1])], tm=1024, tn=1024)
    dufp, dbb, dcc, da, gs["ssm_d"] = _ssm_bwd(ufp, dypre, bb, bb.transpose(0, 2, 1), cc, cc.transpose(0, 2, 1),
                                               lam_dt, row("ssm_d"), hstart, T)
    ns = gsz * P
    from_bb = lambda t: _block_diag_take(t, gsz).transpose(0, 1, 3, 2).reshape(G * P, C)
    from_cc = lambda t: _block_diag_take(t, gsz).transpose(0, 1, 3, 2).reshape(w["ssm_c_re"].shape)
    gs["ssm_c_re"], gs["ssm_c_im"] = from_cc(dcc[:, :ns]), -from_cc(dcc[:, ns:])
    da = da.sum(axis=1)
    dar, dai = da[:, :ns].reshape(G * P, 1), da[:, ns:].reshape(G * P, 1)
    dlr, dli, dlogdt, dbr, dbi = _ssm_disc_bwd(col("ssm_lambda_re"), col("ssm_lambda_im"), logdt_x, b_re, b_im,
                                               dar, dai, from_bb(dbb[:, :, :ns]), from_bb(dbb[:, :, ns:]))
    gs["ssm_lambda_re"], gs["ssm_lambda_im"] = dlr.reshape(w["ssm_lambda_re"].shape), dli.reshape(w["ssm_lambda_im"].shape)
    gs["ssm_log_dt"] = dlogdt.reshape(G, P).sum(axis=1).reshape(w["ssm_log_dt"].shape)
    gs["ssm_b_re"], gs["ssm_b_im"] = dbr.reshape(w["ssm_b_re"].shape), dbi.reshape(w["ssm_b_im"].shape)
    ds_in = _ssm_unperm(dufp, T)

    dq, dk, dv = _attn_bwd(qkv, dya, ya, lse)
    dqkv = jnp.concatenate([dq, dk, dv], axis=1).astype(BF16)
    (d_w_qkv,) = _mm("w_in_dw_qkv", [n2p], [dqkv], [BF16], ta=True, tm=1024, tn=1024, tk=2048)
    (d_w_s,) = _mm("w_in_dw_ssm", [n2], [ds_in], [BF16], ta=True, tm=1024, tn=1024, tk=2048)
    d_w_in = jnp.concatenate([d_w_qkv, d_w_s], axis=1)
    dz = jnp.concatenate([_from_attn_order(dqkv), ds_in.astype(BF16)], axis=1)
    (dn2,) = _mm("w_in_dn", [dz], [w_in], [F32], tb=True, tm=1024, tn=1024)
    (dh1, dh1b), gs["mix_norm"] = _rms_bwd("mix_dnorm", dn2, h1, row("mix_norm"), dres=dh2, copy_scale=0.5)
    for k, g in (("ple_w_gate", d_ple_gate), ("ple_w_proj", d_ple_proj), ("w_out", d_w_out), ("ssm_w_glu", d_w_glu),
                 ("w_in", d_w_in)):
        ex.grad(k, g)
    (dx,), gs["ffn1_norm"] = _ffn_bwd("ffn1", dh1, dh1b, x, row("ffn1_norm"), ex, ffn1_saved, copy_scale=None)
    small = {k: gs[k].reshape(w[k].shape) for k in SMALL}
    return loss, dx, ex.finish(), small


def kernel(x, p, ffn1_norm, ffn1_w_gate, ffn1_w_up, ffn1_w_down, mix_norm, w_in, attn_out_norm, ssm_lambda_re, ssm_lambda_im, ssm_log_dt, ssm_b_re, ssm_b_im, ssm_c_re, ssm_c_im, ssm_d, ssm_w_glu, ssm_b_glu, ssm_out_norm, w_out, ffn2_norm, ffn2_w_gate, ffn2_w_up, ffn2_w_down, ple_norm, ple_w_gate, ple_w_proj, final_norm, loss_target, m_ffn1_norm, m_ffn1_w_gate, m_ffn1_w_up, m_ffn1_w_down, m_mix_norm, m_w_in, m_attn_out_norm, m_ssm_lambda_re, m_ssm_lambda_im, m_ssm_log_dt, m_ssm_b_re, m_ssm_b_im, m_ssm_c_re, m_ssm_c_im, m_ssm_d, m_ssm_w_glu, m_ssm_b_glu, m_ssm_out_norm, m_w_out, m_ffn2_norm, m_ffn2_w_gate, m_ffn2_w_up, m_ffn2_w_down, m_ple_norm, m_ple_w_gate, m_ple_w_proj, m_final_norm, v_ffn1_norm, v_ffn1_w_gate, v_ffn1_w_up, v_ffn1_w_down, v_mix_norm, v_w_in, v_attn_out_norm, v_ssm_lambda_re, v_ssm_lambda_im, v_ssm_log_dt, v_ssm_b_re, v_ssm_b_im, v_ssm_c_re, v_ssm_c_im, v_ssm_d, v_ssm_w_glu, v_ssm_b_glu, v_ssm_out_norm, v_w_out, v_ffn2_norm, v_ffn2_w_gate, v_ffn2_w_up, v_ffn2_w_down, v_ple_norm, v_ple_w_gate, v_ple_w_proj, v_final_norm):
    args = locals()
    w = {k: args[k] for k in WEIGHTS}
    m = {k: args["m_" + k] for k in WEIGHTS}
    v = {k: args["v_" + k] for k in WEIGHTS}
    w2 = {k: w[k].reshape(w[k].shape[-2:]) for k, _ in BIG}

    axes = [ax for _, ax in BIG]
    padded = {k: -(-w2[k].shape[ax] // LANES) * LANES for k, ax in BIG}
    shards = [_pad_to(w2[k].astype(BF16), ax, padded[k]) for k, ax in BIG]
    ex = _Exchange(dict(zip([k for k, _ in BIG], shards)), dict(BIG))
    loss_local, dx, summed, gsmall = _local_step(x[0], p[0, 0], loss_target[0], w, ex)
    loss = lax.psum(loss_local, MESH_AXES)
    n_small = sum(w[k].size for k in SMALL)
    rows = -(-n_small // (2 * SUBLANES * LANES)) * 2 * SUBLANES
    gs_sum = _all_reduce_small(_pack([gsmall[k] for k in SMALL], rows))

    grads, delta, new_m, new_v = {}, {}, {}, {}
    for k, ax in BIG:
        gfull = summed[k]
        g2 = lax.slice_in_dim(gfull, 0, w2[k].shape[ax], axis=ax)
        d2, nm2, nv2 = _adamw("adamw_" + k, w2[k], g2, m[k].reshape(w2[k].shape), v[k].reshape(w2[k].shape))
        grads[k], delta[k], new_m[k], new_v[k] = (t.reshape(w[k].shape) for t in (g2, d2, nm2, nv2))
    small_like = [w[k] for k in SMALL]
    ds, nms, nvs = _adamw("adamw_small", _pack(small_like, rows), gs_sum, _pack([m[k] for k in SMALL], rows),
                          _pack([v[k] for k in SMALL], rows))
    for k, g_, d_, nm_, nv_ in zip(SMALL, _unpack(gs_sum, small_like), _unpack(ds, small_like),
                                   _unpack(nms, small_like), _unpack(nvs, small_like)):
        grads[k], delta[k], new_m[k], new_v[k] = g_, d_, nm_, nv_

    return (loss, dx[None], *[grads[k] for k in WEIGHTS], *[delta[k] for k in WEIGHTS],
            *[new_m[k] for k in WEIGHTS], *[new_v[k] for k in WEIGHTS])
```
